```python
import jax, jax.numpy as jnp
from jax import lax
import numpy as np

D_MODEL = 1024
BATCH = 8
SEQ = 8192
DEPTH = 1

D_CONV = D_MODEL
CONV_A_WIDTH = 3
EXPAND = 2
D_INNER = EXPAND * D_MODEL
HEAD_DIM = 64
N_HEADS = D_INNER // HEAD_DIM
N_GROUPS = 4
D_STATE = 128
SSD_CONV_WIDTH = 4
CHUNK = 128
D_XBC = D_INNER + 2 * N_GROUPS * D_STATE
D_FF = 2816
FFN_CONV_WIDTH = 3
N_IN = 2 * D_MODEL + 3 * D_CONV + D_INNER + D_XBC + N_HEADS
EPS = 1e-5
DT_MIN = 1e-3
DT_MAX = 1e-1

kernel_name = "hybrid_shortconv_ssd_gated_merge_convffn"


def rmsnorm(x, w):
    xf = x.astype(jnp.float32)
    y = xf * lax.rsqrt(jnp.mean(xf * xf, axis=-1, keepdims=True) + EPS)
    return (y * w.astype(jnp.float32)).astype(x.dtype)


def causal_dwconv(x, w):
    k, c = w.shape
    return lax.conv_general_dilated(
        x, w[:, None, :].astype(x.dtype), window_strides=(1,), padding=[(k - 1, 0)],
        dimension_numbers=("NWC", "WIO", "NWC"), feature_group_count=c)


def ssd_chunked_scan(xh, dt, a, bmat, cmat):
    b, s, h, p = xh.shape
    g, n = bmat.shape[-2:]
    j = h // g
    c = s // CHUNK
    x_ = (xh.astype(jnp.float32) * dt[..., None]).reshape(b, c, CHUNK, g, j, p)
    log_a = jnp.moveaxis((dt * a).reshape(b, c, CHUNK, g, j), 2, -1)
    bc = bmat.astype(jnp.float32).reshape(b, c, CHUNK, g, n)
    cc = cmat.astype(jnp.float32).reshape(b, c, CHUNK, g, n)
    a_cum = jnp.cumsum(log_a, axis=-1)

    causal = jnp.tril(jnp.ones((CHUNK, CHUNK), dtype=bool))
    seg = a_cum[..., :, None] - a_cum[..., None, :]
    decay_in = jnp.exp(jnp.where(causal, seg, -jnp.inf))
    cb = jnp.einsum("bclgn,bcsgn->bcgls", cc, bc)
    y_diag = jnp.einsum("bcgjls,bcsgjp->bclgjp", cb[:, :, :, None] * decay_in, x_)

    decay_to_end = jnp.exp(a_cum[..., -1:] - a_cum)
    states = jnp.einsum("bclgn,bcgjl,bclgjp->bcgjpn", bc, decay_to_end, x_)
    chunk_decay = jnp.exp(a_cum[..., -1])

    def step(carry, inp):
        st, dec = inp
        return carry * dec[..., None, None] + st, carry

    init = jnp.zeros((b, g, j, p, n), jnp.float32)
    _, prev = lax.scan(step, init, (jnp.moveaxis(states, 1, 0), jnp.moveaxis(chunk_decay, 1, 0)))
    prev = jnp.moveaxis(prev, 0, 1)
    y_off = jnp.einsum("bclgn,bcgjpn,bcgjl->bclgjp", cc, prev, jnp.exp(a_cum))
    return (y_diag + y_off).reshape(b, s, h, p)


def gated_group_rmsnorm(y, z, w):
    bsz, s, d = y.shape
    yf = (y.astype(jnp.float32) * jax.nn.silu(z.astype(jnp.float32))).reshape(bsz, s, N_GROUPS, d // N_GROUPS)
    yf = yf * lax.rsqrt(jnp.mean(yf * yf, axis=-1, keepdims=True) + EPS)
    return (yf.reshape(bsz, s, d) * w.astype(jnp.float32)).astype(y.dtype)


def hybrid_mixer(u, w_in, conv_a_w, w_a_out, ssd_conv_w, ssd_conv_b, dt_bias, a_log,
                 d_skip, ssd_norm_w, w_s_out, w_o):
    bsz, s, _ = u.shape
    proj = u @ w_in
    sizes = [D_MODEL, D_MODEL, D_CONV, D_CONV, D_CONV, D_INNER, D_XBC, N_HEADS]
    offsets = np.cumsum(sizes)[:-1].tolist()
    gate_a, gate_s, b_a, c_a, v_a, z, xbc, dt_raw = jnp.split(proj, offsets, axis=-1)

    y_a = (b_a * causal_dwconv(c_a * v_a, conv_a_w)) @ w_a_out

    xbc = jax.nn.silu(causal_dwconv(xbc, ssd_conv_w) + ssd_conv_b)
    xs, bs, cs = jnp.split(xbc, [D_INNER, D_INNER + N_GROUPS * D_STATE], axis=-1)
    xh = xs.reshape(bsz, s, N_HEADS, HEAD_DIM)
    dt = jax.nn.softplus(dt_raw.astype(jnp.float32) + dt_bias.astype(jnp.float32))
    a = -jnp.exp(a_log.astype(jnp.float32))
    y = ssd_chunked_scan(xh, dt, a,
                         bs.reshape(bsz, s, N_GROUPS, D_STATE),
                         cs.reshape(bsz, s, N_GROUPS, D_STATE))
    y = y + d_skip.astype(jnp.float32)[:, None] * xh.astype(jnp.float32)
    y = y.reshape(bsz, s, D_INNER).astype(u.dtype)
    y_s = gated_group_rmsnorm(y, z, ssd_norm_w) @ w_s_out

    merged = jax.nn.sigmoid(gate_a) * y_a + jax.nn.sigmoid(gate_s) * y_s
    return merged @ w_o


def conv_gated_mlp(v, w_up, ffn_conv_w, ffn_conv_b, w_down):
    hv = v @ w_up
    h1, h3 = jnp.split(hv, 2, axis=-1)
    h1 = causal_dwconv(h1, ffn_conv_w) + ffn_conv_b
    return (jax.nn.silu(h1) * h3) @ w_down


def _fwd_setup_inputs(seed: int = 0) -> dict:
    key = jax.random.key(seed)
    ks = jax.random.split(key, 24)
    f32 = jnp.float32

    def nrm(k, shape, scale):
        return jax.random.normal(k, shape, f32) * scale

    dt0 = jnp.exp(jax.random.uniform(ks[9], (DEPTH, N_HEADS), f32)
                  * (np.log(DT_MAX) - np.log(DT_MIN)) + np.log(DT_MIN))
    dt_bias = dt0 + jnp.log(-jnp.expm1(-dt0))
    return {
        "x": nrm(ks[0], (BATCH, SEQ, D_MODEL), 1.0),
        "norm_mix_w": 1.0 + nrm(ks[1], (DEPTH, D_MODEL), 0.02),
        "w_in": nrm(ks[2], (DEPTH, D_MODEL, N_IN), D_MODEL ** -0.5),
        "conv_a_w": nrm(ks[3], (DEPTH, CONV_A_WIDTH, D_CONV), CONV_A_WIDTH ** -0.5),
        "w_a_out": nrm(ks[4], (DEPTH, D_CONV, D_MODEL), D_CONV ** -0.5),
        "ssd_conv_w": nrm(ks[5], (DEPTH, SSD_CONV_WIDTH, D_XBC), SSD_CONV_WIDTH ** -0.5),
        "ssd_conv_b": nrm(ks[6], (DEPTH, D_XBC), 0.02),
        "dt_bias": dt_bias,
        "a_log": jnp.log(jax.random.uniform(ks[7], (DEPTH, N_HEADS), f32, 1.0, 16.0)),
        "d_skip": 1.0 + nrm(ks[8], (DEPTH, N_HEADS), 0.02),
        "ssd_norm_w": 1.0 + nrm(ks[10], (DEPTH, D_INNER), 0.02),
        "w_s_out": nrm(ks[11], (DEPTH, D_INNER, D_MODEL), D_INNER ** -0.5),
        "w_o": nrm(ks[12], (DEPTH, D_MODEL, D_MODEL), D_MODEL ** -0.5),
        "norm_ffn_w": 1.0 + nrm(ks[13], (DEPTH, D_MODEL), 0.02),
        "w_up": nrm(ks[14], (DEPTH, D_MODEL, 2 * D_FF), D_MODEL ** -0.5),
        "ffn_conv_w": nrm(ks[15], (DEPTH, FFN_CONV_WIDTH, D_FF), FFN_CONV_WIDTH ** -0.5),
        "ffn_conv_b": nrm(ks[16], (DEPTH, D_FF), 0.02),
        "w_down": nrm(ks[17], (DEPTH, D_FF, D_MODEL), D_FF ** -0.5),
        "final_norm_w": 1.0 + nrm(ks[18], (D_MODEL,), 0.02),
    }


def _fwd_reference(x, norm_mix_w, w_in, conv_a_w, w_a_out, ssd_conv_w, ssd_conv_b, dt_bias,
              a_log, d_skip, ssd_norm_w, w_s_out, w_o, norm_ffn_w, w_up, ffn_conv_w,
              ffn_conv_b, w_down, final_norm_w):
    h = x
    for l in range(DEPTH):
        u = rmsnorm(h, norm_mix_w[l])
        h = h + hybrid_mixer(u, w_in[l], conv_a_w[l], w_a_out[l], ssd_conv_w[l], ssd_conv_b[l],
                             dt_bias[l], a_log[l], d_skip[l], ssd_norm_w[l], w_s_out[l], w_o[l])
        v = rmsnorm(h, norm_ffn_w[l])
        h = h + conv_gated_mlp(v, w_up[l], ffn_conv_w[l], ffn_conv_b[l], w_down[l])
    return rmsnorm(h, final_norm_w)


import jax as _jax
import jax.numpy as _jnp

TWIN_FORMAT = 'train_step'
FWD_PARAMS = ['x', 'norm_mix_w', 'w_in', 'conv_a_w', 'w_a_out', 'ssd_conv_w', 'ssd_conv_b', 'dt_bias', 'a_log', 'd_skip', 'ssd_norm_w', 'w_s_out', 'w_o', 'norm_ffn_w', 'w_up', 'ffn_conv_w', 'ffn_conv_b', 'w_down', 'final_norm_w']
TWIN_WEIGHTS = ['norm_mix_w', 'w_in', 'conv_a_w', 'w_a_out', 'ssd_conv_w', 'ssd_conv_b', 'dt_bias', 'a_log', 'd_skip', 'ssd_norm_w', 'w_s_out', 'w_o', 'norm_ffn_w', 'w_up', 'ffn_conv_w', 'ffn_conv_b', 'w_down', 'final_norm_w']
TWIN_DIFF_INPUT = 'x'
TWIN_INPUTS = ['x', 'norm_mix_w', 'w_in', 'conv_a_w', 'w_a_out', 'ssd_conv_w', 'ssd_conv_b', 'dt_bias', 'a_log', 'd_skip', 'ssd_norm_w', 'w_s_out', 'w_o', 'norm_ffn_w', 'w_up', 'ffn_conv_w', 'ffn_conv_b', 'w_down', 'final_norm_w', 'loss_target', 'm_norm_mix_w', 'm_w_in', 'm_conv_a_w', 'm_w_a_out', 'm_ssd_conv_w', 'm_ssd_conv_b', 'm_dt_bias', 'm_a_log', 'm_d_skip', 'm_ssd_norm_w', 'm_w_s_out', 'm_w_o', 'm_norm_ffn_w', 'm_w_up', 'm_ffn_conv_w', 'm_ffn_conv_b', 'm_w_down', 'm_final_norm_w', 'v_norm_mix_w', 'v_w_in', 'v_conv_a_w', 'v_w_a_out', 'v_ssd_conv_w', 'v_ssd_conv_b', 'v_dt_bias', 'v_a_log', 'v_d_skip', 'v_ssd_norm_w', 'v_w_s_out', 'v_w_o', 'v_norm_ffn_w', 'v_w_up', 'v_ffn_conv_w', 'v_ffn_conv_b', 'v_w_down', 'v_final_norm_w']
TWIN_OUTPUTS = ['loss', 'grad_x', 'grad_norm_mix_w', 'grad_w_in', 'grad_conv_a_w', 'grad_w_a_out', 'grad_ssd_conv_w', 'grad_ssd_conv_b', 'grad_dt_bias', 'grad_a_log', 'grad_d_skip', 'grad_ssd_norm_w', 'grad_w_s_out', 'grad_w_o', 'grad_norm_ffn_w', 'grad_w_up', 'grad_ffn_conv_w', 'grad_ffn_conv_b', 'grad_w_down', 'grad_final_norm_w', 'delta_norm_mix_w', 'delta_w_in', 'delta_conv_a_w', 'delta_w_a_out', 'delta_ssd_conv_w', 'delta_ssd_conv_b', 'delta_dt_bias', 'delta_a_log', 'delta_d_skip', 'delta_ssd_norm_w', 'delta_w_s_out', 'delta_w_o', 'delta_norm_ffn_w', 'delta_w_up', 'delta_ffn_conv_w', 'delta_ffn_conv_b', 'delta_w_down', 'delta_final_norm_w', 'new_m_norm_mix_w', 'new_m_w_in', 'new_m_conv_a_w', 'new_m_w_a_out', 'new_m_ssd_conv_w', 'new_m_ssd_conv_b', 'new_m_dt_bias', 'new_m_a_log', 'new_m_d_skip', 'new_m_ssd_norm_w', 'new_m_w_s_out', 'new_m_w_o', 'new_m_norm_ffn_w', 'new_m_w_up', 'new_m_ffn_conv_w', 'new_m_ffn_conv_b', 'new_m_w_down', 'new_m_final_norm_w', 'new_v_norm_mix_w', 'new_v_w_in', 'new_v_conv_a_w', 'new_v_w_a_out', 'new_v_ssd_conv_w', 'new_v_ssd_conv_b', 'new_v_dt_bias', 'new_v_a_log', 'new_v_d_skip', 'new_v_ssd_norm_w', 'new_v_w_s_out', 'new_v_w_o', 'new_v_norm_ffn_w', 'new_v_w_up', 'new_v_ffn_conv_w', 'new_v_ffn_conv_b', 'new_v_w_down', 'new_v_final_norm_w']
TWIN_LEAF_KINDS = {'loss': 'loss', 'grad_x': 'grad_x', 'grad_norm_mix_w': 'grad_w', 'grad_w_in': 'grad_w', 'grad_conv_a_w': 'grad_w', 'grad_w_a_out': 'grad_w', 'grad_ssd_conv_w': 'grad_w', 'grad_ssd_conv_b': 'grad_w', 'grad_dt_bias': 'grad_w', 'grad_a_log': 'grad_w', 'grad_d_skip': 'grad_w', 'grad_ssd_norm_w': 'grad_w', 'grad_w_s_out': 'grad_w', 'grad_w_o': 'grad_w', 'grad_norm_ffn_w': 'grad_w', 'grad_w_up': 'grad_w', 'grad_ffn_conv_w': 'grad_w', 'grad_ffn_conv_b': 'grad_w', 'grad_w_down': 'grad_w', 'grad_final_norm_w': 'grad_w', 'delta_norm_mix_w': 'delta_w', 'delta_w_in': 'delta_w', 'delta_conv_a_w': 'delta_w', 'delta_w_a_out': 'delta_w', 'delta_ssd_conv_w': 'delta_w', 'delta_ssd_conv_b': 'delta_w', 'delta_dt_bias': 'delta_w', 'delta_a_log': 'delta_w', 'delta_d_skip': 'delta_w', 'delta_ssd_norm_w': 'delta_w', 'delta_w_s_out': 'delta_w', 'delta_w_o': 'delta_w', 'delta_norm_ffn_w': 'delta_w', 'delta_w_up': 'delta_w', 'delta_ffn_conv_w': 'delta_w', 'delta_ffn_conv_b': 'delta_w', 'delta_w_down': 'delta_w', 'delta_final_norm_w': 'delta_w', 'new_m_norm_mix_w': 'new_m', 'new_m_w_in': 'new_m', 'new_m_conv_a_w': 'new_m', 'new_m_w_a_out': 'new_m', 'new_m_ssd_conv_w': 'new_m', 'new_m_ssd_conv_b': 'new_m', 'new_m_dt_bias': 'new_m', 'new_m_a_log': 'new_m', 'new_m_d_skip': 'new_m', 'new_m_ssd_norm_w': 'new_m', 'new_m_w_s_out': 'new_m', 'new_m_w_o': 'new_m', 'new_m_norm_ffn_w': 'new_m', 'new_m_w_up': 'new_m', 'new_m_ffn_conv_w': 'new_m', 'new_m_ffn_conv_b': 'new_m', 'new_m_w_down': 'new_m', 'new_m_final_norm_w': 'new_m', 'new_v_norm_mix_w': 'new_v', 'new_v_w_in': 'new_v', 'new_v_conv_a_w': 'new_v', 'new_v_w_a_out': 'new_v', 'new_v_ssd_conv_w': 'new_v', 'new_v_ssd_conv_b': 'new_v', 'new_v_dt_bias': 'new_v', 'new_v_a_log': 'new_v', 'new_v_d_skip': 'new_v', 'new_v_ssd_norm_w': 'new_v', 'new_v_w_s_out': 'new_v', 'new_v_w_o': 'new_v', 'new_v_norm_ffn_w': 'new_v', 'new_v_w_up': 'new_v', 'new_v_ffn_conv_w': 'new_v', 'new_v_ffn_conv_b': 'new_v', 'new_v_w_down': 'new_v', 'new_v_final_norm_w': 'new_v'}


def _forward(args):
    return _fwd_reference(*[args[k] for k in FWD_PARAMS])


def _output_shape():
    def fwd():
        inp = _fwd_setup_inputs(0)
        return _fwd_reference(*[inp[k] for k in FWD_PARAMS])
    out = _jax.eval_shape(fwd)
    return out.shape, out.dtype

N_MICROBATCH = 1
ADAM_LR = 0.001
ADAM_B1 = 0.9
ADAM_B2 = 0.999
ADAM_EPS = 1e-08
ADAM_WD = 0.01
ADAM_STEP = 10
PER_EXAMPLE_BATCH_AXIS = {'x': 0, 'loss_target': 0}
SHARED_INPUTS = []
_WEIGHT_DTYPES = {'norm_mix_w': _jnp.float32, 'w_in': _jnp.float32, 'conv_a_w': _jnp.float32, 'w_a_out': _jnp.float32, 'ssd_conv_w': _jnp.float32, 'ssd_conv_b': _jnp.float32, 'dt_bias': _jnp.float32, 'a_log': _jnp.float32, 'd_skip': _jnp.float32, 'ssd_norm_w': _jnp.float32, 'w_s_out': _jnp.float32, 'w_o': _jnp.float32, 'norm_ffn_w': _jnp.float32, 'w_up': _jnp.float32, 'ffn_conv_w': _jnp.float32, 'ffn_conv_b': _jnp.float32, 'w_down': _jnp.float32, 'final_norm_w': _jnp.float32}
MOMENT_SCALE = {'norm_mix_w': 2.864175e-01, 'w_in': 8.745362e-02, 'conv_a_w': 1.195617e-01, 'w_a_out': 1.169660e-01, 'ssd_conv_w': 7.161084e-02, 'ssd_conv_b': 9.990487e-02, 'dt_bias': 1.296290e-01, 'a_log': 2.974728e-01, 'd_skip': 6.513709e-01, 'ssd_norm_w': 8.905948e-02, 'w_s_out': 1.199271e-01, 'w_o': 1.685594e-01, 'norm_ffn_w': 1.574583e-01, 'w_up': 6.821549e-02, 'ffn_conv_w': 6.903354e-02, 'ffn_conv_b': 6.558998e-02, 'w_down': 1.116693e-01, 'final_norm_w': 6.400726e+01}


def _to_microbatches(a, axis):
    t = _jnp.moveaxis(a, axis, 0)
    t = t.reshape((N_MICROBATCH, t.shape[0] // N_MICROBATCH) + t.shape[1:])
    return _jnp.moveaxis(t, 1, axis + 1)


def setup_inputs(seed: int = 0) -> dict:
    inp = _fwd_setup_inputs(seed)
    key = _jax.random.fold_in(_jax.random.key(seed), 7919)
    shape, _ = _output_shape()
    out = dict(inp)
    out["loss_target"] = _jax.random.normal(_jax.random.fold_in(key, 0), shape, _jnp.float32)
    for i, name in enumerate(TWIN_WEIGHTS):
        w = inp[name].astype(_jnp.float32)
        if MOMENT_SCALE is None:
            s = _jnp.sqrt(_jnp.mean(_jnp.square(w)) + 1e-30)
        else:
            s = MOMENT_SCALE[name]
        km, kv = _jax.random.split(_jax.random.fold_in(key, i + 1))
        out[name] = w
        out["m_" + name] = s * _jax.random.normal(km, w.shape, _jnp.float32)
        out["v_" + name] = (s * s) * _jax.random.uniform(kv, w.shape, _jnp.float32, 0.5, 1.5)
    if N_MICROBATCH > 1:
        for name, axis in PER_EXAMPLE_BATCH_AXIS.items():
            out[name] = _to_microbatches(out[name], axis)
    return {'x': out['x'], 'norm_mix_w': out['norm_mix_w'], 'w_in': out['w_in'], 'conv_a_w': out['conv_a_w'], 'w_a_out': out['w_a_out'], 'ssd_conv_w': out['ssd_conv_w'], 'ssd_conv_b': out['ssd_conv_b'], 'dt_bias': out['dt_bias'], 'a_log': out['a_log'], 'd_skip': out['d_skip'], 'ssd_norm_w': out['ssd_norm_w'], 'w_s_out': out['w_s_out'], 'w_o': out['w_o'], 'norm_ffn_w': out['norm_ffn_w'], 'w_up': out['w_up'], 'ffn_conv_w': out['ffn_conv_w'], 'ffn_conv_b': out['ffn_conv_b'], 'w_down': out['w_down'], 'final_norm_w': out['final_norm_w'], 'loss_target': out['loss_target'], 'm_norm_mix_w': out['m_norm_mix_w'], 'm_w_in': out['m_w_in'], 'm_conv_a_w': out['m_conv_a_w'], 'm_w_a_out': out['m_w_a_out'], 'm_ssd_conv_w': out['m_ssd_conv_w'], 'm_ssd_conv_b': out['m_ssd_conv_b'], 'm_dt_bias': out['m_dt_bias'], 'm_a_log': out['m_a_log'], 'm_d_skip': out['m_d_skip'], 'm_ssd_norm_w': out['m_ssd_norm_w'], 'm_w_s_out': out['m_w_s_out'], 'm_w_o': out['m_w_o'], 'm_norm_ffn_w': out['m_norm_ffn_w'], 'm_w_up': out['m_w_up'], 'm_ffn_conv_w': out['m_ffn_conv_w'], 'm_ffn_conv_b': out['m_ffn_conv_b'], 'm_w_down': out['m_w_down'], 'm_final_norm_w': out['m_final_norm_w'], 'v_norm_mix_w': out['v_norm_mix_w'], 'v_w_in': out['v_w_in'], 'v_conv_a_w': out['v_conv_a_w'], 'v_w_a_out': out['v_w_a_out'], 'v_ssd_conv_w': out['v_ssd_conv_w'], 'v_ssd_conv_b': out['v_ssd_conv_b'], 'v_dt_bias': out['v_dt_bias'], 'v_a_log': out['v_a_log'], 'v_d_skip': out['v_d_skip'], 'v_ssd_norm_w': out['v_ssd_norm_w'], 'v_w_s_out': out['v_w_s_out'], 'v_w_o': out['v_w_o'], 'v_norm_ffn_w': out['v_norm_ffn_w'], 'v_w_up': out['v_w_up'], 'v_ffn_conv_w': out['v_ffn_conv_w'], 'v_ffn_conv_b': out['v_ffn_conv_b'], 'v_w_down': out['v_w_down'], 'v_final_norm_w': out['v_final_norm_w']}


def _loss(weights, diff, rest, loss_target):
    with _jax.named_scope("forward"):
        args = {**rest, TWIN_DIFF_INPUT: diff, **{k: w.astype(_WEIGHT_DTYPES[k]) for k, w in weights.items()}}
        y = _forward(args)
    with _jax.named_scope("loss_head"):
        err = _jnp.square(y.astype(_jnp.float32) - loss_target)
        return 0.5 * _jnp.sum(_jnp.mean(err, axis=-1)) if err.ndim else 0.5 * err


def _adamw(w, g, m, v):
    m = ADAM_B1 * m + (1.0 - ADAM_B1) * g
    v = ADAM_B2 * v + (1.0 - ADAM_B2) * _jnp.square(g)
    m_hat = m / (1.0 - ADAM_B1 ** ADAM_STEP)
    v_hat = v / (1.0 - ADAM_B2 ** ADAM_STEP)
    delta = -ADAM_LR * (m_hat / (_jnp.sqrt(v_hat) + ADAM_EPS) + ADAM_WD * w)
    return delta, m, v


def reference(x, norm_mix_w, w_in, conv_a_w, w_a_out, ssd_conv_w, ssd_conv_b, dt_bias, a_log, d_skip, ssd_norm_w, w_s_out, w_o, norm_ffn_w, w_up, ffn_conv_w, ffn_conv_b, w_down, final_norm_w, loss_target, m_norm_mix_w, m_w_in, m_conv_a_w, m_w_a_out, m_ssd_conv_w, m_ssd_conv_b, m_dt_bias, m_a_log, m_d_skip, m_ssd_norm_w, m_w_s_out, m_w_o, m_norm_ffn_w, m_w_up, m_ffn_conv_w, m_ffn_conv_b, m_w_down, m_final_norm_w, v_norm_mix_w, v_w_in, v_conv_a_w, v_w_a_out, v_ssd_conv_w, v_ssd_conv_b, v_dt_bias, v_a_log, v_d_skip, v_ssd_norm_w, v_w_s_out, v_w_o, v_norm_ffn_w, v_w_up, v_ffn_conv_w, v_ffn_conv_b, v_w_down, v_final_norm_w):
    given = dict(x=x, norm_mix_w=norm_mix_w, w_in=w_in, conv_a_w=conv_a_w, w_a_out=w_a_out, ssd_conv_w=ssd_conv_w, ssd_conv_b=ssd_conv_b, dt_bias=dt_bias, a_log=a_log, d_skip=d_skip, ssd_norm_w=ssd_norm_w, w_s_out=w_s_out, w_o=w_o, norm_ffn_w=norm_ffn_w, w_up=w_up, ffn_conv_w=ffn_conv_w, ffn_conv_b=ffn_conv_b, w_down=w_down, final_norm_w=final_norm_w, loss_target=loss_target, m_norm_mix_w=m_norm_mix_w, m_w_in=m_w_in, m_conv_a_w=m_conv_a_w, m_w_a_out=m_w_a_out, m_ssd_conv_w=m_ssd_conv_w, m_ssd_conv_b=m_ssd_conv_b, m_dt_bias=m_dt_bias, m_a_log=m_a_log, m_d_skip=m_d_skip, m_ssd_norm_w=m_ssd_norm_w, m_w_s_out=m_w_s_out, m_w_o=m_w_o, m_norm_ffn_w=m_norm_ffn_w, m_w_up=m_w_up, m_ffn_conv_w=m_ffn_conv_w, m_ffn_conv_b=m_ffn_conv_b, m_w_down=m_w_down, m_final_norm_w=m_final_norm_w, v_norm_mix_w=v_norm_mix_w, v_w_in=v_w_in, v_conv_a_w=v_conv_a_w, v_w_a_out=v_w_a_out, v_ssd_conv_w=v_ssd_conv_w, v_ssd_conv_b=v_ssd_conv_b, v_dt_bias=v_dt_bias, v_a_log=v_a_log, v_d_skip=v_d_skip, v_ssd_norm_w=v_ssd_norm_w, v_w_s_out=v_w_s_out, v_w_o=v_w_o, v_norm_ffn_w=v_norm_ffn_w, v_w_up=v_w_up, v_ffn_conv_w=v_ffn_conv_w, v_ffn_conv_b=v_ffn_conv_b, v_w_down=v_w_down, v_final_norm_w=v_final_norm_w)
    weights = {n: given[n] for n in TWIN_WEIGHTS}
    shared = {n: given[n] for n in SHARED_INPUTS}
    per_example = {n: given[n] for n in ['x']}
    grad_fn = _jax.value_and_grad(_loss, argnums=(0, 1))

    def one_microbatch(ex, loss_target):
        ex = dict(ex)
        diff = ex.pop(TWIN_DIFF_INPUT)
        return grad_fn(weights, diff, {**shared, **ex}, loss_target)

    if N_MICROBATCH == 1:
        loss, (grad_w, grad_x) = one_microbatch(per_example, given["loss_target"])
    else:
        def body(carry, xs):
            loss_sum, grad_sum = carry
            l_k, (gw_k, gx_k) = one_microbatch(xs[0], xs[1])
            with _jax.named_scope("update"):
                return (loss_sum + l_k, _jax.tree.map(_jnp.add, grad_sum, gw_k)), gx_k

        init = (_jnp.zeros((), _jnp.float32), _jax.tree.map(_jnp.zeros_like, weights))
        (loss, grad_w), grad_x = _jax.lax.scan(body, init, (per_example, given["loss_target"]))
    with _jax.named_scope("update"):
        delta_w, new_m, new_v = {}, {}, {}
        for n in TWIN_WEIGHTS:
            delta_w[n], new_m[n], new_v[n] = _adamw(weights[n], grad_w[n], given["m_" + n], given["v_" + n])
    return (loss, grad_x, *[grad_w[n] for n in TWIN_WEIGHTS], *[delta_w[n] for n in TWIN_WEIGHTS],
            *[new_m[n] for n in TWIN_WEIGHTS], *[new_v[n] for n in TWIN_WEIGHTS])
```

```python
import functools

import jax
import jax.numpy as jnp
import numpy as np
from jax import lax
from jax.experimental import pallas as pl
from jax.experimental.pallas import tpu as pltpu

F32 = jnp.float32
BF16 = jnp.bfloat16

D = 1024
DI = 2048
NH = 32
HP = 64
NG = 4
NS = 128
CH = 128
DX = 3072
FF = 2816
NI = 10272
EPS = 1e-5

OFF_BCV, OFF_XBC, OFF_G, OFF_Z, OFF_DT = 0, 3072, 6144, 8192, 10240
NIP = 10368
_SEGS = ((0, 2048, OFF_G), (2048, 3072, OFF_BCV), (5120, 2048, OFF_Z), (7168, 3072, OFF_XBC), (10240, 32, OFF_DT))

LANES = 128
HALO = 16
V7X_VMEM_LIMIT = 56 * 2 ** 20

ADAM_LR, ADAM_B1, ADAM_B2, ADAM_EPS, ADAM_WD, ADAM_STEP = 0.001, 0.9, 0.999, 1e-08, 0.01, 10

NN = (((1,), (0,)), ((), ()))
NT = (((1,), (1,)), ((), ()))
TN = (((0,), (0,)), ((), ()))


def _dot(a, b, dims=NN):
    return lax.dot_general(a, b, dims, preferred_element_type=F32)


def _params(sem, **kw):
    return pltpu.CompilerParams(dimension_semantics=sem, vmem_limit_bytes=V7X_VMEM_LIMIT, **kw)


def _pick(dim, cap):
    if dim <= cap:
        return dim
    best = None
    for t in range(LANES, cap + 1, LANES):
        if dim % t == 0:
            best = t
    assert best is not None, (dim, cap)
    return best


def _sigmoid(x):
    return 1.0 / (1.0 + jnp.exp(-x))


def _matmul(a, b, *, mode, out_dtype, name, residual=None, caps=(1408, 1408, 1408)):
    if mode == "nn":
        (M, K), (K2, N) = a.shape, b.shape
    elif mode == "nt":
        (M, K), (N, K2) = a.shape, b.shape
    else:
        (K, M), (K2, N) = a.shape, b.shape
    assert K == K2, (name, a.shape, b.shape)
    tm, tn, tk = _pick(M, caps[0]), _pick(N, caps[1]), _pick(K, caps[2])
    nk = K // tk
    if mode == "tn":
        a_spec = pl.BlockSpec((tk, tm), lambda i, j, k: (k, i))
    else:
        a_spec = pl.BlockSpec((tm, tk), lambda i, j, k: (i, k))
    if mode == "nt":
        b_spec = pl.BlockSpec((tn, tk), lambda i, j, k: (j, k))
    else:
        b_spec = pl.BlockSpec((tk, tn), lambda i, j, k: (k, j))
    dims = {"nn": NN, "nt": NT, "tn": TN}[mode]
    o_spec = pl.BlockSpec((tm, tn), lambda i, j, k: (i, j))
    has_res = residual is not None

    def body(*refs):
        if has_res:
            a_ref, b_ref, r_ref, o_ref, acc_ref = refs
        else:
            a_ref, b_ref, o_ref, acc_ref = refs
        k = pl.program_id(2)

        @pl.when(k == 0)
        def _():
            acc_ref[...] = jnp.zeros_like(acc_ref)

        acc_ref[...] += _dot(a_ref[...], b_ref[...], dims)

        @pl.when(k == nk - 1)
        def _():
            r = acc_ref[...]
            if has_res:
                r = r + r_ref[...].astype(F32)
            o_ref[...] = r.astype(out_dtype)

    in_specs = [a_spec, b_spec] + ([o_spec] if has_res else [])
    args = (a, b) + ((residual,) if has_res else ())
    return pl.pallas_call(
        body, name=name, grid=(M // tm, N // tn, nk), in_specs=in_specs, out_specs=o_spec,
        out_shape=jax.ShapeDtypeStruct((M, N), out_dtype),
        scratch_shapes=[pltpu.VMEM((tm, tn), F32)],
        compiler_params=_params(("parallel", "parallel", "arbitrary")),
    )(*args)


class _Rows:
    def __init__(self, T, tm):
        self.T, self.tm = T, min(tm, T // 2)
        self.nrow = T // self.tm
        self.r = self.tm // HALO
        self.nb = T // HALO

    def tile(self, w, cb=0, step=1):
        return pl.BlockSpec((self.tm, w), lambda j, i: (i, cb + step * j))

    def prev(self, w, cb=0, step=1):
        r = self.r
        return pl.BlockSpec((HALO, w), lambda j, i: (jnp.maximum(i * r - 1, 0), cb + step * j))

    def next(self, w, cb=0, step=1):
        r, nb = self.r, self.nb
        return pl.BlockSpec((HALO, w), lambda j, i: (jnp.minimum((i + 1) * r, nb - 1), cb + step * j))

    def colvec(self, k, w, cb=0, step=1):
        return pl.BlockSpec((k, w), lambda j, i: (0, cb + step * j))

    def call(self, body, name, ncol, in_specs, out_specs, out_shape, args, aliases=None):
        return pl.pallas_call(
            body, name=name, grid=(ncol, self.nrow), in_specs=in_specs, out_specs=out_specs,
            out_shape=out_shape, input_output_aliases=aliases or {},
            compiler_params=_params(("parallel", "arbitrary")),
        )(*args)


ANY = pl.BlockSpec(memory_space=pl.ANY)


def _shifts_causal(ext, nk, tm):
    out = []
    for k in range(nk):
        s = nk - 1 - k
        r = ext if s == 0 else pltpu.roll(ext, s, 0)
        out.append(r[HALO:])
    return out


def _shifts_anticausal(ext, nk, tm):
    n = ext.shape[0]
    out = []
    for k in range(nk):
        s = nk - 1 - k
        r = ext if s == 0 else pltpu.roll(ext, n - s, 0)
        out.append(r[:tm])
    return out


def _wsum(w, parts):
    acc = w[0:1, :] * parts[0]
    for k in range(1, len(parts)):
        acc = acc + w[k:k + 1, :] * parts[k]
    return acc


def _colsum(x):
    return jnp.sum(x, axis=0, keepdims=True)


def _acc_out(ref, val, first):
    @pl.when(first)
    def _():
        ref[...] = val

    @pl.when(jnp.logical_not(first))
    def _():
        ref[...] += val


def _acc_rows(ref, rows, first):
    for k, r in enumerate(rows):
        _acc_out(ref.at[k:k + 1, :], r, first)


def _rmsnorm_fwd(x, w, name):
    T = x.shape[0]
    R = _Rows(T, 256)

    def body(x_ref, w_ref, o_ref):
        xv = x_ref[...]
        r = lax.rsqrt(jnp.mean(xv * xv, axis=-1, keepdims=True) + EPS)
        o_ref[...] = (xv * r * w_ref[...]).astype(BF16)

    return R.call(body, name, 1, [R.tile(D), R.colvec(1, D)], R.tile(D),
                  jax.ShapeDtypeStruct((T, D), BF16), (x, w))


def _rmsnorm_bwd(dy, x, w, dres, name):
    T = x.shape[0]
    R = _Rows(T, 256)

    def body(dy_ref, x_ref, w_ref, dr_ref, dx_ref, dw_ref):
        xv = x_ref[...]
        r = lax.rsqrt(jnp.mean(xv * xv, axis=-1, keepdims=True) + EPS)
        xh = xv * r
        dyv = dy_ref[...].astype(F32)
        dxh = dyv * w_ref[...]
        dx = r * (dxh - xh * jnp.mean(dxh * xh, axis=-1, keepdims=True))
        dx_ref[...] = dx + dr_ref[...]
        _acc_out(dw_ref, _colsum(dyv * xh), pl.program_id(1) == 0)

    return R.call(body, name, 1, [R.tile(D), R.tile(D), R.colvec(1, D), R.tile(D)],
                  [R.tile(D), R.colvec(1, D)],
                  [jax.ShapeDtypeStruct((T, D), F32), jax.ShapeDtypeStruct((1, D), F32)],
                  (dy, x, w, dres))


def _branch_a_fwd(proj, conv_w):
    T = proj.shape[0]
    R = _Rows(T, 256)
    tm = R.tm

    def body(p_ref, pp_ref, w_ref, o_ref):
        keep = (pl.program_id(1) > 0).astype(F32)
        cv = p_ref[:, D:2 * D].astype(F32) * p_ref[:, 2 * D:].astype(F32)
        cvp = pp_ref[:, D:2 * D].astype(F32) * pp_ref[:, 2 * D:].astype(F32) * keep
        sh = _shifts_causal(jnp.concatenate([cvp, cv], axis=0), 3, tm)
        ca = _wsum(w_ref[...], sh)
        o_ref[...] = (p_ref[:, :D].astype(F32) * ca).astype(BF16)

    return R.call(body, "branch_a_fwd", 1, [R.tile(3 * D), R.prev(3 * D), R.colvec(3, D)], R.tile(D),
                  jax.ShapeDtypeStruct((T, D), BF16), (proj, proj, conv_w))


def _branch_a_bwd(dya_in, proj, conv_w, dproj):
    T = proj.shape[0]
    R = _Rows(T, 256)
    tm = R.tm

    def body(d_ref, dn_ref, p_ref, pp_ref, pn_ref, w_ref, _alias, o_ref, dw_ref):
        i = pl.program_id(1)
        keep_p = (i > 0).astype(F32)
        keep_n = (i < R.nrow - 1).astype(F32)
        w = w_ref[...]
        b = p_ref[:, :D].astype(F32)
        c = p_ref[:, D:2 * D].astype(F32)
        v = p_ref[:, 2 * D:].astype(F32)
        cvp = pp_ref[:, D:2 * D].astype(F32) * pp_ref[:, 2 * D:].astype(F32) * keep_p
        sh = _shifts_causal(jnp.concatenate([cvp, c * v], axis=0), 3, tm)
        ca = _wsum(w, sh)
        d = d_ref[...].astype(F32)
        dca = d * b
        dca_n = dn_ref[...].astype(F32) * pn_ref[:, :D].astype(F32) * keep_n
        dsh = _shifts_anticausal(jnp.concatenate([dca, dca_n], axis=0), 3, tm)
        dcv = _wsum(w, dsh)
        o_ref[:, :D] = (d * ca).astype(BF16)
        o_ref[:, D:2 * D] = (dcv * v).astype(BF16)
        o_ref[:, 2 * D:] = (dcv * c).astype(BF16)
        _acc_rows(dw_ref, [_colsum(dca * s) for s in sh], i == 0)

    return R.call(
        body, "branch_a_bwd", 1,
        [R.tile(D), R.next(D), R.tile(3 * D), R.prev(3 * D), R.next(3 * D), R.colvec(3, D), ANY],
        [R.tile(3 * D), R.colvec(3, D)],
        [jax.ShapeDtypeStruct(dproj.shape, BF16), jax.ShapeDtypeStruct((3, D), F32)],
        (dya_in, dya_in, proj, proj, proj, conv_w, dproj), aliases={6: 0})


_XW = 512


def _xbc_fwd(proj, conv_w, conv_b):
    T = proj.shape[0]
    R = _Rows(T, 512)
    tm = R.tm
    cb = OFF_XBC // _XW

    def body(x_ref, xp_ref, w_ref, b_ref, o_ref):
        keep = (pl.program_id(1) > 0).astype(F32)
        ext = jnp.concatenate([xp_ref[...].astype(F32) * keep, x_ref[...].astype(F32)], axis=0)
        pre = _wsum(w_ref[...], _shifts_causal(ext, 4, tm)) + b_ref[...]
        o_ref[...] = (pre * _sigmoid(pre)).astype(BF16)

    return R.call(body, "xbc_fwd", DX // _XW,
                  [R.tile(_XW, cb), R.prev(_XW, cb), R.colvec(4, _XW), R.colvec(1, _XW)], R.tile(_XW),
                  jax.ShapeDtypeStruct((T, DX), BF16), (proj, proj, conv_w, conv_b))


def _xbc_bwd(dact, proj, conv_w, conv_b, dproj):
    T = proj.shape[0]
    R = _Rows(T, 512)
    tm = R.tm
    cb = OFF_XBC // _XW

    def body(d_ref, dn_ref, x_ref, xp_ref, xn_ref, w_ref, b_ref, _alias, o_ref, dw_ref, db_ref):
        i = pl.program_id(1)
        keep_p = (i > 0).astype(F32)
        keep_n = (i < R.nrow - 1).astype(F32)
        w = w_ref[...]
        ext = jnp.concatenate([xp_ref[...].astype(F32) * keep_p, x_ref[...].astype(F32),
                               xn_ref[...].astype(F32)], axis=0)
        sh = _shifts_causal(ext, 4, tm + HALO)
        pre = _wsum(w, sh) + b_ref[...]
        s = _sigmoid(pre)
        dsilu = s * (1.0 + pre * (1.0 - s))
        dext = jnp.concatenate([d_ref[...].astype(F32), dn_ref[...].astype(F32) * keep_n], axis=0)
        dpre = dext * dsilu
        dsh = _shifts_anticausal(dpre, 4, tm)
        o_ref[...] = _wsum(w, dsh).astype(BF16)
        dp = dpre[:tm]
        _acc_rows(dw_ref, [_colsum(dp * q[:tm]) for q in sh], i == 0)
        _acc_out(db_ref, _colsum(dp), i == 0)

    return R.call(
        body, "xbc_bwd", DX // _XW,
        [R.tile(_XW), R.next(_XW), R.tile(_XW, cb), R.prev(_XW, cb), R.next(_XW, cb),
         R.colvec(4, _XW), R.colvec(1, _XW), ANY],
        [R.tile(_XW, cb), R.colvec(4, _XW), R.colvec(1, _XW)],
        [jax.ShapeDtypeStruct(dproj.shape, BF16), jax.ShapeDtypeStruct((4, DX), F32),
         jax.ShapeDtypeStruct((1, DX), F32)],
        (dact, dact, proj, proj, proj, conv_w, conv_b, dproj), aliases={7: 0})


def _softplus(x):
    return jnp.maximum(x, 0.0) + jnp.log(1.0 + jnp.exp(-jnp.abs(x)))


def _dt_fwd(dt_raw, dt_bias_p, a_log_p):
    T = dt_raw.shape[0]

    def body(r_ref, b_ref, al_ref, dt_ref, ac_ref):
        dt = _softplus(r_ref[...] + b_ref[...])
        s = dt * (-jnp.exp(al_ref[...]))
        row = lax.broadcasted_iota(jnp.int32, (CH, LANES), 0)
        k = 1
        while k < CH:
            s = s + jnp.where(row >= k, pltpu.roll(s, k, 0), 0.0)
            k *= 2
        dt_ref[...] = dt
        ac_ref[...] = s

    blk = pl.BlockSpec((CH, LANES), lambda i: (i, 0))
    vec = pl.BlockSpec((1, LANES), lambda i: (0, 0))
    return pl.pallas_call(
        body, name="dt_fwd", grid=(T // CH,), in_specs=[blk, vec, vec], out_specs=[blk, blk],
        out_shape=[jax.ShapeDtypeStruct((T, LANES), F32)] * 2, compiler_params=_params(("parallel",)),
    )(dt_raw, dt_bias_p, a_log_p)


def _dt_bwd(dacum, ddt_x, dt_raw, dt_bias_p, a_log_p, dproj):
    T = dt_raw.shape[0]
    nc = T // CH

    def body(da_ref, dx_ref, r_ref, b_ref, al_ref, _alias, o_ref, db_ref, dal_ref):
        i = pl.program_id(0)
        a = -jnp.exp(al_ref[...])
        z = r_ref[...] + b_ref[...]
        dt = _softplus(z)
        s = da_ref[...]
        row = lax.broadcasted_iota(jnp.int32, (CH, LANES), 0)
        k = 1
        while k < CH:
            s = s + jnp.where(row < CH - k, pltpu.roll(s, CH - k, 0), 0.0)
            k *= 2
        ddt = s * a + dx_ref[...]
        draw = ddt * _sigmoid(z)
        o_ref[...] = draw.astype(BF16)
        _acc_out(db_ref, _colsum(draw), i == 0)
        _acc_out(dal_ref, _colsum(s * dt), i == 0)

        @pl.when(i == nc - 1)
        def _():
            dal_ref[...] = dal_ref[...] * a

    blk = pl.BlockSpec((CH, LANES), lambda i: (i, 0))
    vec = pl.BlockSpec((1, LANES), lambda i: (0, 0))
    oblk = pl.BlockSpec((CH, LANES), lambda i: (i, OFF_DT // LANES))
    return pl.pallas_call(
        body, name="dt_bwd", grid=(nc,), in_specs=[blk, blk, blk, vec, vec, ANY], out_specs=[oblk, vec, vec],
        out_shape=[jax.ShapeDtypeStruct(dproj.shape, BF16), jax.ShapeDtypeStruct((1, LANES), F32),
                   jax.ShapeDtypeStruct((1, LANES), F32)],
        input_output_aliases={5: 0}, compiler_params=_params(("arbitrary",)),
    )(dacum, ddt_x, dt_raw, dt_bias_p, a_log_p, dproj)


_GW = DI // NG
_HG = NH // NG
_NEG = -1e30


def _pair_lanes(left, v0, v1):
    return jnp.where(left, v0, v1)


def _ssd_specs(T, rev):
    nc = T // CH
    cm = (lambda c: nc - 1 - c) if rev else (lambda c: c)
    return dict(
        xs=pl.BlockSpec((CH, _GW), lambda g, c: (cm(c), g)),
        bm=pl.BlockSpec((CH, NS), lambda g, c: (cm(c), DI // NS + g)),
        cmat=pl.BlockSpec((CH, NS), lambda g, c: (cm(c), DI // NS + NG + g)),
        col=pl.BlockSpec((1, CH, _HG), lambda g, c: (g, cm(c), 0)),
        rowT=pl.BlockSpec((1, _HG, CH), lambda g, c: (g, 0, cm(c))),
        dsk=pl.BlockSpec((1, _GW), lambda g, c: (0, g)),
        state=pl.BlockSpec((1, 1, NS, _GW), lambda g, c: (g, cm(c), 0, 0)),
    )


def _ssd_fwd(xact, dtg, acg, acgT, dsk_rep):
    T = xact.shape[0]
    nc = T // CH
    sp = _ssd_specs(T, False)

    def body(xs_ref, b_ref, c_ref, dt_ref, ac_ref, acT_ref, dsk_ref, y_ref, sp_ref, S_ref):
        @pl.when(pl.program_id(1) == 0)
        def _():
            S_ref[...] = jnp.zeros_like(S_ref)

        Bm, Cm = b_ref[...], c_ref[...]
        dt, ac, acT = dt_ref[0], ac_ref[0], acT_ref[0]
        S = S_ref[...]
        sp_ref[0, 0] = S
        cb = _dot(Cm, Bm, NT)
        CS = _dot(Cm, S.astype(BF16))
        row = lax.broadcasted_iota(jnp.int32, (CH, CH), 0)
        col = lax.broadcasted_iota(jnp.int32, (CH, CH), 1)
        tril = row >= col
        left = col < HP
        xd_parts, dec_parts = [], []
        for p in range(_HG // 2):
            sl = slice(p * LANES, (p + 1) * LANES)
            j0, j1 = 2 * p, 2 * p + 1
            xp = xs_ref[:, sl].astype(F32)
            a0, a1 = ac[:, j0:j0 + 1], ac[:, j1:j1 + 1]
            al0, al1 = ac[CH - 1:CH, j0:j0 + 1], ac[CH - 1:CH, j1:j1 + 1]
            X = xp * _pair_lanes(left, dt[:, j0:j0 + 1], dt[:, j1:j1 + 1])
            Xb = X.astype(BF16)
            yd = jnp.zeros((CH, LANES), F32)
            for j, aj, mask in ((j0, a0, left), (j1, a1, jnp.logical_not(left))):
                Lm = jnp.exp(jnp.where(tril, aj - acT[j:j + 1, :], _NEG))
                W = (cb * Lm).astype(BF16)
                yd = yd + _dot(W, jnp.where(mask, Xb, jnp.zeros_like(Xb)))
            eal = _pair_lanes(left, jnp.exp(a0), jnp.exp(a1))
            y = yd + eal * CS[:, sl] + dsk_ref[:, sl] * xp
            y_ref[:, sl] = y.astype(BF16)
            xd_parts.append(X * _pair_lanes(left, jnp.exp(al0 - a0), jnp.exp(al1 - a1)))
            dec_parts.append(_pair_lanes(left[0:1], jnp.exp(al0), jnp.exp(al1)))
        Xd = jnp.concatenate(xd_parts, axis=1).astype(BF16)
        dec = jnp.concatenate(dec_parts, axis=1)
        S_ref[...] = dec * S + _dot(Bm, Xd, TN)

    return pl.pallas_call(
        body, name="ssd_fwd", grid=(NG, nc),
        in_specs=[sp["xs"], sp["bm"], sp["cmat"], sp["col"], sp["col"], sp["rowT"], sp["dsk"]],
        out_specs=[sp["xs"], sp["state"]],
        out_shape=[jax.ShapeDtypeStruct((T, DI), BF16), jax.ShapeDtypeStruct((NG, nc, NS, _GW), F32)],
        scratch_shapes=[pltpu.VMEM((NS, _GW), F32)],
        compiler_params=_params(("parallel", "arbitrary")),
    )(xact, xact, xact, dtg, acg, acgT, dsk_rep)


def _ssd_bwd(dy, xact, dtg, acg, acgT, dsk_rep, sprev):
    T = xact.shape[0]
    nc = T // CH
    sp = _ssd_specs(T, True)
    cm = lambda c: nc - 1 - c

    def body(xs_ref, b_ref, c_ref, dt_ref, ac_ref, acT_ref, dsk_ref, dy_ref, sp_ref,
             dx_ref, dB_ref, dC_ref, ddtx_ref, dAc_ref, dAr_ref, dskacc_ref, dS_ref):
        first = pl.program_id(1) == 0

        @pl.when(first)
        def _():
            dS_ref[...] = jnp.zeros_like(dS_ref)

        Bm, Cm = b_ref[...], c_ref[...]
        dt, ac, acT = dt_ref[0], ac_ref[0], acT_ref[0]
        S = sp_ref[0, 0]
        dS = dS_ref[...]
        Sb, dSb = S.astype(BF16), dS.astype(BF16)
        cb = _dot(Cm, Bm, NT)
        CS = _dot(Cm, Sb)
        T1 = _dot(Bm, dSb)
        row = lax.broadcasted_iota(jnp.int32, (CH, CH), 0)
        col = lax.broadcasted_iota(jnp.int32, (CH, CH), 1)
        tril = row >= col
        left = col < HP
        lane8 = lax.broadcasted_iota(jnp.int32, (1, _HG), 1)
        sub8 = lax.broadcasted_iota(jnp.int32, (_HG, 1), 0)
        lastrow = lax.broadcasted_iota(jnp.int32, (CH, 1), 0) == CH - 1
        dCB = jnp.zeros((CH, CH), F32)
        dAc = jnp.zeros((CH, _HG), F32)
        dAr = jnp.zeros((_HG, CH), F32)
        ddtx = jnp.zeros((CH, _HG), F32)
        xd_parts, dye_parts, dec_parts, dsk_parts = [], [], [], []
        for p in range(_HG // 2):
            sl = slice(p * LANES, (p + 1) * LANES)
            j0, j1 = 2 * p, 2 * p + 1
            xp = xs_ref[:, sl].astype(F32)
            dyp = dy_ref[:, sl].astype(F32)
            a0, a1 = ac[:, j0:j0 + 1], ac[:, j1:j1 + 1]
            al0, al1 = ac[CH - 1:CH, j0:j0 + 1], ac[CH - 1:CH, j1:j1 + 1]
            dtl = _pair_lanes(left, dt[:, j0:j0 + 1], dt[:, j1:j1 + 1])
            X = xp * dtl
            Xb = X.astype(BF16)
            eal = _pair_lanes(left, jnp.exp(a0), jnp.exp(a1))
            dtel = _pair_lanes(left, jnp.exp(al0 - a0), jnp.exp(al1 - a1))
            T1p = T1[:, sl]
            dXd = jnp.zeros((CH, LANES), F32)
            G1 = dyp * (eal * CS[:, sl])
            Rm = T1p * dtel * X
            SdS = dS[:, sl] * S[:, sl]
            for j, aj, alj, mask in ((j0, a0, al0, left), (j1, a1, al1, jnp.logical_not(left))):
                Lm = jnp.exp(jnp.where(tril, aj - acT[j:j + 1, :], _NEG))
                W = cb * Lm
                dYm = jnp.where(mask, dyp, 0.0).astype(BF16)
                dWm = _dot(dYm, Xb, NT)
                dCB = dCB + dWm * Lm
                Q = dWm * W
                dXd = dXd + _dot(W.astype(BF16), dYm, TN)
                g1 = jnp.sum(jnp.where(mask, G1, 0.0), axis=1, keepdims=True)
                r = jnp.sum(jnp.where(mask, Rm, 0.0), axis=1, keepdims=True)
                sds = jnp.sum(jnp.sum(jnp.where(mask, SdS, 0.0), axis=1, keepdims=True), axis=0, keepdims=True)
                dalast = jnp.sum(r, axis=0, keepdims=True) + jnp.exp(alj) * sds
                colv = jnp.sum(Q, axis=1, keepdims=True) + g1 - r + jnp.where(lastrow, dalast, 0.0)
                hot = (lane8 == j).astype(F32)
                dAc = dAc + colv * hot
                dAr = dAr - (sub8 == j).astype(F32) * jnp.sum(Q, axis=0, keepdims=True)
            dX = dXd + dtel * T1p
            dXx = dX * xp
            for j, mask in ((j0, left), (j1, jnp.logical_not(left))):
                dd = jnp.sum(jnp.where(mask, dXx, 0.0), axis=1, keepdims=True)
                ddtx = ddtx + dd * (lane8 == j).astype(F32)
            dx_ref[:, sl] = (dX * dtl + dsk_ref[:, sl] * dyp).astype(BF16)
            dsk_parts.append(_colsum(dyp * xp))
            xd_parts.append(X * dtel)
            dye_parts.append(dyp * eal)
            dec_parts.append(_pair_lanes(left[0:1], jnp.exp(al0), jnp.exp(al1)))
        Xd = jnp.concatenate(xd_parts, axis=1).astype(BF16)
        dYe = jnp.concatenate(dye_parts, axis=1).astype(BF16)
        dec = jnp.concatenate(dec_parts, axis=1)
        dCBb = dCB.astype(BF16)
        dC_ref[...] = (_dot(dCBb, Bm) + _dot(dYe, Sb, NT)).astype(BF16)
        dB_ref[...] = (_dot(dCBb, Cm, TN) + _dot(Xd, dSb, NT)).astype(BF16)
        dS_ref[...] = _dot(Cm, dYe, TN) + dec * dS
        ddtx_ref[0] = ddtx
        dAc_ref[0] = dAc
        dAr_ref[0] = dAr
        _acc_out(dskacc_ref, jnp.concatenate(dsk_parts, axis=1), first)

    dxs, dB, dC, ddtx, dAc, dAr, dskl = pl.pallas_call(
        body, name="ssd_bwd", grid=(NG, nc),
        in_specs=[sp["xs"], sp["bm"], sp["cmat"], sp["col"], sp["col"], sp["rowT"], sp["dsk"], sp["xs"],
                  sp["state"]],
        out_specs=[sp["xs"], pl.BlockSpec((CH, NS), lambda g, c: (cm(c), g)),
                   pl.BlockSpec((CH, NS), lambda g, c: (cm(c), g)), sp["col"], sp["col"], sp["rowT"],
                   sp["dsk"]],
        out_shape=[jax.ShapeDtypeStruct((T, DI), BF16), jax.ShapeDtypeStruct((T, NG * NS), BF16),
                   jax.ShapeDtypeStruct((T, NG * NS), BF16), jax.ShapeDtypeStruct((NG, T, _HG), F32),
                   jax.ShapeDtypeStruct((NG, T, _HG), F32), jax.ShapeDtypeStruct((NG, _HG, T), F32),
                   jax.ShapeDtypeStruct((1, DI), F32)],
        scratch_shapes=[pltpu.VMEM((NS, _GW), F32)],
        compiler_params=_params(("parallel", "arbitrary")),
    )(xact, xact, xact, dtg, acg, acgT, dsk_rep, dy, sprev)
    return dxs, dB, dC, ddtx, dAc, dAr, dskl


def _gnorm_fwd(y, proj, w):
    T = y.shape[0]
    R = _Rows(T, 512)
    zb = OFF_Z // _GW

    def body(y_ref, z_ref, w_ref, o_ref):
        z = z_ref[...].astype(F32)
        yf = y_ref[...].astype(F32) * z * _sigmoid(z)
        r = lax.rsqrt(jnp.mean(yf * yf, axis=-1, keepdims=True) + EPS)
        o_ref[...] = (yf * r * w_ref[...]).astype(BF16)

    return R.call(body, "gnorm_fwd", NG, [R.tile(_GW), R.tile(_GW, zb), R.colvec(1, _GW)], R.tile(_GW),
                  jax.ShapeDtypeStruct((T, DI), BF16), (y, proj, w))


def _gnorm_bwd(dn, y, proj, w, dproj):
    T = y.shape[0]
    R = _Rows(T, 512)
    zb = OFF_Z // _GW

    def body(dn_ref, y_ref, z_ref, w_ref, _alias, dz_ref, dy_ref, dw_ref):
        z = z_ref[...].astype(F32)
        yv = y_ref[...].astype(F32)
        s = _sigmoid(z)
        silu = z * s
        yf = yv * silu
        r = lax.rsqrt(jnp.mean(yf * yf, axis=-1, keepdims=True) + EPS)
        yh = yf * r
        dnv = dn_ref[...].astype(F32)
        dyh = dnv * w_ref[...]
        dyf = r * (dyh - yh * jnp.mean(dyh * yh, axis=-1, keepdims=True))
        dy_ref[...] = (dyf * silu).astype(BF16)
        dz_ref[...] = (dyf * yv * s * (1.0 + z * (1.0 - s))).astype(BF16)
        _acc_out(dw_ref, _colsum(dnv * yh), pl.program_id(1) == 0)

    return R.call(
        body, "gnorm_bwd", NG, [R.tile(_GW), R.tile(_GW), R.tile(_GW, zb), R.colvec(1, _GW), ANY],
        [R.tile(_GW, zb), R.tile(_GW), R.colvec(1, _GW)],
        [jax.ShapeDtypeStruct(dproj.shape, BF16), jax.ShapeDtypeStruct((T, DI), BF16),
         jax.ShapeDtypeStruct((1, DI), F32)],
        (dn, y, proj, w, dproj), aliases={4: 0})


def _merge_fwd(proj, ya, ys):
    T = proj.shape[0]
    R = _Rows(T, 256)
    gb = OFF_G // (2 * D)

    def body(g_ref, ya_ref, ys_ref, o_ref):
        ga = _sigmoid(g_ref[:, :D].astype(F32))
        gs = _sigmoid(g_ref[:, D:].astype(F32))
        o_ref[...] = (ga * ya_ref[...].astype(F32) + gs * ys_ref[...].astype(F32)).astype(BF16)

    return R.call(body, "merge_fwd", 1, [R.tile(2 * D, gb), R.tile(D), R.tile(D)], R.tile(D),
                  jax.ShapeDtypeStruct((T, D), BF16), (proj, ya, ys))


def _merge_bwd(dm, proj, ya, ys, ncols):
    T = proj.shape[0]
    R = _Rows(T, 256)
    gb = OFF_G // (2 * D)

    def body(dm_ref, g_ref, ya_ref, ys_ref, dg_ref, dya_ref, dys_ref):
        d = dm_ref[...].astype(F32)
        ga = _sigmoid(g_ref[:, :D].astype(F32))
        gs = _sigmoid(g_ref[:, D:].astype(F32))
        dya_ref[...] = (d * ga).astype(BF16)
        dys_ref[...] = (d * gs).astype(BF16)
        dg_ref[:, :D] = (d * ya_ref[...].astype(F32) * ga * (1.0 - ga)).astype(BF16)
        dg_ref[:, D:] = (d * ys_ref[...].astype(F32) * gs * (1.0 - gs)).astype(BF16)

    return R.call(
        body, "merge_bwd", 1, [R.tile(D), R.tile(2 * D, gb), R.tile(D), R.tile(D)],
        [R.tile(2 * D, gb), R.tile(D), R.tile(D)],
        [jax.ShapeDtypeStruct((T, ncols), BF16), jax.ShapeDtypeStruct((T, D), BF16),
         jax.ShapeDtypeStruct((T, D), BF16)],
        (dm, proj, ya, ys))


_FW = 256
_FB = FF // _FW


def _ffn_act_fwd(hv, conv_w, conv_b):
    T = hv.shape[0]
    R = _Rows(T, 512)
    tm = R.tm

    def body(h1_ref, h1p_ref, h3_ref, w_ref, b_ref, o_ref):
        keep = (pl.program_id(1) > 0).astype(F32)
        ext = jnp.concatenate([h1p_ref[...].astype(F32) * keep, h1_ref[...].astype(F32)], axis=0)
        pre = _wsum(w_ref[...], _shifts_causal(ext, 3, tm)) + b_ref[...]
        o_ref[...] = (pre * _sigmoid(pre) * h3_ref[...].astype(F32)).astype(BF16)

    return R.call(body, "ffn_act_fwd", _FB,
                  [R.tile(_FW), R.prev(_FW), R.tile(_FW, _FB), R.colvec(3, _FW), R.colvec(1, _FW)],
                  R.tile(_FW), jax.ShapeDtypeStruct((T, FF), BF16), (hv, hv, hv, conv_w, conv_b))


def _ffn_act_bwd(dg, hv, conv_w, conv_b):
    T = hv.shape[0]
    R = _Rows(T, 512)
    tm = R.tm

    def body(dg_ref, h1_ref, h1p_ref, h3_ref, w_ref, b_ref, dh3_ref, dpre_ref, dw_ref, db_ref):
        i = pl.program_id(1)
        keep = (i > 0).astype(F32)
        ext = jnp.concatenate([h1p_ref[...].astype(F32) * keep, h1_ref[...].astype(F32)], axis=0)
        sh = _shifts_causal(ext, 3, tm)
        pre = _wsum(w_ref[...], sh) + b_ref[...]
        s = _sigmoid(pre)
        d = dg_ref[...].astype(F32)
        dh3_ref[...] = (d * pre * s).astype(BF16)
        dpre = d * h3_ref[...].astype(F32) * s * (1.0 + pre * (1.0 - s))
        dpre_ref[...] = dpre.astype(BF16)
        _acc_rows(dw_ref, [_colsum(dpre * q) for q in sh], i == 0)
        _acc_out(db_ref, _colsum(dpre), i == 0)

    return R.call(
        body, "ffn_act_bwd", _FB,
        [R.tile(_FW), R.tile(_FW), R.prev(_FW), R.tile(_FW, _FB), R.colvec(3, _FW), R.colvec(1, _FW)],
        [R.tile(_FW), R.tile(_FW), R.colvec(3, _FW), R.colvec(1, _FW)],
        [jax.ShapeDtypeStruct((T, FF), BF16), jax.ShapeDtypeStruct((T, FF), BF16),
         jax.ShapeDtypeStruct((3, FF), F32), jax.ShapeDtypeStruct((1, FF), F32)],
        (dg, hv, hv, hv, conv_w, conv_b))


def _conv3_transpose(dpre, conv_w):
    T = dpre.shape[0]
    R = _Rows(T, 512)
    tm = R.tm

    def body(d_ref, dn_ref, w_ref, o_ref):
        keep = (pl.program_id(1) < R.nrow - 1).astype(F32)
        ext = jnp.concatenate([d_ref[...].astype(F32), dn_ref[...].astype(F32) * keep], axis=0)
        o_ref[...] = _wsum(w_ref[...], _shifts_anticausal(ext, 3, tm)).astype(BF16)

    return R.call(body, "ffn_conv_bwd", _FB, [R.tile(_FW), R.next(_FW), R.colvec(3, _FW)], R.tile(_FW),
                  jax.ShapeDtypeStruct((T, FF), BF16), (dpre, dpre, conv_w))


def _final_loss(h, w, target):
    T = h.shape[0]
    R = _Rows(T, 256)

    def body(h_ref, w_ref, t_ref, l_ref, dh_ref, dw_ref):
        first = pl.program_id(1) == 0
        xv = h_ref[...]
        wv = w_ref[...]
        r = lax.rsqrt(jnp.mean(xv * xv, axis=-1, keepdims=True) + EPS)
        xh = xv * r
        err = xh * wv - t_ref[...]
        part = 0.5 * jnp.sum(jnp.mean(err * err, axis=-1, keepdims=True), axis=0, keepdims=True)
        _acc_out(l_ref, jnp.broadcast_to(part, l_ref.shape), first)
        dy = err * (1.0 / D)
        dxh = dy * wv
        dh_ref[...] = r * (dxh - xh * jnp.mean(dxh * xh, axis=-1, keepdims=True))
        _acc_out(dw_ref, _colsum(dy * xh), first)

    return R.call(body, "final_loss", 1, [R.tile(D), R.colvec(1, D), R.tile(D)],
                  [R.colvec(8, LANES), R.tile(D), R.colvec(1, D)],
                  [jax.ShapeDtypeStruct((8, LANES), F32), jax.ShapeDtypeStruct((T, D), F32),
                   jax.ShapeDtypeStruct((1, D), F32)], (h, w, target))


def _pad_lanes(v, n=LANES):
    return jnp.pad(v, ((0, 0), (0, n - v.shape[1])))


def _group_cols(a):
    T = a.shape[0]
    return a[:, :NH].reshape(T, NG, _HG).transpose(1, 0, 2)


def _local_step(x, target, wts):
    T = x.shape[0]
    w_in, w_a_out, w_s_out, w_o, w_up, w_down = (wts[k] for k in ("w_in", "w_a_out", "w_s_out", "w_o", "w_up", "w_down"))
    dt_bias_p, a_log_p = _pad_lanes(wts["dt_bias"]), _pad_lanes(wts["a_log"])
    dsk_rep = jnp.repeat(wts["d_skip"], HP, axis=1)

    u = _rmsnorm_fwd(x, wts["norm_mix_w"], "norm_mix_fwd")
    proj = _matmul(u, w_in, mode="nn", out_dtype=BF16, name="mm_in")
    dt_raw = _matmul(u, w_in[:, OFF_DT:], mode="nn", out_dtype=F32, name="mm_dt")
    ya_in = _branch_a_fwd(proj, wts["conv_a_w"])
    y_a = _matmul(ya_in, w_a_out, mode="nn", out_dtype=BF16, name="mm_a_out")
    xact = _xbc_fwd(proj, wts["ssd_conv_w"], wts["ssd_conv_b"])
    dt, acum = _dt_fwd(dt_raw, dt_bias_p, a_log_p)
    dtg, acg = _group_cols(dt), _group_cols(acum)
    acgT = acg.transpose(0, 2, 1)
    y_ssd, sprev = _ssd_fwd(xact, dtg, acg, acgT, dsk_rep)
    yn = _gnorm_fwd(y_ssd, proj, wts["ssd_norm_w"])
    y_s = _matmul(yn, w_s_out, mode="nn", out_dtype=BF16, name="mm_s_out")
    merged = _merge_fwd(proj, y_a, y_s)
    h1 = _matmul(merged, w_o, mode="nn", out_dtype=F32, name="mm_o", residual=x)
    v = _rmsnorm_fwd(h1, wts["norm_ffn_w"], "norm_ffn_fwd")
    hv = _matmul(v, w_up, mode="nn", out_dtype=BF16, name="mm_up")
    gact = _ffn_act_fwd(hv, wts["ffn_conv_w"], wts["ffn_conv_b"])
    h2 = _matmul(gact, w_down, mode="nn", out_dtype=F32, name="mm_down", residual=h1)
    loss, dh2, g_final = _final_loss(h2, wts["final_norm_w"], target)

    grads = {"final_norm_w": g_final}
    dh2b = dh2.astype(BF16)
    grads["w_down"] = _matmul(gact, dh2b, mode="tn", out_dtype=F32, name="mm_down_dw")
    dgact = _matmul(dh2b, w_down, mode="nt", out_dtype=BF16, name="mm_down_dx")
    dh3, dpre, grads["ffn_conv_w"], grads["ffn_conv_b"] = _ffn_act_bwd(dgact, hv, wts["ffn_conv_w"], wts["ffn_conv_b"])
    dh1c = _conv3_transpose(dpre, wts["ffn_conv_w"])
    grads["w_up"] = jnp.concatenate(
        [_matmul(v, dh1c, mode="tn", out_dtype=F32, name="mm_up_dw1"),
         _matmul(v, dh3, mode="tn", out_dtype=F32, name="mm_up_dw3")], axis=1)
    dv = _matmul(dh1c, w_up[:, :FF], mode="nt", out_dtype=F32, name="mm_up_dx1")
    dv = _matmul(dh3, w_up[:, FF:], mode="nt", out_dtype=F32, name="mm_up_dx3", residual=dv)
    dh1, grads["norm_ffn_w"] = _rmsnorm_bwd(dv, h1, wts["norm_ffn_w"], dh2, "norm_ffn_bwd")
    dh1b = dh1.astype(BF16)
    grads["w_o"] = _matmul(merged, dh1b, mode="tn", out_dtype=F32, name="mm_o_dw")
    dmerged = _matmul(dh1b, w_o, mode="nt", out_dtype=BF16, name="mm_o_dx")
    dproj, dya, dys = _merge_bwd(dmerged, proj, y_a, y_s, NIP)
    grads["w_a_out"] = _matmul(ya_in, dya, mode="tn", out_dtype=F32, name="mm_a_out_dw")
    dya_in = _matmul(dya, w_a_out, mode="nt", out_dtype=BF16, name="mm_a_out_dx")
    dproj, grads["conv_a_w"] = _branch_a_bwd(dya_in, proj, wts["conv_a_w"], dproj)
    grads["w_s_out"] = _matmul(yn, dys, mode="tn", out_dtype=F32, name="mm_s_out_dw")
    dyn = _matmul(dys, w_s_out, mode="nt", out_dtype=BF16, name="mm_s_out_dx")
    dproj, dy_ssd, grads["ssd_norm_w"] = _gnorm_bwd(dyn, y_ssd, proj, wts["ssd_norm_w"], dproj)
    dxs, dB, dC, ddtx, dAc, dAr, dskl = _ssd_bwd(dy_ssd, xact, dtg, acg, acgT, dsk_rep, sprev)
    grads["d_skip"] = dskl.reshape(NH, HP).sum(axis=1).reshape(1, NH)
    dxact = jnp.concatenate([dxs, dB, dC], axis=1)
    dproj, grads["ssd_conv_w"], grads["ssd_conv_b"] = _xbc_bwd(dxact, proj, wts["ssd_conv_w"], wts["ssd_conv_b"], dproj)
    dacum = (dAc + dAr.transpose(0, 2, 1)).transpose(1, 0, 2).reshape(T, NH)
    ddt_x = ddtx.transpose(1, 0, 2).reshape(T, NH)
    dproj, g_dtb, g_alog = _dt_bwd(_pad_lanes(dacum), _pad_lanes(ddt_x), dt_raw, dt_bias_p, a_log_p, dproj)
    grads["dt_bias"], grads["a_log"] = g_dtb[:, :NH], g_alog[:, :NH]
    grads["w_in"] = _matmul(u, dproj, mode="tn", out_dtype=F32, name="mm_in_dw")
    du = _matmul(dproj, w_in, mode="nt", out_dtype=F32, name="mm_in_dx")
    grad_x, grads["norm_mix_w"] = _rmsnorm_bwd(du, x, wts["norm_mix_w"], dh1, "norm_mix_bwd")
    return loss, grad_x, grads


def _permute_w_in(w):
    out = jnp.zeros((w.shape[0], NIP), w.dtype)
    for o, n, no in _SEGS:
        out = lax.dynamic_update_slice(out, w[:, o:o + n], (0, no))
    return out


def _unpermute_w_in(g):
    order = sorted(_SEGS)
    return jnp.concatenate([g[:, no:no + n] for o, n, no in order], axis=1)


MESH = pl.DeviceIdType.MESH
NCHIP = 4
NDEV = 8

_BIG = (("w_in", D, NI // NCHIP, 1), ("w_a_out", D // NCHIP, D, 0), ("w_s_out", DI // NCHIP, D, 0),
        ("w_o", D // NCHIP, D, 0), ("w_up", D, 2 * FF // NCHIP, 1), ("w_down", FF // NCHIP, D, 0))
_BIG_ROWS = tuple(r * c // LANES for _, r, c, _ in _BIG)
PACK_ROWS = 46080
HALF_ROWS = PACK_ROWS // 2
assert sum(_BIG_ROWS) <= PACK_ROWS and all(r % 16 == 0 for r in _BIG_ROWS)


def _coords():
    return lax.axis_index("x"), lax.axis_index("y"), lax.axis_index("c")


def _other_chips(x, y):
    return [(1 - x, y), (x, 1 - y), (1 - x, 1 - y)]


def _ag_weights(shard):
    hr = HALF_ROWS

    def body(x_ref, out_ref, send_sems, recv_sems, local_sem):
        x, y, c = _coords()
        me = 2 * x + y
        chips = _other_chips(x, y)

        def rows(s, h):
            return out_ref.at[s, pl.ds(h * hr, hr), :]

        def copy(k, s, h, to, src=None):
            return pltpu.make_async_remote_copy(
                src_ref=rows(s, h) if src is None else src, dst_ref=rows(s, h),
                send_sem=send_sems.at[k], recv_sem=recv_sems.at[k], device_id=to, device_id_type=MESH)

        mine = pltpu.make_async_copy(x_ref, out_ref.at[me], local_sem)
        mine.start()
        first = [copy(k, me, c, (*chip, c), src=x_ref.at[pl.ds(c * hr, hr), :]) for k, chip in enumerate(chips)]
        for cp in first:
            cp.start()
        passed = []
        for k, chip in enumerate(chips):
            s = 2 * chip[0] + chip[1]
            copy(k, s, c, (x, y, c)).wait_recv()
            fwd = copy(3 + k, s, c, (x, y, 1 - c))
            fwd.start()
            passed.append(fwd)
        for k, chip in enumerate(chips):
            copy(3 + k, 2 * chip[0] + chip[1], 1 - c, (x, y, c)).wait_recv()
        for cp in first + passed:
            cp.wait_send()
        mine.wait()

    return pl.pallas_call(
        body, name="ag_weights", in_specs=[ANY], out_specs=ANY,
        out_shape=jax.ShapeDtypeStruct((NCHIP, PACK_ROWS, LANES), shard.dtype),
        scratch_shapes=[pltpu.SemaphoreType.DMA((6,)), pltpu.SemaphoreType.DMA((6,)), pltpu.SemaphoreType.DMA],
        compiler_params=pltpu.CompilerParams(has_side_effects=True),
    )(shard)


def _rs_sibling(g):
    hr = HALF_ROWS

    def body(g_ref, land_ref, send_sem, recv_sem):
        x, y, c = _coords()
        cp = pltpu.make_async_remote_copy(
            src_ref=g_ref.at[:, pl.ds((1 - c) * hr, hr), :], dst_ref=land_ref,
            send_sem=send_sem, recv_sem=recv_sem, device_id=(x, y, 1 - c), device_id_type=MESH)
        cp.start()
        cp.wait()

    return pl.pallas_call(
        body, name="rs_sibling", in_specs=[ANY], out_specs=ANY,
        out_shape=jax.ShapeDtypeStruct((NCHIP, hr, LANES), g.dtype),
        scratch_shapes=[pltpu.SemaphoreType.DMA, pltpu.SemaphoreType.DMA],
        compiler_params=pltpu.CompilerParams(has_side_effects=True),
    )(g)


def _rs_chips(h):
    def body(h_ref, land_ref, send_sems, recv_sems):
        x, y, c = _coords()
        chips = _other_chips(x, y)
        cps = [pltpu.make_async_remote_copy(
            src_ref=h_ref.at[2 * chip[0] + chip[1]], dst_ref=land_ref.at[k],
            send_sem=send_sems.at[k], recv_sem=recv_sems.at[k], device_id=(*chip, c), device_id_type=MESH)
            for k, chip in enumerate(chips)]
        for cp in cps:
            cp.start()
        for cp in cps:
            cp.wait()

    return pl.pallas_call(
        body, name="rs_chips", in_specs=[ANY], out_specs=ANY,
        out_shape=jax.ShapeDtypeStruct((NCHIP - 1, HALF_ROWS, LANES), h.dtype),
        scratch_shapes=[pltpu.SemaphoreType.DMA((3,)), pltpu.SemaphoreType.DMA((3,))],
        compiler_params=pltpu.CompilerParams(has_side_effects=True),
    )(h)


def _bcast_sibling(f):
    hr = HALF_ROWS

    def body(f_ref, out_ref, send_sem, recv_sem, local_sem):
        x, y, c = _coords()
        mine = pltpu.make_async_copy(f_ref, out_ref.at[pl.ds(c * hr, hr), :], local_sem)
        mine.start()
        cp = pltpu.make_async_remote_copy(
            src_ref=f_ref, dst_ref=out_ref.at[pl.ds(c * hr, hr), :],
            send_sem=send_sem, recv_sem=recv_sem, device_id=(x, y, 1 - c), device_id_type=MESH)
        cp.start()
        landed = pltpu.make_async_remote_copy(
            src_ref=f_ref, dst_ref=out_ref.at[pl.ds((1 - c) * hr, hr), :],
            send_sem=send_sem, recv_sem=recv_sem, device_id=(x, y, 1 - c), device_id_type=MESH)
        landed.wait_recv()
        cp.wait_send()
        mine.wait()

    return pl.pallas_call(
        body, name="bcast_sibling", in_specs=[ANY], out_specs=ANY,
        out_shape=jax.ShapeDtypeStruct((PACK_ROWS, LANES), f.dtype),
        scratch_shapes=[pltpu.SemaphoreType.DMA, pltpu.SemaphoreType.DMA, pltpu.SemaphoreType.DMA],
        compiler_params=pltpu.CompilerParams(has_side_effects=True),
    )(f)


_ADD_ROWS = 2560


def _add_halves(g, land, c):
    nb = HALF_ROWS // _ADD_ROWS

    def body(c_ref, g_ref, l_ref, o_ref):
        o_ref[...] = g_ref[...] + l_ref[...]

    blk = (1, _ADD_ROWS, LANES)
    return pl.pallas_call(
        body, name="rs_add_halves",
        grid_spec=pltpu.PrefetchScalarGridSpec(
            num_scalar_prefetch=1, grid=(NCHIP, nb),
            in_specs=[pl.BlockSpec(blk, lambda s, i, c_ref: (s, c_ref[0] * nb + i, 0)),
                      pl.BlockSpec(blk, lambda s, i, c_ref: (s, i, 0))],
            out_specs=pl.BlockSpec(blk, lambda s, i, c_ref: (s, i, 0))),
        out_shape=jax.ShapeDtypeStruct((NCHIP, HALF_ROWS, LANES), F32),
        compiler_params=_params(("parallel", "parallel")),
    )(c, g, land)


def _add_chips(h, land, me):
    nb = HALF_ROWS // _ADD_ROWS

    def body(me_ref, h_ref, l_ref, o_ref):
        o_ref[...] = ((h_ref[0] + l_ref[0]) + l_ref[1]) + l_ref[2]

    return pl.pallas_call(
        body, name="rs_add_chips",
        grid_spec=pltpu.PrefetchScalarGridSpec(
            num_scalar_prefetch=1, grid=(nb,),
            in_specs=[pl.BlockSpec((1, _ADD_ROWS, LANES), lambda i, me_ref: (me_ref[0], i, 0)),
                      pl.BlockSpec((NCHIP - 1, _ADD_ROWS, LANES), lambda i, me_ref: (0, i, 0))],
            out_specs=pl.BlockSpec((_ADD_ROWS, LANES), lambda i, me_ref: (i, 0))),
        out_shape=jax.ShapeDtypeStruct((HALF_ROWS, LANES), F32),
        compiler_params=_params(("parallel",)),
    )(me, h, land)


def _gather8(v, reduce, name):
    rows = v.shape[0]

    def body(v_ref, o_ref, buf, send_sems, recv_sems):
        x, y, c = _coords()
        me = 4 * x + 2 * y + c
        buf[pl.ds(me, 1)] = v_ref[...][None]
        cps, lands = [], []
        for k in range(1, NDEV):
            peer = (1 - x if k & 4 else x, 1 - y if k & 2 else y, 1 - c if k & 1 else c)

            def copy(slot):
                return pltpu.make_async_remote_copy(
                    src_ref=v_ref, dst_ref=buf.at[slot], send_sem=send_sems.at[k - 1],
                    recv_sem=recv_sems.at[k - 1], device_id=peer, device_id_type=MESH)

            cps.append(copy(me))
            lands.append(copy(4 * peer[0] + 2 * peer[1] + peer[2]))
        for cp in cps:
            cp.start()
        for cp, land in zip(cps, lands):
            land.wait_recv()
            cp.wait_send()
        if reduce:
            acc = buf[0]
            for d in range(1, NDEV):
                acc = acc + buf[d]
            o_ref[...] = acc
        else:
            o_ref[...] = buf[...]

    vm = pl.BlockSpec(memory_space=pltpu.VMEM)
    out_shape = (rows, LANES) if reduce else (NDEV, rows, LANES)
    return pl.pallas_call(
        body, name=name, in_specs=[vm], out_specs=vm, out_shape=jax.ShapeDtypeStruct(out_shape, F32),
        scratch_shapes=[pltpu.VMEM((NDEV, rows, LANES), F32), pltpu.SemaphoreType.DMA((NDEV - 1,)),
                        pltpu.SemaphoreType.DMA((NDEV - 1,))],
        compiler_params=pltpu.CompilerParams(has_side_effects=True),
    )(v)


def _adamw(w, g, m, v, name):
    rows, cols = w.shape
    tr = rows
    while tr * cols * 4 > (3 << 19) and tr % 16 == 0:
        tr //= 2
    c1 = 1.0 / (1.0 - ADAM_B1 ** ADAM_STEP)
    c2 = 1.0 / (1.0 - ADAM_B2 ** ADAM_STEP)

    def body(w_ref, g_ref, m_ref, v_ref, d_ref, mo_ref, vo_ref):
        gv = g_ref[...]
        mn = ADAM_B1 * m_ref[...] + (1.0 - ADAM_B1) * gv
        vn = ADAM_B2 * v_ref[...] + (1.0 - ADAM_B2) * (gv * gv)
        d_ref[...] = -ADAM_LR * ((mn * c1) / (jnp.sqrt(vn * c2) + ADAM_EPS) + ADAM_WD * w_ref[...])
        mo_ref[...] = mn
        vo_ref[...] = vn

    blk = pl.BlockSpec((tr, cols), lambda i: (i, 0))
    return pl.pallas_call(
        body, name=name, grid=(rows // tr,), in_specs=[blk] * 4, out_specs=[blk] * 3,
        out_shape=[jax.ShapeDtypeStruct((rows, cols), F32)] * 3, compiler_params=_params(("parallel",)),
    )(w, g, m, v)


def _rows128(a, mult=8):
    flat = a.reshape(-1)
    n = -(-flat.shape[0] // (LANES * mult)) * LANES * mult
    return jnp.pad(flat, (0, n - flat.shape[0])).reshape(-1, LANES)


def _pack_rows(parts, total_rows):
    rows = sum(p.shape[0] for p in parts)
    if total_rows > rows:
        parts = list(parts) + [jnp.zeros((total_rows - rows, LANES), parts[0].dtype)]
    return jnp.concatenate(parts, axis=0)


def _unpack_rows(pack, shapes, mult=8):
    out, r = [], 0
    for shp in shapes:
        n = int(np.prod(shp))
        nr = -(-n // (LANES * mult)) * mult
        out.append(pack[r:r + nr].reshape(-1)[:n].reshape(shp))
        r += nr
    return out


def _big_unpack_full(full):
    out, r = {}, 0
    for (name, rr, cc, axis), nr in zip(_BIG, _BIG_ROWS):
        seg = full[:, r:r + nr].reshape(NCHIP, rr, cc)
        out[name] = seg.reshape(NCHIP * rr, cc) if axis == 0 else seg.transpose(1, 0, 2).reshape(rr, NCHIP * cc)
        r += nr
    return out


def _big_pack_by_chip(grads):
    parts = []
    for (name, rr, cc, axis), nr in zip(_BIG, _BIG_ROWS):
        g = grads[name]
        seg = g.reshape(NCHIP, rr, cc) if axis == 0 else g.reshape(rr, NCHIP, cc).transpose(1, 0, 2)
        parts.append(seg.reshape(NCHIP, nr, LANES))
    parts.append(jnp.zeros((NCHIP, PACK_ROWS - sum(_BIG_ROWS), LANES), F32))
    return jnp.concatenate(parts, axis=1)


_SMALL_REPL = ("norm_mix_w", "ssd_conv_b", "dt_bias", "a_log", "d_skip", "ssd_norm_w", "norm_ffn_w",
               "ffn_conv_b", "final_norm_w")
_SMALL_CONV = (("conv_a_w", 3, D), ("ssd_conv_w", 4, DX), ("ffn_conv_w", 3, FF))


def kernel(x, norm_mix_w, w_in, conv_a_w, w_a_out, ssd_conv_w, ssd_conv_b, dt_bias, a_log, d_skip, ssd_norm_w, w_s_out, w_o, norm_ffn_w, w_up, ffn_conv_w, ffn_conv_b, w_down, final_norm_w, loss_target, m_norm_mix_w, m_w_in, m_conv_a_w, m_w_a_out, m_ssd_conv_w, m_ssd_conv_b, m_dt_bias, m_a_log, m_d_skip, m_ssd_norm_w, m_w_s_out, m_w_o, m_norm_ffn_w, m_w_up, m_ffn_conv_w, m_ffn_conv_b, m_w_down, m_final_norm_w, v_norm_mix_w, v_w_in, v_conv_a_w, v_w_a_out, v_ssd_conv_w, v_ssd_conv_b, v_dt_bias, v_a_log, v_d_skip, v_ssd_norm_w, v_w_s_out, v_w_o, v_norm_ffn_w, v_w_up, v_ffn_conv_w, v_ffn_conv_b, v_w_down, v_final_norm_w):
    names = ("norm_mix_w", "w_in", "conv_a_w", "w_a_out", "ssd_conv_w", "ssd_conv_b", "dt_bias", "a_log", "d_skip",
             "ssd_norm_w", "w_s_out", "w_o", "norm_ffn_w", "w_up", "ffn_conv_w", "ffn_conv_b", "w_down", "final_norm_w")
    W = dict(zip(names, (norm_mix_w, w_in, conv_a_w, w_a_out, ssd_conv_w, ssd_conv_b, dt_bias, a_log, d_skip,
                         ssd_norm_w, w_s_out, w_o, norm_ffn_w, w_up, ffn_conv_w, ffn_conv_b, w_down, final_norm_w)))
    M = dict(zip(names, (m_norm_mix_w, m_w_in, m_conv_a_w, m_w_a_out, m_ssd_conv_w, m_ssd_conv_b, m_dt_bias, m_a_log,
                         m_d_skip, m_ssd_norm_w, m_w_s_out, m_w_o, m_norm_ffn_w, m_w_up, m_ffn_conv_w, m_ffn_conv_b,
                         m_w_down, m_final_norm_w)))
    V = dict(zip(names, (v_norm_mix_w, v_w_in, v_conv_a_w, v_w_a_out, v_ssd_conv_w, v_ssd_conv_b, v_dt_bias, v_a_log,
                         v_d_skip, v_ssd_norm_w, v_w_s_out, v_w_o, v_norm_ffn_w, v_w_up, v_ffn_conv_w, v_ffn_conv_b,
                         v_w_down, v_final_norm_w)))
    two_d = lambda a: a.reshape(-1, a.shape[-1])
    W2, M2, V2 = ({k: two_d(a) for k, a in t.items()} for t in (W, M, V))
    xi, yi, ci = _coords()
    me = 2 * xi + yi

    shard = _pack_rows([_rows128(W2[n], 16) for n, *_ in _BIG], PACK_ROWS).astype(BF16)
    full = _big_unpack_full(_ag_weights(shard))
    conv_shards = _pack_rows([_rows128(W2[n]) for n, *_ in _SMALL_CONV], 0)
    conv_all = _gather8(conv_shards, False, "ag_conv_weights")[0::2]
    wts = {k: W2[k] for k in _SMALL_REPL}
    r = 0
    for n, kk, width in _SMALL_CONV:
        cw = width // NCHIP
        nr = -(-kk * cw // (LANES * 8)) * 8
        wts[n] = conv_all[:, r:r + nr].reshape(NCHIP, -1)[:, :kk * cw].reshape(NCHIP, kk, cw).transpose(1, 0, 2).reshape(kk, width)
        r += nr
    wts["w_in"] = _permute_w_in(full["w_in"])
    for n in ("w_a_out", "w_s_out", "w_o", "w_up", "w_down"):
        wts[n] = full[n]

    loss8, grad_x, grads = _local_step(x[0], loss_target[0], wts)
    grads["w_in"] = _unpermute_w_in(grads["w_in"])

    cidx = ci.reshape(1).astype(jnp.int32)
    meidx = me.reshape(1).astype(jnp.int32)
    gpack = _big_pack_by_chip(grads)
    half = _add_halves(gpack, _rs_sibling(gpack), cidx)
    mine = _add_chips(half, _rs_chips(half), meidx)
    gshard = _bcast_sibling(mine)
    big_shapes = [(rr, cc) for _, rr, cc, _ in _BIG]
    gbig = dict(zip([n for n, *_ in _BIG], _unpack_rows(gshard, big_shapes, 16)))

    small_shapes = [W2[n].shape for n in _SMALL_REPL] + [(1, LANES)] + [(kk, width) for _, kk, width in _SMALL_CONV]
    small_parts = [grads[n] for n in _SMALL_REPL] + [loss8[0:1]] + [grads[n] for n, *_ in _SMALL_CONV]
    small = _gather8(_pack_rows([_rows128(p) for p in small_parts], 0), True, "allreduce_small")
    small_g = _unpack_rows(small, small_shapes)
    gsm = dict(zip(_SMALL_REPL, small_g[:len(_SMALL_REPL)]))
    loss = small_g[len(_SMALL_REPL)][0, 0]
    for (n, kk, width), gfull in zip(_SMALL_CONV, small_g[len(_SMALL_REPL) + 1:]):
        cw = width // NCHIP
        gsm[n] = lax.dynamic_slice(gfull, (0, me * cw), (kk, cw))

    G, DW, NM, NV = {}, {}, {}, {}
    for n in [b[0] for b in _BIG]:
        G[n] = gbig[n]
        DW[n], NM[n], NV[n] = _adamw(W2[n], G[n], M2[n], V2[n], "adamw_" + n)
    sm_names = list(_SMALL_REPL) + [n for n, *_ in _SMALL_CONV]
    sm_shapes = [W2[n].shape for n in sm_names]
    packs = [_pack_rows([_rows128(t[n]) for n in sm_names], 0) for t in (W2, gsm, M2, V2)]
    outs = _adamw(*packs, "adamw_small")
    for t, pk in zip((DW, NM, NV), outs):
        t.update(dict(zip(sm_names, _unpack_rows(pk, sm_shapes))))
    G.update(gsm)

    def shaped(t):
        return [t[n].reshape(W[n].shape) for n in names]

    return (loss, grad_x.reshape(x.shape), *shaped(G), *shaped(DW), *shaped(NM), *shaped(NV))
```

```python
import functools

import jax
import jax.numpy as jnp
import numpy as np
from jax import lax
from jax.experimental import pallas as pl
from jax.experimental.pallas import tpu as pltpu

F32 = jnp.float32
BF16 = jnp.bfloat16

D = 1024
DI = 2048
NH = 32
HP = 64
NG = 4
NS = 128
CH = 128
DX = 3072
FF = 2816
NI = 10272
EPS = 1e-5

OFF_BCV, OFF_XBC, OFF_G, OFF_Z, OFF_DT = 0, 3072, 6144, 8192, 10240
NIP = 10368
_SEGS = ((0, 2048, OFF_G), (2048, 3072, OFF_BCV), (5120, 2048, OFF_Z), (7168, 3072, OFF_XBC), (10240, 32, OFF_DT))

LANES = 128
HALO = 16
V7X_VMEM_LIMIT = 56 * 2 ** 20

ADAM_LR, ADAM_B1, ADAM_B2, ADAM_EPS, ADAM_WD, ADAM_STEP = 0.001, 0.9, 0.999, 1e-08, 0.01, 10

NN = (((1,), (0,)), ((), ()))
NT = (((1,), (1,)), ((), ()))
TN = (((0,), (0,)), ((), ()))


def _dot(a, b, dims=NN):
    return lax.dot_general(a, b, dims, preferred_element_type=F32)


def _params(sem, **kw):
    return pltpu.CompilerParams(dimension_semantics=sem, vmem_limit_bytes=V7X_VMEM_LIMIT, **kw)


def _pick(dim, cap):
    if dim <= cap:
        return dim
    best = None
    for t in range(LANES, cap + 1, LANES):
        if dim % t == 0:
            best = t
    assert best is not None, (dim, cap)
    return best


def _sigmoid(x):
    return 1.0 / (1.0 + jnp.exp(-x))


def _matmul(a, b, *, mode, out_dtype, name, residual=None, caps=(1408, 1408, 1408)):
    if mode == "nn":
        (M, K), (K2, N) = a.shape, b.shape
    elif mode == "nt":
        (M, K), (N, K2) = a.shape, b.shape
    else:
        (K, M), (K2, N) = a.shape, b.shape
    assert K == K2, (name, a.shape, b.shape)
    tm, tn, tk = _pick(M, caps[0]), _pick(N, caps[1]), _pick(K, caps[2])
    nk = K // tk
    if mode == "tn":
        a_spec = pl.BlockSpec((tk, tm), lambda i, j, k: (k, i))
    else:
        a_spec = pl.BlockSpec((tm, tk), lambda i, j, k: (i, k))
    if mode == "nt":
        b_spec = pl.BlockSpec((tn, tk), lambda i, j, k: (j, k))
    else:
        b_spec = pl.BlockSpec((tk, tn), lambda i, j, k: (k, j))
    dims = {"nn": NN, "nt": NT, "tn": TN}[mode]
    o_spec = pl.BlockSpec((tm, tn), lambda i, j, k: (i, j))
    has_res = residual is not None

    def body(*refs):
        if has_res:
            a_ref, b_ref, r_ref, o_ref, acc_ref = refs
        else:
            a_ref, b_ref, o_ref, acc_ref = refs
        k = pl.program_id(2)

        @pl.when(k == 0)
        def _():
            acc_ref[...] = jnp.zeros_like(acc_ref)

        acc_ref[...] += _dot(a_ref[...], b_ref[...], dims)

        @pl.when(k == nk - 1)
        def _():
            r = acc_ref[...]
            if has_res:
                r = r + r_ref[...].astype(F32)
            o_ref[...] = r.astype(out_dtype)

    in_specs = [a_spec, b_spec] + ([o_spec] if has_res else [])
    args = (a, b) + ((residual,) if has_res else ())
    return pl.pallas_call(
        body, name=name, grid=(M // tm, N // tn, nk), in_specs=in_specs, out_specs=o_spec,
        out_shape=jax.ShapeDtypeStruct((M, N), out_dtype),
        scratch_shapes=[pltpu.VMEM((tm, tn), F32)],
        compiler_params=_params(("parallel", "parallel", "arbitrary")),
    )(*args)


class _Rows:
    def __init__(self, T, tm):
        self.T, self.tm = T, min(tm, T // 2)
        self.nrow = T // self.tm
        self.r = self.tm // HALO
        self.nb = T // HALO

    def tile(self, w, cb=0, step=1):
        return pl.BlockSpec((self.tm, w), lambda j, i: (i, cb + step * j))

    def prev(self, w, cb=0, step=1):
        r = self.r
        return pl.BlockSpec((HALO, w), lambda j, i: (jnp.maximum(i * r - 1, 0), cb + step * j))

    def next(self, w, cb=0, step=1):
        r, nb = self.r, self.nb
        return pl.BlockSpec((HALO, w), lambda j, i: (jnp.minimum((i + 1) * r, nb - 1), cb + step * j))

    def colvec(self, k, w, cb=0, step=1):
        return pl.BlockSpec((k, w), lambda j, i: (0, cb + step * j))

    def call(self, body, name, ncol, in_specs, out_specs, out_shape, args, aliases=None):
        return pl.pallas_call(
            body, name=name, grid=(ncol, self.nrow), in_specs=in_specs, out_specs=out_specs,
            out_shape=out_shape, input_output_aliases=aliases or {},
            compiler_params=_params(("parallel", "arbitrary")),
        )(*args)


ANY = pl.BlockSpec(memory_space=pl.ANY)


def _shifts_causal(ext, nk, tm):
    out = []
    for k in range(nk):
        s = nk - 1 - k
        r = ext if s == 0 else pltpu.roll(ext, s, 0)
        out.append(r[HALO:])
    return out


def _shifts_anticausal(ext, nk, tm):
    n = ext.shape[0]
    out = []
    for k in range(nk):
        s = nk - 1 - k
        r = ext if s == 0 else pltpu.roll(ext, n - s, 0)
        out.append(r[:tm])
    return out


def _wsum(w, parts):
    acc = w[0:1, :] * parts[0]
    for k in range(1, len(parts)):
        acc = acc + w[k:k + 1, :] * parts[k]
    return acc


def _colsum(x):
    return jnp.sum(x, axis=0, keepdims=True)


def _acc_out(ref, val, first):
    @pl.when(first)
    def _():
        ref[...] = val

    @pl.when(jnp.logical_not(first))
    def _():
        ref[...] += val


def _acc_rows(ref, rows, first):
    for k, r in enumerate(rows):
        _acc_out(ref.at[k:k + 1, :], r, first)


def _rmsnorm_fwd(x, w, name):
    T = x.shape[0]
    R = _Rows(T, 256)

    def body(x_ref, w_ref, o_ref):
        xv = x_ref[...]
        r = lax.rsqrt(jnp.mean(xv * xv, axis=-1, keepdims=True) + EPS)
        o_ref[...] = (xv * r * w_ref[...]).astype(BF16)

    return R.call(body, name, 1, [R.tile(D), R.colvec(1, D)], R.tile(D),
                  jax.ShapeDtypeStruct((T, D), BF16), (x, w))


def _rmsnorm_bwd(dy, x, w, dres, name):
    T = x.shape[0]
    R = _Rows(T, 256)

    def body(dy_ref, x_ref, w_ref, dr_ref, dx_ref, dw_ref):
        xv = x_ref[...]
        r = lax.rsqrt(jnp.mean(xv * xv, axis=-1, keepdims=True) + EPS)
        xh = xv * r
        dyv = dy_ref[...].astype(F32)
        dxh = dyv * w_ref[...]
        dx = r * (dxh - xh * jnp.mean(dxh * xh, axis=-1, keepdims=True))
        dx_ref[...] = dx + dr_ref[...]
        _acc_out(dw_ref, _colsum(dyv * xh), pl.program_id(1) == 0)

    return R.call(body, name, 1, [R.tile(D), R.tile(D), R.colvec(1, D), R.tile(D)],
                  [R.tile(D), R.colvec(1, D)],
                  [jax.ShapeDtypeStruct((T, D), F32), jax.ShapeDtypeStruct((1, D), F32)],
                  (dy, x, w, dres))


def _branch_a_fwd(proj, conv_w):
    T = proj.shape[0]
    R = _Rows(T, 256)
    tm = R.tm

    def body(p_ref, pp_ref, w_ref, o_ref):
        keep = (pl.program_id(1) > 0).astype(F32)
        cv = p_ref[:, D:2 * D].astype(F32) * p_ref[:, 2 * D:].astype(F32)
        cvp = pp_ref[:, D:2 * D].astype(F32) * pp_ref[:, 2 * D:].astype(F32) * keep
        sh = _shifts_causal(jnp.concatenate([cvp, cv], axis=0), 3, tm)
        ca = _wsum(w_ref[...], sh)
        o_ref[...] = (p_ref[:, :D].astype(F32) * ca).astype(BF16)

    return R.call(body, "branch_a_fwd", 1, [R.tile(3 * D), R.prev(3 * D), R.colvec(3, D)], R.tile(D),
                  jax.ShapeDtypeStruct((T, D), BF16), (proj, proj, conv_w))


def _branch_a_bwd(dya_in, proj, conv_w, dproj):
    T = proj.shape[0]
    R = _Rows(T, 256)
    tm = R.tm

    def body(d_ref, dn_ref, p_ref, pp_ref, pn_ref, w_ref, _alias, o_ref, dw_ref):
        i = pl.program_id(1)
        keep_p = (i > 0).astype(F32)
        keep_n = (i < R.nrow - 1).astype(F32)
        w = w_ref[...]
        b = p_ref[:, :D].astype(F32)
        c = p_ref[:, D:2 * D].astype(F32)
        v = p_ref[:, 2 * D:].astype(F32)
        cvp = pp_ref[:, D:2 * D].astype(F32) * pp_ref[:, 2 * D:].astype(F32) * keep_p
        sh = _shifts_causal(jnp.concatenate([cvp, c * v], axis=0), 3, tm)
        ca = _wsum(w, sh)
        d = d_ref[...].astype(F32)
        dca = d * b
        dca_n = dn_ref[...].astype(F32) * pn_ref[:, :D].astype(F32) * keep_n
        dsh = _shifts_anticausal(jnp.concatenate([dca, dca_n], axis=0), 3, tm)
        dcv = _wsum(w, dsh)
        o_ref[:, :D] = (d * ca).astype(BF16)
        o_ref[:, D:2 * D] = (dcv * v).astype(BF16)
        o_ref[:, 2 * D:] = (dcv * c).astype(BF16)
        _acc_rows(dw_ref, [_colsum(dca * s) for s in sh], i == 0)

    return R.call(
        body, "branch_a_bwd", 1,
        [R.tile(D), R.next(D), R.tile(3 * D), R.prev(3 * D), R.next(3 * D), R.colvec(3, D), ANY],
        [R.tile(3 * D), R.colvec(3, D)],
        [jax.ShapeDtypeStruct(dproj.shape, BF16), jax.ShapeDtypeStruct((3, D), F32)],
        (dya_in, dya_in, proj, proj, proj, conv_w, dproj), aliases={6: 0})


_XW = 512


def _xbc_fwd(proj, conv_w, conv_b):
    T = proj.shape[0]
    R = _Rows(T, 512)
    tm = R.tm
    cb = OFF_XBC // _XW

    def body(x_ref, xp_ref, w_ref, b_ref, o_ref):
        keep = (pl.program_id(1) > 0).astype(F32)
        ext = jnp.concatenate([xp_ref[...].astype(F32) * keep, x_ref[...].astype(F32)], axis=0)
        pre = _wsum(w_ref[...], _shifts_causal(ext, 4, tm)) + b_ref[...]
        o_ref[...] = (pre * _sigmoid(pre)).astype(BF16)

    return R.call(body, "xbc_fwd", DX // _XW,
                  [R.tile(_XW, cb), R.prev(_XW, cb), R.colvec(4, _XW), R.colvec(1, _XW)], R.tile(_XW),
                  jax.ShapeDtypeStruct((T, DX), BF16), (proj, proj, conv_w, conv_b))


def _xbc_bwd(dact, proj, conv_w, conv_b, dproj):
    T = proj.shape[0]
    R = _Rows(T, 512)
    tm = R.tm
    cb = OFF_XBC // _XW

    def body(d_ref, dn_ref, x_ref, xp_ref, xn_ref, w_ref, b_ref, _alias, o_ref, dw_ref, db_ref):
        i = pl.program_id(1)
        keep_p = (i > 0).astype(F32)
        keep_n = (i < R.nrow - 1).astype(F32)
        w = w_ref[...]
        ext = jnp.concatenate([xp_ref[...].astype(F32) * keep_p, x_ref[...].astype(F32),
                               xn_ref[...].astype(F32)], axis=0)
        sh = _shifts_causal(ext, 4, tm + HALO)
        pre = _wsum(w, sh) + b_ref[...]
        s = _sigmoid(pre)
        dsilu = s * (1.0 + pre * (1.0 - s))
        dext = jnp.concatenate([d_ref[...].astype(F32), dn_ref[...].astype(F32) * keep_n], axis=0)
        dpre = dext * dsilu
        dsh = _shifts_anticausal(dpre, 4, tm)
        o_ref[...] = _wsum(w, dsh).astype(BF16)
        dp = dpre[:tm]
        _acc_rows(dw_ref, [_colsum(dp * q[:tm]) for q in sh], i == 0)
        _acc_out(db_ref, _colsum(dp), i == 0)

    return R.call(
        body, "xbc_bwd", DX // _XW,
        [R.tile(_XW), R.next(_XW), R.tile(_XW, cb), R.prev(_XW, cb), R.next(_XW, cb),
         R.colvec(4, _XW), R.colvec(1, _XW), ANY],
        [R.tile(_XW, cb), R.colvec(4, _XW), R.colvec(1, _XW)],
        [jax.ShapeDtypeStruct(dproj.shape, BF16), jax.ShapeDtypeStruct((4, DX), F32),
         jax.ShapeDtypeStruct((1, DX), F32)],
        (dact, dact, proj, proj, proj, conv_w, conv_b, dproj), aliases={7: 0})


def _softplus(x):
    return jnp.maximum(x, 0.0) + jnp.log(1.0 + jnp.exp(-jnp.abs(x)))


def _dt_fwd(dt_raw, dt_bias_p, a_log_p):
    T = dt_raw.shape[0]

    def body(r_ref, b_ref, al_ref, dt_ref, ac_ref):
        dt = _softplus(r_ref[...] + b_ref[...])
        s = dt * (-jnp.exp(al_ref[...]))
        row = lax.broadcasted_iota(jnp.int32, (CH, LANES), 0)
        k = 1
        while k < CH:
            s = s + jnp.where(row >= k, pltpu.roll(s, k, 0), 0.0)
            k *= 2
        dt_ref[...] = dt
        ac_ref[...] = s

    blk = pl.BlockSpec((CH, LANES), lambda i: (i, 0))
    vec = pl.BlockSpec((1, LANES), lambda i: (0, 0))
    return pl.pallas_call(
        body, name="dt_fwd", grid=(T // CH,), in_specs=[blk, vec, vec], out_specs=[blk, blk],
        out_shape=[jax.ShapeDtypeStruct((T, LANES), F32)] * 2, compiler_params=_params(("parallel",)),
    )(dt_raw, dt_bias_p, a_log_p)


def _dt_bwd(dacum, ddt_x, dt_raw, dt_bias_p, a_log_p, dproj):
    T = dt_raw.shape[0]
    nc = T // CH

    def body(da_ref, dx_ref, r_ref, b_ref, al_ref, _alias, o_ref, db_ref, dal_ref):
        i = pl.program_id(0)
        a = -jnp.exp(al_ref[...])
        z = r_ref[...] + b_ref[...]
        dt = _softplus(z)
        s = da_ref[...]
        row = lax.broadcasted_iota(jnp.int32, (CH, LANES), 0)
        k = 1
        while k < CH:
            s = s + jnp.where(row < CH - k, pltpu.roll(s, CH - k, 0), 0.0)
            k *= 2
        ddt = s * a + dx_ref[...]
        draw = ddt * _sigmoid(z)
        o_ref[...] = draw.astype(BF16)
        _acc_out(db_ref, _colsum(draw), i == 0)
        _acc_out(dal_ref, _colsum(s * dt), i == 0)

        @pl.when(i == nc - 1)
        def _():
            dal_ref[...] = dal_ref[...] * a

    blk = pl.BlockSpec((CH, LANES), lambda i: (i, 0))
    vec = pl.BlockSpec((1, LANES), lambda i: (0, 0))
    oblk = pl.BlockSpec((CH, LANES), lambda i: (i, OFF_DT // LANES))
    return pl.pallas_call(
        body, name="dt_bwd", grid=(nc,), in_specs=[blk, blk, blk, vec, vec, ANY], out_specs=[oblk, vec, vec],
        out_shape=[jax.ShapeDtypeStruct(dproj.shape, BF16), jax.ShapeDtypeStruct((1, LANES), F32),
                   jax.ShapeDtypeStruct((1, LANES), F32)],
        input_output_aliases={5: 0}, compiler_params=_params(("arbitrary",)),
    )(dacum, ddt_x, dt_raw, dt_bias_p, a_log_p, dproj)


_GW = DI // NG
_HG = NH // NG
_NEG = -1e30


def _pair_lanes(left, v0, v1):
    return jnp.where(left, v0, v1)


def _ssd_specs(T, rev):
    nc = T // CH
    cm = (lambda c: nc - 1 - c) if rev else (lambda c: c)
    return dict(
        xs=pl.BlockSpec((CH, _GW), lambda g, c: (cm(c), g)),
        bm=pl.BlockSpec((CH, NS), lambda g, c: (cm(c), DI // NS + g)),
        cmat=pl.BlockSpec((CH, NS), lambda g, c: (cm(c), DI // NS + NG + g)),
        col=pl.BlockSpec((1, CH, _HG), lambda g, c: (g, cm(c), 0)),
        rowT=pl.BlockSpec((1, _HG, CH), lambda g, c: (g, 0, cm(c))),
        dsk=pl.BlockSpec((1, _GW), lambda g, c: (0, g)),
        state=pl.BlockSpec((1, 1, NS, _GW), lambda g, c: (g, cm(c), 0, 0)),
    )


def _ssd_fwd(xact, dtg, acg, acgT, dsk_rep):
    T = xact.shape[0]
    nc = T // CH
    sp = _ssd_specs(T, False)

    def body(xs_ref, b_ref, c_ref, dt_ref, ac_ref, acT_ref, dsk_ref, y_ref, sp_ref, S_ref):
        @pl.when(pl.program_id(1) == 0)
        def _():
            S_ref[...] = jnp.zeros_like(S_ref)

        Bm, Cm = b_ref[...], c_ref[...]
        dt, ac, acT = dt_ref[0], ac_ref[0], acT_ref[0]
        S = S_ref[...]
        sp_ref[0, 0] = S
        cb = _dot(Cm, Bm, NT)
        CS = _dot(Cm, S.astype(BF16))
        row = lax.broadcasted_iota(jnp.int32, (CH, CH), 0)
        col = lax.broadcasted_iota(jnp.int32, (CH, CH), 1)
        tril = row >= col
        left = col < HP
        xd_parts, dec_parts = [], []
        for p in range(_HG // 2):
            sl = slice(p * LANES, (p + 1) * LANES)
            j0, j1 = 2 * p, 2 * p + 1
            xp = xs_ref[:, sl].astype(F32)
            a0, a1 = ac[:, j0:j0 + 1], ac[:, j1:j1 + 1]
            al0, al1 = ac[CH - 1:CH, j0:j0 + 1], ac[CH - 1:CH, j1:j1 + 1]
            X = xp * _pair_lanes(left, dt[:, j0:j0 + 1], dt[:, j1:j1 + 1])
            Xb = X.astype(BF16)
            yd = jnp.zeros((CH, LANES), F32)
            for j, aj, mask in ((j0, a0, left), (j1, a1, jnp.logical_not(left))):
                Lm = jnp.exp(jnp.where(tril, aj - acT[j:j + 1, :], _NEG))
                W = (cb * Lm).astype(BF16)
                yd = yd + _dot(W, jnp.where(mask, Xb, jnp.zeros_like(Xb)))
            eal = _pair_lanes(left, jnp.exp(a0), jnp.exp(a1))
            y = yd + eal * CS[:, sl] + dsk_ref[:, sl] * xp
            y_ref[:, sl] = y.astype(BF16)
            xd_parts.append(X * _pair_lanes(left, jnp.exp(al0 - a0), jnp.exp(al1 - a1)))
            dec_parts.append(_pair_lanes(left[0:1], jnp.exp(al0), jnp.exp(al1)))
        Xd = jnp.concatenate(xd_parts, axis=1).astype(BF16)
        dec = jnp.concatenate(dec_parts, axis=1)
        S_ref[...] = dec * S + _dot(Bm, Xd, TN)

    return pl.pallas_call(
        body, name="ssd_fwd", grid=(NG, nc),
        in_specs=[sp["xs"], sp["bm"], sp["cmat"], sp["col"], sp["col"], sp["rowT"], sp["dsk"]],
        out_specs=[sp["xs"], sp["state"]],
        out_shape=[jax.ShapeDtypeStruct((T, DI), BF16), jax.ShapeDtypeStruct((NG, nc, NS, _GW), F32)],
        scratch_shapes=[pltpu.VMEM((NS, _GW), F32)],
        compiler_params=_params(("parallel", "arbitrary")),
    )(xact, xact, xact, dtg, acg, acgT, dsk_rep)


def _ssd_bwd(dy, xact, dtg, acg, acgT, dsk_rep, sprev):
    T = xact.shape[0]
    nc = T // CH
    sp = _ssd_specs(T, True)
    cm = lambda c: nc - 1 - c

    def body(xs_ref, b_ref, c_ref, dt_ref, ac_ref, acT_ref, dsk_ref, dy_ref, sp_ref,
             dx_ref, dB_ref, dC_ref, ddtx_ref, dAc_ref, dAr_ref, dskacc_ref, dS_ref):
        first = pl.program_id(1) == 0

        @pl.when(first)
        def _():
            dS_ref[...] = jnp.zeros_like(dS_ref)

        Bm, Cm = b_ref[...], c_ref[...]
        dt, ac, acT = dt_ref[0], ac_ref[0], acT_ref[0]
        S = sp_ref[0, 0]
        dS = dS_ref[...]
        Sb, dSb = S.astype(BF16), dS.astype(BF16)
        cb = _dot(Cm, Bm, NT)
        CS = _dot(Cm, Sb)
        T1 = _dot(Bm, dSb)
        row = lax.broadcasted_iota(jnp.int32, (CH, CH), 0)
        col = lax.broadcasted_iota(jnp.int32, (CH, CH), 1)
        tril = row >= col
        left = col < HP
        lane8 = lax.broadcasted_iota(jnp.int32, (1, _HG), 1)
        sub8 = lax.broadcasted_iota(jnp.int32, (_HG, 1), 0)
        lastrow = lax.broadcasted_iota(jnp.int32, (CH, 1), 0) == CH - 1
        dCB = jnp.zeros((CH, CH), F32)
        dAc = jnp.zeros((CH, _HG), F32)
        dAr = jnp.zeros((_HG, CH), F32)
        ddtx = jnp.zeros((CH, _HG), F32)
        xd_parts, dye_parts, dec_parts, dsk_parts = [], [], [], []
        for p in range(_HG // 2):
            sl = slice(p * LANES, (p + 1) * LANES)
            j0, j1 = 2 * p, 2 * p + 1
            xp = xs_ref[:, sl].astype(F32)
            dyp = dy_ref[:, sl].astype(F32)
            a0, a1 = ac[:, j0:j0 + 1], ac[:, j1:j1 + 1]
            al0, al1 = ac[CH - 1:CH, j0:j0 + 1], ac[CH - 1:CH, j1:j1 + 1]
            dtl = _pair_lanes(left, dt[:, j0:j0 + 1], dt[:, j1:j1 + 1])
            X = xp * dtl
            Xb = X.astype(BF16)
            eal = _pair_lanes(left, jnp.exp(a0), jnp.exp(a1))
            dtel = _pair_lanes(left, jnp.exp(al0 - a0), jnp.exp(al1 - a1))
            T1p = T1[:, sl]
            dXd = jnp.zeros((CH, LANES), F32)
            G1 = dyp * (eal * CS[:, sl])
            Rm = T1p * dtel * X
            SdS = dS[:, sl] * S[:, sl]
            for j, aj, alj, mask in ((j0, a0, al0, left), (j1, a1, al1, jnp.logical_not(left))):
                Lm = jnp.exp(jnp.where(tril, aj - acT[j:j + 1, :], _NEG))
                W = cb * Lm
                dYm = jnp.where(mask, dyp, 0.0).astype(BF16)
                dWm = _dot(dYm, Xb, NT)
                dCB = dCB + dWm * Lm
                Q = dWm * W
                dXd = dXd + _dot(W.astype(BF16), dYm, TN)
                g1 = jnp.sum(jnp.where(mask, G1, 0.0), axis=1, keepdims=True)
                r = jnp.sum(jnp.where(mask, Rm, 0.0), axis=1, keepdims=True)
                sds = jnp.sum(jnp.sum(jnp.where(mask, SdS, 0.0), axis=1, keepdims=True), axis=0, keepdims=True)
                dalast = jnp.sum(r, axis=0, keepdims=True) + jnp.exp(alj) * sds
                colv = jnp.sum(Q, axis=1, keepdims=True) + g1 - r + jnp.where(lastrow, dalast, 0.0)
                hot = (lane8 == j).astype(F32)
                dAc = dAc + colv * hot
                dAr = dAr - (sub8 == j).astype(F32) * jnp.sum(Q, axis=0, keepdims=True)
            dX = dXd + dtel * T1p
            dXx = dX * xp
            for j, mask in ((j0, left), (j1, jnp.logical_not(left))):
                dd = jnp.sum(jnp.where(mask, dXx, 0.0), axis=1, keepdims=True)
                ddtx = ddtx + dd * (lane8 == j).astype(F32)
            dx_ref[:, sl] = (dX * dtl + dsk_ref[:, sl] * dyp).astype(BF16)
            dsk_parts.append(_colsum(dyp * xp))
            xd_parts.append(X * dtel)
            dye_parts.append(dyp * eal)
            dec_parts.append(_pair_lanes(left[0:1], jnp.exp(al0), jnp.exp(al1)))
        Xd = jnp.concatenate(xd_parts, axis=1).astype(BF16)
        dYe = jnp.concatenate(dye_parts, axis=1).astype(BF16)
        dec = jnp.concatenate(dec_parts, axis=1)
        dCBb = dCB.astype(BF16)
        dC_ref[...] = (_dot(dCBb, Bm) + _dot(dYe, Sb, NT)).astype(BF16)
        dB_ref[...] = (_dot(dCBb, Cm, TN) + _dot(Xd, dSb, NT)).astype(BF16)
        dS_ref[...] = _dot(Cm, dYe, TN) + dec * dS
        ddtx_ref[0] = ddtx
        dAc_ref[0] = dAc
        dAr_ref[0] = dAr
        _acc_out(dskacc_ref, jnp.concatenate(dsk_parts, axis=1), first)

    dxs, dB, dC, ddtx, dAc, dAr, dskl = pl.pallas_call(
        body, name="ssd_bwd", grid=(NG, nc),
        in_specs=[sp["xs"], sp["bm"], sp["cmat"], sp["col"], sp["col"], sp["rowT"], sp["dsk"], sp["xs"],
                  sp["state"]],
        out_specs=[sp["xs"], pl.BlockSpec((CH, NS), lambda g, c: (cm(c), g)),
                   pl.BlockSpec((CH, NS), lambda g, c: (cm(c), g)), sp["col"], sp["col"], sp["rowT"],
                   sp["dsk"]],
        out_shape=[jax.ShapeDtypeStruct((T, DI), BF16), jax.ShapeDtypeStruct((T, NG * NS), BF16),
                   jax.ShapeDtypeStruct((T, NG * NS), BF16), jax.ShapeDtypeStruct((NG, T, _HG), F32),
                   jax.ShapeDtypeStruct((NG, T, _HG), F32), jax.ShapeDtypeStruct((NG, _HG, T), F32),
                   jax.ShapeDtypeStruct((1, DI), F32)],
        scratch_shapes=[pltpu.VMEM((NS, _GW), F32)],
        compiler_params=_params(("parallel", "arbitrary")),
    )(xact, xact, xact, dtg, acg, acgT, dsk_rep, dy, sprev)
    return dxs, dB, dC, ddtx, dAc, dAr, dskl


def _gnorm_fwd(y, proj, w):
    T = y.shape[0]
    R = _Rows(T, 512)
    zb = OFF_Z // _GW

    def body(y_ref, z_ref, w_ref, o_ref):
        z = z_ref[...].astype(F32)
        yf = y_ref[...].astype(F32) * z * _sigmoid(z)
        r = lax.rsqrt(jnp.mean(yf * yf, axis=-1, keepdims=True) + EPS)
        o_ref[...] = (yf * r * w_ref[...]).astype(BF16)

    return R.call(body, "gnorm_fwd", NG, [R.tile(_GW), R.tile(_GW, zb), R.colvec(1, _GW)], R.tile(_GW),
                  jax.ShapeDtypeStruct((T, DI), BF16), (y, proj, w))


def _gnorm_bwd(dn, y, proj, w, dproj):
    T = y.shape[0]
    R = _Rows(T, 512)
    zb = OFF_Z // _GW

    def body(dn_ref, y_ref, z_ref, w_ref, _alias, dz_ref, dy_ref, dw_ref):
        z = z_ref[...].astype(F32)
        yv = y_ref[...].astype(F32)
        s = _sigmoid(z)
        silu = z * s
        yf = yv * silu
        r = lax.rsqrt(jnp.mean(yf * yf, axis=-1, keepdims=True) + EPS)
        yh = yf * r
        dnv = dn_ref[...].astype(F32)
        dyh = dnv * w_ref[...]
        dyf = r * (dyh - yh * jnp.mean(dyh * yh, axis=-1, keepdims=True))
        dy_ref[...] = (dyf * silu).astype(BF16)
        dz_ref[...] = (dyf * yv * s * (1.0 + z * (1.0 - s))).astype(BF16)
        _acc_out(dw_ref, _colsum(dnv * yh), pl.program_id(1) == 0)

    return R.call(
        body, "gnorm_bwd", NG, [R.tile(_GW), R.tile(_GW), R.tile(_GW, zb), R.colvec(1, _GW), ANY],
        [R.tile(_GW, zb), R.tile(_GW), R.colvec(1, _GW)],
        [jax.ShapeDtypeStruct(dproj.shape, BF16), jax.ShapeDtypeStruct((T, DI), BF16),
         jax.ShapeDtypeStruct((1, DI), F32)],
        (dn, y, proj, w, dproj), aliases={4: 0})


def _merge_fwd(proj, ya, ys):
    T = proj.shape[0]
    R = _Rows(T, 256)
    gb = OFF_G // (2 * D)

    def body(g_ref, ya_ref, ys_ref, o_ref):
        ga = _sigmoid(g_ref[:, :D].astype(F32))
        gs = _sigmoid(g_ref[:, D:].astype(F32))
        o_ref[...] = (ga * ya_ref[...].astype(F32) + gs * ys_ref[...].astype(F32)).astype(BF16)

    return R.call(body, "merge_fwd", 1, [R.tile(2 * D, gb), R.tile(D), R.tile(D)], R.tile(D),
                  jax.ShapeDtypeStruct((T, D), BF16), (proj, ya, ys))


def _merge_bwd(dm, proj, ya, ys, ncols):
    T = proj.shape[0]
    R = _Rows(T, 256)
    gb = OFF_G // (2 * D)

    def body(dm_ref, g_ref, ya_ref, ys_ref, dg_ref, dya_ref, dys_ref):
        d = dm_ref[...].astype(F32)
        ga = _sigmoid(g_ref[:, :D].astype(F32))
        gs = _sigmoid(g_ref[:, D:].astype(F32))
        dya_ref[...] = (d * ga).astype(BF16)
        dys_ref[...] = (d * gs).astype(BF16)
        dg_ref[:, :D] = (d * ya_ref[...].astype(F32) * ga * (1.0 - ga)).astype(BF16)
        dg_ref[:, D:] = (d * ys_ref[...].astype(F32) * gs * (1.0 - gs)).astype(BF16)

    return R.call(
        body, "merge_bwd", 1, [R.tile(D), R.tile(2 * D, gb), R.tile(D), R.tile(D)],
        [R.tile(2 * D, gb), R.tile(D), R.tile(D)],
        [jax.ShapeDtypeStruct((T, ncols), BF16), jax.ShapeDtypeStruct((T, D), BF16),
         jax.ShapeDtypeStruct((T, D), BF16)],
        (dm, proj, ya, ys))


_FW = 256
_FB = FF // _FW


def _ffn_act_fwd(hv, conv_w, conv_b):
    T = hv.shape[0]
    R = _Rows(T, 512)
    tm = R.tm

    def body(h1_ref, h1p_ref, h3_ref, w_ref, b_ref, o_ref):
        keep = (pl.program_id(1) > 0).astype(F32)
        ext = jnp.concatenate([h1p_ref[...].astype(F32) * keep, h1_ref[...].astype(F32)], axis=0)
        pre = _wsum(w_ref[...], _shifts_causal(ext, 3, tm)) + b_ref[...]
        o_ref[...] = (pre * _sigmoid(pre) * h3_ref[...].astype(F32)).astype(BF16)

    return R.call(body, "ffn_act_fwd", _FB,
                  [R.tile(_FW), R.prev(_FW), R.tile(_FW, _FB), R.colvec(3, _FW), R.colvec(1, _FW)],
                  R.tile(_FW), jax.ShapeDtypeStruct((T, FF), BF16), (hv, hv, hv, conv_w, conv_b))


def _ffn_act_bwd(dg, hv, conv_w, conv_b):
    T = hv.shape[0]
    R = _Rows(T, 512)
    tm = R.tm

    def body(dg_ref, h1_ref, h1p_ref, h3_ref, w_ref, b_ref, dh3_ref, dpre_ref, dw_ref, db_ref):
        i = pl.program_id(1)
        keep = (i > 0).astype(F32)
        ext = jnp.concatenate([h1p_ref[...].astype(F32) * keep, h1_ref[...].astype(F32)], axis=0)
        sh = _shifts_causal(ext, 3, tm)
        pre = _wsum(w_ref[...], sh) + b_ref[...]
        s = _sigmoid(pre)
        d = dg_ref[...].astype(F32)
        dh3_ref[...] = (d * pre * s).astype(BF16)
        dpre = d * h3_ref[...].astype(F32) * s * (1.0 + pre * (1.0 - s))
        dpre_ref[...] = dpre.astype(BF16)
        _acc_rows(dw_ref, [_colsum(dpre * q) for q in sh], i == 0)
        _acc_out(db_ref, _colsum(dpre), i == 0)

    return R.call(
        body, "ffn_act_bwd", _FB,
        [R.tile(_FW), R.tile(_FW), R.prev(_FW), R.tile(_FW, _FB), R.colvec(3, _FW), R.colvec(1, _FW)],
        [R.tile(_FW), R.tile(_FW), R.colvec(3, _FW), R.colvec(1, _FW)],
        [jax.ShapeDtypeStruct((T, FF), BF16), jax.ShapeDtypeStruct((T, FF), BF16),
         jax.ShapeDtypeStruct((3, FF), F32), jax.ShapeDtypeStruct((1, FF), F32)],
        (dg, hv, hv, hv, conv_w, conv_b))


def _conv3_transpose(dpre, conv_w):
    T = dpre.shape[0]
    R = _Rows(T, 512)
    tm = R.tm

    def body(d_ref, dn_ref, w_ref, o_ref):
        keep = (pl.program_id(1) < R.nrow - 1).astype(F32)
        ext = jnp.concatenate([d_ref[...].astype(F32), dn_ref[...].astype(F32) * keep], axis=0)
        o_ref[...] = _wsum(w_ref[...], _shifts_anticausal(ext, 3, tm)).astype(BF16)

    return R.call(body, "ffn_conv_bwd", _FB, [R.tile(_FW), R.next(_FW), R.colvec(3, _FW)], R.tile(_FW),
                  jax.ShapeDtypeStruct((T, FF), BF16), (dpre, dpre, conv_w))


def _final_loss(h, w, target):
    T = h.shape[0]
    R = _Rows(T, 256)

    def body(h_ref, w_ref, t_ref, l_ref, dh_ref, dw_ref):
        first = pl.program_id(1) == 0
        xv = h_ref[...]
        wv = w_ref[...]
        r = lax.rsqrt(jnp.mean(xv * xv, axis=-1, keepdims=True) + EPS)
        xh = xv * r
        err = xh * wv - t_ref[...]
        part = 0.5 * jnp.sum(jnp.mean(err * err, axis=-1, keepdims=True), axis=0, keepdims=True)
        _acc_out(l_ref, jnp.broadcast_to(part, l_ref.shape), first)
        dy = err * (1.0 / D)
        dxh = dy * wv
        dh_ref[...] = r * (dxh - xh * jnp.mean(dxh * xh, axis=-1, keepdims=True))
        _acc_out(dw_ref, _colsum(dy * xh), first)

    return R.call(body, "final_loss", 1, [R.tile(D), R.colvec(1, D), R.tile(D)],
                  [R.colvec(8, LANES), R.tile(D), R.colvec(1, D)],
                  [jax.ShapeDtypeStruct((8, LANES), F32), jax.ShapeDtypeStruct((T, D), F32),
                   jax.ShapeDtypeStruct((1, D), F32)], (h, w, target))


def _pad_lanes(v, n=LANES):
    return jnp.pad(v, ((0, 0), (0, n - v.shape[1])))


def _group_cols(a):
    T = a.shape[0]
    return a[:, :NH].reshape(T, NG, _HG).transpose(1, 0, 2)


class _Hooks:
    def after_norm(self, u):
        return u

    def late_weights(self, wts, after):
        return wts

    def grads_ready(self, grads, tie):
        return tie

    def mark(self, name, value):
        pass


def _local_step(x, target, wts, hooks=None):
    hooks = hooks or _Hooks()
    T = x.shape[0]
    w_in = wts["w_in"]
    dt_bias_p, a_log_p = _pad_lanes(wts["dt_bias"]), _pad_lanes(wts["a_log"])
    dsk_rep = jnp.repeat(wts["d_skip"], HP, axis=1)

    u = hooks.after_norm(_rmsnorm_fwd(x, wts["norm_mix_w"], "norm_mix_fwd"))
    proj = _matmul(u, w_in, mode="nn", out_dtype=BF16, name="mm_in")
    dt_raw = _matmul(u, w_in[:, OFF_DT:], mode="nn", out_dtype=F32, name="mm_dt")
    ya_in = _branch_a_fwd(proj, wts["conv_a_w"])
    xact = _xbc_fwd(proj, wts["ssd_conv_w"], wts["ssd_conv_b"])
    dt, acum = _dt_fwd(dt_raw, dt_bias_p, a_log_p)
    dtg, acg = _group_cols(dt), _group_cols(acum)
    acgT = acg.transpose(0, 2, 1)
    y_ssd, sprev = _ssd_fwd(xact, dtg, acg, acgT, dsk_rep)
    yn = _gnorm_fwd(y_ssd, proj, wts["ssd_norm_w"])
    late = hooks.late_weights(wts, yn)
    w_a_out, w_s_out, w_o, w_up, w_down = (late[k] for k in ("w_a_out", "w_s_out", "w_o", "w_up", "w_down"))
    y_a = _matmul(ya_in, w_a_out, mode="nn", out_dtype=BF16, name="mm_a_out")
    y_s = _matmul(yn, w_s_out, mode="nn", out_dtype=BF16, name="mm_s_out")
    merged = _merge_fwd(proj, y_a, y_s)
    h1 = _matmul(merged, w_o, mode="nn", out_dtype=F32, name="mm_o", residual=x)
    v = _rmsnorm_fwd(h1, wts["norm_ffn_w"], "norm_ffn_fwd")
    hv = _matmul(v, w_up, mode="nn", out_dtype=BF16, name="mm_up")
    gact = _ffn_act_fwd(hv, wts["ffn_conv_w"], wts["ffn_conv_b"])
    h2 = _matmul(gact, w_down, mode="nn", out_dtype=F32, name="mm_down", residual=h1)
    loss, dh2, g_final = _final_loss(h2, wts["final_norm_w"], target)

    grads = {"final_norm_w": g_final}
    dh2b = dh2.astype(BF16)
    grads["w_down"] = _matmul(gact, dh2b, mode="tn", out_dtype=F32, name="mm_down_dw")
    dgact = _matmul(dh2b, w_down, mode="nt", out_dtype=BF16, name="mm_down_dx")
    dh3, dpre, grads["ffn_conv_w"], grads["ffn_conv_b"] = _ffn_act_bwd(dgact, hv, wts["ffn_conv_w"], wts["ffn_conv_b"])
    dh1c = _conv3_transpose(dpre, wts["ffn_conv_w"])
    grads["w_up"] = jnp.concatenate(
        [_matmul(v, dh1c, mode="tn", out_dtype=F32, name="mm_up_dw1"),
         _matmul(v, dh3, mode="tn", out_dtype=F32, name="mm_up_dw3")], axis=1)
    dv = _matmul(dh1c, w_up[:, :FF], mode="nt", out_dtype=F32, name="mm_up_dx1")
    dv = _matmul(dh3, w_up[:, FF:], mode="nt", out_dtype=F32, name="mm_up_dx3", residual=dv)
    dh1, grads["norm_ffn_w"] = _rmsnorm_bwd(dv, h1, wts["norm_ffn_w"], dh2, "norm_ffn_bwd")
    dh1b = dh1.astype(BF16)
    grads["w_o"] = _matmul(merged, dh1b, mode="tn", out_dtype=F32, name="mm_o_dw")
    dmerged = _matmul(dh1b, w_o, mode="nt", out_dtype=BF16, name="mm_o_dx")
    dproj, dya, dys = _merge_bwd(dmerged, proj, y_a, y_s, NIP)
    grads["w_a_out"] = _matmul(ya_in, dya, mode="tn", out_dtype=F32, name="mm_a_out_dw")
    dya_in = _matmul(dya, w_a_out, mode="nt", out_dtype=BF16, name="mm_a_out_dx")
    dproj, grads["conv_a_w"] = _branch_a_bwd(dya_in, proj, wts["conv_a_w"], dproj)
    grads["w_s_out"] = _matmul(yn, dys, mode="tn", out_dtype=F32, name="mm_s_out_dw")
    dys = hooks.grads_ready({k: grads[k] for k in ("w_a_out", "w_s_out", "w_o", "w_up", "w_down")}, dys)
    dyn =_matmul(dys, w_s_out, mode="nt", out_dtype=BF16, name="mm_s_out_dx")
    dproj, dy_ssd, grads["ssd_norm_w"] = _gnorm_bwd(dyn, y_ssd, proj, wts["ssd_norm_w"], dproj)
    dxs, dB, dC, ddtx, dAc, dAr, dskl = _ssd_bwd(dy_ssd, xact, dtg, acg, acgT, dsk_rep, sprev)
    hooks.mark("ssd_bwd", dxs)
    grads["d_skip"] = dskl.reshape(NH, HP).sum(axis=1).reshape(1, NH)
    dxact = jnp.concatenate([dxs, dB, dC], axis=1)
    dproj, grads["ssd_conv_w"], grads["ssd_conv_b"] = _xbc_bwd(dxact, proj, wts["ssd_conv_w"], wts["ssd_conv_b"], dproj)
    dacum = (dAc + dAr.transpose(0, 2, 1)).transpose(1, 0, 2).reshape(T, NH)
    ddt_x = ddtx.transpose(1, 0, 2).reshape(T, NH)
    dproj, g_dtb, g_alog = _dt_bwd(_pad_lanes(dacum), _pad_lanes(ddt_x), dt_raw, dt_bias_p, a_log_p, dproj)
    grads["dt_bias"], grads["a_log"] = g_dtb[:, :NH], g_alog[:, :NH]
    grads["w_in"] = _matmul(u, dproj, mode="tn", out_dtype=F32, name="mm_in_dw")
    dproj = hooks.grads_ready({"w_in": grads["w_in"]}, dproj)
    du =_matmul(dproj, w_in, mode="nt", out_dtype=F32, name="mm_in_dx")
    grad_x, grads["norm_mix_w"] = _rmsnorm_bwd(du, x, wts["norm_mix_w"], dh1, "norm_mix_bwd")
    return loss, grad_x, grads


def _permute_w_in(w):
    out = jnp.zeros((w.shape[0], NIP), w.dtype)
    for o, n, no in _SEGS:
        out = lax.dynamic_update_slice(out, w[:, o:o + n], (0, no))
    return out


def _unpermute_w_in(g):
    order = sorted(_SEGS)
    return jnp.concatenate([g[:, no:no + n] for o, n, no in order], axis=1)


MESH = pl.DeviceIdType.MESH
NCHIP = 4
NDEV = 8

_W_IN = (("w_in", D, NI // NCHIP, 1),)
_W_REST = (("w_a_out", D // NCHIP, D, 0), ("w_s_out", DI // NCHIP, D, 0), ("w_o", D // NCHIP, D, 0),
           ("w_up", D, 2 * FF // NCHIP, 1), ("w_down", FF // NCHIP, D, 0))


def _slab_rows(group):
    rows = [r * c // LANES for _, r, c, _ in group]
    assert all(n % 32 == 0 for n in rows), rows
    return rows


def _coords():
    return lax.axis_index("x"), lax.axis_index("y"), lax.axis_index("c")


def _other_chips(x, y):
    return [(1 - x, y), (x, 1 - y), (1 - x, 1 - y)]


def _ag_weights(shard):
    nrows = shard.shape[0]
    hr = nrows // 2

    def body(x_ref, out_ref, send_sems, recv_sems, local_sem):
        x, y, c = _coords()
        me = 2 * x + y
        chips = _other_chips(x, y)

        def rows(s, h):
            return out_ref.at[s, pl.ds(h * hr, hr), :]

        def copy(k, s, h, to, src=None):
            return pltpu.make_async_remote_copy(
                src_ref=rows(s, h) if src is None else src, dst_ref=rows(s, h),
                send_sem=send_sems.at[k], recv_sem=recv_sems.at[k], device_id=to, device_id_type=MESH)

        mine = pltpu.make_async_copy(x_ref, out_ref.at[me], local_sem)
        mine.start()
        first = [copy(k, me, c, (*chip, c), src=x_ref.at[pl.ds(c * hr, hr), :]) for k, chip in enumerate(chips)]
        for cp in first:
            cp.start()
        passed = []
        for k, chip in enumerate(chips):
            s = 2 * chip[0] + chip[1]
            copy(k, s, c, (x, y, c)).wait_recv()
            fwd = copy(3 + k, s, c, (x, y, 1 - c))
            fwd.start()
            passed.append(fwd)
        for k, chip in enumerate(chips):
            copy(3 + k, 2 * chip[0] + chip[1], 1 - c, (x, y, c)).wait_recv()
        for cp in first + passed:
            cp.wait_send()
        mine.wait()

    return pl.pallas_call(
        body, name="ag_weights", in_specs=[ANY], out_specs=ANY,
        out_shape=jax.ShapeDtypeStruct((NCHIP, nrows, LANES), shard.dtype),
        scratch_shapes=[pltpu.SemaphoreType.DMA((6,)), pltpu.SemaphoreType.DMA((6,)), pltpu.SemaphoreType.DMA],
        compiler_params=pltpu.CompilerParams(has_side_effects=True),
    )(shard)


HBM = pl.BlockSpec(memory_space=pltpu.HBM)
SEM = pl.BlockSpec(memory_space=pltpu.SEMAPHORE)
_EFFECT = pltpu.SideEffectType.DATAFLOW_SIDE_EFFECTING
_NCOPY = NCHIP - 1


def _plan_bcast(src_ref, land_ref, send_sems, recv_sems):
    x, y, c = _coords()
    sends, lands = [], []
    for k, chip in enumerate(_other_chips(x, y)):
        def copy(slot):
            return pltpu.make_async_remote_copy(
                src_ref=src_ref, dst_ref=land_ref.at[slot], send_sem=send_sems.at[k], recv_sem=recv_sems.at[k],
                device_id=(*chip, c), device_id_type=MESH)
        sends.append(copy(2 * x + y))
        lands.append(copy(2 * chip[0] + chip[1]))
    return sends, lands


def _plan_scatter(src_ref, land_ref, send_sems, recv_sems):
    x, y, c = _coords()
    cps = [pltpu.make_async_remote_copy(
        src_ref=src_ref.at[2 * chip[0] + chip[1]], dst_ref=land_ref.at[k], send_sem=send_sems.at[k],
        recv_sem=recv_sems.at[k], device_id=(*chip, c), device_id_type=MESH)
        for k, chip in enumerate(_other_chips(x, y))]
    return cps, cps


def _split_start(name, src, land, plan):
    def body(src_ref, land_ref, send_sems, recv_sems, src_thru, land_thru, token):
        for cp in plan(src_ref, land_ref, send_sems, recv_sems)[0]:
            cp.start()
        token[...] = jnp.zeros_like(token)

    send_sems, recv_sems, src_thru, land_thru, token = pl.pallas_call(
        body, name=name,
        out_shape=(pltpu.SemaphoreType.DMA((_NCOPY,)), pltpu.SemaphoreType.DMA((_NCOPY,)),
                   pltpu.HBM(src.shape, src.dtype), pltpu.HBM(land.shape, land.dtype),
                   jax.ShapeDtypeStruct((8, LANES), F32)),
        in_specs=(HBM, HBM), out_specs=(SEM, SEM, HBM, HBM, pl.BlockSpec(memory_space=pltpu.VMEM)),
        input_output_aliases={0: 2, 1: 3},
        compiler_params=pltpu.CompilerParams(has_side_effects=_EFFECT),
    )(pltpu.with_memory_space_constraint(src, pltpu.HBM), pltpu.with_memory_space_constraint(land, pltpu.HBM))
    return (send_sems, recv_sems, src_thru, land_thru), token


def _split_wait(name, handle, after, plan):
    send_sems, recv_sems, src_thru, land_thru = handle

    def body(src_ref, land_ref, send_sems, recv_sems, after_ref, src_out, land_out):
        sends, lands = plan(src_ref, land_ref, send_sems, recv_sems)
        for cp in sends:
            cp.wait_send()
        for cp in lands:
            cp.wait_recv()

    return pl.pallas_call(
        body, name=name,
        out_shape=(pltpu.HBM(src_thru.shape, src_thru.dtype), pltpu.HBM(land_thru.shape, land_thru.dtype)),
        in_specs=(HBM, HBM, SEM, SEM, ANY), out_specs=(HBM, HBM), input_output_aliases={0: 0, 1: 1},
        compiler_params=pltpu.CompilerParams(has_side_effects=_EFFECT),
    )(src_thru, land_thru, send_sems, recv_sems, after)


def _tie(x, token, name):
    def body(x_ref, t_ref, o_ref):
        pass

    return pl.pallas_call(
        body, name=name, in_specs=[ANY, pl.BlockSpec(memory_space=pltpu.VMEM)], out_specs=ANY,
        out_shape=jax.ShapeDtypeStruct(x.shape, x.dtype), input_output_aliases={0: 0},
    )(x, token)


def _swap_sibling(p, name):
    def body(p_ref, land_ref, send_sem, recv_sem):
        x, y, c = _coords()
        cp = pltpu.make_async_remote_copy(
            src_ref=p_ref, dst_ref=land_ref, send_sem=send_sem, recv_sem=recv_sem,
            device_id=(x, y, 1 - c), device_id_type=MESH)
        cp.start()
        cp.wait()

    return pl.pallas_call(
        body, name=name, in_specs=[ANY], out_specs=ANY, out_shape=jax.ShapeDtypeStruct(p.shape, p.dtype),
        scratch_shapes=[pltpu.SemaphoreType.DMA, pltpu.SemaphoreType.DMA],
        compiler_params=pltpu.CompilerParams(has_side_effects=True),
    )(p)


_ADD_BYTES = 7 << 19


def _add_tile(rows):
    best = 32
    for t in range(32, rows + 1, 32):
        if rows % t == 0 and t * LANES * 4 <= _ADD_BYTES:
            best = t
    return best


def _add_slabs(pack, land, me, name):
    rows = pack.shape[1]
    tr = _add_tile(rows)

    def body(me_ref, p_ref, l_ref, o_ref):
        f = lambda r: r.astype(F32)
        o_ref[...] = ((f(p_ref[0]) + f(l_ref[0])) + f(l_ref[1])) + f(l_ref[2])

    return pl.pallas_call(
        body, name=name,
        grid_spec=pltpu.PrefetchScalarGridSpec(
            num_scalar_prefetch=1, grid=(rows // tr,),
            in_specs=[pl.BlockSpec((1, tr, LANES), lambda i, me_ref: (me_ref[0], i, 0)),
                      pl.BlockSpec((_NCOPY, tr, LANES), lambda i, me_ref: (0, i, 0))],
            out_specs=pl.BlockSpec((tr, LANES), lambda i, me_ref: (i, 0))),
        out_shape=jax.ShapeDtypeStruct((rows, LANES), F32),
        compiler_params=_params(("parallel",)),
    )(me, pack, land)


def _add_pair(a, b, name):
    rows = a.shape[0]
    tr = _add_tile(rows)

    def body(a_ref, b_ref, o_ref):
        o_ref[...] = a_ref[...] + b_ref[...]

    blk = pl.BlockSpec((tr, LANES), lambda i: (i, 0))
    return pl.pallas_call(
        body, name=name, grid=(rows // tr,), in_specs=[blk, blk], out_specs=blk,
        out_shape=jax.ShapeDtypeStruct((rows, LANES), F32), compiler_params=_params(("parallel",)),
    )(a, b)


def _gather8(v, reduce, name):
    rows = v.shape[0]

    def body(v_ref, o_ref, buf, send_sems, recv_sems):
        x, y, c = _coords()
        me = 4 * x + 2 * y + c
        buf[pl.ds(me, 1)] = v_ref[...][None]
        cps, lands = [], []
        for k in range(1, NDEV):
            peer = (1 - x if k & 4 else x, 1 - y if k & 2 else y, 1 - c if k & 1 else c)

            def copy(slot):
                return pltpu.make_async_remote_copy(
                    src_ref=v_ref, dst_ref=buf.at[slot], send_sem=send_sems.at[k - 1],
                    recv_sem=recv_sems.at[k - 1], device_id=peer, device_id_type=MESH)

            cps.append(copy(me))
            lands.append(copy(4 * peer[0] + 2 * peer[1] + peer[2]))
        for cp in cps:
            cp.start()
        for cp, land in zip(cps, lands):
            land.wait_recv()
            cp.wait_send()
        if reduce:
            acc = buf[0]
            for d in range(1, NDEV):
                acc = acc + buf[d]
            o_ref[...] = acc
        else:
            o_ref[...] = buf[...]

    vm = pl.BlockSpec(memory_space=pltpu.VMEM)
    out_shape = (rows, LANES) if reduce else (NDEV, rows, LANES)
    return pl.pallas_call(
        body, name=name, in_specs=[vm], out_specs=vm, out_shape=jax.ShapeDtypeStruct(out_shape, F32),
        scratch_shapes=[pltpu.VMEM((NDEV, rows, LANES), F32), pltpu.SemaphoreType.DMA((NDEV - 1,)),
                        pltpu.SemaphoreType.DMA((NDEV - 1,))],
        compiler_params=pltpu.CompilerParams(has_side_effects=True),
    )(v)


def _adamw(w, g, m, v, name):
    rows, cols = w.shape
    tr = rows
    while tr * cols * 4 > (3 << 19) and tr % 16 == 0:
        tr //= 2
    c1 = 1.0 / (1.0 - ADAM_B1 ** ADAM_STEP)
    c2 = 1.0 / (1.0 - ADAM_B2 ** ADAM_STEP)

    def body(w_ref, g_ref, m_ref, v_ref, d_ref, mo_ref, vo_ref):
        gv = g_ref[...]
        mn = ADAM_B1 * m_ref[...] + (1.0 - ADAM_B1) * gv
        vn = ADAM_B2 * v_ref[...] + (1.0 - ADAM_B2) * (gv * gv)
        d_ref[...] = -ADAM_LR * ((mn * c1) / (jnp.sqrt(vn * c2) + ADAM_EPS) + ADAM_WD * w_ref[...])
        mo_ref[...] = mn
        vo_ref[...] = vn

    blk = pl.BlockSpec((tr, cols), lambda i: (i, 0))
    return pl.pallas_call(
        body, name=name, grid=(rows // tr,), in_specs=[blk] * 4, out_specs=[blk] * 3,
        out_shape=[jax.ShapeDtypeStruct((rows, cols), F32)] * 3, compiler_params=_params(("parallel",)),
    )(w, g, m, v)


def _rows128(a, mult=8):
    flat = a.reshape(-1)
    n = -(-flat.shape[0] // (LANES * mult)) * LANES * mult
    return jnp.pad(flat, (0, n - flat.shape[0])).reshape(-1, LANES)


def _pack_rows(parts, total_rows):
    rows = sum(p.shape[0] for p in parts)
    if total_rows > rows:
        parts = list(parts) + [jnp.zeros((total_rows - rows, LANES), parts[0].dtype)]
    return jnp.concatenate(parts, axis=0)


def _unpack_rows(pack, shapes, mult=8):
    out, r = [], 0
    for shp in shapes:
        n = int(np.prod(shp))
        nr = -(-n // (LANES * mult)) * mult
        out.append(pack[r:r + nr].reshape(-1)[:n].reshape(shp))
        r += nr
    return out


def _unpack_full(full, group):
    out, r = {}, 0
    for (name, rr, cc, axis), nr in zip(group, _slab_rows(group)):
        seg = full[:, r:r + nr].reshape(NCHIP, rr, cc)
        out[name] = seg.reshape(NCHIP * rr, cc) if axis == 0 else seg.transpose(1, 0, 2).reshape(rr, NCHIP * cc)
        r += nr
    return out


def _pack_by_chip(grads, group, dtype):
    parts = []
    for (name, rr, cc, axis), nr in zip(group, _slab_rows(group)):
        g = grads[name].astype(dtype)
        seg = g.reshape(NCHIP, rr, cc) if axis == 0 else g.reshape(rr, NCHIP, cc).transpose(1, 0, 2)
        parts.append(seg.reshape(NCHIP, nr, LANES))
    return parts[0] if len(parts) == 1 else jnp.concatenate(parts, axis=1)


_SMALL_REPL = ("norm_mix_w", "ssd_conv_b", "dt_bias", "a_log", "d_skip", "ssd_norm_w", "norm_ffn_w",
               "ffn_conv_b", "final_norm_w")
_SMALL_CONV = (("conv_a_w", 3, D), ("ssd_conv_w", 4, DX), ("ffn_conv_w", 3, FF))


def kernel(x, norm_mix_w, w_in, conv_a_w, w_a_out, ssd_conv_w, ssd_conv_b, dt_bias, a_log, d_skip, ssd_norm_w, w_s_out, w_o, norm_ffn_w, w_up, ffn_conv_w, ffn_conv_b, w_down, final_norm_w, loss_target, m_norm_mix_w, m_w_in, m_conv_a_w, m_w_a_out, m_ssd_conv_w, m_ssd_conv_b, m_dt_bias, m_a_log, m_d_skip, m_ssd_norm_w, m_w_s_out, m_w_o, m_norm_ffn_w, m_w_up, m_ffn_conv_w, m_ffn_conv_b, m_w_down, m_final_norm_w, v_norm_mix_w, v_w_in, v_conv_a_w, v_w_a_out, v_ssd_conv_w, v_ssd_conv_b, v_dt_bias, v_a_log, v_d_skip, v_ssd_norm_w, v_w_s_out, v_w_o, v_norm_ffn_w, v_w_up, v_ffn_conv_w, v_ffn_conv_b, v_w_down, v_final_norm_w):
    names = ("norm_mix_w", "w_in", "conv_a_w", "w_a_out", "ssd_conv_w", "ssd_conv_b", "dt_bias", "a_log", "d_skip",
             "ssd_norm_w", "w_s_out", "w_o", "norm_ffn_w", "w_up", "ffn_conv_w", "ffn_conv_b", "w_down", "final_norm_w")
    W = dict(zip(names, (norm_mix_w, w_in, conv_a_w, w_a_out, ssd_conv_w, ssd_conv_b, dt_bias, a_log, d_skip,
                         ssd_norm_w, w_s_out, w_o, norm_ffn_w, w_up, ffn_conv_w, ffn_conv_b, w_down, final_norm_w)))
    M = dict(zip(names, (m_norm_mix_w, m_w_in, m_conv_a_w, m_w_a_out, m_ssd_conv_w, m_ssd_conv_b, m_dt_bias, m_a_log,
                         m_d_skip, m_ssd_norm_w, m_w_s_out, m_w_o, m_norm_ffn_w, m_w_up, m_ffn_conv_w, m_ffn_conv_b,
                         m_w_down, m_final_norm_w)))
    V = dict(zip(names, (v_norm_mix_w, v_w_in, v_conv_a_w, v_w_a_out, v_ssd_conv_w, v_ssd_conv_b, v_dt_bias, v_a_log,
                         v_d_skip, v_ssd_norm_w, v_w_s_out, v_w_o, v_norm_ffn_w, v_w_up, v_ffn_conv_w, v_ffn_conv_b,
                         v_w_down, v_final_norm_w)))
    two_d = lambda a: a.reshape(-1, a.shape[-1])
    W2, M2, V2 = ({k: two_d(a) for k, a in t.items()} for t in (W, M, V))
    xi, yi, ci = _coords()
    me = 2 * xi + yi

    meidx = me.reshape(1).astype(jnp.int32)
    state = {}

    def slab(group):
        return _pack_rows([_rows128(W2[n], 16) for n, *_ in group], 0).astype(BF16)

    class Hooks(_Hooks):
        def after_norm(self, u):
            return _tie(u, state["rest_token"], "tie_ag_rest")

        def late_weights(self, wts, after):
            own, land = _split_wait("ag_rest_wait", state["rest"], after, _plan_bcast)
            full = _unpack_full(lax.dynamic_update_slice(land, own[None], (me, 0, 0)), _W_REST)
            return {**wts, **full}

        def grads_ready(self, grads, tie):
            if "w_in" in grads:
                key, group, grads = "g_in", _W_IN, {"w_in": _unpermute_w_in(grads["w_in"])}
            else:
                key, group = "g_rest", _W_REST
            pack = _pack_by_chip(grads, group, BF16)
            land = lax.empty((_NCOPY,) + pack.shape[1:], BF16)
            state[key], token = _split_start("rs_" + key + "_start", pack, land, _plan_scatter)
            return _tie(tie, token, "tie_" + key)

        def mark(self, name, value):
            state[name] = value

    def reduced(key, after):
        pack, land = _split_wait("rs_" + key + "_wait", state[key], after, _plan_scatter)
        mine = _add_slabs(pack, land, meidx, "rs_" + key + "_add_chips")
        return _add_pair(mine, _swap_sibling(mine, "rs_" + key + "_swap"), "rs_" + key + "_add_cores")

    w_in_full = _unpack_full(_ag_weights(slab(_W_IN)), _W_IN)["w_in"]
    rest_slab = slab(_W_REST)
    state["rest"], state["rest_token"] = _split_start(
        "ag_rest_start", rest_slab, lax.empty((NCHIP,) + rest_slab.shape, BF16), _plan_bcast)
    conv_shards = _pack_rows([_rows128(W2[n]) for n, *_ in _SMALL_CONV], 0)
    conv_all = _gather8(conv_shards, False, "ag_conv_weights")[0::2]
    wts = {k: W2[k] for k in _SMALL_REPL}
    r = 0
    for n, kk, width in _SMALL_CONV:
        cw = width // NCHIP
        nr = -(-kk * cw // (LANES * 8)) * 8
        wts[n] = conv_all[:, r:r + nr].reshape(NCHIP, -1)[:, :kk * cw].reshape(NCHIP, kk, cw).transpose(1, 0, 2).reshape(kk, width)
        r += nr
    wts["w_in"] = _permute_w_in(w_in_full)

    loss8, grad_x, grads = _local_step(x[0], loss_target[0], wts, Hooks())

    gbig = {}
    for key, group, after in (("g_rest", _W_REST, state["ssd_bwd"]), ("g_in", _W_IN, grad_x)):
        shapes = [(rr, cc) for _, rr, cc, _ in group]
        gbig.update(zip([n for n, *_ in group], _unpack_rows(reduced(key, after), shapes, 16)))

    small_shapes = [W2[n].shape for n in _SMALL_REPL] + [(1, LANES)] + [(kk, width) for _, kk, width in _SMALL_CONV]
    small_parts = [grads[n] for n in _SMALL_REPL] + [loss8[0:1]] + [grads[n] for n, *_ in _SMALL_CONV]
    small = _gather8(_pack_rows([_rows128(p) for p in small_parts], 0), True, "allreduce_small")
    small_g = _unpack_rows(small, small_shapes)
    gsm = dict(zip(_SMALL_REPL, small_g[:len(_SMALL_REPL)]))
    loss = small_g[len(_SMALL_REPL)][0, 0]
    for (n, kk, width), gfull in zip(_SMALL_CONV, small_g[len(_SMALL_REPL) + 1:]):
        cw = width // NCHIP
        gsm[n] = lax.dynamic_slice(gfull, (0, me * cw), (kk, cw))

    G, DW, NM, NV = {}, {}, {}, {}
    for n in [b[0] for b in _W_IN + _W_REST]:
        G[n] = gbig[n]
        DW[n], NM[n], NV[n] = _adamw(W2[n], G[n], M2[n], V2[n], "adamw_" + n)
    sm_names = list(_SMALL_REPL) + [n for n, *_ in _SMALL_CONV]
    sm_shapes = [W2[n].shape for n in sm_names]
    packs = [_pack_rows([_rows128(t[n]) for n in sm_names], 0) for t in (W2, gsm, M2, V2)]
    outs = _adamw(*packs, "adamw_small")
    for t, pk in zip((DW, NM, NV), outs):
        t.update(dict(zip(sm_names, _unpack_rows(pk, sm_shapes))))
    G.update(gsm)

    def shaped(t):
        return [t[n].reshape(W[n].shape) for n in names]

    return (loss, grad_x.reshape(x.shape), *shaped(G), *shaped(DW), *shaped(NM), *shaped(NV))
```

```python
import functools

import jax
import jax.numpy as jnp
import numpy as np
from jax import lax
from jax.experimental import pallas as pl
from jax.experimental.pallas import tpu as pltpu

F32 = jnp.float32
BF16 = jnp.bfloat16

D = 1024
DI = 2048
NH = 32
HP = 64
NG = 4
NS = 128
CH = 128
DX = 3072
FF = 2816
NI = 10272
EPS = 1e-5

OFF_BCV, OFF_XBC, OFF_G, OFF_Z, OFF_DT = 0, 3072, 6144, 8192, 10240
NIP = 10368
_SEGS = ((0, 2048, OFF_G), (2048, 3072, OFF_BCV), (5120, 2048, OFF_Z), (7168, 3072, OFF_XBC), (10240, 32, OFF_DT))

LANES = 128
HALO = 16
V7X_VMEM_LIMIT = 56 * 2 ** 20

ADAM_LR, ADAM_B1, ADAM_B2, ADAM_EPS, ADAM_WD, ADAM_STEP = 0.001, 0.9, 0.999, 1e-08, 0.01, 10

NN = (((1,), (0,)), ((), ()))
NT = (((1,), (1,)), ((), ()))
TN = (((0,), (0,)), ((), ()))


def _dot(a, b, dims=NN):
    return lax.dot_general(a, b, dims, preferred_element_type=F32)


def _params(sem, **kw):
    return pltpu.CompilerParams(dimension_semantics=sem, vmem_limit_bytes=V7X_VMEM_LIMIT, **kw)


def _pick(dim, cap):
    if dim <= cap:
        return dim
    best = None
    for t in range(LANES, cap + 1, LANES):
        if dim % t == 0:
            best = t
    assert best is not None, (dim, cap)
    return best


def _sigmoid(x):
    return 1.0 / (1.0 + jnp.exp(-x))


def _matmul(a, b, *, mode, out_dtype, name, residual=None, caps=(1408, 1408, 1408)):
    if mode == "nn":
        (M, K), (K2, N) = a.shape, b.shape
    elif mode == "nt":
        (M, K), (N, K2) = a.shape, b.shape
    else:
        (K, M), (K2, N) = a.shape, b.shape
    assert K == K2, (name, a.shape, b.shape)
    tm, tn, tk = _pick(M, caps[0]), _pick(N, caps[1]), _pick(K, caps[2])
    nk = K // tk
    if mode == "tn":
        a_spec = pl.BlockSpec((tk, tm), lambda i, j, k: (k, i))
    else:
        a_spec = pl.BlockSpec((tm, tk), lambda i, j, k: (i, k))
    if mode == "nt":
        b_spec = pl.BlockSpec((tn, tk), lambda i, j, k: (j, k))
    else:
        b_spec = pl.BlockSpec((tk, tn), lambda i, j, k: (k, j))
    dims = {"nn": NN, "nt": NT, "tn": TN}[mode]
    o_spec = pl.BlockSpec((tm, tn), lambda i, j, k: (i, j))
    has_res = residual is not None

    def body(*refs):
        if has_res:
            a_ref, b_ref, r_ref, o_ref, acc_ref = refs
        else:
            a_ref, b_ref, o_ref, acc_ref = refs
        k = pl.program_id(2)

        @pl.when(k == 0)
        def _():
            acc_ref[...] = jnp.zeros_like(acc_ref)

        acc_ref[...] += _dot(a_ref[...], b_ref[...], dims)

        @pl.when(k == nk - 1)
        def _():
            r = acc_ref[...]
            if has_res:
                r = r + r_ref[...].astype(F32)
            o_ref[...] = r.astype(out_dtype)

    in_specs = [a_spec, b_spec] + ([o_spec] if has_res else [])
    args = (a, b) + ((residual,) if has_res else ())
    return pl.pallas_call(
        body, name=name, grid=(M // tm, N // tn, nk), in_specs=in_specs, out_specs=o_spec,
        out_shape=jax.ShapeDtypeStruct((M, N), out_dtype),
        scratch_shapes=[pltpu.VMEM((tm, tn), F32)],
        compiler_params=_params(("parallel", "parallel", "arbitrary")),
    )(*args)


class _Rows:
    def __init__(self, T, tm):
        self.T, self.tm = T, min(tm, T // 2)
        self.nrow = T // self.tm
        self.r = self.tm // HALO
        self.nb = T // HALO

    def tile(self, w, cb=0, step=1):
        return pl.BlockSpec((self.tm, w), lambda j, i: (i, cb + step * j))

    def prev(self, w, cb=0, step=1):
        r = self.r
        return pl.BlockSpec((HALO, w), lambda j, i: (jnp.maximum(i * r - 1, 0), cb + step * j))

    def next(self, w, cb=0, step=1):
        r, nb = self.r, self.nb
        return pl.BlockSpec((HALO, w), lambda j, i: (jnp.minimum((i + 1) * r, nb - 1), cb + step * j))

    def colvec(self, k, w, cb=0, step=1):
        return pl.BlockSpec((k, w), lambda j, i: (0, cb + step * j))

    def call(self, body, name, ncol, in_specs, out_specs, out_shape, args, aliases=None):
        return pl.pallas_call(
            body, name=name, grid=(ncol, self.nrow), in_specs=in_specs, out_specs=out_specs,
            out_shape=out_shape, input_output_aliases=aliases or {},
            compiler_params=_params(("parallel", "arbitrary")),
        )(*args)


ANY = pl.BlockSpec(memory_space=pl.ANY)


def _shifts_causal(ext, nk, tm):
    out = []
    for k in range(nk):
        s = nk - 1 - k
        r = ext if s == 0 else pltpu.roll(ext, s, 0)
        out.append(r[HALO:])
    return out


def _shifts_anticausal(ext, nk, tm):
    n = ext.shape[0]
    out = []
    for k in range(nk):
        s = nk - 1 - k
        r = ext if s == 0 else pltpu.roll(ext, n - s, 0)
        out.append(r[:tm])
    return out


def _wsum(w, parts):
    acc = w[0:1, :] * parts[0]
    for k in range(1, len(parts)):
        acc = acc + w[k:k + 1, :] * parts[k]
    return acc


def _colsum(x):
    return jnp.sum(x, axis=0, keepdims=True)


def _acc_out(ref, val, first):
    @pl.when(first)
    def _():
        ref[...] = val

    @pl.when(jnp.logical_not(first))
    def _():
        ref[...] += val


def _acc_rows(ref, rows, first):
    for k, r in enumerate(rows):
        _acc_out(ref.at[k:k + 1, :], r, first)


def _rmsnorm_fwd(x, w, name):
    T = x.shape[0]
    R = _Rows(T, 512)

    def body(x_ref, w_ref, o_ref):
        xv = x_ref[...]
        r = lax.rsqrt(jnp.mean(xv * xv, axis=-1, keepdims=True) + EPS)
        o_ref[...] = (xv * r * w_ref[...]).astype(BF16)

    return R.call(body, name, 1, [R.tile(D), R.colvec(1, D)], R.tile(D),
                  jax.ShapeDtypeStruct((T, D), BF16), (x, w))


def _rmsnorm_bwd(dy, x, w, dres, name):
    T = x.shape[0]
    R = _Rows(T, 512)

    def body(dy_ref, x_ref, w_ref, dr_ref, dx_ref, dxb_ref, dw_ref):
        xv = x_ref[...]
        r = lax.rsqrt(jnp.mean(xv * xv, axis=-1, keepdims=True) + EPS)
        xh = xv * r
        dyv = dy_ref[...].astype(F32)
        dxh = dyv * w_ref[...]
        dx = r * (dxh - xh * jnp.mean(dxh * xh, axis=-1, keepdims=True)) + dr_ref[...]
        dx_ref[...] = dx
        dxb_ref[...] = dx.astype(BF16)
        _acc_out(dw_ref, _colsum(dyv * xh), pl.program_id(1) == 0)

    return R.call(body, name, 1, [R.tile(D), R.tile(D), R.colvec(1, D), R.tile(D)],
                  [R.tile(D), R.tile(D), R.colvec(1, D)],
                  [jax.ShapeDtypeStruct((T, D), F32), jax.ShapeDtypeStruct((T, D), BF16),
                   jax.ShapeDtypeStruct((1, D), F32)],
                  (dy, x, w, dres))


def _branch_a_fwd(proj, conv_w):
    T = proj.shape[0]
    R = _Rows(T, 512)
    tm = R.tm

    def body(p_ref, pp_ref, w_ref, o_ref):
        keep = (pl.program_id(1) > 0).astype(F32)
        cv = p_ref[:, D:2 * D].astype(F32) * p_ref[:, 2 * D:].astype(F32)
        cvp = pp_ref[:, D:2 * D].astype(F32) * pp_ref[:, 2 * D:].astype(F32) * keep
        sh = _shifts_causal(jnp.concatenate([cvp, cv], axis=0), 3, tm)
        ca = _wsum(w_ref[...], sh)
        o_ref[...] = (p_ref[:, :D].astype(F32) * ca).astype(BF16)

    return R.call(body, "branch_a_fwd", 1, [R.tile(3 * D), R.prev(3 * D), R.colvec(3, D)], R.tile(D),
                  jax.ShapeDtypeStruct((T, D), BF16), (proj, proj, conv_w))


def _branch_a_bwd(dya_in, proj, conv_w, dproj):
    T = proj.shape[0]
    R = _Rows(T, 256)
    tm = R.tm

    def body(d_ref, dn_ref, p_ref, pp_ref, pn_ref, w_ref, _alias, o_ref, dw_ref):
        i = pl.program_id(1)
        keep_p = (i > 0).astype(F32)
        keep_n = (i < R.nrow - 1).astype(F32)
        w = w_ref[...]
        b = p_ref[:, :D].astype(F32)
        c = p_ref[:, D:2 * D].astype(F32)
        v = p_ref[:, 2 * D:].astype(F32)
        cvp = pp_ref[:, D:2 * D].astype(F32) * pp_ref[:, 2 * D:].astype(F32) * keep_p
        sh = _shifts_causal(jnp.concatenate([cvp, c * v], axis=0), 3, tm)
        ca = _wsum(w, sh)
        d = d_ref[...].astype(F32)
        dca = d * b
        dca_n = dn_ref[...].astype(F32) * pn_ref[:, :D].astype(F32) * keep_n
        dsh = _shifts_anticausal(jnp.concatenate([dca, dca_n], axis=0), 3, tm)
        dcv = _wsum(w, dsh)
        o_ref[:, :D] = (d * ca).astype(BF16)
        o_ref[:, D:2 * D] = (dcv * v).astype(BF16)
        o_ref[:, 2 * D:] = (dcv * c).astype(BF16)
        _acc_rows(dw_ref, [_colsum(dca * s) for s in sh], i == 0)

    return R.call(
        body, "branch_a_bwd", 1,
        [R.tile(D), R.next(D), R.tile(3 * D), R.prev(3 * D), R.next(3 * D), R.colvec(3, D), ANY],
        [R.tile(3 * D), R.colvec(3, D)],
        [jax.ShapeDtypeStruct(dproj.shape, BF16), jax.ShapeDtypeStruct((3, D), F32)],
        (dya_in, dya_in, proj, proj, proj, conv_w, dproj), aliases={6: 0})


_XW = 512


def _xbc_fwd(proj, conv_w, conv_b):
    T = proj.shape[0]
    R = _Rows(T, 512)
    tm = R.tm
    cb = OFF_XBC // _XW

    def body(x_ref, xp_ref, w_ref, b_ref, o_ref):
        keep = (pl.program_id(1) > 0).astype(F32)
        ext = jnp.concatenate([xp_ref[...].astype(F32) * keep, x_ref[...].astype(F32)], axis=0)
        pre = _wsum(w_ref[...], _shifts_causal(ext, 4, tm)) + b_ref[...]
        o_ref[...] = (pre * _sigmoid(pre)).astype(BF16)

    return R.call(body, "xbc_fwd", DX // _XW,
                  [R.tile(_XW, cb), R.prev(_XW, cb), R.colvec(4, _XW), R.colvec(1, _XW)], R.tile(_XW),
                  jax.ShapeDtypeStruct((T, DX), BF16), (proj, proj, conv_w, conv_b))


def _xbc_bwd(dact, proj, conv_w, conv_b, dproj):
    T = proj.shape[0]
    R = _Rows(T, 512)
    tm = R.tm
    cb = OFF_XBC // _XW

    def body(d_ref, dn_ref, x_ref, xp_ref, xn_ref, w_ref, b_ref, _alias, o_ref, dw_ref, db_ref):
        i = pl.program_id(1)
        keep_p = (i > 0).astype(F32)
        keep_n = (i < R.nrow - 1).astype(F32)
        w = w_ref[...]
        ext = jnp.concatenate([xp_ref[...].astype(F32) * keep_p, x_ref[...].astype(F32),
                               xn_ref[...].astype(F32)], axis=0)
        sh = _shifts_causal(ext, 4, tm + HALO)
        pre = _wsum(w, sh) + b_ref[...]
        s = _sigmoid(pre)
        dsilu = s * (1.0 + pre * (1.0 - s))
        dext = jnp.concatenate([d_ref[...].astype(F32), dn_ref[...].astype(F32) * keep_n], axis=0)
        dpre = dext * dsilu
        dsh = _shifts_anticausal(dpre, 4, tm)
        o_ref[...] = _wsum(w, dsh).astype(BF16)
        dp = dpre[:tm]
        _acc_rows(dw_ref, [_colsum(dp * q[:tm]) for q in sh], i == 0)
        _acc_out(db_ref, _colsum(dp), i == 0)

    return R.call(
        body, "xbc_bwd", DX // _XW,
        [R.tile(_XW), R.next(_XW), R.tile(_XW, cb), R.prev(_XW, cb), R.next(_XW, cb),
         R.colvec(4, _XW), R.colvec(1, _XW), ANY],
        [R.tile(_XW, cb), R.colvec(4, _XW), R.colvec(1, _XW)],
        [jax.ShapeDtypeStruct(dproj.shape, BF16), jax.ShapeDtypeStruct((4, DX), F32),
         jax.ShapeDtypeStruct((1, DX), F32)],
        (dact, dact, proj, proj, proj, conv_w, conv_b, dproj), aliases={7: 0})


def _softplus(x):
    return jnp.maximum(x, 0.0) + jnp.log(1.0 + jnp.exp(-jnp.abs(x)))


def _dt_fwd(dt_raw, dt_bias_p, a_log_p):
    T = dt_raw.shape[0]

    def body(r_ref, b_ref, al_ref, dt_ref, ac_ref):
        dt = _softplus(r_ref[...] + b_ref[...])
        s = dt * (-jnp.exp(al_ref[...]))
        row = lax.broadcasted_iota(jnp.int32, (CH, LANES), 0)
        k = 1
        while k < CH:
            s = s + jnp.where(row >= k, pltpu.roll(s, k, 0), 0.0)
            k *= 2
        dt_ref[...] = dt
        ac_ref[...] = s

    blk = pl.BlockSpec((CH, LANES), lambda i: (i, 0))
    vec = pl.BlockSpec((1, LANES), lambda i: (0, 0))
    return pl.pallas_call(
        body, name="dt_fwd", grid=(T // CH,), in_specs=[blk, vec, vec], out_specs=[blk, blk],
        out_shape=[jax.ShapeDtypeStruct((T, LANES), F32)] * 2, compiler_params=_params(("parallel",)),
    )(dt_raw, dt_bias_p, a_log_p)


def _dt_bwd(dacum, ddt_x, dt_raw, dt_bias_p, a_log_p, dproj):
    T = dt_raw.shape[0]
    nc = T // CH

    def body(da_ref, dx_ref, r_ref, b_ref, al_ref, _alias, o_ref, db_ref, dal_ref):
        i = pl.program_id(0)
        a = -jnp.exp(al_ref[...])
        z = r_ref[...] + b_ref[...]
        dt = _softplus(z)
        s = da_ref[...]
        row = lax.broadcasted_iota(jnp.int32, (CH, LANES), 0)
        k = 1
        while k < CH:
            s = s + jnp.where(row < CH - k, pltpu.roll(s, CH - k, 0), 0.0)
            k *= 2
        ddt = s * a + dx_ref[...]
        draw = ddt * _sigmoid(z)
        o_ref[...] = draw.astype(BF16)
        _acc_out(db_ref, _colsum(draw), i == 0)
        _acc_out(dal_ref, _colsum(s * dt), i == 0)

        @pl.when(i == nc - 1)
        def _():
            dal_ref[...] = dal_ref[...] * a

    blk = pl.BlockSpec((CH, LANES), lambda i: (i, 0))
    vec = pl.BlockSpec((1, LANES), lambda i: (0, 0))
    oblk = pl.BlockSpec((CH, LANES), lambda i: (i, OFF_DT // LANES))
    return pl.pallas_call(
        body, name="dt_bwd", grid=(nc,), in_specs=[blk, blk, blk, vec, vec, ANY], out_specs=[oblk, vec, vec],
        out_shape=[jax.ShapeDtypeStruct(dproj.shape, BF16), jax.ShapeDtypeStruct((1, LANES), F32),
                   jax.ShapeDtypeStruct((1, LANES), F32)],
        input_output_aliases={5: 0}, compiler_params=_params(("arbitrary",)),
    )(dacum, ddt_x, dt_raw, dt_bias_p, a_log_p, dproj)


_GW = DI // NG
_HG = NH // NG
_NEG = -1e30


def _pair_lanes(left, v0, v1):
    return jnp.where(left, v0, v1)


def _ssd_specs(T, rev):
    nc = T // CH
    cm = (lambda c: nc - 1 - c) if rev else (lambda c: c)
    bw = NG * NS
    return dict(
        xs=pl.BlockSpec((CH, DI), lambda c: (cm(c), 0)),
        bm=pl.BlockSpec((CH, bw), lambda c: (cm(c), DI // bw)),
        cmat=pl.BlockSpec((CH, bw), lambda c: (cm(c), DI // bw + 1)),
        xbc=pl.BlockSpec((CH, DX), lambda c: (cm(c), 0)),
        col=pl.BlockSpec((NG, CH, _HG), lambda c: (0, cm(c), 0)),
        rowT=pl.BlockSpec((NG, _HG, CH), lambda c: (0, 0, cm(c))),
        dsk=pl.BlockSpec((1, DI), lambda c: (0, 0)),
        state=pl.BlockSpec((1, NS, DI), lambda c: (cm(c), 0, 0)),
    )


def _last(ref, lo, hi):
    return ref.at[(slice(None),) * (len(ref.shape) - 1) + (slice(lo, hi),)]


def _group_views(g, wide, narrow, heads):
    return ([_last(r, g * _GW, (g + 1) * _GW) for r in wide] + [_last(r, g * NS, (g + 1) * NS) for r in narrow]
            + [r.at[g:g + 1] for r in heads])


def _ssd_fwd(xact, dtg, acg, acgT, dsk_rep):
    T = xact.shape[0]
    nc = T // CH
    sp = _ssd_specs(T, False)

    def body(*refs):
        xs, bm, cmat, dtr, acr, actr, dsk, y, spv, S_ref = refs

        @pl.when(pl.program_id(0) == 0)
        def _():
            S_ref[...] = jnp.zeros_like(S_ref)

        for g in range(NG):
            group(*_group_views(g, (xs, dsk, y, spv, S_ref), (bm, cmat), (dtr, acr, actr)))

    def group(xs_ref, dsk_ref, y_ref, sp_ref, S_ref, b_ref, c_ref, dt_ref, ac_ref, acT_ref):
        Bm, Cm = b_ref[...], c_ref[...]
        dt, ac, acT = dt_ref[0], ac_ref[0], acT_ref[0]
        S = S_ref[...]
        sp_ref[0] = S
        cb = _dot(Cm, Bm, NT)
        CS = _dot(Cm, S.astype(BF16))
        row = lax.broadcasted_iota(jnp.int32, (CH, CH), 0)
        col = lax.broadcasted_iota(jnp.int32, (CH, CH), 1)
        tril = row >= col
        left = col < HP
        xd_parts, dec_parts = [], []
        for p in range(_HG // 2):
            sl = slice(p * LANES, (p + 1) * LANES)
            j0, j1 = 2 * p, 2 * p + 1
            xp = xs_ref[:, sl].astype(F32)
            a0, a1 = ac[:, j0:j0 + 1], ac[:, j1:j1 + 1]
            al0, al1 = ac[CH - 1:CH, j0:j0 + 1], ac[CH - 1:CH, j1:j1 + 1]
            X = xp * _pair_lanes(left, dt[:, j0:j0 + 1], dt[:, j1:j1 + 1])
            Xb = X.astype(BF16)
            yd = jnp.zeros((CH, LANES), F32)
            for j, aj, mask in ((j0, a0, left), (j1, a1, jnp.logical_not(left))):
                Lm = jnp.exp(jnp.where(tril, aj - acT[j:j + 1, :], _NEG))
                W = (cb * Lm).astype(BF16)
                yd = yd + _dot(W, jnp.where(mask, Xb, jnp.zeros_like(Xb)))
            eal = _pair_lanes(left, jnp.exp(a0), jnp.exp(a1))
            y = yd + eal * CS[:, sl] + dsk_ref[:, sl] * xp
            y_ref[:, sl] = y.astype(BF16)
            xd_parts.append(X * _pair_lanes(left, jnp.exp(al0 - a0), jnp.exp(al1 - a1)))
            dec_parts.append(_pair_lanes(left[0:1], jnp.exp(al0), jnp.exp(al1)))
        Xd = jnp.concatenate(xd_parts, axis=1).astype(BF16)
        dec = jnp.concatenate(dec_parts, axis=1)
        S_ref[...] = dec * S + _dot(Bm, Xd, TN)

    return pl.pallas_call(
        body, name="ssd_fwd", grid=(nc,),
        in_specs=[sp["xs"], sp["bm"], sp["cmat"], sp["col"], sp["col"], sp["rowT"], sp["dsk"]],
        out_specs=[sp["xs"], sp["state"]],
        out_shape=[jax.ShapeDtypeStruct((T, DI), BF16), jax.ShapeDtypeStruct((nc, NS, DI), F32)],
        scratch_shapes=[pltpu.VMEM((NS, DI), F32)],
        compiler_params=_params(("arbitrary",)),
    )(xact, xact, xact, dtg, acg, acgT, dsk_rep)


def _ssd_bwd(dy, xact, dtg, acg, acgT, dsk_rep, sprev):
    T = xact.shape[0]
    nc = T // CH
    sp = _ssd_specs(T, True)

    def body(*refs):
        xs, bm, cmat, dtr, acr, actr, dsk, dyr, spv, dxa, ddtx, dAc, dskacc, dS_ref = refs
        first = pl.program_id(0) == 0

        @pl.when(first)
        def _():
            dS_ref[...] = jnp.zeros_like(dS_ref)

        dbc = _last(dxa, DI, DX)
        for g in range(NG):
            group(first, *_group_views(g, (xs, dsk, dyr, spv, dxa, dskacc, dS_ref),
                                       (bm, cmat, dbc, _last(dbc, NG * NS, 2 * NG * NS)),
                                       (dtr, acr, actr, ddtx, dAc)))

    def group(first, xs_ref, dsk_ref, dy_ref, sp_ref, dx_ref, dskacc_ref, dS_ref, b_ref, c_ref, dB_ref, dC_ref,
              dt_ref, ac_ref, acT_ref, ddtx_ref, dAc_ref):
        Bm, Cm = b_ref[...], c_ref[...]
        dt, ac, acT = dt_ref[0], ac_ref[0], acT_ref[0]
        S = sp_ref[0]
        dS = dS_ref[...]
        Sb, dSb = S.astype(BF16), dS.astype(BF16)
        cb = _dot(Cm, Bm, NT)
        cbT = _dot(Bm, Cm, NT)
        CmT = Cm.T
        CS = _dot(Cm, Sb)
        T1 = _dot(Bm, dSb)
        row = lax.broadcasted_iota(jnp.int32, (CH, CH), 0)
        col = lax.broadcasted_iota(jnp.int32, (CH, CH), 1)
        tril = row >= col
        triu = row <= col
        left = col < HP
        lane8 = lax.broadcasted_iota(jnp.int32, (1, _HG), 1)
        lastrow = lax.broadcasted_iota(jnp.int32, (CH, 1), 0) == CH - 1
        dCB = jnp.zeros((CH, CH), F32)
        dCBT = jnp.zeros((CH, CH), F32)
        dAc = jnp.zeros((CH, _HG), F32)
        ddtx = jnp.zeros((CH, _HG), F32)
        xd_parts, dye_parts, dec_parts, dsk_parts = [], [], [], []
        for p in range(_HG // 2):
            sl = slice(p * LANES, (p + 1) * LANES)
            j0, j1 = 2 * p, 2 * p + 1
            xp = xs_ref[:, sl].astype(F32)
            dyp = dy_ref[:, sl].astype(F32)
            a0, a1 = ac[:, j0:j0 + 1], ac[:, j1:j1 + 1]
            al0, al1 = ac[CH - 1:CH, j0:j0 + 1], ac[CH - 1:CH, j1:j1 + 1]
            dtl = _pair_lanes(left, dt[:, j0:j0 + 1], dt[:, j1:j1 + 1])
            X = xp * dtl
            Xb = X.astype(BF16)
            eal = _pair_lanes(left, jnp.exp(a0), jnp.exp(a1))
            dtel = _pair_lanes(left, jnp.exp(al0 - a0), jnp.exp(al1 - a1))
            T1p = T1[:, sl]
            dXd = jnp.zeros((CH, LANES), F32)
            Rm = T1p * dtel * X
            GR = dyp * (eal * CS[:, sl]) - Rm
            SdS = dS[:, sl] * S[:, sl]
            for j, aj, alj, mask in ((j0, a0, al0, left), (j1, a1, al1, jnp.logical_not(left))):
                arow = acT[j:j + 1, :]
                Lm = jnp.exp(jnp.where(tril, aj - arow, _NEG))
                LmT = jnp.exp(jnp.where(triu, arow - aj, _NEG))
                dYm = jnp.where(mask, dyp, 0.0).astype(BF16)
                dWm = _dot(dYm, Xb, NT)
                dWmT = _dot(Xb, dYm, NT)
                dCB = dCB + dWm * Lm
                dCBT = dCBT + dWmT * LmT
                WT = cbT * LmT
                dXd = dXd + _dot(WT.astype(BF16), dYm)
                qd = dWm * (cb * Lm) - dWmT * WT
                colv = jnp.sum(qd + jnp.where(mask, GR, 0.0), axis=1, keepdims=True)
                tot = jnp.where(mask, Rm + jnp.exp(alj) * SdS, 0.0)
                dalast = jnp.sum(jnp.sum(tot, axis=0, keepdims=True), axis=1, keepdims=True)
                dAc = dAc + (colv + jnp.where(lastrow, dalast, 0.0)) * (lane8 == j).astype(F32)
            dX = dXd + dtel * T1p
            dXx = dX * xp
            for j, mask in ((j0, left), (j1, jnp.logical_not(left))):
                dd = jnp.sum(jnp.where(mask, dXx, 0.0), axis=1, keepdims=True)
                ddtx = ddtx + dd * (lane8 == j).astype(F32)
            dx_ref[:, sl] = (dX * dtl + dsk_ref[:, sl] * dyp).astype(BF16)
            dsk_parts.append(_colsum(dyp * xp))
            xd_parts.append(X * dtel)
            dye_parts.append(dyp * eal)
            dec_parts.append(_pair_lanes(left[0:1], jnp.exp(al0), jnp.exp(al1)))
        Xd = jnp.concatenate(xd_parts, axis=1).astype(BF16)
        dYe = jnp.concatenate(dye_parts, axis=1).astype(BF16)
        dec = jnp.concatenate(dec_parts, axis=1)
        dC_ref[...] = (_dot(dCB.astype(BF16), Bm) + _dot(dYe, Sb, NT)).astype(BF16)
        dB_ref[...] = (_dot(dCBT.astype(BF16), Cm) + _dot(Xd, dSb, NT)).astype(BF16)
        dS_ref[...] = _dot(CmT, dYe) + dec * dS
        ddtx_ref[0] = ddtx
        dAc_ref[0] = dAc
        _acc_out(dskacc_ref, jnp.concatenate(dsk_parts, axis=1), first)

    return pl.pallas_call(
        body, name="ssd_bwd", grid=(nc,),
        in_specs=[sp["xs"], sp["bm"], sp["cmat"], sp["col"], sp["col"], sp["rowT"], sp["dsk"], sp["xs"],
                  sp["state"]],
        out_specs=[sp["xbc"], sp["col"], sp["col"], sp["dsk"]],
        out_shape=[jax.ShapeDtypeStruct((T, DX), BF16), jax.ShapeDtypeStruct((NG, T, _HG), F32),
                   jax.ShapeDtypeStruct((NG, T, _HG), F32), jax.ShapeDtypeStruct((1, DI), F32)],
        scratch_shapes=[pltpu.VMEM((NS, DI), F32)],
        compiler_params=_params(("arbitrary",)),
    )(xact, xact, xact, dtg, acg, acgT, dsk_rep, dy, sprev)


def _gnorm_fwd(y, proj, w):
    T = y.shape[0]
    R = _Rows(T, 1024)
    zb = OFF_Z // _GW

    def body(y_ref, z_ref, w_ref, o_ref):
        z = z_ref[...].astype(F32)
        yf = y_ref[...].astype(F32) * z * _sigmoid(z)
        r = lax.rsqrt(jnp.mean(yf * yf, axis=-1, keepdims=True) + EPS)
        o_ref[...] = (yf * r * w_ref[...]).astype(BF16)

    return R.call(body, "gnorm_fwd", NG, [R.tile(_GW), R.tile(_GW, zb), R.colvec(1, _GW)], R.tile(_GW),
                  jax.ShapeDtypeStruct((T, DI), BF16), (y, proj, w))


def _gnorm_bwd(dn, y, proj, w, dproj):
    T = y.shape[0]
    R = _Rows(T, 1024)
    zb = OFF_Z // _GW

    def body(dn_ref, y_ref, z_ref, w_ref, _alias, dz_ref, dy_ref, dw_ref):
        z = z_ref[...].astype(F32)
        yv = y_ref[...].astype(F32)
        s = _sigmoid(z)
        silu = z * s
        yf = yv * silu
        r = lax.rsqrt(jnp.mean(yf * yf, axis=-1, keepdims=True) + EPS)
        yh = yf * r
        dnv = dn_ref[...].astype(F32)
        dyh = dnv * w_ref[...]
        dyf = r * (dyh - yh * jnp.mean(dyh * yh, axis=-1, keepdims=True))
        dy_ref[...] = (dyf * silu).astype(BF16)
        dz_ref[...] = (dyf * yv * s * (1.0 + z * (1.0 - s))).astype(BF16)
        _acc_out(dw_ref, _colsum(dnv * yh), pl.program_id(1) == 0)

    return R.call(
        body, "gnorm_bwd", NG, [R.tile(_GW), R.tile(_GW), R.tile(_GW, zb), R.colvec(1, _GW), ANY],
        [R.tile(_GW, zb), R.tile(_GW), R.colvec(1, _GW)],
        [jax.ShapeDtypeStruct(dproj.shape, BF16), jax.ShapeDtypeStruct((T, DI), BF16),
         jax.ShapeDtypeStruct((1, DI), F32)],
        (dn, y, proj, w, dproj), aliases={4: 0})


def _merge_fwd(proj, ya, ys):
    T = proj.shape[0]
    R = _Rows(T, 512)
    gb = OFF_G // (2 * D)

    def body(g_ref, ya_ref, ys_ref, o_ref):
        ga = _sigmoid(g_ref[:, :D].astype(F32))
        gs = _sigmoid(g_ref[:, D:].astype(F32))
        o_ref[...] = (ga * ya_ref[...].astype(F32) + gs * ys_ref[...].astype(F32)).astype(BF16)

    return R.call(body, "merge_fwd", 1, [R.tile(2 * D, gb), R.tile(D), R.tile(D)], R.tile(D),
                  jax.ShapeDtypeStruct((T, D), BF16), (proj, ya, ys))


def _merge_bwd(dm, proj, ya, ys, ncols):
    T = proj.shape[0]
    R = _Rows(T, 256)
    gb = OFF_G // (2 * D)

    def body(dm_ref, g_ref, ya_ref, ys_ref, dg_ref, dya_ref, dys_ref):
        d = dm_ref[...].astype(F32)
        ga = _sigmoid(g_ref[:, :D].astype(F32))
        gs = _sigmoid(g_ref[:, D:].astype(F32))
        dya_ref[...] = (d * ga).astype(BF16)
        dys_ref[...] = (d * gs).astype(BF16)
        dg_ref[:, :D] = (d * ya_ref[...].astype(F32) * ga * (1.0 - ga)).astype(BF16)
        dg_ref[:, D:] = (d * ys_ref[...].astype(F32) * gs * (1.0 - gs)).astype(BF16)

    return R.call(
        body, "merge_bwd", 1, [R.tile(D), R.tile(2 * D, gb), R.tile(D), R.tile(D)],
        [R.tile(2 * D, gb), R.tile(D), R.tile(D)],
        [jax.ShapeDtypeStruct((T, ncols), BF16), jax.ShapeDtypeStruct((T, D), BF16),
         jax.ShapeDtypeStruct((T, D), BF16)],
        (dm, proj, ya, ys))


_FW = 1408
_FB = FF // _FW


def _ffn_act_fwd(hv, conv_w, conv_b):
    T = hv.shape[0]
    R = _Rows(T, 256)
    tm = R.tm

    def body(h1_ref, h1p_ref, h3_ref, w_ref, b_ref, o_ref):
        keep = (pl.program_id(1) > 0).astype(F32)
        ext = jnp.concatenate([h1p_ref[...].astype(F32) * keep, h1_ref[...].astype(F32)], axis=0)
        pre = _wsum(w_ref[...], _shifts_causal(ext, 3, tm)) + b_ref[...]
        o_ref[...] = (pre * _sigmoid(pre) * h3_ref[...].astype(F32)).astype(BF16)

    return R.call(body, "ffn_act_fwd", _FB,
                  [R.tile(_FW), R.prev(_FW), R.tile(_FW, _FB), R.colvec(3, _FW), R.colvec(1, _FW)],
                  R.tile(_FW), jax.ShapeDtypeStruct((T, FF), BF16), (hv, hv, hv, conv_w, conv_b))


def _ffn_act_bwd(dg, hv, conv_w, conv_b):
    T = hv.shape[0]
    R = _Rows(T, 256)
    tm = R.tm

    def body(dg_ref, h1_ref, h1p_ref, h3_ref, w_ref, b_ref, dh3_ref, dpre_ref, dw_ref, db_ref):
        i = pl.program_id(1)
        keep = (i > 0).astype(F32)
        ext = jnp.concatenate([h1p_ref[...].astype(F32) * keep, h1_ref[...].astype(F32)], axis=0)
        sh = _shifts_causal(ext, 3, tm)
        pre = _wsum(w_ref[...], sh) + b_ref[...]
        s = _sigmoid(pre)
        d = dg_ref[...].astype(F32)
        dh3_ref[...] = (d * pre * s).astype(BF16)
        dpre = d * h3_ref[...].astype(F32) * s * (1.0 + pre * (1.0 - s))
        dpre_ref[...] = dpre.astype(BF16)
        _acc_rows(dw_ref, [_colsum(dpre * q) for q in sh], i == 0)
        _acc_out(db_ref, _colsum(dpre), i == 0)

    return R.call(
        body, "ffn_act_bwd", _FB,
        [R.tile(_FW), R.tile(_FW), R.prev(_FW), R.tile(_FW, _FB), R.colvec(3, _FW), R.colvec(1, _FW)],
        [R.tile(_FW), R.tile(_FW), R.colvec(3, _FW), R.colvec(1, _FW)],
        [jax.ShapeDtypeStruct((T, FF), BF16), jax.ShapeDtypeStruct((T, FF), BF16),
         jax.ShapeDtypeStruct((3, FF), F32), jax.ShapeDtypeStruct((1, FF), F32)],
        (dg, hv, hv, hv, conv_w, conv_b))


def _conv3_transpose(dpre, conv_w):
    T = dpre.shape[0]
    R = _Rows(T, 256)
    tm = R.tm

    def body(d_ref, dn_ref, w_ref, o_ref):
        keep = (pl.program_id(1) < R.nrow - 1).astype(F32)
        ext = jnp.concatenate([d_ref[...].astype(F32), dn_ref[...].astype(F32) * keep], axis=0)
        o_ref[...] = _wsum(w_ref[...], _shifts_anticausal(ext, 3, tm)).astype(BF16)

    return R.call(body, "ffn_conv_bwd", _FB, [R.tile(_FW), R.next(_FW), R.colvec(3, _FW)], R.tile(_FW),
                  jax.ShapeDtypeStruct((T, FF), BF16), (dpre, dpre, conv_w))


def _final_loss(h, w, target):
    T = h.shape[0]
    R = _Rows(T, 512)

    def body(h_ref, w_ref, t_ref, l_ref, dh_ref, dhb_ref, dw_ref):
        first = pl.program_id(1) == 0
        xv = h_ref[...]
        wv = w_ref[...]
        r = lax.rsqrt(jnp.mean(xv * xv, axis=-1, keepdims=True) + EPS)
        xh = xv * r
        err = xh * wv - t_ref[...]
        part = 0.5 * jnp.sum(jnp.mean(err * err, axis=-1, keepdims=True), axis=0, keepdims=True)
        _acc_out(l_ref, jnp.broadcast_to(part, l_ref.shape), first)
        dy = err * (1.0 / D)
        dxh = dy * wv
        dh = r * (dxh - xh * jnp.mean(dxh * xh, axis=-1, keepdims=True))
        dh_ref[...] = dh
        dhb_ref[...] = dh.astype(BF16)
        _acc_out(dw_ref, _colsum(dy * xh), first)

    return R.call(body, "final_loss", 1, [R.tile(D), R.colvec(1, D), R.tile(D)],
                  [R.colvec(8, LANES), R.tile(D), R.tile(D), R.colvec(1, D)],
                  [jax.ShapeDtypeStruct((8, LANES), F32), jax.ShapeDtypeStruct((T, D), F32),
                   jax.ShapeDtypeStruct((T, D), BF16), jax.ShapeDtypeStruct((1, D), F32)], (h, w, target))


def _pad_lanes(v, n=LANES):
    return jnp.pad(v, ((0, 0), (0, n - v.shape[1])))


def _group_cols(a):
    T = a.shape[0]
    return a[:, :NH].reshape(T, NG, _HG).transpose(1, 0, 2)


class _Hooks:
    def after_norm(self, u):
        return u

    def late_weights(self, wts, after):
        return wts

    def grads_ready(self, grads, tie):
        return tie

    def mark(self, name, value):
        pass


def _local_step(x, target, wts, hooks=None):
    hooks = hooks or _Hooks()
    T = x.shape[0]
    w_in = wts["w_in"]
    dt_bias_p, a_log_p = _pad_lanes(wts["dt_bias"]), _pad_lanes(wts["a_log"])
    dsk_rep = jnp.repeat(wts["d_skip"], HP, axis=1)

    u = hooks.after_norm(_rmsnorm_fwd(x, wts["norm_mix_w"], "norm_mix_fwd"))
    proj = _matmul(u, w_in, mode="nn", out_dtype=BF16, name="mm_in")
    dt_raw = _matmul(u, w_in[:, OFF_DT:], mode="nn", out_dtype=F32, name="mm_dt")
    ya_in = _branch_a_fwd(proj, wts["conv_a_w"])
    xact = _xbc_fwd(proj, wts["ssd_conv_w"], wts["ssd_conv_b"])
    dt, acum = _dt_fwd(dt_raw, dt_bias_p, a_log_p)
    dtg, acg = _group_cols(dt), _group_cols(acum)
    acgT = acg.transpose(0, 2, 1)
    y_ssd, sprev = _ssd_fwd(xact, dtg, acg, acgT, dsk_rep)
    yn = _gnorm_fwd(y_ssd, proj, wts["ssd_norm_w"])
    late = hooks.late_weights(wts, yn)
    w_a_out, w_s_out, w_o, w_up, w_down = (late[k] for k in ("w_a_out", "w_s_out", "w_o", "w_up", "w_down"))
    y_a = _matmul(ya_in, w_a_out, mode="nn", out_dtype=BF16, name="mm_a_out")
    y_s = _matmul(yn, w_s_out, mode="nn", out_dtype=BF16, name="mm_s_out")
    merged = _merge_fwd(proj, y_a, y_s)
    h1 = _matmul(merged, w_o, mode="nn", out_dtype=F32, name="mm_o", residual=x)
    v = _rmsnorm_fwd(h1, wts["norm_ffn_w"], "norm_ffn_fwd")
    hv = _matmul(v, w_up, mode="nn", out_dtype=BF16, name="mm_up")
    gact = _ffn_act_fwd(hv, wts["ffn_conv_w"], wts["ffn_conv_b"])
    h2 = _matmul(gact, w_down, mode="nn", out_dtype=F32, name="mm_down", residual=h1)
    loss, dh2, dh2b, g_final = _final_loss(h2, wts["final_norm_w"], target)

    grads = {"final_norm_w": g_final}
    grads["w_down"] = _matmul(gact, dh2b, mode="tn", out_dtype=F32, name="mm_down_dw")
    dgact = _matmul(dh2b, w_down, mode="nt", out_dtype=BF16, name="mm_down_dx")
    dh3, dpre, grads["ffn_conv_w"], grads["ffn_conv_b"] = _ffn_act_bwd(dgact, hv, wts["ffn_conv_w"], wts["ffn_conv_b"])
    dh1c = _conv3_transpose(dpre, wts["ffn_conv_w"])
    grads["w_up"] = jnp.concatenate(
        [_matmul(v, dh1c, mode="tn", out_dtype=F32, name="mm_up_dw1"),
         _matmul(v, dh3, mode="tn", out_dtype=F32, name="mm_up_dw3")], axis=1)
    dv = _matmul(dh1c, w_up[:, :FF], mode="nt", out_dtype=F32, name="mm_up_dx1")
    dv = _matmul(dh3, w_up[:, FF:], mode="nt", out_dtype=F32, name="mm_up_dx3", residual=dv)
    dh1, dh1b, grads["norm_ffn_w"] = _rmsnorm_bwd(dv, h1, wts["norm_ffn_w"], dh2, "norm_ffn_bwd")
    grads["w_o"] = _matmul(merged, dh1b, mode="tn", out_dtype=F32, name="mm_o_dw")
    dmerged = _matmul(dh1b, w_o, mode="nt", out_dtype=BF16, name="mm_o_dx")
    dproj, dya, dys = _merge_bwd(dmerged, proj, y_a, y_s, NIP)
    grads["w_a_out"] = _matmul(ya_in, dya, mode="tn", out_dtype=F32, name="mm_a_out_dw")
    dya_in = _matmul(dya, w_a_out, mode="nt", out_dtype=BF16, name="mm_a_out_dx")
    dproj, grads["conv_a_w"] = _branch_a_bwd(dya_in, proj, wts["conv_a_w"], dproj)
    grads["w_s_out"] = _matmul(yn, dys, mode="tn", out_dtype=F32, name="mm_s_out_dw")
    dys = hooks.grads_ready({k: grads[k] for k in ("w_a_out", "w_s_out", "w_o", "w_up", "w_down")}, dys)
    dyn =_matmul(dys, w_s_out, mode="nt", out_dtype=BF16, name="mm_s_out_dx")
    dproj, dy_ssd, grads["ssd_norm_w"] = _gnorm_bwd(dyn, y_ssd, proj, wts["ssd_norm_w"], dproj)
    dxact, ddtx, dAc, dskl = _ssd_bwd(dy_ssd, xact, dtg, acg, acgT, dsk_rep, sprev)
    hooks.mark("ssd_bwd", dxact)
    grads["d_skip"] = dskl.reshape(NH, HP).sum(axis=1).reshape(1, NH)
    dproj, grads["ssd_conv_w"], grads["ssd_conv_b"] = _xbc_bwd(dxact, proj, wts["ssd_conv_w"], wts["ssd_conv_b"], dproj)
    dacum = dAc.transpose(1, 0, 2).reshape(T, NH)
    ddt_x = ddtx.transpose(1, 0, 2).reshape(T, NH)
    dproj, g_dtb, g_alog = _dt_bwd(_pad_lanes(dacum), _pad_lanes(ddt_x), dt_raw, dt_bias_p, a_log_p, dproj)
    grads["dt_bias"], grads["a_log"] = g_dtb[:, :NH], g_alog[:, :NH]
    grads["w_in"] = _matmul(u, dproj, mode="tn", out_dtype=F32, name="mm_in_dw")
    dproj = hooks.grads_ready({"w_in": grads["w_in"]}, dproj)
    du =_matmul(dproj, w_in, mode="nt", out_dtype=F32, name="mm_in_dx")
    grad_x, _, grads["norm_mix_w"] = _rmsnorm_bwd(du, x, wts["norm_mix_w"], dh1, "norm_mix_bwd")
    return loss, grad_x, grads


def _permute_w_in(w):
    out = jnp.zeros((w.shape[0], NIP), w.dtype)
    for o, n, no in _SEGS:
        out = lax.dynamic_update_slice(out, w[:, o:o + n], (0, no))
    return out


def _unpermute_w_in(g):
    order = sorted(_SEGS)
    return jnp.concatenate([g[:, no:no + n] for o, n, no in order], axis=1)


MESH = pl.DeviceIdType.MESH
NCHIP = 4
NDEV = 8

_W_IN = (("w_in", D, NI // NCHIP, 1),)
_W_REST = (("w_a_out", D // NCHIP, D, 0), ("w_s_out", DI // NCHIP, D, 0), ("w_o", D // NCHIP, D, 0),
           ("w_up", D, 2 * FF // NCHIP, 1), ("w_down", FF // NCHIP, D, 0))


def _slab_rows(group):
    rows = [r * c // LANES for _, r, c, _ in group]
    assert all(n % 32 == 0 for n in rows), rows
    return rows


def _coords():
    return lax.axis_index("x"), lax.axis_index("y"), lax.axis_index("c")


def _other_chips(x, y):
    return [(1 - x, y), (x, 1 - y), (1 - x, 1 - y)]


def _ag_weights(shard):
    nrows = shard.shape[0]
    hr = nrows // 2

    def body(x_ref, out_ref, send_sems, recv_sems, local_sem):
        x, y, c = _coords()
        me = 2 * x + y
        chips = _other_chips(x, y)

        def rows(s, h):
            return out_ref.at[s, pl.ds(h * hr, hr), :]

        def copy(k, s, h, to, src=None):
            return pltpu.make_async_remote_copy(
                src_ref=rows(s, h) if src is None else src, dst_ref=rows(s, h),
                send_sem=send_sems.at[k], recv_sem=recv_sems.at[k], device_id=to, device_id_type=MESH)

        mine = pltpu.make_async_copy(x_ref, out_ref.at[me], local_sem)
        mine.start()
        first = [copy(k, me, c, (*chip, c), src=x_ref.at[pl.ds(c * hr, hr), :]) for k, chip in enumerate(chips)]
        for cp in first:
            cp.start()
        passed = []
        for k, chip in enumerate(chips):
            s = 2 * chip[0] + chip[1]
            copy(k, s, c, (x, y, c)).wait_recv()
            fwd = copy(3 + k, s, c, (x, y, 1 - c))
            fwd.start()
            passed.append(fwd)
        for k, chip in enumerate(chips):
            copy(3 + k, 2 * chip[0] + chip[1], 1 - c, (x, y, c)).wait_recv()
        for cp in first + passed:
            cp.wait_send()
        mine.wait()

    return pl.pallas_call(
        body, name="ag_weights", in_specs=[ANY], out_specs=ANY,
        out_shape=jax.ShapeDtypeStruct((NCHIP, nrows, LANES), shard.dtype),
        scratch_shapes=[pltpu.SemaphoreType.DMA((6,)), pltpu.SemaphoreType.DMA((6,)), pltpu.SemaphoreType.DMA],
        compiler_params=pltpu.CompilerParams(has_side_effects=True),
    )(shard)


HBM = pl.BlockSpec(memory_space=pltpu.HBM)
SEM = pl.BlockSpec(memory_space=pltpu.SEMAPHORE)
_EFFECT = pltpu.SideEffectType.DATAFLOW_SIDE_EFFECTING
_NCOPY = NCHIP - 1


def _plan_bcast(src_ref, land_ref, send_sems, recv_sems):
    x, y, c = _coords()
    sends, lands = [], []
    for k, chip in enumerate(_other_chips(x, y)):
        def copy(slot):
            return pltpu.make_async_remote_copy(
                src_ref=src_ref, dst_ref=land_ref.at[slot], send_sem=send_sems.at[k], recv_sem=recv_sems.at[k],
                device_id=(*chip, c), device_id_type=MESH)
        sends.append(copy(2 * x + y))
        lands.append(copy(2 * chip[0] + chip[1]))
    return sends, lands


def _plan_scatter(src_ref, land_ref, send_sems, recv_sems):
    x, y, c = _coords()
    cps = [pltpu.make_async_remote_copy(
        src_ref=src_ref.at[2 * chip[0] + chip[1]], dst_ref=land_ref.at[k], send_sem=send_sems.at[k],
        recv_sem=recv_sems.at[k], device_id=(*chip, c), device_id_type=MESH)
        for k, chip in enumerate(_other_chips(x, y))]
    return cps, cps


def _split_start(name, src, land, plan):
    def body(src_ref, land_ref, send_sems, recv_sems, src_thru, land_thru, token):
        for cp in plan(src_ref, land_ref, send_sems, recv_sems)[0]:
            cp.start()
        token[...] = jnp.zeros_like(token)

    send_sems, recv_sems, src_thru, land_thru, token = pl.pallas_call(
        body, name=name,
        out_shape=(pltpu.SemaphoreType.DMA((_NCOPY,)), pltpu.SemaphoreType.DMA((_NCOPY,)),
                   pltpu.HBM(src.shape, src.dtype), pltpu.HBM(land.shape, land.dtype),
                   jax.ShapeDtypeStruct((8, LANES), F32)),
        in_specs=(HBM, HBM), out_specs=(SEM, SEM, HBM, HBM, pl.BlockSpec(memory_space=pltpu.VMEM)),
        input_output_aliases={0: 2, 1: 3},
        compiler_params=pltpu.CompilerParams(has_side_effects=_EFFECT),
    )(pltpu.with_memory_space_constraint(src, pltpu.HBM), pltpu.with_memory_space_constraint(land, pltpu.HBM))
    return (send_sems, recv_sems, src_thru, land_thru), token


def _split_wait(name, handle, after, plan):
    send_sems, recv_sems, src_thru, land_thru = handle

    def body(src_ref, land_ref, send_sems, recv_sems, after_ref, src_out, land_out):
        sends, lands = plan(src_ref, land_ref, send_sems, recv_sems)
        for cp in sends:
            cp.wait_send()
        for cp in lands:
            cp.wait_recv()

    return pl.pallas_call(
        body, name=name,
        out_shape=(pltpu.HBM(src_thru.shape, src_thru.dtype), pltpu.HBM(land_thru.shape, land_thru.dtype)),
        in_specs=(HBM, HBM, SEM, SEM, ANY), out_specs=(HBM, HBM), input_output_aliases={0: 0, 1: 1},
        compiler_params=pltpu.CompilerParams(has_side_effects=_EFFECT),
    )(src_thru, land_thru, send_sems, recv_sems, after)


def _tie(x, token, name):
    def body(x_ref, t_ref, o_ref):
        pass

    return pl.pallas_call(
        body, name=name, in_specs=[ANY, pl.BlockSpec(memory_space=pltpu.VMEM)], out_specs=ANY,
        out_shape=jax.ShapeDtypeStruct(x.shape, x.dtype), input_output_aliases={0: 0},
    )(x, token)


def _swap_sibling(p, name):
    def body(p_ref, land_ref, send_sem, recv_sem):
        x, y, c = _coords()
        cp = pltpu.make_async_remote_copy(
            src_ref=p_ref, dst_ref=land_ref, send_sem=send_sem, recv_sem=recv_sem,
            device_id=(x, y, 1 - c), device_id_type=MESH)
        cp.start()
        cp.wait()

    return pl.pallas_call(
        body, name=name, in_specs=[ANY], out_specs=ANY, out_shape=jax.ShapeDtypeStruct(p.shape, p.dtype),
        scratch_shapes=[pltpu.SemaphoreType.DMA, pltpu.SemaphoreType.DMA],
        compiler_params=pltpu.CompilerParams(has_side_effects=True),
    )(p)


_ADD_BYTES = 7 << 19


def _add_tile(rows):
    best = 32
    for t in range(32, rows + 1, 32):
        if rows % t == 0 and t * LANES * 4 <= _ADD_BYTES:
            best = t
    return best


def _add_slabs(pack, land, me, name):
    rows = pack.shape[1]
    tr = _add_tile(rows)

    def body(me_ref, p_ref, l_ref, o_ref):
        f = lambda r: r.astype(F32)
        o_ref[...] = ((f(p_ref[0]) + f(l_ref[0])) + f(l_ref[1])) + f(l_ref[2])

    return pl.pallas_call(
        body, name=name,
        grid_spec=pltpu.PrefetchScalarGridSpec(
            num_scalar_prefetch=1, grid=(rows // tr,),
            in_specs=[pl.BlockSpec((1, tr, LANES), lambda i, me_ref: (me_ref[0], i, 0)),
                      pl.BlockSpec((_NCOPY, tr, LANES), lambda i, me_ref: (0, i, 0))],
            out_specs=pl.BlockSpec((tr, LANES), lambda i, me_ref: (i, 0))),
        out_shape=jax.ShapeDtypeStruct((rows, LANES), F32),
        compiler_params=_params(("parallel",)),
    )(me, pack, land)


def _add_pair(a, b, name):
    rows = a.shape[0]
    tr = _add_tile(rows)

    def body(a_ref, b_ref, o_ref):
        o_ref[...] = a_ref[...] + b_ref[...]

    blk = pl.BlockSpec((tr, LANES), lambda i: (i, 0))
    return pl.pallas_call(
        body, name=name, grid=(rows // tr,), in_specs=[blk, blk], out_specs=blk,
        out_shape=jax.ShapeDtypeStruct((rows, LANES), F32), compiler_params=_params(("parallel",)),
    )(a, b)


def _gather8(v, reduce, name):
    rows = v.shape[0]

    def body(v_ref, o_ref, buf, send_sems, recv_sems):
        x, y, c = _coords()
        me = 4 * x + 2 * y + c
        buf[pl.ds(me, 1)] = v_ref[...][None]
        cps, lands = [], []
        for k in range(1, NDEV):
            peer = (1 - x if k & 4 else x, 1 - y if k & 2 else y, 1 - c if k & 1 else c)

            def copy(slot):
                return pltpu.make_async_remote_copy(
                    src_ref=v_ref, dst_ref=buf.at[slot], send_sem=send_sems.at[k - 1],
                    recv_sem=recv_sems.at[k - 1], device_id=peer, device_id_type=MESH)

            cps.append(copy(me))
            lands.append(copy(4 * peer[0] + 2 * peer[1] + peer[2]))
        for cp in cps:
            cp.start()
        for cp, land in zip(cps, lands):
            land.wait_recv()
            cp.wait_send()
        if reduce:
            acc = buf[0]
            for d in range(1, NDEV):
                acc = acc + buf[d]
            o_ref[...] = acc
        else:
            o_ref[...] = buf[...]

    vm = pl.BlockSpec(memory_space=pltpu.VMEM)
    out_shape = (rows, LANES) if reduce else (NDEV, rows, LANES)
    return pl.pallas_call(
        body, name=name, in_specs=[vm], out_specs=vm, out_shape=jax.ShapeDtypeStruct(out_shape, F32),
        scratch_shapes=[pltpu.VMEM((NDEV, rows, LANES), F32), pltpu.SemaphoreType.DMA((NDEV - 1,)),
                        pltpu.SemaphoreType.DMA((NDEV - 1,))],
        compiler_params=pltpu.CompilerParams(has_side_effects=True),
    )(v)


def _adamw(w, g, m, v, name):
    rows, cols = w.shape
    tr = rows
    while tr * cols * 4 > (3 << 19) and tr % 16 == 0:
        tr //= 2
    c1 = 1.0 / (1.0 - ADAM_B1 ** ADAM_STEP)
    c2 = 1.0 / (1.0 - ADAM_B2 ** ADAM_STEP)

    def body(w_ref, g_ref, m_ref, v_ref, d_ref, mo_ref, vo_ref):
        gv = g_ref[...]
        mn = ADAM_B1 * m_ref[...] + (1.0 - ADAM_B1) * gv
        vn = ADAM_B2 * v_ref[...] + (1.0 - ADAM_B2) * (gv * gv)
        d_ref[...] = -ADAM_LR * ((mn * c1) / (jnp.sqrt(vn * c2) + ADAM_EPS) + ADAM_WD * w_ref[...])
        mo_ref[...] = mn
        vo_ref[...] = vn

    blk = pl.BlockSpec((tr, cols), lambda i: (i, 0))
    return pl.pallas_call(
        body, name=name, grid=(rows // tr,), in_specs=[blk] * 4, out_specs=[blk] * 3,
        out_shape=[jax.ShapeDtypeStruct((rows, cols), F32)] * 3, compiler_params=_params(("parallel",)),
    )(w, g, m, v)


def _rows128(a, mult=8):
    flat = a.reshape(-1)
    n = -(-flat.shape[0] // (LANES * mult)) * LANES * mult
    return jnp.pad(flat, (0, n - flat.shape[0])).reshape(-1, LANES)


def _pack_rows(parts, total_rows):
    rows = sum(p.shape[0] for p in parts)
    if total_rows > rows:
        parts = list(parts) + [jnp.zeros((total_rows - rows, LANES), parts[0].dtype)]
    return jnp.concatenate(parts, axis=0)


def _unpack_rows(pack, shapes, mult=8):
    out, r = [], 0
    for shp in shapes:
        n = int(np.prod(shp))
        nr = -(-n // (LANES * mult)) * mult
        out.append(pack[r:r + nr].reshape(-1)[:n].reshape(shp))
        r += nr
    return out


def _unpack_full(full, group):
    out, r = {}, 0
    for (name, rr, cc, axis), nr in zip(group, _slab_rows(group)):
        seg = full[:, r:r + nr].reshape(NCHIP, rr, cc)
        out[name] = seg.reshape(NCHIP * rr, cc) if axis == 0 else seg.transpose(1, 0, 2).reshape(rr, NCHIP * cc)
        r += nr
    return out


def _pack_by_chip(grads, group, dtype):
    parts = []
    for (name, rr, cc, axis), nr in zip(group, _slab_rows(group)):
        g = grads[name].astype(dtype)
        seg = g.reshape(NCHIP, rr, cc) if axis == 0 else g.reshape(rr, NCHIP, cc).transpose(1, 0, 2)
        parts.append(seg.reshape(NCHIP, nr, LANES))
    return parts[0] if len(parts) == 1 else jnp.concatenate(parts, axis=1)


_SMALL_REPL = ("norm_mix_w", "ssd_conv_b", "dt_bias", "a_log", "d_skip", "ssd_norm_w", "norm_ffn_w",
               "ffn_conv_b", "final_norm_w")
_SMALL_CONV = (("conv_a_w", 3, D), ("ssd_conv_w", 4, DX), ("ffn_conv_w", 3, FF))


def kernel(x, norm_mix_w, w_in, conv_a_w, w_a_out, ssd_conv_w, ssd_conv_b, dt_bias, a_log, d_skip, ssd_norm_w, w_s_out, w_o, norm_ffn_w, w_up, ffn_conv_w, ffn_conv_b, w_down, final_norm_w, loss_target, m_norm_mix_w, m_w_in, m_conv_a_w, m_w_a_out, m_ssd_conv_w, m_ssd_conv_b, m_dt_bias, m_a_log, m_d_skip, m_ssd_norm_w, m_w_s_out, m_w_o, m_norm_ffn_w, m_w_up, m_ffn_conv_w, m_ffn_conv_b, m_w_down, m_final_norm_w, v_norm_mix_w, v_w_in, v_conv_a_w, v_w_a_out, v_ssd_conv_w, v_ssd_conv_b, v_dt_bias, v_a_log, v_d_skip, v_ssd_norm_w, v_w_s_out, v_w_o, v_norm_ffn_w, v_w_up, v_ffn_conv_w, v_ffn_conv_b, v_w_down, v_final_norm_w):
    names = ("norm_mix_w", "w_in", "conv_a_w", "w_a_out", "ssd_conv_w", "ssd_conv_b", "dt_bias", "a_log", "d_skip",
             "ssd_norm_w", "w_s_out", "w_o", "norm_ffn_w", "w_up", "ffn_conv_w", "ffn_conv_b", "w_down", "final_norm_w")
    W = dict(zip(names, (norm_mix_w, w_in, conv_a_w, w_a_out, ssd_conv_w, ssd_conv_b, dt_bias, a_log, d_skip,
                         ssd_norm_w, w_s_out, w_o, norm_ffn_w, w_up, ffn_conv_w, ffn_conv_b, w_down, final_norm_w)))
    M = dict(zip(names, (m_norm_mix_w, m_w_in, m_conv_a_w, m_w_a_out, m_ssd_conv_w, m_ssd_conv_b, m_dt_bias, m_a_log,
                         m_d_skip, m_ssd_norm_w, m_w_s_out, m_w_o, m_norm_ffn_w, m_w_up, m_ffn_conv_w, m_ffn_conv_b,
                         m_w_down, m_final_norm_w)))
    V = dict(zip(names, (v_norm_mix_w, v_w_in, v_conv_a_w, v_w_a_out, v_ssd_conv_w, v_ssd_conv_b, v_dt_bias, v_a_log,
                         v_d_skip, v_ssd_norm_w, v_w_s_out, v_w_o, v_norm_ffn_w, v_w_up, v_ffn_conv_w, v_ffn_conv_b,
                         v_w_down, v_final_norm_w)))
    two_d = lambda a: a.reshape(-1, a.shape[-1])
    W2, M2, V2 = ({k: two_d(a) for k, a in t.items()} for t in (W, M, V))
    xi, yi, ci = _coords()
    me = 2 * xi + yi

    meidx = me.reshape(1).astype(jnp.int32)
    state = {}

    def slab(group):
        return _pack_rows([_rows128(W2[n], 16) for n, *_ in group], 0).astype(BF16)

    class Hooks(_Hooks):
        def after_norm(self, u):
            return _tie(u, state["rest_token"], "tie_ag_rest")

        def late_weights(self, wts, after):
            own, land = _split_wait("ag_rest_wait", state["rest"], after, _plan_bcast)
            full = _unpack_full(lax.dynamic_update_slice(land, own[None], (me, 0, 0)), _W_REST)
            return {**wts, **full}

        def grads_ready(self, grads, tie):
            if "w_in" in grads:
                key, group, grads = "g_in", _W_IN, {"w_in": _unpermute_w_in(grads["w_in"])}
            else:
                key, group = "g_rest", _W_REST
            pack = _pack_by_chip(grads, group, BF16)
            land = lax.empty((_NCOPY,) + pack.shape[1:], BF16)
            state[key], token = _split_start("rs_" + key + "_start", pack, land, _plan_scatter)
            return _tie(tie, token, "tie_" + key)

        def mark(self, name, value):
            state[name] = value

    def reduced(key, after):
        pack, land = _split_wait("rs_" + key + "_wait", state[key], after, _plan_scatter)
        mine = _add_slabs(pack, land, meidx, "rs_" + key + "_add_chips")
        return _add_pair(mine, _swap_sibling(mine, "rs_" + key + "_swap"), "rs_" + key + "_add_cores")

    w_in_full = _unpack_full(_ag_weights(slab(_W_IN)), _W_IN)["w_in"]
    rest_slab = slab(_W_REST)
    state["rest"], state["rest_token"] = _split_start(
        "ag_rest_start", rest_slab, lax.empty((NCHIP,) + rest_slab.shape, BF16), _plan_bcast)
    conv_shards = _pack_rows([_rows128(W2[n]) for n, *_ in _SMALL_CONV], 0)
    conv_all = _gather8(conv_shards, False, "ag_conv_weights")[0::2]
    wts = {k: W2[k] for k in _SMALL_REPL}
    r = 0
    for n, kk, width in _SMALL_CONV:
        cw = width // NCHIP
        nr = -(-kk * cw // (LANES * 8)) * 8
        wts[n] = conv_all[:, r:r + nr].reshape(NCHIP, -1)[:, :kk * cw].reshape(NCHIP, kk, cw).transpose(1, 0, 2).reshape(kk, width)
        r += nr
    wts["w_in"] = _permute_w_in(w_in_full)

    loss8, grad_x, grads = _local_step(x[0], loss_target[0], wts, Hooks())

    gbig = {}
    for key, group, after in (("g_rest", _W_REST, state["ssd_bwd"]), ("g_in", _W_IN, grad_x)):
        shapes = [(rr, cc) for _, rr, cc, _ in group]
        gbig.update(zip([n for n, *_ in group], _unpack_rows(reduced(key, after), shapes, 16)))

    small_shapes = [W2[n].shape for n in _SMALL_REPL] + [(1, LANES)] + [(kk, width) for _, kk, width in _SMALL_CONV]
    small_parts = [grads[n] for n in _SMALL_REPL] + [loss8[0:1]] + [grads[n] for n, *_ in _SMALL_CONV]
    small = _gather8(_pack_rows([_rows128(p) for p in small_parts], 0), True, "allreduce_small")
    small_g = _unpack_rows(small, small_shapes)
    gsm = dict(zip(_SMALL_REPL, small_g[:len(_SMALL_REPL)]))
    loss = small_g[len(_SMALL_REPL)][0, 0]
    for (n, kk, width), gfull in zip(_SMALL_CONV, small_g[len(_SMALL_REPL) + 1:]):
        cw = width // NCHIP
        gsm[n] = lax.dynamic_slice(gfull, (0, me * cw), (kk, cw))

    G, DW, NM, NV = {}, {}, {}, {}
    for n in [b[0] for b in _W_IN + _W_REST]:
        G[n] = gbig[n]
        DW[n], NM[n], NV[n] = _adamw(W2[n], G[n], M2[n], V2[n], "adamw_" + n)
    sm_names = list(_SMALL_REPL) + [n for n, *_ in _SMALL_CONV]
    sm_shapes = [W2[n].shape for n in sm_names]
    packs = [_pack_rows([_rows128(t[n]) for n in sm_names], 0) for t in (W2, gsm, M2, V2)]
    outs = _adamw(*packs, "adamw_small")
    for t, pk in zip((DW, NM, NV), outs):
        t.update(dict(zip(sm_names, _unpack_rows(pk, sm_shapes))))
    G.update(gsm)

    def shaped(t):
        return [t[n].reshape(W[n].shape) for n in names]

    return (loss, grad_x.reshape(x.shape), *shaped(G), *shaped(DW), *shaped(NM), *shaped(NV))
```

```python
import functools

import jax
import jax.numpy as jnp
import numpy as np
from jax import lax
from jax.experimental import pallas as pl
from jax.experimental.pallas import tpu as pltpu

F32 = jnp.float32
BF16 = jnp.bfloat16

D = 1024
DI = 2048
NH = 32
HP = 64
NG = 4
NS = 128
CH = 128
DX = 3072
FF = 2816
NI = 10272
EPS = 1e-5

OFF_BCV, OFF_XBC, OFF_G, OFF_Z, OFF_DT = 0, 3072, 6144, 8192, 10240
NIP = 10368
_SEGS = ((0, 2048, OFF_G), (2048, 3072, OFF_BCV), (5120, 2048, OFF_Z), (7168, 3072, OFF_XBC), (10240, 32, OFF_DT))

LANES = 128
HALO = 16
V7X_VMEM_LIMIT = 56 * 2 ** 20

ADAM_LR, ADAM_B1, ADAM_B2, ADAM_EPS, ADAM_WD, ADAM_STEP = 0.001, 0.9, 0.999, 1e-08, 0.01, 10

NN = (((1,), (0,)), ((), ()))
NT = (((1,), (1,)), ((), ()))
TN = (((0,), (0,)), ((), ()))


def _dot(a, b, dims=NN):
    return lax.dot_general(a, b, dims, preferred_element_type=F32)


def _params(sem, **kw):
    return pltpu.CompilerParams(dimension_semantics=sem, vmem_limit_bytes=V7X_VMEM_LIMIT, **kw)


def _pick(dim, cap):
    if dim <= cap:
        return dim
    best = None
    for t in range(LANES, cap + 1, LANES):
        if dim % t == 0:
            best = t
    assert best is not None, (dim, cap)
    return best


def _sigmoid(x):
    return 1.0 / (1.0 + jnp.exp(-x))


def _matmul(a, b, *, mode, out_dtype, name, residual=None, caps=(1408, 1408, 1408)):
    if mode == "nn":
        (M, K), (K2, N) = a.shape, b.shape
    elif mode == "nt":
        (M, K), (N, K2) = a.shape, b.shape
    else:
        (K, M), (K2, N) = a.shape, b.shape
    assert K == K2, (name, a.shape, b.shape)
    tm, tn, tk = _pick(M, caps[0]), _pick(N, caps[1]), _pick(K, 2048 if mode == "tn" else caps[2])
    nk = K // tk
    if mode == "tn":
        a_spec = pl.BlockSpec((tk, tm), lambda i, j, k: (k, i))
    else:
        a_spec = pl.BlockSpec((tm, tk), lambda i, j, k: (i, k))
    if mode == "nt":
        b_spec = pl.BlockSpec((tn, tk), lambda i, j, k: (j, k))
    else:
        b_spec = pl.BlockSpec((tk, tn), lambda i, j, k: (k, j))
    dims = {"nn": NN, "nt": NT, "tn": TN}[mode]
    o_spec = pl.BlockSpec((tm, tn), lambda i, j, k: (i, j))
    has_res = residual is not None

    def body(*refs):
        a_ref, b_ref = refs[:2]
        r_ref = refs[2] if has_res else None
        o_ref = refs[3 if has_res else 2]
        acc_ref = refs[-1]
        k = pl.program_id(2)
        part = _dot(a_ref[...], b_ref[...], dims)

        def finish(r):
            if has_res:
                r = r + r_ref[...].astype(F32)
            o_ref[...] = r.astype(out_dtype)

        if nk == 1:
            finish(part)
            return

        @pl.when(k == 0)
        def _():
            acc_ref[...] = part

        @pl.when(jnp.logical_and(k > 0, k < nk - 1))
        def _():
            acc_ref[...] += part

        @pl.when(k == nk - 1)
        def _():
            finish(acc_ref[...] + part)

    in_specs = [a_spec, b_spec] + ([o_spec] if has_res else [])
    args = (a, b) + ((residual,) if has_res else ())
    return pl.pallas_call(
        body, name=name, grid=(M // tm, N // tn, nk), in_specs=in_specs, out_specs=o_spec,
        out_shape=jax.ShapeDtypeStruct((M, N), out_dtype),
        scratch_shapes=[pltpu.VMEM((tm, tn), F32)] if nk > 1 else [],
        compiler_params=_params(("parallel", "parallel", "arbitrary")),
    )(*args)


class _Rows:
    def __init__(self, T, tm):
        self.T, self.tm = T, min(tm, T // 2)
        self.nrow = T // self.tm
        self.r = self.tm // HALO
        self.nb = T // HALO

    def tile(self, w, cb=0, step=1):
        return pl.BlockSpec((self.tm, w), lambda j, i: (i, cb + step * j))

    def prev(self, w, cb=0, step=1):
        r = self.r
        return pl.BlockSpec((HALO, w), lambda j, i: (jnp.maximum(i * r - 1, 0), cb + step * j))

    def next(self, w, cb=0, step=1):
        r, nb = self.r, self.nb
        return pl.BlockSpec((HALO, w), lambda j, i: (jnp.minimum((i + 1) * r, nb - 1), cb + step * j))

    def colvec(self, k, w, cb=0, step=1):
        return pl.BlockSpec((k, w), lambda j, i: (0, cb + step * j))

    def call(self, body, name, ncol, in_specs, out_specs, out_shape, args, aliases=None):
        return pl.pallas_call(
            body, name=name, grid=(ncol, self.nrow), in_specs=in_specs, out_specs=out_specs,
            out_shape=out_shape, input_output_aliases=aliases or {},
            compiler_params=_params(("parallel", "arbitrary")),
        )(*args)


ANY = pl.BlockSpec(memory_space=pl.ANY)


def _shifts_causal(ext, nk, tm):
    out = []
    for k in range(nk):
        s = nk - 1 - k
        r = ext if s == 0 else pltpu.roll(ext, s, 0)
        out.append(r[HALO:])
    return out


def _shifts_anticausal(ext, nk, tm):
    n = ext.shape[0]
    out = []
    for k in range(nk):
        s = nk - 1 - k
        r = ext if s == 0 else pltpu.roll(ext, n - s, 0)
        out.append(r[:tm])
    return out


def _wsum(w, parts):
    acc = w[0:1, :] * parts[0]
    for k in range(1, len(parts)):
        acc = acc + w[k:k + 1, :] * parts[k]
    return acc


def _colsum(x):
    return jnp.sum(x, axis=0, keepdims=True)


def _acc_out(ref, val, first):
    @pl.when(first)
    def _():
        ref[...] = val

    @pl.when(jnp.logical_not(first))
    def _():
        ref[...] += val


def _acc_rows(ref, rows, first):
    for k, r in enumerate(rows):
        _acc_out(ref.at[k:k + 1, :], r, first)


def _rmsnorm_fwd(x, w, name):
    T = x.shape[0]
    R = _Rows(T, 512)

    def body(x_ref, w_ref, o_ref):
        xv = x_ref[...]
        r = lax.rsqrt(jnp.mean(xv * xv, axis=-1, keepdims=True) + EPS)
        o_ref[...] = (xv * r * w_ref[...]).astype(BF16)

    return R.call(body, name, 1, [R.tile(D), R.colvec(1, D)], R.tile(D),
                  jax.ShapeDtypeStruct((T, D), BF16), (x, w))


def _rmsnorm_bwd(dy, x, w, dres, name):
    T = x.shape[0]
    R = _Rows(T, 512)

    def body(dy_ref, x_ref, w_ref, dr_ref, dx_ref, dxb_ref, dw_ref):
        xv = x_ref[...]
        r = lax.rsqrt(jnp.mean(xv * xv, axis=-1, keepdims=True) + EPS)
        xh = xv * r
        dyv = dy_ref[...].astype(F32)
        dxh = dyv * w_ref[...]
        dx = r * (dxh - xh * jnp.mean(dxh * xh, axis=-1, keepdims=True)) + dr_ref[...]
        dx_ref[...] = dx
        dxb_ref[...] = dx.astype(BF16)
        _acc_out(dw_ref, _colsum(dyv * xh), pl.program_id(1) == 0)

    return R.call(body, name, 1, [R.tile(D), R.tile(D), R.colvec(1, D), R.tile(D)],
                  [R.tile(D), R.tile(D), R.colvec(1, D)],
                  [jax.ShapeDtypeStruct((T, D), F32), jax.ShapeDtypeStruct((T, D), BF16),
                   jax.ShapeDtypeStruct((1, D), F32)],
                  (dy, x, w, dres))


def _branch_a_fwd(proj, conv_w):
    T = proj.shape[0]
    R = _Rows(T, 512)
    tm = R.tm

    def body(p_ref, pp_ref, w_ref, o_ref):
        keep = (pl.program_id(1) > 0).astype(F32)
        cv = p_ref[:, D:2 * D].astype(F32) * p_ref[:, 2 * D:].astype(F32)
        cvp = pp_ref[:, D:2 * D].astype(F32) * pp_ref[:, 2 * D:].astype(F32) * keep
        sh = _shifts_causal(jnp.concatenate([cvp, cv], axis=0), 3, tm)
        ca = _wsum(w_ref[...], sh)
        o_ref[...] = (p_ref[:, :D].astype(F32) * ca).astype(BF16)

    return R.call(body, "branch_a_fwd", 1, [R.tile(3 * D), R.prev(3 * D), R.colvec(3, D)], R.tile(D),
                  jax.ShapeDtypeStruct((T, D), BF16), (proj, proj, conv_w))


def _branch_a_bwd(dya_in, proj, conv_w, dproj):
    T = proj.shape[0]
    R = _Rows(T, 256)
    tm = R.tm

    def body(d_ref, dn_ref, p_ref, pp_ref, pn_ref, w_ref, _alias, o_ref, dw_ref):
        i = pl.program_id(1)
        keep_p = (i > 0).astype(F32)
        keep_n = (i < R.nrow - 1).astype(F32)
        w = w_ref[...]
        b = p_ref[:, :D].astype(F32)
        c = p_ref[:, D:2 * D].astype(F32)
        v = p_ref[:, 2 * D:].astype(F32)
        cvp = pp_ref[:, D:2 * D].astype(F32) * pp_ref[:, 2 * D:].astype(F32) * keep_p
        sh = _shifts_causal(jnp.concatenate([cvp, c * v], axis=0), 3, tm)
        ca = _wsum(w, sh)
        d = d_ref[...].astype(F32)
        dca = d * b
        dca_n = dn_ref[...].astype(F32) * pn_ref[:, :D].astype(F32) * keep_n
        dsh = _shifts_anticausal(jnp.concatenate([dca, dca_n], axis=0), 3, tm)
        dcv = _wsum(w, dsh)
        o_ref[:, :D] = (d * ca).astype(BF16)
        o_ref[:, D:2 * D] = (dcv * v).astype(BF16)
        o_ref[:, 2 * D:] = (dcv * c).astype(BF16)
        _acc_rows(dw_ref, [_colsum(dca * s) for s in sh], i == 0)

    return R.call(
        body, "branch_a_bwd", 1,
        [R.tile(D), R.next(D), R.tile(3 * D), R.prev(3 * D), R.next(3 * D), R.colvec(3, D), ANY],
        [R.tile(3 * D), R.colvec(3, D)],
        [jax.ShapeDtypeStruct(dproj.shape, BF16), jax.ShapeDtypeStruct((3, D), F32)],
        (dya_in, dya_in, proj, proj, proj, conv_w, dproj), aliases={6: 0})


_XW = 512


def _xbc_fwd(proj, conv_w, conv_b):
    T = proj.shape[0]
    R = _Rows(T, 512)
    tm = R.tm
    cb = OFF_XBC // _XW

    def body(x_ref, xp_ref, w_ref, b_ref, o_ref):
        keep = (pl.program_id(1) > 0).astype(F32)
        ext = jnp.concatenate([xp_ref[...].astype(F32) * keep, x_ref[...].astype(F32)], axis=0)
        pre = _wsum(w_ref[...], _shifts_causal(ext, 4, tm)) + b_ref[...]
        o_ref[...] = (pre * _sigmoid(pre)).astype(BF16)

    return R.call(body, "xbc_fwd", DX // _XW,
                  [R.tile(_XW, cb), R.prev(_XW, cb), R.colvec(4, _XW), R.colvec(1, _XW)], R.tile(_XW),
                  jax.ShapeDtypeStruct((T, DX), BF16), (proj, proj, conv_w, conv_b))


def _xbc_bwd(dact, proj, conv_w, conv_b, dproj):
    T = proj.shape[0]
    R = _Rows(T, 512)
    tm = R.tm
    cb = OFF_XBC // _XW

    def body(d_ref, dn_ref, x_ref, xp_ref, xn_ref, w_ref, b_ref, _alias, o_ref, dw_ref, db_ref):
        i = pl.program_id(1)
        keep_p = (i > 0).astype(F32)
        keep_n = (i < R.nrow - 1).astype(F32)
        w = w_ref[...]
        ext = jnp.concatenate([xp_ref[...].astype(F32) * keep_p, x_ref[...].astype(F32),
                               xn_ref[...].astype(F32)], axis=0)
        sh = _shifts_causal(ext, 4, tm + HALO)
        pre = _wsum(w, sh) + b_ref[...]
        s = _sigmoid(pre)
        dsilu = s * (1.0 + pre * (1.0 - s))
        dext = jnp.concatenate([d_ref[...].astype(F32), dn_ref[...].astype(F32) * keep_n], axis=0)
        dpre = dext * dsilu
        dsh = _shifts_anticausal(dpre, 4, tm)
        o_ref[...] = _wsum(w, dsh).astype(BF16)
        dp = dpre[:tm]
        _acc_rows(dw_ref, [_colsum(dp * q[:tm]) for q in sh], i == 0)
        _acc_out(db_ref, _colsum(dp), i == 0)

    return R.call(
        body, "xbc_bwd", DX // _XW,
        [R.tile(_XW), R.next(_XW), R.tile(_XW, cb), R.prev(_XW, cb), R.next(_XW, cb),
         R.colvec(4, _XW), R.colvec(1, _XW), ANY],
        [R.tile(_XW, cb), R.colvec(4, _XW), R.colvec(1, _XW)],
        [jax.ShapeDtypeStruct(dproj.shape, BF16), jax.ShapeDtypeStruct((4, DX), F32),
         jax.ShapeDtypeStruct((1, DX), F32)],
        (dact, dact, proj, proj, proj, conv_w, conv_b, dproj), aliases={7: 0})


def _softplus(x):
    return jnp.maximum(x, 0.0) + jnp.log(1.0 + jnp.exp(-jnp.abs(x)))


def _dt_fwd(dt_raw, dt_bias_p, a_log_p):
    T = dt_raw.shape[0]

    def body(r_ref, b_ref, al_ref, dt_ref, ac_ref, acT_ref):
        dt = _softplus(r_ref[...] + b_ref[...])
        s = dt * (-jnp.exp(al_ref[...]))
        row = lax.broadcasted_iota(jnp.int32, (CH, LANES), 0)
        k = 1
        while k < CH:
            s = s + jnp.where(row >= k, pltpu.roll(s, k, 0), 0.0)
            k *= 2
        dt_ref[...] = dt
        ac_ref[...] = s
        acT_ref[...] = s.T

    blk = pl.BlockSpec((CH, LANES), lambda i: (i, 0))
    vec = pl.BlockSpec((1, LANES), lambda i: (0, 0))
    return pl.pallas_call(
        body, name="dt_fwd", grid=(T // CH,), in_specs=[blk, vec, vec], out_specs=[blk, blk, blk],
        out_shape=[jax.ShapeDtypeStruct((T, LANES), F32)] * 3, compiler_params=_params(("parallel",)),
    )(dt_raw, dt_bias_p, a_log_p)


def _dt_bwd(dacum, ddt_x, dt_raw, dt_bias_p, a_log_p, dproj):
    T = dt_raw.shape[0]
    nc = T // CH

    def body(da_ref, dx_ref, r_ref, b_ref, al_ref, _alias, o_ref, db_ref, dal_ref):
        i = pl.program_id(0)
        a = -jnp.exp(al_ref[...])
        z = r_ref[...] + b_ref[...]
        dt = _softplus(z)
        s = da_ref[...]
        row = lax.broadcasted_iota(jnp.int32, (CH, LANES), 0)
        k = 1
        while k < CH:
            s = s + jnp.where(row < CH - k, pltpu.roll(s, CH - k, 0), 0.0)
            k *= 2
        ddt = s * a + dx_ref[...]
        draw = ddt * _sigmoid(z)
        o_ref[...] = draw.astype(BF16)
        _acc_out(db_ref, _colsum(draw), i == 0)
        _acc_out(dal_ref, _colsum(s * dt), i == 0)

        @pl.when(i == nc - 1)
        def _():
            dal_ref[...] = dal_ref[...] * a

    blk = pl.BlockSpec((CH, LANES), lambda i: (i, 0))
    vec = pl.BlockSpec((1, LANES), lambda i: (0, 0))
    oblk = pl.BlockSpec((CH, LANES), lambda i: (i, OFF_DT // LANES))
    return pl.pallas_call(
        body, name="dt_bwd", grid=(nc,), in_specs=[blk, blk, blk, vec, vec, ANY], out_specs=[oblk, vec, vec],
        out_shape=[jax.ShapeDtypeStruct(dproj.shape, BF16), jax.ShapeDtypeStruct((1, LANES), F32),
                   jax.ShapeDtypeStruct((1, LANES), F32)],
        input_output_aliases={5: 0}, compiler_params=_params(("arbitrary",)),
    )(dacum, ddt_x, dt_raw, dt_bias_p, a_log_p, dproj)


_GW = DI // NG
_HG = NH // NG
_NEG = -1e30


def _pair_lanes(left, v0, v1):
    return jnp.where(left, v0, v1)


def _ssd_specs(T, rev):
    nc = T // CH
    cm = (lambda c: nc - 1 - c) if rev else (lambda c: c)
    bw = NG * NS
    return dict(
        xs=pl.BlockSpec((CH, DI), lambda c: (cm(c), 0)),
        bm=pl.BlockSpec((CH, bw), lambda c: (cm(c), DI // bw)),
        cmat=pl.BlockSpec((CH, bw), lambda c: (cm(c), DI // bw + 1)),
        xbc=pl.BlockSpec((CH, DX), lambda c: (cm(c), 0)),
        col=pl.BlockSpec((CH, LANES), lambda c: (cm(c), 0)),
        dsk=pl.BlockSpec((1, DI), lambda c: (0, 0)),
        state=pl.BlockSpec((1, NS, DI), lambda c: (cm(c), 0, 0)),
    )


def _last(ref, lo, hi):
    return ref.at[(slice(None),) * (len(ref.shape) - 1) + (slice(lo, hi),)]


def _group_views(g, wide, narrow):
    return [_last(r, g * _GW, (g + 1) * _GW) for r in wide] + [_last(r, g * NS, (g + 1) * NS) for r in narrow]


def _ssd_fwd(xact, dt, acum, acumT, dsk_rep):
    T = xact.shape[0]
    nc = T // CH
    sp = _ssd_specs(T, False)

    def body(*refs):
        xs, bm, cmat, dtr, acr, actr, dsk, y, spv, S_ref = refs

        @pl.when(pl.program_id(0) == 0)
        def _():
            S_ref[...] = jnp.zeros_like(S_ref)

        for g in range(NG):
            group(g * _HG, dtr[...], acr[...], actr[...], *_group_views(g, (xs, dsk, y, spv, S_ref), (bm, cmat)))

    def group(hb, dt, ac, acT, xs_ref, dsk_ref, y_ref, sp_ref, S_ref, b_ref, c_ref):
        Bm, Cm = b_ref[...], c_ref[...]
        S = S_ref[...]
        sp_ref[0] = S
        cb = _dot(Cm, Bm, NT)
        CS = _dot(Cm, S.astype(BF16))
        row = lax.broadcasted_iota(jnp.int32, (CH, CH), 0)
        col = lax.broadcasted_iota(jnp.int32, (CH, CH), 1)
        tril = row >= col
        left = col < HP
        xd_parts, dec_parts = [], []
        for p in range(_HG // 2):
            sl = slice(p * LANES, (p + 1) * LANES)
            j0, j1 = hb + 2 * p, hb + 2 * p + 1
            xp = xs_ref[:, sl].astype(F32)
            a0, a1 = ac[:, j0:j0 + 1], ac[:, j1:j1 + 1]
            al0, al1 = ac[CH - 1:CH, j0:j0 + 1], ac[CH - 1:CH, j1:j1 + 1]
            X = xp * _pair_lanes(left, dt[:, j0:j0 + 1], dt[:, j1:j1 + 1])
            Xb = X.astype(BF16)
            yd = jnp.zeros((CH, LANES), F32)
            for j, aj, mask in ((j0, a0, left), (j1, a1, jnp.logical_not(left))):
                Lm = jnp.exp(jnp.where(tril, aj - acT[j:j + 1, :], _NEG))
                W = (cb * Lm).astype(BF16)
                yd = yd + _dot(W, jnp.where(mask, Xb, jnp.zeros_like(Xb)))
            eal = _pair_lanes(left, jnp.exp(a0), jnp.exp(a1))
            y = yd + eal * CS[:, sl] + dsk_ref[:, sl] * xp
            y_ref[:, sl] = y.astype(BF16)
            xd_parts.append(X * _pair_lanes(left, jnp.exp(al0 - a0), jnp.exp(al1 - a1)))
            dec_parts.append(_pair_lanes(left[0:1], jnp.exp(al0), jnp.exp(al1)))
        Xd = jnp.concatenate(xd_parts, axis=1).astype(BF16)
        dec = jnp.concatenate(dec_parts, axis=1)
        S_ref[...] = dec * S + _dot(Bm, Xd, TN)

    return pl.pallas_call(
        body, name="ssd_fwd", grid=(nc,),
        in_specs=[sp["xs"], sp["bm"], sp["cmat"], sp["col"], sp["col"], sp["col"], sp["dsk"]],
        out_specs=[sp["xs"], sp["state"]],
        out_shape=[jax.ShapeDtypeStruct((T, DI), BF16), jax.ShapeDtypeStruct((nc, NS, DI), F32)],
        scratch_shapes=[pltpu.VMEM((NS, DI), F32)],
        compiler_params=_params(("arbitrary",)),
    )(xact, xact, xact, dt, acum, acumT, dsk_rep)


def _ssd_bwd(dy, xact, dt, acum, acumT, dsk_rep, sprev):
    T = xact.shape[0]
    nc = T // CH
    sp = _ssd_specs(T, True)

    def body(*refs):
        xs, bm, cmat, dtr, acr, actr, dsk, dyr, spv, dxa, ddtx, dAc, dskacc, dS_ref = refs
        first = pl.program_id(0) == 0

        @pl.when(first)
        def _():
            dS_ref[...] = jnp.zeros_like(dS_ref)

        dbc = _last(dxa, DI, DX)
        ddtx_sum = jnp.zeros((CH, LANES), F32)
        dAc_sum = jnp.zeros((CH, LANES), F32)
        for g in range(NG):
            a, b = group(first, g * _HG, dtr[...], acr[...], actr[...],
                         *_group_views(g, (xs, dsk, dyr, spv, dxa, dskacc, dS_ref),
                                       (bm, cmat, dbc, _last(dbc, NG * NS, 2 * NG * NS))))
            ddtx_sum, dAc_sum = ddtx_sum + a, dAc_sum + b
        ddtx[...] = ddtx_sum
        dAc[...] = dAc_sum

    def group(first, hb, dt, ac, acT, xs_ref, dsk_ref, dy_ref, sp_ref, dx_ref, dskacc_ref, dS_ref, b_ref, c_ref,
              dB_ref, dC_ref):
        Bm, Cm = b_ref[...], c_ref[...]
        S = sp_ref[0]
        dS = dS_ref[...]
        Sb, dSb = S.astype(BF16), dS.astype(BF16)
        cb = _dot(Cm, Bm, NT)
        cbT = _dot(Bm, Cm, NT)
        CmT = Cm.T
        CS = _dot(Cm, Sb)
        T1 = _dot(Bm, dSb)
        row = lax.broadcasted_iota(jnp.int32, (CH, CH), 0)
        col = lax.broadcasted_iota(jnp.int32, (CH, CH), 1)
        tril = row >= col
        triu = row <= col
        left = col < HP
        lane8 = lax.broadcasted_iota(jnp.int32, (1, LANES), 1)
        lastrow = lax.broadcasted_iota(jnp.int32, (CH, 1), 0) == CH - 1
        dCB = jnp.zeros((CH, CH), F32)
        dCBT = jnp.zeros((CH, CH), F32)
        dAc = jnp.zeros((CH, LANES), F32)
        ddtx = jnp.zeros((CH, LANES), F32)
        xd_parts, dye_parts, dec_parts, dsk_parts = [], [], [], []
        for p in range(_HG // 2):
            sl = slice(p * LANES, (p + 1) * LANES)
            j0, j1 = hb + 2 * p, hb + 2 * p + 1
            xp = xs_ref[:, sl].astype(F32)
            dyp = dy_ref[:, sl].astype(F32)
            a0, a1 = ac[:, j0:j0 + 1], ac[:, j1:j1 + 1]
            al0, al1 = ac[CH - 1:CH, j0:j0 + 1], ac[CH - 1:CH, j1:j1 + 1]
            dtl = _pair_lanes(left, dt[:, j0:j0 + 1], dt[:, j1:j1 + 1])
            X = xp * dtl
            Xb = X.astype(BF16)
            eal = _pair_lanes(left, jnp.exp(a0), jnp.exp(a1))
            dtel = _pair_lanes(left, jnp.exp(al0 - a0), jnp.exp(al1 - a1))
            T1p = T1[:, sl]
            dXd = jnp.zeros((CH, LANES), F32)
            Rm = T1p * dtel * X
            GR = dyp * (eal * CS[:, sl]) - Rm
            SdS = dS[:, sl] * S[:, sl]
            for j, aj, alj, mask in ((j0, a0, al0, left), (j1, a1, al1, jnp.logical_not(left))):
                arow = acT[j:j + 1, :]
                Lm = jnp.exp(jnp.where(tril, aj - arow, _NEG))
                LmT = jnp.exp(jnp.where(triu, arow - aj, _NEG))
                dYm = jnp.where(mask, dyp, 0.0).astype(BF16)
                dWm = _dot(dYm, Xb, NT)
                dWmT = _dot(Xb, dYm, NT)
                dCB = dCB + dWm * Lm
                dCBT = dCBT + dWmT * LmT
                WT = cbT * LmT
                dXd = dXd + _dot(WT.astype(BF16), dYm)
                qd = dWm * (cb * Lm) - dWmT * WT
                colv = jnp.sum(qd + jnp.where(mask, GR, 0.0), axis=1, keepdims=True)
                tot = jnp.where(mask, Rm + jnp.exp(alj) * SdS, 0.0)
                dalast = jnp.sum(jnp.sum(tot, axis=0, keepdims=True), axis=1, keepdims=True)
                dAc = dAc + (colv + jnp.where(lastrow, dalast, 0.0)) * (lane8 == j).astype(F32)
            dX = dXd + dtel * T1p
            dXx = dX * xp
            for j, mask in ((j0, left), (j1, jnp.logical_not(left))):
                dd = jnp.sum(jnp.where(mask, dXx, 0.0), axis=1, keepdims=True)
                ddtx = ddtx + dd * (lane8 == j).astype(F32)
            dx_ref[:, sl] = (dX * dtl + dsk_ref[:, sl] * dyp).astype(BF16)
            dsk_parts.append(_colsum(dyp * xp))
            xd_parts.append(X * dtel)
            dye_parts.append(dyp * eal)
            dec_parts.append(_pair_lanes(left[0:1], jnp.exp(al0), jnp.exp(al1)))
        Xd = jnp.concatenate(xd_parts, axis=1).astype(BF16)
        dYe = jnp.concatenate(dye_parts, axis=1).astype(BF16)
        dec = jnp.concatenate(dec_parts, axis=1)
        dC_ref[...] = (_dot(dCB.astype(BF16), Bm) + _dot(dYe, Sb, NT)).astype(BF16)
        dB_ref[...] = (_dot(dCBT.astype(BF16), Cm) + _dot(Xd, dSb, NT)).astype(BF16)
        dS_ref[...] = _dot(CmT, dYe) + dec * dS
        _acc_out(dskacc_ref, jnp.concatenate(dsk_parts, axis=1), first)
        return ddtx, dAc

    return pl.pallas_call(
        body, name="ssd_bwd", grid=(nc,),
        in_specs=[sp["xs"], sp["bm"], sp["cmat"], sp["col"], sp["col"], sp["col"], sp["dsk"], sp["xs"],
                  sp["state"]],
        out_specs=[sp["xbc"], sp["col"], sp["col"], sp["dsk"]],
        out_shape=[jax.ShapeDtypeStruct((T, DX), BF16), jax.ShapeDtypeStruct((T, LANES), F32),
                   jax.ShapeDtypeStruct((T, LANES), F32), jax.ShapeDtypeStruct((1, DI), F32)],
        scratch_shapes=[pltpu.VMEM((NS, DI), F32)],
        compiler_params=_params(("arbitrary",)),
    )(xact, xact, xact, dt, acum, acumT, dsk_rep, dy, sprev)


def _gnorm_fwd(y, proj, w):
    T = y.shape[0]
    R = _Rows(T, 1024)
    zb = OFF_Z // _GW

    def body(y_ref, z_ref, w_ref, o_ref):
        z = z_ref[...].astype(F32)
        yf = y_ref[...].astype(F32) * z * _sigmoid(z)
        r = lax.rsqrt(jnp.mean(yf * yf, axis=-1, keepdims=True) + EPS)
        o_ref[...] = (yf * r * w_ref[...]).astype(BF16)

    return R.call(body, "gnorm_fwd", NG, [R.tile(_GW), R.tile(_GW, zb), R.colvec(1, _GW)], R.tile(_GW),
                  jax.ShapeDtypeStruct((T, DI), BF16), (y, proj, w))


def _gnorm_bwd(dn, y, proj, w, dproj):
    T = y.shape[0]
    R = _Rows(T, 1024)
    zb = OFF_Z // _GW

    def body(dn_ref, y_ref, z_ref, w_ref, _alias, dz_ref, dy_ref, dw_ref):
        z = z_ref[...].astype(F32)
        yv = y_ref[...].astype(F32)
        s = _sigmoid(z)
        silu = z * s
        yf = yv * silu
        r = lax.rsqrt(jnp.mean(yf * yf, axis=-1, keepdims=True) + EPS)
        yh = yf * r
        dnv = dn_ref[...].astype(F32)
        dyh = dnv * w_ref[...]
        dyf = r * (dyh - yh * jnp.mean(dyh * yh, axis=-1, keepdims=True))
        dy_ref[...] = (dyf * silu).astype(BF16)
        dz_ref[...] = (dyf * yv * s * (1.0 + z * (1.0 - s))).astype(BF16)
        _acc_out(dw_ref, _colsum(dnv * yh), pl.program_id(1) == 0)

    return R.call(
        body, "gnorm_bwd", NG, [R.tile(_GW), R.tile(_GW), R.tile(_GW, zb), R.colvec(1, _GW), ANY],
        [R.tile(_GW, zb), R.tile(_GW), R.colvec(1, _GW)],
        [jax.ShapeDtypeStruct(dproj.shape, BF16), jax.ShapeDtypeStruct((T, DI), BF16),
         jax.ShapeDtypeStruct((1, DI), F32)],
        (dn, y, proj, w, dproj), aliases={4: 0})


def _merge_fwd(proj, ya, ys):
    T = proj.shape[0]
    R = _Rows(T, 512)
    gb = OFF_G // (2 * D)

    def body(g_ref, ya_ref, ys_ref, o_ref):
        ga = _sigmoid(g_ref[:, :D].astype(F32))
        gs = _sigmoid(g_ref[:, D:].astype(F32))
        o_ref[...] = (ga * ya_ref[...].astype(F32) + gs * ys_ref[...].astype(F32)).astype(BF16)

    return R.call(body, "merge_fwd", 1, [R.tile(2 * D, gb), R.tile(D), R.tile(D)], R.tile(D),
                  jax.ShapeDtypeStruct((T, D), BF16), (proj, ya, ys))


def _merge_bwd(dm, proj, ya, ys, ncols):
    T = proj.shape[0]
    R = _Rows(T, 256)
    gb = OFF_G // (2 * D)

    def body(dm_ref, g_ref, ya_ref, ys_ref, dg_ref, dya_ref, dys_ref):
        d = dm_ref[...].astype(F32)
        ga = _sigmoid(g_ref[:, :D].astype(F32))
        gs = _sigmoid(g_ref[:, D:].astype(F32))
        dya_ref[...] = (d * ga).astype(BF16)
        dys_ref[...] = (d * gs).astype(BF16)
        dg_ref[:, :D] = (d * ya_ref[...].astype(F32) * ga * (1.0 - ga)).astype(BF16)
        dg_ref[:, D:] = (d * ys_ref[...].astype(F32) * gs * (1.0 - gs)).astype(BF16)

    return R.call(
        body, "merge_bwd", 1, [R.tile(D), R.tile(2 * D, gb), R.tile(D), R.tile(D)],
        [R.tile(2 * D, gb), R.tile(D), R.tile(D)],
        [jax.ShapeDtypeStruct((T, ncols), BF16), jax.ShapeDtypeStruct((T, D), BF16),
         jax.ShapeDtypeStruct((T, D), BF16)],
        (dm, proj, ya, ys))


_FW = 1408
_FB = FF // _FW


def _ffn_act_fwd(hv, conv_w, conv_b):
    T = hv.shape[0]
    R = _Rows(T, 256)
    tm = R.tm

    def body(h1_ref, h1p_ref, h3_ref, w_ref, b_ref, o_ref):
        keep = (pl.program_id(1) > 0).astype(F32)
        ext = jnp.concatenate([h1p_ref[...].astype(F32) * keep, h1_ref[...].astype(F32)], axis=0)
        pre = _wsum(w_ref[...], _shifts_causal(ext, 3, tm)) + b_ref[...]
        o_ref[...] = (pre * _sigmoid(pre) * h3_ref[...].astype(F32)).astype(BF16)

    return R.call(body, "ffn_act_fwd", _FB,
                  [R.tile(_FW), R.prev(_FW), R.tile(_FW, _FB), R.colvec(3, _FW), R.colvec(1, _FW)],
                  R.tile(_FW), jax.ShapeDtypeStruct((T, FF), BF16), (hv, hv, hv, conv_w, conv_b))


def _ffn_act_bwd(dg, hv, conv_w, conv_b):
    T = hv.shape[0]
    R = _Rows(T, 256)
    tm = R.tm

    def body(dg_ref, h1_ref, h1p_ref, h3_ref, w_ref, b_ref, dh3_ref, dpre_ref, dw_ref, db_ref):
        i = pl.program_id(1)
        keep = (i > 0).astype(F32)
        ext = jnp.concatenate([h1p_ref[...].astype(F32) * keep, h1_ref[...].astype(F32)], axis=0)
        sh = _shifts_causal(ext, 3, tm)
        pre = _wsum(w_ref[...], sh) + b_ref[...]
        s = _sigmoid(pre)
        d = dg_ref[...].astype(F32)
        dh3_ref[...] = (d * pre * s).astype(BF16)
        dpre = d * h3_ref[...].astype(F32) * s * (1.0 + pre * (1.0 - s))
        dpre_ref[...] = dpre.astype(BF16)
        _acc_rows(dw_ref, [_colsum(dpre * q) for q in sh], i == 0)
        _acc_out(db_ref, _colsum(dpre), i == 0)

    return R.call(
        body, "ffn_act_bwd", _FB,
        [R.tile(_FW), R.tile(_FW), R.prev(_FW), R.tile(_FW, _FB), R.colvec(3, _FW), R.colvec(1, _FW)],
        [R.tile(_FW), R.tile(_FW), R.colvec(3, _FW), R.colvec(1, _FW)],
        [jax.ShapeDtypeStruct((T, FF), BF16), jax.ShapeDtypeStruct((T, FF), BF16),
         jax.ShapeDtypeStruct((3, FF), F32), jax.ShapeDtypeStruct((1, FF), F32)],
        (dg, hv, hv, hv, conv_w, conv_b))


def _conv3_transpose(dpre, conv_w):
    T = dpre.shape[0]
    R = _Rows(T, 256)
    tm = R.tm

    def body(d_ref, dn_ref, w_ref, o_ref):
        keep = (pl.program_id(1) < R.nrow - 1).astype(F32)
        ext = jnp.concatenate([d_ref[...].astype(F32), dn_ref[...].astype(F32) * keep], axis=0)
        o_ref[...] = _wsum(w_ref[...], _shifts_anticausal(ext, 3, tm)).astype(BF16)

    return R.call(body, "ffn_conv_bwd", _FB, [R.tile(_FW), R.next(_FW), R.colvec(3, _FW)], R.tile(_FW),
                  jax.ShapeDtypeStruct((T, FF), BF16), (dpre, dpre, conv_w))


def _final_loss(h, w, target):
    T = h.shape[0]
    R = _Rows(T, 512)

    def body(h_ref, w_ref, t_ref, l_ref, dh_ref, dhb_ref, dw_ref):
        first = pl.program_id(1) == 0
        xv = h_ref[...]
        wv = w_ref[...]
        r = lax.rsqrt(jnp.mean(xv * xv, axis=-1, keepdims=True) + EPS)
        xh = xv * r
        err = xh * wv - t_ref[...]
        part = 0.5 * jnp.sum(jnp.mean(err * err, axis=-1, keepdims=True), axis=0, keepdims=True)
        _acc_out(l_ref, jnp.broadcast_to(part, l_ref.shape), first)
        dy = err * (1.0 / D)
        dxh = dy * wv
        dh = r * (dxh - xh * jnp.mean(dxh * xh, axis=-1, keepdims=True))
        dh_ref[...] = dh
        dhb_ref[...] = dh.astype(BF16)
        _acc_out(dw_ref, _colsum(dy * xh), first)

    return R.call(body, "final_loss", 1, [R.tile(D), R.colvec(1, D), R.tile(D)],
                  [R.colvec(8, LANES), R.tile(D), R.tile(D), R.colvec(1, D)],
                  [jax.ShapeDtypeStruct((8, LANES), F32), jax.ShapeDtypeStruct((T, D), F32),
                   jax.ShapeDtypeStruct((T, D), BF16), jax.ShapeDtypeStruct((1, D), F32)], (h, w, target))


def _pad_lanes(v, n=LANES):
    return jnp.pad(v, ((0, 0), (0, n - v.shape[1])))


class _Hooks:
    def after_norm(self, u):
        return u

    def late_weights(self, wts, after):
        return wts

    def grads_ready(self, grads, tie):
        return tie

    def mark(self, name, value):
        pass


def _local_step(x, target, wts, hooks=None):
    hooks = hooks or _Hooks()
    T = x.shape[0]
    w_in = wts["w_in"]
    dt_bias_p, a_log_p = _pad_lanes(wts["dt_bias"]), _pad_lanes(wts["a_log"])
    dsk_rep = jnp.repeat(wts["d_skip"], HP, axis=1)

    u = hooks.after_norm(_rmsnorm_fwd(x, wts["norm_mix_w"], "norm_mix_fwd"))
    proj = _matmul(u, w_in, mode="nn", out_dtype=BF16, name="mm_in")
    dt_raw = _matmul(u, w_in[:, OFF_DT:], mode="nn", out_dtype=F32, name="mm_dt")
    ya_in = _branch_a_fwd(proj, wts["conv_a_w"])
    xact = _xbc_fwd(proj, wts["ssd_conv_w"], wts["ssd_conv_b"])
    dt, acum, acumT = _dt_fwd(dt_raw, dt_bias_p, a_log_p)
    y_ssd, sprev = _ssd_fwd(xact, dt, acum, acumT, dsk_rep)
    yn = _gnorm_fwd(y_ssd, proj, wts["ssd_norm_w"])
    late = hooks.late_weights(wts, yn)
    w_a_out, w_s_out, w_o, w_up, w_down = (late[k] for k in ("w_a_out", "w_s_out", "w_o", "w_up", "w_down"))
    y_a = _matmul(ya_in, w_a_out, mode="nn", out_dtype=BF16, name="mm_a_out")
    y_s = _matmul(yn, w_s_out, mode="nn", out_dtype=BF16, name="mm_s_out")
    merged = _merge_fwd(proj, y_a, y_s)
    h1 = _matmul(merged, w_o, mode="nn", out_dtype=F32, name="mm_o", residual=x)
    v = _rmsnorm_fwd(h1, wts["norm_ffn_w"], "norm_ffn_fwd")
    hv = _matmul(v, w_up, mode="nn", out_dtype=BF16, name="mm_up")
    gact = _ffn_act_fwd(hv, wts["ffn_conv_w"], wts["ffn_conv_b"])
    h2 = _matmul(gact, w_down, mode="nn", out_dtype=F32, name="mm_down", residual=h1)
    loss, dh2, dh2b, g_final = _final_loss(h2, wts["final_norm_w"], target)

    grads = {"final_norm_w": g_final}
    grads["w_down"] = _matmul(gact, dh2b, mode="tn", out_dtype=F32, name="mm_down_dw")
    dgact = _matmul(dh2b, w_down, mode="nt", out_dtype=BF16, name="mm_down_dx")
    dh3, dpre, grads["ffn_conv_w"], grads["ffn_conv_b"] = _ffn_act_bwd(dgact, hv, wts["ffn_conv_w"], wts["ffn_conv_b"])
    dh1c = _conv3_transpose(dpre, wts["ffn_conv_w"])
    grads["w_up"] = jnp.concatenate(
        [_matmul(v, dh1c, mode="tn", out_dtype=F32, name="mm_up_dw1"),
         _matmul(v, dh3, mode="tn", out_dtype=F32, name="mm_up_dw3")], axis=1)
    dv = _matmul(dh1c, w_up[:, :FF], mode="nt", out_dtype=F32, name="mm_up_dx1")
    dv = _matmul(dh3, w_up[:, FF:], mode="nt", out_dtype=F32, name="mm_up_dx3", residual=dv)
    dh1, dh1b, grads["norm_ffn_w"] = _rmsnorm_bwd(dv, h1, wts["norm_ffn_w"], dh2, "norm_ffn_bwd")
    grads["w_o"] = _matmul(merged, dh1b, mode="tn", out_dtype=F32, name="mm_o_dw")
    dmerged = _matmul(dh1b, w_o, mode="nt", out_dtype=BF16, name="mm_o_dx")
    dproj, dya, dys = _merge_bwd(dmerged, proj, y_a, y_s, NIP)
    grads["w_a_out"] = _matmul(ya_in, dya, mode="tn", out_dtype=F32, name="mm_a_out_dw")
    dya_in = _matmul(dya, w_a_out, mode="nt", out_dtype=BF16, name="mm_a_out_dx")
    dproj, grads["conv_a_w"] = _branch_a_bwd(dya_in, proj, wts["conv_a_w"], dproj)
    grads["w_s_out"] = _matmul(yn, dys, mode="tn", out_dtype=F32, name="mm_s_out_dw")
    dys = hooks.grads_ready({k: grads[k] for k in ("w_a_out", "w_s_out", "w_o", "w_up", "w_down")}, dys)
    dyn =_matmul(dys, w_s_out, mode="nt", out_dtype=BF16, name="mm_s_out_dx")
    dproj, dy_ssd, grads["ssd_norm_w"] = _gnorm_bwd(dyn, y_ssd, proj, wts["ssd_norm_w"], dproj)
    dxact, ddt_x, dacum, dskl = _ssd_bwd(dy_ssd, xact, dt, acum, acumT, dsk_rep, sprev)
    hooks.mark("ssd_bwd", dxact)
    grads["d_skip"] = dskl.reshape(NH, HP).sum(axis=1).reshape(1, NH)
    dproj, grads["ssd_conv_w"], grads["ssd_conv_b"] = _xbc_bwd(dxact, proj, wts["ssd_conv_w"], wts["ssd_conv_b"], dproj)
    dproj, g_dtb, g_alog = _dt_bwd(dacum, ddt_x, dt_raw, dt_bias_p, a_log_p, dproj)
    grads["dt_bias"], grads["a_log"] = g_dtb[:, :NH], g_alog[:, :NH]
    grads["w_in"] = _matmul(u, dproj, mode="tn", out_dtype=F32, name="mm_in_dw")
    dproj = hooks.grads_ready({"w_in": grads["w_in"]}, dproj)
    du =_matmul(dproj, w_in, mode="nt", out_dtype=F32, name="mm_in_dx")
    grad_x, _, grads["norm_mix_w"] = _rmsnorm_bwd(du, x, wts["norm_mix_w"], dh1, "norm_mix_bwd")
    return loss, grad_x, grads


def _permute_w_in(w):
    out = jnp.zeros((w.shape[0], NIP), w.dtype)
    for o, n, no in _SEGS:
        out = lax.dynamic_update_slice(out, w[:, o:o + n], (0, no))
    return out


def _unpermute_w_in(g):
    order = sorted(_SEGS)
    return jnp.concatenate([g[:, no:no + n] for o, n, no in order], axis=1)


MESH = pl.DeviceIdType.MESH
NCHIP = 4
NDEV = 8

_W_IN = (("w_in", D, NI // NCHIP, 1),)
_W_REST = (("w_a_out", D // NCHIP, D, 0), ("w_s_out", DI // NCHIP, D, 0), ("w_o", D // NCHIP, D, 0),
           ("w_up", D, 2 * FF // NCHIP, 1), ("w_down", FF // NCHIP, D, 0))


def _slab_rows(group):
    rows = [r * c // LANES for _, r, c, _ in group]
    assert all(n % 32 == 0 for n in rows), rows
    return rows


def _coords():
    return lax.axis_index("x"), lax.axis_index("y"), lax.axis_index("c")


def _other_chips(x, y):
    return [(1 - x, y), (x, 1 - y), (1 - x, 1 - y)]


def _ag_weights(shard):
    nrows = shard.shape[0]
    hr = nrows // 2

    def body(x_ref, out_ref, send_sems, recv_sems, local_sem):
        x, y, c = _coords()
        me = 2 * x + y
        chips = _other_chips(x, y)

        def rows(s, h):
            return out_ref.at[s, pl.ds(h * hr, hr), :]

        def copy(k, s, h, to, src=None):
            return pltpu.make_async_remote_copy(
                src_ref=rows(s, h) if src is None else src, dst_ref=rows(s, h),
                send_sem=send_sems.at[k], recv_sem=recv_sems.at[k], device_id=to, device_id_type=MESH)

        mine = pltpu.make_async_copy(x_ref, out_ref.at[me], local_sem)
        mine.start()
        first = [copy(k, me, c, (*chip, c), src=x_ref.at[pl.ds(c * hr, hr), :]) for k, chip in enumerate(chips)]
        for cp in first:
            cp.start()
        passed = []
        for k, chip in enumerate(chips):
            s = 2 * chip[0] + chip[1]
            copy(k, s, c, (x, y, c)).wait_recv()
            fwd = copy(3 + k, s, c, (x, y, 1 - c))
            fwd.start()
            passed.append(fwd)
        for k, chip in enumerate(chips):
            copy(3 + k, 2 * chip[0] + chip[1], 1 - c, (x, y, c)).wait_recv()
        for cp in first + passed:
            cp.wait_send()
        mine.wait()

    return pl.pallas_call(
        body, name="ag_weights", in_specs=[ANY], out_specs=ANY,
        out_shape=jax.ShapeDtypeStruct((NCHIP,) + shard.shape, shard.dtype),
        scratch_shapes=[pltpu.SemaphoreType.DMA((6,)), pltpu.SemaphoreType.DMA((6,)), pltpu.SemaphoreType.DMA],
        compiler_params=pltpu.CompilerParams(has_side_effects=True),
    )(shard)


HBM = pl.BlockSpec(memory_space=pltpu.HBM)
SEM = pl.BlockSpec(memory_space=pltpu.SEMAPHORE)
_EFFECT = pltpu.SideEffectType.DATAFLOW_SIDE_EFFECTING
_NCOPY = NCHIP - 1


def _plan_bcast(src_ref, land_ref, send_sems, recv_sems):
    x, y, c = _coords()
    sends, lands = [], []
    for k, chip in enumerate(_other_chips(x, y)):
        def copy(slot):
            return pltpu.make_async_remote_copy(
                src_ref=src_ref, dst_ref=land_ref.at[slot], send_sem=send_sems.at[k], recv_sem=recv_sems.at[k],
                device_id=(*chip, c), device_id_type=MESH)
        sends.append(copy(2 * x + y))
        lands.append(copy(2 * chip[0] + chip[1]))
    return sends, lands


def _plan_scatter(src_ref, land_ref, send_sems, recv_sems):
    x, y, c = _coords()
    cps = [pltpu.make_async_remote_copy(
        src_ref=src_ref.at[2 * chip[0] + chip[1]], dst_ref=land_ref.at[k], send_sem=send_sems.at[k],
        recv_sem=recv_sems.at[k], device_id=(*chip, c), device_id_type=MESH)
        for k, chip in enumerate(_other_chips(x, y))]
    return cps, cps


def _split_start(name, src, land, plan):
    def body(src_ref, land_ref, send_sems, recv_sems, src_thru, land_thru, token):
        for cp in plan(src_ref, land_ref, send_sems, recv_sems)[0]:
            cp.start()
        token[...] = jnp.zeros_like(token)

    send_sems, recv_sems, src_thru, land_thru, token = pl.pallas_call(
        body, name=name,
        out_shape=(pltpu.SemaphoreType.DMA((_NCOPY,)), pltpu.SemaphoreType.DMA((_NCOPY,)),
                   pltpu.HBM(src.shape, src.dtype), pltpu.HBM(land.shape, land.dtype),
                   jax.ShapeDtypeStruct((8, LANES), F32)),
        in_specs=(HBM, HBM), out_specs=(SEM, SEM, HBM, HBM, pl.BlockSpec(memory_space=pltpu.VMEM)),
        input_output_aliases={0: 2, 1: 3},
        compiler_params=pltpu.CompilerParams(has_side_effects=_EFFECT),
    )(pltpu.with_memory_space_constraint(src, pltpu.HBM), pltpu.with_memory_space_constraint(land, pltpu.HBM))
    return (send_sems, recv_sems, src_thru, land_thru), token


def _split_wait(name, handle, after, plan):
    send_sems, recv_sems, src_thru, land_thru = handle

    def body(src_ref, land_ref, send_sems, recv_sems, after_ref, src_out, land_out):
        sends, lands = plan(src_ref, land_ref, send_sems, recv_sems)
        for cp in sends:
            cp.wait_send()
        for cp in lands:
            cp.wait_recv()

    return pl.pallas_call(
        body, name=name,
        out_shape=(pltpu.HBM(src_thru.shape, src_thru.dtype), pltpu.HBM(land_thru.shape, land_thru.dtype)),
        in_specs=(HBM, HBM, SEM, SEM, ANY), out_specs=(HBM, HBM), input_output_aliases={0: 0, 1: 1},
        compiler_params=pltpu.CompilerParams(has_side_effects=_EFFECT),
    )(src_thru, land_thru, send_sems, recv_sems, after)


def _tie(x, token, name):
    def body(x_ref, t_ref, o_ref):
        pass

    return pl.pallas_call(
        body, name=name, in_specs=[ANY, pl.BlockSpec(memory_space=pltpu.VMEM)], out_specs=ANY,
        out_shape=jax.ShapeDtypeStruct(x.shape, x.dtype), input_output_aliases={0: 0},
    )(x, token)


def _swap_sibling(p, name):
    def body(p_ref, land_ref, send_sem, recv_sem):
        x, y, c = _coords()
        cp = pltpu.make_async_remote_copy(
            src_ref=p_ref, dst_ref=land_ref, send_sem=send_sem, recv_sem=recv_sem,
            device_id=(x, y, 1 - c), device_id_type=MESH)
        cp.start()
        cp.wait()

    return pl.pallas_call(
        body, name=name, in_specs=[ANY], out_specs=ANY, out_shape=jax.ShapeDtypeStruct(p.shape, p.dtype),
        scratch_shapes=[pltpu.SemaphoreType.DMA, pltpu.SemaphoreType.DMA],
        compiler_params=pltpu.CompilerParams(has_side_effects=True),
    )(p)


_ADD_BYTES = 7 << 19


def _add_tile(rows, cols):
    best = 32
    for t in range(32, rows + 1, 32):
        if rows % t == 0 and t * cols * 4 <= _ADD_BYTES:
            best = t
    return best


def _add_slabs(pack, land, me, name):
    rows, cols = pack.shape[1:]
    tr = _add_tile(rows, cols)

    def body(me_ref, p_ref, l_ref, o_ref):
        f = lambda r: r.astype(F32)
        o_ref[...] = ((f(p_ref[0]) + f(l_ref[0])) + f(l_ref[1])) + f(l_ref[2])

    return pl.pallas_call(
        body, name=name,
        grid_spec=pltpu.PrefetchScalarGridSpec(
            num_scalar_prefetch=1, grid=(rows // tr,),
            in_specs=[pl.BlockSpec((1, tr, cols), lambda i, me_ref: (me_ref[0], i, 0)),
                      pl.BlockSpec((_NCOPY, tr, cols), lambda i, me_ref: (0, i, 0))],
            out_specs=pl.BlockSpec((tr, cols), lambda i, me_ref: (i, 0))),
        out_shape=jax.ShapeDtypeStruct((rows, cols), F32),
        compiler_params=_params(("parallel",)),
    )(me, pack, land)


def _add_pair(a, b, name):
    rows, cols = a.shape
    tr = _add_tile(rows, cols)

    def body(a_ref, b_ref, o_ref):
        o_ref[...] = a_ref[...] + b_ref[...]

    blk = pl.BlockSpec((tr, cols), lambda i: (i, 0))
    return pl.pallas_call(
        body, name=name, grid=(rows // tr,), in_specs=[blk, blk], out_specs=blk,
        out_shape=jax.ShapeDtypeStruct((rows, cols), F32), compiler_params=_params(("parallel",)),
    )(a, b)


_STAGE_W = 1024


def _stage_rows(shapes):
    pieces, r = [], 0
    for i, (k, w) in enumerate(shapes):
        for a in range(k):
            for q in range(0, w, _STAGE_W):
                pieces.append((i, a, q, min(_STAGE_W, w - q), r))
                r += 1
    return pieces, -(-r // 8) * 8


def _gather8(parts, reduce, name):
    shapes = [p.shape for p in parts]
    pieces, rows = _stage_rows(shapes)
    n = len(parts)

    def body(*refs):
        ins, outs = refs[:n], refs[n:2 * n]
        stage, buf, res, send_sems, recv_sems = refs[2 * n:]
        x, y, c = _coords()
        me = 4 * x + 2 * y + c
        stage[...] = jnp.zeros_like(stage)
        for i, a, q, w, r in pieces:
            stage[r:r + 1, 0:w] = ins[i][a:a + 1, q:q + w]
        buf[pl.ds(me, 1)] = stage[...][None]
        cps, lands = [], []
        for k in range(1, NDEV):
            peer = (1 - x if k & 4 else x, 1 - y if k & 2 else y, 1 - c if k & 1 else c)

            def copy(slot):
                return pltpu.make_async_remote_copy(
                    src_ref=stage, dst_ref=buf.at[slot], send_sem=send_sems.at[k - 1],
                    recv_sem=recv_sems.at[k - 1], device_id=peer, device_id_type=MESH)

            cps.append(copy(me))
            lands.append(copy(4 * peer[0] + 2 * peer[1] + peer[2]))
        for cp in cps:
            cp.start()
        for cp, land in zip(cps, lands):
            land.wait_recv()
            cp.wait_send()
        if reduce:
            acc = buf[0]
            for d in range(1, NDEV):
                acc = acc + buf[d]
            res[...] = acc
            for i, a, q, w, r in pieces:
                outs[i][a:a + 1, q:q + w] = res[r:r + 1, 0:w]
        else:
            for i, a, q, w, r in pieces:
                for s in range(NCHIP):
                    outs[i][s, a:a + 1, q:q + w] = buf[2 * s, r:r + 1, 0:w]

    vm = pl.BlockSpec(memory_space=pltpu.VMEM)
    out_shapes = [jax.ShapeDtypeStruct(s if reduce else (NCHIP,) + s, F32) for s in shapes]
    return pl.pallas_call(
        body, name=name, in_specs=[vm] * n, out_specs=[vm] * n, out_shape=out_shapes,
        scratch_shapes=[pltpu.VMEM((rows, _STAGE_W), F32), pltpu.VMEM((NDEV, rows, _STAGE_W), F32),
                        pltpu.VMEM((rows, _STAGE_W), F32), pltpu.SemaphoreType.DMA((NDEV - 1,)),
                        pltpu.SemaphoreType.DMA((NDEV - 1,))],
        compiler_params=pltpu.CompilerParams(has_side_effects=True),
    )(*parts)


def _adamw_update(w_ref, g_ref, m_ref, v_ref, d_ref, mo_ref, vo_ref):
    c1 = 1.0 / (1.0 - ADAM_B1 ** ADAM_STEP)
    c2 = 1.0 / (1.0 - ADAM_B2 ** ADAM_STEP)
    gv = g_ref[...]
    mn = ADAM_B1 * m_ref[...] + (1.0 - ADAM_B1) * gv
    vn = ADAM_B2 * v_ref[...] + (1.0 - ADAM_B2) * (gv * gv)
    d_ref[...] = -ADAM_LR * ((mn * c1) / (jnp.sqrt(vn * c2) + ADAM_EPS) + ADAM_WD * w_ref[...])
    mo_ref[...] = mn
    vo_ref[...] = vn


def _adamw_small(ws, gs, ms, vs):
    n = len(ws)

    def body(*refs):
        for i in range(n):
            _adamw_update(*(refs[j * n + i] for j in range(7)))

    vm = pl.BlockSpec(memory_space=pltpu.VMEM)
    outs = pl.pallas_call(
        body, name="adamw_small", in_specs=[vm] * (4 * n), out_specs=[vm] * (3 * n),
        out_shape=[jax.ShapeDtypeStruct(w.shape, F32) for w in ws] * 3,
    )(*ws, *gs, *ms, *vs)
    return outs[:n], outs[n:2 * n], outs[2 * n:]


def _adamw(w, g, m, v, name):
    rows, cols = w.shape
    tr = rows
    while tr * cols * 4 > (3 << 19) and tr % 16 == 0:
        tr //= 2

    def body(*refs):
        _adamw_update(*refs)

    blk = pl.BlockSpec((tr, cols), lambda i: (i, 0))
    return pl.pallas_call(
        body, name=name, grid=(rows // tr,), in_specs=[blk] * 4, out_specs=[blk] * 3,
        out_shape=[jax.ShapeDtypeStruct((rows, cols), F32)] * 3, compiler_params=_params(("parallel",)),
    )(w, g, m, v)


def _rows128(a, mult=8):
    flat = a.reshape(-1)
    n = -(-flat.shape[0] // (LANES * mult)) * LANES * mult
    return jnp.pad(flat, (0, n - flat.shape[0])).reshape(-1, LANES)


def _pack_rows(parts, total_rows):
    rows = sum(p.shape[0] for p in parts)
    if total_rows > rows:
        parts = list(parts) + [jnp.zeros((total_rows - rows, LANES), parts[0].dtype)]
    return jnp.concatenate(parts, axis=0)


def _unpack_rows(pack, shapes, mult=8):
    out, r = [], 0
    for shp in shapes:
        n = int(np.prod(shp))
        nr = -(-n // (LANES * mult)) * mult
        out.append(pack[r:r + nr].reshape(-1)[:n].reshape(shp))
        r += nr
    return out


def _unpack_full(full, group):
    out, r = {}, 0
    for (name, rr, cc, axis), nr in zip(group, _slab_rows(group)):
        seg = full[:, r:r + nr].reshape(NCHIP, rr, cc)
        out[name] = seg.reshape(NCHIP * rr, cc) if axis == 0 else seg.transpose(1, 0, 2).reshape(rr, NCHIP * cc)
        r += nr
    return out


def _by_chip(g, rr, cc, axis):
    return g.reshape(NCHIP, rr, cc) if axis == 0 else g.reshape(rr, NCHIP, cc).transpose(1, 0, 2)


def _pack_by_chip(grads, group, dtype):
    parts = [_by_chip(grads[name].astype(dtype), rr, cc, axis).reshape(NCHIP, nr, LANES)
             for (name, rr, cc, axis), nr in zip(group, _slab_rows(group))]
    return jnp.concatenate(parts, axis=1)


_SMALL_REPL = ("norm_mix_w", "ssd_conv_b", "dt_bias", "a_log", "d_skip", "ssd_norm_w", "norm_ffn_w",
               "ffn_conv_b", "final_norm_w")
_SMALL_CONV = (("conv_a_w", 3, D), ("ssd_conv_w", 4, DX), ("ffn_conv_w", 3, FF))


def kernel(x, norm_mix_w, w_in, conv_a_w, w_a_out, ssd_conv_w, ssd_conv_b, dt_bias, a_log, d_skip, ssd_norm_w, w_s_out, w_o, norm_ffn_w, w_up, ffn_conv_w, ffn_conv_b, w_down, final_norm_w, loss_target, m_norm_mix_w, m_w_in, m_conv_a_w, m_w_a_out, m_ssd_conv_w, m_ssd_conv_b, m_dt_bias, m_a_log, m_d_skip, m_ssd_norm_w, m_w_s_out, m_w_o, m_norm_ffn_w, m_w_up, m_ffn_conv_w, m_ffn_conv_b, m_w_down, m_final_norm_w, v_norm_mix_w, v_w_in, v_conv_a_w, v_w_a_out, v_ssd_conv_w, v_ssd_conv_b, v_dt_bias, v_a_log, v_d_skip, v_ssd_norm_w, v_w_s_out, v_w_o, v_norm_ffn_w, v_w_up, v_ffn_conv_w, v_ffn_conv_b, v_w_down, v_final_norm_w):
    names = ("norm_mix_w", "w_in", "conv_a_w", "w_a_out", "ssd_conv_w", "ssd_conv_b", "dt_bias", "a_log", "d_skip",
             "ssd_norm_w", "w_s_out", "w_o", "norm_ffn_w", "w_up", "ffn_conv_w", "ffn_conv_b", "w_down", "final_norm_w")
    W = dict(zip(names, (norm_mix_w, w_in, conv_a_w, w_a_out, ssd_conv_w, ssd_conv_b, dt_bias, a_log, d_skip,
                         ssd_norm_w, w_s_out, w_o, norm_ffn_w, w_up, ffn_conv_w, ffn_conv_b, w_down, final_norm_w)))
    M = dict(zip(names, (m_norm_mix_w, m_w_in, m_conv_a_w, m_w_a_out, m_ssd_conv_w, m_ssd_conv_b, m_dt_bias, m_a_log,
                         m_d_skip, m_ssd_norm_w, m_w_s_out, m_w_o, m_norm_ffn_w, m_w_up, m_ffn_conv_w, m_ffn_conv_b,
                         m_w_down, m_final_norm_w)))
    V = dict(zip(names, (v_norm_mix_w, v_w_in, v_conv_a_w, v_w_a_out, v_ssd_conv_w, v_ssd_conv_b, v_dt_bias, v_a_log,
                         v_d_skip, v_ssd_norm_w, v_w_s_out, v_w_o, v_norm_ffn_w, v_w_up, v_ffn_conv_w, v_ffn_conv_b,
                         v_w_down, v_final_norm_w)))
    two_d = lambda a: a.reshape(-1, a.shape[-1])
    W2, M2, V2 = ({k: two_d(a) for k, a in t.items()} for t in (W, M, V))
    xi, yi, ci = _coords()
    me = 2 * xi + yi

    meidx = me.reshape(1).astype(jnp.int32)
    state = {}


    class Hooks(_Hooks):
        def after_norm(self, u):
            return _tie(u, state["rest_token"], "tie_ag_rest")

        def late_weights(self, wts, after):
            own, land = _split_wait("ag_rest_wait", state["rest"], after, _plan_bcast)
            full = _unpack_full(lax.dynamic_update_slice(land, own[None], (me, 0, 0)), _W_REST)
            return {**wts, **full}

        def grads_ready(self, grads, tie):
            if "w_in" in grads:
                key = "g_in"
                pack = _by_chip(_unpermute_w_in(grads["w_in"]).astype(BF16), *_W_IN[0][1:])
            else:
                key, pack = "g_rest", _pack_by_chip(grads, _W_REST, BF16)
            land = lax.empty((_NCOPY,) + pack.shape[1:], BF16)
            state[key], token = _split_start("rs_" + key + "_start", pack, land, _plan_scatter)
            return _tie(tie, token, "tie_" + key)

        def mark(self, name, value):
            state[name] = value

    def reduced(key, after):
        pack, land = _split_wait("rs_" + key + "_wait", state[key], after, _plan_scatter)
        mine = _add_slabs(pack, land, meidx, "rs_" + key + "_add_chips")
        return _add_pair(mine, _swap_sibling(mine, "rs_" + key + "_swap"), "rs_" + key + "_add_cores")

    w_in_full = _ag_weights(W2["w_in"].astype(BF16)).transpose(1, 0, 2).reshape(D, NI)
    rest_slab = _pack_rows([_rows128(W2[n], 16) for n, *_ in _W_REST], 0).astype(BF16)
    state["rest"], state["rest_token"] = _split_start(
        "ag_rest_start", rest_slab, lax.empty((NCHIP,) + rest_slab.shape, BF16), _plan_bcast)
    wts = {k: W2[k] for k in _SMALL_REPL}
    conv_by_chip = _gather8([W2[n] for n, *_ in _SMALL_CONV], False, "ag_conv_weights")
    for (n, kk, width), stacked in zip(_SMALL_CONV, conv_by_chip):
        wts[n] = stacked.transpose(1, 0, 2).reshape(kk, width)
    wts["w_in"] = _permute_w_in(w_in_full)

    loss8, grad_x, grads = _local_step(x[0], loss_target[0], wts, Hooks())

    gbig = dict(zip([n for n, *_ in _W_REST],
                    _unpack_rows(reduced("g_rest", state["ssd_bwd"]), [(rr, cc) for _, rr, cc, _ in _W_REST], 16)))
    gbig["w_in"] = reduced("g_in", grad_x)

    small_parts = [grads[n] for n in _SMALL_REPL] + [loss8[0:1]] + [grads[n] for n, *_ in _SMALL_CONV]
    small_g = _gather8(small_parts, True, "allreduce_small")
    gsm = dict(zip(_SMALL_REPL, small_g[:len(_SMALL_REPL)]))
    loss = small_g[len(_SMALL_REPL)][0, 0]
    for (n, kk, width), gfull in zip(_SMALL_CONV, small_g[len(_SMALL_REPL) + 1:]):
        cw = width // NCHIP
        gsm[n] = lax.dynamic_slice(gfull, (0, me * cw), (kk, cw))

    G, DW, NM, NV = {}, {}, {}, {}
    for n in [b[0] for b in _W_IN + _W_REST]:
        G[n] = gbig[n]
        DW[n], NM[n], NV[n] = _adamw(W2[n], G[n], M2[n], V2[n], "adamw_" + n)
    sm_names = list(_SMALL_REPL) + [n for n, *_ in _SMALL_CONV]
    outs = _adamw_small(*([t[n] for n in sm_names] for t in (W2, gsm, M2, V2)))
    for t, vals in zip((DW, NM, NV), outs):
        t.update(zip(sm_names, vals))
    G.update(gsm)

    def shaped(t):
        return [t[n].reshape(W[n].shape) for n in names]

    return (loss, grad_x.reshape(x.shape), *shaped(G), *shaped(DW), *shaped(NM), *shaped(NV))
```

```python
import functools

import jax
import jax.numpy as jnp
import numpy as np
from jax import lax
from jax.experimental import pallas as pl
from jax.experimental.pallas import tpu as pltpu

F32 = jnp.float32
BF16 = jnp.bfloat16

D = 1024
DI = 2048
NH = 32
HP = 64
NG = 4
NS = 128
CH = 128
DX = 3072
FF = 2816
NI = 10272
EPS = 1e-5

OFF_BCV, OFF_XBC, OFF_G, OFF_Z, OFF_DT = 0, 3072, 6144, 8192, 10240
NIP = 10368
_SEGS = ((0, 2048, OFF_G), (2048, 3072, OFF_BCV), (5120, 2048, OFF_Z), (7168, 3072, OFF_XBC), (10240, 32, OFF_DT))

LANES = 128
HALO = 16
V7X_VMEM_LIMIT = 56 * 2 ** 20

ADAM_LR, ADAM_B1, ADAM_B2, ADAM_EPS, ADAM_WD, ADAM_STEP = 0.001, 0.9, 0.999, 1e-08, 0.01, 10

NN = (((1,), (0,)), ((), ()))
NT = (((1,), (1,)), ((), ()))
TN = (((0,), (0,)), ((), ()))


def _dot(a, b, dims=NN):
    return lax.dot_general(a, b, dims, preferred_element_type=F32)


def _params(sem, **kw):
    return pltpu.CompilerParams(dimension_semantics=sem, vmem_limit_bytes=V7X_VMEM_LIMIT, **kw)


def _pick(dim, cap):
    if dim <= cap:
        return dim
    best = None
    for t in range(LANES, cap + 1, LANES):
        if dim % t == 0:
            best = t
    assert best is not None, (dim, cap)
    return best


def _sigmoid(x):
    return 1.0 / (1.0 + jnp.exp(-x))


def _matmul(a, b, *, mode, out_dtype, name, residual=None, caps=(1408, 1408, 1408)):
    if mode == "nn":
        (M, K), (K2, N) = a.shape, b.shape
    elif mode == "nt":
        (M, K), (N, K2) = a.shape, b.shape
    else:
        (K, M), (K2, N) = a.shape, b.shape
    assert K == K2, (name, a.shape, b.shape)
    tm, tn, tk = _pick(M, caps[0]), _pick(N, caps[1]), _pick(K, 2048 if mode == "tn" else caps[2])
    nk = K // tk
    if mode == "tn":
        a_spec = pl.BlockSpec((tk, tm), lambda i, j, k: (k, i))
    else:
        a_spec = pl.BlockSpec((tm, tk), lambda i, j, k: (i, k))
    if mode == "nt":
        b_spec = pl.BlockSpec((tn, tk), lambda i, j, k: (j, k))
    else:
        b_spec = pl.BlockSpec((tk, tn), lambda i, j, k: (k, j))
    dims = {"nn": NN, "nt": NT, "tn": TN}[mode]
    o_spec = pl.BlockSpec((tm, tn), lambda i, j, k: (i, j))
    has_res = residual is not None

    def body(*refs):
        a_ref, b_ref = refs[:2]
        r_ref = refs[2] if has_res else None
        o_ref = refs[3 if has_res else 2]
        acc_ref = refs[-1]
        k = pl.program_id(2)
        part = _dot(a_ref[...], b_ref[...], dims)

        def finish(r):
            if has_res:
                r = r + r_ref[...].astype(F32)
            o_ref[...] = r.astype(out_dtype)

        if nk == 1:
            finish(part)
            return

        @pl.when(k == 0)
        def _():
            acc_ref[...] = part

        @pl.when(jnp.logical_and(k > 0, k < nk - 1))
        def _():
            acc_ref[...] += part

        @pl.when(k == nk - 1)
        def _():
            finish(acc_ref[...] + part)

    in_specs = [a_spec, b_spec] + ([o_spec] if has_res else [])
    args = (a, b) + ((residual,) if has_res else ())
    return pl.pallas_call(
        body, name=name, grid=(M // tm, N // tn, nk), in_specs=in_specs, out_specs=o_spec,
        out_shape=jax.ShapeDtypeStruct((M, N), out_dtype),
        scratch_shapes=[pltpu.VMEM((tm, tn), F32)] if nk > 1 else [],
        compiler_params=_params(("parallel", "parallel", "arbitrary")),
    )(*args)


class _Rows:
    def __init__(self, T, tm):
        self.T, self.tm = T, min(tm, T // 2)
        self.nrow = T // self.tm
        self.r = self.tm // HALO
        self.nb = T // HALO

    def tile(self, w, cb=0, step=1):
        return pl.BlockSpec((self.tm, w), lambda j, i: (i, cb + step * j))

    def prev(self, w, cb=0, step=1):
        r = self.r
        return pl.BlockSpec((HALO, w), lambda j, i: (jnp.maximum(i * r - 1, 0), cb + step * j))

    def next(self, w, cb=0, step=1):
        r, nb = self.r, self.nb
        return pl.BlockSpec((HALO, w), lambda j, i: (jnp.minimum((i + 1) * r, nb - 1), cb + step * j))

    def colvec(self, k, w, cb=0, step=1):
        return pl.BlockSpec((k, w), lambda j, i: (0, cb + step * j))

    def call(self, body, name, ncol, in_specs, out_specs, out_shape, args, aliases=None):
        return pl.pallas_call(
            body, name=name, grid=(ncol, self.nrow), in_specs=in_specs, out_specs=out_specs,
            out_shape=out_shape, input_output_aliases=aliases or {},
            compiler_params=_params(("parallel", "arbitrary")),
        )(*args)


ANY = pl.BlockSpec(memory_space=pl.ANY)


def _shifts_causal(ext, nk, tm):
    out = []
    for k in range(nk):
        s = nk - 1 - k
        r = ext if s == 0 else pltpu.roll(ext, s, 0)
        out.append(r[HALO:])
    return out


def _shifts_anticausal(ext, nk, tm):
    n = ext.shape[0]
    out = []
    for k in range(nk):
        s = nk - 1 - k
        r = ext if s == 0 else pltpu.roll(ext, n - s, 0)
        out.append(r[:tm])
    return out


def _wsum(w, parts):
    acc = w[0:1, :] * parts[0]
    for k in range(1, len(parts)):
        acc = acc + w[k:k + 1, :] * parts[k]
    return acc


def _colsum(x):
    return jnp.sum(x, axis=0, keepdims=True)


def _acc_out(ref, val, first):
    @pl.when(first)
    def _():
        ref[...] = val

    @pl.when(jnp.logical_not(first))
    def _():
        ref[...] += val


def _acc_rows(ref, rows, first):
    for k, r in enumerate(rows):
        _acc_out(ref.at[k:k + 1, :], r, first)


def _rmsnorm_fwd(x, w, name):
    T = x.shape[0]
    R = _Rows(T, 512)

    def body(x_ref, w_ref, o_ref):
        xv = x_ref[...]
        r = lax.rsqrt(jnp.mean(xv * xv, axis=-1, keepdims=True) + EPS)
        o_ref[...] = (xv * r * w_ref[...]).astype(BF16)

    return R.call(body, name, 1, [R.tile(D), R.colvec(1, D)], R.tile(D),
                  jax.ShapeDtypeStruct((T, D), BF16), (x, w))


def _rmsnorm_bwd(dy, x, w, dres, name):
    T = x.shape[0]
    R = _Rows(T, 512)

    def body(dy_ref, x_ref, w_ref, dr_ref, dx_ref, dxb_ref, dw_ref):
        xv = x_ref[...]
        r = lax.rsqrt(jnp.mean(xv * xv, axis=-1, keepdims=True) + EPS)
        xh = xv * r
        dyv = dy_ref[...].astype(F32)
        dxh = dyv * w_ref[...]
        dx = r * (dxh - xh * jnp.mean(dxh * xh, axis=-1, keepdims=True)) + dr_ref[...]
        dx_ref[...] = dx
        dxb_ref[...] = dx.astype(BF16)
        _acc_out(dw_ref, _colsum(dyv * xh), pl.program_id(1) == 0)

    return R.call(body, name, 1, [R.tile(D), R.tile(D), R.colvec(1, D), R.tile(D)],
                  [R.tile(D), R.tile(D), R.colvec(1, D)],
                  [jax.ShapeDtypeStruct((T, D), F32), jax.ShapeDtypeStruct((T, D), BF16),
                   jax.ShapeDtypeStruct((1, D), F32)],
                  (dy, x, w, dres))


def _branch_a_fwd(proj, conv_w):
    T = proj.shape[0]
    R = _Rows(T, 512)
    tm = R.tm

    def body(p_ref, pp_ref, w_ref, o_ref):
        keep = (pl.program_id(1) > 0).astype(F32)
        cv = p_ref[:, D:2 * D].astype(F32) * p_ref[:, 2 * D:].astype(F32)
        cvp = pp_ref[:, D:2 * D].astype(F32) * pp_ref[:, 2 * D:].astype(F32) * keep
        sh = _shifts_causal(jnp.concatenate([cvp, cv], axis=0), 3, tm)
        ca = _wsum(w_ref[...], sh)
        o_ref[...] = (p_ref[:, :D].astype(F32) * ca).astype(BF16)

    return R.call(body, "branch_a_fwd", 1, [R.tile(3 * D), R.prev(3 * D), R.colvec(3, D)], R.tile(D),
                  jax.ShapeDtypeStruct((T, D), BF16), (proj, proj, conv_w))


def _branch_a_bwd(dya_in, proj, conv_w, dproj):
    T = proj.shape[0]
    R = _Rows(T, 256)
    tm = R.tm

    def body(d_ref, dn_ref, p_ref, pp_ref, pn_ref, w_ref, _alias, o_ref, dw_ref):
        i = pl.program_id(1)
        keep_p = (i > 0).astype(F32)
        keep_n = (i < R.nrow - 1).astype(F32)
        w = w_ref[...]
        b = p_ref[:, :D].astype(F32)
        c = p_ref[:, D:2 * D].astype(F32)
        v = p_ref[:, 2 * D:].astype(F32)
        cvp = pp_ref[:, D:2 * D].astype(F32) * pp_ref[:, 2 * D:].astype(F32) * keep_p
        sh = _shifts_causal(jnp.concatenate([cvp, c * v], axis=0), 3, tm)
        ca = _wsum(w, sh)
        d = d_ref[...].astype(F32)
        dca = d * b
        dca_n = dn_ref[...].astype(F32) * pn_ref[:, :D].astype(F32) * keep_n
        dsh = _shifts_anticausal(jnp.concatenate([dca, dca_n], axis=0), 3, tm)
        dcv = _wsum(w, dsh)
        o_ref[:, :D] = (d * ca).astype(BF16)
        o_ref[:, D:2 * D] = (dcv * v).astype(BF16)
        o_ref[:, 2 * D:] = (dcv * c).astype(BF16)
        _acc_rows(dw_ref, [_colsum(dca * s) for s in sh], i == 0)

    return R.call(
        body, "branch_a_bwd", 1,
        [R.tile(D), R.next(D), R.tile(3 * D), R.prev(3 * D), R.next(3 * D), R.colvec(3, D), ANY],
        [R.tile(3 * D), R.colvec(3, D)],
        [jax.ShapeDtypeStruct(dproj.shape, BF16), jax.ShapeDtypeStruct((3, D), F32)],
        (dya_in, dya_in, proj, proj, proj, conv_w, dproj), aliases={6: 0})


_XW = 512


def _xbc_fwd(proj, conv_w, conv_b):
    T = proj.shape[0]
    R = _Rows(T, 512)
    tm = R.tm
    cb = OFF_XBC // _XW

    def body(x_ref, xp_ref, w_ref, b_ref, o_ref):
        keep = (pl.program_id(1) > 0).astype(F32)
        ext = jnp.concatenate([xp_ref[...].astype(F32) * keep, x_ref[...].astype(F32)], axis=0)
        pre = _wsum(w_ref[...], _shifts_causal(ext, 4, tm)) + b_ref[...]
        o_ref[...] = (pre * _sigmoid(pre)).astype(BF16)

    return R.call(body, "xbc_fwd", DX // _XW,
                  [R.tile(_XW, cb), R.prev(_XW, cb), R.colvec(4, _XW), R.colvec(1, _XW)], R.tile(_XW),
                  jax.ShapeDtypeStruct((T, DX), BF16), (proj, proj, conv_w, conv_b))


def _xbc_bwd(dact, proj, conv_w, conv_b, dproj):
    T = proj.shape[0]
    R = _Rows(T, 512)
    tm = R.tm
    cb = OFF_XBC // _XW

    def body(d_ref, dn_ref, x_ref, xp_ref, xn_ref, w_ref, b_ref, _alias, o_ref, dw_ref, db_ref):
        i = pl.program_id(1)
        keep_p = (i > 0).astype(F32)
        keep_n = (i < R.nrow - 1).astype(F32)
        w = w_ref[...]
        ext = jnp.concatenate([xp_ref[...].astype(F32) * keep_p, x_ref[...].astype(F32),
                               xn_ref[...].astype(F32)], axis=0)
        sh = _shifts_causal(ext, 4, tm + HALO)
        pre = _wsum(w, sh) + b_ref[...]
        s = _sigmoid(pre)
        dsilu = s * (1.0 + pre * (1.0 - s))
        dext = jnp.concatenate([d_ref[...].astype(F32), dn_ref[...].astype(F32) * keep_n], axis=0)
        dpre = dext * dsilu
        dsh = _shifts_anticausal(dpre, 4, tm)
        o_ref[...] = _wsum(w, dsh).astype(BF16)
        dp = dpre[:tm]
        _acc_rows(dw_ref, [_colsum(dp * q[:tm]) for q in sh], i == 0)
        _acc_out(db_ref, _colsum(dp), i == 0)

    return R.call(
        body, "xbc_bwd", DX // _XW,
        [R.tile(_XW), R.next(_XW), R.tile(_XW, cb), R.prev(_XW, cb), R.next(_XW, cb),
         R.colvec(4, _XW), R.colvec(1, _XW), ANY],
        [R.tile(_XW, cb), R.colvec(4, _XW), R.colvec(1, _XW)],
        [jax.ShapeDtypeStruct(dproj.shape, BF16), jax.ShapeDtypeStruct((4, DX), F32),
         jax.ShapeDtypeStruct((1, DX), F32)],
        (dact, dact, proj, proj, proj, conv_w, conv_b, dproj), aliases={7: 0})


def _softplus(x):
    return jnp.maximum(x, 0.0) + jnp.log(1.0 + jnp.exp(-jnp.abs(x)))


def _dt_fwd(dt_raw, dt_bias_p, a_log_p):
    T = dt_raw.shape[0]

    def body(r_ref, b_ref, al_ref, dt_ref, ac_ref, acT_ref):
        dt = _softplus(r_ref[...] + b_ref[...])
        s = dt * (-jnp.exp(al_ref[...]))
        row = lax.broadcasted_iota(jnp.int32, (CH, LANES), 0)
        k = 1
        while k < CH:
            s = s + jnp.where(row >= k, pltpu.roll(s, k, 0), 0.0)
            k *= 2
        dt_ref[...] = dt
        ac_ref[...] = s
        acT_ref[...] = s.T

    blk = pl.BlockSpec((CH, LANES), lambda i: (i, 0))
    vec = pl.BlockSpec((1, LANES), lambda i: (0, 0))
    return pl.pallas_call(
        body, name="dt_fwd", grid=(T // CH,), in_specs=[blk, vec, vec], out_specs=[blk, blk, blk],
        out_shape=[jax.ShapeDtypeStruct((T, LANES), F32)] * 3, compiler_params=_params(("parallel",)),
    )(dt_raw, dt_bias_p, a_log_p)


def _dt_bwd(dacum, ddt_x, dt_raw, dt_bias_p, a_log_p, dproj):
    T = dt_raw.shape[0]
    nc = T // CH

    def body(da_ref, dx_ref, r_ref, b_ref, al_ref, _alias, o_ref, db_ref, dal_ref):
        i = pl.program_id(0)
        a = -jnp.exp(al_ref[...])
        z = r_ref[...] + b_ref[...]
        dt = _softplus(z)
        s = da_ref[...]
        row = lax.broadcasted_iota(jnp.int32, (CH, LANES), 0)
        k = 1
        while k < CH:
            s = s + jnp.where(row < CH - k, pltpu.roll(s, CH - k, 0), 0.0)
            k *= 2
        ddt = s * a + dx_ref[...]
        draw = ddt * _sigmoid(z)
        o_ref[...] = draw.astype(BF16)
        _acc_out(db_ref, _colsum(draw), i == 0)
        _acc_out(dal_ref, _colsum(s * dt), i == 0)

        @pl.when(i == nc - 1)
        def _():
            dal_ref[...] = dal_ref[...] * a

    blk = pl.BlockSpec((CH, LANES), lambda i: (i, 0))
    vec = pl.BlockSpec((1, LANES), lambda i: (0, 0))
    oblk = pl.BlockSpec((CH, LANES), lambda i: (i, OFF_DT // LANES))
    return pl.pallas_call(
        body, name="dt_bwd", grid=(nc,), in_specs=[blk, blk, blk, vec, vec, ANY], out_specs=[oblk, vec, vec],
        out_shape=[jax.ShapeDtypeStruct(dproj.shape, BF16), jax.ShapeDtypeStruct((1, LANES), F32),
                   jax.ShapeDtypeStruct((1, LANES), F32)],
        input_output_aliases={5: 0}, compiler_params=_params(("arbitrary",)),
    )(dacum, ddt_x, dt_raw, dt_bias_p, a_log_p, dproj)


_GW = DI // NG
_HG = NH // NG
_NEG = -1e30


def _pair_lanes(left, v0, v1):
    return jnp.where(left, v0, v1)


def _ssd_specs(T, rev):
    nc = T // CH
    cm = (lambda c: nc - 1 - c) if rev else (lambda c: c)
    bw = NG * NS
    return dict(
        xs=pl.BlockSpec((CH, DI), lambda c: (cm(c), 0)),
        bm=pl.BlockSpec((CH, bw), lambda c: (cm(c), DI // bw)),
        cmat=pl.BlockSpec((CH, bw), lambda c: (cm(c), DI // bw + 1)),
        xbc=pl.BlockSpec((CH, DX), lambda c: (cm(c), 0)),
        col=pl.BlockSpec((CH, LANES), lambda c: (cm(c), 0)),
        dsk=pl.BlockSpec((1, DI), lambda c: (0, 0)),
        state=pl.BlockSpec((1, NS, DI), lambda c: (cm(c), 0, 0)),
    )


def _last(ref, lo, hi):
    return ref.at[(slice(None),) * (len(ref.shape) - 1) + (slice(lo, hi),)]


def _group_views(g, wide, narrow):
    return [_last(r, g * _GW, (g + 1) * _GW) for r in wide] + [_last(r, g * NS, (g + 1) * NS) for r in narrow]


def _ssd_fwd(xact, dt, acum, acumT, dsk_rep):
    T = xact.shape[0]
    nc = T // CH
    sp = _ssd_specs(T, False)

    def body(*refs):
        xs, bm, cmat, dtr, acr, actr, dsk, y, spv, S_ref = refs

        @pl.when(pl.program_id(0) == 0)
        def _():
            S_ref[...] = jnp.zeros_like(S_ref)

        for g in range(NG):
            group(g * _HG, dtr[...], acr[...], actr[...], *_group_views(g, (xs, dsk, y, spv, S_ref), (bm, cmat)))

    def group(hb, dt, ac, acT, xs_ref, dsk_ref, y_ref, sp_ref, S_ref, b_ref, c_ref):
        Bm, Cm = b_ref[...], c_ref[...]
        S = S_ref[...]
        sp_ref[0] = S
        cb = _dot(Cm, Bm, NT)
        CS = _dot(Cm, S.astype(BF16))
        row = lax.broadcasted_iota(jnp.int32, (CH, CH), 0)
        col = lax.broadcasted_iota(jnp.int32, (CH, CH), 1)
        tril = row >= col
        left = col < HP
        xd_parts, dec_parts = [], []
        for p in range(_HG // 2):
            sl = slice(p * LANES, (p + 1) * LANES)
            j0, j1 = hb + 2 * p, hb + 2 * p + 1
            xp = xs_ref[:, sl].astype(F32)
            a0, a1 = ac[:, j0:j0 + 1], ac[:, j1:j1 + 1]
            al0, al1 = ac[CH - 1:CH, j0:j0 + 1], ac[CH - 1:CH, j1:j1 + 1]
            X = xp * _pair_lanes(left, dt[:, j0:j0 + 1], dt[:, j1:j1 + 1])
            Xb = X.astype(BF16)
            yd = jnp.zeros((CH, LANES), F32)
            for j, aj, mask in ((j0, a0, left), (j1, a1, jnp.logical_not(left))):
                Lm = jnp.exp(jnp.where(tril, aj - acT[j:j + 1, :], _NEG))
                W = (cb * Lm).astype(BF16)
                yd = yd + _dot(W, jnp.where(mask, Xb, jnp.zeros_like(Xb)))
            eal = _pair_lanes(left, jnp.exp(a0), jnp.exp(a1))
            y = yd + eal * CS[:, sl] + dsk_ref[:, sl] * xp
            y_ref[:, sl] = y.astype(BF16)
            xd_parts.append(X * _pair_lanes(left, jnp.exp(al0 - a0), jnp.exp(al1 - a1)))
            dec_parts.append(_pair_lanes(left[0:1], jnp.exp(al0), jnp.exp(al1)))
        Xd = jnp.concatenate(xd_parts, axis=1).astype(BF16)
        dec = jnp.concatenate(dec_parts, axis=1)
        S_ref[...] = dec * S + _dot(Bm, Xd, TN)

    return pl.pallas_call(
        body, name="ssd_fwd", grid=(nc,),
        in_specs=[sp["xs"], sp["bm"], sp["cmat"], sp["col"], sp["col"], sp["col"], sp["dsk"]],
        out_specs=[sp["xs"], sp["state"]],
        out_shape=[jax.ShapeDtypeStruct((T, DI), BF16), jax.ShapeDtypeStruct((nc, NS, DI), F32)],
        scratch_shapes=[pltpu.VMEM((NS, DI), F32)],
        compiler_params=_params(("arbitrary",)),
    )(xact, xact, xact, dt, acum, acumT, dsk_rep)


def _ssd_bwd(dy, xact, dt, acum, acumT, dsk_rep, sprev):
    T = xact.shape[0]
    nc = T // CH
    sp = _ssd_specs(T, True)

    def body(*refs):
        xs, bm, cmat, dtr, acr, actr, dsk, dyr, spv, dxa, ddtx, dAc, dskacc, dS_ref = refs
        first = pl.program_id(0) == 0

        @pl.when(first)
        def _():
            dS_ref[...] = jnp.zeros_like(dS_ref)

        dbc = _last(dxa, DI, DX)
        ddtx_sum = jnp.zeros((CH, LANES), F32)
        dAc_sum = jnp.zeros((CH, LANES), F32)
        for g in range(NG):
            a, b = group(first, g * _HG, dtr[...], acr[...], actr[...],
                         *_group_views(g, (xs, dsk, dyr, spv, dxa, dskacc, dS_ref),
                                       (bm, cmat, dbc, _last(dbc, NG * NS, 2 * NG * NS))))
            ddtx_sum, dAc_sum = ddtx_sum + a, dAc_sum + b
        ddtx[...] = ddtx_sum
        dAc[...] = dAc_sum

    def group(first, hb, dt, ac, acT, xs_ref, dsk_ref, dy_ref, sp_ref, dx_ref, dskacc_ref, dS_ref, b_ref, c_ref,
              dB_ref, dC_ref):
        Bm, Cm = b_ref[...], c_ref[...]
        S = sp_ref[0]
        dS = dS_ref[...]
        Sb, dSb = S.astype(BF16), dS.astype(BF16)
        cb = _dot(Cm, Bm, NT)
        cbT = _dot(Bm, Cm, NT)
        CmT = Cm.T
        CS = _dot(Cm, Sb)
        T1 = _dot(Bm, dSb)
        row = lax.broadcasted_iota(jnp.int32, (CH, CH), 0)
        col = lax.broadcasted_iota(jnp.int32, (CH, CH), 1)
        tril = row >= col
        triu = row <= col
        left = col < HP
        lane8 = lax.broadcasted_iota(jnp.int32, (1, LANES), 1)
        lastrow = lax.broadcasted_iota(jnp.int32, (CH, 1), 0) == CH - 1
        dCB = jnp.zeros((CH, CH), F32)
        dCBT = jnp.zeros((CH, CH), F32)
        dAc = jnp.zeros((CH, LANES), F32)
        ddtx = jnp.zeros((CH, LANES), F32)
        xd_parts, dye_parts, dec_parts, dsk_parts = [], [], [], []
        for p in range(_HG // 2):
            sl = slice(p * LANES, (p + 1) * LANES)
            j0, j1 = hb + 2 * p, hb + 2 * p + 1
            xp = xs_ref[:, sl].astype(F32)
            dyp = dy_ref[:, sl].astype(F32)
            a0, a1 = ac[:, j0:j0 + 1], ac[:, j1:j1 + 1]
            al0, al1 = ac[CH - 1:CH, j0:j0 + 1], ac[CH - 1:CH, j1:j1 + 1]
            dtl = _pair_lanes(left, dt[:, j0:j0 + 1], dt[:, j1:j1 + 1])
            X = xp * dtl
            Xb = X.astype(BF16)
            eal = _pair_lanes(left, jnp.exp(a0), jnp.exp(a1))
            dtel = _pair_lanes(left, jnp.exp(al0 - a0), jnp.exp(al1 - a1))
            T1p = T1[:, sl]
            dXd = jnp.zeros((CH, LANES), F32)
            Rm = T1p * dtel * X
            GR = dyp * (eal * CS[:, sl]) - Rm
            SdS = dS[:, sl] * S[:, sl]
            for j, aj, alj, mask in ((j0, a0, al0, left), (j1, a1, al1, jnp.logical_not(left))):
                arow = acT[j:j + 1, :]
                Lm = jnp.exp(jnp.where(tril, aj - arow, _NEG))
                LmT = jnp.exp(jnp.where(triu, arow - aj, _NEG))
                dYm = jnp.where(mask, dyp, 0.0).astype(BF16)
                dWm = _dot(dYm, Xb, NT)
                dWmT = _dot(Xb, dYm, NT)
                dCB = dCB + dWm * Lm
                dCBT = dCBT + dWmT * LmT
                WT = cbT * LmT
                dXd = dXd + _dot(WT.astype(BF16), dYm)
                qd = dWm * (cb * Lm) - dWmT * WT
                colv = jnp.sum(qd + jnp.where(mask, GR, 0.0), axis=1, keepdims=True)
                tot = jnp.where(mask, Rm + jnp.exp(alj) * SdS, 0.0)
                dalast = jnp.sum(jnp.sum(tot, axis=0, keepdims=True), axis=1, keepdims=True)
                dAc = dAc + (colv + jnp.where(lastrow, dalast, 0.0)) * (lane8 == j).astype(F32)
            dX = dXd + dtel * T1p
            dXx = dX * xp
            for j, mask in ((j0, left), (j1, jnp.logical_not(left))):
                dd = jnp.sum(jnp.where(mask, dXx, 0.0), axis=1, keepdims=True)
                ddtx = ddtx + dd * (lane8 == j).astype(F32)
            dx_ref[:, sl] = (dX * dtl + dsk_ref[:, sl] * dyp).astype(BF16)
            dsk_parts.append(_colsum(dyp * xp))
            xd_parts.append(X * dtel)
            dye_parts.append(dyp * eal)
            dec_parts.append(_pair_lanes(left[0:1], jnp.exp(al0), jnp.exp(al1)))
        Xd = jnp.concatenate(xd_parts, axis=1).astype(BF16)
        dYe = jnp.concatenate(dye_parts, axis=1).astype(BF16)
        dec = jnp.concatenate(dec_parts, axis=1)
        dC_ref[...] = (_dot(dCB.astype(BF16), Bm) + _dot(dYe, Sb, NT)).astype(BF16)
        dB_ref[...] = (_dot(dCBT.astype(BF16), Cm) + _dot(Xd, dSb, NT)).astype(BF16)
        dS_ref[...] = _dot(CmT, dYe) + dec * dS
        _acc_out(dskacc_ref, jnp.concatenate(dsk_parts, axis=1), first)
        return ddtx, dAc

    return pl.pallas_call(
        body, name="ssd_bwd", grid=(nc,),
        in_specs=[sp["xs"], sp["bm"], sp["cmat"], sp["col"], sp["col"], sp["col"], sp["dsk"], sp["xs"],
                  sp["state"]],
        out_specs=[sp["xbc"], sp["col"], sp["col"], sp["dsk"]],
        out_shape=[jax.ShapeDtypeStruct((T, DX), BF16), jax.ShapeDtypeStruct((T, LANES), F32),
                   jax.ShapeDtypeStruct((T, LANES), F32), jax.ShapeDtypeStruct((1, DI), F32)],
        scratch_shapes=[pltpu.VMEM((NS, DI), F32)],
        compiler_params=_params(("arbitrary",)),
    )(xact, xact, xact, dt, acum, acumT, dsk_rep, dy, sprev)


def _gnorm_fwd(y, proj, w):
    T = y.shape[0]
    R = _Rows(T, 1024)
    zb = OFF_Z // _GW

    def body(y_ref, z_ref, w_ref, o_ref):
        z = z_ref[...].astype(F32)
        yf = y_ref[...].astype(F32) * z * _sigmoid(z)
        r = lax.rsqrt(jnp.mean(yf * yf, axis=-1, keepdims=True) + EPS)
        o_ref[...] = (yf * r * w_ref[...]).astype(BF16)

    return R.call(body, "gnorm_fwd", NG, [R.tile(_GW), R.tile(_GW, zb), R.colvec(1, _GW)], R.tile(_GW),
                  jax.ShapeDtypeStruct((T, DI), BF16), (y, proj, w))


def _gnorm_bwd(dn, y, proj, w, dproj):
    T = y.shape[0]
    R = _Rows(T, 1024)
    zb = OFF_Z // _GW

    def body(dn_ref, y_ref, z_ref, w_ref, _alias, dz_ref, dy_ref, dw_ref):
        z = z_ref[...].astype(F32)
        yv = y_ref[...].astype(F32)
        s = _sigmoid(z)
        silu = z * s
        yf = yv * silu
        r = lax.rsqrt(jnp.mean(yf * yf, axis=-1, keepdims=True) + EPS)
        yh = yf * r
        dnv = dn_ref[...].astype(F32)
        dyh = dnv * w_ref[...]
        dyf = r * (dyh - yh * jnp.mean(dyh * yh, axis=-1, keepdims=True))
        dy_ref[...] = (dyf * silu).astype(BF16)
        dz_ref[...] = (dyf * yv * s * (1.0 + z * (1.0 - s))).astype(BF16)
        _acc_out(dw_ref, _colsum(dnv * yh), pl.program_id(1) == 0)

    return R.call(
        body, "gnorm_bwd", NG, [R.tile(_GW), R.tile(_GW), R.tile(_GW, zb), R.colvec(1, _GW), ANY],
        [R.tile(_GW, zb), R.tile(_GW), R.colvec(1, _GW)],
        [jax.ShapeDtypeStruct(dproj.shape, BF16), jax.ShapeDtypeStruct((T, DI), BF16),
         jax.ShapeDtypeStruct((1, DI), F32)],
        (dn, y, proj, w, dproj), aliases={4: 0})


def _merge_fwd(proj, ya, ys):
    T = proj.shape[0]
    R = _Rows(T, 512)
    gb = OFF_G // (2 * D)

    def body(g_ref, ya_ref, ys_ref, o_ref):
        ga = _sigmoid(g_ref[:, :D].astype(F32))
        gs = _sigmoid(g_ref[:, D:].astype(F32))
        o_ref[...] = (ga * ya_ref[...].astype(F32) + gs * ys_ref[...].astype(F32)).astype(BF16)

    return R.call(body, "merge_fwd", 1, [R.tile(2 * D, gb), R.tile(D), R.tile(D)], R.tile(D),
                  jax.ShapeDtypeStruct((T, D), BF16), (proj, ya, ys))


def _merge_bwd(dm, proj, ya, ys, ncols):
    T = proj.shape[0]
    R = _Rows(T, 256)
    gb = OFF_G // (2 * D)

    def body(dm_ref, g_ref, ya_ref, ys_ref, dg_ref, dya_ref, dys_ref):
        d = dm_ref[...].astype(F32)
        ga = _sigmoid(g_ref[:, :D].astype(F32))
        gs = _sigmoid(g_ref[:, D:].astype(F32))
        dya_ref[...] = (d * ga).astype(BF16)
        dys_ref[...] = (d * gs).astype(BF16)
        dg_ref[:, :D] = (d * ya_ref[...].astype(F32) * ga * (1.0 - ga)).astype(BF16)
        dg_ref[:, D:] = (d * ys_ref[...].astype(F32) * gs * (1.0 - gs)).astype(BF16)

    return R.call(
        body, "merge_bwd", 1, [R.tile(D), R.tile(2 * D, gb), R.tile(D), R.tile(D)],
        [R.tile(2 * D, gb), R.tile(D), R.tile(D)],
        [jax.ShapeDtypeStruct((T, ncols), BF16), jax.ShapeDtypeStruct((T, D), BF16),
         jax.ShapeDtypeStruct((T, D), BF16)],
        (dm, proj, ya, ys))


_FW = 1408
_FB = FF // _FW


def _ffn_act_fwd(hv, conv_w, conv_b):
    T = hv.shape[0]
    R = _Rows(T, 256)
    tm = R.tm

    def body(h1_ref, h1p_ref, h3_ref, w_ref, b_ref, o_ref):
        keep = (pl.program_id(1) > 0).astype(F32)
        ext = jnp.concatenate([h1p_ref[...].astype(F32) * keep, h1_ref[...].astype(F32)], axis=0)
        pre = _wsum(w_ref[...], _shifts_causal(ext, 3, tm)) + b_ref[...]
        o_ref[...] = (pre * _sigmoid(pre) * h3_ref[...].astype(F32)).astype(BF16)

    return R.call(body, "ffn_act_fwd", _FB,
                  [R.tile(_FW), R.prev(_FW), R.tile(_FW, _FB), R.colvec(3, _FW), R.colvec(1, _FW)],
                  R.tile(_FW), jax.ShapeDtypeStruct((T, FF), BF16), (hv, hv, hv, conv_w, conv_b))


def _ffn_act_bwd(dg, hv, conv_w, conv_b):
    T = hv.shape[0]
    R = _Rows(T, 256)
    tm = R.tm

    def body(dg_ref, h1_ref, h1p_ref, h3_ref, w_ref, b_ref, dh3_ref, dpre_ref, dw_ref, db_ref):
        i = pl.program_id(1)
        keep = (i > 0).astype(F32)
        ext = jnp.concatenate([h1p_ref[...].astype(F32) * keep, h1_ref[...].astype(F32)], axis=0)
        sh = _shifts_causal(ext, 3, tm)
        pre = _wsum(w_ref[...], sh) + b_ref[...]
        s = _sigmoid(pre)
        d = dg_ref[...].astype(F32)
        dh3_ref[...] = (d * pre * s).astype(BF16)
        dpre = d * h3_ref[...].astype(F32) * s * (1.0 + pre * (1.0 - s))
        dpre_ref[...] = dpre.astype(BF16)
        _acc_rows(dw_ref, [_colsum(dpre * q) for q in sh], i == 0)
        _acc_out(db_ref, _colsum(dpre), i == 0)

    return R.call(
        body, "ffn_act_bwd", _FB,
        [R.tile(_FW), R.tile(_FW), R.prev(_FW), R.tile(_FW, _FB), R.colvec(3, _FW), R.colvec(1, _FW)],
        [R.tile(_FW), R.tile(_FW), R.colvec(3, _FW), R.colvec(1, _FW)],
        [jax.ShapeDtypeStruct((T, FF), BF16), jax.ShapeDtypeStruct((T, FF), BF16),
         jax.ShapeDtypeStruct((3, FF), F32), jax.ShapeDtypeStruct((1, FF), F32)],
        (dg, hv, hv, hv, conv_w, conv_b))


def _conv3_transpose(dpre, conv_w):
    T = dpre.shape[0]
    R = _Rows(T, 256)
    tm = R.tm

    def body(d_ref, dn_ref, w_ref, o_ref):
        keep = (pl.program_id(1) < R.nrow - 1).astype(F32)
        ext = jnp.concatenate([d_ref[...].astype(F32), dn_ref[...].astype(F32) * keep], axis=0)
        o_ref[...] = _wsum(w_ref[...], _shifts_anticausal(ext, 3, tm)).astype(BF16)

    return R.call(body, "ffn_conv_bwd", _FB, [R.tile(_FW), R.next(_FW), R.colvec(3, _FW)], R.tile(_FW),
                  jax.ShapeDtypeStruct((T, FF), BF16), (dpre, dpre, conv_w))


def _final_loss(h, w, target):
    T = h.shape[0]
    R = _Rows(T, 512)

    def body(h_ref, w_ref, t_ref, l_ref, dh_ref, dhb_ref, dw_ref):
        first = pl.program_id(1) == 0
        xv = h_ref[...]
        wv = w_ref[...]
        r = lax.rsqrt(jnp.mean(xv * xv, axis=-1, keepdims=True) + EPS)
        xh = xv * r
        err = xh * wv - t_ref[...]
        part = 0.5 * jnp.sum(jnp.mean(err * err, axis=-1, keepdims=True), axis=0, keepdims=True)
        _acc_out(l_ref, jnp.broadcast_to(part, l_ref.shape), first)
        dy = err * (1.0 / D)
        dxh = dy * wv
        dh = r * (dxh - xh * jnp.mean(dxh * xh, axis=-1, keepdims=True))
        dh_ref[...] = dh
        dhb_ref[...] = dh.astype(BF16)
        _acc_out(dw_ref, _colsum(dy * xh), first)

    return R.call(body, "final_loss", 1, [R.tile(D), R.colvec(1, D), R.tile(D)],
                  [R.colvec(8, LANES), R.tile(D), R.tile(D), R.colvec(1, D)],
                  [jax.ShapeDtypeStruct((8, LANES), F32), jax.ShapeDtypeStruct((T, D), F32),
                   jax.ShapeDtypeStruct((T, D), BF16), jax.ShapeDtypeStruct((1, D), F32)], (h, w, target))


def _pad_lanes(v, n=LANES):
    return jnp.pad(v, ((0, 0), (0, n - v.shape[1])))


class _Hooks:
    def after_norm(self, u):
        return u

    def late_weights(self, wts, after):
        return wts

    def grads_ready(self, grads, tie):
        return tie

    def mark(self, name, value):
        pass


def _local_step(x, target, wts, hooks=None):
    hooks = hooks or _Hooks()
    T = x.shape[0]
    w_in = wts["w_in"]
    dt_bias_p, a_log_p = _pad_lanes(wts["dt_bias"]), _pad_lanes(wts["a_log"])
    dsk_rep = jnp.repeat(wts["d_skip"], HP, axis=1)

    u = hooks.after_norm(_rmsnorm_fwd(x, wts["norm_mix_w"], "norm_mix_fwd"))
    proj = _matmul(u, w_in, mode="nn", out_dtype=BF16, name="mm_in")
    dt_raw = _matmul(u, w_in[:, OFF_DT:], mode="nn", out_dtype=F32, name="mm_dt")
    ya_in = _branch_a_fwd(proj, wts["conv_a_w"])
    xact = _xbc_fwd(proj, wts["ssd_conv_w"], wts["ssd_conv_b"])
    dt, acum, acumT = _dt_fwd(dt_raw, dt_bias_p, a_log_p)
    y_ssd, sprev = _ssd_fwd(xact, dt, acum, acumT, dsk_rep)
    yn = _gnorm_fwd(y_ssd, proj, wts["ssd_norm_w"])
    late = hooks.late_weights(wts, yn)
    w_a_out, w_s_out, w_o, w_up, w_down = (late[k] for k in ("w_a_out", "w_s_out", "w_o", "w_up", "w_down"))
    y_a = _matmul(ya_in, w_a_out, mode="nn", out_dtype=BF16, name="mm_a_out")
    y_s = _matmul(yn, w_s_out, mode="nn", out_dtype=BF16, name="mm_s_out")
    merged = _merge_fwd(proj, y_a, y_s)
    h1 = _matmul(merged, w_o, mode="nn", out_dtype=F32, name="mm_o", residual=x)
    v = _rmsnorm_fwd(h1, wts["norm_ffn_w"], "norm_ffn_fwd")
    hv = _matmul(v, w_up, mode="nn", out_dtype=BF16, name="mm_up")
    gact = _ffn_act_fwd(hv, wts["ffn_conv_w"], wts["ffn_conv_b"])
    h2 = _matmul(gact, w_down, mode="nn", out_dtype=F32, name="mm_down", residual=h1)
    loss, dh2, dh2b, g_final = _final_loss(h2, wts["final_norm_w"], target)

    grads = {"final_norm_w": g_final}
    grads["w_down"] = _matmul(gact, dh2b, mode="tn", out_dtype=F32, name="mm_down_dw")
    dgact = _matmul(dh2b, w_down, mode="nt", out_dtype=BF16, name="mm_down_dx")
    dh3, dpre, grads["ffn_conv_w"], grads["ffn_conv_b"] = _ffn_act_bwd(dgact, hv, wts["ffn_conv_w"], wts["ffn_conv_b"])
    dh1c = _conv3_transpose(dpre, wts["ffn_conv_w"])
    grads["w_up"] = jnp.concatenate(
        [_matmul(v, dh1c, mode="tn", out_dtype=F32, name="mm_up_dw1"),
         _matmul(v, dh3, mode="tn", out_dtype=F32, name="mm_up_dw3")], axis=1)
    dv = _matmul(dh1c, w_up[:, :FF], mode="nt", out_dtype=F32, name="mm_up_dx1")
    dv = _matmul(dh3, w_up[:, FF:], mode="nt", out_dtype=F32, name="mm_up_dx3", residual=dv)
    dh1, dh1b, grads["norm_ffn_w"] = _rmsnorm_bwd(dv, h1, wts["norm_ffn_w"], dh2, "norm_ffn_bwd")
    grads["w_o"] = _matmul(merged, dh1b, mode="tn", out_dtype=F32, name="mm_o_dw")
    dmerged = _matmul(dh1b, w_o, mode="nt", out_dtype=BF16, name="mm_o_dx")
    dproj, dya, dys = _merge_bwd(dmerged, proj, y_a, y_s, NIP)
    grads["w_a_out"] = _matmul(ya_in, dya, mode="tn", out_dtype=F32, name="mm_a_out_dw")
    dya_in = _matmul(dya, w_a_out, mode="nt", out_dtype=BF16, name="mm_a_out_dx")
    dproj, grads["conv_a_w"] = _branch_a_bwd(dya_in, proj, wts["conv_a_w"], dproj)
    grads["w_s_out"] = _matmul(yn, dys, mode="tn", out_dtype=F32, name="mm_s_out_dw")
    dys = hooks.grads_ready({k: grads[k] for k in ("w_a_out", "w_s_out", "w_o", "w_up", "w_down")}, dys)
    dyn =_matmul(dys, w_s_out, mode="nt", out_dtype=BF16, name="mm_s_out_dx")
    dproj, dy_ssd, grads["ssd_norm_w"] = _gnorm_bwd(dyn, y_ssd, proj, wts["ssd_norm_w"], dproj)
    dxact, ddt_x, dacum, dskl = _ssd_bwd(dy_ssd, xact, dt, acum, acumT, dsk_rep, sprev)
    hooks.mark("ssd_bwd", dxact)
    grads["d_skip"] = dskl.reshape(NH, HP).sum(axis=1).reshape(1, NH)
    dproj, grads["ssd_conv_w"], grads["ssd_conv_b"] = _xbc_bwd(dxact, proj, wts["ssd_conv_w"], wts["ssd_conv_b"], dproj)
    dproj, g_dtb, g_alog = _dt_bwd(dacum, ddt_x, dt_raw, dt_bias_p, a_log_p, dproj)
    grads["dt_bias"], grads["a_log"] = g_dtb[:, :NH], g_alog[:, :NH]
    grads["w_in"] = _matmul(u, dproj, mode="tn", out_dtype=F32, name="mm_in_dw")
    dproj = hooks.grads_ready({"w_in": grads["w_in"]}, dproj)
    du =_matmul(dproj, w_in, mode="nt", out_dtype=F32, name="mm_in_dx")
    grad_x, _, grads["norm_mix_w"] = _rmsnorm_bwd(du, x, wts["norm_mix_w"], dh1, "norm_mix_bwd")
    return loss, grad_x, grads


def _permute_w_in(w):
    out = jnp.zeros((w.shape[0], NIP), w.dtype)
    for o, n, no in _SEGS:
        out = lax.dynamic_update_slice(out, w[:, o:o + n], (0, no))
    return out


def _unpermute_w_in(g):
    order = sorted(_SEGS)
    return jnp.concatenate([g[:, no:no + n] for o, n, no in order], axis=1)


MESH = pl.DeviceIdType.MESH
NCHIP = 4
NDEV = 8

_W_IN = (("w_in", D, NI // NCHIP, 1),)
_W_REST = (("w_a_out", D // NCHIP, D, 0), ("w_s_out", DI // NCHIP, D, 0), ("w_o", D // NCHIP, D, 0),
           ("w_up", D, 2 * FF // NCHIP, 1), ("w_down", FF // NCHIP, D, 0))


def _slab_rows(group):
    rows = [r * c // LANES for _, r, c, _ in group]
    assert all(n % 32 == 0 for n in rows), rows
    return rows


def _coords():
    return lax.axis_index("x"), lax.axis_index("y"), lax.axis_index("c")


def _other_chips(x, y):
    return [(1 - x, y), (x, 1 - y), (1 - x, 1 - y)]


def _ag_weights(shard):
    nrows = shard.shape[0]
    hr = nrows // 2

    def body(x_ref, out_ref, send_sems, recv_sems, local_sem):
        x, y, c = _coords()
        me = 2 * x + y
        chips = _other_chips(x, y)

        def rows(s, h):
            return out_ref.at[s, pl.ds(h * hr, hr), :]

        def copy(k, s, h, to, src=None):
            return pltpu.make_async_remote_copy(
                src_ref=rows(s, h) if src is None else src, dst_ref=rows(s, h),
                send_sem=send_sems.at[k], recv_sem=recv_sems.at[k], device_id=to, device_id_type=MESH)

        mine = pltpu.make_async_copy(x_ref, out_ref.at[me], local_sem)
        mine.start()
        first = [copy(k, me, c, (*chip, c), src=x_ref.at[pl.ds(c * hr, hr), :]) for k, chip in enumerate(chips)]
        for cp in first:
            cp.start()
        passed = []
        for k, chip in enumerate(chips):
            s = 2 * chip[0] + chip[1]
            copy(k, s, c, (x, y, c)).wait_recv()
            fwd = copy(3 + k, s, c, (x, y, 1 - c))
            fwd.start()
            passed.append(fwd)
        for k, chip in enumerate(chips):
            copy(3 + k, 2 * chip[0] + chip[1], 1 - c, (x, y, c)).wait_recv()
        for cp in first + passed:
            cp.wait_send()
        mine.wait()

    return pl.pallas_call(
        body, name="ag_weights", in_specs=[ANY], out_specs=ANY,
        out_shape=jax.ShapeDtypeStruct((NCHIP,) + shard.shape, shard.dtype),
        scratch_shapes=[pltpu.SemaphoreType.DMA((6,)), pltpu.SemaphoreType.DMA((6,)), pltpu.SemaphoreType.DMA],
        compiler_params=pltpu.CompilerParams(has_side_effects=True),
    )(shard)


HBM = pl.BlockSpec(memory_space=pltpu.HBM)
SEM = pl.BlockSpec(memory_space=pltpu.SEMAPHORE)
_EFFECT = pltpu.SideEffectType.DATAFLOW_SIDE_EFFECTING
_NCOPY = NCHIP - 1


def _plan_bcast(src_ref, land_ref, send_sems, recv_sems):
    x, y, c = _coords()
    sends, lands = [], []
    for k, chip in enumerate(_other_chips(x, y)):
        def copy(slot):
            return pltpu.make_async_remote_copy(
                src_ref=src_ref, dst_ref=land_ref.at[slot], send_sem=send_sems.at[k], recv_sem=recv_sems.at[k],
                device_id=(*chip, c), device_id_type=MESH)
        sends.append(copy(2 * x + y))
        lands.append(copy(2 * chip[0] + chip[1]))
    return sends, lands


def _plan_scatter(src_ref, land_ref, send_sems, recv_sems):
    x, y, c = _coords()
    cps = [pltpu.make_async_remote_copy(
        src_ref=src_ref.at[2 * chip[0] + chip[1]], dst_ref=land_ref.at[k], send_sem=send_sems.at[k],
        recv_sem=recv_sems.at[k], device_id=(*chip, c), device_id_type=MESH)
        for k, chip in enumerate(_other_chips(x, y))]
    return cps, cps


def _split_start(name, src, land, plan):
    def body(src_ref, land_ref, send_sems, recv_sems, src_thru, land_thru, token):
        for cp in plan(src_ref, land_ref, send_sems, recv_sems)[0]:
            cp.start()
        token[...] = jnp.zeros_like(token)

    send_sems, recv_sems, src_thru, land_thru, token = pl.pallas_call(
        body, name=name,
        out_shape=(pltpu.SemaphoreType.DMA((_NCOPY,)), pltpu.SemaphoreType.DMA((_NCOPY,)),
                   pltpu.HBM(src.shape, src.dtype), pltpu.HBM(land.shape, land.dtype),
                   jax.ShapeDtypeStruct((8, LANES), F32)),
        in_specs=(HBM, HBM), out_specs=(SEM, SEM, HBM, HBM, pl.BlockSpec(memory_space=pltpu.VMEM)),
        input_output_aliases={0: 2, 1: 3},
        compiler_params=pltpu.CompilerParams(has_side_effects=_EFFECT),
    )(pltpu.with_memory_space_constraint(src, pltpu.HBM), pltpu.with_memory_space_constraint(land, pltpu.HBM))
    return (send_sems, recv_sems, src_thru, land_thru), token


def _split_wait(name, handle, after, plan):
    send_sems, recv_sems, src_thru, land_thru = handle

    def body(src_ref, land_ref, send_sems, recv_sems, after_ref, src_out, land_out):
        sends, lands = plan(src_ref, land_ref, send_sems, recv_sems)
        for cp in sends:
            cp.wait_send()
        for cp in lands:
            cp.wait_recv()

    return pl.pallas_call(
        body, name=name,
        out_shape=(pltpu.HBM(src_thru.shape, src_thru.dtype), pltpu.HBM(land_thru.shape, land_thru.dtype)),
        in_specs=(HBM, HBM, SEM, SEM, ANY), out_specs=(HBM, HBM), input_output_aliases={0: 0, 1: 1},
        compiler_params=pltpu.CompilerParams(has_side_effects=_EFFECT),
    )(src_thru, land_thru, send_sems, recv_sems, after)


def _tie(x, token, name):
    def body(x_ref, t_ref, o_ref):
        pass

    return pl.pallas_call(
        body, name=name, in_specs=[ANY, pl.BlockSpec(memory_space=pltpu.VMEM)], out_specs=ANY,
        out_shape=jax.ShapeDtypeStruct(x.shape, x.dtype), input_output_aliases={0: 0},
    )(x, token)


def _swap_sibling(p, name):
    def body(p_ref, land_ref, send_sem, recv_sem):
        x, y, c = _coords()
        cp = pltpu.make_async_remote_copy(
            src_ref=p_ref, dst_ref=land_ref, send_sem=send_sem, recv_sem=recv_sem,
            device_id=(x, y, 1 - c), device_id_type=MESH)
        cp.start()
        cp.wait()

    return pl.pallas_call(
        body, name=name, in_specs=[ANY], out_specs=ANY, out_shape=jax.ShapeDtypeStruct(p.shape, p.dtype),
        scratch_shapes=[pltpu.SemaphoreType.DMA, pltpu.SemaphoreType.DMA],
        compiler_params=pltpu.CompilerParams(has_side_effects=True),
    )(p)


_ADD_BYTES = 7 << 19


def _add_tile(rows, cols):
    best = 32
    for t in range(32, rows + 1, 32):
        if rows % t == 0 and t * cols * 4 <= _ADD_BYTES:
            best = t
    return best


def _add_slabs(pack, land, me, name):
    rows, cols = pack.shape[1:]
    tr = _add_tile(rows, cols)

    def body(me_ref, p_ref, l_ref, o_ref):
        f = lambda r: r.astype(F32)
        o_ref[...] = ((f(p_ref[0]) + f(l_ref[0])) + f(l_ref[1])) + f(l_ref[2])

    return pl.pallas_call(
        body, name=name,
        grid_spec=pltpu.PrefetchScalarGridSpec(
            num_scalar_prefetch=1, grid=(rows // tr,),
            in_specs=[pl.BlockSpec((1, tr, cols), lambda i, me_ref: (me_ref[0], i, 0)),
                      pl.BlockSpec((_NCOPY, tr, cols), lambda i, me_ref: (0, i, 0))],
            out_specs=pl.BlockSpec((tr, cols), lambda i, me_ref: (i, 0))),
        out_shape=jax.ShapeDtypeStruct((rows, cols), F32),
        compiler_params=_params(("parallel",)),
    )(me, pack, land)


def _add_pair(a, b, name):
    rows, cols = a.shape
    tr = _add_tile(rows, cols)

    def body(a_ref, b_ref, o_ref):
        o_ref[...] = a_ref[...] + b_ref[...]

    blk = pl.BlockSpec((tr, cols), lambda i: (i, 0))
    return pl.pallas_call(
        body, name=name, grid=(rows // tr,), in_specs=[blk, blk], out_specs=blk,
        out_shape=jax.ShapeDtypeStruct((rows, cols), F32), compiler_params=_params(("parallel",)),
    )(a, b)


_STAGE_W = 1024


def _stage_rows(shapes):
    pieces, r = [], 0
    for i, (k, w) in enumerate(shapes):
        for a in range(k):
            for q in range(0, w, _STAGE_W):
                pieces.append((i, a, q, min(_STAGE_W, w - q), r))
                r += 1
    return pieces, -(-r // 8) * 8


def _gather8(parts, reduce, name):
    shapes = [p.shape for p in parts]
    pieces, rows = _stage_rows(shapes)
    n = len(parts)

    def body(*refs):
        ins, outs = refs[:n], refs[n:2 * n]
        stage, buf, res, send_sems, recv_sems = refs[2 * n:]
        x, y, c = _coords()
        me = 4 * x + 2 * y + c
        stage[...] = jnp.zeros_like(stage)
        for i, a, q, w, r in pieces:
            stage[r:r + 1, 0:w] = ins[i][a:a + 1, q:q + w]
        buf[pl.ds(me, 1)] = stage[...][None]
        cps, lands = [], []
        for k in range(1, NDEV):
            peer = (1 - x if k & 4 else x, 1 - y if k & 2 else y, 1 - c if k & 1 else c)

            def copy(slot):
                return pltpu.make_async_remote_copy(
                    src_ref=stage, dst_ref=buf.at[slot], send_sem=send_sems.at[k - 1],
                    recv_sem=recv_sems.at[k - 1], device_id=peer, device_id_type=MESH)

            cps.append(copy(me))
            lands.append(copy(4 * peer[0] + 2 * peer[1] + peer[2]))
        for cp in cps:
            cp.start()
        for cp, land in zip(cps, lands):
            land.wait_recv()
            cp.wait_send()
        if reduce:
            acc = buf[0]
            for d in range(1, NDEV):
                acc = acc + buf[d]
            res[...] = acc
            for i, a, q, w, r in pieces:
                outs[i][a:a + 1, q:q + w] = res[r:r + 1, 0:w]
        else:
            for i, a, q, w, r in pieces:
                for s in range(NCHIP):
                    outs[i][s, a:a + 1, q:q + w] = buf[2 * s, r:r + 1, 0:w]

    vm = pl.BlockSpec(memory_space=pltpu.VMEM)
    out_shapes = [jax.ShapeDtypeStruct(s if reduce else (NCHIP,) + s, F32) for s in shapes]
    return pl.pallas_call(
        body, name=name, in_specs=[vm] * n, out_specs=[vm] * n, out_shape=out_shapes,
        scratch_shapes=[pltpu.VMEM((rows, _STAGE_W), F32), pltpu.VMEM((NDEV, rows, _STAGE_W), F32),
                        pltpu.VMEM((rows, _STAGE_W), F32), pltpu.SemaphoreType.DMA((NDEV - 1,)),
                        pltpu.SemaphoreType.DMA((NDEV - 1,))],
        compiler_params=pltpu.CompilerParams(has_side_effects=True),
    )(*parts)


def _adamw_update(w_ref, g_ref, m_ref, v_ref, d_ref, mo_ref, vo_ref):
    c1 = 1.0 / (1.0 - ADAM_B1 ** ADAM_STEP)
    c2 = 1.0 / (1.0 - ADAM_B2 ** ADAM_STEP)
    gv = g_ref[...]
    mn = ADAM_B1 * m_ref[...] + (1.0 - ADAM_B1) * gv
    vn = ADAM_B2 * v_ref[...] + (1.0 - ADAM_B2) * (gv * gv)
    d_ref[...] = -ADAM_LR * ((mn * c1) / (jnp.sqrt(vn * c2) + ADAM_EPS) + ADAM_WD * w_ref[...])
    mo_ref[...] = mn
    vo_ref[...] = vn


def _adamw_small(ws, gs, ms, vs):
    n = len(ws)

    def body(*refs):
        for i in range(n):
            _adamw_update(*(refs[j * n + i] for j in range(7)))

    vm = pl.BlockSpec(memory_space=pltpu.VMEM)
    outs = pl.pallas_call(
        body, name="adamw_small", in_specs=[vm] * (4 * n), out_specs=[vm] * (3 * n),
        out_shape=[jax.ShapeDtypeStruct(w.shape, F32) for w in ws] * 3,
    )(*ws, *gs, *ms, *vs)
    return outs[:n], outs[n:2 * n], outs[2 * n:]


def _adamw(w, g, m, v, name):
    rows, cols = w.shape
    tr = rows
    while tr * cols * 4 > (3 << 19) and tr % 16 == 0:
        tr //= 2

    def body(*refs):
        _adamw_update(*refs)

    blk = pl.BlockSpec((tr, cols), lambda i: (i, 0))
    return pl.pallas_call(
        body, name=name, grid=(rows // tr,), in_specs=[blk] * 4, out_specs=[blk] * 3,
        out_shape=[jax.ShapeDtypeStruct((rows, cols), F32)] * 3, compiler_params=_params(("parallel",)),
    )(w, g, m, v)


def _rows128(a, mult=8):
    flat = a.reshape(-1)
    n = -(-flat.shape[0] // (LANES * mult)) * LANES * mult
    return jnp.pad(flat, (0, n - flat.shape[0])).reshape(-1, LANES)


def _pack_rows(parts, total_rows):
    rows = sum(p.shape[0] for p in parts)
    if total_rows > rows:
        parts = list(parts) + [jnp.zeros((total_rows - rows, LANES), parts[0].dtype)]
    return jnp.concatenate(parts, axis=0)


def _unpack_rows(pack, shapes, mult=8):
    out, r = [], 0
    for shp in shapes:
        n = int(np.prod(shp))
        nr = -(-n // (LANES * mult)) * mult
        out.append(pack[r:r + nr].reshape(-1)[:n].reshape(shp))
        r += nr
    return out


def _unpack_full(full, group):
    out, r = {}, 0
    for (name, rr, cc, axis), nr in zip(group, _slab_rows(group)):
        seg = full[:, r:r + nr].reshape(NCHIP, rr, cc)
        out[name] = seg.reshape(NCHIP * rr, cc) if axis == 0 else seg.transpose(1, 0, 2).reshape(rr, NCHIP * cc)
        r += nr
    return out


def _by_chip(g, rr, cc, axis):
    return g.reshape(NCHIP, rr, cc) if axis == 0 else g.reshape(rr, NCHIP, cc).transpose(1, 0, 2)


def _pack_by_chip(grads, group, dtype):
    parts = [_by_chip(grads[name].astype(dtype), rr, cc, axis).reshape(NCHIP, nr, LANES)
             for (name, rr, cc, axis), nr in zip(group, _slab_rows(group))]
    return jnp.concatenate(parts, axis=1)


_SMALL_REPL = ("norm_mix_w", "ssd_conv_b", "dt_bias", "a_log", "d_skip", "ssd_norm_w", "norm_ffn_w",
               "ffn_conv_b", "final_norm_w")
_SMALL_CONV = (("conv_a_w", 3, D), ("ssd_conv_w", 4, DX), ("ffn_conv_w", 3, FF))


def kernel(x, norm_mix_w, w_in, conv_a_w, w_a_out, ssd_conv_w, ssd_conv_b, dt_bias, a_log, d_skip, ssd_norm_w, w_s_out, w_o, norm_ffn_w, w_up, ffn_conv_w, ffn_conv_b, w_down, final_norm_w, loss_target, m_norm_mix_w, m_w_in, m_conv_a_w, m_w_a_out, m_ssd_conv_w, m_ssd_conv_b, m_dt_bias, m_a_log, m_d_skip, m_ssd_norm_w, m_w_s_out, m_w_o, m_norm_ffn_w, m_w_up, m_ffn_conv_w, m_ffn_conv_b, m_w_down, m_final_norm_w, v_norm_mix_w, v_w_in, v_conv_a_w, v_w_a_out, v_ssd_conv_w, v_ssd_conv_b, v_dt_bias, v_a_log, v_d_skip, v_ssd_norm_w, v_w_s_out, v_w_o, v_norm_ffn_w, v_w_up, v_ffn_conv_w, v_ffn_conv_b, v_w_down, v_final_norm_w):
    names = ("norm_mix_w", "w_in", "conv_a_w", "w_a_out", "ssd_conv_w", "ssd_conv_b", "dt_bias", "a_log", "d_skip",
             "ssd_norm_w", "w_s_out", "w_o", "norm_ffn_w", "w_up", "ffn_conv_w", "ffn_conv_b", "w_down", "final_norm_w")
    W = dict(zip(names, (norm_mix_w, w_in, conv_a_w, w_a_out, ssd_conv_w, ssd_conv_b, dt_bias, a_log, d_skip,
                         ssd_norm_w, w_s_out, w_o, norm_ffn_w, w_up, ffn_conv_w, ffn_conv_b, w_down, final_norm_w)))
    M = dict(zip(names, (m_norm_mix_w, m_w_in, m_conv_a_w, m_w_a_out, m_ssd_conv_w, m_ssd_conv_b, m_dt_bias, m_a_log,
                         m_d_skip, m_ssd_norm_w, m_w_s_out, m_w_o, m_norm_ffn_w, m_w_up, m_ffn_conv_w, m_ffn_conv_b,
                         m_w_down, m_final_norm_w)))
    V = dict(zip(names, (v_norm_mix_w, v_w_in, v_conv_a_w, v_w_a_out, v_ssd_conv_w, v_ssd_conv_b, v_dt_bias, v_a_log,
                         v_d_skip, v_ssd_norm_w, v_w_s_out, v_w_o, v_norm_ffn_w, v_w_up, v_ffn_conv_w, v_ffn_conv_b,
                         v_w_down, v_final_norm_w)))
    two_d = lambda a: a.reshape(-1, a.shape[-1])
    W2, M2, V2 = ({k: two_d(a) for k, a in t.items()} for t in (W, M, V))
    xi, yi, ci = _coords()
    me = 2 * xi + yi

    meidx = me.reshape(1).astype(jnp.int32)
    state = {}


    class Hooks(_Hooks):
        def after_norm(self, u):
            return _tie(u, state["rest_token"], "tie_ag_rest")

        def late_weights(self, wts, after):
            own, land = _split_wait("ag_rest_wait", state["rest"], after, _plan_bcast)
            full = _unpack_full(lax.dynamic_update_slice(land, own[None], (me, 0, 0)), _W_REST)
            return {**wts, **full}

        def grads_ready(self, grads, tie):
            if "w_in" in grads:
                key = "g_in"
                pack = _by_chip(_unpermute_w_in(grads["w_in"]).astype(BF16), *_W_IN[0][1:])
            else:
                key, pack = "g_rest", _pack_by_chip(grads, _W_REST, BF16)
            land = lax.empty((_NCOPY,) + pack.shape[1:], BF16)
            state[key], token = _split_start("rs_" + key + "_start", pack, land, _plan_scatter)
            return _tie(tie, token, "tie_" + key)

        def mark(self, name, value):
            state[name] = value

    def reduced(key, after):
        pack, land = _split_wait("rs_" + key + "_wait", state[key], after, _plan_scatter)
        mine = _add_slabs(pack, land, meidx, "rs_" + key + "_add_chips")
        return _add_pair(mine, _swap_sibling(mine, "rs_" + key + "_swap"), "rs_" + key + "_add_cores")

    w_in_full = _ag_weights(W2["w_in"].astype(BF16)).transpose(1, 0, 2).reshape(D, NI)
    wts = {k: W2[k] for k in _SMALL_REPL}
    conv_by_chip = _gather8([W2[n] for n, *_ in _SMALL_CONV], False, "ag_conv_weights")
    for (n, kk, width), stacked in zip(_SMALL_CONV, conv_by_chip):
        wts[n] = stacked.transpose(1, 0, 2).reshape(kk, width)
    rest_slab = _tie(_pack_rows([_rows128(W2[n], 16) for n, *_ in _W_REST], 0).astype(BF16), conv_by_chip[0],
                     "tie_ag_order")
    state["rest"], state["rest_token"] = _split_start(
        "ag_rest_start", rest_slab, lax.empty((NCHIP,) + rest_slab.shape, BF16), _plan_bcast)
    wts["w_in"] = _permute_w_in(w_in_full)

    loss8, grad_x, grads = _local_step(x[0], loss_target[0], wts, Hooks())

    gbig = dict(zip([n for n, *_ in _W_REST],
                    _unpack_rows(reduced("g_rest", state["ssd_bwd"]), [(rr, cc) for _, rr, cc, _ in _W_REST], 16)))
    gbig["w_in"] = reduced("g_in", grad_x)

    small_parts = [grads[n] for n in _SMALL_REPL] + [loss8[0:1]] + [grads[n] for n, *_ in _SMALL_CONV]
    small_g = _gather8(small_parts, True, "allreduce_small")
    gsm = dict(zip(_SMALL_REPL, small_g[:len(_SMALL_REPL)]))
    loss = small_g[len(_SMALL_REPL)][0, 0]
    for (n, kk, width), gfull in zip(_SMALL_CONV, small_g[len(_SMALL_REPL) + 1:]):
        cw = width // NCHIP
        gsm[n] = lax.dynamic_slice(gfull, (0, me * cw), (kk, cw))

    G, DW, NM, NV = {}, {}, {}, {}
    for n in [b[0] for b in _W_IN + _W_REST]:
        G[n] = gbig[n]
        DW[n], NM[n], NV[n] = _adamw(W2[n], G[n], M2[n], V2[n], "adamw_" + n)
    sm_names = list(_SMALL_REPL) + [n for n, *_ in _SMALL_CONV]
    outs = _adamw_small(*([t[n] for n in sm_names] for t in (W2, gsm, M2, V2)))
    for t, vals in zip((DW, NM, NV), outs):
        t.update(zip(sm_names, vals))
    G.update(gsm)

    def shaped(t):
        return [t[n].reshape(W[n].shape) for n in names]

    return (loss, grad_x.reshape(x.shape), *shaped(G), *shaped(DW), *shaped(NM), *shaped(NV))
```

```python
import functools

import jax
import jax.numpy as jnp
import numpy as np
from jax import lax
from jax.experimental import pallas as pl
from jax.experimental.pallas import tpu as pltpu

F32 = jnp.float32
BF16 = jnp.bfloat16

D = 1024
DI = 2048
NH = 32
HP = 64
NG = 4
NS = 128
CH = 128
DX = 3072
FF = 2816
NI = 10272
EPS = 1e-5

OFF_BCV, OFF_XBC, OFF_G, OFF_Z, OFF_DT = 0, 3072, 6144, 8192, 10240
NIP = 10752
_SEGS = ((0, 2048, OFF_G), (2048, 3072, OFF_BCV), (5120, 2048, OFF_Z), (7168, 3072, OFF_XBC), (10240, 32, OFF_DT))

LANES = 128
HALO = 16
V7X_VMEM_LIMIT = 56 * 2 ** 20

ADAM_LR, ADAM_B1, ADAM_B2, ADAM_EPS, ADAM_WD, ADAM_STEP = 0.001, 0.9, 0.999, 1e-08, 0.01, 10

NN = (((1,), (0,)), ((), ()))
NT = (((1,), (1,)), ((), ()))
TN = (((0,), (0,)), ((), ()))


def _dot(a, b, dims=NN):
    return lax.dot_general(a, b, dims, preferred_element_type=F32)


def _params(sem, **kw):
    return pltpu.CompilerParams(dimension_semantics=sem, vmem_limit_bytes=V7X_VMEM_LIMIT, **kw)


V7X_MXU = 256
V7X_HBM_BYTES_PER_S = 3.5e12
STEP_S = 0.35e-6
MATMUL_VMEM = 40 * 2 ** 20


ACC_BYTES_PER_S = 4e12


def _divisors(dim, cap, units):
    for unit in units:
        c = [t for t in range(unit, min(dim, cap) + 1, unit) if dim % t == 0]
        if c:
            return c
    return [dim]


def _tiles(M, N, K, out_bytes, has_res):
    best = None
    for tn in _divisors(N, 2816, (V7X_MXU, LANES)):
        for tm in _divisors(M, 2816, (LANES,)):
            for tk in _divisors(K, 2816, (V7X_MXU, LANES)):
                nk, ni, nj = K // tk, M // tm, N // tn
                vmem = 4 * (tm * tk + tk * tn) + 2 * tm * tn * out_bytes
                vmem += (4 * tm * tn if nk > 1 else 0) + (8 * tm * tn if has_res else 0)
                if vmem > MATMUL_VMEM:
                    continue
                a_reads = M * K * 2 * (nj if nk > 1 else 1)
                b_reads = K * N * 2 * (ni if nk * nj > 1 else 1)
                cost = (a_reads + b_reads + M * N * out_bytes) / V7X_HBM_BYTES_PER_S + ni * nj * nk * STEP_S
                cost += (nk - 1) * M * N * 8 / ACC_BYTES_PER_S
                if best is None or cost < best[0]:
                    best = (cost, tm, tn, tk)
    assert best is not None, (M, N, K)
    return best[1:]


def _sigmoid(x):
    return 1.0 / (1.0 + jnp.exp(-x))


def _matmul(a, b, *, mode, out_dtype, name, residual=None):
    if mode == "nn":
        (M, K), (K2, N) = a.shape, b.shape
    elif mode == "nt":
        (M, K), (N, K2) = a.shape, b.shape
    else:
        (K, M), (K2, N) = a.shape, b.shape
    assert K == K2, (name, a.shape, b.shape)
    tm, tn, tk = _tiles(M, N, K, jnp.dtype(out_dtype).itemsize, residual is not None)
    nk = K // tk
    if mode == "tn":
        a_spec = pl.BlockSpec((tk, tm), lambda i, j, k: (k, i))
    else:
        a_spec = pl.BlockSpec((tm, tk), lambda i, j, k: (i, k))
    if mode == "nt":
        b_spec = pl.BlockSpec((tn, tk), lambda i, j, k: (j, k))
    else:
        b_spec = pl.BlockSpec((tk, tn), lambda i, j, k: (k, j))
    dims = {"nn": NN, "nt": NT, "tn": TN}[mode]
    o_spec = pl.BlockSpec((tm, tn), lambda i, j, k: (i, j))
    has_res = residual is not None

    def body(*refs):
        a_ref, b_ref = refs[:2]
        r_ref = refs[2] if has_res else None
        o_ref = refs[3 if has_res else 2]
        acc_ref = refs[-1]
        k = pl.program_id(2)
        part = _dot(a_ref[...], b_ref[...], dims)

        def finish(r):
            if has_res:
                r = r + r_ref[...].astype(F32)
            o_ref[...] = r.astype(out_dtype)

        if nk == 1:
            finish(part)
            return

        @pl.when(k == 0)
        def _():
            acc_ref[...] = part

        @pl.when(jnp.logical_and(k > 0, k < nk - 1))
        def _():
            acc_ref[...] += part

        @pl.when(k == nk - 1)
        def _():
            finish(acc_ref[...] + part)

    in_specs = [a_spec, b_spec] + ([o_spec] if has_res else [])
    args = (a, b) + ((residual,) if has_res else ())
    return pl.pallas_call(
        body, name=name, grid=(M // tm, N // tn, nk), in_specs=in_specs, out_specs=o_spec,
        out_shape=jax.ShapeDtypeStruct((M, N), out_dtype),
        scratch_shapes=[pltpu.VMEM((tm, tn), F32)] if nk > 1 else [],
        compiler_params=_params(("parallel", "parallel", "arbitrary")),
    )(*args)


class _Rows:
    def __init__(self, T, tm):
        self.T, self.tm = T, min(tm, T // 2)
        self.nrow = T // self.tm
        self.r = self.tm // HALO
        self.nb = T // HALO

    def tile(self, w, cb=0, step=1):
        return pl.BlockSpec((self.tm, w), lambda j, i: (i, cb + step * j))

    def prev(self, w, cb=0, step=1):
        r = self.r
        return pl.BlockSpec((HALO, w), lambda j, i: (jnp.maximum(i * r - 1, 0), cb + step * j))

    def next(self, w, cb=0, step=1):
        r, nb = self.r, self.nb
        return pl.BlockSpec((HALO, w), lambda j, i: (jnp.minimum((i + 1) * r, nb - 1), cb + step * j))

    def colvec(self, k, w, cb=0, step=1):
        return pl.BlockSpec((k, w), lambda j, i: (0, cb + step * j))

    def call(self, body, name, ncol, in_specs, out_specs, out_shape, args, aliases=None):
        return pl.pallas_call(
            body, name=name, grid=(ncol, self.nrow), in_specs=in_specs, out_specs=out_specs,
            out_shape=out_shape, input_output_aliases=aliases or {},
            compiler_params=_params(("parallel", "arbitrary")),
        )(*args)


ANY = pl.BlockSpec(memory_space=pl.ANY)


def _shifts_causal(ext, nk, tm):
    out = []
    for k in range(nk):
        s = nk - 1 - k
        r = ext if s == 0 else pltpu.roll(ext, s, 0)
        out.append(r[HALO:])
    return out


def _shifts_anticausal(ext, nk, tm):
    n = ext.shape[0]
    out = []
    for k in range(nk):
        s = nk - 1 - k
        r = ext if s == 0 else pltpu.roll(ext, n - s, 0)
        out.append(r[:tm])
    return out


def _wsum(w, parts):
    acc = w[0:1, :] * parts[0]
    for k in range(1, len(parts)):
        acc = acc + w[k:k + 1, :] * parts[k]
    return acc


def _colsum(x):
    return jnp.sum(x, axis=0, keepdims=True)


def _acc_out(ref, val, first):
    @pl.when(first)
    def _():
        ref[...] = val

    @pl.when(jnp.logical_not(first))
    def _():
        ref[...] += val


def _acc_rows(ref, rows, first):
    for k, r in enumerate(rows):
        _acc_out(ref.at[k:k + 1, :], r, first)


def _rmsnorm_fwd(x, w, name):
    T = x.shape[0]
    R = _Rows(T, 512)

    def body(x_ref, w_ref, o_ref):
        xv = x_ref[...]
        r = lax.rsqrt(jnp.mean(xv * xv, axis=-1, keepdims=True) + EPS)
        o_ref[...] = (xv * r * w_ref[...]).astype(BF16)

    return R.call(body, name, 1, [R.tile(D), R.colvec(1, D)], R.tile(D),
                  jax.ShapeDtypeStruct((T, D), BF16), (x, w))


def _rmsnorm_bwd(dy, x, w, dres, name):
    T = x.shape[0]
    R = _Rows(T, 512)

    def body(dy_ref, x_ref, w_ref, dr_ref, dx_ref, dxb_ref, dw_ref):
        xv = x_ref[...]
        r = lax.rsqrt(jnp.mean(xv * xv, axis=-1, keepdims=True) + EPS)
        xh = xv * r
        dyv = dy_ref[...].astype(F32)
        dxh = dyv * w_ref[...]
        dx = r * (dxh - xh * jnp.mean(dxh * xh, axis=-1, keepdims=True)) + dr_ref[...]
        dx_ref[...] = dx
        dxb_ref[...] = dx.astype(BF16)
        _acc_out(dw_ref, _colsum(dyv * xh), pl.program_id(1) == 0)

    return R.call(body, name, 1, [R.tile(D), R.tile(D), R.colvec(1, D), R.tile(D)],
                  [R.tile(D), R.tile(D), R.colvec(1, D)],
                  [jax.ShapeDtypeStruct((T, D), F32), jax.ShapeDtypeStruct((T, D), BF16),
                   jax.ShapeDtypeStruct((1, D), F32)],
                  (dy, x, w, dres))


def _branch_a_fwd(proj, conv_w):
    T = proj.shape[0]
    R = _Rows(T, 512)
    tm = R.tm

    def body(p_ref, pp_ref, w_ref, o_ref):
        keep = (pl.program_id(1) > 0).astype(F32)
        cv = p_ref[:, D:2 * D].astype(F32) * p_ref[:, 2 * D:].astype(F32)
        cvp = pp_ref[:, D:2 * D].astype(F32) * pp_ref[:, 2 * D:].astype(F32) * keep
        sh = _shifts_causal(jnp.concatenate([cvp, cv], axis=0), 3, tm)
        ca = _wsum(w_ref[...], sh)
        o_ref[...] = (p_ref[:, :D].astype(F32) * ca).astype(BF16)

    return R.call(body, "branch_a_fwd", 1, [R.tile(3 * D), R.prev(3 * D), R.colvec(3, D)], R.tile(D),
                  jax.ShapeDtypeStruct((T, D), BF16), (proj, proj, conv_w))


def _branch_a_bwd(dya_in, proj, conv_w, dproj):
    T = proj.shape[0]
    R = _Rows(T, 256)
    tm = R.tm

    def body(d_ref, dn_ref, p_ref, pp_ref, pn_ref, w_ref, _alias, o_ref, dw_ref):
        i = pl.program_id(1)
        keep_p = (i > 0).astype(F32)
        keep_n = (i < R.nrow - 1).astype(F32)
        w = w_ref[...]
        b = p_ref[:, :D].astype(F32)
        c = p_ref[:, D:2 * D].astype(F32)
        v = p_ref[:, 2 * D:].astype(F32)
        cvp = pp_ref[:, D:2 * D].astype(F32) * pp_ref[:, 2 * D:].astype(F32) * keep_p
        sh = _shifts_causal(jnp.concatenate([cvp, c * v], axis=0), 3, tm)
        ca = _wsum(w, sh)
        d = d_ref[...].astype(F32)
        dca = d * b
        dca_n = dn_ref[...].astype(F32) * pn_ref[:, :D].astype(F32) * keep_n
        dsh = _shifts_anticausal(jnp.concatenate([dca, dca_n], axis=0), 3, tm)
        dcv = _wsum(w, dsh)
        o_ref[:, :D] = (d * ca).astype(BF16)
        o_ref[:, D:2 * D] = (dcv * v).astype(BF16)
        o_ref[:, 2 * D:] = (dcv * c).astype(BF16)
        _acc_rows(dw_ref, [_colsum(dca * s) for s in sh], i == 0)

    return R.call(
        body, "branch_a_bwd", 1,
        [R.tile(D), R.next(D), R.tile(3 * D), R.prev(3 * D), R.next(3 * D), R.colvec(3, D), ANY],
        [R.tile(3 * D), R.colvec(3, D)],
        [jax.ShapeDtypeStruct(dproj.shape, BF16), jax.ShapeDtypeStruct((3, D), F32)],
        (dya_in, dya_in, proj, proj, proj, conv_w, dproj), aliases={6: 0})


_XW = 512


def _xbc_fwd(proj, conv_w, conv_b):
    T = proj.shape[0]
    R = _Rows(T, 512)
    tm = R.tm
    cb = OFF_XBC // _XW

    def body(x_ref, xp_ref, w_ref, b_ref, o_ref):
        keep = (pl.program_id(1) > 0).astype(F32)
        ext = jnp.concatenate([xp_ref[...].astype(F32) * keep, x_ref[...].astype(F32)], axis=0)
        pre = _wsum(w_ref[...], _shifts_causal(ext, 4, tm)) + b_ref[...]
        o_ref[...] = (pre * _sigmoid(pre)).astype(BF16)

    return R.call(body, "xbc_fwd", DX // _XW,
                  [R.tile(_XW, cb), R.prev(_XW, cb), R.colvec(4, _XW), R.colvec(1, _XW)], R.tile(_XW),
                  jax.ShapeDtypeStruct((T, DX), BF16), (proj, proj, conv_w, conv_b))


def _xbc_bwd(dact, proj, conv_w, conv_b, dproj):
    T = proj.shape[0]
    R = _Rows(T, 512)
    tm = R.tm
    cb = OFF_XBC // _XW

    def body(d_ref, dn_ref, x_ref, xp_ref, xn_ref, w_ref, b_ref, _alias, o_ref, dw_ref, db_ref):
        i = pl.program_id(1)
        keep_p = (i > 0).astype(F32)
        keep_n = (i < R.nrow - 1).astype(F32)
        w = w_ref[...]
        ext = jnp.concatenate([xp_ref[...].astype(F32) * keep_p, x_ref[...].astype(F32),
                               xn_ref[...].astype(F32)], axis=0)
        sh = _shifts_causal(ext, 4, tm + HALO)
        pre = _wsum(w, sh) + b_ref[...]
        s = _sigmoid(pre)
        dsilu = s * (1.0 + pre * (1.0 - s))
        dext = jnp.concatenate([d_ref[...].astype(F32), dn_ref[...].astype(F32) * keep_n], axis=0)
        dpre = dext * dsilu
        dsh = _shifts_anticausal(dpre, 4, tm)
        o_ref[...] = _wsum(w, dsh).astype(BF16)
        dp = dpre[:tm]
        _acc_rows(dw_ref, [_colsum(dp * q[:tm]) for q in sh], i == 0)
        _acc_out(db_ref, _colsum(dp), i == 0)

    return R.call(
        body, "xbc_bwd", DX // _XW,
        [R.tile(_XW), R.next(_XW), R.tile(_XW, cb), R.prev(_XW, cb), R.next(_XW, cb),
         R.colvec(4, _XW), R.colvec(1, _XW), ANY],
        [R.tile(_XW, cb), R.colvec(4, _XW), R.colvec(1, _XW)],
        [jax.ShapeDtypeStruct(dproj.shape, BF16), jax.ShapeDtypeStruct((4, DX), F32),
         jax.ShapeDtypeStruct((1, DX), F32)],
        (dact, dact, proj, proj, proj, conv_w, conv_b, dproj), aliases={7: 0})


def _softplus(x):
    return jnp.maximum(x, 0.0) + jnp.log(1.0 + jnp.exp(-jnp.abs(x)))


def _dt_fwd(dt_raw, dt_bias_p, a_log_p):
    T = dt_raw.shape[0]

    def body(r_ref, b_ref, al_ref, dt_ref, ac_ref, acT_ref):
        dt = _softplus(r_ref[...] + b_ref[...])
        s = dt * (-jnp.exp(al_ref[...]))
        row = lax.broadcasted_iota(jnp.int32, (CH, LANES), 0)
        k = 1
        while k < CH:
            s = s + jnp.where(row >= k, pltpu.roll(s, k, 0), 0.0)
            k *= 2
        dt_ref[...] = dt
        ac_ref[...] = s
        acT_ref[...] = s.T

    blk = pl.BlockSpec((CH, LANES), lambda i: (i, 0))
    vec = pl.BlockSpec((1, LANES), lambda i: (0, 0))
    return pl.pallas_call(
        body, name="dt_fwd", grid=(T // CH,), in_specs=[blk, vec, vec], out_specs=[blk, blk, blk],
        out_shape=[jax.ShapeDtypeStruct((T, LANES), F32)] * 3, compiler_params=_params(("parallel",)),
    )(dt_raw, dt_bias_p, a_log_p)


def _dt_bwd(dacum, ddt_x, dt_raw, dt_bias_p, a_log_p, dproj):
    T = dt_raw.shape[0]
    nc = T // CH

    def body(da_ref, dx_ref, r_ref, b_ref, al_ref, _alias, o_ref, db_ref, dal_ref):
        i = pl.program_id(0)
        a = -jnp.exp(al_ref[...])
        z = r_ref[...] + b_ref[...]
        dt = _softplus(z)
        s = da_ref[...]
        row = lax.broadcasted_iota(jnp.int32, (CH, LANES), 0)
        k = 1
        while k < CH:
            s = s + jnp.where(row < CH - k, pltpu.roll(s, CH - k, 0), 0.0)
            k *= 2
        ddt = s * a + dx_ref[...]
        draw = ddt * _sigmoid(z)
        o_ref[:, :LANES] = draw.astype(BF16)
        o_ref[:, LANES:] = jnp.zeros((CH, NIP - OFF_DT - LANES), BF16)
        _acc_out(db_ref, _colsum(draw), i == 0)
        _acc_out(dal_ref, _colsum(s * dt), i == 0)

        @pl.when(i == nc - 1)
        def _():
            dal_ref[...] = dal_ref[...] * a

    blk = pl.BlockSpec((CH, LANES), lambda i: (i, 0))
    vec = pl.BlockSpec((1, LANES), lambda i: (0, 0))
    oblk = pl.BlockSpec((CH, NIP - OFF_DT), lambda i: (i, OFF_DT // (NIP - OFF_DT)))
    return pl.pallas_call(
        body, name="dt_bwd", grid=(nc,), in_specs=[blk, blk, blk, vec, vec, ANY], out_specs=[oblk, vec, vec],
        out_shape=[jax.ShapeDtypeStruct(dproj.shape, BF16), jax.ShapeDtypeStruct((1, LANES), F32),
                   jax.ShapeDtypeStruct((1, LANES), F32)],
        input_output_aliases={5: 0}, compiler_params=_params(("arbitrary",)),
    )(dacum, ddt_x, dt_raw, dt_bias_p, a_log_p, dproj)


_GW = DI // NG
_HG = NH // NG
_NEG = -1e30


def _pair_lanes(left, v0, v1):
    return jnp.where(left, v0, v1)


def _ssd_specs(T, rev):
    nc = T // CH
    cm = (lambda c: nc - 1 - c) if rev else (lambda c: c)
    bw = NG * NS
    return dict(
        xs=pl.BlockSpec((CH, DI), lambda c: (cm(c), 0)),
        bm=pl.BlockSpec((CH, bw), lambda c: (cm(c), DI // bw)),
        cmat=pl.BlockSpec((CH, bw), lambda c: (cm(c), DI // bw + 1)),
        xbc=pl.BlockSpec((CH, DX), lambda c: (cm(c), 0)),
        col=pl.BlockSpec((CH, LANES), lambda c: (cm(c), 0)),
        dsk=pl.BlockSpec((1, DI), lambda c: (0, 0)),
        state=pl.BlockSpec((1, NS, DI), lambda c: (cm(c), 0, 0)),
    )


def _last(ref, lo, hi):
    return ref.at[(slice(None),) * (len(ref.shape) - 1) + (slice(lo, hi),)]


def _group_views(g, wide, narrow):
    return [_last(r, g * _GW, (g + 1) * _GW) for r in wide] + [_last(r, g * NS, (g + 1) * NS) for r in narrow]


def _ssd_fwd(xact, dt, acum, acumT, dsk_rep):
    T = xact.shape[0]
    nc = T // CH
    sp = _ssd_specs(T, False)

    def body(*refs):
        xs, bm, cmat, dtr, acr, actr, dsk, y, spv, S_ref = refs

        @pl.when(pl.program_id(0) == 0)
        def _():
            S_ref[...] = jnp.zeros_like(S_ref)

        for g in range(NG):
            group(g * _HG, dtr[...], acr[...], actr[...], *_group_views(g, (xs, dsk, y, spv, S_ref), (bm, cmat)))

    def group(hb, dt, ac, acT, xs_ref, dsk_ref, y_ref, sp_ref, S_ref, b_ref, c_ref):
        Bm, Cm = b_ref[...], c_ref[...]
        S = S_ref[...]
        sp_ref[0] = S
        cb = _dot(Cm, Bm, NT)
        CS = _dot(Cm, S.astype(BF16))
        row = lax.broadcasted_iota(jnp.int32, (CH, CH), 0)
        col = lax.broadcasted_iota(jnp.int32, (CH, CH), 1)
        tril = row >= col
        left = col < HP
        xd_parts, dec_parts = [], []
        for p in range(_HG // 2):
            sl = slice(p * LANES, (p + 1) * LANES)
            j0, j1 = hb + 2 * p, hb + 2 * p + 1
            xp = xs_ref[:, sl].astype(F32)
            a0, a1 = ac[:, j0:j0 + 1], ac[:, j1:j1 + 1]
            al0, al1 = ac[CH - 1:CH, j0:j0 + 1], ac[CH - 1:CH, j1:j1 + 1]
            X = xp * _pair_lanes(left, dt[:, j0:j0 + 1], dt[:, j1:j1 + 1])
            Xb = X.astype(BF16)
            yd = jnp.zeros((CH, LANES), F32)
            for j, aj, mask in ((j0, a0, left), (j1, a1, jnp.logical_not(left))):
                Lm = jnp.exp(jnp.where(tril, aj - acT[j:j + 1, :], _NEG))
                W = (cb * Lm).astype(BF16)
                yd = yd + _dot(W, jnp.where(mask, Xb, jnp.zeros_like(Xb)))
            eal = _pair_lanes(left, jnp.exp(a0), jnp.exp(a1))
            y = yd + eal * CS[:, sl] + dsk_ref[:, sl] * xp
            y_ref[:, sl] = y.astype(BF16)
            xd_parts.append(X * _pair_lanes(left, jnp.exp(al0 - a0), jnp.exp(al1 - a1)))
            dec_parts.append(_pair_lanes(left[0:1], jnp.exp(al0), jnp.exp(al1)))
        Xd = jnp.concatenate(xd_parts, axis=1).astype(BF16)
        dec = jnp.concatenate(dec_parts, axis=1)
        S_ref[...] = dec * S + _dot(Bm, Xd, TN)

    return pl.pallas_call(
        body, name="ssd_fwd", grid=(nc,),
        in_specs=[sp["xs"], sp["bm"], sp["cmat"], sp["col"], sp["col"], sp["col"], sp["dsk"]],
        out_specs=[sp["xs"], sp["state"]],
        out_shape=[jax.ShapeDtypeStruct((T, DI), BF16), jax.ShapeDtypeStruct((nc, NS, DI), F32)],
        scratch_shapes=[pltpu.VMEM((NS, DI), F32)],
        compiler_params=_params(("arbitrary",)),
    )(xact, xact, xact, dt, acum, acumT, dsk_rep)


def _ssd_bwd(dy, xact, dt, acum, acumT, dsk_rep, sprev):
    T = xact.shape[0]
    nc = T // CH
    sp = _ssd_specs(T, True)

    def body(*refs):
        xs, bm, cmat, dtr, acr, actr, dsk, dyr, spv, dxa, ddtx, dAc, dskacc, dS_ref = refs
        first = pl.program_id(0) == 0

        @pl.when(first)
        def _():
            dS_ref[...] = jnp.zeros_like(dS_ref)

        dbc = _last(dxa, DI, DX)
        ddtx_sum = jnp.zeros((CH, LANES), F32)
        dAc_sum = jnp.zeros((CH, LANES), F32)
        for g in range(NG):
            a, b = group(first, g * _HG, dtr[...], acr[...], actr[...],
                         *_group_views(g, (xs, dsk, dyr, spv, dxa, dskacc, dS_ref),
                                       (bm, cmat, dbc, _last(dbc, NG * NS, 2 * NG * NS))))
            ddtx_sum, dAc_sum = ddtx_sum + a, dAc_sum + b
        ddtx[...] = ddtx_sum
        dAc[...] = dAc_sum

    def group(first, hb, dt, ac, acT, xs_ref, dsk_ref, dy_ref, sp_ref, dx_ref, dskacc_ref, dS_ref, b_ref, c_ref,
              dB_ref, dC_ref):
        Bm, Cm = b_ref[...], c_ref[...]
        S = sp_ref[0]
        dS = dS_ref[...]
        Sb, dSb = S.astype(BF16), dS.astype(BF16)
        cb = _dot(Cm, Bm, NT)
        cbT = _dot(Bm, Cm, NT)
        CmT = Cm.T
        CS = _dot(Cm, Sb)
        T1 = _dot(Bm, dSb)
        row = lax.broadcasted_iota(jnp.int32, (CH, CH), 0)
        col = lax.broadcasted_iota(jnp.int32, (CH, CH), 1)
        tril = row >= col
        triu = row <= col
        left = col < HP
        lane8 = lax.broadcasted_iota(jnp.int32, (1, LANES), 1)
        lastrow = lax.broadcasted_iota(jnp.int32, (CH, 1), 0) == CH - 1
        dCB = jnp.zeros((CH, CH), F32)
        dCBT = jnp.zeros((CH, CH), F32)
        dAc = jnp.zeros((CH, LANES), F32)
        ddtx = jnp.zeros((CH, LANES), F32)
        xd_parts, dye_parts, dec_parts, dsk_parts = [], [], [], []
        for p in range(_HG // 2):
            sl = slice(p * LANES, (p + 1) * LANES)
            j0, j1 = hb + 2 * p, hb + 2 * p + 1
            xp = xs_ref[:, sl].astype(F32)
            dyp = dy_ref[:, sl].astype(F32)
            a0, a1 = ac[:, j0:j0 + 1], ac[:, j1:j1 + 1]
            al0, al1 = ac[CH - 1:CH, j0:j0 + 1], ac[CH - 1:CH, j1:j1 + 1]
            dtl = _pair_lanes(left, dt[:, j0:j0 + 1], dt[:, j1:j1 + 1])
            X = xp * dtl
            Xb = X.astype(BF16)
            eal = _pair_lanes(left, jnp.exp(a0), jnp.exp(a1))
            dtel = _pair_lanes(left, jnp.exp(al0 - a0), jnp.exp(al1 - a1))
            T1p = T1[:, sl]
            dXd = jnp.zeros((CH, LANES), F32)
            Rm = T1p * dtel * X
            GR = dyp * (eal * CS[:, sl]) - Rm
            SdS = dS[:, sl] * S[:, sl]
            for j, aj, alj, mask in ((j0, a0, al0, left), (j1, a1, al1, jnp.logical_not(left))):
                arow = acT[j:j + 1, :]
                Lm = jnp.exp(jnp.where(tril, aj - arow, _NEG))
                LmT = jnp.exp(jnp.where(triu, arow - aj, _NEG))
                dYm = jnp.where(mask, dyp, 0.0).astype(BF16)
                dWm = _dot(dYm, Xb, NT)
                dWmT = _dot(Xb, dYm, NT)
                dCB = dCB + dWm * Lm
                dCBT = dCBT + dWmT * LmT
                WT = cbT * LmT
                dXd = dXd + _dot(WT.astype(BF16), dYm)
                qd = dWm * (cb * Lm) - dWmT * WT
                colv = jnp.sum(qd + jnp.where(mask, GR, 0.0), axis=1, keepdims=True)
                tot = jnp.where(mask, Rm + jnp.exp(alj) * SdS, 0.0)
                dalast = jnp.sum(jnp.sum(tot, axis=0, keepdims=True), axis=1, keepdims=True)
                dAc = dAc + (colv + jnp.where(lastrow, dalast, 0.0)) * (lane8 == j).astype(F32)
            dX = dXd + dtel * T1p
            dXx = dX * xp
            for j, mask in ((j0, left), (j1, jnp.logical_not(left))):
                dd = jnp.sum(jnp.where(mask, dXx, 0.0), axis=1, keepdims=True)
                ddtx = ddtx + dd * (lane8 == j).astype(F32)
            dx_ref[:, sl] = (dX * dtl + dsk_ref[:, sl] * dyp).astype(BF16)
            dsk_parts.append(_colsum(dyp * xp))
            xd_parts.append(X * dtel)
            dye_parts.append(dyp * eal)
            dec_parts.append(_pair_lanes(left[0:1], jnp.exp(al0), jnp.exp(al1)))
        Xd = jnp.concatenate(xd_parts, axis=1).astype(BF16)
        dYe = jnp.concatenate(dye_parts, axis=1).astype(BF16)
        dec = jnp.concatenate(dec_parts, axis=1)
        dC_ref[...] = (_dot(dCB.astype(BF16), Bm) + _dot(dYe, Sb, NT)).astype(BF16)
        dB_ref[...] = (_dot(dCBT.astype(BF16), Cm) + _dot(Xd, dSb, NT)).astype(BF16)
        dS_ref[...] = _dot(CmT, dYe) + dec * dS
        _acc_out(dskacc_ref, jnp.concatenate(dsk_parts, axis=1), first)
        return ddtx, dAc

    return pl.pallas_call(
        body, name="ssd_bwd", grid=(nc,),
        in_specs=[sp["xs"], sp["bm"], sp["cmat"], sp["col"], sp["col"], sp["col"], sp["dsk"], sp["xs"],
                  sp["state"]],
        out_specs=[sp["xbc"], sp["col"], sp["col"], sp["dsk"]],
        out_shape=[jax.ShapeDtypeStruct((T, DX), BF16), jax.ShapeDtypeStruct((T, LANES), F32),
                   jax.ShapeDtypeStruct((T, LANES), F32), jax.ShapeDtypeStruct((1, DI), F32)],
        scratch_shapes=[pltpu.VMEM((NS, DI), F32)],
        compiler_params=_params(("arbitrary",)),
    )(xact, xact, xact, dt, acum, acumT, dsk_rep, dy, sprev)


def _gnorm_fwd(y, proj, w):
    T = y.shape[0]
    R = _Rows(T, 1024)
    zb = OFF_Z // _GW

    def body(y_ref, z_ref, w_ref, o_ref):
        z = z_ref[...].astype(F32)
        yf = y_ref[...].astype(F32) * z * _sigmoid(z)
        r = lax.rsqrt(jnp.mean(yf * yf, axis=-1, keepdims=True) + EPS)
        o_ref[...] = (yf * r * w_ref[...]).astype(BF16)

    return R.call(body, "gnorm_fwd", NG, [R.tile(_GW), R.tile(_GW, zb), R.colvec(1, _GW)], R.tile(_GW),
                  jax.ShapeDtypeStruct((T, DI), BF16), (y, proj, w))


def _gnorm_bwd(dn, y, proj, w, dproj):
    T = y.shape[0]
    R = _Rows(T, 1024)
    zb = OFF_Z // _GW

    def body(dn_ref, y_ref, z_ref, w_ref, _alias, dz_ref, dy_ref, dw_ref):
        z = z_ref[...].astype(F32)
        yv = y_ref[...].astype(F32)
        s = _sigmoid(z)
        silu = z * s
        yf = yv * silu
        r = lax.rsqrt(jnp.mean(yf * yf, axis=-1, keepdims=True) + EPS)
        yh = yf * r
        dnv = dn_ref[...].astype(F32)
        dyh = dnv * w_ref[...]
        dyf = r * (dyh - yh * jnp.mean(dyh * yh, axis=-1, keepdims=True))
        dy_ref[...] = (dyf * silu).astype(BF16)
        dz_ref[...] = (dyf * yv * s * (1.0 + z * (1.0 - s))).astype(BF16)
        _acc_out(dw_ref, _colsum(dnv * yh), pl.program_id(1) == 0)

    return R.call(
        body, "gnorm_bwd", NG, [R.tile(_GW), R.tile(_GW), R.tile(_GW, zb), R.colvec(1, _GW), ANY],
        [R.tile(_GW, zb), R.tile(_GW), R.colvec(1, _GW)],
        [jax.ShapeDtypeStruct(dproj.shape, BF16), jax.ShapeDtypeStruct((T, DI), BF16),
         jax.ShapeDtypeStruct((1, DI), F32)],
        (dn, y, proj, w, dproj), aliases={4: 0})


def _merge_fwd(proj, ya, ys):
    T = proj.shape[0]
    R = _Rows(T, 512)
    gb = OFF_G // (2 * D)

    def body(g_ref, ya_ref, ys_ref, o_ref):
        ga = _sigmoid(g_ref[:, :D].astype(F32))
        gs = _sigmoid(g_ref[:, D:].astype(F32))
        o_ref[...] = (ga * ya_ref[...].astype(F32) + gs * ys_ref[...].astype(F32)).astype(BF16)

    return R.call(body, "merge_fwd", 1, [R.tile(2 * D, gb), R.tile(D), R.tile(D)], R.tile(D),
                  jax.ShapeDtypeStruct((T, D), BF16), (proj, ya, ys))


def _merge_bwd(dm, proj, ya, ys, ncols):
    T = proj.shape[0]
    R = _Rows(T, 256)
    gb = OFF_G // (2 * D)

    def body(dm_ref, g_ref, ya_ref, ys_ref, dg_ref, dya_ref, dys_ref):
        d = dm_ref[...].astype(F32)
        ga = _sigmoid(g_ref[:, :D].astype(F32))
        gs = _sigmoid(g_ref[:, D:].astype(F32))
        dya_ref[...] = (d * ga).astype(BF16)
        dys_ref[...] = (d * gs).astype(BF16)
        dg_ref[:, :D] = (d * ya_ref[...].astype(F32) * ga * (1.0 - ga)).astype(BF16)
        dg_ref[:, D:] = (d * ys_ref[...].astype(F32) * gs * (1.0 - gs)).astype(BF16)

    return R.call(
        body, "merge_bwd", 1, [R.tile(D), R.tile(2 * D, gb), R.tile(D), R.tile(D)],
        [R.tile(2 * D, gb), R.tile(D), R.tile(D)],
        [jax.ShapeDtypeStruct((T, ncols), BF16), jax.ShapeDtypeStruct((T, D), BF16),
         jax.ShapeDtypeStruct((T, D), BF16)],
        (dm, proj, ya, ys))


_FW = 1408
_FB = FF // _FW


def _ffn_act_fwd(hv, conv_w, conv_b):
    T = hv.shape[0]
    R = _Rows(T, 256)
    tm = R.tm

    def body(h1_ref, h1p_ref, h3_ref, w_ref, b_ref, o_ref):
        keep = (pl.program_id(1) > 0).astype(F32)
        ext = jnp.concatenate([h1p_ref[...].astype(F32) * keep, h1_ref[...].astype(F32)], axis=0)
        pre = _wsum(w_ref[...], _shifts_causal(ext, 3, tm)) + b_ref[...]
        o_ref[...] = (pre * _sigmoid(pre) * h3_ref[...].astype(F32)).astype(BF16)

    return R.call(body, "ffn_act_fwd", _FB,
                  [R.tile(_FW), R.prev(_FW), R.tile(_FW, _FB), R.colvec(3, _FW), R.colvec(1, _FW)],
                  R.tile(_FW), jax.ShapeDtypeStruct((T, FF), BF16), (hv, hv, hv, conv_w, conv_b))


def _ffn_act_bwd(dg, hv, conv_w, conv_b):
    T = hv.shape[0]
    R = _Rows(T, 256)
    tm = R.tm

    def body(dg_ref, h1_ref, h1p_ref, h3_ref, w_ref, b_ref, dh3_ref, dpre_ref, dw_ref, db_ref):
        i = pl.program_id(1)
        keep = (i > 0).astype(F32)
        ext = jnp.concatenate([h1p_ref[...].astype(F32) * keep, h1_ref[...].astype(F32)], axis=0)
        sh = _shifts_causal(ext, 3, tm)
        pre = _wsum(w_ref[...], sh) + b_ref[...]
        s = _sigmoid(pre)
        d = dg_ref[...].astype(F32)
        dh3_ref[...] = (d * pre * s).astype(BF16)
        dpre = d * h3_ref[...].astype(F32) * s * (1.0 + pre * (1.0 - s))
        dpre_ref[...] = dpre.astype(BF16)
        _acc_rows(dw_ref, [_colsum(dpre * q) for q in sh], i == 0)
        _acc_out(db_ref, _colsum(dpre), i == 0)

    return R.call(
        body, "ffn_act_bwd", _FB,
        [R.tile(_FW), R.tile(_FW), R.prev(_FW), R.tile(_FW, _FB), R.colvec(3, _FW), R.colvec(1, _FW)],
        [R.tile(_FW), R.tile(_FW), R.colvec(3, _FW), R.colvec(1, _FW)],
        [jax.ShapeDtypeStruct((T, FF), BF16), jax.ShapeDtypeStruct((T, FF), BF16),
         jax.ShapeDtypeStruct((3, FF), F32), jax.ShapeDtypeStruct((1, FF), F32)],
        (dg, hv, hv, hv, conv_w, conv_b))


def _conv3_transpose(dpre, conv_w):
    T = dpre.shape[0]
    R = _Rows(T, 256)
    tm = R.tm

    def body(d_ref, dn_ref, w_ref, o_ref):
        keep = (pl.program_id(1) < R.nrow - 1).astype(F32)
        ext = jnp.concatenate([d_ref[...].astype(F32), dn_ref[...].astype(F32) * keep], axis=0)
        o_ref[...] = _wsum(w_ref[...], _shifts_anticausal(ext, 3, tm)).astype(BF16)

    return R.call(body, "ffn_conv_bwd", _FB, [R.tile(_FW), R.next(_FW), R.colvec(3, _FW)], R.tile(_FW),
                  jax.ShapeDtypeStruct((T, FF), BF16), (dpre, dpre, conv_w))


def _final_loss(h, w, target):
    T = h.shape[0]
    R = _Rows(T, 512)

    def body(h_ref, w_ref, t_ref, l_ref, dh_ref, dhb_ref, dw_ref):
        first = pl.program_id(1) == 0
        xv = h_ref[...]
        wv = w_ref[...]
        r = lax.rsqrt(jnp.mean(xv * xv, axis=-1, keepdims=True) + EPS)
        xh = xv * r
        err = xh * wv - t_ref[...]
        part = 0.5 * jnp.sum(jnp.mean(err * err, axis=-1, keepdims=True), axis=0, keepdims=True)
        _acc_out(l_ref, jnp.broadcast_to(part, l_ref.shape), first)
        dy = err * (1.0 / D)
        dxh = dy * wv
        dh = r * (dxh - xh * jnp.mean(dxh * xh, axis=-1, keepdims=True))
        dh_ref[...] = dh
        dhb_ref[...] = dh.astype(BF16)
        _acc_out(dw_ref, _colsum(dy * xh), first)

    return R.call(body, "final_loss", 1, [R.tile(D), R.colvec(1, D), R.tile(D)],
                  [R.colvec(8, LANES), R.tile(D), R.tile(D), R.colvec(1, D)],
                  [jax.ShapeDtypeStruct((8, LANES), F32), jax.ShapeDtypeStruct((T, D), F32),
                   jax.ShapeDtypeStruct((T, D), BF16), jax.ShapeDtypeStruct((1, D), F32)], (h, w, target))


def _pad_lanes(v, n=LANES):
    return jnp.pad(v, ((0, 0), (0, n - v.shape[1])))


class _Hooks:
    def after_norm(self, u):
        return u

    def late_weights(self, wts, after):
        return wts

    def grads_ready(self, grads, tie):
        return tie

    def mark(self, name, value):
        pass


def _local_step(x, target, wts, hooks=None):
    hooks = hooks or _Hooks()
    T = x.shape[0]
    w_in = wts["w_in"]
    dt_bias_p, a_log_p = _pad_lanes(wts["dt_bias"]), _pad_lanes(wts["a_log"])
    dsk_rep = jnp.repeat(wts["d_skip"], HP, axis=1)

    u = hooks.after_norm(_rmsnorm_fwd(x, wts["norm_mix_w"], "norm_mix_fwd"))
    proj = _matmul(u, w_in, mode="nn", out_dtype=BF16, name="mm_in")
    dt_raw = _matmul(u, w_in[:, OFF_DT:OFF_DT + LANES], mode="nn", out_dtype=F32, name="mm_dt")
    ya_in = _branch_a_fwd(proj, wts["conv_a_w"])
    xact = _xbc_fwd(proj, wts["ssd_conv_w"], wts["ssd_conv_b"])
    dt, acum, acumT = _dt_fwd(dt_raw, dt_bias_p, a_log_p)
    y_ssd, sprev = _ssd_fwd(xact, dt, acum, acumT, dsk_rep)
    yn = _gnorm_fwd(y_ssd, proj, wts["ssd_norm_w"])
    late = hooks.late_weights(wts, yn)
    w_a_out, w_s_out, w_o, w_up, w_down = (late[k] for k in ("w_a_out", "w_s_out", "w_o", "w_up", "w_down"))
    y_a = _matmul(ya_in, w_a_out, mode="nn", out_dtype=BF16, name="mm_a_out")
    y_s = _matmul(yn, w_s_out, mode="nn", out_dtype=BF16, name="mm_s_out")
    merged = _merge_fwd(proj, y_a, y_s)
    h1 = _matmul(merged, w_o, mode="nn", out_dtype=F32, name="mm_o", residual=x)
    v = _rmsnorm_fwd(h1, wts["norm_ffn_w"], "norm_ffn_fwd")
    hv = _matmul(v, w_up, mode="nn", out_dtype=BF16, name="mm_up")
    gact = _ffn_act_fwd(hv, wts["ffn_conv_w"], wts["ffn_conv_b"])
    h2 = _matmul(gact, w_down, mode="nn", out_dtype=F32, name="mm_down", residual=h1)
    loss, dh2, dh2b, g_final = _final_loss(h2, wts["final_norm_w"], target)

    grads = {"final_norm_w": g_final}
    grads["w_down"] = _matmul(gact, dh2b, mode="tn", out_dtype=F32, name="mm_down_dw")
    dgact = _matmul(dh2b, w_down, mode="nt", out_dtype=BF16, name="mm_down_dx")
    dh3, dpre, grads["ffn_conv_w"], grads["ffn_conv_b"] = _ffn_act_bwd(dgact, hv, wts["ffn_conv_w"], wts["ffn_conv_b"])
    dh1c = _conv3_transpose(dpre, wts["ffn_conv_w"])
    grads["w_up"] = jnp.concatenate(
        [_matmul(v, dh1c, mode="tn", out_dtype=F32, name="mm_up_dw1"),
         _matmul(v, dh3, mode="tn", out_dtype=F32, name="mm_up_dw3")], axis=1)
    dv = _matmul(dh1c, w_up[:, :FF], mode="nt", out_dtype=F32, name="mm_up_dx1")
    dv = _matmul(dh3, w_up[:, FF:], mode="nt", out_dtype=F32, name="mm_up_dx3", residual=dv)
    dh1, dh1b, grads["norm_ffn_w"] = _rmsnorm_bwd(dv, h1, wts["norm_ffn_w"], dh2, "norm_ffn_bwd")
    grads["w_o"] = _matmul(merged, dh1b, mode="tn", out_dtype=F32, name="mm_o_dw")
    dmerged = _matmul(dh1b, w_o, mode="nt", out_dtype=BF16, name="mm_o_dx")
    dproj, dya, dys = _merge_bwd(dmerged, proj, y_a, y_s, NIP)
    grads["w_a_out"] = _matmul(ya_in, dya, mode="tn", out_dtype=F32, name="mm_a_out_dw")
    dya_in = _matmul(dya, w_a_out, mode="nt", out_dtype=BF16, name="mm_a_out_dx")
    dproj, grads["conv_a_w"] = _branch_a_bwd(dya_in, proj, wts["conv_a_w"], dproj)
    grads["w_s_out"] = _matmul(yn, dys, mode="tn", out_dtype=F32, name="mm_s_out_dw")
    dys = hooks.grads_ready({k: grads[k] for k in ("w_a_out", "w_s_out", "w_o", "w_up", "w_down")}, dys)
    dyn =_matmul(dys, w_s_out, mode="nt", out_dtype=BF16, name="mm_s_out_dx")
    dproj, dy_ssd, grads["ssd_norm_w"] = _gnorm_bwd(dyn, y_ssd, proj, wts["ssd_norm_w"], dproj)
    dxact, ddt_x, dacum, dskl = _ssd_bwd(dy_ssd, xact, dt, acum, acumT, dsk_rep, sprev)
    hooks.mark("ssd_bwd", dxact)
    grads["d_skip"] = dskl.reshape(NH, HP).sum(axis=1).reshape(1, NH)
    dproj, grads["ssd_conv_w"], grads["ssd_conv_b"] = _xbc_bwd(dxact, proj, wts["ssd_conv_w"], wts["ssd_conv_b"], dproj)
    dproj, g_dtb, g_alog = _dt_bwd(dacum, ddt_x, dt_raw, dt_bias_p, a_log_p, dproj)
    grads["dt_bias"], grads["a_log"] = g_dtb[:, :NH], g_alog[:, :NH]
    grads["w_in"] = _matmul(u, dproj, mode="tn", out_dtype=F32, name="mm_in_dw")
    dproj = hooks.grads_ready({"w_in": grads["w_in"]}, dproj)
    du =_matmul(dproj, w_in, mode="nt", out_dtype=F32, name="mm_in_dx")
    grad_x, _, grads["norm_mix_w"] = _rmsnorm_bwd(du, x, wts["norm_mix_w"], dh1, "norm_mix_bwd")
    return loss, grad_x, grads


def _permute_w_in(w):
    out = jnp.zeros((w.shape[0], NIP), w.dtype)
    for o, n, no in _SEGS:
        out = lax.dynamic_update_slice(out, w[:, o:o + n], (0, no))
    return out


def _unpermute_w_in(g):
    order = sorted(_SEGS)
    return jnp.concatenate([g[:, no:no + n] for o, n, no in order], axis=1)


MESH = pl.DeviceIdType.MESH
NCHIP = 4
NDEV = 8

_W_IN = (("w_in", D, NI // NCHIP, 1),)
_W_REST = (("w_a_out", D // NCHIP, D, 0), ("w_s_out", DI // NCHIP, D, 0), ("w_o", D // NCHIP, D, 0),
           ("w_up", D, 2 * FF // NCHIP, 1), ("w_down", FF // NCHIP, D, 0))


def _slab_rows(group):
    rows = [r * c // LANES for _, r, c, _ in group]
    assert all(n % 32 == 0 for n in rows), rows
    return rows


def _coords():
    return lax.axis_index("x"), lax.axis_index("y"), lax.axis_index("c")


def _other_chips(x, y):
    return [(1 - x, y), (x, 1 - y), (1 - x, 1 - y)]


def _ag_weights(shard):
    nrows = shard.shape[0]
    hr = nrows // 2

    def body(x_ref, out_ref, send_sems, recv_sems, local_sem):
        x, y, c = _coords()
        me = 2 * x + y
        chips = _other_chips(x, y)

        def rows(s, h):
            return out_ref.at[s, pl.ds(h * hr, hr), :]

        def copy(k, s, h, to, src=None):
            return pltpu.make_async_remote_copy(
                src_ref=rows(s, h) if src is None else src, dst_ref=rows(s, h),
                send_sem=send_sems.at[k], recv_sem=recv_sems.at[k], device_id=to, device_id_type=MESH)

        mine = pltpu.make_async_copy(x_ref, out_ref.at[me], local_sem)
        mine.start()
        first = [copy(k, me, c, (*chip, c), src=x_ref.at[pl.ds(c * hr, hr), :]) for k, chip in enumerate(chips)]
        for cp in first:
            cp.start()
        passed = []
        for k, chip in enumerate(chips):
            s = 2 * chip[0] + chip[1]
            copy(k, s, c, (x, y, c)).wait_recv()
            fwd = copy(3 + k, s, c, (x, y, 1 - c))
            fwd.start()
            passed.append(fwd)
        for k, chip in enumerate(chips):
            copy(3 + k, 2 * chip[0] + chip[1], 1 - c, (x, y, c)).wait_recv()
        for cp in first + passed:
            cp.wait_send()
        mine.wait()

    return pl.pallas_call(
        body, name="ag_weights", in_specs=[ANY], out_specs=ANY,
        out_shape=jax.ShapeDtypeStruct((NCHIP,) + shard.shape, shard.dtype),
        scratch_shapes=[pltpu.SemaphoreType.DMA((6,)), pltpu.SemaphoreType.DMA((6,)), pltpu.SemaphoreType.DMA],
        compiler_params=pltpu.CompilerParams(has_side_effects=True),
    )(shard)


HBM = pl.BlockSpec(memory_space=pltpu.HBM)
SEM = pl.BlockSpec(memory_space=pltpu.SEMAPHORE)
_EFFECT = pltpu.SideEffectType.DATAFLOW_SIDE_EFFECTING
_NCOPY = NCHIP - 1


def _plan_bcast(src_ref, land_ref, send_sems, recv_sems):
    x, y, c = _coords()
    sends, lands = [], []
    for k, chip in enumerate(_other_chips(x, y)):
        def copy(slot):
            return pltpu.make_async_remote_copy(
                src_ref=src_ref, dst_ref=land_ref.at[slot], send_sem=send_sems.at[k], recv_sem=recv_sems.at[k],
                device_id=(*chip, c), device_id_type=MESH)
        sends.append(copy(2 * x + y))
        lands.append(copy(2 * chip[0] + chip[1]))
    return sends, lands


def _plan_scatter(src_ref, land_ref, send_sems, recv_sems):
    x, y, c = _coords()
    cps = [pltpu.make_async_remote_copy(
        src_ref=src_ref.at[2 * chip[0] + chip[1]], dst_ref=land_ref.at[k], send_sem=send_sems.at[k],
        recv_sem=recv_sems.at[k], device_id=(*chip, c), device_id_type=MESH)
        for k, chip in enumerate(_other_chips(x, y))]
    return cps, cps


def _split_start(name, src, land, plan):
    def body(src_ref, land_ref, send_sems, recv_sems, src_thru, land_thru, token):
        for cp in plan(src_ref, land_ref, send_sems, recv_sems)[0]:
            cp.start()
        token[...] = jnp.zeros_like(token)

    send_sems, recv_sems, src_thru, land_thru, token = pl.pallas_call(
        body, name=name,
        out_shape=(pltpu.SemaphoreType.DMA((_NCOPY,)), pltpu.SemaphoreType.DMA((_NCOPY,)),
                   pltpu.HBM(src.shape, src.dtype), pltpu.HBM(land.shape, land.dtype),
                   jax.ShapeDtypeStruct((8, LANES), F32)),
        in_specs=(HBM, HBM), out_specs=(SEM, SEM, HBM, HBM, pl.BlockSpec(memory_space=pltpu.VMEM)),
        input_output_aliases={0: 2, 1: 3},
        compiler_params=pltpu.CompilerParams(has_side_effects=_EFFECT),
    )(pltpu.with_memory_space_constraint(src, pltpu.HBM), pltpu.with_memory_space_constraint(land, pltpu.HBM))
    return (send_sems, recv_sems, src_thru, land_thru), token


def _split_wait(name, handle, after, plan):
    send_sems, recv_sems, src_thru, land_thru = handle

    def body(src_ref, land_ref, send_sems, recv_sems, after_ref, src_out, land_out):
        sends, lands = plan(src_ref, land_ref, send_sems, recv_sems)
        for cp in sends:
            cp.wait_send()
        for cp in lands:
            cp.wait_recv()

    return pl.pallas_call(
        body, name=name,
        out_shape=(pltpu.HBM(src_thru.shape, src_thru.dtype), pltpu.HBM(land_thru.shape, land_thru.dtype)),
        in_specs=(HBM, HBM, SEM, SEM, ANY), out_specs=(HBM, HBM), input_output_aliases={0: 0, 1: 1},
        compiler_params=pltpu.CompilerParams(has_side_effects=_EFFECT),
    )(src_thru, land_thru, send_sems, recv_sems, after)


def _tie(x, token, name):
    def body(x_ref, t_ref, o_ref):
        pass

    return pl.pallas_call(
        body, name=name, in_specs=[ANY, pl.BlockSpec(memory_space=pltpu.VMEM)], out_specs=ANY,
        out_shape=jax.ShapeDtypeStruct(x.shape, x.dtype), input_output_aliases={0: 0},
    )(x, token)


def _swap_sibling(p, name):
    def body(p_ref, land_ref, send_sem, recv_sem):
        x, y, c = _coords()
        cp = pltpu.make_async_remote_copy(
            src_ref=p_ref, dst_ref=land_ref, send_sem=send_sem, recv_sem=recv_sem,
            device_id=(x, y, 1 - c), device_id_type=MESH)
        cp.start()
        cp.wait()

    return pl.pallas_call(
        body, name=name, in_specs=[ANY], out_specs=ANY, out_shape=jax.ShapeDtypeStruct(p.shape, p.dtype),
        scratch_shapes=[pltpu.SemaphoreType.DMA, pltpu.SemaphoreType.DMA],
        compiler_params=pltpu.CompilerParams(has_side_effects=True),
    )(p)


_ADD_BYTES = 7 << 19


def _add_tile(rows, cols):
    best = 32
    for t in range(32, rows + 1, 32):
        if rows % t == 0 and t * cols * 4 <= _ADD_BYTES:
            best = t
    return best


def _add_slabs(pack, land, me, name):
    rows, cols = pack.shape[1:]
    tr = _add_tile(rows, cols)

    def body(me_ref, p_ref, l_ref, o_ref):
        f = lambda r: r.astype(F32)
        o_ref[...] = ((f(p_ref[0]) + f(l_ref[0])) + f(l_ref[1])) + f(l_ref[2])

    return pl.pallas_call(
        body, name=name,
        grid_spec=pltpu.PrefetchScalarGridSpec(
            num_scalar_prefetch=1, grid=(rows // tr,),
            in_specs=[pl.BlockSpec((1, tr, cols), lambda i, me_ref: (me_ref[0], i, 0)),
                      pl.BlockSpec((_NCOPY, tr, cols), lambda i, me_ref: (0, i, 0))],
            out_specs=pl.BlockSpec((tr, cols), lambda i, me_ref: (i, 0))),
        out_shape=jax.ShapeDtypeStruct((rows, cols), F32),
        compiler_params=_params(("parallel",)),
    )(me, pack, land)


def _add_pair(a, b, name):
    rows, cols = a.shape
    tr = _add_tile(rows, cols)

    def body(a_ref, b_ref, o_ref):
        o_ref[...] = a_ref[...] + b_ref[...]

    blk = pl.BlockSpec((tr, cols), lambda i: (i, 0))
    return pl.pallas_call(
        body, name=name, grid=(rows // tr,), in_specs=[blk, blk], out_specs=blk,
        out_shape=jax.ShapeDtypeStruct((rows, cols), F32), compiler_params=_params(("parallel",)),
    )(a, b)


_STAGE_W = 1024


def _stage_rows(shapes):
    pieces, r = [], 0
    for i, (k, w) in enumerate(shapes):
        for a in range(k):
            for q in range(0, w, _STAGE_W):
                pieces.append((i, a, q, min(_STAGE_W, w - q), r))
                r += 1
    return pieces, -(-r // 8) * 8


def _gather8(parts, reduce, name):
    shapes = [p.shape for p in parts]
    pieces, rows = _stage_rows(shapes)
    n = len(parts)

    def body(*refs):
        ins, outs = refs[:n], refs[n:2 * n]
        stage, buf, res, send_sems, recv_sems = refs[2 * n:]
        x, y, c = _coords()
        me = 4 * x + 2 * y + c
        stage[...] = jnp.zeros_like(stage)
        for i, a, q, w, r in pieces:
            stage[r:r + 1, 0:w] = ins[i][a:a + 1, q:q + w]
        buf[pl.ds(me, 1)] = stage[...][None]
        cps, lands = [], []
        for k in range(1, NDEV):
            peer = (1 - x if k & 4 else x, 1 - y if k & 2 else y, 1 - c if k & 1 else c)

            def copy(slot):
                return pltpu.make_async_remote_copy(
                    src_ref=stage, dst_ref=buf.at[slot], send_sem=send_sems.at[k - 1],
                    recv_sem=recv_sems.at[k - 1], device_id=peer, device_id_type=MESH)

            cps.append(copy(me))
            lands.append(copy(4 * peer[0] + 2 * peer[1] + peer[2]))
        for cp in cps:
            cp.start()
        for cp, land in zip(cps, lands):
            land.wait_recv()
            cp.wait_send()
        if reduce:
            acc = buf[0]
            for d in range(1, NDEV):
                acc = acc + buf[d]
            res[...] = acc
            for i, a, q, w, r in pieces:
                outs[i][a:a + 1, q:q + w] = res[r:r + 1, 0:w]
        else:
            for i, a, q, w, r in pieces:
                for s in range(NCHIP):
                    outs[i][s, a:a + 1, q:q + w] = buf[2 * s, r:r + 1, 0:w]

    vm = pl.BlockSpec(memory_space=pltpu.VMEM)
    out_shapes = [jax.ShapeDtypeStruct(s if reduce else (NCHIP,) + s, F32) for s in shapes]
    return pl.pallas_call(
        body, name=name, in_specs=[vm] * n, out_specs=[vm] * n, out_shape=out_shapes,
        scratch_shapes=[pltpu.VMEM((rows, _STAGE_W), F32), pltpu.VMEM((NDEV, rows, _STAGE_W), F32),
                        pltpu.VMEM((rows, _STAGE_W), F32), pltpu.SemaphoreType.DMA((NDEV - 1,)),
                        pltpu.SemaphoreType.DMA((NDEV - 1,))],
        compiler_params=pltpu.CompilerParams(has_side_effects=True),
    )(*parts)


def _adamw_update(w_ref, g_ref, m_ref, v_ref, d_ref, mo_ref, vo_ref):
    c1 = 1.0 / (1.0 - ADAM_B1 ** ADAM_STEP)
    c2 = 1.0 / (1.0 - ADAM_B2 ** ADAM_STEP)
    gv = g_ref[...]
    mn = ADAM_B1 * m_ref[...] + (1.0 - ADAM_B1) * gv
    vn = ADAM_B2 * v_ref[...] + (1.0 - ADAM_B2) * (gv * gv)
    d_ref[...] = -ADAM_LR * ((mn * c1) / (jnp.sqrt(vn * c2) + ADAM_EPS) + ADAM_WD * w_ref[...])
    mo_ref[...] = mn
    vo_ref[...] = vn


def _adamw_small(ws, gs, ms, vs):
    n = len(ws)

    def body(*refs):
        for i in range(n):
            _adamw_update(*(refs[j * n + i] for j in range(7)))

    vm = pl.BlockSpec(memory_space=pltpu.VMEM)
    outs = pl.pallas_call(
        body, name="adamw_small", in_specs=[vm] * (4 * n), out_specs=[vm] * (3 * n),
        out_shape=[jax.ShapeDtypeStruct(w.shape, F32) for w in ws] * 3,
    )(*ws, *gs, *ms, *vs)
    return outs[:n], outs[n:2 * n], outs[2 * n:]


def _adamw(w, g, m, v, name):
    rows, cols = w.shape
    tr = rows
    while tr * cols * 4 > (3 << 19) and tr % 16 == 0:
        tr //= 2

    def body(*refs):
        _adamw_update(*refs)

    blk = pl.BlockSpec((tr, cols), lambda i: (i, 0))
    return pl.pallas_call(
        body, name=name, grid=(rows // tr,), in_specs=[blk] * 4, out_specs=[blk] * 3,
        out_shape=[jax.ShapeDtypeStruct((rows, cols), F32)] * 3, compiler_params=_params(("parallel",)),
    )(w, g, m, v)


def _rows128(a, mult=8):
    flat = a.reshape(-1)
    n = -(-flat.shape[0] // (LANES * mult)) * LANES * mult
    return jnp.pad(flat, (0, n - flat.shape[0])).reshape(-1, LANES)


def _pack_rows(parts, total_rows):
    rows = sum(p.shape[0] for p in parts)
    if total_rows > rows:
        parts = list(parts) + [jnp.zeros((total_rows - rows, LANES), parts[0].dtype)]
    return jnp.concatenate(parts, axis=0)


def _unpack_rows(pack, shapes, mult=8):
    out, r = [], 0
    for shp in shapes:
        n = int(np.prod(shp))
        nr = -(-n // (LANES * mult)) * mult
        out.append(pack[r:r + nr].reshape(-1)[:n].reshape(shp))
        r += nr
    return out


def _unpack_full(full, group):
    out, r = {}, 0
    for (name, rr, cc, axis), nr in zip(group, _slab_rows(group)):
        seg = full[:, r:r + nr].reshape(NCHIP, rr, cc)
        out[name] = seg.reshape(NCHIP * rr, cc) if axis == 0 else seg.transpose(1, 0, 2).reshape(rr, NCHIP * cc)
        r += nr
    return out


def _by_chip(g, rr, cc, axis):
    return g.reshape(NCHIP, rr, cc) if axis == 0 else g.reshape(rr, NCHIP, cc).transpose(1, 0, 2)


def _pack_by_chip(grads, group, dtype):
    parts = [_by_chip(grads[name].astype(dtype), rr, cc, axis).reshape(NCHIP, nr, LANES)
             for (name, rr, cc, axis), nr in zip(group, _slab_rows(group))]
    return jnp.concatenate(parts, axis=1)


_SMALL_REPL = ("norm_mix_w", "ssd_conv_b", "dt_bias", "a_log", "d_skip", "ssd_norm_w", "norm_ffn_w",
               "ffn_conv_b", "final_norm_w")
_SMALL_CONV = (("conv_a_w", 3, D), ("ssd_conv_w", 4, DX), ("ffn_conv_w", 3, FF))


def kernel(x, norm_mix_w, w_in, conv_a_w, w_a_out, ssd_conv_w, ssd_conv_b, dt_bias, a_log, d_skip, ssd_norm_w, w_s_out, w_o, norm_ffn_w, w_up, ffn_conv_w, ffn_conv_b, w_down, final_norm_w, loss_target, m_norm_mix_w, m_w_in, m_conv_a_w, m_w_a_out, m_ssd_conv_w, m_ssd_conv_b, m_dt_bias, m_a_log, m_d_skip, m_ssd_norm_w, m_w_s_out, m_w_o, m_norm_ffn_w, m_w_up, m_ffn_conv_w, m_ffn_conv_b, m_w_down, m_final_norm_w, v_norm_mix_w, v_w_in, v_conv_a_w, v_w_a_out, v_ssd_conv_w, v_ssd_conv_b, v_dt_bias, v_a_log, v_d_skip, v_ssd_norm_w, v_w_s_out, v_w_o, v_norm_ffn_w, v_w_up, v_ffn_conv_w, v_ffn_conv_b, v_w_down, v_final_norm_w):
    names = ("norm_mix_w", "w_in", "conv_a_w", "w_a_out", "ssd_conv_w", "ssd_conv_b", "dt_bias", "a_log", "d_skip",
             "ssd_norm_w", "w_s_out", "w_o", "norm_ffn_w", "w_up", "ffn_conv_w", "ffn_conv_b", "w_down", "final_norm_w")
    W = dict(zip(names, (norm_mix_w, w_in, conv_a_w, w_a_out, ssd_conv_w, ssd_conv_b, dt_bias, a_log, d_skip,
                         ssd_norm_w, w_s_out, w_o, norm_ffn_w, w_up, ffn_conv_w, ffn_conv_b, w_down, final_norm_w)))
    M = dict(zip(names, (m_norm_mix_w, m_w_in, m_conv_a_w, m_w_a_out, m_ssd_conv_w, m_ssd_conv_b, m_dt_bias, m_a_log,
                         m_d_skip, m_ssd_norm_w, m_w_s_out, m_w_o, m_norm_ffn_w, m_w_up, m_ffn_conv_w, m_ffn_conv_b,
                         m_w_down, m_final_norm_w)))
    V = dict(zip(names, (v_norm_mix_w, v_w_in, v_conv_a_w, v_w_a_out, v_ssd_conv_w, v_ssd_conv_b, v_dt_bias, v_a_log,
                         v_d_skip, v_ssd_norm_w, v_w_s_out, v_w_o, v_norm_ffn_w, v_w_up, v_ffn_conv_w, v_ffn_conv_b,
                         v_w_down, v_final_norm_w)))
    two_d = lambda a: a.reshape(-1, a.shape[-1])
    W2, M2, V2 = ({k: two_d(a) for k, a in t.items()} for t in (W, M, V))
    xi, yi, ci = _coords()
    me = 2 * xi + yi

    meidx = me.reshape(1).astype(jnp.int32)
    state = {}


    class Hooks(_Hooks):
        def after_norm(self, u):
            return _tie(u, state["rest_token"], "tie_ag_rest")

        def late_weights(self, wts, after):
            own, land = _split_wait("ag_rest_wait", state["rest"], after, _plan_bcast)
            full = _unpack_full(lax.dynamic_update_slice(land, own[None], (me, 0, 0)), _W_REST)
            return {**wts, **full}

        def grads_ready(self, grads, tie):
            if "w_in" in grads:
                key = "g_in"
                pack = _by_chip(_unpermute_w_in(grads["w_in"]).astype(BF16), *_W_IN[0][1:])
            else:
                key, pack = "g_rest", _pack_by_chip(grads, _W_REST, BF16)
            land = lax.empty((_NCOPY,) + pack.shape[1:], BF16)
            state[key], token = _split_start("rs_" + key + "_start", pack, land, _plan_scatter)
            return _tie(tie, token, "tie_" + key)

        def mark(self, name, value):
            state[name] = value

    def reduced(key, after):
        pack, land = _split_wait("rs_" + key + "_wait", state[key], after, _plan_scatter)
        mine = _add_slabs(pack, land, meidx, "rs_" + key + "_add_chips")
        return _add_pair(mine, _swap_sibling(mine, "rs_" + key + "_swap"), "rs_" + key + "_add_cores")

    w_in_full = _ag_weights(W2["w_in"].astype(BF16)).transpose(1, 0, 2).reshape(D, NI)
    wts = {k: W2[k] for k in _SMALL_REPL}
    conv_by_chip = _gather8([W2[n] for n, *_ in _SMALL_CONV], False, "ag_conv_weights")
    for (n, kk, width), stacked in zip(_SMALL_CONV, conv_by_chip):
        wts[n] = stacked.transpose(1, 0, 2).reshape(kk, width)
    rest_slab = _tie(_pack_rows([_rows128(W2[n], 16) for n, *_ in _W_REST], 0).astype(BF16), conv_by_chip[0],
                     "tie_ag_order")
    state["rest"], state["rest_token"] = _split_start(
        "ag_rest_start", rest_slab, lax.empty((NCHIP,) + rest_slab.shape, BF16), _plan_bcast)
    wts["w_in"] = _permute_w_in(w_in_full)

    loss8, grad_x, grads = _local_step(x[0], loss_target[0], wts, Hooks())

    gbig = dict(zip([n for n, *_ in _W_REST],
                    _unpack_rows(reduced("g_rest", state["ssd_bwd"]), [(rr, cc) for _, rr, cc, _ in _W_REST], 16)))
    gbig["w_in"] = reduced("g_in", grad_x)

    small_parts = [grads[n] for n in _SMALL_REPL] + [loss8[0:1]] + [grads[n] for n, *_ in _SMALL_CONV]
    small_g = _gather8(small_parts, True, "allreduce_small")
    gsm = dict(zip(_SMALL_REPL, small_g[:len(_SMALL_REPL)]))
    loss = small_g[len(_SMALL_REPL)][0, 0]
    for (n, kk, width), gfull in zip(_SMALL_CONV, small_g[len(_SMALL_REPL) + 1:]):
        cw = width // NCHIP
        gsm[n] = lax.dynamic_slice(gfull, (0, me * cw), (kk, cw))

    G, DW, NM, NV = {}, {}, {}, {}
    for n in [b[0] for b in _W_IN + _W_REST]:
        G[n] = gbig[n]
        DW[n], NM[n], NV[n] = _adamw(W2[n], G[n], M2[n], V2[n], "adamw_" + n)
    sm_names = list(_SMALL_REPL) + [n for n, *_ in _SMALL_CONV]
    outs = _adamw_small(*([t[n] for n in sm_names] for t in (W2, gsm, M2, V2)))
    for t, vals in zip((DW, NM, NV), outs):
        t.update(zip(sm_names, vals))
    G.update(gsm)

    def shaped(t):
        return [t[n].reshape(W[n].shape) for n in names]

    return (loss, grad_x.reshape(x.shape), *shaped(G), *shaped(DW), *shaped(NM), *shaped(NV))
```

```python
import functools

import jax
import jax.numpy as jnp
import numpy as np
from jax import lax
from jax.experimental import pallas as pl
from jax.experimental.pallas import tpu as pltpu

F32 = jnp.float32
BF16 = jnp.bfloat16

D = 1024
DI = 2048
NH = 32
HP = 64
NG = 4
NS = 128
CH = 128
DX = 3072
FF = 2816
NI = 10272
EPS = 1e-5

OFF_BCV, OFF_XBC, OFF_G, OFF_Z, OFF_DT = 0, 3072, 6144, 8192, 10240
NIP = 10752
_SEGS = ((0, 2048, OFF_G), (2048, 3072, OFF_BCV), (5120, 2048, OFF_Z), (7168, 3072, OFF_XBC), (10240, 32, OFF_DT))

LANES = 128
HALO = 16
V7X_VMEM_LIMIT = 56 * 2 ** 20

ADAM_LR, ADAM_B1, ADAM_B2, ADAM_EPS, ADAM_WD, ADAM_STEP = 0.001, 0.9, 0.999, 1e-08, 0.01, 10

NN = (((1,), (0,)), ((), ()))
NT = (((1,), (1,)), ((), ()))
TN = (((0,), (0,)), ((), ()))


def _dot(a, b, dims=NN):
    return lax.dot_general(a, b, dims, preferred_element_type=F32)


def _params(sem, **kw):
    return pltpu.CompilerParams(dimension_semantics=sem, vmem_limit_bytes=V7X_VMEM_LIMIT, **kw)


V7X_MXU = 256
V7X_HBM_BYTES_PER_S = 3.5e12
STEP_S = 0.35e-6
MATMUL_VMEM = 40 * 2 ** 20


ACC_BYTES_PER_S = 4e12


def _divisors(dim, cap, units):
    for unit in units:
        c = [t for t in range(unit, min(dim, cap) + 1, unit) if dim % t == 0]
        if c:
            return c
    return [dim]


def _tiles(M, N, K, out_bytes, has_res):
    best = None
    for tn in _divisors(N, 2816, (V7X_MXU, LANES)):
        for tm in _divisors(M, 2816, (LANES,)):
            for tk in _divisors(K, 2816, (V7X_MXU, LANES)):
                nk, ni, nj = K // tk, M // tm, N // tn
                vmem = 4 * (tm * tk + tk * tn) + 2 * tm * tn * out_bytes
                vmem += (4 * tm * tn if nk > 1 else 0) + (8 * tm * tn if has_res else 0)
                if vmem > MATMUL_VMEM:
                    continue
                a_reads = M * K * 2 * (nj if nk > 1 else 1)
                b_reads = K * N * 2 * (ni if nk * nj > 1 else 1)
                cost = (a_reads + b_reads + M * N * out_bytes) / V7X_HBM_BYTES_PER_S + ni * nj * nk * STEP_S
                cost += (nk - 1) * M * N * 8 / ACC_BYTES_PER_S
                if best is None or cost < best[0]:
                    best = (cost, tm, tn, tk)
    assert best is not None, (M, N, K)
    return best[1:]


def _sigmoid(x):
    return 1.0 / (1.0 + jnp.exp(-x))


class _Epilogue:
    def __init__(self, fn, ins, outs, tile_bytes):
        self.fn, self.ins, self.outs, self.tile_bytes = fn, tuple(ins), tuple(outs), tile_bytes


def _matmul(a, b, *, mode, out_dtype, name, residual=None, b_k_off=0, epilogue=None):
    if mode == "nn":
        (M, K), (K2, N) = a.shape, b.shape
    elif mode == "nt":
        (M, K), (N, K2) = a.shape, (b.shape[0], a.shape[1])
        assert b_k_off + K <= b.shape[1]
    else:
        (K, M), (K2, N) = a.shape, b.shape
    assert K == K2, (name, a.shape, b.shape)
    tm, tn, tk = _tiles(M, N, K, jnp.dtype(out_dtype).itemsize, residual is not None)
    if epilogue is not None:
        tn = N
        tm = min(tm, 1024)
        while 4 * (tm * tk + tk * tn) + 12 * tm * tn + 2 * tm * epilogue.tile_bytes > MATMUL_VMEM and tm % 256 == 0:
            tm //= 2
    nk = K // tk
    if mode == "tn":
        a_spec = pl.BlockSpec((tk, tm), lambda i, j, k: (k, i))
    else:
        a_spec = pl.BlockSpec((tm, tk), lambda i, j, k: (i, k))
    if mode == "nt":
        assert b_k_off % tk == 0
        b_spec = pl.BlockSpec((tn, tk), lambda i, j, k: (j, k + b_k_off // tk))
    else:
        b_spec = pl.BlockSpec((tk, tn), lambda i, j, k: (k, j))
    dims = {"nn": NN, "nt": NT, "tn": TN}[mode]
    o_spec = pl.BlockSpec((tm, tn), lambda i, j, k: (i, j))
    has_res = residual is not None

    def rows_or_whole(shape):
        if shape[0] == M:
            return pl.BlockSpec((tm,) + tuple(shape[1:]), lambda i, j, k: (i,) + (0,) * (len(shape) - 1))
        return pl.BlockSpec(tuple(shape), lambda i, j, k: (0,) * len(shape))

    n_in = 2 + has_res + (len(epilogue.ins) if epilogue else 0)
    n_out = len(epilogue.outs) if epilogue else 1

    def body(*refs):
        a_ref, b_ref = refs[:2]
        r_ref = refs[2] if has_res else None
        out_refs = refs[n_in:n_in + n_out]
        acc_ref = refs[-1]
        k = pl.program_id(2)
        part = _dot(a_ref[...], b_ref[...], dims)

        def finish(r):
            if has_res:
                r = r + r_ref[...].astype(F32)
            if epilogue is None:
                out_refs[0][...] = r.astype(out_dtype)
            else:
                epilogue.fn(r, refs[2 + has_res:n_in], out_refs, pl.program_id(0) == 0)

        if nk == 1:
            finish(part)
            return

        @pl.when(k == 0)
        def _():
            acc_ref[...] = part

        @pl.when(jnp.logical_and(k > 0, k < nk - 1))
        def _():
            acc_ref[...] += part

        @pl.when(k == nk - 1)
        def _():
            finish(acc_ref[...] + part)

    in_specs = [a_spec, b_spec] + ([o_spec] if has_res else [])
    args = (a, b) + ((residual,) if has_res else ())
    if epilogue is None:
        out_specs, out_shape = o_spec, jax.ShapeDtypeStruct((M, N), out_dtype)
        sem = ("parallel", "parallel", "arbitrary")
    else:
        in_specs += [rows_or_whole(x.shape) for x in epilogue.ins]
        args += epilogue.ins
        out_specs = [rows_or_whole(shp) for shp, _ in epilogue.outs]
        out_shape = [jax.ShapeDtypeStruct(shp, dt) for shp, dt in epilogue.outs]
        sem = ("arbitrary", "arbitrary", "arbitrary")
    return pl.pallas_call(
        body, name=name, grid=(M // tm, N // tn, nk), in_specs=in_specs, out_specs=out_specs,
        out_shape=out_shape, scratch_shapes=[pltpu.VMEM((tm, tn), F32)] if nk > 1 else [],
        compiler_params=_params(sem),
    )(*args)


class _Rows:
    def __init__(self, T, tm):
        self.T, self.tm = T, min(tm, T // 2)
        self.nrow = T // self.tm
        self.r = self.tm // HALO
        self.nb = T // HALO

    def tile(self, w, cb=0, step=1):
        return pl.BlockSpec((self.tm, w), lambda j, i: (i, cb + step * j))

    def prev(self, w, cb=0, step=1):
        r = self.r
        return pl.BlockSpec((HALO, w), lambda j, i: (jnp.maximum(i * r - 1, 0), cb + step * j))

    def next(self, w, cb=0, step=1):
        r, nb = self.r, self.nb
        return pl.BlockSpec((HALO, w), lambda j, i: (jnp.minimum((i + 1) * r, nb - 1), cb + step * j))

    def colvec(self, k, w, cb=0, step=1):
        return pl.BlockSpec((k, w), lambda j, i: (0, cb + step * j))

    def call(self, body, name, ncol, in_specs, out_specs, out_shape, args, aliases=None):
        return pl.pallas_call(
            body, name=name, grid=(ncol, self.nrow), in_specs=in_specs, out_specs=out_specs,
            out_shape=out_shape, input_output_aliases=aliases or {},
            compiler_params=_params(("parallel", "arbitrary")),
        )(*args)


ANY = pl.BlockSpec(memory_space=pl.ANY)


def _shifts_causal(ext, nk, tm):
    out = []
    for k in range(nk):
        s = nk - 1 - k
        r = ext if s == 0 else pltpu.roll(ext, s, 0)
        out.append(r[HALO:])
    return out


def _shifts_anticausal(ext, nk, tm):
    n = ext.shape[0]
    out = []
    for k in range(nk):
        s = nk - 1 - k
        r = ext if s == 0 else pltpu.roll(ext, n - s, 0)
        out.append(r[:tm])
    return out


def _wsum(w, parts):
    acc = w[0:1, :] * parts[0]
    for k in range(1, len(parts)):
        acc = acc + w[k:k + 1, :] * parts[k]
    return acc


def _colsum(x):
    return jnp.sum(x, axis=0, keepdims=True)


def _acc_out(ref, val, first):
    @pl.when(first)
    def _():
        ref[...] = val

    @pl.when(jnp.logical_not(first))
    def _():
        ref[...] += val


def _acc_rows(ref, rows, first):
    for k, r in enumerate(rows):
        _acc_out(ref.at[k:k + 1, :], r, first)


def _norm_matmul(x, wn, b, name):
    T, N = x.shape[0], b.shape[1]
    tm = min(1024, T)
    tn = max(t for t in _divisors(N, 2816, (V7X_MXU, LANES))
             if 8 * tm * D + 6 * tm * D + 4 * D * t + 4 * tm * t <= MATMUL_VMEM)

    def body(x_ref, wn_ref, b_ref, o_ref, u_ref, keep_ref):
        @pl.when(pl.program_id(1) == 0)
        def _():
            xv = x_ref[...]
            r = lax.rsqrt(jnp.mean(xv * xv, axis=-1, keepdims=True) + EPS)
            u = (xv * r * wn_ref[...]).astype(BF16)
            keep_ref[...] = u
            u_ref[...] = u

        o_ref[...] = _dot(keep_ref[...], b_ref[...]).astype(BF16)

    rows = pl.BlockSpec((tm, D), lambda i, j: (i, 0))
    return pl.pallas_call(
        body, name=name, grid=(T // tm, N // tn),
        in_specs=[rows, pl.BlockSpec((1, D), lambda i, j: (0, 0)), pl.BlockSpec((D, tn), lambda i, j: (0, j))],
        out_specs=[pl.BlockSpec((tm, tn), lambda i, j: (i, j)), rows],
        out_shape=[jax.ShapeDtypeStruct((T, N), BF16), jax.ShapeDtypeStruct((T, D), BF16)],
        scratch_shapes=[pltpu.VMEM((tm, D), BF16)],
        compiler_params=_params(("parallel", "arbitrary")),
    )(x, wn, b)


def _rmsnorm_bwd_epilogue(x, w, dres):
    T = x.shape[0]

    def fn(dyv, ins, outs, first):
        x_ref, w_ref, dr_ref = ins
        dx_ref, dxb_ref, dw_ref = outs
        xv = x_ref[...]
        r = lax.rsqrt(jnp.mean(xv * xv, axis=-1, keepdims=True) + EPS)
        xh = xv * r
        dxh = dyv * w_ref[...]
        dx = r * (dxh - xh * jnp.mean(dxh * xh, axis=-1, keepdims=True)) + dr_ref[...]
        dx_ref[...] = dx
        dxb_ref[...] = dx.astype(BF16)
        _acc_out(dw_ref, _colsum(dyv * xh), first)

    return _Epilogue(fn, (x, w, dres), (((T, D), F32), ((T, D), BF16), ((1, D), F32)), 14 * D)


def _branch_a_fwd(proj, conv_w):
    T = proj.shape[0]
    R = _Rows(T, 512)
    tm = R.tm

    def body(p_ref, pp_ref, w_ref, o_ref):
        keep = (pl.program_id(1) > 0).astype(F32)
        cv = p_ref[:, D:2 * D].astype(F32) * p_ref[:, 2 * D:].astype(F32)
        cvp = pp_ref[:, D:2 * D].astype(F32) * pp_ref[:, 2 * D:].astype(F32) * keep
        sh = _shifts_causal(jnp.concatenate([cvp, cv], axis=0), 3, tm)
        ca = _wsum(w_ref[...], sh)
        o_ref[...] = (p_ref[:, :D].astype(F32) * ca).astype(BF16)

    return R.call(body, "branch_a_fwd", 1, [R.tile(3 * D), R.prev(3 * D), R.colvec(3, D)], R.tile(D),
                  jax.ShapeDtypeStruct((T, D), BF16), (proj, proj, conv_w))


def _branch_a_bwd(dya_in, proj, conv_w, dproj):
    T = proj.shape[0]
    R = _Rows(T, 256)
    tm = R.tm

    def body(d_ref, dn_ref, p_ref, pp_ref, pn_ref, w_ref, _alias, o_ref, dw_ref):
        i = pl.program_id(1)
        keep_p = (i > 0).astype(F32)
        keep_n = (i < R.nrow - 1).astype(F32)
        w = w_ref[...]
        b = p_ref[:, :D].astype(F32)
        c = p_ref[:, D:2 * D].astype(F32)
        v = p_ref[:, 2 * D:].astype(F32)
        cvp = pp_ref[:, D:2 * D].astype(F32) * pp_ref[:, 2 * D:].astype(F32) * keep_p
        sh = _shifts_causal(jnp.concatenate([cvp, c * v], axis=0), 3, tm)
        ca = _wsum(w, sh)
        d = d_ref[...].astype(F32)
        dca = d * b
        dca_n = dn_ref[...].astype(F32) * pn_ref[:, :D].astype(F32) * keep_n
        dsh = _shifts_anticausal(jnp.concatenate([dca, dca_n], axis=0), 3, tm)
        dcv = _wsum(w, dsh)
        o_ref[:, :D] = (d * ca).astype(BF16)
        o_ref[:, D:2 * D] = (dcv * v).astype(BF16)
        o_ref[:, 2 * D:] = (dcv * c).astype(BF16)
        _acc_rows(dw_ref, [_colsum(dca * s) for s in sh], i == 0)

    return R.call(
        body, "branch_a_bwd", 1,
        [R.tile(D), R.next(D), R.tile(3 * D), R.prev(3 * D), R.next(3 * D), R.colvec(3, D), ANY],
        [R.tile(3 * D), R.colvec(3, D)],
        [jax.ShapeDtypeStruct(dproj.shape, BF16), jax.ShapeDtypeStruct((3, D), F32)],
        (dya_in, dya_in, proj, proj, proj, conv_w, dproj), aliases={6: 0})


_XW = 512


def _xbc_fwd(proj, conv_w, conv_b):
    T = proj.shape[0]
    R = _Rows(T, 512)
    tm = R.tm
    cb = OFF_XBC // _XW

    def body(x_ref, xp_ref, w_ref, b_ref, o_ref):
        keep = (pl.program_id(1) > 0).astype(F32)
        ext = jnp.concatenate([xp_ref[...].astype(F32) * keep, x_ref[...].astype(F32)], axis=0)
        pre = _wsum(w_ref[...], _shifts_causal(ext, 4, tm)) + b_ref[...]
        o_ref[...] = (pre * _sigmoid(pre)).astype(BF16)

    return R.call(body, "xbc_fwd", DX // _XW,
                  [R.tile(_XW, cb), R.prev(_XW, cb), R.colvec(4, _XW), R.colvec(1, _XW)], R.tile(_XW),
                  jax.ShapeDtypeStruct((T, DX), BF16), (proj, proj, conv_w, conv_b))


def _xbc_bwd(dact, proj, conv_w, conv_b, dproj):
    T = proj.shape[0]
    R = _Rows(T, 512)
    tm = R.tm
    cb = OFF_XBC // _XW

    def body(d_ref, dn_ref, x_ref, xp_ref, xn_ref, w_ref, b_ref, _alias, o_ref, dw_ref, db_ref):
        i = pl.program_id(1)
        keep_p = (i > 0).astype(F32)
        keep_n = (i < R.nrow - 1).astype(F32)
        w = w_ref[...]
        ext = jnp.concatenate([xp_ref[...].astype(F32) * keep_p, x_ref[...].astype(F32),
                               xn_ref[...].astype(F32)], axis=0)
        sh = _shifts_causal(ext, 4, tm + HALO)
        pre = _wsum(w, sh) + b_ref[...]
        s = _sigmoid(pre)
        dsilu = s * (1.0 + pre * (1.0 - s))
        dext = jnp.concatenate([d_ref[...].astype(F32), dn_ref[...].astype(F32) * keep_n], axis=0)
        dpre = dext * dsilu
        dsh = _shifts_anticausal(dpre, 4, tm)
        o_ref[...] = _wsum(w, dsh).astype(BF16)
        dp = dpre[:tm]
        _acc_rows(dw_ref, [_colsum(dp * q[:tm]) for q in sh], i == 0)
        _acc_out(db_ref, _colsum(dp), i == 0)

    return R.call(
        body, "xbc_bwd", DX // _XW,
        [R.tile(_XW), R.next(_XW), R.tile(_XW, cb), R.prev(_XW, cb), R.next(_XW, cb),
         R.colvec(4, _XW), R.colvec(1, _XW), ANY],
        [R.tile(_XW, cb), R.colvec(4, _XW), R.colvec(1, _XW)],
        [jax.ShapeDtypeStruct(dproj.shape, BF16), jax.ShapeDtypeStruct((4, DX), F32),
         jax.ShapeDtypeStruct((1, DX), F32)],
        (dact, dact, proj, proj, proj, conv_w, conv_b, dproj), aliases={7: 0})


def _softplus(x):
    return jnp.maximum(x, 0.0) + jnp.log(1.0 + jnp.exp(-jnp.abs(x)))


def _dt_fwd(dt_raw, dt_bias_p, a_log_p):
    T = dt_raw.shape[0]

    def body(r_ref, b_ref, al_ref, dt_ref, ac_ref, acT_ref):
        dt = _softplus(r_ref[...] + b_ref[...])
        s = dt * (-jnp.exp(al_ref[...]))
        row = lax.broadcasted_iota(jnp.int32, (CH, LANES), 0)
        k = 1
        while k < CH:
            s = s + jnp.where(row >= k, pltpu.roll(s, k, 0), 0.0)
            k *= 2
        dt_ref[...] = dt
        ac_ref[...] = s
        acT_ref[...] = s.T

    blk = pl.BlockSpec((CH, LANES), lambda i: (i, 0))
    vec = pl.BlockSpec((1, LANES), lambda i: (0, 0))
    return pl.pallas_call(
        body, name="dt_fwd", grid=(T // CH,), in_specs=[blk, vec, vec], out_specs=[blk, blk, blk],
        out_shape=[jax.ShapeDtypeStruct((T, LANES), F32)] * 3, compiler_params=_params(("parallel",)),
    )(dt_raw, dt_bias_p, a_log_p)


def _dt_bwd(dacum, ddt_x, dt_raw, dt_bias_p, a_log_p, dproj):
    T = dt_raw.shape[0]
    nc = T // CH

    def body(da_ref, dx_ref, r_ref, b_ref, al_ref, _alias, o_ref, db_ref, dal_ref):
        i = pl.program_id(0)
        a = -jnp.exp(al_ref[...])
        z = r_ref[...] + b_ref[...]
        dt = _softplus(z)
        s = da_ref[...]
        row = lax.broadcasted_iota(jnp.int32, (CH, LANES), 0)
        k = 1
        while k < CH:
            s = s + jnp.where(row < CH - k, pltpu.roll(s, CH - k, 0), 0.0)
            k *= 2
        ddt = s * a + dx_ref[...]
        draw = ddt * _sigmoid(z)
        o_ref[:, :LANES] = draw.astype(BF16)
        o_ref[:, LANES:] = jnp.zeros((CH, NIP - OFF_DT - LANES), BF16)
        _acc_out(db_ref, _colsum(draw), i == 0)
        _acc_out(dal_ref, _colsum(s * dt), i == 0)

        @pl.when(i == nc - 1)
        def _():
            dal_ref[...] = dal_ref[...] * a

    blk = pl.BlockSpec((CH, LANES), lambda i: (i, 0))
    vec = pl.BlockSpec((1, LANES), lambda i: (0, 0))
    oblk = pl.BlockSpec((CH, NIP - OFF_DT), lambda i: (i, OFF_DT // (NIP - OFF_DT)))
    return pl.pallas_call(
        body, name="dt_bwd", grid=(nc,), in_specs=[blk, blk, blk, vec, vec, ANY], out_specs=[oblk, vec, vec],
        out_shape=[jax.ShapeDtypeStruct(dproj.shape, BF16), jax.ShapeDtypeStruct((1, LANES), F32),
                   jax.ShapeDtypeStruct((1, LANES), F32)],
        input_output_aliases={5: 0}, compiler_params=_params(("arbitrary",)),
    )(dacum, ddt_x, dt_raw, dt_bias_p, a_log_p, dproj)


_GW = DI // NG
_HG = NH // NG
_NEG = -1e30


def _interleave(gens):
    out, live = [None] * len(gens), list(range(len(gens)))
    while live:
        for i in list(live):
            try:
                next(gens[i])
            except StopIteration as stop:
                out[i] = stop.value
                live.remove(i)
    return out


def _pair_lanes(left, v0, v1):
    return jnp.where(left, v0, v1)


def _ssd_specs(T, rev):
    nc = T // CH
    cm = (lambda c: nc - 1 - c) if rev else (lambda c: c)
    bw = NG * NS
    return dict(
        xs=pl.BlockSpec((CH, DI), lambda c: (cm(c), 0)),
        bm=pl.BlockSpec((CH, bw), lambda c: (cm(c), DI // bw)),
        cmat=pl.BlockSpec((CH, bw), lambda c: (cm(c), DI // bw + 1)),
        xbc=pl.BlockSpec((CH, DX), lambda c: (cm(c), 0)),
        col=pl.BlockSpec((CH, LANES), lambda c: (cm(c), 0)),
        dsk=pl.BlockSpec((1, DI), lambda c: (0, 0)),
        state=pl.BlockSpec((1, NS, DI), lambda c: (cm(c), 0, 0)),
    )


def _last(ref, lo, hi):
    return ref.at[(slice(None),) * (len(ref.shape) - 1) + (slice(lo, hi),)]


def _group_views(g, wide, narrow):
    return [_last(r, g * _GW, (g + 1) * _GW) for r in wide] + [_last(r, g * NS, (g + 1) * NS) for r in narrow]


def _ssd_fwd(xact, dt, acum, acumT, dsk_rep):
    T = xact.shape[0]
    nc = T // CH
    sp = _ssd_specs(T, False)

    def body(*refs):
        xs, bm, cmat, dtr, acr, actr, dsk, y, spv, S_ref = refs

        @pl.when(pl.program_id(0) == 0)
        def _():
            S_ref[...] = jnp.zeros_like(S_ref)

        _interleave([group(g * _HG, dtr[...], acr[...], actr[...],
                           *_group_views(g, (xs, dsk, y, spv, S_ref), (bm, cmat))) for g in range(NG)])

    def group(hb, dt, ac, acT, xs_ref, dsk_ref, y_ref, sp_ref, S_ref, b_ref, c_ref):
        Bm, Cm = b_ref[...], c_ref[...]
        S = S_ref[...]
        sp_ref[0] = S
        cb = _dot(Cm, Bm, NT)
        CS = _dot(Cm, S.astype(BF16))
        row = lax.broadcasted_iota(jnp.int32, (CH, CH), 0)
        col = lax.broadcasted_iota(jnp.int32, (CH, CH), 1)
        tril = row >= col
        left = col < HP
        xd_parts, dec_parts = [], []
        for p in range(_HG // 2):
            sl = slice(p * LANES, (p + 1) * LANES)
            j0, j1 = hb + 2 * p, hb + 2 * p + 1
            xp = xs_ref[:, sl].astype(F32)
            a0, a1 = ac[:, j0:j0 + 1], ac[:, j1:j1 + 1]
            al0, al1 = ac[CH - 1:CH, j0:j0 + 1], ac[CH - 1:CH, j1:j1 + 1]
            X = xp * _pair_lanes(left, dt[:, j0:j0 + 1], dt[:, j1:j1 + 1])
            Xb = X.astype(BF16)
            Ws = [(cb * jnp.exp(jnp.where(tril, aj - acT[j:j + 1, :], _NEG))).astype(BF16)
                  for j, aj in ((j0, a0), (j1, a1))]
            Xs = [jnp.where(m, Xb, jnp.zeros_like(Xb)) for m in (left, jnp.logical_not(left))]
            yield
            yd = _dot(jnp.concatenate(Ws, axis=1), jnp.concatenate(Xs, axis=0))
            yield
            eal = _pair_lanes(left, jnp.exp(a0), jnp.exp(a1))
            y = yd + eal * CS[:, sl] + dsk_ref[:, sl] * xp
            y_ref[:, sl] = y.astype(BF16)
            xd_parts.append(X * _pair_lanes(left, jnp.exp(al0 - a0), jnp.exp(al1 - a1)))
            dec_parts.append(_pair_lanes(left[0:1], jnp.exp(al0), jnp.exp(al1)))
        Xd = jnp.concatenate(xd_parts, axis=1).astype(BF16)
        dec = jnp.concatenate(dec_parts, axis=1)
        S_ref[...] = dec * S + _dot(Bm, Xd, TN)

    return pl.pallas_call(
        body, name="ssd_fwd", grid=(nc,),
        in_specs=[sp["xs"], sp["bm"], sp["cmat"], sp["col"], sp["col"], sp["col"], sp["dsk"]],
        out_specs=[sp["xs"], sp["state"]],
        out_shape=[jax.ShapeDtypeStruct((T, DI), BF16), jax.ShapeDtypeStruct((nc, NS, DI), F32)],
        scratch_shapes=[pltpu.VMEM((NS, DI), F32)],
        compiler_params=_params(("arbitrary",)),
    )(xact, xact, xact, dt, acum, acumT, dsk_rep)


def _ssd_bwd(dy, xact, dt, acum, acumT, dsk_rep, sprev):
    T = xact.shape[0]
    nc = T // CH
    sp = _ssd_specs(T, True)

    def body(*refs):
        xs, bm, cmat, dtr, acr, actr, dsk, dyr, spv, dxa, ddtx, dAc, dskacc, dS_ref = refs
        first = pl.program_id(0) == 0

        @pl.when(first)
        def _():
            dS_ref[...] = jnp.zeros_like(dS_ref)

        dbc = _last(dxa, DI, DX)
        ddtx_sum = jnp.zeros((CH, LANES), F32)
        dAc_sum = jnp.zeros((CH, LANES), F32)
        for a, b in _interleave([group(first, g * _HG, dtr[...], acr[...], actr[...],
                                       *_group_views(g, (xs, dsk, dyr, spv, dxa, dskacc, dS_ref),
                                                     (bm, cmat, dbc, _last(dbc, NG * NS, 2 * NG * NS))))
                                 for g in range(NG)]):
            ddtx_sum, dAc_sum = ddtx_sum + a, dAc_sum + b
        ddtx[...] = ddtx_sum
        dAc[...] = dAc_sum

    def group(first, hb, dt, ac, acT, xs_ref, dsk_ref, dy_ref, sp_ref, dx_ref, dskacc_ref, dS_ref, b_ref, c_ref,
              dB_ref, dC_ref):
        Bm, Cm = b_ref[...], c_ref[...]
        S = sp_ref[0]
        dS = dS_ref[...]
        Sb, dSb = S.astype(BF16), dS.astype(BF16)
        cb = _dot(Cm, Bm, NT)
        cbT = _dot(Bm, Cm, NT)
        CmT = Cm.T
        CS = _dot(Cm, Sb)
        T1 = _dot(Bm, dSb)
        yield
        row = lax.broadcasted_iota(jnp.int32, (CH, CH), 0)
        col = lax.broadcasted_iota(jnp.int32, (CH, CH), 1)
        tril = row >= col
        triu = row <= col
        left = col < HP
        lane8 = lax.broadcasted_iota(jnp.int32, (1, LANES), 1)
        lastrow = lax.broadcasted_iota(jnp.int32, (CH, 1), 0) == CH - 1
        dCB = jnp.zeros((CH, CH), F32)
        dCBT = jnp.zeros((CH, CH), F32)
        dAc = jnp.zeros((CH, LANES), F32)
        ddtx = jnp.zeros((CH, LANES), F32)
        xd_parts, dye_parts, dec_parts, dsk_parts = [], [], [], []
        for p in range(_HG // 2):
            sl = slice(p * LANES, (p + 1) * LANES)
            j0, j1 = hb + 2 * p, hb + 2 * p + 1
            xp = xs_ref[:, sl].astype(F32)
            dyp = dy_ref[:, sl].astype(F32)
            a0, a1 = ac[:, j0:j0 + 1], ac[:, j1:j1 + 1]
            al0, al1 = ac[CH - 1:CH, j0:j0 + 1], ac[CH - 1:CH, j1:j1 + 1]
            dtl = _pair_lanes(left, dt[:, j0:j0 + 1], dt[:, j1:j1 + 1])
            X = xp * dtl
            Xb = X.astype(BF16)
            eal = _pair_lanes(left, jnp.exp(a0), jnp.exp(a1))
            dtel = _pair_lanes(left, jnp.exp(al0 - a0), jnp.exp(al1 - a1))
            T1p = T1[:, sl]
            Rm = T1p * dtel * X
            GR = dyp * (eal * CS[:, sl]) - Rm
            SdS = dS[:, sl] * S[:, sl]
            dXd = jnp.zeros((CH, LANES), F32)
            for j, aj, alj, mask in ((j0, a0, al0, left), (j1, a1, al1, jnp.logical_not(left))):
                dYm = jnp.where(mask, dyp, 0.0).astype(BF16)
                dWm = _dot(dYm, Xb, NT)
                dWmT = _dot(Xb, dYm, NT)
                yield
                e = aj - acT[j:j + 1, :]
                P = dWm * jnp.exp(jnp.where(tril, e, _NEG))
                LmT = jnp.exp(jnp.where(triu, -e, _NEG))
                PT = dWmT * LmT
                dCB = dCB + P
                dCBT = dCBT + PT
                yield
                dXd = dXd + _dot((cbT * LmT).astype(BF16), dYm)
                qd = P * cb - PT * cbT + jnp.where(mask, GR, 0.0)
                colv = jnp.sum(qd, axis=1, keepdims=True)
                tot = jnp.where(mask, Rm + jnp.exp(alj) * SdS, 0.0)
                dalast = jnp.sum(jnp.sum(tot, axis=0, keepdims=True), axis=1, keepdims=True)
                dAc = dAc + (colv + jnp.where(lastrow, dalast, 0.0)) * (lane8 == j).astype(F32)
                yield
            dX = dXd + dtel * T1p
            dXx = dX * xp
            for j, mask in ((j0, left), (j1, jnp.logical_not(left))):
                dd = jnp.sum(jnp.where(mask, dXx, 0.0), axis=1, keepdims=True)
                ddtx = ddtx + dd * (lane8 == j).astype(F32)
            dx_ref[:, sl] = (dX * dtl + dsk_ref[:, sl] * dyp).astype(BF16)
            dsk_parts.append(_colsum(dyp * xp))
            xd_parts.append(X * dtel)
            dye_parts.append(dyp * eal)
            dec_parts.append(_pair_lanes(left[0:1], jnp.exp(al0), jnp.exp(al1)))
            yield
        Xd = jnp.concatenate(xd_parts, axis=1).astype(BF16)
        dYe = jnp.concatenate(dye_parts, axis=1).astype(BF16)
        dec = jnp.concatenate(dec_parts, axis=1)
        dC_ref[...] = (_dot(dCB.astype(BF16), Bm) + _dot(dYe, Sb, NT)).astype(BF16)
        dB_ref[...] = (_dot(dCBT.astype(BF16), Cm) + _dot(Xd, dSb, NT)).astype(BF16)
        dS_ref[...] = _dot(CmT, dYe) + dec * dS
        _acc_out(dskacc_ref, jnp.concatenate(dsk_parts, axis=1), first)
        return ddtx, dAc

    return pl.pallas_call(
        body, name="ssd_bwd", grid=(nc,),
        in_specs=[sp["xs"], sp["bm"], sp["cmat"], sp["col"], sp["col"], sp["col"], sp["dsk"], sp["xs"],
                  sp["state"]],
        out_specs=[sp["xbc"], sp["col"], sp["col"], sp["dsk"]],
        out_shape=[jax.ShapeDtypeStruct((T, DX), BF16), jax.ShapeDtypeStruct((T, LANES), F32),
                   jax.ShapeDtypeStruct((T, LANES), F32), jax.ShapeDtypeStruct((1, DI), F32)],
        scratch_shapes=[pltpu.VMEM((NS, DI), F32)],
        compiler_params=_params(("arbitrary",)),
    )(xact, xact, xact, dt, acum, acumT, dsk_rep, dy, sprev)


def _gnorm_fwd(y, proj, w):
    T = y.shape[0]
    R = _Rows(T, 1024)
    zb = OFF_Z // _GW

    def body(y_ref, z_ref, w_ref, o_ref):
        z = z_ref[...].astype(F32)
        yf = y_ref[...].astype(F32) * z * _sigmoid(z)
        r = lax.rsqrt(jnp.mean(yf * yf, axis=-1, keepdims=True) + EPS)
        o_ref[...] = (yf * r * w_ref[...]).astype(BF16)

    return R.call(body, "gnorm_fwd", NG, [R.tile(_GW), R.tile(_GW, zb), R.colvec(1, _GW)], R.tile(_GW),
                  jax.ShapeDtypeStruct((T, DI), BF16), (y, proj, w))


def _gnorm_bwd(dn, y, proj, w, dproj):
    T = y.shape[0]
    R = _Rows(T, 1024)
    zb = OFF_Z // _GW

    def body(dn_ref, y_ref, z_ref, w_ref, _alias, dz_ref, dy_ref, dw_ref):
        z = z_ref[...].astype(F32)
        yv = y_ref[...].astype(F32)
        s = _sigmoid(z)
        silu = z * s
        yf = yv * silu
        r = lax.rsqrt(jnp.mean(yf * yf, axis=-1, keepdims=True) + EPS)
        yh = yf * r
        dnv = dn_ref[...].astype(F32)
        dyh = dnv * w_ref[...]
        dyf = r * (dyh - yh * jnp.mean(dyh * yh, axis=-1, keepdims=True))
        dy_ref[...] = (dyf * silu).astype(BF16)
        dz_ref[...] = (dyf * yv * s * (1.0 + z * (1.0 - s))).astype(BF16)
        _acc_out(dw_ref, _colsum(dnv * yh), pl.program_id(1) == 0)

    return R.call(
        body, "gnorm_bwd", NG, [R.tile(_GW), R.tile(_GW), R.tile(_GW, zb), R.colvec(1, _GW), ANY],
        [R.tile(_GW, zb), R.tile(_GW), R.colvec(1, _GW)],
        [jax.ShapeDtypeStruct(dproj.shape, BF16), jax.ShapeDtypeStruct((T, DI), BF16),
         jax.ShapeDtypeStruct((1, DI), F32)],
        (dn, y, proj, w, dproj), aliases={4: 0})


def _merge_fwd(proj, ya, ys):
    T = proj.shape[0]
    R = _Rows(T, 512)
    gb = OFF_G // (2 * D)

    def body(g_ref, ya_ref, ys_ref, o_ref):
        ga = _sigmoid(g_ref[:, :D].astype(F32))
        gs = _sigmoid(g_ref[:, D:].astype(F32))
        o_ref[...] = (ga * ya_ref[...].astype(F32) + gs * ys_ref[...].astype(F32)).astype(BF16)

    return R.call(body, "merge_fwd", 1, [R.tile(2 * D, gb), R.tile(D), R.tile(D)], R.tile(D),
                  jax.ShapeDtypeStruct((T, D), BF16), (proj, ya, ys))


def _merge_bwd(dm, proj, ya, ys, ncols):
    T = proj.shape[0]
    R = _Rows(T, 256)
    gb = OFF_G // (2 * D)

    def body(dm_ref, g_ref, ya_ref, ys_ref, dg_ref, dya_ref, dys_ref):
        d = dm_ref[...].astype(F32)
        ga = _sigmoid(g_ref[:, :D].astype(F32))
        gs = _sigmoid(g_ref[:, D:].astype(F32))
        dya_ref[...] = (d * ga).astype(BF16)
        dys_ref[...] = (d * gs).astype(BF16)
        dg_ref[:, :D] = (d * ya_ref[...].astype(F32) * ga * (1.0 - ga)).astype(BF16)
        dg_ref[:, D:] = (d * ys_ref[...].astype(F32) * gs * (1.0 - gs)).astype(BF16)

    return R.call(
        body, "merge_bwd", 1, [R.tile(D), R.tile(2 * D, gb), R.tile(D), R.tile(D)],
        [R.tile(2 * D, gb), R.tile(D), R.tile(D)],
        [jax.ShapeDtypeStruct((T, ncols), BF16), jax.ShapeDtypeStruct((T, D), BF16),
         jax.ShapeDtypeStruct((T, D), BF16)],
        (dm, proj, ya, ys))


_FW = 1408
_FB = FF // _FW


def _ffn_act_fwd(hv, conv_w, conv_b):
    T = hv.shape[0]
    R = _Rows(T, 256)
    tm = R.tm

    def body(h1_ref, h1p_ref, h3_ref, w_ref, b_ref, o_ref):
        keep = (pl.program_id(1) > 0).astype(F32)
        ext = jnp.concatenate([h1p_ref[...].astype(F32) * keep, h1_ref[...].astype(F32)], axis=0)
        pre = _wsum(w_ref[...], _shifts_causal(ext, 3, tm)) + b_ref[...]
        o_ref[...] = (pre * _sigmoid(pre) * h3_ref[...].astype(F32)).astype(BF16)

    return R.call(body, "ffn_act_fwd", _FB,
                  [R.tile(_FW), R.prev(_FW), R.tile(_FW, _FB), R.colvec(3, _FW), R.colvec(1, _FW)],
                  R.tile(_FW), jax.ShapeDtypeStruct((T, FF), BF16), (hv, hv, hv, conv_w, conv_b))


def _ffn_act_bwd(dg, hv, conv_w, conv_b):
    T = hv.shape[0]
    R = _Rows(T, 256)
    tm = R.tm

    def body(dg_ref, h1_ref, h1p_ref, h3_ref, w_ref, b_ref, dh3_ref, dpre_ref, dw_ref, db_ref):
        i = pl.program_id(1)
        keep = (i > 0).astype(F32)
        ext = jnp.concatenate([h1p_ref[...].astype(F32) * keep, h1_ref[...].astype(F32)], axis=0)
        sh = _shifts_causal(ext, 3, tm)
        pre = _wsum(w_ref[...], sh) + b_ref[...]
        s = _sigmoid(pre)
        d = dg_ref[...].astype(F32)
        dh3_ref[...] = (d * pre * s).astype(BF16)
        dpre = d * h3_ref[...].astype(F32) * s * (1.0 + pre * (1.0 - s))
        dpre_ref[...] = dpre.astype(BF16)
        _acc_rows(dw_ref, [_colsum(dpre * q) for q in sh], i == 0)
        _acc_out(db_ref, _colsum(dpre), i == 0)

    return R.call(
        body, "ffn_act_bwd", _FB,
        [R.tile(_FW), R.tile(_FW), R.prev(_FW), R.tile(_FW, _FB), R.colvec(3, _FW), R.colvec(1, _FW)],
        [R.tile(_FW), R.tile(_FW), R.colvec(3, _FW), R.colvec(1, _FW)],
        [jax.ShapeDtypeStruct((T, FF), BF16), jax.ShapeDtypeStruct((T, FF), BF16),
         jax.ShapeDtypeStruct((3, FF), F32), jax.ShapeDtypeStruct((1, FF), F32)],
        (dg, hv, hv, hv, conv_w, conv_b))


def _conv3_transpose(dpre, conv_w):
    T = dpre.shape[0]
    R = _Rows(T, 256)
    tm = R.tm

    def body(d_ref, dn_ref, w_ref, o_ref):
        keep = (pl.program_id(1) < R.nrow - 1).astype(F32)
        ext = jnp.concatenate([d_ref[...].astype(F32), dn_ref[...].astype(F32) * keep], axis=0)
        o_ref[...] = _wsum(w_ref[...], _shifts_anticausal(ext, 3, tm)).astype(BF16)

    return R.call(body, "ffn_conv_bwd", _FB, [R.tile(_FW), R.next(_FW), R.colvec(3, _FW)], R.tile(_FW),
                  jax.ShapeDtypeStruct((T, FF), BF16), (dpre, dpre, conv_w))


def _final_loss_epilogue(w, target):
    T = target.shape[0]

    def fn(xv, ins, outs, first):
        w_ref, t_ref = ins
        l_ref, dh_ref, dhb_ref, dw_ref = outs
        wv = w_ref[...]
        r = lax.rsqrt(jnp.mean(xv * xv, axis=-1, keepdims=True) + EPS)
        xh = xv * r
        err = xh * wv - t_ref[...]
        part = 0.5 * jnp.sum(jnp.mean(err * err, axis=-1, keepdims=True), axis=0, keepdims=True)
        _acc_out(l_ref, jnp.broadcast_to(part, l_ref.shape), first)
        dy = err * (1.0 / D)
        dxh = dy * wv
        dh = r * (dxh - xh * jnp.mean(dxh * xh, axis=-1, keepdims=True))
        dh_ref[...] = dh
        dhb_ref[...] = dh.astype(BF16)
        _acc_out(dw_ref, _colsum(dy * xh), first)

    return _Epilogue(fn, (w, target),
                     (((8, LANES), F32), ((T, D), F32), ((T, D), BF16), ((1, D), F32)), 10 * D)


def _pad_lanes(v, n=LANES):
    return jnp.pad(v, ((0, 0), (0, n - v.shape[1])))


class _Hooks:
    def before_in_proj(self, w_in):
        return w_in

    def late_weights(self, wts, after):
        return wts

    def grads_ready(self, grads, tie):
        return tie

    def mark(self, name, value):
        pass


def _local_step(x, target, wts, hooks=None):
    hooks = hooks or _Hooks()
    T = x.shape[0]
    w_in = wts["w_in"]
    dt_bias_p, a_log_p = _pad_lanes(wts["dt_bias"]), _pad_lanes(wts["a_log"])
    dsk_rep = jnp.repeat(wts["d_skip"], HP, axis=1)

    w_in = hooks.before_in_proj(w_in)
    proj, u = _norm_matmul(x, wts["norm_mix_w"], w_in, "norm_mm_in")
    dt_raw = _matmul(u, w_in[:, OFF_DT:OFF_DT + LANES], mode="nn", out_dtype=F32, name="mm_dt")
    ya_in = _branch_a_fwd(proj, wts["conv_a_w"])
    xact = _xbc_fwd(proj, wts["ssd_conv_w"], wts["ssd_conv_b"])
    dt, acum, acumT = _dt_fwd(dt_raw, dt_bias_p, a_log_p)
    y_ssd, sprev = _ssd_fwd(xact, dt, acum, acumT, dsk_rep)
    yn = _gnorm_fwd(y_ssd, proj, wts["ssd_norm_w"])
    late = hooks.late_weights(wts, yn)
    w_a_out, w_s_out, w_o, w_up, w_down = (late[k] for k in ("w_a_out", "w_s_out", "w_o", "w_up", "w_down"))
    y_a = _matmul(ya_in, w_a_out, mode="nn", out_dtype=BF16, name="mm_a_out")
    y_s = _matmul(yn, w_s_out, mode="nn", out_dtype=BF16, name="mm_s_out")
    merged = _merge_fwd(proj, y_a, y_s)
    h1 = _matmul(merged, w_o, mode="nn", out_dtype=F32, name="mm_o", residual=x)
    hv, v = _norm_matmul(h1, wts["norm_ffn_w"], w_up, "norm_mm_up")
    gact = _ffn_act_fwd(hv, wts["ffn_conv_w"], wts["ffn_conv_b"])
    loss, dh2, dh2b, g_final = _matmul(gact, w_down, mode="nn", out_dtype=F32, name="mm_down_loss", residual=h1,
                                       epilogue=_final_loss_epilogue(wts["final_norm_w"], target))

    grads = {"final_norm_w": g_final}
    grads["w_down"] = _matmul(gact, dh2b, mode="tn", out_dtype=F32, name="mm_down_dw")
    dgact = _matmul(dh2b, w_down, mode="nt", out_dtype=BF16, name="mm_down_dx")
    dh3, dpre, grads["ffn_conv_w"], grads["ffn_conv_b"] = _ffn_act_bwd(dgact, hv, wts["ffn_conv_w"], wts["ffn_conv_b"])
    dh1c = _conv3_transpose(dpre, wts["ffn_conv_w"])
    grads["w_up"] = (_matmul(v, dh1c, mode="tn", out_dtype=F32, name="mm_up_dw1"),
                     _matmul(v, dh3, mode="tn", out_dtype=F32, name="mm_up_dw3"))
    dv = _matmul(dh1c, w_up, mode="nt", out_dtype=F32, name="mm_up_dx1")
    dh1, dh1b, grads["norm_ffn_w"] = _matmul(
        dh3, w_up, mode="nt", out_dtype=F32, name="mm_up_dx3_norm", residual=dv, b_k_off=FF,
        epilogue=_rmsnorm_bwd_epilogue(h1, wts["norm_ffn_w"], dh2))
    grads["w_o"] = _matmul(merged, dh1b, mode="tn", out_dtype=F32, name="mm_o_dw")
    dmerged = _matmul(dh1b, w_o, mode="nt", out_dtype=BF16, name="mm_o_dx")
    dproj, dya, dys = _merge_bwd(dmerged, proj, y_a, y_s, NIP)
    grads["w_a_out"] = _matmul(ya_in, dya, mode="tn", out_dtype=F32, name="mm_a_out_dw")
    dya_in = _matmul(dya, w_a_out, mode="nt", out_dtype=BF16, name="mm_a_out_dx")
    dproj, grads["conv_a_w"] = _branch_a_bwd(dya_in, proj, wts["conv_a_w"], dproj)
    grads["w_s_out"] = _matmul(yn, dys, mode="tn", out_dtype=F32, name="mm_s_out_dw")
    dys = hooks.grads_ready({k: grads[k] for k in ("w_a_out", "w_s_out", "w_o", "w_up", "w_down")}, dys)
    dyn =_matmul(dys, w_s_out, mode="nt", out_dtype=BF16, name="mm_s_out_dx")
    dproj, dy_ssd, grads["ssd_norm_w"] = _gnorm_bwd(dyn, y_ssd, proj, wts["ssd_norm_w"], dproj)
    dxact, ddt_x, dacum, dskl = _ssd_bwd(dy_ssd, xact, dt, acum, acumT, dsk_rep, sprev)
    hooks.mark("ssd_bwd", dxact)
    grads["d_skip"] = dskl.reshape(NH, HP).sum(axis=1).reshape(1, NH)
    dproj, grads["ssd_conv_w"], grads["ssd_conv_b"] = _xbc_bwd(dxact, proj, wts["ssd_conv_w"], wts["ssd_conv_b"], dproj)
    dproj, g_dtb, g_alog = _dt_bwd(dacum, ddt_x, dt_raw, dt_bias_p, a_log_p, dproj)
    grads["dt_bias"], grads["a_log"] = g_dtb[:, :NH], g_alog[:, :NH]
    grads["w_in"] = _matmul(u, dproj, mode="tn", out_dtype=F32, name="mm_in_dw")
    dproj = hooks.grads_ready({"w_in": grads["w_in"]}, dproj)
    grad_x, _, grads["norm_mix_w"] = _matmul(dproj, w_in, mode="nt", out_dtype=F32, name="mm_in_dx_norm",
                                             epilogue=_rmsnorm_bwd_epilogue(x, wts["norm_mix_w"], dh1))
    return loss, grad_x, grads


def _permute_w_in(w):
    out = jnp.zeros((w.shape[0], NIP), w.dtype)
    for o, n, no in _SEGS:
        out = lax.dynamic_update_slice(out, w[:, o:o + n], (0, no))
    return out


def _unpermute_w_in(g):
    order = sorted(_SEGS)
    return jnp.concatenate([g[:, no:no + n] for o, n, no in order], axis=1)


MESH = pl.DeviceIdType.MESH
NCHIP = 4
NDEV = 8

_W_IN = (("w_in", D, NI // NCHIP, 1),)
_W_REST = (("w_a_out", D // NCHIP, D, 0), ("w_s_out", DI // NCHIP, D, 0), ("w_o", D // NCHIP, D, 0),
           ("w_up", D, 2 * FF // NCHIP, 1), ("w_down", FF // NCHIP, D, 0))


def _slab_rows(group):
    rows = [r * c // LANES for _, r, c, _ in group]
    assert all(n % 32 == 0 for n in rows), rows
    return rows


def _coords():
    return lax.axis_index("x"), lax.axis_index("y"), lax.axis_index("c")


def _other_chips(x, y):
    return [(1 - x, y), (x, 1 - y), (1 - x, 1 - y)]


def _ag_weights(shard):
    nrows = shard.shape[0]
    hr = nrows // 2

    def body(x_ref, out_ref, send_sems, recv_sems, local_sem):
        x, y, c = _coords()
        me = 2 * x + y
        chips = _other_chips(x, y)

        def rows(s, h):
            return out_ref.at[s, pl.ds(h * hr, hr), :]

        def copy(k, s, h, to, src=None):
            return pltpu.make_async_remote_copy(
                src_ref=rows(s, h) if src is None else src, dst_ref=rows(s, h),
                send_sem=send_sems.at[k], recv_sem=recv_sems.at[k], device_id=to, device_id_type=MESH)

        mine = pltpu.make_async_copy(x_ref, out_ref.at[me], local_sem)
        mine.start()
        first = [copy(k, me, c, (*chip, c), src=x_ref.at[pl.ds(c * hr, hr), :]) for k, chip in enumerate(chips)]
        for cp in first:
            cp.start()
        passed = []
        for k, chip in enumerate(chips):
            s = 2 * chip[0] + chip[1]
            copy(k, s, c, (x, y, c)).wait_recv()
            fwd = copy(3 + k, s, c, (x, y, 1 - c))
            fwd.start()
            passed.append(fwd)
        for k, chip in enumerate(chips):
            copy(3 + k, 2 * chip[0] + chip[1], 1 - c, (x, y, c)).wait_recv()
        for cp in first + passed:
            cp.wait_send()
        mine.wait()

    return pl.pallas_call(
        body, name="ag_weights", in_specs=[ANY], out_specs=ANY,
        out_shape=jax.ShapeDtypeStruct((NCHIP,) + shard.shape, shard.dtype),
        scratch_shapes=[pltpu.SemaphoreType.DMA((6,)), pltpu.SemaphoreType.DMA((6,)), pltpu.SemaphoreType.DMA],
        compiler_params=pltpu.CompilerParams(has_side_effects=True),
    )(shard)


HBM = pl.BlockSpec(memory_space=pltpu.HBM)
SEM = pl.BlockSpec(memory_space=pltpu.SEMAPHORE)
_EFFECT = pltpu.SideEffectType.DATAFLOW_SIDE_EFFECTING
_NCOPY = NCHIP - 1


def _plan_bcast(src_ref, land_ref, send_sems, recv_sems):
    x, y, c = _coords()
    sends, lands = [], []
    for k, chip in enumerate(_other_chips(x, y)):
        def copy(slot):
            return pltpu.make_async_remote_copy(
                src_ref=src_ref, dst_ref=land_ref.at[slot], send_sem=send_sems.at[k], recv_sem=recv_sems.at[k],
                device_id=(*chip, c), device_id_type=MESH)
        sends.append(copy(2 * x + y))
        lands.append(copy(2 * chip[0] + chip[1]))
    return sends, lands


def _plan_scatter(src_ref, land_ref, send_sems, recv_sems):
    x, y, c = _coords()
    cps = [pltpu.make_async_remote_copy(
        src_ref=src_ref.at[2 * chip[0] + chip[1]], dst_ref=land_ref.at[k], send_sem=send_sems.at[k],
        recv_sem=recv_sems.at[k], device_id=(*chip, c), device_id_type=MESH)
        for k, chip in enumerate(_other_chips(x, y))]
    return cps, cps


def _split_start(name, src, land, plan):
    def body(src_ref, land_ref, send_sems, recv_sems, src_thru, land_thru, token):
        for cp in plan(src_ref, land_ref, send_sems, recv_sems)[0]:
            cp.start()
        token[...] = jnp.zeros_like(token)

    send_sems, recv_sems, src_thru, land_thru, token = pl.pallas_call(
        body, name=name,
        out_shape=(pltpu.SemaphoreType.DMA((_NCOPY,)), pltpu.SemaphoreType.DMA((_NCOPY,)),
                   pltpu.HBM(src.shape, src.dtype), pltpu.HBM(land.shape, land.dtype),
                   jax.ShapeDtypeStruct((8, LANES), F32)),
        in_specs=(HBM, HBM), out_specs=(SEM, SEM, HBM, HBM, pl.BlockSpec(memory_space=pltpu.VMEM)),
        input_output_aliases={0: 2, 1: 3},
        compiler_params=pltpu.CompilerParams(has_side_effects=_EFFECT),
    )(pltpu.with_memory_space_constraint(src, pltpu.HBM), pltpu.with_memory_space_constraint(land, pltpu.HBM))
    return (send_sems, recv_sems, src_thru, land_thru), token


def _split_wait(name, handle, after, plan):
    send_sems, recv_sems, src_thru, land_thru = handle

    def body(src_ref, land_ref, send_sems, recv_sems, after_ref, src_out, land_out):
        sends, lands = plan(src_ref, land_ref, send_sems, recv_sems)
        for cp in sends:
            cp.wait_send()
        for cp in lands:
            cp.wait_recv()

    return pl.pallas_call(
        body, name=name,
        out_shape=(pltpu.HBM(src_thru.shape, src_thru.dtype), pltpu.HBM(land_thru.shape, land_thru.dtype)),
        in_specs=(HBM, HBM, SEM, SEM, ANY), out_specs=(HBM, HBM), input_output_aliases={0: 0, 1: 1},
        compiler_params=pltpu.CompilerParams(has_side_effects=_EFFECT),
    )(src_thru, land_thru, send_sems, recv_sems, after)


def _tie(x, token, name):
    def body(x_ref, t_ref, o_ref):
        pass

    return pl.pallas_call(
        body, name=name, in_specs=[ANY, pl.BlockSpec(memory_space=pltpu.VMEM)], out_specs=ANY,
        out_shape=jax.ShapeDtypeStruct(x.shape, x.dtype), input_output_aliases={0: 0},
    )(x, token)


def _swap_sibling(p, name):
    def body(p_ref, land_ref, send_sem, recv_sem):
        x, y, c = _coords()
        cp = pltpu.make_async_remote_copy(
            src_ref=p_ref, dst_ref=land_ref, send_sem=send_sem, recv_sem=recv_sem,
            device_id=(x, y, 1 - c), device_id_type=MESH)
        cp.start()
        cp.wait()

    return pl.pallas_call(
        body, name=name, in_specs=[ANY], out_specs=ANY, out_shape=jax.ShapeDtypeStruct(p.shape, p.dtype),
        scratch_shapes=[pltpu.SemaphoreType.DMA, pltpu.SemaphoreType.DMA],
        compiler_params=pltpu.CompilerParams(has_side_effects=True),
    )(p)


_ADD_BYTES = 7 << 19


def _add_tile(rows, cols):
    best = 32
    for t in range(32, rows + 1, 32):
        if rows % t == 0 and t * cols * 4 <= _ADD_BYTES:
            best = t
    return best


def _add_slabs(pack, land, me, name):
    rows, cols = pack.shape[1:]
    tr = _add_tile(rows, cols)

    def body(me_ref, p_ref, l_ref, o_ref):
        f = lambda r: r.astype(F32)
        o_ref[...] = ((f(p_ref[0]) + f(l_ref[0])) + f(l_ref[1])) + f(l_ref[2])

    return pl.pallas_call(
        body, name=name,
        grid_spec=pltpu.PrefetchScalarGridSpec(
            num_scalar_prefetch=1, grid=(rows // tr,),
            in_specs=[pl.BlockSpec((1, tr, cols), lambda i, me_ref: (me_ref[0], i, 0)),
                      pl.BlockSpec((_NCOPY, tr, cols), lambda i, me_ref: (0, i, 0))],
            out_specs=pl.BlockSpec((tr, cols), lambda i, me_ref: (i, 0))),
        out_shape=jax.ShapeDtypeStruct((rows, cols), F32),
        compiler_params=_params(("parallel",)),
    )(me, pack, land)


def _add_pair(a, b, name):
    rows, cols = a.shape
    tr = _add_tile(rows, cols)

    def body(a_ref, b_ref, o_ref):
        o_ref[...] = a_ref[...] + b_ref[...]

    blk = pl.BlockSpec((tr, cols), lambda i: (i, 0))
    return pl.pallas_call(
        body, name=name, grid=(rows // tr,), in_specs=[blk, blk], out_specs=blk,
        out_shape=jax.ShapeDtypeStruct((rows, cols), F32), compiler_params=_params(("parallel",)),
    )(a, b)


_STAGE_W = 1024


def _stage_rows(shapes):
    pieces, r = [], 0
    for i, (k, w) in enumerate(shapes):
        for a in range(k):
            for q in range(0, w, _STAGE_W):
                pieces.append((i, a, q, min(_STAGE_W, w - q), r))
                r += 1
    return pieces, -(-r // 8) * 8


def _gather8(parts, reduce, name):
    shapes = [p.shape for p in parts]
    pieces, rows = _stage_rows(shapes)
    n = len(parts)

    def body(*refs):
        ins, outs = refs[:n], refs[n:2 * n]
        stage, buf, res, send_sems, recv_sems = refs[2 * n:]
        x, y, c = _coords()
        me = 4 * x + 2 * y + c
        stage[...] = jnp.zeros_like(stage)
        for i, a, q, w, r in pieces:
            stage[r:r + 1, 0:w] = ins[i][a:a + 1, q:q + w]
        buf[pl.ds(me, 1)] = stage[...][None]
        cps, lands = [], []
        for k in range(1, NDEV):
            peer = (1 - x if k & 4 else x, 1 - y if k & 2 else y, 1 - c if k & 1 else c)

            def copy(slot):
                return pltpu.make_async_remote_copy(
                    src_ref=stage, dst_ref=buf.at[slot], send_sem=send_sems.at[k - 1],
                    recv_sem=recv_sems.at[k - 1], device_id=peer, device_id_type=MESH)

            cps.append(copy(me))
            lands.append(copy(4 * peer[0] + 2 * peer[1] + peer[2]))
        for cp in cps:
            cp.start()
        for cp, land in zip(cps, lands):
            land.wait_recv()
            cp.wait_send()
        if reduce:
            acc = buf[0]
            for d in range(1, NDEV):
                acc = acc + buf[d]
            res[...] = acc
            for i, a, q, w, r in pieces:
                outs[i][a:a + 1, q:q + w] = res[r:r + 1, 0:w]
        else:
            for i, a, q, w, r in pieces:
                for s in range(NCHIP):
                    outs[i][s, a:a + 1, q:q + w] = buf[2 * s, r:r + 1, 0:w]

    vm = pl.BlockSpec(memory_space=pltpu.VMEM)
    out_shapes = [jax.ShapeDtypeStruct(s if reduce else (NCHIP,) + s, F32) for s in shapes]
    return pl.pallas_call(
        body, name=name, in_specs=[vm] * n, out_specs=[vm] * n, out_shape=out_shapes,
        scratch_shapes=[pltpu.VMEM((rows, _STAGE_W), F32), pltpu.VMEM((NDEV, rows, _STAGE_W), F32),
                        pltpu.VMEM((rows, _STAGE_W), F32), pltpu.SemaphoreType.DMA((NDEV - 1,)),
                        pltpu.SemaphoreType.DMA((NDEV - 1,))],
        compiler_params=pltpu.CompilerParams(has_side_effects=True),
    )(*parts)


def _adamw_update(w_ref, g_ref, m_ref, v_ref, d_ref, mo_ref, vo_ref):
    c1 = 1.0 / (1.0 - ADAM_B1 ** ADAM_STEP)
    c2 = 1.0 / (1.0 - ADAM_B2 ** ADAM_STEP)
    gv = g_ref[...]
    mn = ADAM_B1 * m_ref[...] + (1.0 - ADAM_B1) * gv
    vn = ADAM_B2 * v_ref[...] + (1.0 - ADAM_B2) * (gv * gv)
    d_ref[...] = -ADAM_LR * ((mn * c1) / (jnp.sqrt(vn * c2) + ADAM_EPS) + ADAM_WD * w_ref[...])
    mo_ref[...] = mn
    vo_ref[...] = vn


def _adamw_small(ws, gs, ms, vs):
    n = len(ws)

    def body(*refs):
        for i in range(n):
            _adamw_update(*(refs[j * n + i] for j in range(7)))

    vm = pl.BlockSpec(memory_space=pltpu.VMEM)
    outs = pl.pallas_call(
        body, name="adamw_small", in_specs=[vm] * (4 * n), out_specs=[vm] * (3 * n),
        out_shape=[jax.ShapeDtypeStruct(w.shape, F32) for w in ws] * 3,
    )(*ws, *gs, *ms, *vs)
    return outs[:n], outs[n:2 * n], outs[2 * n:]


def _adamw(w, g, m, v, name):
    rows, cols = w.shape
    tr = rows
    while tr * cols * 4 > (3 << 19) and tr % 16 == 0:
        tr //= 2

    def body(*refs):
        _adamw_update(*refs)

    blk = pl.BlockSpec((tr, cols), lambda i: (i, 0))
    return pl.pallas_call(
        body, name=name, grid=(rows // tr,), in_specs=[blk] * 4, out_specs=[blk] * 3,
        out_shape=[jax.ShapeDtypeStruct((rows, cols), F32)] * 3, compiler_params=_params(("parallel",)),
    )(w, g, m, v)


def _rows128(a, mult=8):
    flat = a.reshape(-1)
    n = -(-flat.shape[0] // (LANES * mult)) * LANES * mult
    return jnp.pad(flat, (0, n - flat.shape[0])).reshape(-1, LANES)


def _pack_rows(parts, total_rows):
    rows = sum(p.shape[0] for p in parts)
    if total_rows > rows:
        parts = list(parts) + [jnp.zeros((total_rows - rows, LANES), parts[0].dtype)]
    return jnp.concatenate(parts, axis=0)


def _unpack_rows(pack, shapes, mult=8):
    out, r = [], 0
    for shp in shapes:
        n = int(np.prod(shp))
        nr = -(-n // (LANES * mult)) * mult
        out.append(pack[r:r + nr].reshape(-1)[:n].reshape(shp))
        r += nr
    return out


def _unpack_full(full, group):
    out, r = {}, 0
    for (name, rr, cc, axis), nr in zip(group, _slab_rows(group)):
        seg = full[:, r:r + nr].reshape(NCHIP, rr, cc)
        out[name] = seg.reshape(NCHIP * rr, cc) if axis == 0 else seg.transpose(1, 0, 2).reshape(rr, NCHIP * cc)
        r += nr
    return out


def _by_chip(g, rr, cc, axis):
    if isinstance(g, tuple):
        n = NCHIP // len(g)
        return jnp.concatenate([h.reshape(rr, n, cc).transpose(1, 0, 2) for h in g], axis=0)
    return g.reshape(NCHIP, rr, cc) if axis == 0 else g.reshape(rr, NCHIP, cc).transpose(1, 0, 2)


def _pack_by_chip(grads, group, dtype):
    parts = [_by_chip(jax.tree.map(lambda t: t.astype(dtype), grads[name]), rr, cc, axis).reshape(NCHIP, nr, LANES)
             for (name, rr, cc, axis), nr in zip(group, _slab_rows(group))]
    return jnp.concatenate(parts, axis=1)


_SMALL_REPL = ("norm_mix_w", "ssd_conv_b", "dt_bias", "a_log", "d_skip", "ssd_norm_w", "norm_ffn_w",
               "ffn_conv_b", "final_norm_w")
_SMALL_CONV = (("conv_a_w", 3, D), ("ssd_conv_w", 4, DX), ("ffn_conv_w", 3, FF))


def kernel(x, norm_mix_w, w_in, conv_a_w, w_a_out, ssd_conv_w, ssd_conv_b, dt_bias, a_log, d_skip, ssd_norm_w, w_s_out, w_o, norm_ffn_w, w_up, ffn_conv_w, ffn_conv_b, w_down, final_norm_w, loss_target, m_norm_mix_w, m_w_in, m_conv_a_w, m_w_a_out, m_ssd_conv_w, m_ssd_conv_b, m_dt_bias, m_a_log, m_d_skip, m_ssd_norm_w, m_w_s_out, m_w_o, m_norm_ffn_w, m_w_up, m_ffn_conv_w, m_ffn_conv_b, m_w_down, m_final_norm_w, v_norm_mix_w, v_w_in, v_conv_a_w, v_w_a_out, v_ssd_conv_w, v_ssd_conv_b, v_dt_bias, v_a_log, v_d_skip, v_ssd_norm_w, v_w_s_out, v_w_o, v_norm_ffn_w, v_w_up, v_ffn_conv_w, v_ffn_conv_b, v_w_down, v_final_norm_w):
    names = ("norm_mix_w", "w_in", "conv_a_w", "w_a_out", "ssd_conv_w", "ssd_conv_b", "dt_bias", "a_log", "d_skip",
             "ssd_norm_w", "w_s_out", "w_o", "norm_ffn_w", "w_up", "ffn_conv_w", "ffn_conv_b", "w_down", "final_norm_w")
    W = dict(zip(names, (norm_mix_w, w_in, conv_a_w, w_a_out, ssd_conv_w, ssd_conv_b, dt_bias, a_log, d_skip,
                         ssd_norm_w, w_s_out, w_o, norm_ffn_w, w_up, ffn_conv_w, ffn_conv_b, w_down, final_norm_w)))
    M = dict(zip(names, (m_norm_mix_w, m_w_in, m_conv_a_w, m_w_a_out, m_ssd_conv_w, m_ssd_conv_b, m_dt_bias, m_a_log,
                         m_d_skip, m_ssd_norm_w, m_w_s_out, m_w_o, m_norm_ffn_w, m_w_up, m_ffn_conv_w, m_ffn_conv_b,
                         m_w_down, m_final_norm_w)))
    V = dict(zip(names, (v_norm_mix_w, v_w_in, v_conv_a_w, v_w_a_out, v_ssd_conv_w, v_ssd_conv_b, v_dt_bias, v_a_log,
                         v_d_skip, v_ssd_norm_w, v_w_s_out, v_w_o, v_norm_ffn_w, v_w_up, v_ffn_conv_w, v_ffn_conv_b,
                         v_w_down, v_final_norm_w)))
    two_d = lambda a: a.reshape(-1, a.shape[-1])
    W2, M2, V2 = ({k: two_d(a) for k, a in t.items()} for t in (W, M, V))
    xi, yi, ci = _coords()
    me = 2 * xi + yi

    meidx = me.reshape(1).astype(jnp.int32)
    state = {}


    class Hooks(_Hooks):
        def before_in_proj(self, w_in):
            return _tie(w_in, state["rest_token"], "tie_ag_rest")

        def late_weights(self, wts, after):
            own, land = _split_wait("ag_rest_wait", state["rest"], after, _plan_bcast)
            full = _unpack_full(lax.dynamic_update_slice(land, own[None], (me, 0, 0)), _W_REST)
            return {**wts, **full}

        def grads_ready(self, grads, tie):
            if "w_in" in grads:
                key = "g_in"
                pack = _by_chip(_unpermute_w_in(grads["w_in"]).astype(BF16), *_W_IN[0][1:])
            else:
                key, pack = "g_rest", _pack_by_chip(grads, _W_REST, BF16)
            land = lax.empty((_NCOPY,) + pack.shape[1:], BF16)
            state[key], token = _split_start("rs_" + key + "_start", pack, land, _plan_scatter)
            return _tie(tie, token, "tie_" + key)

        def mark(self, name, value):
            state[name] = value

    def reduced(key, after):
        pack, land = _split_wait("rs_" + key + "_wait", state[key], after, _plan_scatter)
        mine = _add_slabs(pack, land, meidx, "rs_" + key + "_add_chips")
        return _add_pair(mine, _swap_sibling(mine, "rs_" + key + "_swap"), "rs_" + key + "_add_cores")

    w_in_full = _ag_weights(W2["w_in"].astype(BF16)).transpose(1, 0, 2).reshape(D, NI)
    wts = {k: W2[k] for k in _SMALL_REPL}
    conv_by_chip = _gather8([W2[n] for n, *_ in _SMALL_CONV], False, "ag_conv_weights")
    for (n, kk, width), stacked in zip(_SMALL_CONV, conv_by_chip):
        wts[n] = stacked.transpose(1, 0, 2).reshape(kk, width)
    rest_slab = _tie(_pack_rows([_rows128(W2[n], 16) for n, *_ in _W_REST], 0).astype(BF16), conv_by_chip[0],
                     "tie_ag_order")
    state["rest"], state["rest_token"] = _split_start(
        "ag_rest_start", rest_slab, lax.empty((NCHIP,) + rest_slab.shape, BF16), _plan_bcast)
    wts["w_in"] = _permute_w_in(w_in_full)

    loss8, grad_x, grads = _local_step(x[0], loss_target[0], wts, Hooks())

    gbig = dict(zip([n for n, *_ in _W_REST],
                    _unpack_rows(reduced("g_rest", state["ssd_bwd"]), [(rr, cc) for _, rr, cc, _ in _W_REST], 16)))
    gbig["w_in"] = reduced("g_in", grad_x)

    small_parts = [grads[n] for n in _SMALL_REPL] + [loss8[0:1]] + [grads[n] for n, *_ in _SMALL_CONV]
    small_g = _gather8(small_parts, True, "allreduce_small")
    gsm = dict(zip(_SMALL_REPL, small_g[:len(_SMALL_REPL)]))
    loss = small_g[len(_SMALL_REPL)][0, 0]
    for (n, kk, width), gfull in zip(_SMALL_CONV, small_g[len(_SMALL_REPL) + 1:]):
        cw = width // NCHIP
        gsm[n] = lax.dynamic_slice(gfull, (0, me * cw), (kk, cw))

    G, DW, NM, NV = {}, {}, {}, {}
    for n in [b[0] for b in _W_IN + _W_REST]:
        G[n] = gbig[n]
        DW[n], NM[n], NV[n] = _adamw(W2[n], G[n], M2[n], V2[n], "adamw_" + n)
    sm_names = list(_SMALL_REPL) + [n for n, *_ in _SMALL_CONV]
    outs = _adamw_small(*([t[n] for n in sm_names] for t in (W2, gsm, M2, V2)))
    for t, vals in zip((DW, NM, NV), outs):
        t.update(zip(sm_names, vals))
    G.update(gsm)

    def shaped(t):
        return [t[n].reshape(W[n].shape) for n in names]

    return (loss, grad_x.reshape(x.shape), *shaped(G), *shaped(DW), *shaped(NM), *shaped(NV))
```

```python
import functools

import jax
import jax.numpy as jnp
import numpy as np
from jax import lax
from jax.experimental import pallas as pl
from jax.experimental.pallas import tpu as pltpu

F32 = jnp.float32
BF16 = jnp.bfloat16

D = 1024
DI = 2048
NH = 32
HP = 64
NG = 4
NS = 128
CH = 128
DX = 3072
FF = 2816
NI = 10272
EPS = 1e-5

OFF_BCV, OFF_XBC, OFF_G, OFF_Z, OFF_DT = 0, 3072, 6144, 8192, 10240
NIP = 10752
_SEGS = ((0, 2048, OFF_G), (2048, 3072, OFF_BCV), (5120, 2048, OFF_Z), (7168, 3072, OFF_XBC), (10240, 32, OFF_DT))

LANES = 128
HALO = 16
V7X_VMEM_LIMIT = 56 * 2 ** 20

ADAM_LR, ADAM_B1, ADAM_B2, ADAM_EPS, ADAM_WD, ADAM_STEP = 0.001, 0.9, 0.999, 1e-08, 0.01, 10

NN = (((1,), (0,)), ((), ()))
NT = (((1,), (1,)), ((), ()))
TN = (((0,), (0,)), ((), ()))


def _dot(a, b, dims=NN):
    return lax.dot_general(a, b, dims, preferred_element_type=F32)


def _params(sem, **kw):
    return pltpu.CompilerParams(dimension_semantics=sem, vmem_limit_bytes=V7X_VMEM_LIMIT, **kw)


V7X_MXU = 256
V7X_HBM_BYTES_PER_S = 3.5e12
STEP_S = 0.35e-6
MATMUL_VMEM = 40 * 2 ** 20
EPILOGUE_VMEM = 46 * 2 ** 20


ACC_BYTES_PER_S = 4e12


def _divisors(dim, cap, units):
    for unit in units:
        c = [t for t in range(unit, min(dim, cap) + 1, unit) if dim % t == 0]
        if c:
            return c
    return [dim]


def _tiles(M, N, K, out_bytes, has_res):
    best = None
    for tn in _divisors(N, 2816, (V7X_MXU, LANES)):
        for tm in _divisors(M, 2816, (LANES,)):
            for tk in _divisors(K, 2816, (V7X_MXU, LANES)):
                nk, ni, nj = K // tk, M // tm, N // tn
                vmem = 4 * (tm * tk + tk * tn) + 2 * tm * tn * out_bytes
                vmem += (4 * tm * tn if nk > 1 else 0) + (8 * tm * tn if has_res else 0)
                if vmem > MATMUL_VMEM:
                    continue
                a_reads = M * K * 2 * (nj if nk > 1 else 1)
                b_reads = K * N * 2 * (ni if nk * nj > 1 else 1)
                cost = (a_reads + b_reads + M * N * out_bytes) / V7X_HBM_BYTES_PER_S + ni * nj * nk * STEP_S
                cost += (nk - 1) * M * N * 8 / ACC_BYTES_PER_S
                if best is None or cost < best[0]:
                    best = (cost, tm, tn, tk)
    assert best is not None, (M, N, K)
    return best[1:]


def _sigmoid(x):
    return 1.0 / (1.0 + jnp.exp(-x))


class _Epilogue:
    def __init__(self, fn, ins, outs, tile_bytes):
        self.fn, self.ins, self.outs, self.tile_bytes = fn, tuple(ins), tuple(outs), tile_bytes


def _matmul(a, b, *, mode, out_dtype, name, residual=None, b_k_off=0, epilogue=None):
    if mode == "nn":
        (M, K), (K2, N) = a.shape, b.shape
    elif mode == "nt":
        (M, K), (N, K2) = a.shape, (b.shape[0], a.shape[1])
        assert b_k_off + K <= b.shape[1]
    else:
        (K, M), (K2, N) = a.shape, b.shape
    assert K == K2, (name, a.shape, b.shape)
    tm, tn, tk = _tiles(M, N, K, jnp.dtype(out_dtype).itemsize, residual is not None)
    if epilogue is not None:
        tn = N
        fits = [(t, q) for t in (1024, 512, 256) if M % t == 0
                for q in sorted(_divisors(K, 2816, (V7X_MXU, LANES)), reverse=True)
                if 4 * (t * q + q * tn) + (4 * t * tn if K > q else 0) + (8 * t * tn if residual is not None else 0)
                + 2 * t * epilogue.tile_bytes <= EPILOGUE_VMEM]
        tm, tk = fits[0]
    nk = K // tk
    if mode == "tn":
        a_spec = pl.BlockSpec((tk, tm), lambda i, j, k: (k, i))
    else:
        a_spec = pl.BlockSpec((tm, tk), lambda i, j, k: (i, k))
    if mode == "nt":
        assert b_k_off % tk == 0
        b_spec = pl.BlockSpec((tn, tk), lambda i, j, k: (j, k + b_k_off // tk))
    else:
        b_spec = pl.BlockSpec((tk, tn), lambda i, j, k: (k, j))
    dims = {"nn": NN, "nt": NT, "tn": TN}[mode]
    o_spec = pl.BlockSpec((tm, tn), lambda i, j, k: (i, j))
    has_res = residual is not None

    def rows_or_whole(shape):
        if shape[0] == M:
            return pl.BlockSpec((tm,) + tuple(shape[1:]), lambda i, j, k: (i,) + (0,) * (len(shape) - 1))
        return pl.BlockSpec(tuple(shape), lambda i, j, k: (0,) * len(shape))

    n_in = 2 + has_res + (len(epilogue.ins) if epilogue else 0)
    n_out = len(epilogue.outs) if epilogue else 1

    def body(*refs):
        a_ref, b_ref = refs[:2]
        r_ref = refs[2] if has_res else None
        out_refs = refs[n_in:n_in + n_out]
        acc_ref = refs[-1]
        k = pl.program_id(2)
        part = _dot(a_ref[...], b_ref[...], dims)

        def finish(r):
            if has_res:
                r = r + r_ref[...].astype(F32)
            if epilogue is None:
                out_refs[0][...] = r.astype(out_dtype)
            else:
                epilogue.fn(r, refs[2 + has_res:n_in], out_refs, pl.program_id(0) == 0)

        if nk == 1:
            finish(part)
            return

        @pl.when(k == 0)
        def _():
            acc_ref[...] = part

        @pl.when(jnp.logical_and(k > 0, k < nk - 1))
        def _():
            acc_ref[...] += part

        @pl.when(k == nk - 1)
        def _():
            finish(acc_ref[...] + part)

    in_specs = [a_spec, b_spec] + ([o_spec] if has_res else [])
    args = (a, b) + ((residual,) if has_res else ())
    if epilogue is None:
        out_specs, out_shape = o_spec, jax.ShapeDtypeStruct((M, N), out_dtype)
        sem = ("parallel", "parallel", "arbitrary")
    else:
        in_specs += [rows_or_whole(x.shape) for x in epilogue.ins]
        args += epilogue.ins
        out_specs = [rows_or_whole(shp) for shp, _ in epilogue.outs]
        out_shape = [jax.ShapeDtypeStruct(shp, dt) for shp, dt in epilogue.outs]
        sem = ("arbitrary", "arbitrary", "arbitrary")
    return pl.pallas_call(
        body, name=name, grid=(M // tm, N // tn, nk), in_specs=in_specs, out_specs=out_specs,
        out_shape=out_shape, scratch_shapes=[pltpu.VMEM((tm, tn), F32)] if nk > 1 else [],
        compiler_params=_params(sem),
    )(*args)


class _Rows:
    def __init__(self, T, tm):
        self.T, self.tm = T, min(tm, T // 2)
        self.nrow = T // self.tm
        self.r = self.tm // HALO
        self.nb = T // HALO

    def tile(self, w, cb=0, step=1):
        return pl.BlockSpec((self.tm, w), lambda j, i: (i, cb + step * j))

    def prev(self, w, cb=0, step=1):
        r = self.r
        return pl.BlockSpec((HALO, w), lambda j, i: (jnp.maximum(i * r - 1, 0), cb + step * j))

    def next(self, w, cb=0, step=1):
        r, nb = self.r, self.nb
        return pl.BlockSpec((HALO, w), lambda j, i: (jnp.minimum((i + 1) * r, nb - 1), cb + step * j))

    def colvec(self, k, w, cb=0, step=1):
        return pl.BlockSpec((k, w), lambda j, i: (0, cb + step * j))

    def call(self, body, name, ncol, in_specs, out_specs, out_shape, args, aliases=None):
        return pl.pallas_call(
            body, name=name, grid=(ncol, self.nrow), in_specs=in_specs, out_specs=out_specs,
            out_shape=out_shape, input_output_aliases=aliases or {},
            compiler_params=_params(("parallel", "arbitrary")),
        )(*args)


ANY = pl.BlockSpec(memory_space=pl.ANY)


def _shifts_causal(ext, nk, tm):
    out = []
    for k in range(nk):
        s = nk - 1 - k
        r = ext if s == 0 else pltpu.roll(ext, s, 0)
        out.append(r[HALO:])
    return out


def _shifts_anticausal(ext, nk, tm):
    n = ext.shape[0]
    out = []
    for k in range(nk):
        s = nk - 1 - k
        r = ext if s == 0 else pltpu.roll(ext, n - s, 0)
        out.append(r[:tm])
    return out


def _wsum(w, parts):
    acc = w[0:1, :] * parts[0]
    for k in range(1, len(parts)):
        acc = acc + w[k:k + 1, :] * parts[k]
    return acc


def _colsum(x):
    return jnp.sum(x, axis=0, keepdims=True)


def _acc_out(ref, val, first):
    @pl.when(first)
    def _():
        ref[...] = val

    @pl.when(jnp.logical_not(first))
    def _():
        ref[...] += val


def _acc_rows(ref, rows, first):
    for k, r in enumerate(rows):
        _acc_out(ref.at[k:k + 1, :], r, first)


def _norm_matmul(x, wn, b, name, b_f32=None):
    T, N = x.shape[0], b.shape[1]
    tm = min(1024, T)
    tn = max(t for t in _divisors(N, 2816, (V7X_MXU, LANES))
             if 8 * tm * D + 6 * tm * D + 4 * D * t + 4 * tm * t <= MATMUL_VMEM)

    extra = b_f32 is not None

    def body(*refs):
        x_ref, wn_ref, b_ref = refs[:3]
        o_ref, u_ref = refs[3 + extra:5 + extra]
        keep_ref = refs[-1]

        @pl.when(pl.program_id(1) == 0)
        def _():
            xv = x_ref[...]
            r = lax.rsqrt(jnp.mean(xv * xv, axis=-1, keepdims=True) + EPS)
            u = (xv * r * wn_ref[...]).astype(BF16)
            keep_ref[...] = u
            u_ref[...] = u
            if extra:
                refs[5 + extra][...] = _dot(u, refs[3][...])

        o_ref[...] = _dot(keep_ref[...], b_ref[...]).astype(BF16)

    rows = pl.BlockSpec((tm, D), lambda i, j: (i, 0))
    whole = lambda shape: pl.BlockSpec(shape, lambda i, j: (0, 0))
    narrow = pl.BlockSpec((tm, LANES), lambda i, j: (i, 0))
    return pl.pallas_call(
        body, name=name, grid=(T // tm, N // tn),
        in_specs=[rows, whole((1, D)), pl.BlockSpec((D, tn), lambda i, j: (0, j))] + [whole((D, LANES))] * extra,
        out_specs=[pl.BlockSpec((tm, tn), lambda i, j: (i, j)), rows] + [narrow] * extra,
        out_shape=[jax.ShapeDtypeStruct((T, N), BF16), jax.ShapeDtypeStruct((T, D), BF16)]
        + [jax.ShapeDtypeStruct((T, LANES), F32)] * extra,
        scratch_shapes=[pltpu.VMEM((tm, D), BF16)],
        compiler_params=_params(("parallel", "arbitrary")),
    )(*((x, wn, b) + ((b_f32,) if extra else ())))


def _rmsnorm_bwd_epilogue(x, w, dres):
    T = x.shape[0]

    def fn(dyv, ins, outs, first):
        x_ref, w_ref, dr_ref = ins
        dx_ref, dxb_ref, dw_ref = outs
        xv = x_ref[...]
        r = lax.rsqrt(jnp.mean(xv * xv, axis=-1, keepdims=True) + EPS)
        xh = xv * r
        dxh = dyv * w_ref[...]
        dx = r * (dxh - xh * jnp.mean(dxh * xh, axis=-1, keepdims=True)) + dr_ref[...]
        dx_ref[...] = dx
        dxb_ref[...] = dx.astype(BF16)
        _acc_out(dw_ref, _colsum(dyv * xh), first)

    return _Epilogue(fn, (x, w, dres), (((T, D), F32), ((T, D), BF16), ((1, D), F32)), 14 * D)


def _branch_a_fwd(proj, conv_w):
    T = proj.shape[0]
    R = _Rows(T, 512)
    tm = R.tm

    def body(p_ref, pp_ref, w_ref, o_ref):
        keep = (pl.program_id(1) > 0).astype(F32)
        cv = p_ref[:, D:2 * D].astype(F32) * p_ref[:, 2 * D:].astype(F32)
        cvp = pp_ref[:, D:2 * D].astype(F32) * pp_ref[:, 2 * D:].astype(F32) * keep
        sh = _shifts_causal(jnp.concatenate([cvp, cv], axis=0), 3, tm)
        ca = _wsum(w_ref[...], sh)
        o_ref[...] = (p_ref[:, :D].astype(F32) * ca).astype(BF16)

    return R.call(body, "branch_a_fwd", 1, [R.tile(3 * D), R.prev(3 * D), R.colvec(3, D)], R.tile(D),
                  jax.ShapeDtypeStruct((T, D), BF16), (proj, proj, conv_w))


def _branch_a_bwd(dya_in, proj, conv_w, dproj):
    T = proj.shape[0]
    R = _Rows(T, 256)
    tm = R.tm

    def body(d_ref, dn_ref, p_ref, pp_ref, pn_ref, w_ref, _alias, o_ref, dw_ref):
        i = pl.program_id(1)
        keep_p = (i > 0).astype(F32)
        keep_n = (i < R.nrow - 1).astype(F32)
        w = w_ref[...]
        b = p_ref[:, :D].astype(F32)
        c = p_ref[:, D:2 * D].astype(F32)
        v = p_ref[:, 2 * D:].astype(F32)
        cvp = pp_ref[:, D:2 * D].astype(F32) * pp_ref[:, 2 * D:].astype(F32) * keep_p
        sh = _shifts_causal(jnp.concatenate([cvp, c * v], axis=0), 3, tm)
        ca = _wsum(w, sh)
        d = d_ref[...].astype(F32)
        dca = d * b
        dca_n = dn_ref[...].astype(F32) * pn_ref[:, :D].astype(F32) * keep_n
        dsh = _shifts_anticausal(jnp.concatenate([dca, dca_n], axis=0), 3, tm)
        dcv = _wsum(w, dsh)
        o_ref[:, :D] = (d * ca).astype(BF16)
        o_ref[:, D:2 * D] = (dcv * v).astype(BF16)
        o_ref[:, 2 * D:] = (dcv * c).astype(BF16)
        _acc_rows(dw_ref, [_colsum(dca * s) for s in sh], i == 0)

    return R.call(
        body, "branch_a_bwd", 1,
        [R.tile(D), R.next(D), R.tile(3 * D), R.prev(3 * D), R.next(3 * D), R.colvec(3, D), ANY],
        [R.tile(3 * D), R.colvec(3, D)],
        [jax.ShapeDtypeStruct(dproj.shape, BF16), jax.ShapeDtypeStruct((3, D), F32)],
        (dya_in, dya_in, proj, proj, proj, conv_w, dproj), aliases={6: 0})


_XW = 512


def _xbc_fwd(proj, conv_w, conv_b):
    T = proj.shape[0]
    R = _Rows(T, 512)
    tm = R.tm
    cb = OFF_XBC // _XW

    def body(x_ref, xp_ref, w_ref, b_ref, o_ref):
        keep = (pl.program_id(1) > 0).astype(F32)
        ext = jnp.concatenate([xp_ref[...].astype(F32) * keep, x_ref[...].astype(F32)], axis=0)
        pre = _wsum(w_ref[...], _shifts_causal(ext, 4, tm)) + b_ref[...]
        o_ref[...] = (pre * _sigmoid(pre)).astype(BF16)

    return R.call(body, "xbc_fwd", DX // _XW,
                  [R.tile(_XW, cb), R.prev(_XW, cb), R.colvec(4, _XW), R.colvec(1, _XW)], R.tile(_XW),
                  jax.ShapeDtypeStruct((T, DX), BF16), (proj, proj, conv_w, conv_b))


def _xbc_bwd(dact, proj, conv_w, conv_b, dproj):
    T = proj.shape[0]
    R = _Rows(T, 512)
    tm = R.tm
    cb = OFF_XBC // _XW

    def body(d_ref, dn_ref, x_ref, xp_ref, xn_ref, w_ref, b_ref, _alias, o_ref, dw_ref, db_ref):
        i = pl.program_id(1)
        keep_p = (i > 0).astype(F32)
        keep_n = (i < R.nrow - 1).astype(F32)
        w = w_ref[...]
        ext = jnp.concatenate([xp_ref[...].astype(F32) * keep_p, x_ref[...].astype(F32),
                               xn_ref[...].astype(F32)], axis=0)
        sh = _shifts_causal(ext, 4, tm + HALO)
        pre = _wsum(w, sh) + b_ref[...]
        s = _sigmoid(pre)
        dsilu = s * (1.0 + pre * (1.0 - s))
        dext = jnp.concatenate([d_ref[...].astype(F32), dn_ref[...].astype(F32) * keep_n], axis=0)
        dpre = dext * dsilu
        dsh = _shifts_anticausal(dpre, 4, tm)
        o_ref[...] = _wsum(w, dsh).astype(BF16)
        dp = dpre[:tm]
        _acc_rows(dw_ref, [_colsum(dp * q[:tm]) for q in sh], i == 0)
        _acc_out(db_ref, _colsum(dp), i == 0)

    return R.call(
        body, "xbc_bwd", DX // _XW,
        [R.tile(_XW), R.next(_XW), R.tile(_XW, cb), R.prev(_XW, cb), R.next(_XW, cb),
         R.colvec(4, _XW), R.colvec(1, _XW), ANY],
        [R.tile(_XW, cb), R.colvec(4, _XW), R.colvec(1, _XW)],
        [jax.ShapeDtypeStruct(dproj.shape, BF16), jax.ShapeDtypeStruct((4, DX), F32),
         jax.ShapeDtypeStruct((1, DX), F32)],
        (dact, dact, proj, proj, proj, conv_w, conv_b, dproj), aliases={7: 0})


def _softplus(x):
    return jnp.maximum(x, 0.0) + jnp.log(1.0 + jnp.exp(-jnp.abs(x)))


def _dt_rows(T):
    return min(8 * CH, T // 2)


def _dt_fwd(dt_raw, dt_bias_p, a_log_p):
    T = dt_raw.shape[0]
    rows = _dt_rows(T)

    def body(r_ref, b_ref, al_ref, dt_ref, ac_ref, acT_ref):
        dt = _softplus(r_ref[...] + b_ref[...])
        s = dt * (-jnp.exp(al_ref[...]))
        row = lax.broadcasted_iota(jnp.int32, (rows, LANES), 0) % CH
        k = 1
        while k < CH:
            s = s + jnp.where(row >= k, pltpu.roll(s, k, 0), 0.0)
            k *= 2
        dt_ref[...] = dt
        ac_ref[...] = s
        for q in range(0, rows, CH):
            acT_ref[q:q + CH] = s[q:q + CH].T

    blk = pl.BlockSpec((rows, LANES), lambda i: (i, 0))
    vec = pl.BlockSpec((1, LANES), lambda i: (0, 0))
    return pl.pallas_call(
        body, name="dt_fwd", grid=(T // rows,), in_specs=[blk, vec, vec], out_specs=[blk, blk, blk],
        out_shape=[jax.ShapeDtypeStruct((T, LANES), F32)] * 3, compiler_params=_params(("parallel",)),
    )(dt_raw, dt_bias_p, a_log_p)


def _dt_bwd(dacum, ddt_x, dt_raw, dt_bias_p, a_log_p, dproj):
    T = dt_raw.shape[0]
    rows = _dt_rows(T)
    nc = T // rows

    def body(da_ref, dx_ref, r_ref, b_ref, al_ref, _alias, o_ref, db_ref, dal_ref):
        i = pl.program_id(0)
        a = -jnp.exp(al_ref[...])
        z = r_ref[...] + b_ref[...]
        dt = _softplus(z)
        s = da_ref[...]
        row = lax.broadcasted_iota(jnp.int32, (rows, LANES), 0) % CH
        k = 1
        while k < CH:
            s = s + jnp.where(row < CH - k, pltpu.roll(s, rows - k, 0), 0.0)
            k *= 2
        ddt = s * a + dx_ref[...]
        draw = ddt * _sigmoid(z)
        o_ref[:, :LANES] = draw.astype(BF16)
        o_ref[:, LANES:] = jnp.zeros((rows, NIP - OFF_DT - LANES), BF16)
        _acc_out(db_ref, _colsum(draw), i == 0)
        _acc_out(dal_ref, _colsum(s * dt), i == 0)

        @pl.when(i == nc - 1)
        def _():
            dal_ref[...] = dal_ref[...] * a

    blk = pl.BlockSpec((rows, LANES), lambda i: (i, 0))
    vec = pl.BlockSpec((1, LANES), lambda i: (0, 0))
    oblk = pl.BlockSpec((rows, NIP - OFF_DT), lambda i: (i, OFF_DT // (NIP - OFF_DT)))
    return pl.pallas_call(
        body, name="dt_bwd", grid=(nc,), in_specs=[blk, blk, blk, vec, vec, ANY], out_specs=[oblk, vec, vec],
        out_shape=[jax.ShapeDtypeStruct(dproj.shape, BF16), jax.ShapeDtypeStruct((1, LANES), F32),
                   jax.ShapeDtypeStruct((1, LANES), F32)],
        input_output_aliases={5: 0}, compiler_params=_params(("arbitrary",)),
    )(dacum, ddt_x, dt_raw, dt_bias_p, a_log_p, dproj)


_GW = DI // NG
_HG = NH // NG
_NEG = -1e30


def _interleave(gens):
    out, live = [None] * len(gens), list(range(len(gens)))
    while live:
        for i in list(live):
            try:
                next(gens[i])
            except StopIteration as stop:
                out[i] = stop.value
                live.remove(i)
    return out


def _pair_lanes(left, v0, v1):
    return jnp.where(left, v0, v1)


def _ssd_specs(T, rev):
    nc = T // CH
    cm = (lambda c: nc - 1 - c) if rev else (lambda c: c)
    bw = NG * NS
    return dict(
        xs=pl.BlockSpec((CH, DI), lambda c: (cm(c), 0)),
        bm=pl.BlockSpec((CH, bw), lambda c: (cm(c), DI // bw)),
        cmat=pl.BlockSpec((CH, bw), lambda c: (cm(c), DI // bw + 1)),
        xbc=pl.BlockSpec((CH, DX), lambda c: (cm(c), 0)),
        col=pl.BlockSpec((CH, LANES), lambda c: (cm(c), 0)),
        dsk=pl.BlockSpec((1, DI), lambda c: (0, 0)),
        state=pl.BlockSpec((1, NS, DI), lambda c: (cm(c), 0, 0)),
    )


def _last(ref, lo, hi):
    return ref.at[(slice(None),) * (len(ref.shape) - 1) + (slice(lo, hi),)]


def _group_views(g, wide, narrow):
    return [_last(r, g * _GW, (g + 1) * _GW) for r in wide] + [_last(r, g * NS, (g + 1) * NS) for r in narrow]


def _ssd_fwd(xact, dt, acum, acumT, dsk_rep):
    T = xact.shape[0]
    nc = T // CH
    sp = _ssd_specs(T, False)

    def body(*refs):
        xs, bm, cmat, dtr, acr, actr, dsk, y, spv, S_ref = refs

        @pl.when(pl.program_id(0) == 0)
        def _():
            S_ref[...] = jnp.zeros_like(S_ref)

        _interleave([group(g * _HG, dtr[...], acr[...], actr[...],
                           *_group_views(g, (xs, dsk, y, spv, S_ref), (bm, cmat))) for g in range(NG)])

    def group(hb, dt, ac, acT, xs_ref, dsk_ref, y_ref, sp_ref, S_ref, b_ref, c_ref):
        Bm, Cm = b_ref[...], c_ref[...]
        S = S_ref[...]
        sp_ref[0] = S
        cb = _dot(Cm, Bm, NT)
        CS = _dot(Cm, S.astype(BF16))
        row = lax.broadcasted_iota(jnp.int32, (CH, CH), 0)
        col = lax.broadcasted_iota(jnp.int32, (CH, CH), 1)
        tril = row >= col
        left = col < HP
        xd_parts, dec_parts = [], []
        for p in range(_HG // 2):
            sl = slice(p * LANES, (p + 1) * LANES)
            j0, j1 = hb + 2 * p, hb + 2 * p + 1
            xp = xs_ref[:, sl].astype(F32)
            a0, a1 = ac[:, j0:j0 + 1], ac[:, j1:j1 + 1]
            al0, al1 = ac[CH - 1:CH, j0:j0 + 1], ac[CH - 1:CH, j1:j1 + 1]
            X = xp * _pair_lanes(left, dt[:, j0:j0 + 1], dt[:, j1:j1 + 1])
            Xb = X.astype(BF16)
            Ws = [(cb * jnp.exp(jnp.where(tril, aj - acT[j:j + 1, :], _NEG))).astype(BF16)
                  for j, aj in ((j0, a0), (j1, a1))]
            Xs = [jnp.where(m, Xb, jnp.zeros_like(Xb)) for m in (left, jnp.logical_not(left))]
            yield
            yd = _dot(jnp.concatenate(Ws, axis=1), jnp.concatenate(Xs, axis=0))
            yield
            eal = _pair_lanes(left, jnp.exp(a0), jnp.exp(a1))
            y = yd + eal * CS[:, sl] + dsk_ref[:, sl] * xp
            y_ref[:, sl] = y.astype(BF16)
            xd_parts.append(X * _pair_lanes(left, jnp.exp(al0 - a0), jnp.exp(al1 - a1)))
            dec_parts.append(_pair_lanes(left[0:1], jnp.exp(al0), jnp.exp(al1)))
        Xd = jnp.concatenate(xd_parts, axis=1).astype(BF16)
        dec = jnp.concatenate(dec_parts, axis=1)
        S_ref[...] = dec * S + _dot(Bm, Xd, TN)

    return pl.pallas_call(
        body, name="ssd_fwd", grid=(nc,),
        in_specs=[sp["xs"], sp["bm"], sp["cmat"], sp["col"], sp["col"], sp["col"], sp["dsk"]],
        out_specs=[sp["xs"], sp["state"]],
        out_shape=[jax.ShapeDtypeStruct((T, DI), BF16), jax.ShapeDtypeStruct((nc, NS, DI), F32)],
        scratch_shapes=[pltpu.VMEM((NS, DI), F32)],
        compiler_params=_params(("arbitrary",)),
    )(xact, xact, xact, dt, acum, acumT, dsk_rep)


def _ssd_bwd(dy, xact, dt, acum, acumT, dsk_rep, sprev):
    T = xact.shape[0]
    nc = T // CH
    sp = _ssd_specs(T, True)

    def body(*refs):
        xs, bm, cmat, dtr, acr, actr, dsk, dyr, spv, dxa, ddtx, dAc, dskacc, dS_ref = refs
        first = pl.program_id(0) == 0

        @pl.when(first)
        def _():
            dS_ref[...] = jnp.zeros_like(dS_ref)

        dbc = _last(dxa, DI, DX)
        ddtx_sum = jnp.zeros((CH, LANES), F32)
        dAc_sum = jnp.zeros((CH, LANES), F32)
        for a, b in _interleave([group(first, g * _HG, dtr[...], acr[...], actr[...],
                                       *_group_views(g, (xs, dsk, dyr, spv, dxa, dskacc, dS_ref),
                                                     (bm, cmat, dbc, _last(dbc, NG * NS, 2 * NG * NS))))
                                 for g in range(NG)]):
            ddtx_sum, dAc_sum = ddtx_sum + a, dAc_sum + b
        ddtx[...] = ddtx_sum
        dAc[...] = dAc_sum

    def group(first, hb, dt, ac, acT, xs_ref, dsk_ref, dy_ref, sp_ref, dx_ref, dskacc_ref, dS_ref, b_ref, c_ref,
              dB_ref, dC_ref):
        Bm, Cm = b_ref[...], c_ref[...]
        S = sp_ref[0]
        dS = dS_ref[...]
        Sb, dSb = S.astype(BF16), dS.astype(BF16)
        cb = _dot(Cm, Bm, NT)
        cbT = _dot(Bm, Cm, NT)
        CmT = Cm.T
        CS = _dot(Cm, Sb)
        T1 = _dot(Bm, dSb)
        yield
        row = lax.broadcasted_iota(jnp.int32, (CH, CH), 0)
        col = lax.broadcasted_iota(jnp.int32, (CH, CH), 1)
        tril = row >= col
        triu = row <= col
        left = col < HP
        lane8 = lax.broadcasted_iota(jnp.int32, (1, LANES), 1)
        lastrow = lax.broadcasted_iota(jnp.int32, (CH, 1), 0) == CH - 1
        dCB = jnp.zeros((CH, CH), F32)
        dCBT = jnp.zeros((CH, CH), F32)
        dAc = jnp.zeros((CH, LANES), F32)
        ddtx = jnp.zeros((CH, LANES), F32)
        xd_parts, dye_parts, dec_parts, dsk_parts = [], [], [], []
        for p in range(_HG // 2):
            sl = slice(p * LANES, (p + 1) * LANES)
            j0, j1 = hb + 2 * p, hb + 2 * p + 1
            xp = xs_ref[:, sl].astype(F32)
            dyp = dy_ref[:, sl].astype(F32)
            a0, a1 = ac[:, j0:j0 + 1], ac[:, j1:j1 + 1]
            al0, al1 = ac[CH - 1:CH, j0:j0 + 1], ac[CH - 1:CH, j1:j1 + 1]
            dtl = _pair_lanes(left, dt[:, j0:j0 + 1], dt[:, j1:j1 + 1])
            X = xp * dtl
            Xb = X.astype(BF16)
            eal = _pair_lanes(left, jnp.exp(a0), jnp.exp(a1))
            dtel = _pair_lanes(left, jnp.exp(al0 - a0), jnp.exp(al1 - a1))
            T1p = T1[:, sl]
            Rm = T1p * dtel * X
            GR = dyp * (eal * CS[:, sl]) - Rm
            SdS = dS[:, sl] * S[:, sl]
            dXd = jnp.zeros((CH, LANES), F32)
            for j, aj, alj, mask in ((j0, a0, al0, left), (j1, a1, al1, jnp.logical_not(left))):
                dYm = jnp.where(mask, dyp, 0.0).astype(BF16)
                dWm = _dot(dYm, Xb, NT)
                dWmT = _dot(Xb, dYm, NT)
                yield
                e = aj - acT[j:j + 1, :]
                P = dWm * jnp.exp(jnp.where(tril, e, _NEG))
                LmT = jnp.exp(jnp.where(triu, -e, _NEG))
                PT = dWmT * LmT
                dCB = dCB + P
                dCBT = dCBT + PT
                yield
                dXd = dXd + _dot((cbT * LmT).astype(BF16), dYm)
                qd = P * cb - PT * cbT + jnp.where(mask, GR, 0.0)
                colv = jnp.sum(qd, axis=1, keepdims=True)
                tot = jnp.where(mask, Rm + jnp.exp(alj) * SdS, 0.0)
                dalast = jnp.sum(jnp.sum(tot, axis=0, keepdims=True), axis=1, keepdims=True)
                dAc = dAc + (colv + jnp.where(lastrow, dalast, 0.0)) * (lane8 == j).astype(F32)
                yield
            dX = dXd + dtel * T1p
            dXx = dX * xp
            for j, mask in ((j0, left), (j1, jnp.logical_not(left))):
                dd = jnp.sum(jnp.where(mask, dXx, 0.0), axis=1, keepdims=True)
                ddtx = ddtx + dd * (lane8 == j).astype(F32)
            dx_ref[:, sl] = (dX * dtl + dsk_ref[:, sl] * dyp).astype(BF16)
            dsk_parts.append(_colsum(dyp * xp))
            xd_parts.append(X * dtel)
            dye_parts.append(dyp * eal)
            dec_parts.append(_pair_lanes(left[0:1], jnp.exp(al0), jnp.exp(al1)))
            yield
        Xd = jnp.concatenate(xd_parts, axis=1).astype(BF16)
        dYe = jnp.concatenate(dye_parts, axis=1).astype(BF16)
        dec = jnp.concatenate(dec_parts, axis=1)
        dC_ref[...] = (_dot(dCB.astype(BF16), Bm) + _dot(dYe, Sb, NT)).astype(BF16)
        dB_ref[...] = (_dot(dCBT.astype(BF16), Cm) + _dot(Xd, dSb, NT)).astype(BF16)
        dS_ref[...] = _dot(CmT, dYe) + dec * dS
        _acc_out(dskacc_ref, jnp.concatenate(dsk_parts, axis=1), first)
        return ddtx, dAc

    return pl.pallas_call(
        body, name="ssd_bwd", grid=(nc,),
        in_specs=[sp["xs"], sp["bm"], sp["cmat"], sp["col"], sp["col"], sp["col"], sp["dsk"], sp["xs"],
                  sp["state"]],
        out_specs=[sp["xbc"], sp["col"], sp["col"], sp["dsk"]],
        out_shape=[jax.ShapeDtypeStruct((T, DX), BF16), jax.ShapeDtypeStruct((T, LANES), F32),
                   jax.ShapeDtypeStruct((T, LANES), F32), jax.ShapeDtypeStruct((1, DI), F32)],
        scratch_shapes=[pltpu.VMEM((NS, DI), F32)],
        compiler_params=_params(("arbitrary",)),
    )(xact, xact, xact, dt, acum, acumT, dsk_rep, dy, sprev)


def _gnorm_fwd(y, proj, w):
    T = y.shape[0]
    R = _Rows(T, 1024)
    zb = OFF_Z // _GW

    def body(y_ref, z_ref, w_ref, o_ref):
        z = z_ref[...].astype(F32)
        yf = y_ref[...].astype(F32) * z * _sigmoid(z)
        r = lax.rsqrt(jnp.mean(yf * yf, axis=-1, keepdims=True) + EPS)
        o_ref[...] = (yf * r * w_ref[...]).astype(BF16)

    return R.call(body, "gnorm_fwd", NG, [R.tile(_GW), R.tile(_GW, zb), R.colvec(1, _GW)], R.tile(_GW),
                  jax.ShapeDtypeStruct((T, DI), BF16), (y, proj, w))


def _gnorm_bwd(dn, y, proj, w, dproj):
    T = y.shape[0]
    R = _Rows(T, 1024)
    zb = OFF_Z // _GW

    def body(dn_ref, y_ref, z_ref, w_ref, _alias, dz_ref, dy_ref, dw_ref):
        z = z_ref[...].astype(F32)
        yv = y_ref[...].astype(F32)
        s = _sigmoid(z)
        silu = z * s
        yf = yv * silu
        r = lax.rsqrt(jnp.mean(yf * yf, axis=-1, keepdims=True) + EPS)
        yh = yf * r
        dnv = dn_ref[...].astype(F32)
        dyh = dnv * w_ref[...]
        dyf = r * (dyh - yh * jnp.mean(dyh * yh, axis=-1, keepdims=True))
        dy_ref[...] = (dyf * silu).astype(BF16)
        dz_ref[...] = (dyf * yv * s * (1.0 + z * (1.0 - s))).astype(BF16)
        _acc_out(dw_ref, _colsum(dnv * yh), pl.program_id(1) == 0)

    return R.call(
        body, "gnorm_bwd", NG, [R.tile(_GW), R.tile(_GW), R.tile(_GW, zb), R.colvec(1, _GW), ANY],
        [R.tile(_GW, zb), R.tile(_GW), R.colvec(1, _GW)],
        [jax.ShapeDtypeStruct(dproj.shape, BF16), jax.ShapeDtypeStruct((T, DI), BF16),
         jax.ShapeDtypeStruct((1, DI), F32)],
        (dn, y, proj, w, dproj), aliases={4: 0})


def _merge_fwd(proj, ya, ys):
    T = proj.shape[0]
    R = _Rows(T, 512)
    gb = OFF_G // (2 * D)

    def body(g_ref, ya_ref, ys_ref, o_ref):
        ga = _sigmoid(g_ref[:, :D].astype(F32))
        gs = _sigmoid(g_ref[:, D:].astype(F32))
        o_ref[...] = (ga * ya_ref[...].astype(F32) + gs * ys_ref[...].astype(F32)).astype(BF16)

    return R.call(body, "merge_fwd", 1, [R.tile(2 * D, gb), R.tile(D), R.tile(D)], R.tile(D),
                  jax.ShapeDtypeStruct((T, D), BF16), (proj, ya, ys))


def _merge_bwd(dm, proj, ya, ys, ncols):
    T = proj.shape[0]
    R = _Rows(T, 256)
    gb = OFF_G // (2 * D)

    def body(dm_ref, g_ref, ya_ref, ys_ref, dg_ref, dya_ref, dys_ref):
        d = dm_ref[...].astype(F32)
        ga = _sigmoid(g_ref[:, :D].astype(F32))
        gs = _sigmoid(g_ref[:, D:].astype(F32))
        dya_ref[...] = (d * ga).astype(BF16)
        dys_ref[...] = (d * gs).astype(BF16)
        dg_ref[:, :D] = (d * ya_ref[...].astype(F32) * ga * (1.0 - ga)).astype(BF16)
        dg_ref[:, D:] = (d * ys_ref[...].astype(F32) * gs * (1.0 - gs)).astype(BF16)

    return R.call(
        body, "merge_bwd", 1, [R.tile(D), R.tile(2 * D, gb), R.tile(D), R.tile(D)],
        [R.tile(2 * D, gb), R.tile(D), R.tile(D)],
        [jax.ShapeDtypeStruct((T, ncols), BF16), jax.ShapeDtypeStruct((T, D), BF16),
         jax.ShapeDtypeStruct((T, D), BF16)],
        (dm, proj, ya, ys))


_FW = 1408
_FB = FF // _FW


def _ffn_act_fwd(hv, conv_w, conv_b):
    T = hv.shape[0]
    R = _Rows(T, 256)
    tm = R.tm

    def body(h1_ref, h1p_ref, h3_ref, w_ref, b_ref, o_ref):
        keep = (pl.program_id(1) > 0).astype(F32)
        ext = jnp.concatenate([h1p_ref[...].astype(F32) * keep, h1_ref[...].astype(F32)], axis=0)
        pre = _wsum(w_ref[...], _shifts_causal(ext, 3, tm)) + b_ref[...]
        o_ref[...] = (pre * _sigmoid(pre) * h3_ref[...].astype(F32)).astype(BF16)

    return R.call(body, "ffn_act_fwd", _FB,
                  [R.tile(_FW), R.prev(_FW), R.tile(_FW, _FB), R.colvec(3, _FW), R.colvec(1, _FW)],
                  R.tile(_FW), jax.ShapeDtypeStruct((T, FF), BF16), (hv, hv, hv, conv_w, conv_b))


def _ffn_act_bwd(dg, hv, conv_w, conv_b):
    T = hv.shape[0]
    R = _Rows(T, 256)
    tm = R.tm

    def body(dg_ref, h1_ref, h1p_ref, h3_ref, w_ref, b_ref, dh3_ref, dpre_ref, dw_ref, db_ref):
        i = pl.program_id(1)
        keep = (i > 0).astype(F32)
        ext = jnp.concatenate([h1p_ref[...].astype(F32) * keep, h1_ref[...].astype(F32)], axis=0)
        sh = _shifts_causal(ext, 3, tm)
        pre = _wsum(w_ref[...], sh) + b_ref[...]
        s = _sigmoid(pre)
        d = dg_ref[...].astype(F32)
        dh3_ref[...] = (d * pre * s).astype(BF16)
        dpre = d * h3_ref[...].astype(F32) * s * (1.0 + pre * (1.0 - s))
        dpre_ref[...] = dpre.astype(BF16)
        _acc_rows(dw_ref, [_colsum(dpre * q) for q in sh], i == 0)
        _acc_out(db_ref, _colsum(dpre), i == 0)

    return R.call(
        body, "ffn_act_bwd", _FB,
        [R.tile(_FW), R.tile(_FW), R.prev(_FW), R.tile(_FW, _FB), R.colvec(3, _FW), R.colvec(1, _FW)],
        [R.tile(_FW), R.tile(_FW), R.colvec(3, _FW), R.colvec(1, _FW)],
        [jax.ShapeDtypeStruct((T, FF), BF16), jax.ShapeDtypeStruct((T, FF), BF16),
         jax.ShapeDtypeStruct((3, FF), F32), jax.ShapeDtypeStruct((1, FF), F32)],
        (dg, hv, hv, hv, conv_w, conv_b))


def _conv3_transpose(dpre, conv_w):
    T = dpre.shape[0]
    R = _Rows(T, 256)
    tm = R.tm

    def body(d_ref, dn_ref, w_ref, o_ref):
        keep = (pl.program_id(1) < R.nrow - 1).astype(F32)
        ext = jnp.concatenate([d_ref[...].astype(F32), dn_ref[...].astype(F32) * keep], axis=0)
        o_ref[...] = _wsum(w_ref[...], _shifts_anticausal(ext, 3, tm)).astype(BF16)

    return R.call(body, "ffn_conv_bwd", _FB, [R.tile(_FW), R.next(_FW), R.colvec(3, _FW)], R.tile(_FW),
                  jax.ShapeDtypeStruct((T, FF), BF16), (dpre, dpre, conv_w))


def _final_loss_epilogue(w, target):
    T = target.shape[0]

    def fn(xv, ins, outs, first):
        w_ref, t_ref = ins
        l_ref, dh_ref, dhb_ref, dw_ref = outs
        wv = w_ref[...]
        r = lax.rsqrt(jnp.mean(xv * xv, axis=-1, keepdims=True) + EPS)
        xh = xv * r
        err = xh * wv - t_ref[...]
        part = 0.5 * jnp.sum(jnp.mean(err * err, axis=-1, keepdims=True), axis=0, keepdims=True)
        _acc_out(l_ref, jnp.broadcast_to(part, l_ref.shape), first)
        dy = err * (1.0 / D)
        dxh = dy * wv
        dh = r * (dxh - xh * jnp.mean(dxh * xh, axis=-1, keepdims=True))
        dh_ref[...] = dh
        dhb_ref[...] = dh.astype(BF16)
        _acc_out(dw_ref, _colsum(dy * xh), first)

    return _Epilogue(fn, (w, target),
                     (((8, LANES), F32), ((T, D), F32), ((T, D), BF16), ((1, D), F32)), 10 * D)


def _pad_lanes(v, n=LANES):
    return jnp.pad(v, ((0, 0), (0, n - v.shape[1])))


class _Hooks:
    def before_in_proj(self, w_in):
        return w_in

    def late_weights(self, wts, after):
        return wts

    def grads_ready(self, grads, tie):
        return tie

    def mark(self, name, value):
        pass


def _local_step(x, target, wts, hooks=None):
    hooks = hooks or _Hooks()
    T = x.shape[0]
    w_in = wts["w_in"]
    dt_bias_p, a_log_p = _pad_lanes(wts["dt_bias"]), _pad_lanes(wts["a_log"])
    dsk_rep = jnp.repeat(wts["d_skip"], HP, axis=1)

    w_in = hooks.before_in_proj(w_in)
    proj, u, dt_raw = _norm_matmul(x, wts["norm_mix_w"], w_in, "norm_mm_in", w_in[:, OFF_DT:OFF_DT + LANES])
    ya_in = _branch_a_fwd(proj, wts["conv_a_w"])
    xact = _xbc_fwd(proj, wts["ssd_conv_w"], wts["ssd_conv_b"])
    dt, acum, acumT = _dt_fwd(dt_raw, dt_bias_p, a_log_p)
    y_ssd, sprev = _ssd_fwd(xact, dt, acum, acumT, dsk_rep)
    yn = _gnorm_fwd(y_ssd, proj, wts["ssd_norm_w"])
    late = hooks.late_weights(wts, yn)
    w_a_out, w_s_out, w_o, w_up, w_down = (late[k] for k in ("w_a_out", "w_s_out", "w_o", "w_up", "w_down"))
    y_a = _matmul(ya_in, w_a_out, mode="nn", out_dtype=BF16, name="mm_a_out")
    y_s = _matmul(yn, w_s_out, mode="nn", out_dtype=BF16, name="mm_s_out")
    merged = _merge_fwd(proj, y_a, y_s)
    h1 = _matmul(merged, w_o, mode="nn", out_dtype=F32, name="mm_o", residual=x)
    hv, v = _norm_matmul(h1, wts["norm_ffn_w"], w_up, "norm_mm_up")
    gact = _ffn_act_fwd(hv, wts["ffn_conv_w"], wts["ffn_conv_b"])
    loss, dh2, dh2b, g_final = _matmul(gact, w_down, mode="nn", out_dtype=F32, name="mm_down_loss", residual=h1,
                                       epilogue=_final_loss_epilogue(wts["final_norm_w"], target))

    grads = {"final_norm_w": g_final}
    grads["w_down"] = _matmul(gact, dh2b, mode="tn", out_dtype=F32, name="mm_down_dw")
    dgact = _matmul(dh2b, w_down, mode="nt", out_dtype=BF16, name="mm_down_dx")
    dh3, dpre, grads["ffn_conv_w"], grads["ffn_conv_b"] = _ffn_act_bwd(dgact, hv, wts["ffn_conv_w"], wts["ffn_conv_b"])
    dh1c = _conv3_transpose(dpre, wts["ffn_conv_w"])
    grads["w_up"] = (_matmul(v, dh1c, mode="tn", out_dtype=F32, name="mm_up_dw1"),
                     _matmul(v, dh3, mode="tn", out_dtype=F32, name="mm_up_dw3"))
    dv = _matmul(dh1c, w_up, mode="nt", out_dtype=F32, name="mm_up_dx1")
    dh1, dh1b, grads["norm_ffn_w"] = _matmul(
        dh3, w_up, mode="nt", out_dtype=F32, name="mm_up_dx3_norm", residual=dv, b_k_off=FF,
        epilogue=_rmsnorm_bwd_epilogue(h1, wts["norm_ffn_w"], dh2))
    grads["w_o"] = _matmul(merged, dh1b, mode="tn", out_dtype=F32, name="mm_o_dw")
    dmerged = _matmul(dh1b, w_o, mode="nt", out_dtype=BF16, name="mm_o_dx")
    dproj, dya, dys = _merge_bwd(dmerged, proj, y_a, y_s, NIP)
    grads["w_a_out"] = _matmul(ya_in, dya, mode="tn", out_dtype=F32, name="mm_a_out_dw")
    dya_in = _matmul(dya, w_a_out, mode="nt", out_dtype=BF16, name="mm_a_out_dx")
    dproj, grads["conv_a_w"] = _branch_a_bwd(dya_in, proj, wts["conv_a_w"], dproj)
    grads["w_s_out"] = _matmul(yn, dys, mode="tn", out_dtype=F32, name="mm_s_out_dw")
    dys = hooks.grads_ready({k: grads[k] for k in ("w_a_out", "w_s_out", "w_o", "w_up", "w_down")}, dys)
    dyn =_matmul(dys, w_s_out, mode="nt", out_dtype=BF16, name="mm_s_out_dx")
    dproj, dy_ssd, grads["ssd_norm_w"] = _gnorm_bwd(dyn, y_ssd, proj, wts["ssd_norm_w"], dproj)
    dxact, ddt_x, dacum, dskl = _ssd_bwd(dy_ssd, xact, dt, acum, acumT, dsk_rep, sprev)
    hooks.mark("ssd_bwd", dxact)
    grads["d_skip"] = dskl.reshape(NH, HP).sum(axis=1).reshape(1, NH)
    dproj, grads["ssd_conv_w"], grads["ssd_conv_b"] = _xbc_bwd(dxact, proj, wts["ssd_conv_w"], wts["ssd_conv_b"], dproj)
    dproj, g_dtb, g_alog = _dt_bwd(dacum, ddt_x, dt_raw, dt_bias_p, a_log_p, dproj)
    grads["dt_bias"], grads["a_log"] = g_dtb[:, :NH], g_alog[:, :NH]
    grads["w_in"] = _matmul(u, dproj, mode="tn", out_dtype=F32, name="mm_in_dw")
    dproj = hooks.grads_ready({"w_in": grads["w_in"]}, dproj)
    grad_x, _, grads["norm_mix_w"] = _matmul(dproj, w_in, mode="nt", out_dtype=F32, name="mm_in_dx_norm",
                                             epilogue=_rmsnorm_bwd_epilogue(x, wts["norm_mix_w"], dh1))
    return loss, grad_x, grads


def _permute_w_in(w):
    out = jnp.zeros((w.shape[0], NIP), w.dtype)
    for o, n, no in _SEGS:
        out = lax.dynamic_update_slice(out, w[:, o:o + n], (0, no))
    return out


def _unpermute_w_in(g):
    order = sorted(_SEGS)
    return jnp.concatenate([g[:, no:no + n] for o, n, no in order], axis=1)


MESH = pl.DeviceIdType.MESH
NCHIP = 4
NDEV = 8

_W_IN = (("w_in", D, NI // NCHIP, 1),)
_W_REST = (("w_a_out", D // NCHIP, D, 0), ("w_s_out", DI // NCHIP, D, 0), ("w_o", D // NCHIP, D, 0),
           ("w_up", D, 2 * FF // NCHIP, 1), ("w_down", FF // NCHIP, D, 0))


def _slab_rows(group):
    rows = [r * c // LANES for _, r, c, _ in group]
    assert all(n % 32 == 0 for n in rows), rows
    return rows


def _coords():
    return lax.axis_index("x"), lax.axis_index("y"), lax.axis_index("c")


def _other_chips(x, y):
    return [(1 - x, y), (x, 1 - y), (1 - x, 1 - y)]


def _ag_weights(shard):
    nrows = shard.shape[0]
    hr = nrows // 2

    def body(x_ref, out_ref, send_sems, recv_sems, local_sem):
        x, y, c = _coords()
        me = 2 * x + y
        chips = _other_chips(x, y)

        def rows(s, h):
            return out_ref.at[s, pl.ds(h * hr, hr), :]

        def copy(k, s, h, to, src=None):
            return pltpu.make_async_remote_copy(
                src_ref=rows(s, h) if src is None else src, dst_ref=rows(s, h),
                send_sem=send_sems.at[k], recv_sem=recv_sems.at[k], device_id=to, device_id_type=MESH)

        mine = pltpu.make_async_copy(x_ref, out_ref.at[me], local_sem)
        mine.start()
        first = [copy(k, me, c, (*chip, c), src=x_ref.at[pl.ds(c * hr, hr), :]) for k, chip in enumerate(chips)]
        for cp in first:
            cp.start()
        passed = []
        for k, chip in enumerate(chips):
            s = 2 * chip[0] + chip[1]
            copy(k, s, c, (x, y, c)).wait_recv()
            fwd = copy(3 + k, s, c, (x, y, 1 - c))
            fwd.start()
            passed.append(fwd)
        for k, chip in enumerate(chips):
            copy(3 + k, 2 * chip[0] + chip[1], 1 - c, (x, y, c)).wait_recv()
        for cp in first + passed:
            cp.wait_send()
        mine.wait()

    return pl.pallas_call(
        body, name="ag_weights", in_specs=[ANY], out_specs=ANY,
        out_shape=jax.ShapeDtypeStruct((NCHIP,) + shard.shape, shard.dtype),
        scratch_shapes=[pltpu.SemaphoreType.DMA((6,)), pltpu.SemaphoreType.DMA((6,)), pltpu.SemaphoreType.DMA],
        compiler_params=pltpu.CompilerParams(has_side_effects=True),
    )(shard)


HBM = pl.BlockSpec(memory_space=pltpu.HBM)
SEM = pl.BlockSpec(memory_space=pltpu.SEMAPHORE)
_EFFECT = pltpu.SideEffectType.DATAFLOW_SIDE_EFFECTING
_NCOPY = NCHIP - 1


def _plan_bcast(src_ref, land_ref, send_sems, recv_sems):
    x, y, c = _coords()
    sends, lands = [], []
    for k, chip in enumerate(_other_chips(x, y)):
        def copy(slot):
            return pltpu.make_async_remote_copy(
                src_ref=src_ref, dst_ref=land_ref.at[slot], send_sem=send_sems.at[k], recv_sem=recv_sems.at[k],
                device_id=(*chip, c), device_id_type=MESH)
        sends.append(copy(2 * x + y))
        lands.append(copy(2 * chip[0] + chip[1]))
    return sends, lands


def _plan_scatter(src_ref, land_ref, send_sems, recv_sems):
    x, y, c = _coords()
    cps = [pltpu.make_async_remote_copy(
        src_ref=src_ref.at[2 * chip[0] + chip[1]], dst_ref=land_ref.at[k], send_sem=send_sems.at[k],
        recv_sem=recv_sems.at[k], device_id=(*chip, c), device_id_type=MESH)
        for k, chip in enumerate(_other_chips(x, y))]
    return cps, cps


def _split_start(name, src, land, plan):
    def body(src_ref, land_ref, send_sems, recv_sems, src_thru, land_thru, token):
        for cp in plan(src_ref, land_ref, send_sems, recv_sems)[0]:
            cp.start()
        token[...] = jnp.zeros_like(token)

    send_sems, recv_sems, src_thru, land_thru, token = pl.pallas_call(
        body, name=name,
        out_shape=(pltpu.SemaphoreType.DMA((_NCOPY,)), pltpu.SemaphoreType.DMA((_NCOPY,)),
                   pltpu.HBM(src.shape, src.dtype), pltpu.HBM(land.shape, land.dtype),
                   jax.ShapeDtypeStruct((8, LANES), F32)),
        in_specs=(HBM, HBM), out_specs=(SEM, SEM, HBM, HBM, pl.BlockSpec(memory_space=pltpu.VMEM)),
        input_output_aliases={0: 2, 1: 3},
        compiler_params=pltpu.CompilerParams(has_side_effects=_EFFECT),
    )(pltpu.with_memory_space_constraint(src, pltpu.HBM), pltpu.with_memory_space_constraint(land, pltpu.HBM))
    return (send_sems, recv_sems, src_thru, land_thru), token


def _split_wait(name, handle, after, plan):
    send_sems, recv_sems, src_thru, land_thru = handle

    def body(src_ref, land_ref, send_sems, recv_sems, after_ref, src_out, land_out):
        sends, lands = plan(src_ref, land_ref, send_sems, recv_sems)
        for cp in sends:
            cp.wait_send()
        for cp in lands:
            cp.wait_recv()

    return pl.pallas_call(
        body, name=name,
        out_shape=(pltpu.HBM(src_thru.shape, src_thru.dtype), pltpu.HBM(land_thru.shape, land_thru.dtype)),
        in_specs=(HBM, HBM, SEM, SEM, ANY), out_specs=(HBM, HBM), input_output_aliases={0: 0, 1: 1},
        compiler_params=pltpu.CompilerParams(has_side_effects=_EFFECT),
    )(src_thru, land_thru, send_sems, recv_sems, after)


def _tie(x, token, name):
    def body(x_ref, t_ref, o_ref):
        pass

    return pl.pallas_call(
        body, name=name, in_specs=[ANY, pl.BlockSpec(memory_space=pltpu.VMEM)], out_specs=ANY,
        out_shape=jax.ShapeDtypeStruct(x.shape, x.dtype), input_output_aliases={0: 0},
    )(x, token)


def _swap_sibling(p, name):
    def body(p_ref, land_ref, send_sem, recv_sem):
        x, y, c = _coords()
        cp = pltpu.make_async_remote_copy(
            src_ref=p_ref, dst_ref=land_ref, send_sem=send_sem, recv_sem=recv_sem,
            device_id=(x, y, 1 - c), device_id_type=MESH)
        cp.start()
        cp.wait()

    return pl.pallas_call(
        body, name=name, in_specs=[ANY], out_specs=ANY, out_shape=jax.ShapeDtypeStruct(p.shape, p.dtype),
        scratch_shapes=[pltpu.SemaphoreType.DMA, pltpu.SemaphoreType.DMA],
        compiler_params=pltpu.CompilerParams(has_side_effects=True),
    )(p)


_ADD_BYTES = 7 << 19


def _add_tile(rows, cols):
    best = 32
    for t in range(32, rows + 1, 32):
        if rows % t == 0 and t * cols * 4 <= _ADD_BYTES:
            best = t
    return best


def _add_slabs(pack, land, me, name):
    rows, cols = pack.shape[1:]
    tr = _add_tile(rows, cols)

    def body(me_ref, p_ref, l_ref, o_ref):
        f = lambda r: r.astype(F32)
        o_ref[...] = ((f(p_ref[0]) + f(l_ref[0])) + f(l_ref[1])) + f(l_ref[2])

    return pl.pallas_call(
        body, name=name,
        grid_spec=pltpu.PrefetchScalarGridSpec(
            num_scalar_prefetch=1, grid=(rows // tr,),
            in_specs=[pl.BlockSpec((1, tr, cols), lambda i, me_ref: (me_ref[0], i, 0)),
                      pl.BlockSpec((_NCOPY, tr, cols), lambda i, me_ref: (0, i, 0))],
            out_specs=pl.BlockSpec((tr, cols), lambda i, me_ref: (i, 0))),
        out_shape=jax.ShapeDtypeStruct((rows, cols), F32),
        compiler_params=_params(("parallel",)),
    )(me, pack, land)


def _add_pair(a, b, name):
    rows, cols = a.shape
    tr = _add_tile(rows, cols)

    def body(a_ref, b_ref, o_ref):
        o_ref[...] = a_ref[...] + b_ref[...]

    blk = pl.BlockSpec((tr, cols), lambda i: (i, 0))
    return pl.pallas_call(
        body, name=name, grid=(rows // tr,), in_specs=[blk, blk], out_specs=blk,
        out_shape=jax.ShapeDtypeStruct((rows, cols), F32), compiler_params=_params(("parallel",)),
    )(a, b)


_STAGE_W = 1024


def _stage_rows(shapes):
    pieces, r = [], 0
    for i, (k, w) in enumerate(shapes):
        for a in range(k):
            for q in range(0, w, _STAGE_W):
                pieces.append((i, a, q, min(_STAGE_W, w - q), r))
                r += 1
    return pieces, -(-r // 8) * 8


def _gather8(parts, reduce, name):
    shapes = [p.shape for p in parts]
    pieces, rows = _stage_rows(shapes)
    n = len(parts)

    def body(*refs):
        ins, outs = refs[:n], refs[n:2 * n]
        stage, buf, res, send_sems, recv_sems = refs[2 * n:]
        x, y, c = _coords()
        me = 4 * x + 2 * y + c
        stage[...] = jnp.zeros_like(stage)
        for i, a, q, w, r in pieces:
            stage[r:r + 1, 0:w] = ins[i][a:a + 1, q:q + w]
        buf[pl.ds(me, 1)] = stage[...][None]
        cps, lands = [], []
        for k in range(1, NDEV):
            peer = (1 - x if k & 4 else x, 1 - y if k & 2 else y, 1 - c if k & 1 else c)

            def copy(slot):
                return pltpu.make_async_remote_copy(
                    src_ref=stage, dst_ref=buf.at[slot], send_sem=send_sems.at[k - 1],
                    recv_sem=recv_sems.at[k - 1], device_id=peer, device_id_type=MESH)

            cps.append(copy(me))
            lands.append(copy(4 * peer[0] + 2 * peer[1] + peer[2]))
        for cp in cps:
            cp.start()
        for cp, land in zip(cps, lands):
            land.wait_recv()
            cp.wait_send()
        if reduce:
            acc = buf[0]
            for d in range(1, NDEV):
                acc = acc + buf[d]
            res[...] = acc
            for i, a, q, w, r in pieces:
                outs[i][a:a + 1, q:q + w] = res[r:r + 1, 0:w]
        else:
            for i, a, q, w, r in pieces:
                for s in range(NCHIP):
                    outs[i][s, a:a + 1, q:q + w] = buf[2 * s, r:r + 1, 0:w]

    vm = pl.BlockSpec(memory_space=pltpu.VMEM)
    out_shapes = [jax.ShapeDtypeStruct(s if reduce else (NCHIP,) + s, F32) for s in shapes]
    return pl.pallas_call(
        body, name=name, in_specs=[vm] * n, out_specs=[vm] * n, out_shape=out_shapes,
        scratch_shapes=[pltpu.VMEM((rows, _STAGE_W), F32), pltpu.VMEM((NDEV, rows, _STAGE_W), F32),
                        pltpu.VMEM((rows, _STAGE_W), F32), pltpu.SemaphoreType.DMA((NDEV - 1,)),
                        pltpu.SemaphoreType.DMA((NDEV - 1,))],
        compiler_params=pltpu.CompilerParams(has_side_effects=True),
    )(*parts)


def _adamw_update(w_ref, g_ref, m_ref, v_ref, d_ref, mo_ref, vo_ref):
    c1 = 1.0 / (1.0 - ADAM_B1 ** ADAM_STEP)
    c2 = 1.0 / (1.0 - ADAM_B2 ** ADAM_STEP)
    gv = g_ref[...]
    mn = ADAM_B1 * m_ref[...] + (1.0 - ADAM_B1) * gv
    vn = ADAM_B2 * v_ref[...] + (1.0 - ADAM_B2) * (gv * gv)
    d_ref[...] = -ADAM_LR * ((mn * c1) / (jnp.sqrt(vn * c2) + ADAM_EPS) + ADAM_WD * w_ref[...])
    mo_ref[...] = mn
    vo_ref[...] = vn


def _adamw_small(ws, gs, ms, vs):
    n = len(ws)

    def body(*refs):
        for i in range(n):
            _adamw_update(*(refs[j * n + i] for j in range(7)))

    vm = pl.BlockSpec(memory_space=pltpu.VMEM)
    outs = pl.pallas_call(
        body, name="adamw_small", in_specs=[vm] * (4 * n), out_specs=[vm] * (3 * n),
        out_shape=[jax.ShapeDtypeStruct(w.shape, F32) for w in ws] * 3,
    )(*ws, *gs, *ms, *vs)
    return outs[:n], outs[n:2 * n], outs[2 * n:]


def _adamw(w, g, m, v, name):
    rows, cols = w.shape
    tr = rows
    while tr * cols * 4 > (3 << 19) and tr % 16 == 0:
        tr //= 2

    def body(*refs):
        _adamw_update(*refs)

    blk = pl.BlockSpec((tr, cols), lambda i: (i, 0))
    return pl.pallas_call(
        body, name=name, grid=(rows // tr,), in_specs=[blk] * 4, out_specs=[blk] * 3,
        out_shape=[jax.ShapeDtypeStruct((rows, cols), F32)] * 3, compiler_params=_params(("parallel",)),
    )(w, g, m, v)


def _rows128(a, mult=8):
    flat = a.reshape(-1)
    n = -(-flat.shape[0] // (LANES * mult)) * LANES * mult
    return jnp.pad(flat, (0, n - flat.shape[0])).reshape(-1, LANES)


def _pack_rows(parts, total_rows):
    rows = sum(p.shape[0] for p in parts)
    if total_rows > rows:
        parts = list(parts) + [jnp.zeros((total_rows - rows, LANES), parts[0].dtype)]
    return jnp.concatenate(parts, axis=0)


def _unpack_rows(pack, shapes, mult=8):
    out, r = [], 0
    for shp in shapes:
        n = int(np.prod(shp))
        nr = -(-n // (LANES * mult)) * mult
        out.append(pack[r:r + nr].reshape(-1)[:n].reshape(shp))
        r += nr
    return out


def _unpack_full(full, group):
    out, r = {}, 0
    for (name, rr, cc, axis), nr in zip(group, _slab_rows(group)):
        seg = full[:, r:r + nr].reshape(NCHIP, rr, cc)
        out[name] = seg.reshape(NCHIP * rr, cc) if axis == 0 else seg.transpose(1, 0, 2).reshape(rr, NCHIP * cc)
        r += nr
    return out


def _by_chip(g, rr, cc, axis):
    if isinstance(g, tuple):
        n = NCHIP // len(g)
        return jnp.concatenate([h.reshape(rr, n, cc).transpose(1, 0, 2) for h in g], axis=0)
    return g.reshape(NCHIP, rr, cc) if axis == 0 else g.reshape(rr, NCHIP, cc).transpose(1, 0, 2)


def _pack_by_chip(grads, group, dtype):
    parts = [_by_chip(jax.tree.map(lambda t: t.astype(dtype), grads[name]), rr, cc, axis).reshape(NCHIP, nr, LANES)
             for (name, rr, cc, axis), nr in zip(group, _slab_rows(group))]
    return jnp.concatenate(parts, axis=1)


_SMALL_REPL = ("norm_mix_w", "ssd_conv_b", "dt_bias", "a_log", "d_skip", "ssd_norm_w", "norm_ffn_w",
               "ffn_conv_b", "final_norm_w")
_SMALL_CONV = (("conv_a_w", 3, D), ("ssd_conv_w", 4, DX), ("ffn_conv_w", 3, FF))


def kernel(x, norm_mix_w, w_in, conv_a_w, w_a_out, ssd_conv_w, ssd_conv_b, dt_bias, a_log, d_skip, ssd_norm_w, w_s_out, w_o, norm_ffn_w, w_up, ffn_conv_w, ffn_conv_b, w_down, final_norm_w, loss_target, m_norm_mix_w, m_w_in, m_conv_a_w, m_w_a_out, m_ssd_conv_w, m_ssd_conv_b, m_dt_bias, m_a_log, m_d_skip, m_ssd_norm_w, m_w_s_out, m_w_o, m_norm_ffn_w, m_w_up, m_ffn_conv_w, m_ffn_conv_b, m_w_down, m_final_norm_w, v_norm_mix_w, v_w_in, v_conv_a_w, v_w_a_out, v_ssd_conv_w, v_ssd_conv_b, v_dt_bias, v_a_log, v_d_skip, v_ssd_norm_w, v_w_s_out, v_w_o, v_norm_ffn_w, v_w_up, v_ffn_conv_w, v_ffn_conv_b, v_w_down, v_final_norm_w):
    names = ("norm_mix_w", "w_in", "conv_a_w", "w_a_out", "ssd_conv_w", "ssd_conv_b", "dt_bias", "a_log", "d_skip",
             "ssd_norm_w", "w_s_out", "w_o", "norm_ffn_w", "w_up", "ffn_conv_w", "ffn_conv_b", "w_down", "final_norm_w")
    W = dict(zip(names, (norm_mix_w, w_in, conv_a_w, w_a_out, ssd_conv_w, ssd_conv_b, dt_bias, a_log, d_skip,
                         ssd_norm_w, w_s_out, w_o, norm_ffn_w, w_up, ffn_conv_w, ffn_conv_b, w_down, final_norm_w)))
    M = dict(zip(names, (m_norm_mix_w, m_w_in, m_conv_a_w, m_w_a_out, m_ssd_conv_w, m_ssd_conv_b, m_dt_bias, m_a_log,
                         m_d_skip, m_ssd_norm_w, m_w_s_out, m_w_o, m_norm_ffn_w, m_w_up, m_ffn_conv_w, m_ffn_conv_b,
                         m_w_down, m_final_norm_w)))
    V = dict(zip(names, (v_norm_mix_w, v_w_in, v_conv_a_w, v_w_a_out, v_ssd_conv_w, v_ssd_conv_b, v_dt_bias, v_a_log,
                         v_d_skip, v_ssd_norm_w, v_w_s_out, v_w_o, v_norm_ffn_w, v_w_up, v_ffn_conv_w, v_ffn_conv_b,
                         v_w_down, v_final_norm_w)))
    two_d = lambda a: a.reshape(-1, a.shape[-1])
    W2, M2, V2 = ({k: two_d(a) for k, a in t.items()} for t in (W, M, V))
    xi, yi, ci = _coords()
    me = 2 * xi + yi

    meidx = me.reshape(1).astype(jnp.int32)
    state = {}


    class Hooks(_Hooks):
        def before_in_proj(self, w_in):
            return _tie(w_in, state["rest_token"], "tie_ag_rest")

        def late_weights(self, wts, after):
            own, land = _split_wait("ag_rest_wait", state["rest"], after, _plan_bcast)
            full = _unpack_full(lax.dynamic_update_slice(land, own[None], (me, 0, 0)), _W_REST)
            return {**wts, **full}

        def grads_ready(self, grads, tie):
            if "w_in" in grads:
                key = "g_in"
                pack = _by_chip(_unpermute_w_in(grads["w_in"]).astype(BF16), *_W_IN[0][1:])
            else:
                key, pack = "g_rest", _pack_by_chip(grads, _W_REST, BF16)
            land = lax.empty((_NCOPY,) + pack.shape[1:], BF16)
            state[key], token = _split_start("rs_" + key + "_start", pack, land, _plan_scatter)
            return _tie(tie, token, "tie_" + key)

        def mark(self, name, value):
            state[name] = value

    def reduced(key, after):
        pack, land = _split_wait("rs_" + key + "_wait", state[key], after, _plan_scatter)
        mine = _add_slabs(pack, land, meidx, "rs_" + key + "_add_chips")
        return _add_pair(mine, _swap_sibling(mine, "rs_" + key + "_swap"), "rs_" + key + "_add_cores")

    w_in_full = _ag_weights(W2["w_in"].astype(BF16)).transpose(1, 0, 2).reshape(D, NI)
    wts = {k: W2[k] for k in _SMALL_REPL}
    conv_by_chip = _gather8([W2[n] for n, *_ in _SMALL_CONV], False, "ag_conv_weights")
    for (n, kk, width), stacked in zip(_SMALL_CONV, conv_by_chip):
        wts[n] = stacked.transpose(1, 0, 2).reshape(kk, width)
    rest_slab = _tie(_pack_rows([_rows128(W2[n], 16) for n, *_ in _W_REST], 0).astype(BF16), conv_by_chip[0],
                     "tie_ag_order")
    state["rest"], state["rest_token"] = _split_start(
        "ag_rest_start", rest_slab, lax.empty((NCHIP,) + rest_slab.shape, BF16), _plan_bcast)
    wts["w_in"] = _permute_w_in(w_in_full)

    loss8, grad_x, grads = _local_step(x[0], loss_target[0], wts, Hooks())

    gbig = dict(zip([n for n, *_ in _W_REST],
                    _unpack_rows(reduced("g_rest", state["ssd_bwd"]), [(rr, cc) for _, rr, cc, _ in _W_REST], 16)))
    gbig["w_in"] = reduced("g_in", grad_x)

    small_parts = [grads[n] for n in _SMALL_REPL] + [loss8[0:1]] + [grads[n] for n, *_ in _SMALL_CONV]
    small_g = _gather8(small_parts, True, "allreduce_small")
    gsm = dict(zip(_SMALL_REPL, small_g[:len(_SMALL_REPL)]))
    loss = small_g[len(_SMALL_REPL)][0, 0]
    for (n, kk, width), gfull in zip(_SMALL_CONV, small_g[len(_SMALL_REPL) + 1:]):
        cw = width // NCHIP
        gsm[n] = lax.dynamic_slice(gfull, (0, me * cw), (kk, cw))

    G, DW, NM, NV = {}, {}, {}, {}
    for n in [b[0] for b in _W_IN + _W_REST]:
        G[n] = gbig[n]
        DW[n], NM[n], NV[n] = _adamw(W2[n], G[n], M2[n], V2[n], "adamw_" + n)
    sm_names = list(_SMALL_REPL) + [n for n, *_ in _SMALL_CONV]
    outs = _adamw_small(*([t[n] for n in sm_names] for t in (W2, gsm, M2, V2)))
    for t, vals in zip((DW, NM, NV), outs):
        t.update(zip(sm_names, vals))
    G.update(gsm)

    def shaped(t):
        return [t[n].reshape(W[n].shape) for n in names]

    return (loss, grad_x.reshape(x.shape), *shaped(G), *shaped(DW), *shaped(NM), *shaped(NV))
```

```python
import functools

import jax
import jax.numpy as jnp
import numpy as np
from jax import lax
from jax.experimental import pallas as pl
from jax.experimental.pallas import tpu as pltpu

F32 = jnp.float32
BF16 = jnp.bfloat16

D = 1024
DI = 2048
NH = 32
HP = 64
NG = 4
NS = 128
CH = 128
DX = 3072
FF = 2816
NI = 10272
EPS = 1e-5

OFF_BCV, OFF_XBC, OFF_G, OFF_Z, OFF_DT = 0, 3072, 6144, 8192, 10240
NIP = 10752
_SEGS = ((0, 2048, OFF_G), (2048, 3072, OFF_BCV), (5120, 2048, OFF_Z), (7168, 3072, OFF_XBC), (10240, 32, OFF_DT))

LANES = 128
HALO = 16
V7X_VMEM_LIMIT = 56 * 2 ** 20

ADAM_LR, ADAM_B1, ADAM_B2, ADAM_EPS, ADAM_WD, ADAM_STEP = 0.001, 0.9, 0.999, 1e-08, 0.01, 10

NN = (((1,), (0,)), ((), ()))
NT = (((1,), (1,)), ((), ()))
TN = (((0,), (0,)), ((), ()))


def _dot(a, b, dims=NN):
    return lax.dot_general(a, b, dims, preferred_element_type=F32)


def _params(sem, **kw):
    return pltpu.CompilerParams(dimension_semantics=sem, vmem_limit_bytes=V7X_VMEM_LIMIT, **kw)


V7X_MXU = 256
V7X_HBM_BYTES_PER_S = 3.5e12
STEP_S = 0.35e-6
MATMUL_VMEM = 40 * 2 ** 20
EPILOGUE_VMEM = 46 * 2 ** 20


ACC_BYTES_PER_S = 1.2e13


def _divisors(dim, cap, units):
    for unit in units:
        c = [t for t in range(unit, min(dim, cap) + 1, unit) if dim % t == 0]
        if c:
            return c
    return [dim]


def _tiles(M, N, K, out_bytes, has_res):
    best = None
    for tn in _divisors(N, 2816, (V7X_MXU, LANES)):
        for tm in _divisors(M, 2816, (LANES,)):
            for tk in _divisors(K, 2816, (V7X_MXU, LANES)):
                nk, ni, nj = K // tk, M // tm, N // tn
                vmem = 4 * (tm * tk + tk * tn) + 2 * tm * tn * out_bytes
                vmem += (4 * tm * tn if nk > 1 else 0) + (8 * tm * tn if has_res else 0)
                if vmem > MATMUL_VMEM:
                    continue
                a_reads = M * K * 2 * (nj if nk > 1 else 1)
                b_reads = K * N * 2 * (ni if nk * nj > 1 else 1)
                cost = (a_reads + b_reads + M * N * out_bytes) / V7X_HBM_BYTES_PER_S + ni * nj * nk * STEP_S
                cost += (nk - 1) * M * N * 8 / ACC_BYTES_PER_S
                if best is None or cost < best[0]:
                    best = (cost, tm, tn, tk)
    assert best is not None, (M, N, K)
    return best[1:]


def _sigmoid(x):
    return 1.0 / (1.0 + jnp.exp(-x))


class _Epilogue:
    def __init__(self, fn, ins, outs, tile_bytes):
        self.fn, self.ins, self.outs, self.tile_bytes = fn, tuple(ins), tuple(outs), tile_bytes


def _matmul(a, b, *, mode, out_dtype, name, residual=None, b_k_off=0, epilogue=None):
    if mode == "nn":
        (M, K), (K2, N) = a.shape, b.shape
    elif mode == "nt":
        (M, K), (N, K2) = a.shape, (b.shape[0], a.shape[1])
        assert b_k_off + K <= b.shape[1]
    else:
        (K, M), (K2, N) = a.shape, b.shape
    assert K == K2, (name, a.shape, b.shape)
    tm, tn, tk = _tiles(M, N, K, jnp.dtype(out_dtype).itemsize, residual is not None)
    if epilogue is not None:
        tn = N
        fits = [(K * N * 2 * (M // t) / V7X_HBM_BYTES_PER_S + (K // q - 1) * M * N * 8 / ACC_BYTES_PER_S
                 + (M // t) * (K // q) * STEP_S, t, q)
                for t in (1024, 512, 256) if M % t == 0 for q in _divisors(K, 2816, (V7X_MXU, LANES))
                if 4 * (t * q + q * tn) + (4 * t * tn if K > q else 0) + (8 * t * tn if residual is not None else 0)
                + 2 * t * epilogue.tile_bytes <= EPILOGUE_VMEM]
        _, tm, tk = min(fits)
    nk = K // tk
    if mode == "tn":
        a_spec = pl.BlockSpec((tk, tm), lambda i, j, k: (k, i))
    else:
        a_spec = pl.BlockSpec((tm, tk), lambda i, j, k: (i, k))
    if mode == "nt":
        assert b_k_off % tk == 0
        b_spec = pl.BlockSpec((tn, tk), lambda i, j, k: (j, k + b_k_off // tk))
    else:
        b_spec = pl.BlockSpec((tk, tn), lambda i, j, k: (k, j))
    dims = {"nn": NN, "nt": NT, "tn": TN}[mode]
    o_spec = pl.BlockSpec((tm, tn), lambda i, j, k: (i, j))
    has_res = residual is not None

    def rows_or_whole(shape):
        if shape[0] == M:
            return pl.BlockSpec((tm,) + tuple(shape[1:]), lambda i, j, k: (i,) + (0,) * (len(shape) - 1))
        return pl.BlockSpec(tuple(shape), lambda i, j, k: (0,) * len(shape))

    n_in = 2 + has_res + (len(epilogue.ins) if epilogue else 0)
    n_out = len(epilogue.outs) if epilogue else 1

    def body(*refs):
        a_ref, b_ref = refs[:2]
        r_ref = refs[2] if has_res else None
        out_refs = refs[n_in:n_in + n_out]
        acc_ref = refs[-1]
        k = pl.program_id(2)
        part = _dot(a_ref[...], b_ref[...], dims)

        def finish(r):
            if has_res:
                r = r + r_ref[...].astype(F32)
            if epilogue is None:
                out_refs[0][...] = r.astype(out_dtype)
            else:
                epilogue.fn(r, refs[2 + has_res:n_in], out_refs, pl.program_id(0) == 0)

        if nk == 1:
            finish(part)
            return

        @pl.when(k == 0)
        def _():
            acc_ref[...] = part

        @pl.when(jnp.logical_and(k > 0, k < nk - 1))
        def _():
            acc_ref[...] += part

        @pl.when(k == nk - 1)
        def _():
            finish(acc_ref[...] + part)

    in_specs = [a_spec, b_spec] + ([o_spec] if has_res else [])
    args = (a, b) + ((residual,) if has_res else ())
    if epilogue is None:
        out_specs, out_shape = o_spec, jax.ShapeDtypeStruct((M, N), out_dtype)
        sem = ("parallel", "parallel", "arbitrary")
    else:
        in_specs += [rows_or_whole(x.shape) for x in epilogue.ins]
        args += epilogue.ins
        out_specs = [rows_or_whole(shp) for shp, _ in epilogue.outs]
        out_shape = [jax.ShapeDtypeStruct(shp, dt) for shp, dt in epilogue.outs]
        sem = ("arbitrary", "arbitrary", "arbitrary")
    return pl.pallas_call(
        body, name=name, grid=(M // tm, N // tn, nk), in_specs=in_specs, out_specs=out_specs,
        out_shape=out_shape, scratch_shapes=[pltpu.VMEM((tm, tn), F32)] if nk > 1 else [],
        compiler_params=_params(sem),
    )(*args)


class _Rows:
    def __init__(self, T, tm):
        self.T, self.tm = T, min(tm, T // 2)
        self.nrow = T // self.tm
        self.r = self.tm // HALO
        self.nb = T // HALO

    def tile(self, w, cb=0, step=1):
        return pl.BlockSpec((self.tm, w), lambda j, i: (i, cb + step * j))

    def prev(self, w, cb=0, step=1):
        r = self.r
        return pl.BlockSpec((HALO, w), lambda j, i: (jnp.maximum(i * r - 1, 0), cb + step * j))

    def next(self, w, cb=0, step=1):
        r, nb = self.r, self.nb
        return pl.BlockSpec((HALO, w), lambda j, i: (jnp.minimum((i + 1) * r, nb - 1), cb + step * j))

    def colvec(self, k, w, cb=0, step=1):
        return pl.BlockSpec((k, w), lambda j, i: (0, cb + step * j))

    def call(self, body, name, ncol, in_specs, out_specs, out_shape, args, aliases=None):
        return pl.pallas_call(
            body, name=name, grid=(ncol, self.nrow), in_specs=in_specs, out_specs=out_specs,
            out_shape=out_shape, input_output_aliases=aliases or {},
            compiler_params=_params(("parallel", "arbitrary")),
        )(*args)


ANY = pl.BlockSpec(memory_space=pl.ANY)


def _shifts_causal(ext, nk, tm):
    out = []
    for k in range(nk):
        s = nk - 1 - k
        r = ext if s == 0 else pltpu.roll(ext, s, 0)
        out.append(r[HALO:])
    return out


def _shifts_anticausal(ext, nk, tm):
    n = ext.shape[0]
    out = []
    for k in range(nk):
        s = nk - 1 - k
        r = ext if s == 0 else pltpu.roll(ext, n - s, 0)
        out.append(r[:tm])
    return out


def _wsum(w, parts):
    acc = w[0:1, :] * parts[0]
    for k in range(1, len(parts)):
        acc = acc + w[k:k + 1, :] * parts[k]
    return acc


def _colsum(x):
    return jnp.sum(x, axis=0, keepdims=True)


def _acc_out(ref, val, first):
    @pl.when(first)
    def _():
        ref[...] = val

    @pl.when(jnp.logical_not(first))
    def _():
        ref[...] += val


def _acc_rows(ref, rows, first):
    for k, r in enumerate(rows):
        _acc_out(ref.at[k:k + 1, :], r, first)


def _norm_matmul(x, wn, b, name, b_f32=None):
    T, N = x.shape[0], b.shape[1]
    tm = min(1024, T)
    tn = max(t for t in _divisors(N, 2816, (V7X_MXU, LANES))
             if 8 * tm * D + 6 * tm * D + 4 * D * t + 4 * tm * t <= MATMUL_VMEM)

    extra = b_f32 is not None

    def body(*refs):
        x_ref, wn_ref, b_ref = refs[:3]
        o_ref, u_ref = refs[3 + extra:5 + extra]
        keep_ref = refs[-1]

        @pl.when(pl.program_id(1) == 0)
        def _():
            xv = x_ref[...]
            r = lax.rsqrt(jnp.mean(xv * xv, axis=-1, keepdims=True) + EPS)
            u = (xv * r * wn_ref[...]).astype(BF16)
            keep_ref[...] = u
            u_ref[...] = u
            if extra:
                refs[5 + extra][...] = _dot(u, refs[3][...])

        o_ref[...] = _dot(keep_ref[...], b_ref[...]).astype(BF16)

    rows = pl.BlockSpec((tm, D), lambda i, j: (i, 0))
    whole = lambda shape: pl.BlockSpec(shape, lambda i, j: (0, 0))
    narrow = pl.BlockSpec((tm, LANES), lambda i, j: (i, 0))
    return pl.pallas_call(
        body, name=name, grid=(T // tm, N // tn),
        in_specs=[rows, whole((1, D)), pl.BlockSpec((D, tn), lambda i, j: (0, j))] + [whole((D, LANES))] * extra,
        out_specs=[pl.BlockSpec((tm, tn), lambda i, j: (i, j)), rows] + [narrow] * extra,
        out_shape=[jax.ShapeDtypeStruct((T, N), BF16), jax.ShapeDtypeStruct((T, D), BF16)]
        + [jax.ShapeDtypeStruct((T, LANES), F32)] * extra,
        scratch_shapes=[pltpu.VMEM((tm, D), BF16)],
        compiler_params=_params(("parallel", "arbitrary")),
    )(*((x, wn, b) + ((b_f32,) if extra else ())))


def _rmsnorm_bwd_epilogue(x, w, dres):
    T = x.shape[0]

    def fn(dyv, ins, outs, first):
        x_ref, w_ref, dr_ref = ins
        dx_ref, dxb_ref, dw_ref = outs
        xv = x_ref[...]
        r = lax.rsqrt(jnp.mean(xv * xv, axis=-1, keepdims=True) + EPS)
        xh = xv * r
        dxh = dyv * w_ref[...]
        dx = r * (dxh - xh * jnp.mean(dxh * xh, axis=-1, keepdims=True)) + dr_ref[...]
        dx_ref[...] = dx
        dxb_ref[...] = dx.astype(BF16)
        _acc_out(dw_ref, _colsum(dyv * xh), first)

    return _Epilogue(fn, (x, w, dres), (((T, D), F32), ((T, D), BF16), ((1, D), F32)), 14 * D)


def _branch_a_fwd(proj, conv_w):
    T = proj.shape[0]
    R = _Rows(T, 512)
    tm = R.tm

    def body(p_ref, pp_ref, w_ref, o_ref):
        keep = (pl.program_id(1) > 0).astype(F32)
        cv = p_ref[:, D:2 * D].astype(F32) * p_ref[:, 2 * D:].astype(F32)
        cvp = pp_ref[:, D:2 * D].astype(F32) * pp_ref[:, 2 * D:].astype(F32) * keep
        sh = _shifts_causal(jnp.concatenate([cvp, cv], axis=0), 3, tm)
        ca = _wsum(w_ref[...], sh)
        o_ref[...] = (p_ref[:, :D].astype(F32) * ca).astype(BF16)

    return R.call(body, "branch_a_fwd", 1, [R.tile(3 * D), R.prev(3 * D), R.colvec(3, D)], R.tile(D),
                  jax.ShapeDtypeStruct((T, D), BF16), (proj, proj, conv_w))


def _branch_a_bwd(dya_in, proj, conv_w, dproj):
    T = proj.shape[0]
    R = _Rows(T, 256)
    tm = R.tm

    def body(d_ref, dn_ref, p_ref, pp_ref, pn_ref, w_ref, _alias, o_ref, dw_ref):
        i = pl.program_id(1)
        keep_p = (i > 0).astype(F32)
        keep_n = (i < R.nrow - 1).astype(F32)
        w = w_ref[...]
        b = p_ref[:, :D].astype(F32)
        c = p_ref[:, D:2 * D].astype(F32)
        v = p_ref[:, 2 * D:].astype(F32)
        cvp = pp_ref[:, D:2 * D].astype(F32) * pp_ref[:, 2 * D:].astype(F32) * keep_p
        sh = _shifts_causal(jnp.concatenate([cvp, c * v], axis=0), 3, tm)
        ca = _wsum(w, sh)
        d = d_ref[...].astype(F32)
        dca = d * b
        dca_n = dn_ref[...].astype(F32) * pn_ref[:, :D].astype(F32) * keep_n
        dsh = _shifts_anticausal(jnp.concatenate([dca, dca_n], axis=0), 3, tm)
        dcv = _wsum(w, dsh)
        o_ref[:, :D] = (d * ca).astype(BF16)
        o_ref[:, D:2 * D] = (dcv * v).astype(BF16)
        o_ref[:, 2 * D:] = (dcv * c).astype(BF16)
        _acc_rows(dw_ref, [_colsum(dca * s) for s in sh], i == 0)

    return R.call(
        body, "branch_a_bwd", 1,
        [R.tile(D), R.next(D), R.tile(3 * D), R.prev(3 * D), R.next(3 * D), R.colvec(3, D), ANY],
        [R.tile(3 * D), R.colvec(3, D)],
        [jax.ShapeDtypeStruct(dproj.shape, BF16), jax.ShapeDtypeStruct((3, D), F32)],
        (dya_in, dya_in, proj, proj, proj, conv_w, dproj), aliases={6: 0})


_XW = 512


def _xbc_fwd(proj, conv_w, conv_b):
    T = proj.shape[0]
    R = _Rows(T, 512)
    tm = R.tm
    cb = OFF_XBC // _XW

    def body(x_ref, xp_ref, w_ref, b_ref, o_ref):
        keep = (pl.program_id(1) > 0).astype(F32)
        ext = jnp.concatenate([xp_ref[...].astype(F32) * keep, x_ref[...].astype(F32)], axis=0)
        pre = _wsum(w_ref[...], _shifts_causal(ext, 4, tm)) + b_ref[...]
        o_ref[...] = (pre * _sigmoid(pre)).astype(BF16)

    return R.call(body, "xbc_fwd", DX // _XW,
                  [R.tile(_XW, cb), R.prev(_XW, cb), R.colvec(4, _XW), R.colvec(1, _XW)], R.tile(_XW),
                  jax.ShapeDtypeStruct((T, DX), BF16), (proj, proj, conv_w, conv_b))


def _xbc_bwd(dact, proj, conv_w, conv_b, dproj):
    T = proj.shape[0]
    R = _Rows(T, 512)
    tm = R.tm
    cb = OFF_XBC // _XW

    def body(d_ref, dn_ref, x_ref, xp_ref, xn_ref, w_ref, b_ref, _alias, o_ref, dw_ref, db_ref):
        i = pl.program_id(1)
        keep_p = (i > 0).astype(F32)
        keep_n = (i < R.nrow - 1).astype(F32)
        w = w_ref[...]
        ext = jnp.concatenate([xp_ref[...].astype(F32) * keep_p, x_ref[...].astype(F32),
                               xn_ref[...].astype(F32)], axis=0)
        sh = _shifts_causal(ext, 4, tm + HALO)
        pre = _wsum(w, sh) + b_ref[...]
        s = _sigmoid(pre)
        dsilu = s * (1.0 + pre * (1.0 - s))
        dext = jnp.concatenate([d_ref[...].astype(F32), dn_ref[...].astype(F32) * keep_n], axis=0)
        dpre = dext * dsilu
        dsh = _shifts_anticausal(dpre, 4, tm)
        o_ref[...] = _wsum(w, dsh).astype(BF16)
        dp = dpre[:tm]
        _acc_rows(dw_ref, [_colsum(dp * q[:tm]) for q in sh], i == 0)
        _acc_out(db_ref, _colsum(dp), i == 0)

    return R.call(
        body, "xbc_bwd", DX // _XW,
        [R.tile(_XW), R.next(_XW), R.tile(_XW, cb), R.prev(_XW, cb), R.next(_XW, cb),
         R.colvec(4, _XW), R.colvec(1, _XW), ANY],
        [R.tile(_XW, cb), R.colvec(4, _XW), R.colvec(1, _XW)],
        [jax.ShapeDtypeStruct(dproj.shape, BF16), jax.ShapeDtypeStruct((4, DX), F32),
         jax.ShapeDtypeStruct((1, DX), F32)],
        (dact, dact, proj, proj, proj, conv_w, conv_b, dproj), aliases={7: 0})


def _softplus(x):
    return jnp.maximum(x, 0.0) + jnp.log(1.0 + jnp.exp(-jnp.abs(x)))


def _dt_rows(T):
    return min(8 * CH, T // 2)


def _dt_fwd(dt_raw, dt_bias_p, a_log_p):
    T = dt_raw.shape[0]
    rows = _dt_rows(T)

    def body(r_ref, b_ref, al_ref, dt_ref, ac_ref, acT_ref):
        dt = _softplus(r_ref[...] + b_ref[...])
        s = dt * (-jnp.exp(al_ref[...]))
        row = lax.broadcasted_iota(jnp.int32, (rows, LANES), 0) % CH
        k = 1
        while k < CH:
            s = s + jnp.where(row >= k, pltpu.roll(s, k, 0), 0.0)
            k *= 2
        dt_ref[...] = dt
        ac_ref[...] = s
        for q in range(0, rows, CH):
            acT_ref[q:q + CH] = s[q:q + CH].T

    blk = pl.BlockSpec((rows, LANES), lambda i: (i, 0))
    vec = pl.BlockSpec((1, LANES), lambda i: (0, 0))
    return pl.pallas_call(
        body, name="dt_fwd", grid=(T // rows,), in_specs=[blk, vec, vec], out_specs=[blk, blk, blk],
        out_shape=[jax.ShapeDtypeStruct((T, LANES), F32)] * 3, compiler_params=_params(("parallel",)),
    )(dt_raw, dt_bias_p, a_log_p)


def _dt_bwd(dacum, ddt_x, dt_raw, dt_bias_p, a_log_p, dproj):
    T = dt_raw.shape[0]
    rows = _dt_rows(T)
    nc = T // rows

    def body(da_ref, dx_ref, r_ref, b_ref, al_ref, _alias, o_ref, db_ref, dal_ref):
        i = pl.program_id(0)
        a = -jnp.exp(al_ref[...])
        z = r_ref[...] + b_ref[...]
        dt = _softplus(z)
        s = da_ref[...]
        row = lax.broadcasted_iota(jnp.int32, (rows, LANES), 0) % CH
        k = 1
        while k < CH:
            s = s + jnp.where(row < CH - k, pltpu.roll(s, rows - k, 0), 0.0)
            k *= 2
        ddt = s * a + dx_ref[...]
        draw = ddt * _sigmoid(z)
        o_ref[:, :LANES] = draw.astype(BF16)
        o_ref[:, LANES:] = jnp.zeros((rows, NIP - OFF_DT - LANES), BF16)
        _acc_out(db_ref, _colsum(draw), i == 0)
        _acc_out(dal_ref, _colsum(s * dt), i == 0)

        @pl.when(i == nc - 1)
        def _():
            dal_ref[...] = dal_ref[...] * a

    blk = pl.BlockSpec((rows, LANES), lambda i: (i, 0))
    vec = pl.BlockSpec((1, LANES), lambda i: (0, 0))
    oblk = pl.BlockSpec((rows, NIP - OFF_DT), lambda i: (i, OFF_DT // (NIP - OFF_DT)))
    return pl.pallas_call(
        body, name="dt_bwd", grid=(nc,), in_specs=[blk, blk, blk, vec, vec, ANY], out_specs=[oblk, vec, vec],
        out_shape=[jax.ShapeDtypeStruct(dproj.shape, BF16), jax.ShapeDtypeStruct((1, LANES), F32),
                   jax.ShapeDtypeStruct((1, LANES), F32)],
        input_output_aliases={5: 0}, compiler_params=_params(("arbitrary",)),
    )(dacum, ddt_x, dt_raw, dt_bias_p, a_log_p, dproj)


_GW = DI // NG
_HG = NH // NG
_NEG = -1e30


def _interleave(gens):
    out, live = [None] * len(gens), list(range(len(gens)))
    while live:
        for i in list(live):
            try:
                next(gens[i])
            except StopIteration as stop:
                out[i] = stop.value
                live.remove(i)
    return out


def _pair_lanes(left, v0, v1):
    return jnp.where(left, v0, v1)


def _ssd_specs(T, rev):
    nc = T // CH
    cm = (lambda c: nc - 1 - c) if rev else (lambda c: c)
    bw = NG * NS
    return dict(
        xs=pl.BlockSpec((CH, DI), lambda c: (cm(c), 0)),
        bm=pl.BlockSpec((CH, bw), lambda c: (cm(c), DI // bw)),
        cmat=pl.BlockSpec((CH, bw), lambda c: (cm(c), DI // bw + 1)),
        xbc=pl.BlockSpec((CH, DX), lambda c: (cm(c), 0)),
        col=pl.BlockSpec((CH, LANES), lambda c: (cm(c), 0)),
        dsk=pl.BlockSpec((1, DI), lambda c: (0, 0)),
        state=pl.BlockSpec((1, NS, DI), lambda c: (cm(c), 0, 0)),
    )


def _last(ref, lo, hi):
    return ref.at[(slice(None),) * (len(ref.shape) - 1) + (slice(lo, hi),)]


def _group_views(g, wide, narrow):
    return [_last(r, g * _GW, (g + 1) * _GW) for r in wide] + [_last(r, g * NS, (g + 1) * NS) for r in narrow]


def _ssd_fwd(xact, dt, acum, acumT, dsk_rep):
    T = xact.shape[0]
    nc = T // CH
    sp = _ssd_specs(T, False)

    def body(*refs):
        xs, bm, cmat, dtr, acr, actr, dsk, y, spv, S_ref = refs

        @pl.when(pl.program_id(0) == 0)
        def _():
            S_ref[...] = jnp.zeros_like(S_ref)

        _interleave([group(g * _HG, dtr[...], acr[...], actr[...],
                           *_group_views(g, (xs, dsk, y, spv, S_ref), (bm, cmat))) for g in range(NG)])

    def group(hb, dt, ac, acT, xs_ref, dsk_ref, y_ref, sp_ref, S_ref, b_ref, c_ref):
        Bm, Cm = b_ref[...], c_ref[...]
        S = S_ref[...]
        sp_ref[0] = S
        cb = _dot(Cm, Bm, NT)
        CS = _dot(Cm, S.astype(BF16))
        row = lax.broadcasted_iota(jnp.int32, (CH, CH), 0)
        col = lax.broadcasted_iota(jnp.int32, (CH, CH), 1)
        tril = row >= col
        left = col < HP
        xd_parts, dec_parts = [], []
        for p in range(_HG // 2):
            sl = slice(p * LANES, (p + 1) * LANES)
            j0, j1 = hb + 2 * p, hb + 2 * p + 1
            xp = xs_ref[:, sl].astype(F32)
            a0, a1 = ac[:, j0:j0 + 1], ac[:, j1:j1 + 1]
            al0, al1 = ac[CH - 1:CH, j0:j0 + 1], ac[CH - 1:CH, j1:j1 + 1]
            X = xp * _pair_lanes(left, dt[:, j0:j0 + 1], dt[:, j1:j1 + 1])
            Xb = X.astype(BF16)
            Ws = [(cb * jnp.exp(jnp.where(tril, aj - acT[j:j + 1, :], _NEG))).astype(BF16)
                  for j, aj in ((j0, a0), (j1, a1))]
            Xs = [jnp.where(m, Xb, jnp.zeros_like(Xb)) for m in (left, jnp.logical_not(left))]
            yield
            yd = _dot(jnp.concatenate(Ws, axis=1), jnp.concatenate(Xs, axis=0))
            yield
            eal = _pair_lanes(left, jnp.exp(a0), jnp.exp(a1))
            y = yd + eal * CS[:, sl] + dsk_ref[:, sl] * xp
            y_ref[:, sl] = y.astype(BF16)
            xd_parts.append(X * _pair_lanes(left, jnp.exp(al0 - a0), jnp.exp(al1 - a1)))
            dec_parts.append(_pair_lanes(left[0:1], jnp.exp(al0), jnp.exp(al1)))
        Xd = jnp.concatenate(xd_parts, axis=1).astype(BF16)
        dec = jnp.concatenate(dec_parts, axis=1)
        S_ref[...] = dec * S + _dot(Bm, Xd, TN)

    return pl.pallas_call(
        body, name="ssd_fwd", grid=(nc,),
        in_specs=[sp["xs"], sp["bm"], sp["cmat"], sp["col"], sp["col"], sp["col"], sp["dsk"]],
        out_specs=[sp["xs"], sp["state"]],
        out_shape=[jax.ShapeDtypeStruct((T, DI), BF16), jax.ShapeDtypeStruct((nc, NS, DI), F32)],
        scratch_shapes=[pltpu.VMEM((NS, DI), F32)],
        compiler_params=_params(("arbitrary",)),
    )(xact, xact, xact, dt, acum, acumT, dsk_rep)


def _ssd_bwd(dy, xact, dt, acum, acumT, dsk_rep, sprev):
    T = xact.shape[0]
    nc = T // CH
    sp = _ssd_specs(T, True)

    def body(*refs):
        xs, bm, cmat, dtr, acr, actr, dsk, dyr, spv, dxa, ddtx, dAc, dskacc, dS_ref = refs
        first = pl.program_id(0) == 0

        @pl.when(first)
        def _():
            dS_ref[...] = jnp.zeros_like(dS_ref)

        dbc = _last(dxa, DI, DX)
        ddtx_sum = jnp.zeros((CH, LANES), F32)
        dAc_sum = jnp.zeros((CH, LANES), F32)
        for a, b in _interleave([group(first, g * _HG, dtr[...], acr[...], actr[...],
                                       *_group_views(g, (xs, dsk, dyr, spv, dxa, dskacc, dS_ref),
                                                     (bm, cmat, dbc, _last(dbc, NG * NS, 2 * NG * NS))))
                                 for g in range(NG)]):
            ddtx_sum, dAc_sum = ddtx_sum + a, dAc_sum + b
        ddtx[...] = ddtx_sum
        dAc[...] = dAc_sum

    def group(first, hb, dt, ac, acT, xs_ref, dsk_ref, dy_ref, sp_ref, dx_ref, dskacc_ref, dS_ref, b_ref, c_ref,
              dB_ref, dC_ref):
        Bm, Cm = b_ref[...], c_ref[...]
        S = sp_ref[0]
        dS = dS_ref[...]
        Sb, dSb = S.astype(BF16), dS.astype(BF16)
        cb = _dot(Cm, Bm, NT)
        cbT = _dot(Bm, Cm, NT)
        CmT = Cm.T
        CS = _dot(Cm, Sb)
        T1 = _dot(Bm, dSb)
        yield
        row = lax.broadcasted_iota(jnp.int32, (CH, CH), 0)
        col = lax.broadcasted_iota(jnp.int32, (CH, CH), 1)
        tril = row >= col
        triu = row <= col
        left = col < HP
        lane8 = lax.broadcasted_iota(jnp.int32, (1, LANES), 1)
        lastrow = lax.broadcasted_iota(jnp.int32, (CH, 1), 0) == CH - 1
        dCB = jnp.zeros((CH, CH), F32)
        dCBT = jnp.zeros((CH, CH), F32)
        dAc = jnp.zeros((CH, LANES), F32)
        ddtx = jnp.zeros((CH, LANES), F32)
        xd_parts, dye_parts, dec_parts, dsk_parts = [], [], [], []
        for p in range(_HG // 2):
            sl = slice(p * LANES, (p + 1) * LANES)
            j0, j1 = hb + 2 * p, hb + 2 * p + 1
            xp = xs_ref[:, sl].astype(F32)
            dyp = dy_ref[:, sl].astype(F32)
            a0, a1 = ac[:, j0:j0 + 1], ac[:, j1:j1 + 1]
            al0, al1 = ac[CH - 1:CH, j0:j0 + 1], ac[CH - 1:CH, j1:j1 + 1]
            dtl = _pair_lanes(left, dt[:, j0:j0 + 1], dt[:, j1:j1 + 1])
            X = xp * dtl
            Xb = X.astype(BF16)
            eal = _pair_lanes(left, jnp.exp(a0), jnp.exp(a1))
            dtel = _pair_lanes(left, jnp.exp(al0 - a0), jnp.exp(al1 - a1))
            T1p = T1[:, sl]
            Rm = T1p * dtel * X
            GR = dyp * (eal * CS[:, sl]) - Rm
            SdS = dS[:, sl] * S[:, sl]
            dXd = jnp.zeros((CH, LANES), F32)
            for j, aj, alj, mask in ((j0, a0, al0, left), (j1, a1, al1, jnp.logical_not(left))):
                dYm = jnp.where(mask, dyp, 0.0).astype(BF16)
                dWm = _dot(dYm, Xb, NT)
                dWmT = _dot(Xb, dYm, NT)
                yield
                e = aj - acT[j:j + 1, :]
                P = dWm * jnp.exp(jnp.where(tril, e, _NEG))
                LmT = jnp.exp(jnp.where(triu, -e, _NEG))
                PT = dWmT * LmT
                dCB = dCB + P
                dCBT = dCBT + PT
                yield
                dXd = dXd + _dot((cbT * LmT).astype(BF16), dYm)
                qd = P * cb - PT * cbT + jnp.where(mask, GR, 0.0)
                colv = jnp.sum(qd, axis=1, keepdims=True)
                tot = jnp.where(mask, Rm + jnp.exp(alj) * SdS, 0.0)
                dalast = jnp.sum(jnp.sum(tot, axis=0, keepdims=True), axis=1, keepdims=True)
                dAc = dAc + (colv + jnp.where(lastrow, dalast, 0.0)) * (lane8 == j).astype(F32)
                yield
            dX = dXd + dtel * T1p
            dXx = dX * xp
            for j, mask in ((j0, left), (j1, jnp.logical_not(left))):
                dd = jnp.sum(jnp.where(mask, dXx, 0.0), axis=1, keepdims=True)
                ddtx = ddtx + dd * (lane8 == j).astype(F32)
            dx_ref[:, sl] = (dX * dtl + dsk_ref[:, sl] * dyp).astype(BF16)
            dsk_parts.append(_colsum(dyp * xp))
            xd_parts.append(X * dtel)
            dye_parts.append(dyp * eal)
            dec_parts.append(_pair_lanes(left[0:1], jnp.exp(al0), jnp.exp(al1)))
            yield
        Xd = jnp.concatenate(xd_parts, axis=1).astype(BF16)
        dYe = jnp.concatenate(dye_parts, axis=1).astype(BF16)
        dec = jnp.concatenate(dec_parts, axis=1)
        dC_ref[...] = (_dot(dCB.astype(BF16), Bm) + _dot(dYe, Sb, NT)).astype(BF16)
        dB_ref[...] = (_dot(dCBT.astype(BF16), Cm) + _dot(Xd, dSb, NT)).astype(BF16)
        dS_ref[...] = _dot(CmT, dYe) + dec * dS
        _acc_out(dskacc_ref, jnp.concatenate(dsk_parts, axis=1), first)
        return ddtx, dAc

    return pl.pallas_call(
        body, name="ssd_bwd", grid=(nc,),
        in_specs=[sp["xs"], sp["bm"], sp["cmat"], sp["col"], sp["col"], sp["col"], sp["dsk"], sp["xs"],
                  sp["state"]],
        out_specs=[sp["xbc"], sp["col"], sp["col"], sp["dsk"]],
        out_shape=[jax.ShapeDtypeStruct((T, DX), BF16), jax.ShapeDtypeStruct((T, LANES), F32),
                   jax.ShapeDtypeStruct((T, LANES), F32), jax.ShapeDtypeStruct((1, DI), F32)],
        scratch_shapes=[pltpu.VMEM((NS, DI), F32)],
        compiler_params=_params(("arbitrary",)),
    )(xact, xact, xact, dt, acum, acumT, dsk_rep, dy, sprev)


def _gnorm_fwd(y, proj, w):
    T = y.shape[0]
    R = _Rows(T, 1024)
    zb = OFF_Z // _GW

    def body(y_ref, z_ref, w_ref, o_ref):
        z = z_ref[...].astype(F32)
        yf = y_ref[...].astype(F32) * z * _sigmoid(z)
        r = lax.rsqrt(jnp.mean(yf * yf, axis=-1, keepdims=True) + EPS)
        o_ref[...] = (yf * r * w_ref[...]).astype(BF16)

    return R.call(body, "gnorm_fwd", NG, [R.tile(_GW), R.tile(_GW, zb), R.colvec(1, _GW)], R.tile(_GW),
                  jax.ShapeDtypeStruct((T, DI), BF16), (y, proj, w))


def _gnorm_bwd(dn, y, proj, w, dproj):
    T = y.shape[0]
    R = _Rows(T, 1024)
    zb = OFF_Z // _GW

    def body(dn_ref, y_ref, z_ref, w_ref, _alias, dz_ref, dy_ref, dw_ref):
        z = z_ref[...].astype(F32)
        yv = y_ref[...].astype(F32)
        s = _sigmoid(z)
        silu = z * s
        yf = yv * silu
        r = lax.rsqrt(jnp.mean(yf * yf, axis=-1, keepdims=True) + EPS)
        yh = yf * r
        dnv = dn_ref[...].astype(F32)
        dyh = dnv * w_ref[...]
        dyf = r * (dyh - yh * jnp.mean(dyh * yh, axis=-1, keepdims=True))
        dy_ref[...] = (dyf * silu).astype(BF16)
        dz_ref[...] = (dyf * yv * s * (1.0 + z * (1.0 - s))).astype(BF16)
        _acc_out(dw_ref, _colsum(dnv * yh), pl.program_id(1) == 0)

    return R.call(
        body, "gnorm_bwd", NG, [R.tile(_GW), R.tile(_GW), R.tile(_GW, zb), R.colvec(1, _GW), ANY],
        [R.tile(_GW, zb), R.tile(_GW), R.colvec(1, _GW)],
        [jax.ShapeDtypeStruct(dproj.shape, BF16), jax.ShapeDtypeStruct((T, DI), BF16),
         jax.ShapeDtypeStruct((1, DI), F32)],
        (dn, y, proj, w, dproj), aliases={4: 0})


def _merge_fwd(proj, ya, ys):
    T = proj.shape[0]
    R = _Rows(T, 512)
    gb = OFF_G // (2 * D)

    def body(g_ref, ya_ref, ys_ref, o_ref):
        ga = _sigmoid(g_ref[:, :D].astype(F32))
        gs = _sigmoid(g_ref[:, D:].astype(F32))
        o_ref[...] = (ga * ya_ref[...].astype(F32) + gs * ys_ref[...].astype(F32)).astype(BF16)

    return R.call(body, "merge_fwd", 1, [R.tile(2 * D, gb), R.tile(D), R.tile(D)], R.tile(D),
                  jax.ShapeDtypeStruct((T, D), BF16), (proj, ya, ys))


def _merge_bwd(dm, proj, ya, ys, ncols):
    T = proj.shape[0]
    R = _Rows(T, 256)
    gb = OFF_G // (2 * D)

    def body(dm_ref, g_ref, ya_ref, ys_ref, dg_ref, dya_ref, dys_ref):
        d = dm_ref[...].astype(F32)
        ga = _sigmoid(g_ref[:, :D].astype(F32))
        gs = _sigmoid(g_ref[:, D:].astype(F32))
        dya_ref[...] = (d * ga).astype(BF16)
        dys_ref[...] = (d * gs).astype(BF16)
        dg_ref[:, :D] = (d * ya_ref[...].astype(F32) * ga * (1.0 - ga)).astype(BF16)
        dg_ref[:, D:] = (d * ys_ref[...].astype(F32) * gs * (1.0 - gs)).astype(BF16)

    return R.call(
        body, "merge_bwd", 1, [R.tile(D), R.tile(2 * D, gb), R.tile(D), R.tile(D)],
        [R.tile(2 * D, gb), R.tile(D), R.tile(D)],
        [jax.ShapeDtypeStruct((T, ncols), BF16), jax.ShapeDtypeStruct((T, D), BF16),
         jax.ShapeDtypeStruct((T, D), BF16)],
        (dm, proj, ya, ys))


_FW = 1408
_FB = FF // _FW


def _ffn_act_fwd(hv, conv_w, conv_b):
    T = hv.shape[0]
    R = _Rows(T, 256)
    tm = R.tm

    def body(h1_ref, h1p_ref, h3_ref, w_ref, b_ref, o_ref):
        keep = (pl.program_id(1) > 0).astype(F32)
        ext = jnp.concatenate([h1p_ref[...].astype(F32) * keep, h1_ref[...].astype(F32)], axis=0)
        pre = _wsum(w_ref[...], _shifts_causal(ext, 3, tm)) + b_ref[...]
        o_ref[...] = (pre * _sigmoid(pre) * h3_ref[...].astype(F32)).astype(BF16)

    return R.call(body, "ffn_act_fwd", _FB,
                  [R.tile(_FW), R.prev(_FW), R.tile(_FW, _FB), R.colvec(3, _FW), R.colvec(1, _FW)],
                  R.tile(_FW), jax.ShapeDtypeStruct((T, FF), BF16), (hv, hv, hv, conv_w, conv_b))


def _ffn_act_bwd(dg, hv, conv_w, conv_b):
    T = hv.shape[0]
    R = _Rows(T, 256)
    tm = R.tm

    def body(dg_ref, h1_ref, h1p_ref, h3_ref, w_ref, b_ref, dh3_ref, dpre_ref, dw_ref, db_ref):
        i = pl.program_id(1)
        keep = (i > 0).astype(F32)
        ext = jnp.concatenate([h1p_ref[...].astype(F32) * keep, h1_ref[...].astype(F32)], axis=0)
        sh = _shifts_causal(ext, 3, tm)
        pre = _wsum(w_ref[...], sh) + b_ref[...]
        s = _sigmoid(pre)
        d = dg_ref[...].astype(F32)
        dh3_ref[...] = (d * pre * s).astype(BF16)
        dpre = d * h3_ref[...].astype(F32) * s * (1.0 + pre * (1.0 - s))
        dpre_ref[...] = dpre.astype(BF16)
        _acc_rows(dw_ref, [_colsum(dpre * q) for q in sh], i == 0)
        _acc_out(db_ref, _colsum(dpre), i == 0)

    return R.call(
        body, "ffn_act_bwd", _FB,
        [R.tile(_FW), R.tile(_FW), R.prev(_FW), R.tile(_FW, _FB), R.colvec(3, _FW), R.colvec(1, _FW)],
        [R.tile(_FW), R.tile(_FW), R.colvec(3, _FW), R.colvec(1, _FW)],
        [jax.ShapeDtypeStruct((T, FF), BF16), jax.ShapeDtypeStruct((T, FF), BF16),
         jax.ShapeDtypeStruct((3, FF), F32), jax.ShapeDtypeStruct((1, FF), F32)],
        (dg, hv, hv, hv, conv_w, conv_b))


def _conv3_transpose(dpre, conv_w):
    T = dpre.shape[0]
    R = _Rows(T, 256)
    tm = R.tm

    def body(d_ref, dn_ref, w_ref, o_ref):
        keep = (pl.program_id(1) < R.nrow - 1).astype(F32)
        ext = jnp.concatenate([d_ref[...].astype(F32), dn_ref[...].astype(F32) * keep], axis=0)
        o_ref[...] = _wsum(w_ref[...], _shifts_anticausal(ext, 3, tm)).astype(BF16)

    return R.call(body, "ffn_conv_bwd", _FB, [R.tile(_FW), R.next(_FW), R.colvec(3, _FW)], R.tile(_FW),
                  jax.ShapeDtypeStruct((T, FF), BF16), (dpre, dpre, conv_w))


def _final_loss_epilogue(w, target):
    T = target.shape[0]

    def fn(xv, ins, outs, first):
        w_ref, t_ref = ins
        l_ref, dh_ref, dhb_ref, dw_ref = outs
        wv = w_ref[...]
        r = lax.rsqrt(jnp.mean(xv * xv, axis=-1, keepdims=True) + EPS)
        xh = xv * r
        err = xh * wv - t_ref[...]
        part = 0.5 * jnp.sum(jnp.mean(err * err, axis=-1, keepdims=True), axis=0, keepdims=True)
        _acc_out(l_ref, jnp.broadcast_to(part, l_ref.shape), first)
        dy = err * (1.0 / D)
        dxh = dy * wv
        dh = r * (dxh - xh * jnp.mean(dxh * xh, axis=-1, keepdims=True))
        dh_ref[...] = dh
        dhb_ref[...] = dh.astype(BF16)
        _acc_out(dw_ref, _colsum(dy * xh), first)

    return _Epilogue(fn, (w, target),
                     (((8, LANES), F32), ((T, D), F32), ((T, D), BF16), ((1, D), F32)), 10 * D)


def _pad_lanes(v, n=LANES):
    return jnp.pad(v, ((0, 0), (0, n - v.shape[1])))


class _Hooks:
    def before_in_proj(self, w_in):
        return w_in

    def late_weights(self, wts, after):
        return wts

    def grads_ready(self, grads, tie):
        return tie

    def mark(self, name, value):
        pass


def _local_step(x, target, wts, hooks=None):
    hooks = hooks or _Hooks()
    T = x.shape[0]
    w_in = wts["w_in"]
    dt_bias_p, a_log_p = _pad_lanes(wts["dt_bias"]), _pad_lanes(wts["a_log"])
    dsk_rep = jnp.repeat(wts["d_skip"], HP, axis=1)

    w_in = hooks.before_in_proj(w_in)
    proj, u, dt_raw = _norm_matmul(x, wts["norm_mix_w"], w_in, "norm_mm_in", w_in[:, OFF_DT:OFF_DT + LANES])
    ya_in = _branch_a_fwd(proj, wts["conv_a_w"])
    xact = _xbc_fwd(proj, wts["ssd_conv_w"], wts["ssd_conv_b"])
    dt, acum, acumT = _dt_fwd(dt_raw, dt_bias_p, a_log_p)
    y_ssd, sprev = _ssd_fwd(xact, dt, acum, acumT, dsk_rep)
    yn = _gnorm_fwd(y_ssd, proj, wts["ssd_norm_w"])
    late = hooks.late_weights(wts, yn)
    w_a_out, w_s_out, w_o, w_up, w_down = (late[k] for k in ("w_a_out", "w_s_out", "w_o", "w_up", "w_down"))
    y_a = _matmul(ya_in, w_a_out, mode="nn", out_dtype=BF16, name="mm_a_out")
    y_s = _matmul(yn, w_s_out, mode="nn", out_dtype=BF16, name="mm_s_out")
    merged = _merge_fwd(proj, y_a, y_s)
    h1 = _matmul(merged, w_o, mode="nn", out_dtype=F32, name="mm_o", residual=x)
    hv, v = _norm_matmul(h1, wts["norm_ffn_w"], w_up, "norm_mm_up")
    gact = _ffn_act_fwd(hv, wts["ffn_conv_w"], wts["ffn_conv_b"])
    loss, dh2, dh2b, g_final = _matmul(gact, w_down, mode="nn", out_dtype=F32, name="mm_down_loss", residual=h1,
                                       epilogue=_final_loss_epilogue(wts["final_norm_w"], target))

    grads = {"final_norm_w": g_final}
    grads["w_down"] = _matmul(gact, dh2b, mode="tn", out_dtype=F32, name="mm_down_dw")
    dgact = _matmul(dh2b, w_down, mode="nt", out_dtype=BF16, name="mm_down_dx")
    dh3, dpre, grads["ffn_conv_w"], grads["ffn_conv_b"] = _ffn_act_bwd(dgact, hv, wts["ffn_conv_w"], wts["ffn_conv_b"])
    dh1c = _conv3_transpose(dpre, wts["ffn_conv_w"])
    grads["w_up"] = (_matmul(v, dh1c, mode="tn", out_dtype=F32, name="mm_up_dw1"),
                     _matmul(v, dh3, mode="tn", out_dtype=F32, name="mm_up_dw3"))
    dv = _matmul(dh1c, w_up, mode="nt", out_dtype=F32, name="mm_up_dx1")
    dh1, dh1b, grads["norm_ffn_w"] = _matmul(
        dh3, w_up, mode="nt", out_dtype=F32, name="mm_up_dx3_norm", residual=dv, b_k_off=FF,
        epilogue=_rmsnorm_bwd_epilogue(h1, wts["norm_ffn_w"], dh2))
    grads["w_o"] = _matmul(merged, dh1b, mode="tn", out_dtype=F32, name="mm_o_dw")
    dmerged = _matmul(dh1b, w_o, mode="nt", out_dtype=BF16, name="mm_o_dx")
    dproj, dya, dys = _merge_bwd(dmerged, proj, y_a, y_s, NIP)
    grads["w_a_out"] = _matmul(ya_in, dya, mode="tn", out_dtype=F32, name="mm_a_out_dw")
    dya_in = _matmul(dya, w_a_out, mode="nt", out_dtype=BF16, name="mm_a_out_dx")
    dproj, grads["conv_a_w"] = _branch_a_bwd(dya_in, proj, wts["conv_a_w"], dproj)
    grads["w_s_out"] = _matmul(yn, dys, mode="tn", out_dtype=F32, name="mm_s_out_dw")
    dys = hooks.grads_ready({k: grads[k] for k in ("w_a_out", "w_s_out", "w_o", "w_up", "w_down")}, dys)
    dyn =_matmul(dys, w_s_out, mode="nt", out_dtype=BF16, name="mm_s_out_dx")
    dproj, dy_ssd, grads["ssd_norm_w"] = _gnorm_bwd(dyn, y_ssd, proj, wts["ssd_norm_w"], dproj)
    dxact, ddt_x, dacum, dskl = _ssd_bwd(dy_ssd, xact, dt, acum, acumT, dsk_rep, sprev)
    hooks.mark("ssd_bwd", dxact)
    grads["d_skip"] = dskl.reshape(NH, HP).sum(axis=1).reshape(1, NH)
    dproj, grads["ssd_conv_w"], grads["ssd_conv_b"] = _xbc_bwd(dxact, proj, wts["ssd_conv_w"], wts["ssd_conv_b"], dproj)
    dproj, g_dtb, g_alog = _dt_bwd(dacum, ddt_x, dt_raw, dt_bias_p, a_log_p, dproj)
    grads["dt_bias"], grads["a_log"] = g_dtb[:, :NH], g_alog[:, :NH]
    grads["w_in"] = _matmul(u, dproj, mode="tn", out_dtype=F32, name="mm_in_dw")
    dproj = hooks.grads_ready({"w_in": grads["w_in"]}, dproj)
    grad_x, _, grads["norm_mix_w"] = _matmul(dproj, w_in, mode="nt", out_dtype=F32, name="mm_in_dx_norm",
                                             epilogue=_rmsnorm_bwd_epilogue(x, wts["norm_mix_w"], dh1))
    return loss, grad_x, grads


def _permute_w_in(w):
    out = jnp.zeros((w.shape[0], NIP), w.dtype)
    for o, n, no in _SEGS:
        out = lax.dynamic_update_slice(out, w[:, o:o + n], (0, no))
    return out


def _unpermute_w_in(g):
    order = sorted(_SEGS)
    return jnp.concatenate([g[:, no:no + n] for o, n, no in order], axis=1)


MESH = pl.DeviceIdType.MESH
NCHIP = 4
NDEV = 8

_W_IN = (("w_in", D, NI // NCHIP, 1),)
_W_REST = (("w_a_out", D // NCHIP, D, 0), ("w_s_out", DI // NCHIP, D, 0), ("w_o", D // NCHIP, D, 0),
           ("w_up", D, 2 * FF // NCHIP, 1), ("w_down", FF // NCHIP, D, 0))


def _slab_rows(group):
    rows = [r * c // LANES for _, r, c, _ in group]
    assert all(n % 32 == 0 for n in rows), rows
    return rows


def _coords():
    return lax.axis_index("x"), lax.axis_index("y"), lax.axis_index("c")


def _other_chips(x, y):
    return [(1 - x, y), (x, 1 - y), (1 - x, 1 - y)]


def _ag_weights(shard):
    nrows = shard.shape[0]
    hr = nrows // 2

    def body(x_ref, out_ref, send_sems, recv_sems, local_sem):
        x, y, c = _coords()
        me = 2 * x + y
        chips = _other_chips(x, y)

        def rows(s, h):
            return out_ref.at[s, pl.ds(h * hr, hr), :]

        def copy(k, s, h, to, src=None):
            return pltpu.make_async_remote_copy(
                src_ref=rows(s, h) if src is None else src, dst_ref=rows(s, h),
                send_sem=send_sems.at[k], recv_sem=recv_sems.at[k], device_id=to, device_id_type=MESH)

        mine = pltpu.make_async_copy(x_ref, out_ref.at[me], local_sem)
        mine.start()
        first = [copy(k, me, c, (*chip, c), src=x_ref.at[pl.ds(c * hr, hr), :]) for k, chip in enumerate(chips)]
        for cp in first:
            cp.start()
        passed = []
        for k, chip in enumerate(chips):
            s = 2 * chip[0] + chip[1]
            copy(k, s, c, (x, y, c)).wait_recv()
            fwd = copy(3 + k, s, c, (x, y, 1 - c))
            fwd.start()
            passed.append(fwd)
        for k, chip in enumerate(chips):
            copy(3 + k, 2 * chip[0] + chip[1], 1 - c, (x, y, c)).wait_recv()
        for cp in first + passed:
            cp.wait_send()
        mine.wait()

    return pl.pallas_call(
        body, name="ag_weights", in_specs=[ANY], out_specs=ANY,
        out_shape=jax.ShapeDtypeStruct((NCHIP,) + shard.shape, shard.dtype),
        scratch_shapes=[pltpu.SemaphoreType.DMA((6,)), pltpu.SemaphoreType.DMA((6,)), pltpu.SemaphoreType.DMA],
        compiler_params=pltpu.CompilerParams(has_side_effects=True),
    )(shard)


HBM = pl.BlockSpec(memory_space=pltpu.HBM)
SEM = pl.BlockSpec(memory_space=pltpu.SEMAPHORE)
_EFFECT = pltpu.SideEffectType.DATAFLOW_SIDE_EFFECTING
_NCOPY = NCHIP - 1


def _plan_bcast(src_ref, land_ref, send_sems, recv_sems):
    x, y, c = _coords()
    sends, lands = [], []
    for k, chip in enumerate(_other_chips(x, y)):
        def copy(slot):
            return pltpu.make_async_remote_copy(
                src_ref=src_ref, dst_ref=land_ref.at[slot], send_sem=send_sems.at[k], recv_sem=recv_sems.at[k],
                device_id=(*chip, c), device_id_type=MESH)
        sends.append(copy(2 * x + y))
        lands.append(copy(2 * chip[0] + chip[1]))
    return sends, lands


def _plan_scatter(src_ref, land_ref, send_sems, recv_sems):
    x, y, c = _coords()
    cps = [pltpu.make_async_remote_copy(
        src_ref=src_ref.at[2 * chip[0] + chip[1]], dst_ref=land_ref.at[k], send_sem=send_sems.at[k],
        recv_sem=recv_sems.at[k], device_id=(*chip, c), device_id_type=MESH)
        for k, chip in enumerate(_other_chips(x, y))]
    return cps, cps


def _split_start(name, src, land, plan):
    def body(src_ref, land_ref, send_sems, recv_sems, src_thru, land_thru, token):
        for cp in plan(src_ref, land_ref, send_sems, recv_sems)[0]:
            cp.start()
        token[...] = jnp.zeros_like(token)

    send_sems, recv_sems, src_thru, land_thru, token = pl.pallas_call(
        body, name=name,
        out_shape=(pltpu.SemaphoreType.DMA((_NCOPY,)), pltpu.SemaphoreType.DMA((_NCOPY,)),
                   pltpu.HBM(src.shape, src.dtype), pltpu.HBM(land.shape, land.dtype),
                   jax.ShapeDtypeStruct((8, LANES), F32)),
        in_specs=(HBM, HBM), out_specs=(SEM, SEM, HBM, HBM, pl.BlockSpec(memory_space=pltpu.VMEM)),
        input_output_aliases={0: 2, 1: 3},
        compiler_params=pltpu.CompilerParams(has_side_effects=_EFFECT),
    )(pltpu.with_memory_space_constraint(src, pltpu.HBM), pltpu.with_memory_space_constraint(land, pltpu.HBM))
    return (send_sems, recv_sems, src_thru, land_thru), token


def _split_wait(name, handle, after, plan):
    send_sems, recv_sems, src_thru, land_thru = handle

    def body(src_ref, land_ref, send_sems, recv_sems, after_ref, src_out, land_out):
        sends, lands = plan(src_ref, land_ref, send_sems, recv_sems)
        for cp in sends:
            cp.wait_send()
        for cp in lands:
            cp.wait_recv()

    return pl.pallas_call(
        body, name=name,
        out_shape=(pltpu.HBM(src_thru.shape, src_thru.dtype), pltpu.HBM(land_thru.shape, land_thru.dtype)),
        in_specs=(HBM, HBM, SEM, SEM, ANY), out_specs=(HBM, HBM), input_output_aliases={0: 0, 1: 1},
        compiler_params=pltpu.CompilerParams(has_side_effects=_EFFECT),
    )(src_thru, land_thru, send_sems, recv_sems, after)


def _tie(x, token, name):
    def body(x_ref, t_ref, o_ref):
        pass

    return pl.pallas_call(
        body, name=name, in_specs=[ANY, pl.BlockSpec(memory_space=pltpu.VMEM)], out_specs=ANY,
        out_shape=jax.ShapeDtypeStruct(x.shape, x.dtype), input_output_aliases={0: 0},
    )(x, token)


def _swap_sibling(p, name):
    def body(p_ref, land_ref, send_sem, recv_sem):
        x, y, c = _coords()
        cp = pltpu.make_async_remote_copy(
            src_ref=p_ref, dst_ref=land_ref, send_sem=send_sem, recv_sem=recv_sem,
            device_id=(x, y, 1 - c), device_id_type=MESH)
        cp.start()
        cp.wait()

    return pl.pallas_call(
        body, name=name, in_specs=[ANY], out_specs=ANY, out_shape=jax.ShapeDtypeStruct(p.shape, p.dtype),
        scratch_shapes=[pltpu.SemaphoreType.DMA, pltpu.SemaphoreType.DMA],
        compiler_params=pltpu.CompilerParams(has_side_effects=True),
    )(p)


_ADD_BYTES = 7 << 19


def _add_tile(rows, cols):
    best = 32
    for t in range(32, rows + 1, 32):
        if rows % t == 0 and t * cols * 4 <= _ADD_BYTES:
            best = t
    return best


def _add_slabs(pack, land, me, name):
    rows, cols = pack.shape[1:]
    tr = _add_tile(rows, cols)

    def body(me_ref, p_ref, l_ref, o_ref):
        f = lambda r: r.astype(F32)
        o_ref[...] = ((f(p_ref[0]) + f(l_ref[0])) + f(l_ref[1])) + f(l_ref[2])

    return pl.pallas_call(
        body, name=name,
        grid_spec=pltpu.PrefetchScalarGridSpec(
            num_scalar_prefetch=1, grid=(rows // tr,),
            in_specs=[pl.BlockSpec((1, tr, cols), lambda i, me_ref: (me_ref[0], i, 0)),
                      pl.BlockSpec((_NCOPY, tr, cols), lambda i, me_ref: (0, i, 0))],
            out_specs=pl.BlockSpec((tr, cols), lambda i, me_ref: (i, 0))),
        out_shape=jax.ShapeDtypeStruct((rows, cols), F32),
        compiler_params=_params(("parallel",)),
    )(me, pack, land)


def _add_pair(a, b, name):
    rows, cols = a.shape
    tr = _add_tile(rows, cols)

    def body(a_ref, b_ref, o_ref):
        o_ref[...] = a_ref[...] + b_ref[...]

    blk = pl.BlockSpec((tr, cols), lambda i: (i, 0))
    return pl.pallas_call(
        body, name=name, grid=(rows // tr,), in_specs=[blk, blk], out_specs=blk,
        out_shape=jax.ShapeDtypeStruct((rows, cols), F32), compiler_params=_params(("parallel",)),
    )(a, b)


_STAGE_W = 1024


def _stage_rows(shapes):
    pieces, r = [], 0
    for i, (k, w) in enumerate(shapes):
        for a in range(k):
            for q in range(0, w, _STAGE_W):
                pieces.append((i, a, q, min(_STAGE_W, w - q), r))
                r += 1
    return pieces, -(-r // 8) * 8


def _gather8(parts, reduce, name):
    shapes = [p.shape for p in parts]
    pieces, rows = _stage_rows(shapes)
    n = len(parts)

    def body(*refs):
        ins, outs = refs[:n], refs[n:2 * n]
        stage, buf, res, send_sems, recv_sems = refs[2 * n:]
        x, y, c = _coords()
        me = 4 * x + 2 * y + c
        stage[...] = jnp.zeros_like(stage)
        for i, a, q, w, r in pieces:
            stage[r:r + 1, 0:w] = ins[i][a:a + 1, q:q + w]
        buf[pl.ds(me, 1)] = stage[...][None]
        cps, lands = [], []
        for k in range(1, NDEV):
            peer = (1 - x if k & 4 else x, 1 - y if k & 2 else y, 1 - c if k & 1 else c)

            def copy(slot):
                return pltpu.make_async_remote_copy(
                    src_ref=stage, dst_ref=buf.at[slot], send_sem=send_sems.at[k - 1],
                    recv_sem=recv_sems.at[k - 1], device_id=peer, device_id_type=MESH)

            cps.append(copy(me))
            lands.append(copy(4 * peer[0] + 2 * peer[1] + peer[2]))
        for cp in cps:
            cp.start()
        for cp, land in zip(cps, lands):
            land.wait_recv()
            cp.wait_send()
        if reduce:
            acc = buf[0]
            for d in range(1, NDEV):
                acc = acc + buf[d]
            res[...] = acc
            for i, a, q, w, r in pieces:
                outs[i][a:a + 1, q:q + w] = res[r:r + 1, 0:w]
        else:
            for i, a, q, w, r in pieces:
                for s in range(NCHIP):
                    outs[i][s, a:a + 1, q:q + w] = buf[2 * s, r:r + 1, 0:w]

    vm = pl.BlockSpec(memory_space=pltpu.VMEM)
    out_shapes = [jax.ShapeDtypeStruct(s if reduce else (NCHIP,) + s, F32) for s in shapes]
    return pl.pallas_call(
        body, name=name, in_specs=[vm] * n, out_specs=[vm] * n, out_shape=out_shapes,
        scratch_shapes=[pltpu.VMEM((rows, _STAGE_W), F32), pltpu.VMEM((NDEV, rows, _STAGE_W), F32),
                        pltpu.VMEM((rows, _STAGE_W), F32), pltpu.SemaphoreType.DMA((NDEV - 1,)),
                        pltpu.SemaphoreType.DMA((NDEV - 1,))],
        compiler_params=pltpu.CompilerParams(has_side_effects=True),
    )(*parts)


def _adamw_update(w_ref, g_ref, m_ref, v_ref, d_ref, mo_ref, vo_ref):
    c1 = 1.0 / (1.0 - ADAM_B1 ** ADAM_STEP)
    c2 = 1.0 / (1.0 - ADAM_B2 ** ADAM_STEP)
    gv = g_ref[...]
    mn = ADAM_B1 * m_ref[...] + (1.0 - ADAM_B1) * gv
    vn = ADAM_B2 * v_ref[...] + (1.0 - ADAM_B2) * (gv * gv)
    d_ref[...] = -ADAM_LR * ((mn * c1) / (jnp.sqrt(vn * c2) + ADAM_EPS) + ADAM_WD * w_ref[...])
    mo_ref[...] = mn
    vo_ref[...] = vn


def _adamw_small(ws, gs, ms, vs):
    n = len(ws)

    def body(*refs):
        for i in range(n):
            _adamw_update(*(refs[j * n + i] for j in range(7)))

    vm = pl.BlockSpec(memory_space=pltpu.VMEM)
    outs = pl.pallas_call(
        body, name="adamw_small", in_specs=[vm] * (4 * n), out_specs=[vm] * (3 * n),
        out_shape=[jax.ShapeDtypeStruct(w.shape, F32) for w in ws] * 3,
    )(*ws, *gs, *ms, *vs)
    return outs[:n], outs[n:2 * n], outs[2 * n:]


def _adamw(w, g, m, v, name):
    rows, cols = w.shape
    tr = rows
    while tr * cols * 4 > (3 << 19) and tr % 16 == 0:
        tr //= 2

    def body(*refs):
        _adamw_update(*refs)

    blk = pl.BlockSpec((tr, cols), lambda i: (i, 0))
    return pl.pallas_call(
        body, name=name, grid=(rows // tr,), in_specs=[blk] * 4, out_specs=[blk] * 3,
        out_shape=[jax.ShapeDtypeStruct((rows, cols), F32)] * 3, compiler_params=_params(("parallel",)),
    )(w, g, m, v)


def _rows128(a, mult=8):
    flat = a.reshape(-1)
    n = -(-flat.shape[0] // (LANES * mult)) * LANES * mult
    return jnp.pad(flat, (0, n - flat.shape[0])).reshape(-1, LANES)


def _pack_rows(parts, total_rows):
    rows = sum(p.shape[0] for p in parts)
    if total_rows > rows:
        parts = list(parts) + [jnp.zeros((total_rows - rows, LANES), parts[0].dtype)]
    return jnp.concatenate(parts, axis=0)


def _unpack_rows(pack, shapes, mult=8):
    out, r = [], 0
    for shp in shapes:
        n = int(np.prod(shp))
        nr = -(-n // (LANES * mult)) * mult
        out.append(pack[r:r + nr].reshape(-1)[:n].reshape(shp))
        r += nr
    return out


def _unpack_full(full, group):
    out, r = {}, 0
    for (name, rr, cc, axis), nr in zip(group, _slab_rows(group)):
        seg = full[:, r:r + nr].reshape(NCHIP, rr, cc)
        out[name] = seg.reshape(NCHIP * rr, cc) if axis == 0 else seg.transpose(1, 0, 2).reshape(rr, NCHIP * cc)
        r += nr
    return out


def _by_chip(g, rr, cc, axis):
    if isinstance(g, tuple):
        n = NCHIP // len(g)
        return jnp.concatenate([h.reshape(rr, n, cc).transpose(1, 0, 2) for h in g], axis=0)
    return g.reshape(NCHIP, rr, cc) if axis == 0 else g.reshape(rr, NCHIP, cc).transpose(1, 0, 2)


def _pack_by_chip(grads, group, dtype):
    parts = [_by_chip(jax.tree.map(lambda t: t.astype(dtype), grads[name]), rr, cc, axis).reshape(NCHIP, nr, LANES)
             for (name, rr, cc, axis), nr in zip(group, _slab_rows(group))]
    return jnp.concatenate(parts, axis=1)


_SMALL_REPL = ("norm_mix_w", "ssd_conv_b", "dt_bias", "a_log", "d_skip", "ssd_norm_w", "norm_ffn_w",
               "ffn_conv_b", "final_norm_w")
_SMALL_CONV = (("conv_a_w", 3, D), ("ssd_conv_w", 4, DX), ("ffn_conv_w", 3, FF))


def kernel(x, norm_mix_w, w_in, conv_a_w, w_a_out, ssd_conv_w, ssd_conv_b, dt_bias, a_log, d_skip, ssd_norm_w, w_s_out, w_o, norm_ffn_w, w_up, ffn_conv_w, ffn_conv_b, w_down, final_norm_w, loss_target, m_norm_mix_w, m_w_in, m_conv_a_w, m_w_a_out, m_ssd_conv_w, m_ssd_conv_b, m_dt_bias, m_a_log, m_d_skip, m_ssd_norm_w, m_w_s_out, m_w_o, m_norm_ffn_w, m_w_up, m_ffn_conv_w, m_ffn_conv_b, m_w_down, m_final_norm_w, v_norm_mix_w, v_w_in, v_conv_a_w, v_w_a_out, v_ssd_conv_w, v_ssd_conv_b, v_dt_bias, v_a_log, v_d_skip, v_ssd_norm_w, v_w_s_out, v_w_o, v_norm_ffn_w, v_w_up, v_ffn_conv_w, v_ffn_conv_b, v_w_down, v_final_norm_w):
    names = ("norm_mix_w", "w_in", "conv_a_w", "w_a_out", "ssd_conv_w", "ssd_conv_b", "dt_bias", "a_log", "d_skip",
             "ssd_norm_w", "w_s_out", "w_o", "norm_ffn_w", "w_up", "ffn_conv_w", "ffn_conv_b", "w_down", "final_norm_w")
    W = dict(zip(names, (norm_mix_w, w_in, conv_a_w, w_a_out, ssd_conv_w, ssd_conv_b, dt_bias, a_log, d_skip,
                         ssd_norm_w, w_s_out, w_o, norm_ffn_w, w_up, ffn_conv_w, ffn_conv_b, w_down, final_norm_w)))
    M = dict(zip(names, (m_norm_mix_w, m_w_in, m_conv_a_w, m_w_a_out, m_ssd_conv_w, m_ssd_conv_b, m_dt_bias, m_a_log,
                         m_d_skip, m_ssd_norm_w, m_w_s_out, m_w_o, m_norm_ffn_w, m_w_up, m_ffn_conv_w, m_ffn_conv_b,
                         m_w_down, m_final_norm_w)))
    V = dict(zip(names, (v_norm_mix_w, v_w_in, v_conv_a_w, v_w_a_out, v_ssd_conv_w, v_ssd_conv_b, v_dt_bias, v_a_log,
                         v_d_skip, v_ssd_norm_w, v_w_s_out, v_w_o, v_norm_ffn_w, v_w_up, v_ffn_conv_w, v_ffn_conv_b,
                         v_w_down, v_final_norm_w)))
    two_d = lambda a: a.reshape(-1, a.shape[-1])
    W2, M2, V2 = ({k: two_d(a) for k, a in t.items()} for t in (W, M, V))
    xi, yi, ci = _coords()
    me = 2 * xi + yi

    meidx = me.reshape(1).astype(jnp.int32)
    state = {}


    class Hooks(_Hooks):
        def before_in_proj(self, w_in):
            return _tie(w_in, state["rest_token"], "tie_ag_rest")

        def late_weights(self, wts, after):
            own, land = _split_wait("ag_rest_wait", state["rest"], after, _plan_bcast)
            full = _unpack_full(lax.dynamic_update_slice(land, own[None], (me, 0, 0)), _W_REST)
            return {**wts, **full}

        def grads_ready(self, grads, tie):
            if "w_in" in grads:
                key = "g_in"
                pack = _by_chip(_unpermute_w_in(grads["w_in"]).astype(BF16), *_W_IN[0][1:])
            else:
                key, pack = "g_rest", _pack_by_chip(grads, _W_REST, BF16)
            land = lax.empty((_NCOPY,) + pack.shape[1:], BF16)
            state[key], token = _split_start("rs_" + key + "_start", pack, land, _plan_scatter)
            return _tie(tie, token, "tie_" + key)

        def mark(self, name, value):
            state[name] = value

    def reduced(key, after):
        pack, land = _split_wait("rs_" + key + "_wait", state[key], after, _plan_scatter)
        mine = _add_slabs(pack, land, meidx, "rs_" + key + "_add_chips")
        return _add_pair(mine, _swap_sibling(mine, "rs_" + key + "_swap"), "rs_" + key + "_add_cores")

    w_in_full = _ag_weights(W2["w_in"].astype(BF16)).transpose(1, 0, 2).reshape(D, NI)
    wts = {k: W2[k] for k in _SMALL_REPL}
    conv_by_chip = _gather8([W2[n] for n, *_ in _SMALL_CONV], False, "ag_conv_weights")
    for (n, kk, width), stacked in zip(_SMALL_CONV, conv_by_chip):
        wts[n] = stacked.transpose(1, 0, 2).reshape(kk, width)
    rest_slab = _tie(_pack_rows([_rows128(W2[n], 16) for n, *_ in _W_REST], 0).astype(BF16), conv_by_chip[0],
                     "tie_ag_order")
    state["rest"], state["rest_token"] = _split_start(
        "ag_rest_start", rest_slab, lax.empty((NCHIP,) + rest_slab.shape, BF16), _plan_bcast)
    wts["w_in"] = _permute_w_in(w_in_full)

    loss8, grad_x, grads = _local_step(x[0], loss_target[0], wts, Hooks())

    gbig = dict(zip([n for n, *_ in _W_REST],
                    _unpack_rows(reduced("g_rest", state["ssd_bwd"]), [(rr, cc) for _, rr, cc, _ in _W_REST], 16)))
    gbig["w_in"] = reduced("g_in", grad_x)

    small_parts = [grads[n] for n in _SMALL_REPL] + [loss8[0:1]] + [grads[n] for n, *_ in _SMALL_CONV]
    small_g = _gather8(small_parts, True, "allreduce_small")
    gsm = dict(zip(_SMALL_REPL, small_g[:len(_SMALL_REPL)]))
    loss = small_g[len(_SMALL_REPL)][0, 0]
    for (n, kk, width), gfull in zip(_SMALL_CONV, small_g[len(_SMALL_REPL) + 1:]):
        cw = width // NCHIP
        gsm[n] = lax.dynamic_slice(gfull, (0, me * cw), (kk, cw))

    G, DW, NM, NV = {}, {}, {}, {}
    for n in [b[0] for b in _W_IN + _W_REST]:
        G[n] = gbig[n]
        DW[n], NM[n], NV[n] = _adamw(W2[n], G[n], M2[n], V2[n], "adamw_" + n)
    sm_names = list(_SMALL_REPL) + [n for n, *_ in _SMALL_CONV]
    outs = _adamw_small(*([t[n] for n in sm_names] for t in (W2, gsm, M2, V2)))
    for t, vals in zip((DW, NM, NV), outs):
        t.update(zip(sm_names, vals))
    G.update(gsm)

    def shaped(t):
        return [t[n].reshape(W[n].shape) for n in names]

    return (loss, grad_x.reshape(x.shape), *shaped(G), *shaped(DW), *shaped(NM), *shaped(NV))
```

```python
import functools

import jax
import jax.numpy as jnp
import numpy as np
from jax import lax
from jax.experimental import pallas as pl
from jax.experimental.pallas import tpu as pltpu

F32 = jnp.float32
BF16 = jnp.bfloat16

D = 1024
DI = 2048
NH = 32
HP = 64
NG = 4
NS = 128
CH = 128
DX = 3072
FF = 2816
NI = 10272
EPS = 1e-5

OFF_BCV, OFF_XBC, OFF_G, OFF_Z, OFF_DT = 0, 3072, 6144, 8192, 10240
NIP = 10752
_SEGS = ((0, 2048, OFF_G), (2048, 3072, OFF_BCV), (5120, 2048, OFF_Z), (7168, 3072, OFF_XBC), (10240, 32, OFF_DT))

LANES = 128
HALO = 16
V7X_VMEM_LIMIT = 56 * 2 ** 20

ADAM_LR, ADAM_B1, ADAM_B2, ADAM_EPS, ADAM_WD, ADAM_STEP = 0.001, 0.9, 0.999, 1e-08, 0.01, 10

NN = (((1,), (0,)), ((), ()))
NT = (((1,), (1,)), ((), ()))
TN = (((0,), (0,)), ((), ()))


def _dot(a, b, dims=NN):
    return lax.dot_general(a, b, dims, preferred_element_type=F32)


def _params(sem, **kw):
    return pltpu.CompilerParams(dimension_semantics=sem, vmem_limit_bytes=V7X_VMEM_LIMIT, **kw)


V7X_MXU = 256
V7X_HBM_BYTES_PER_S = 3.5e12
STEP_S = 0.35e-6
MATMUL_VMEM = 40 * 2 ** 20
EPILOGUE_VMEM = 46 * 2 ** 20


ACC_BYTES_PER_S = 1.2e13


def _divisors(dim, cap, units):
    for unit in units:
        c = [t for t in range(unit, min(dim, cap) + 1, unit) if dim % t == 0]
        if c:
            return c
    return [dim]


def _tiles(M, N, K, out_bytes, has_res):
    best = None
    for tn in _divisors(N, 2816, (V7X_MXU, LANES)):
        for tm in _divisors(M, 2816, (LANES,)):
            for tk in _divisors(K, 2816, (V7X_MXU, LANES)):
                nk, ni, nj = K // tk, M // tm, N // tn
                vmem = 4 * (tm * tk + tk * tn) + 2 * tm * tn * out_bytes
                vmem += (4 * tm * tn if nk > 1 else 0) + (8 * tm * tn if has_res else 0)
                if vmem > MATMUL_VMEM:
                    continue
                a_reads = M * K * 2 * (nj if nk > 1 else 1)
                b_reads = K * N * 2 * (ni if nk * nj > 1 else 1)
                cost = (a_reads + b_reads + M * N * out_bytes) / V7X_HBM_BYTES_PER_S + ni * nj * nk * STEP_S
                cost += (nk - 1) * M * N * 8 / ACC_BYTES_PER_S
                if best is None or cost < best[0]:
                    best = (cost, tm, tn, tk)
    assert best is not None, (M, N, K)
    return best[1:]


def _sigmoid(x):
    return 1.0 / (1.0 + jnp.exp(-x))


class _Epilogue:
    def __init__(self, fn, ins, outs, tile_bytes):
        self.fn, self.ins, self.outs, self.tile_bytes = fn, tuple(ins), tuple(outs), tile_bytes


def _matmul(a, b, *, mode, out_dtype, name, residual=None, b_k_off=0, epilogue=None):
    if mode == "nn":
        (M, K), (K2, N) = a.shape, b.shape
    elif mode == "nt":
        (M, K), (N, K2) = a.shape, (b.shape[0], a.shape[1])
        assert b_k_off + K <= b.shape[1]
    else:
        (K, M), (K2, N) = a.shape, b.shape
    assert K == K2, (name, a.shape, b.shape)
    tm, tn, tk = _tiles(M, N, K, jnp.dtype(out_dtype).itemsize, residual is not None)
    if epilogue is not None:
        tn = N
        fits = [(K * N * 2 * (M // t) / V7X_HBM_BYTES_PER_S + (K // q - 1) * M * N * 8 / ACC_BYTES_PER_S
                 + (M // t) * (K // q) * STEP_S, t, q)
                for t in (1024, 512, 256) if M % t == 0 for q in _divisors(K, 2816, (V7X_MXU, LANES))
                if 4 * (t * q + q * tn) + (4 * t * tn if K > q else 0) + (8 * t * tn if residual is not None else 0)
                + 2 * t * epilogue.tile_bytes <= EPILOGUE_VMEM]
        _, tm, tk = min(fits)
    nk = K // tk
    if mode == "tn":
        a_spec = pl.BlockSpec((tk, tm), lambda i, j, k: (k, i))
    else:
        a_spec = pl.BlockSpec((tm, tk), lambda i, j, k: (i, k))
    if mode == "nt":
        assert b_k_off % tk == 0
        b_spec = pl.BlockSpec((tn, tk), lambda i, j, k: (j, k + b_k_off // tk))
    else:
        b_spec = pl.BlockSpec((tk, tn), lambda i, j, k: (k, j))
    dims = {"nn": NN, "nt": NT, "tn": TN}[mode]
    o_spec = pl.BlockSpec((tm, tn), lambda i, j, k: (i, j))
    has_res = residual is not None

    def rows_or_whole(shape):
        if shape[0] == M:
            return pl.BlockSpec((tm,) + tuple(shape[1:]), lambda i, j, k: (i,) + (0,) * (len(shape) - 1))
        return pl.BlockSpec(tuple(shape), lambda i, j, k: (0,) * len(shape))

    n_in = 2 + has_res + (len(epilogue.ins) if epilogue else 0)
    n_out = len(epilogue.outs) if epilogue else 1

    def body(*refs):
        a_ref, b_ref = refs[:2]
        r_ref = refs[2] if has_res else None
        out_refs = refs[n_in:n_in + n_out]
        acc_ref = refs[-1]
        k = pl.program_id(2)
        part = _dot(a_ref[...], b_ref[...], dims)

        def finish(r):
            if has_res:
                r = r + r_ref[...].astype(F32)
            if epilogue is None:
                out_refs[0][...] = r.astype(out_dtype)
            else:
                epilogue.fn(r, refs[2 + has_res:n_in], out_refs, pl.program_id(0) == 0)

        if nk == 1:
            finish(part)
            return

        @pl.when(k == 0)
        def _():
            acc_ref[...] = part

        @pl.when(jnp.logical_and(k > 0, k < nk - 1))
        def _():
            acc_ref[...] += part

        @pl.when(k == nk - 1)
        def _():
            finish(acc_ref[...] + part)

    in_specs = [a_spec, b_spec] + ([o_spec] if has_res else [])
    args = (a, b) + ((residual,) if has_res else ())
    if epilogue is None:
        out_specs, out_shape = o_spec, jax.ShapeDtypeStruct((M, N), out_dtype)
        sem = ("parallel", "parallel", "arbitrary")
    else:
        in_specs += [rows_or_whole(x.shape) for x in epilogue.ins]
        args += epilogue.ins
        out_specs = [rows_or_whole(shp) for shp, _ in epilogue.outs]
        out_shape = [jax.ShapeDtypeStruct(shp, dt) for shp, dt in epilogue.outs]
        sem = ("arbitrary", "arbitrary", "arbitrary")
    return pl.pallas_call(
        body, name=name, grid=(M // tm, N // tn, nk), in_specs=in_specs, out_specs=out_specs,
        out_shape=out_shape, scratch_shapes=[pltpu.VMEM((tm, tn), F32)] if nk > 1 else [],
        compiler_params=_params(sem),
    )(*args)


class _Rows:
    def __init__(self, T, tm):
        self.T, self.tm = T, min(tm, T // 2)
        self.nrow = T // self.tm
        self.r = self.tm // HALO
        self.nb = T // HALO

    def tile(self, w, cb=0, step=1):
        return pl.BlockSpec((self.tm, w), lambda j, i: (i, cb + step * j))

    def prev(self, w, cb=0, step=1):
        r = self.r
        return pl.BlockSpec((HALO, w), lambda j, i: (jnp.maximum(i * r - 1, 0), cb + step * j))

    def next(self, w, cb=0, step=1):
        r, nb = self.r, self.nb
        return pl.BlockSpec((HALO, w), lambda j, i: (jnp.minimum((i + 1) * r, nb - 1), cb + step * j))

    def colvec(self, k, w, cb=0, step=1):
        return pl.BlockSpec((k, w), lambda j, i: (0, cb + step * j))

    def call(self, body, name, ncol, in_specs, out_specs, out_shape, args, aliases=None):
        return pl.pallas_call(
            body, name=name, grid=(ncol, self.nrow), in_specs=in_specs, out_specs=out_specs,
            out_shape=out_shape, input_output_aliases=aliases or {},
            compiler_params=_params(("parallel", "arbitrary")),
        )(*args)


ANY = pl.BlockSpec(memory_space=pl.ANY)


def _shifts_causal(ext, nk, tm):
    out = []
    for k in range(nk):
        s = nk - 1 - k
        r = ext if s == 0 else pltpu.roll(ext, s, 0)
        out.append(r[HALO:])
    return out


def _shifts_anticausal(ext, nk, tm):
    n = ext.shape[0]
    out = []
    for k in range(nk):
        s = nk - 1 - k
        r = ext if s == 0 else pltpu.roll(ext, n - s, 0)
        out.append(r[:tm])
    return out


def _wsum(w, parts):
    acc = w[0:1, :] * parts[0]
    for k in range(1, len(parts)):
        acc = acc + w[k:k + 1, :] * parts[k]
    return acc


def _colsum(x):
    return jnp.sum(x, axis=0, keepdims=True)


def _acc_out(ref, val, first):
    @pl.when(first)
    def _():
        ref[...] = val

    @pl.when(jnp.logical_not(first))
    def _():
        ref[...] += val


def _acc_rows(ref, rows, first):
    for k, r in enumerate(rows):
        _acc_out(ref.at[k:k + 1, :], r, first)


def _norm_matmul(x, wn, b, name, b_f32=None):
    T, N = x.shape[0], b.shape[1]
    tm = min(1024, T)
    tn = max(t for t in _divisors(N, 2816, (V7X_MXU, LANES))
             if 8 * tm * D + 6 * tm * D + 4 * D * t + 4 * tm * t <= MATMUL_VMEM)

    extra = b_f32 is not None

    def body(*refs):
        x_ref, wn_ref, b_ref = refs[:3]
        o_ref, u_ref = refs[3 + extra:5 + extra]
        keep_ref = refs[-1]

        @pl.when(pl.program_id(1) == 0)
        def _():
            xv = x_ref[...]
            r = lax.rsqrt(jnp.mean(xv * xv, axis=-1, keepdims=True) + EPS)
            u = (xv * r * wn_ref[...]).astype(BF16)
            keep_ref[...] = u
            u_ref[...] = u
            if extra:
                refs[5 + extra][...] = _dot(u, refs[3][...])

        o_ref[...] = _dot(keep_ref[...], b_ref[...]).astype(BF16)

    rows = pl.BlockSpec((tm, D), lambda i, j: (i, 0))
    whole = lambda shape: pl.BlockSpec(shape, lambda i, j: (0, 0))
    narrow = pl.BlockSpec((tm, LANES), lambda i, j: (i, 0))
    return pl.pallas_call(
        body, name=name, grid=(T // tm, N // tn),
        in_specs=[rows, whole((1, D)), pl.BlockSpec((D, tn), lambda i, j: (0, j))] + [whole((D, LANES))] * extra,
        out_specs=[pl.BlockSpec((tm, tn), lambda i, j: (i, j)), rows] + [narrow] * extra,
        out_shape=[jax.ShapeDtypeStruct((T, N), BF16), jax.ShapeDtypeStruct((T, D), BF16)]
        + [jax.ShapeDtypeStruct((T, LANES), F32)] * extra,
        scratch_shapes=[pltpu.VMEM((tm, D), BF16)],
        compiler_params=_params(("parallel", "arbitrary")),
    )(*((x, wn, b) + ((b_f32,) if extra else ())))


def _rmsnorm_bwd_epilogue(x, w, dres):
    T = x.shape[0]

    def fn(dyv, ins, outs, first):
        x_ref, w_ref, dr_ref = ins
        dx_ref, dxb_ref, dw_ref = outs
        xv = x_ref[...]
        r = lax.rsqrt(jnp.mean(xv * xv, axis=-1, keepdims=True) + EPS)
        xh = xv * r
        dxh = dyv * w_ref[...]
        dx = r * (dxh - xh * jnp.mean(dxh * xh, axis=-1, keepdims=True)) + dr_ref[...]
        dx_ref[...] = dx
        dxb_ref[...] = dx.astype(BF16)
        _acc_out(dw_ref, _colsum(dyv * xh), first)

    return _Epilogue(fn, (x, w, dres), (((T, D), F32), ((T, D), BF16), ((1, D), F32)), 14 * D)


def _branch_a_fwd(proj, conv_w):
    T = proj.shape[0]
    R = _Rows(T, 512)
    tm = R.tm

    def body(p_ref, pp_ref, w_ref, o_ref):
        keep = (pl.program_id(1) > 0).astype(F32)
        cv = p_ref[:, D:2 * D].astype(F32) * p_ref[:, 2 * D:].astype(F32)
        cvp = pp_ref[:, D:2 * D].astype(F32) * pp_ref[:, 2 * D:].astype(F32) * keep
        sh = _shifts_causal(jnp.concatenate([cvp, cv], axis=0), 3, tm)
        ca = _wsum(w_ref[...], sh)
        o_ref[...] = (p_ref[:, :D].astype(F32) * ca).astype(BF16)

    return R.call(body, "branch_a_fwd", 1, [R.tile(3 * D), R.prev(3 * D), R.colvec(3, D)], R.tile(D),
                  jax.ShapeDtypeStruct((T, D), BF16), (proj, proj, conv_w))


def _branch_a_bwd(dya_in, proj, conv_w, dproj):
    T = proj.shape[0]
    R = _Rows(T, 256)
    tm = R.tm

    def body(d_ref, dn_ref, p_ref, pp_ref, pn_ref, w_ref, _alias, o_ref, dw_ref):
        i = pl.program_id(1)
        keep_p = (i > 0).astype(F32)
        keep_n = (i < R.nrow - 1).astype(F32)
        w = w_ref[...]
        b = p_ref[:, :D].astype(F32)
        c = p_ref[:, D:2 * D].astype(F32)
        v = p_ref[:, 2 * D:].astype(F32)
        cvp = pp_ref[:, D:2 * D].astype(F32) * pp_ref[:, 2 * D:].astype(F32) * keep_p
        sh = _shifts_causal(jnp.concatenate([cvp, c * v], axis=0), 3, tm)
        ca = _wsum(w, sh)
        d = d_ref[...].astype(F32)
        dca = d * b
        dca_n = dn_ref[...].astype(F32) * pn_ref[:, :D].astype(F32) * keep_n
        dsh = _shifts_anticausal(jnp.concatenate([dca, dca_n], axis=0), 3, tm)
        dcv = _wsum(w, dsh)
        o_ref[:, :D] = (d * ca).astype(BF16)
        o_ref[:, D:2 * D] = (dcv * v).astype(BF16)
        o_ref[:, 2 * D:] = (dcv * c).astype(BF16)
        _acc_rows(dw_ref, [_colsum(dca * s) for s in sh], i == 0)

    return R.call(
        body, "branch_a_bwd", 1,
        [R.tile(D), R.next(D), R.tile(3 * D), R.prev(3 * D), R.next(3 * D), R.colvec(3, D), ANY],
        [R.tile(3 * D), R.colvec(3, D)],
        [jax.ShapeDtypeStruct(dproj.shape, BF16), jax.ShapeDtypeStruct((3, D), F32)],
        (dya_in, dya_in, proj, proj, proj, conv_w, dproj), aliases={6: 0})


_XW = 512


def _xbc_fwd(proj, conv_w, conv_b):
    T = proj.shape[0]
    R = _Rows(T, 512)
    tm = R.tm
    cb = OFF_XBC // _XW

    def body(x_ref, xp_ref, w_ref, b_ref, o_ref):
        keep = (pl.program_id(1) > 0).astype(F32)
        ext = jnp.concatenate([xp_ref[...].astype(F32) * keep, x_ref[...].astype(F32)], axis=0)
        pre = _wsum(w_ref[...], _shifts_causal(ext, 4, tm)) + b_ref[...]
        o_ref[...] = (pre * _sigmoid(pre)).astype(BF16)

    return R.call(body, "xbc_fwd", DX // _XW,
                  [R.tile(_XW, cb), R.prev(_XW, cb), R.colvec(4, _XW), R.colvec(1, _XW)], R.tile(_XW),
                  jax.ShapeDtypeStruct((T, DX), BF16), (proj, proj, conv_w, conv_b))


def _xbc_bwd(dact, proj, conv_w, conv_b, dproj):
    T = proj.shape[0]
    R = _Rows(T, 512)
    tm = R.tm
    cb = OFF_XBC // _XW

    def body(d_ref, dn_ref, x_ref, xp_ref, xn_ref, w_ref, b_ref, _alias, o_ref, dw_ref, db_ref):
        i = pl.program_id(1)
        keep_p = (i > 0).astype(F32)
        keep_n = (i < R.nrow - 1).astype(F32)
        w = w_ref[...]
        ext = jnp.concatenate([xp_ref[...].astype(F32) * keep_p, x_ref[...].astype(F32),
                               xn_ref[...].astype(F32)], axis=0)
        sh = _shifts_causal(ext, 4, tm + HALO)
        pre = _wsum(w, sh) + b_ref[...]
        s = _sigmoid(pre)
        dsilu = s * (1.0 + pre * (1.0 - s))
        dext = jnp.concatenate([d_ref[...].astype(F32), dn_ref[...].astype(F32) * keep_n], axis=0)
        dpre = dext * dsilu
        dsh = _shifts_anticausal(dpre, 4, tm)
        o_ref[...] = _wsum(w, dsh).astype(BF16)
        dp = dpre[:tm]
        _acc_rows(dw_ref, [_colsum(dp * q[:tm]) for q in sh], i == 0)
        _acc_out(db_ref, _colsum(dp), i == 0)

    return R.call(
        body, "xbc_bwd", DX // _XW,
        [R.tile(_XW), R.next(_XW), R.tile(_XW, cb), R.prev(_XW, cb), R.next(_XW, cb),
         R.colvec(4, _XW), R.colvec(1, _XW), ANY],
        [R.tile(_XW, cb), R.colvec(4, _XW), R.colvec(1, _XW)],
        [jax.ShapeDtypeStruct(dproj.shape, BF16), jax.ShapeDtypeStruct((4, DX), F32),
         jax.ShapeDtypeStruct((1, DX), F32)],
        (dact, dact, proj, proj, proj, conv_w, conv_b, dproj), aliases={7: 0})


def _softplus(x):
    return jnp.maximum(x, 0.0) + jnp.log(1.0 + jnp.exp(-jnp.abs(x)))


def _dt_rows(T):
    return min(8 * CH, T // 2)


def _dt_fwd(dt_raw, dt_bias_p, a_log_p):
    T = dt_raw.shape[0]
    rows = _dt_rows(T)

    def body(r_ref, b_ref, al_ref, dt_ref, ac_ref, acT_ref):
        dt = _softplus(r_ref[...] + b_ref[...])
        s = dt * (-jnp.exp(al_ref[...]))
        row = lax.broadcasted_iota(jnp.int32, (rows, LANES), 0) % CH
        k = 1
        while k < CH:
            s = s + jnp.where(row >= k, pltpu.roll(s, k, 0), 0.0)
            k *= 2
        dt_ref[...] = dt
        ac_ref[...] = s
        for q in range(0, rows, CH):
            acT_ref[q:q + CH] = s[q:q + CH].T

    blk = pl.BlockSpec((rows, LANES), lambda i: (i, 0))
    vec = pl.BlockSpec((1, LANES), lambda i: (0, 0))
    return pl.pallas_call(
        body, name="dt_fwd", grid=(T // rows,), in_specs=[blk, vec, vec], out_specs=[blk, blk, blk],
        out_shape=[jax.ShapeDtypeStruct((T, LANES), F32)] * 3, compiler_params=_params(("parallel",)),
    )(dt_raw, dt_bias_p, a_log_p)


def _dt_bwd(dacum, ddt_x, dt_raw, dt_bias_p, a_log_p, dproj):
    T = dt_raw.shape[0]
    rows = _dt_rows(T)
    nc = T // rows

    def body(da_ref, dx_ref, r_ref, b_ref, al_ref, _alias, o_ref, db_ref, dal_ref):
        i = pl.program_id(0)
        a = -jnp.exp(al_ref[...])
        z = r_ref[...] + b_ref[...]
        dt = _softplus(z)
        s = da_ref[...]
        row = lax.broadcasted_iota(jnp.int32, (rows, LANES), 0) % CH
        k = 1
        while k < CH:
            s = s + jnp.where(row < CH - k, pltpu.roll(s, rows - k, 0), 0.0)
            k *= 2
        ddt = s * a + dx_ref[...]
        draw = ddt * _sigmoid(z)
        o_ref[:, :LANES] = draw.astype(BF16)
        o_ref[:, LANES:] = jnp.zeros((rows, NIP - OFF_DT - LANES), BF16)
        _acc_out(db_ref, _colsum(draw), i == 0)
        _acc_out(dal_ref, _colsum(s * dt), i == 0)

        @pl.when(i == nc - 1)
        def _():
            dal_ref[...] = dal_ref[...] * a

    blk = pl.BlockSpec((rows, LANES), lambda i: (i, 0))
    vec = pl.BlockSpec((1, LANES), lambda i: (0, 0))
    oblk = pl.BlockSpec((rows, NIP - OFF_DT), lambda i: (i, OFF_DT // (NIP - OFF_DT)))
    return pl.pallas_call(
        body, name="dt_bwd", grid=(nc,), in_specs=[blk, blk, blk, vec, vec, ANY], out_specs=[oblk, vec, vec],
        out_shape=[jax.ShapeDtypeStruct(dproj.shape, BF16), jax.ShapeDtypeStruct((1, LANES), F32),
                   jax.ShapeDtypeStruct((1, LANES), F32)],
        input_output_aliases={5: 0}, compiler_params=_params(("arbitrary",)),
    )(dacum, ddt_x, dt_raw, dt_bias_p, a_log_p, dproj)


_GW = DI // NG
_HG = NH // NG
_NEG = -1e30


def _interleave(gens):
    out, live = [None] * len(gens), list(range(len(gens)))
    while live:
        for i in list(live):
            try:
                next(gens[i])
            except StopIteration as stop:
                out[i] = stop.value
                live.remove(i)
    return out


def _pair_lanes(left, v0, v1):
    return jnp.where(left, v0, v1)


def _ssd_specs(T, rev):
    nc = T // CH
    cm = (lambda c: nc - 1 - c) if rev else (lambda c: c)
    bw = NG * NS
    return dict(
        xs=pl.BlockSpec((CH, DI), lambda c: (cm(c), 0)),
        bm=pl.BlockSpec((CH, bw), lambda c: (cm(c), DI // bw)),
        cmat=pl.BlockSpec((CH, bw), lambda c: (cm(c), DI // bw + 1)),
        xbc=pl.BlockSpec((CH, DX), lambda c: (cm(c), 0)),
        col=pl.BlockSpec((CH, LANES), lambda c: (cm(c), 0)),
        dsk=pl.BlockSpec((1, DI), lambda c: (0, 0)),
        state=pl.BlockSpec((1, NS, DI), lambda c: (cm(c), 0, 0)),
    )


def _last(ref, lo, hi):
    return ref.at[(slice(None),) * (len(ref.shape) - 1) + (slice(lo, hi),)]


def _group_views(g, wide, narrow):
    return [_last(r, g * _GW, (g + 1) * _GW) for r in wide] + [_last(r, g * NS, (g + 1) * NS) for r in narrow]


def _ssd_fwd(xact, dt, acum, acumT, dsk_rep):
    T = xact.shape[0]
    nc = T // CH
    sp = _ssd_specs(T, False)

    def body(*refs):
        xs, bm, cmat, dtr, acr, actr, dsk, y, spv, S_ref = refs

        @pl.when(pl.program_id(0) == 0)
        def _():
            S_ref[...] = jnp.zeros_like(S_ref)

        _interleave([group(g * _HG, dtr[...], acr[...], actr[...],
                           *_group_views(g, (xs, dsk, y, spv, S_ref), (bm, cmat))) for g in range(NG)])

    def group(hb, dt, ac, acT, xs_ref, dsk_ref, y_ref, sp_ref, S_ref, b_ref, c_ref):
        Bm, Cm = b_ref[...], c_ref[...]
        S = S_ref[...]
        sp_ref[0] = S
        cb = _dot(Cm, Bm, NT)
        CS = _dot(Cm, S.astype(BF16))
        row = lax.broadcasted_iota(jnp.int32, (CH, CH), 0)
        col = lax.broadcasted_iota(jnp.int32, (CH, CH), 1)
        tril = row >= col
        left = col < HP
        xd_parts, dec_parts = [], []
        for p in range(_HG // 2):
            sl = slice(p * LANES, (p + 1) * LANES)
            j0, j1 = hb + 2 * p, hb + 2 * p + 1
            xp = xs_ref[:, sl].astype(F32)
            a0, a1 = ac[:, j0:j0 + 1], ac[:, j1:j1 + 1]
            al0, al1 = ac[CH - 1:CH, j0:j0 + 1], ac[CH - 1:CH, j1:j1 + 1]
            X = xp * _pair_lanes(left, dt[:, j0:j0 + 1], dt[:, j1:j1 + 1])
            Xb = X.astype(BF16)
            Ws = [(cb * jnp.exp(jnp.where(tril, aj - acT[j:j + 1, :], _NEG))).astype(BF16)
                  for j, aj in ((j0, a0), (j1, a1))]
            Xs = [jnp.where(m, Xb, jnp.zeros_like(Xb)) for m in (left, jnp.logical_not(left))]
            yield
            yd = _dot(jnp.concatenate(Ws, axis=1), jnp.concatenate(Xs, axis=0))
            yield
            eal = _pair_lanes(left, jnp.exp(a0), jnp.exp(a1))
            y = yd + eal * CS[:, sl] + dsk_ref[:, sl] * xp
            y_ref[:, sl] = y.astype(BF16)
            xd_parts.append(X * _pair_lanes(left, jnp.exp(al0 - a0), jnp.exp(al1 - a1)))
            dec_parts.append(_pair_lanes(left[0:1], jnp.exp(al0), jnp.exp(al1)))
        Xd = jnp.concatenate(xd_parts, axis=1).astype(BF16)
        dec = jnp.concatenate(dec_parts, axis=1)
        S_ref[...] = dec * S + _dot(Bm, Xd, TN)

    return pl.pallas_call(
        body, name="ssd_fwd", grid=(nc,),
        in_specs=[sp["xs"], sp["bm"], sp["cmat"], sp["col"], sp["col"], sp["col"], sp["dsk"]],
        out_specs=[sp["xs"], sp["state"]],
        out_shape=[jax.ShapeDtypeStruct((T, DI), BF16), jax.ShapeDtypeStruct((nc, NS, DI), F32)],
        scratch_shapes=[pltpu.VMEM((NS, DI), F32)],
        compiler_params=_params(("arbitrary",)),
    )(xact, xact, xact, dt, acum, acumT, dsk_rep)


def _ssd_bwd(dy, xact, dt, acum, acumT, dsk_rep, sprev):
    T = xact.shape[0]
    nc = T // CH
    sp = _ssd_specs(T, True)

    def body(*refs):
        xs, bm, cmat, dtr, acr, actr, dsk, dyr, spv, dxa, ddtx, dAc, dskacc, dS_ref = refs
        first = pl.program_id(0) == 0

        @pl.when(first)
        def _():
            dS_ref[...] = jnp.zeros_like(dS_ref)

        dbc = _last(dxa, DI, DX)
        ddtx_sum = jnp.zeros((CH, LANES), F32)
        dAc_sum = jnp.zeros((CH, LANES), F32)
        for a, b in _interleave([group(first, g * _HG, dtr[...], acr[...], actr[...],
                                       *_group_views(g, (xs, dsk, dyr, spv, dxa, dskacc, dS_ref),
                                                     (bm, cmat, dbc, _last(dbc, NG * NS, 2 * NG * NS))))
                                 for g in range(NG)]):
            ddtx_sum, dAc_sum = ddtx_sum + a, dAc_sum + b
        ddtx[...] = ddtx_sum
        dAc[...] = dAc_sum

    def group(first, hb, dt, ac, acT, xs_ref, dsk_ref, dy_ref, sp_ref, dx_ref, dskacc_ref, dS_ref, b_ref, c_ref,
              dB_ref, dC_ref):
        Bm, Cm = b_ref[...], c_ref[...]
        S = sp_ref[0]
        dS = dS_ref[...]
        Sb, dSb = S.astype(BF16), dS.astype(BF16)
        cb = _dot(Cm, Bm, NT)
        cbT = _dot(Bm, Cm, NT)
        CmT = Cm.T
        CS = _dot(Cm, Sb)
        T1 = _dot(Bm, dSb)
        yield
        row = lax.broadcasted_iota(jnp.int32, (CH, CH), 0)
        col = lax.broadcasted_iota(jnp.int32, (CH, CH), 1)
        tril = row >= col
        triu = row <= col
        left = col < HP
        lane8 = lax.broadcasted_iota(jnp.int32, (1, LANES), 1)
        lastrow = lax.broadcasted_iota(jnp.int32, (CH, 1), 0) == CH - 1
        dCB = jnp.zeros((CH, CH), F32)
        dCBT = jnp.zeros((CH, CH), F32)
        dAc = jnp.zeros((CH, LANES), F32)
        ddtx = jnp.zeros((CH, LANES), F32)
        xd_parts, dye_parts, dec_parts, dsk_parts = [], [], [], []
        for p in range(_HG // 2):
            sl = slice(p * LANES, (p + 1) * LANES)
            j0, j1 = hb + 2 * p, hb + 2 * p + 1
            xp = xs_ref[:, sl].astype(F32)
            dyp = dy_ref[:, sl].astype(F32)
            a0, a1 = ac[:, j0:j0 + 1], ac[:, j1:j1 + 1]
            al0, al1 = ac[CH - 1:CH, j0:j0 + 1], ac[CH - 1:CH, j1:j1 + 1]
            dtl = _pair_lanes(left, dt[:, j0:j0 + 1], dt[:, j1:j1 + 1])
            X = xp * dtl
            Xb = X.astype(BF16)
            eal = _pair_lanes(left, jnp.exp(a0), jnp.exp(a1))
            dtel = _pair_lanes(left, jnp.exp(al0 - a0), jnp.exp(al1 - a1))
            T1p = T1[:, sl]
            Rm = T1p * dtel * X
            GR = dyp * (eal * CS[:, sl]) - Rm
            SdS = dS[:, sl] * S[:, sl]
            dXd = jnp.zeros((CH, LANES), F32)
            for j, aj, alj, mask in ((j0, a0, al0, left), (j1, a1, al1, jnp.logical_not(left))):
                dYm = jnp.where(mask, dyp, 0.0).astype(BF16)
                dWm = _dot(dYm, Xb, NT)
                dWmT = _dot(Xb, dYm, NT)
                yield
                e = aj - acT[j:j + 1, :]
                P = dWm * jnp.exp(jnp.where(tril, e, _NEG))
                LmT = jnp.exp(jnp.where(triu, -e, _NEG))
                PT = dWmT * LmT
                dCB = dCB + P
                dCBT = dCBT + PT
                yield
                dXd = dXd + _dot((cbT * LmT).astype(BF16), dYm)
                qd = P * cb - PT * cbT + jnp.where(mask, GR, 0.0)
                colv = jnp.sum(qd, axis=1, keepdims=True)
                tot = jnp.where(mask, Rm + jnp.exp(alj) * SdS, 0.0)
                dalast = jnp.sum(jnp.sum(tot, axis=0, keepdims=True), axis=1, keepdims=True)
                dAc = dAc + (colv + jnp.where(lastrow, dalast, 0.0)) * (lane8 == j).astype(F32)
                yield
            dX = dXd + dtel * T1p
            dXx = dX * xp
            for j, mask in ((j0, left), (j1, jnp.logical_not(left))):
                dd = jnp.sum(jnp.where(mask, dXx, 0.0), axis=1, keepdims=True)
                ddtx = ddtx + dd * (lane8 == j).astype(F32)
            dx_ref[:, sl] = (dX * dtl + dsk_ref[:, sl] * dyp).astype(BF16)
            dsk_parts.append(_colsum(dyp * xp))
            xd_parts.append(X * dtel)
            dye_parts.append(dyp * eal)
            dec_parts.append(_pair_lanes(left[0:1], jnp.exp(al0), jnp.exp(al1)))
            yield
        Xd = jnp.concatenate(xd_parts, axis=1).astype(BF16)
        dYe = jnp.concatenate(dye_parts, axis=1).astype(BF16)
        dec = jnp.concatenate(dec_parts, axis=1)
        dC_ref[...] = (_dot(dCB.astype(BF16), Bm) + _dot(dYe, Sb, NT)).astype(BF16)
        dB_ref[...] = (_dot(dCBT.astype(BF16), Cm) + _dot(Xd, dSb, NT)).astype(BF16)
        dS_ref[...] = _dot(CmT, dYe) + dec * dS
        _acc_out(dskacc_ref, jnp.concatenate(dsk_parts, axis=1), first)
        return ddtx, dAc

    return pl.pallas_call(
        body, name="ssd_bwd", grid=(nc,),
        in_specs=[sp["xs"], sp["bm"], sp["cmat"], sp["col"], sp["col"], sp["col"], sp["dsk"], sp["xs"],
                  sp["state"]],
        out_specs=[sp["xbc"], sp["col"], sp["col"], sp["dsk"]],
        out_shape=[jax.ShapeDtypeStruct((T, DX), BF16), jax.ShapeDtypeStruct((T, LANES), F32),
                   jax.ShapeDtypeStruct((T, LANES), F32), jax.ShapeDtypeStruct((1, DI), F32)],
        scratch_shapes=[pltpu.VMEM((NS, DI), F32)],
        compiler_params=_params(("arbitrary",)),
    )(xact, xact, xact, dt, acum, acumT, dsk_rep, dy, sprev)


def _gnorm_fwd(y, proj, w):
    T = y.shape[0]
    R = _Rows(T, 1024)
    zb = OFF_Z // _GW

    def body(y_ref, z_ref, w_ref, o_ref):
        z = z_ref[...].astype(F32)
        yf = y_ref[...].astype(F32) * z * _sigmoid(z)
        r = lax.rsqrt(jnp.mean(yf * yf, axis=-1, keepdims=True) + EPS)
        o_ref[...] = (yf * r * w_ref[...]).astype(BF16)

    return R.call(body, "gnorm_fwd", NG, [R.tile(_GW), R.tile(_GW, zb), R.colvec(1, _GW)], R.tile(_GW),
                  jax.ShapeDtypeStruct((T, DI), BF16), (y, proj, w))


def _gnorm_bwd(dn, y, proj, w, dproj):
    T = y.shape[0]
    R = _Rows(T, 1024)
    zb = OFF_Z // _GW

    def body(dn_ref, y_ref, z_ref, w_ref, _alias, dz_ref, dy_ref, dw_ref):
        z = z_ref[...].astype(F32)
        yv = y_ref[...].astype(F32)
        s = _sigmoid(z)
        silu = z * s
        yf = yv * silu
        r = lax.rsqrt(jnp.mean(yf * yf, axis=-1, keepdims=True) + EPS)
        yh = yf * r
        dnv = dn_ref[...].astype(F32)
        dyh = dnv * w_ref[...]
        dyf = r * (dyh - yh * jnp.mean(dyh * yh, axis=-1, keepdims=True))
        dy_ref[...] = (dyf * silu).astype(BF16)
        dz_ref[...] = (dyf * yv * s * (1.0 + z * (1.0 - s))).astype(BF16)
        _acc_out(dw_ref, _colsum(dnv * yh), pl.program_id(1) == 0)

    return R.call(
        body, "gnorm_bwd", NG, [R.tile(_GW), R.tile(_GW), R.tile(_GW, zb), R.colvec(1, _GW), ANY],
        [R.tile(_GW, zb), R.tile(_GW), R.colvec(1, _GW)],
        [jax.ShapeDtypeStruct(dproj.shape, BF16), jax.ShapeDtypeStruct((T, DI), BF16),
         jax.ShapeDtypeStruct((1, DI), F32)],
        (dn, y, proj, w, dproj), aliases={4: 0})


def _merge_fwd(proj, ya, ys):
    T = proj.shape[0]
    R = _Rows(T, 512)
    gb = OFF_G // (2 * D)

    def body(g_ref, ya_ref, ys_ref, o_ref):
        ga = _sigmoid(g_ref[:, :D].astype(F32))
        gs = _sigmoid(g_ref[:, D:].astype(F32))
        o_ref[...] = (ga * ya_ref[...].astype(F32) + gs * ys_ref[...].astype(F32)).astype(BF16)

    return R.call(body, "merge_fwd", 1, [R.tile(2 * D, gb), R.tile(D), R.tile(D)], R.tile(D),
                  jax.ShapeDtypeStruct((T, D), BF16), (proj, ya, ys))


def _merge_bwd(dm, proj, ya, ys, ncols):
    T = proj.shape[0]
    R = _Rows(T, 256)
    gb = OFF_G // (2 * D)

    def body(dm_ref, g_ref, ya_ref, ys_ref, dg_ref, dya_ref, dys_ref):
        d = dm_ref[...].astype(F32)
        ga = _sigmoid(g_ref[:, :D].astype(F32))
        gs = _sigmoid(g_ref[:, D:].astype(F32))
        dya_ref[...] = (d * ga).astype(BF16)
        dys_ref[...] = (d * gs).astype(BF16)
        dg_ref[:, :D] = (d * ya_ref[...].astype(F32) * ga * (1.0 - ga)).astype(BF16)
        dg_ref[:, D:] = (d * ys_ref[...].astype(F32) * gs * (1.0 - gs)).astype(BF16)

    return R.call(
        body, "merge_bwd", 1, [R.tile(D), R.tile(2 * D, gb), R.tile(D), R.tile(D)],
        [R.tile(2 * D, gb), R.tile(D), R.tile(D)],
        [jax.ShapeDtypeStruct((T, ncols), BF16), jax.ShapeDtypeStruct((T, D), BF16),
         jax.ShapeDtypeStruct((T, D), BF16)],
        (dm, proj, ya, ys))


_FW = 1408
_FB = FF // _FW


def _ffn_act_fwd(hv, conv_w, conv_b):
    T = hv.shape[0]
    R = _Rows(T, 256)
    tm = R.tm

    def body(h1_ref, h1p_ref, h3_ref, w_ref, b_ref, o_ref):
        keep = (pl.program_id(1) > 0).astype(F32)
        ext = jnp.concatenate([h1p_ref[...].astype(F32) * keep, h1_ref[...].astype(F32)], axis=0)
        pre = _wsum(w_ref[...], _shifts_causal(ext, 3, tm)) + b_ref[...]
        o_ref[...] = (pre * _sigmoid(pre) * h3_ref[...].astype(F32)).astype(BF16)

    return R.call(body, "ffn_act_fwd", _FB,
                  [R.tile(_FW), R.prev(_FW), R.tile(_FW, _FB), R.colvec(3, _FW), R.colvec(1, _FW)],
                  R.tile(_FW), jax.ShapeDtypeStruct((T, FF), BF16), (hv, hv, hv, conv_w, conv_b))


def _ffn_act_bwd(dg, hv, conv_w, conv_b):
    T = hv.shape[0]
    R = _Rows(T, 256)
    tm = R.tm

    def body(dg_ref, h1_ref, h1p_ref, h3_ref, w_ref, b_ref, dh3_ref, dpre_ref, dw_ref, db_ref):
        i = pl.program_id(1)
        keep = (i > 0).astype(F32)
        ext = jnp.concatenate([h1p_ref[...].astype(F32) * keep, h1_ref[...].astype(F32)], axis=0)
        sh = _shifts_causal(ext, 3, tm)
        pre = _wsum(w_ref[...], sh) + b_ref[...]
        s = _sigmoid(pre)
        d = dg_ref[...].astype(F32)
        dh3_ref[...] = (d * pre * s).astype(BF16)
        dpre = d * h3_ref[...].astype(F32) * s * (1.0 + pre * (1.0 - s))
        dpre_ref[...] = dpre.astype(BF16)
        _acc_rows(dw_ref, [_colsum(dpre * q) for q in sh], i == 0)
        _acc_out(db_ref, _colsum(dpre), i == 0)

    return R.call(
        body, "ffn_act_bwd", _FB,
        [R.tile(_FW), R.tile(_FW), R.prev(_FW), R.tile(_FW, _FB), R.colvec(3, _FW), R.colvec(1, _FW)],
        [R.tile(_FW), R.tile(_FW), R.colvec(3, _FW), R.colvec(1, _FW)],
        [jax.ShapeDtypeStruct((T, FF), BF16), jax.ShapeDtypeStruct((T, FF), BF16),
         jax.ShapeDtypeStruct((3, FF), F32), jax.ShapeDtypeStruct((1, FF), F32)],
        (dg, hv, hv, hv, conv_w, conv_b))


def _conv3_transpose(dpre, conv_w):
    T = dpre.shape[0]
    R = _Rows(T, 256)
    tm = R.tm

    def body(d_ref, dn_ref, w_ref, o_ref):
        keep = (pl.program_id(1) < R.nrow - 1).astype(F32)
        ext = jnp.concatenate([d_ref[...].astype(F32), dn_ref[...].astype(F32) * keep], axis=0)
        o_ref[...] = _wsum(w_ref[...], _shifts_anticausal(ext, 3, tm)).astype(BF16)

    return R.call(body, "ffn_conv_bwd", _FB, [R.tile(_FW), R.next(_FW), R.colvec(3, _FW)], R.tile(_FW),
                  jax.ShapeDtypeStruct((T, FF), BF16), (dpre, dpre, conv_w))


def _final_loss_epilogue(w, target):
    T = target.shape[0]

    def fn(xv, ins, outs, first):
        w_ref, t_ref = ins
        l_ref, dh_ref, dhb_ref, dw_ref = outs
        wv = w_ref[...]
        r = lax.rsqrt(jnp.mean(xv * xv, axis=-1, keepdims=True) + EPS)
        xh = xv * r
        err = xh * wv - t_ref[...]
        part = 0.5 * jnp.sum(jnp.mean(err * err, axis=-1, keepdims=True), axis=0, keepdims=True)
        _acc_out(l_ref, jnp.broadcast_to(part, l_ref.shape), first)
        dy = err * (1.0 / D)
        dxh = dy * wv
        dh = r * (dxh - xh * jnp.mean(dxh * xh, axis=-1, keepdims=True))
        dh_ref[...] = dh
        dhb_ref[...] = dh.astype(BF16)
        _acc_out(dw_ref, _colsum(dy * xh), first)

    return _Epilogue(fn, (w, target),
                     (((8, LANES), F32), ((T, D), F32), ((T, D), BF16), ((1, D), F32)), 10 * D)


def _pad_lanes(v, n=LANES):
    return jnp.pad(v, ((0, 0), (0, n - v.shape[1])))


class _Hooks:
    def before_in_proj(self, w_in):
        return w_in

    def late_weights(self, wts, after):
        return wts

    def grads_ready(self, grads, tie):
        return tie

    def mark(self, name, value):
        pass


def _local_step(x, target, wts, hooks=None):
    hooks = hooks or _Hooks()
    T = x.shape[0]
    w_in = wts["w_in"]
    dt_bias_p, a_log_p = _pad_lanes(wts["dt_bias"]), _pad_lanes(wts["a_log"])
    dsk_rep = jnp.repeat(wts["d_skip"], HP, axis=1)

    w_in = hooks.before_in_proj(w_in)
    proj, u, dt_raw = _norm_matmul(x, wts["norm_mix_w"], w_in, "norm_mm_in", w_in[:, OFF_DT:OFF_DT + LANES])
    ya_in = _branch_a_fwd(proj, wts["conv_a_w"])
    xact = _xbc_fwd(proj, wts["ssd_conv_w"], wts["ssd_conv_b"])
    dt, acum, acumT = _dt_fwd(dt_raw, dt_bias_p, a_log_p)
    y_ssd, sprev = _ssd_fwd(xact, dt, acum, acumT, dsk_rep)
    yn = _gnorm_fwd(y_ssd, proj, wts["ssd_norm_w"])
    late = hooks.late_weights(wts, yn)
    w_a_out, w_s_out, w_o, w_up, w_down = (late[k] for k in ("w_a_out", "w_s_out", "w_o", "w_up", "w_down"))
    y_a = _matmul(ya_in, w_a_out, mode="nn", out_dtype=BF16, name="mm_a_out")
    y_s = _matmul(yn, w_s_out, mode="nn", out_dtype=BF16, name="mm_s_out")
    merged = _merge_fwd(proj, y_a, y_s)
    h1 = _matmul(merged, w_o, mode="nn", out_dtype=F32, name="mm_o", residual=x)
    hv, v = _norm_matmul(h1, wts["norm_ffn_w"], w_up, "norm_mm_up")
    gact = _ffn_act_fwd(hv, wts["ffn_conv_w"], wts["ffn_conv_b"])
    loss, dh2, dh2b, g_final = _matmul(gact, w_down, mode="nn", out_dtype=F32, name="mm_down_loss", residual=h1,
                                       epilogue=_final_loss_epilogue(wts["final_norm_w"], target))

    grads = {"final_norm_w": g_final}
    grads["w_down"] = _matmul(gact, dh2b, mode="tn", out_dtype=F32, name="mm_down_dw")
    dgact = _matmul(dh2b, w_down, mode="nt", out_dtype=BF16, name="mm_down_dx")
    dh3, dpre, grads["ffn_conv_w"], grads["ffn_conv_b"] = _ffn_act_bwd(dgact, hv, wts["ffn_conv_w"], wts["ffn_conv_b"])
    dh1c = _conv3_transpose(dpre, wts["ffn_conv_w"])
    grads["w_up"] = (_matmul(v, dh1c, mode="tn", out_dtype=F32, name="mm_up_dw1"),
                     _matmul(v, dh3, mode="tn", out_dtype=F32, name="mm_up_dw3"))
    dv = _matmul(dh1c, w_up, mode="nt", out_dtype=F32, name="mm_up_dx1")
    dh1, dh1b, grads["norm_ffn_w"] = _matmul(
        dh3, w_up, mode="nt", out_dtype=F32, name="mm_up_dx3_norm", residual=dv, b_k_off=FF,
        epilogue=_rmsnorm_bwd_epilogue(h1, wts["norm_ffn_w"], dh2))
    grads["w_o"] = _matmul(merged, dh1b, mode="tn", out_dtype=F32, name="mm_o_dw")
    dmerged = _matmul(dh1b, w_o, mode="nt", out_dtype=BF16, name="mm_o_dx")
    dproj, dya, dys = _merge_bwd(dmerged, proj, y_a, y_s, NIP)
    grads["w_a_out"] = _matmul(ya_in, dya, mode="tn", out_dtype=F32, name="mm_a_out_dw")
    dya_in = _matmul(dya, w_a_out, mode="nt", out_dtype=BF16, name="mm_a_out_dx")
    dproj, grads["conv_a_w"] = _branch_a_bwd(dya_in, proj, wts["conv_a_w"], dproj)
    grads["w_s_out"] = _matmul(yn, dys, mode="tn", out_dtype=F32, name="mm_s_out_dw")
    dys = hooks.grads_ready({k: grads[k] for k in ("w_a_out", "w_s_out", "w_o", "w_up", "w_down")}, dys)
    dyn =_matmul(dys, w_s_out, mode="nt", out_dtype=BF16, name="mm_s_out_dx")
    dproj, dy_ssd, grads["ssd_norm_w"] = _gnorm_bwd(dyn, y_ssd, proj, wts["ssd_norm_w"], dproj)
    dxact, ddt_x, dacum, dskl = _ssd_bwd(dy_ssd, xact, dt, acum, acumT, dsk_rep, sprev)
    hooks.mark("ssd_bwd", dxact)
    grads["d_skip"] = dskl.reshape(NH, HP).sum(axis=1).reshape(1, NH)
    dproj, grads["ssd_conv_w"], grads["ssd_conv_b"] = _xbc_bwd(dxact, proj, wts["ssd_conv_w"], wts["ssd_conv_b"], dproj)
    dproj, g_dtb, g_alog = _dt_bwd(dacum, ddt_x, dt_raw, dt_bias_p, a_log_p, dproj)
    grads["dt_bias"], grads["a_log"] = g_dtb[:, :NH], g_alog[:, :NH]
    grads["w_in"] = _matmul(u, dproj, mode="tn", out_dtype=F32, name="mm_in_dw")
    dproj = hooks.grads_ready({"w_in": grads["w_in"]}, dproj)
    grad_x, _, grads["norm_mix_w"] = _matmul(dproj, w_in, mode="nt", out_dtype=F32, name="mm_in_dx_norm",
                                             epilogue=_rmsnorm_bwd_epilogue(x, wts["norm_mix_w"], dh1))
    return loss, grad_x, grads


def _permute_w_in(w):
    out = jnp.zeros((w.shape[0], NIP), w.dtype)
    for o, n, no in _SEGS:
        out = lax.dynamic_update_slice(out, w[:, o:o + n], (0, no))
    return out


def _unpermute_w_in(g):
    order = sorted(_SEGS)
    return jnp.concatenate([g[:, no:no + n] for o, n, no in order], axis=1)


MESH = pl.DeviceIdType.MESH
NCHIP = 4
NDEV = 8

_W_IN = (("w_in", D, NI // NCHIP, 1),)
_W_REST = (("w_a_out", D // NCHIP, D, 0), ("w_s_out", DI // NCHIP, D, 0), ("w_o", D // NCHIP, D, 0),
           ("w_up", D, 2 * FF // NCHIP, 1), ("w_down", FF // NCHIP, D, 0))


def _coords():
    return lax.axis_index("x"), lax.axis_index("y"), lax.axis_index("c")


def _other_chips(x, y):
    return [(1 - x, y), (x, 1 - y), (1 - x, 1 - y)]


def _ag_weights(shard):
    nrows = shard.shape[0]
    hr = nrows // 2

    def body(x_ref, out_ref, send_sems, recv_sems, local_sem):
        x, y, c = _coords()
        me = 2 * x + y
        chips = _other_chips(x, y)

        def rows(s, h):
            return out_ref.at[s, pl.ds(h * hr, hr), :]

        def copy(k, s, h, to, src=None):
            return pltpu.make_async_remote_copy(
                src_ref=rows(s, h) if src is None else src, dst_ref=rows(s, h),
                send_sem=send_sems.at[k], recv_sem=recv_sems.at[k], device_id=to, device_id_type=MESH)

        mine = pltpu.make_async_copy(x_ref, out_ref.at[me], local_sem)
        mine.start()
        first = [copy(k, me, c, (*chip, c), src=x_ref.at[pl.ds(c * hr, hr), :]) for k, chip in enumerate(chips)]
        for cp in first:
            cp.start()
        passed = []
        for k, chip in enumerate(chips):
            s = 2 * chip[0] + chip[1]
            copy(k, s, c, (x, y, c)).wait_recv()
            fwd = copy(3 + k, s, c, (x, y, 1 - c))
            fwd.start()
            passed.append(fwd)
        for k, chip in enumerate(chips):
            copy(3 + k, 2 * chip[0] + chip[1], 1 - c, (x, y, c)).wait_recv()
        for cp in first + passed:
            cp.wait_send()
        mine.wait()

    return pl.pallas_call(
        body, name="ag_weights", in_specs=[ANY], out_specs=ANY,
        out_shape=jax.ShapeDtypeStruct((NCHIP,) + shard.shape, shard.dtype),
        scratch_shapes=[pltpu.SemaphoreType.DMA((6,)), pltpu.SemaphoreType.DMA((6,)), pltpu.SemaphoreType.DMA],
        compiler_params=pltpu.CompilerParams(has_side_effects=True),
    )(shard)


HBM = pl.BlockSpec(memory_space=pltpu.HBM)
SEM = pl.BlockSpec(memory_space=pltpu.SEMAPHORE)
_EFFECT = pltpu.SideEffectType.DATAFLOW_SIDE_EFFECTING
_NCOPY = NCHIP - 1


def _plan_bcast(src_ref, land_ref, send_sems, recv_sems, base):
    x, y, c = _coords()
    sends, lands = [], []
    for k, chip in enumerate(_other_chips(x, y)):
        def copy(slot):
            return pltpu.make_async_remote_copy(
                src_ref=src_ref, dst_ref=land_ref.at[slot], send_sem=send_sems.at[base + k],
                recv_sem=recv_sems.at[base + k], device_id=(*chip, c), device_id_type=MESH)
        sends.append(copy(2 * x + y))
        lands.append(copy(2 * chip[0] + chip[1]))
    return sends, lands


def _plan_scatter(src_ref, land_ref, send_sems, recv_sems, base):
    x, y, c = _coords()
    cps = [pltpu.make_async_remote_copy(
        src_ref=src_ref.at[2 * chip[0] + chip[1]], dst_ref=land_ref.at[k], send_sem=send_sems.at[base + k],
        recv_sem=recv_sems.at[base + k], device_id=(*chip, c), device_id_type=MESH)
        for k, chip in enumerate(_other_chips(x, y))]
    return cps, cps


def _plan_all(plan, refs, n):
    sends, lands = [], []
    for t in range(n):
        s, l = plan(refs[t], refs[n + t], refs[2 * n], refs[2 * n + 1], t * _NCOPY)
        sends += s
        lands += l
    return sends, lands


def _split_start(name, srcs, lands, plan):
    n = len(srcs)

    def body(*refs):
        for cp in _plan_all(plan, refs, n)[0]:
            cp.start()
        refs[-1][...] = jnp.zeros_like(refs[-1])

    arrays = list(srcs) + list(lands)
    outs = pl.pallas_call(
        body, name=name,
        out_shape=(pltpu.SemaphoreType.DMA((n * _NCOPY,)), pltpu.SemaphoreType.DMA((n * _NCOPY,)),
                   *[pltpu.HBM(a.shape, a.dtype) for a in arrays], jax.ShapeDtypeStruct((8, LANES), F32)),
        in_specs=(HBM,) * (2 * n),
        out_specs=(SEM, SEM) + (HBM,) * (2 * n) + (pl.BlockSpec(memory_space=pltpu.VMEM),),
        input_output_aliases={t: 2 + t for t in range(2 * n)},
        compiler_params=pltpu.CompilerParams(has_side_effects=_EFFECT),
    )(*[pltpu.with_memory_space_constraint(a, pltpu.HBM) for a in arrays])
    return (outs[0], outs[1], tuple(outs[2:2 + 2 * n])), outs[-1]


def _split_wait(name, handle, after, plan):
    send_sems, recv_sems, arrays = handle
    n = len(arrays) // 2

    def body(*refs):
        sends, lands = _plan_all(plan, refs[:2 * n] + refs[2 * n:2 * n + 2], n)
        for cp in sends:
            cp.wait_send()
        for cp in lands:
            cp.wait_recv()

    outs = pl.pallas_call(
        body, name=name, out_shape=tuple(pltpu.HBM(a.shape, a.dtype) for a in arrays),
        in_specs=(HBM,) * (2 * n) + (SEM, SEM, ANY), out_specs=(HBM,) * (2 * n),
        input_output_aliases={t: t for t in range(2 * n)},
        compiler_params=pltpu.CompilerParams(has_side_effects=_EFFECT),
    )(*arrays, send_sems, recv_sems, after)
    return outs[:n], outs[n:]


def _tie(x, token, name):
    def body(x_ref, t_ref, o_ref):
        pass

    return pl.pallas_call(
        body, name=name, in_specs=[ANY, pl.BlockSpec(memory_space=pltpu.VMEM)], out_specs=ANY,
        out_shape=jax.ShapeDtypeStruct(x.shape, x.dtype), input_output_aliases={0: 0},
    )(x, token)


def _swap_sibling(ps, name):
    n = len(ps)

    def body(*refs):
        x, y, c = _coords()
        cps = [pltpu.make_async_remote_copy(
            src_ref=refs[t], dst_ref=refs[n + t], send_sem=refs[2 * n].at[t], recv_sem=refs[2 * n + 1].at[t],
            device_id=(x, y, 1 - c), device_id_type=MESH) for t in range(n)]
        for cp in cps:
            cp.start()
        for cp in cps:
            cp.wait()

    return pl.pallas_call(
        body, name=name, in_specs=[ANY] * n, out_specs=[ANY] * n,
        out_shape=[jax.ShapeDtypeStruct(p.shape, p.dtype) for p in ps],
        scratch_shapes=[pltpu.SemaphoreType.DMA((n,)), pltpu.SemaphoreType.DMA((n,))],
        compiler_params=pltpu.CompilerParams(has_side_effects=True),
    )(*ps)


_ADD_BYTES = 7 << 19


def _add_tile(rows, cols):
    best = 32
    for t in range(32, rows + 1, 32):
        if rows % t == 0 and t * cols * 4 <= _ADD_BYTES:
            best = t
    return best


def _add_slabs(pack, land, me, name):
    rows, cols = pack.shape[1:]
    tr = _add_tile(rows, cols)

    def body(me_ref, p_ref, l_ref, o_ref):
        f = lambda r: r.astype(F32)
        o_ref[...] = ((f(p_ref[0]) + f(l_ref[0])) + f(l_ref[1])) + f(l_ref[2])

    return pl.pallas_call(
        body, name=name,
        grid_spec=pltpu.PrefetchScalarGridSpec(
            num_scalar_prefetch=1, grid=(rows // tr,),
            in_specs=[pl.BlockSpec((1, tr, cols), lambda i, me_ref: (me_ref[0], i, 0)),
                      pl.BlockSpec((_NCOPY, tr, cols), lambda i, me_ref: (0, i, 0))],
            out_specs=pl.BlockSpec((tr, cols), lambda i, me_ref: (i, 0))),
        out_shape=jax.ShapeDtypeStruct((rows, cols), F32),
        compiler_params=_params(("parallel",)),
    )(me, pack, land)


_STAGE_W = 1024


def _stage_rows(shapes):
    pieces, r = [], 0
    for i, (k, w) in enumerate(shapes):
        for a in range(k):
            for q in range(0, w, _STAGE_W):
                pieces.append((i, a, q, min(_STAGE_W, w - q), r))
                r += 1
    return pieces, -(-r // 8) * 8


def _gather8(parts, reduce, name):
    shapes = [p.shape for p in parts]
    pieces, rows = _stage_rows(shapes)
    n = len(parts)

    def body(*refs):
        ins, outs = refs[:n], refs[n:2 * n]
        stage, buf, res, send_sems, recv_sems = refs[2 * n:]
        x, y, c = _coords()
        me = 4 * x + 2 * y + c
        stage[...] = jnp.zeros_like(stage)
        for i, a, q, w, r in pieces:
            stage[r:r + 1, 0:w] = ins[i][a:a + 1, q:q + w]
        buf[pl.ds(me, 1)] = stage[...][None]
        cps, lands = [], []
        for k in range(1, NDEV):
            peer = (1 - x if k & 4 else x, 1 - y if k & 2 else y, 1 - c if k & 1 else c)

            def copy(slot):
                return pltpu.make_async_remote_copy(
                    src_ref=stage, dst_ref=buf.at[slot], send_sem=send_sems.at[k - 1],
                    recv_sem=recv_sems.at[k - 1], device_id=peer, device_id_type=MESH)

            cps.append(copy(me))
            lands.append(copy(4 * peer[0] + 2 * peer[1] + peer[2]))
        for cp in cps:
            cp.start()
        for cp, land in zip(cps, lands):
            land.wait_recv()
            cp.wait_send()
        if reduce:
            acc = buf[0]
            for d in range(1, NDEV):
                acc = acc + buf[d]
            res[...] = acc
            for i, a, q, w, r in pieces:
                outs[i][a:a + 1, q:q + w] = res[r:r + 1, 0:w]
        else:
            for i, a, q, w, r in pieces:
                for s in range(NCHIP):
                    outs[i][s, a:a + 1, q:q + w] = buf[2 * s, r:r + 1, 0:w]

    vm = pl.BlockSpec(memory_space=pltpu.VMEM)
    out_shapes = [jax.ShapeDtypeStruct(s if reduce else (NCHIP,) + s, F32) for s in shapes]
    return pl.pallas_call(
        body, name=name, in_specs=[vm] * n, out_specs=[vm] * n, out_shape=out_shapes,
        scratch_shapes=[pltpu.VMEM((rows, _STAGE_W), F32), pltpu.VMEM((NDEV, rows, _STAGE_W), F32),
                        pltpu.VMEM((rows, _STAGE_W), F32), pltpu.SemaphoreType.DMA((NDEV - 1,)),
                        pltpu.SemaphoreType.DMA((NDEV - 1,))],
        compiler_params=pltpu.CompilerParams(has_side_effects=True),
    )(*parts)


def _adamw_update(w_ref, g_ref, m_ref, v_ref, d_ref, mo_ref, vo_ref):
    c1 = 1.0 / (1.0 - ADAM_B1 ** ADAM_STEP)
    c2 = 1.0 / (1.0 - ADAM_B2 ** ADAM_STEP)
    gv = g_ref[...]
    mn = ADAM_B1 * m_ref[...] + (1.0 - ADAM_B1) * gv
    vn = ADAM_B2 * v_ref[...] + (1.0 - ADAM_B2) * (gv * gv)
    d_ref[...] = -ADAM_LR * ((mn * c1) / (jnp.sqrt(vn * c2) + ADAM_EPS) + ADAM_WD * w_ref[...])
    mo_ref[...] = mn
    vo_ref[...] = vn


def _adamw_small(ws, gs, ms, vs):
    n = len(ws)

    def body(*refs):
        for i in range(n):
            _adamw_update(*(refs[j * n + i] for j in range(7)))

    vm = pl.BlockSpec(memory_space=pltpu.VMEM)
    outs = pl.pallas_call(
        body, name="adamw_small", in_specs=[vm] * (4 * n), out_specs=[vm] * (3 * n),
        out_shape=[jax.ShapeDtypeStruct(w.shape, F32) for w in ws] * 3,
    )(*ws, *gs, *ms, *vs)
    return outs[:n], outs[n:2 * n], outs[2 * n:]


def _adamw(w, g_parts, m, v, name):
    rows, cols = w.shape
    tr = rows
    while tr * cols * 4 > (1 << 20) and tr % 16 == 0:
        tr //= 2

    def body(w_ref, ga_ref, gb_ref, m_ref, v_ref, g_ref, d_ref, mo_ref, vo_ref):
        g_ref[...] = ga_ref[...] + gb_ref[...]
        _adamw_update(w_ref, g_ref, m_ref, v_ref, d_ref, mo_ref, vo_ref)

    blk = pl.BlockSpec((tr, cols), lambda i: (i, 0))
    return pl.pallas_call(
        body, name=name, grid=(rows // tr,), in_specs=[blk] * 5, out_specs=[blk] * 4,
        out_shape=[jax.ShapeDtypeStruct((rows, cols), F32)] * 4, compiler_params=_params(("parallel",)),
    )(w, *g_parts, m, v)


def _by_chip(g, rr, cc, axis):
    if isinstance(g, tuple):
        n = NCHIP // len(g)
        return jnp.concatenate([h.reshape(rr, n, cc).transpose(1, 0, 2) for h in g], axis=0)
    return g.reshape(NCHIP, rr, cc) if axis == 0 else g.reshape(rr, NCHIP, cc).transpose(1, 0, 2)


_SMALL_REPL = ("norm_mix_w", "ssd_conv_b", "dt_bias", "a_log", "d_skip", "ssd_norm_w", "norm_ffn_w",
               "ffn_conv_b", "final_norm_w")
_SMALL_CONV = (("conv_a_w", 3, D), ("ssd_conv_w", 4, DX), ("ffn_conv_w", 3, FF))


def kernel(x, norm_mix_w, w_in, conv_a_w, w_a_out, ssd_conv_w, ssd_conv_b, dt_bias, a_log, d_skip, ssd_norm_w, w_s_out, w_o, norm_ffn_w, w_up, ffn_conv_w, ffn_conv_b, w_down, final_norm_w, loss_target, m_norm_mix_w, m_w_in, m_conv_a_w, m_w_a_out, m_ssd_conv_w, m_ssd_conv_b, m_dt_bias, m_a_log, m_d_skip, m_ssd_norm_w, m_w_s_out, m_w_o, m_norm_ffn_w, m_w_up, m_ffn_conv_w, m_ffn_conv_b, m_w_down, m_final_norm_w, v_norm_mix_w, v_w_in, v_conv_a_w, v_w_a_out, v_ssd_conv_w, v_ssd_conv_b, v_dt_bias, v_a_log, v_d_skip, v_ssd_norm_w, v_w_s_out, v_w_o, v_norm_ffn_w, v_w_up, v_ffn_conv_w, v_ffn_conv_b, v_w_down, v_final_norm_w):
    names = ("norm_mix_w", "w_in", "conv_a_w", "w_a_out", "ssd_conv_w", "ssd_conv_b", "dt_bias", "a_log", "d_skip",
             "ssd_norm_w", "w_s_out", "w_o", "norm_ffn_w", "w_up", "ffn_conv_w", "ffn_conv_b", "w_down", "final_norm_w")
    W = dict(zip(names, (norm_mix_w, w_in, conv_a_w, w_a_out, ssd_conv_w, ssd_conv_b, dt_bias, a_log, d_skip,
                         ssd_norm_w, w_s_out, w_o, norm_ffn_w, w_up, ffn_conv_w, ffn_conv_b, w_down, final_norm_w)))
    M = dict(zip(names, (m_norm_mix_w, m_w_in, m_conv_a_w, m_w_a_out, m_ssd_conv_w, m_ssd_conv_b, m_dt_bias, m_a_log,
                         m_d_skip, m_ssd_norm_w, m_w_s_out, m_w_o, m_norm_ffn_w, m_w_up, m_ffn_conv_w, m_ffn_conv_b,
                         m_w_down, m_final_norm_w)))
    V = dict(zip(names, (v_norm_mix_w, v_w_in, v_conv_a_w, v_w_a_out, v_ssd_conv_w, v_ssd_conv_b, v_dt_bias, v_a_log,
                         v_d_skip, v_ssd_norm_w, v_w_s_out, v_w_o, v_norm_ffn_w, v_w_up, v_ffn_conv_w, v_ffn_conv_b,
                         v_w_down, v_final_norm_w)))
    two_d = lambda a: a.reshape(-1, a.shape[-1])
    W2, M2, V2 = ({k: two_d(a) for k, a in t.items()} for t in (W, M, V))
    xi, yi, ci = _coords()
    me = 2 * xi + yi

    meidx = me.reshape(1).astype(jnp.int32)
    state = {}


    class Hooks(_Hooks):
        def before_in_proj(self, w_in):
            return _tie(w_in, state["rest_token"], "tie_ag_rest")

        def late_weights(self, wts, after):
            owns, lands = _split_wait("ag_rest_wait", state["rest"], after, _plan_bcast)
            full = {}
            for (n, rr, cc, axis), own, land in zip(_W_REST, owns, lands):
                slabs = lax.dynamic_update_slice(land, own[None], (me, 0, 0))
                full[n] = slabs.reshape(NCHIP * rr, cc) if axis == 0 else slabs.transpose(1, 0, 2).reshape(rr, NCHIP * cc)
            return {**wts, **full}

        def grads_ready(self, grads, tie):
            if "w_in" in grads:
                key, group, grads = "g_in", _W_IN, {"w_in": _unpermute_w_in(grads["w_in"])}
            else:
                key, group = "g_rest", _W_REST
            packs = [_by_chip(jax.tree.map(lambda t: t.astype(BF16), grads[n]), rr, cc, axis)
                     for n, rr, cc, axis in group]
            lands = [lax.empty((_NCOPY,) + p.shape[1:], BF16) for p in packs]
            state[key], token = _split_start("rs_" + key + "_start", packs, lands, _plan_scatter)
            return _tie(tie, token, "tie_" + key)

        def mark(self, name, value):
            state[name] = value

    def reduced(key, after, group):
        packs, lands = _split_wait("rs_" + key + "_wait", state[key], after, _plan_scatter)
        mines = [_add_slabs(p, l, meidx, "rs_add_chips_" + n) for (n, *_), p, l in zip(group, packs, lands)]
        return dict(zip([n for n, *_ in group], zip(mines, _swap_sibling(mines, "rs_" + key + "_swap"))))

    w_in_full = _ag_weights(W2["w_in"].astype(BF16)).transpose(1, 0, 2).reshape(D, NI)
    wts = {k: W2[k] for k in _SMALL_REPL}
    conv_by_chip = _gather8([W2[n] for n, *_ in _SMALL_CONV], False, "ag_conv_weights")
    for (n, kk, width), stacked in zip(_SMALL_CONV, conv_by_chip):
        wts[n] = stacked.transpose(1, 0, 2).reshape(kk, width)
    rest = [W2[n].astype(BF16) for n, *_ in _W_REST]
    rest[0] = _tie(rest[0], conv_by_chip[0], "tie_ag_order")
    state["rest"], state["rest_token"] = _split_start(
        "ag_rest_start", rest, [lax.empty((NCHIP,) + r.shape, BF16) for r in rest], _plan_bcast)
    wts["w_in"] = _permute_w_in(w_in_full)

    loss8, grad_x, grads = _local_step(x[0], loss_target[0], wts, Hooks())

    gbig = {**reduced("g_rest", state["ssd_bwd"], _W_REST), **reduced("g_in", grad_x, _W_IN)}

    small_parts = [grads[n] for n in _SMALL_REPL] + [loss8[0:1]] + [grads[n] for n, *_ in _SMALL_CONV]
    small_g = _gather8(small_parts, True, "allreduce_small")
    gsm = dict(zip(_SMALL_REPL, small_g[:len(_SMALL_REPL)]))
    loss = small_g[len(_SMALL_REPL)][0, 0]
    for (n, kk, width), gfull in zip(_SMALL_CONV, small_g[len(_SMALL_REPL) + 1:]):
        cw = width // NCHIP
        gsm[n] = lax.dynamic_slice(gfull, (0, me * cw), (kk, cw))

    G, DW, NM, NV = {}, {}, {}, {}
    for n in [b[0] for b in _W_IN + _W_REST]:
        G[n], DW[n], NM[n], NV[n] = _adamw(W2[n], gbig[n], M2[n], V2[n], "adamw_" + n)
    sm_names = list(_SMALL_REPL) + [n for n, *_ in _SMALL_CONV]
    outs = _adamw_small(*([t[n] for n in sm_names] for t in (W2, gsm, M2, V2)))
    for t, vals in zip((DW, NM, NV), outs):
        t.update(zip(sm_names, vals))
    G.update(gsm)

    def shaped(t):
        return [t[n].reshape(W[n].shape) for n in names]

    return (loss, grad_x.reshape(x.shape), *shaped(G), *shaped(DW), *shaped(NM), *shaped(NV))
```

```python
import jax
import jax.numpy as jnp
from jax import lax
from jax.experimental import pallas as pl
from jax.experimental.pallas import tpu as pltpu

F32 = jnp.float32
BF16 = jnp.bfloat16

D = 1024
DI = 2048
NH = 32
HP = 64
NG = 4
NS = 128
CH = 128
DX = 3072
FF = 2816
NI = 10272
EPS = 1e-5

OFF_BCV, OFF_XBC, OFF_G, OFF_Z, OFF_DT = 0, 3072, 6144, 8192, 10240
NIP = 10752
_SEGS = ((0, 2048, OFF_G), (2048, 3072, OFF_BCV), (5120, 2048, OFF_Z), (7168, 3072, OFF_XBC), (10240, 32, OFF_DT))

LANES = 128
HALO = 16
V7X_VMEM_LIMIT = 56 * 2 ** 20

ADAM_LR, ADAM_B1, ADAM_B2, ADAM_EPS, ADAM_WD, ADAM_STEP = 0.001, 0.9, 0.999, 1e-08, 0.01, 10

NN = (((1,), (0,)), ((), ()))
NT = (((1,), (1,)), ((), ()))
TN = (((0,), (0,)), ((), ()))


def _dot(a, b, dims=NN):
    return lax.dot_general(a, b, dims, preferred_element_type=F32)


def _params(sem, **kw):
    return pltpu.CompilerParams(dimension_semantics=sem, vmem_limit_bytes=V7X_VMEM_LIMIT, **kw)


V7X_MXU = 256
V7X_HBM_BYTES_PER_S = 3.5e12
STEP_S = 0.35e-6
MATMUL_VMEM = 40 * 2 ** 20
EPILOGUE_VMEM = 46 * 2 ** 20


ACC_BYTES_PER_S = 1.2e13


def _divisors(dim, cap, units):
    for unit in units:
        c = [t for t in range(unit, min(dim, cap) + 1, unit) if dim % t == 0]
        if c:
            return c
    return [dim]


def _tiles(M, N, K, out_bytes, has_res):
    best = None
    for tn in _divisors(N, 2816, (V7X_MXU, LANES)):
        for tm in _divisors(M, 2816, (LANES,)):
            for tk in _divisors(K, 2816, (V7X_MXU, LANES)):
                nk, ni, nj = K // tk, M // tm, N // tn
                vmem = 4 * (tm * tk + tk * tn) + 2 * tm * tn * out_bytes
                vmem += (4 * tm * tn if nk > 1 else 0) + (8 * tm * tn if has_res else 0)
                if vmem > MATMUL_VMEM:
                    continue
                a_reads = M * K * 2 * (nj if nk > 1 else 1)
                b_reads = K * N * 2 * (ni if nk * nj > 1 else 1)
                cost = (a_reads + b_reads + M * N * out_bytes) / V7X_HBM_BYTES_PER_S + ni * nj * nk * STEP_S
                cost += (nk - 1) * M * N * 8 / ACC_BYTES_PER_S
                if best is None or cost < best[0]:
                    best = (cost, tm, tn, tk)
    assert best is not None, (M, N, K)
    return best[1:]


def _sigmoid(x):
    return 1.0 / (1.0 + jnp.exp(-x))


class _Epilogue:
    def __init__(self, fn, ins, outs, tile_bytes):
        self.fn, self.ins, self.outs, self.tile_bytes = fn, tuple(ins), tuple(outs), tile_bytes


def _matmul(a, b, *, mode, out_dtype, name, residual=None, b_k_off=0, epilogue=None):
    if mode == "nn":
        (M, K), (K2, N) = a.shape, b.shape
    elif mode == "nt":
        (M, K), (N, K2) = a.shape, (b.shape[0], a.shape[1])
        assert b_k_off + K <= b.shape[1]
    else:
        (K, M), (K2, N) = a.shape, b.shape
    assert K == K2, (name, a.shape, b.shape)
    tm, tn, tk = _tiles(M, N, K, jnp.dtype(out_dtype).itemsize, residual is not None)
    if epilogue is not None:
        tn = N
        fits = [(K * N * 2 * (M // t) / V7X_HBM_BYTES_PER_S + (K // q - 1) * M * N * 8 / ACC_BYTES_PER_S
                 + (M // t) * (K // q) * STEP_S, t, q)
                for t in (1024, 512, 256) if M % t == 0 for q in _divisors(K, 2816, (V7X_MXU, LANES))
                if 4 * (t * q + q * tn) + (4 * t * tn if K > q else 0) + (8 * t * tn if residual is not None else 0)
                + 2 * t * epilogue.tile_bytes <= EPILOGUE_VMEM]
        _, tm, tk = min(fits)
    nk = K // tk
    if mode == "tn":
        a_spec = pl.BlockSpec((tk, tm), lambda i, j, k: (k, i))
    else:
        a_spec = pl.BlockSpec((tm, tk), lambda i, j, k: (i, k))
    if mode == "nt":
        assert b_k_off % tk == 0
        b_spec = pl.BlockSpec((tn, tk), lambda i, j, k: (j, k + b_k_off // tk))
    else:
        b_spec = pl.BlockSpec((tk, tn), lambda i, j, k: (k, j))
    dims = {"nn": NN, "nt": NT, "tn": TN}[mode]
    o_spec = pl.BlockSpec((tm, tn), lambda i, j, k: (i, j))
    has_res = residual is not None

    def rows_or_whole(shape):
        if shape[0] == M:
            return pl.BlockSpec((tm,) + tuple(shape[1:]), lambda i, j, k: (i,) + (0,) * (len(shape) - 1))
        return pl.BlockSpec(tuple(shape), lambda i, j, k: (0,) * len(shape))

    n_in = 2 + has_res + (len(epilogue.ins) if epilogue else 0)
    n_out = len(epilogue.outs) if epilogue else 1

    def body(*refs):
        a_ref, b_ref = refs[:2]
        r_ref = refs[2] if has_res else None
        out_refs = refs[n_in:n_in + n_out]
        acc_ref = refs[-1]
        k = pl.program_id(2)
        part = _dot(a_ref[...], b_ref[...], dims)

        def finish(r):
            if has_res:
                r = r + r_ref[...].astype(F32)
            if epilogue is None:
                out_refs[0][...] = r.astype(out_dtype)
            else:
                epilogue.fn(r, refs[2 + has_res:n_in], out_refs, pl.program_id(0) == 0)

        if nk == 1:
            finish(part)
            return

        @pl.when(k == 0)
        def _():
            acc_ref[...] = part

        @pl.when(jnp.logical_and(k > 0, k < nk - 1))
        def _():
            acc_ref[...] += part

        @pl.when(k == nk - 1)
        def _():
            finish(acc_ref[...] + part)

    in_specs = [a_spec, b_spec] + ([o_spec] if has_res else [])
    args = (a, b) + ((residual,) if has_res else ())
    if epilogue is None:
        out_specs, out_shape = o_spec, jax.ShapeDtypeStruct((M, N), out_dtype)
        sem = ("parallel", "parallel", "arbitrary")
    else:
        in_specs += [rows_or_whole(x.shape) for x in epilogue.ins]
        args += epilogue.ins
        out_specs = [rows_or_whole(shp) for shp, _ in epilogue.outs]
        out_shape = [jax.ShapeDtypeStruct(shp, dt) for shp, dt in epilogue.outs]
        sem = ("arbitrary", "arbitrary", "arbitrary")
    return pl.pallas_call(
        body, name=name, grid=(M // tm, N // tn, nk), in_specs=in_specs, out_specs=out_specs,
        out_shape=out_shape, scratch_shapes=[pltpu.VMEM((tm, tn), F32)] if nk > 1 else [],
        compiler_params=_params(sem),
    )(*args)


class _Rows:
    def __init__(self, T, tm):
        self.T, self.tm = T, min(tm, T // 2)
        self.nrow = T // self.tm
        self.r = self.tm // HALO
        self.nb = T // HALO

    def tile(self, w, cb=0, step=1):
        return pl.BlockSpec((self.tm, w), lambda j, i: (i, cb + step * j))

    def prev(self, w, cb=0, step=1):
        r = self.r
        return pl.BlockSpec((HALO, w), lambda j, i: (jnp.maximum(i * r - 1, 0), cb + step * j))

    def next(self, w, cb=0, step=1):
        r, nb = self.r, self.nb
        return pl.BlockSpec((HALO, w), lambda j, i: (jnp.minimum((i + 1) * r, nb - 1), cb + step * j))

    def colvec(self, k, w, cb=0, step=1):
        return pl.BlockSpec((k, w), lambda j, i: (0, cb + step * j))

    def call(self, body, name, ncol, in_specs, out_specs, out_shape, args, aliases=None):
        return pl.pallas_call(
            body, name=name, grid=(ncol, self.nrow), in_specs=in_specs, out_specs=out_specs,
            out_shape=out_shape, input_output_aliases=aliases or {},
            compiler_params=_params(("parallel", "arbitrary")),
        )(*args)


ANY = pl.BlockSpec(memory_space=pl.ANY)


def _shifts_causal(ext, nk, tm):
    out = []
    for k in range(nk):
        s = nk - 1 - k
        r = ext if s == 0 else pltpu.roll(ext, s, 0)
        out.append(r[HALO:])
    return out


def _shifts_anticausal(ext, nk, tm):
    n = ext.shape[0]
    out = []
    for k in range(nk):
        s = nk - 1 - k
        r = ext if s == 0 else pltpu.roll(ext, n - s, 0)
        out.append(r[:tm])
    return out


def _wsum(w, parts):
    acc = w[0:1, :] * parts[0]
    for k in range(1, len(parts)):
        acc = acc + w[k:k + 1, :] * parts[k]
    return acc


def _colsum(x):
    return jnp.sum(x, axis=0, keepdims=True)


def _acc_out(ref, val, first):
    @pl.when(first)
    def _():
        ref[...] = val

    @pl.when(jnp.logical_not(first))
    def _():
        ref[...] += val


def _acc_rows(ref, rows, first):
    for k, r in enumerate(rows):
        _acc_out(ref.at[k:k + 1, :], r, first)


def _norm_matmul(x, wn, b, name, b_f32=None):
    T, N = x.shape[0], b.shape[1]
    tm = min(1024, T)
    tn = max(t for t in _divisors(N, 2816, (V7X_MXU, LANES))
             if 8 * tm * D + 6 * tm * D + 4 * D * t + 4 * tm * t <= MATMUL_VMEM)

    extra = b_f32 is not None

    def body(*refs):
        x_ref, wn_ref, b_ref = refs[:3]
        o_ref, u_ref = refs[3 + extra:5 + extra]
        keep_ref = refs[-1]

        @pl.when(pl.program_id(1) == 0)
        def _():
            xv = x_ref[...]
            r = lax.rsqrt(jnp.mean(xv * xv, axis=-1, keepdims=True) + EPS)
            u = (xv * r * wn_ref[...]).astype(BF16)
            keep_ref[...] = u
            u_ref[...] = u
            if extra:
                refs[5 + extra][...] = _dot(u, refs[3][...])

        o_ref[...] = _dot(keep_ref[...], b_ref[...]).astype(BF16)

    rows = pl.BlockSpec((tm, D), lambda i, j: (i, 0))
    whole = lambda shape: pl.BlockSpec(shape, lambda i, j: (0, 0))
    narrow = pl.BlockSpec((tm, LANES), lambda i, j: (i, 0))
    return pl.pallas_call(
        body, name=name, grid=(T // tm, N // tn),
        in_specs=[rows, whole((1, D)), pl.BlockSpec((D, tn), lambda i, j: (0, j))] + [whole((D, LANES))] * extra,
        out_specs=[pl.BlockSpec((tm, tn), lambda i, j: (i, j)), rows] + [narrow] * extra,
        out_shape=[jax.ShapeDtypeStruct((T, N), BF16), jax.ShapeDtypeStruct((T, D), BF16)]
        + [jax.ShapeDtypeStruct((T, LANES), F32)] * extra,
        scratch_shapes=[pltpu.VMEM((tm, D), BF16)],
        compiler_params=_params(("parallel", "arbitrary")),
    )(*((x, wn, b) + ((b_f32,) if extra else ())))


def _rmsnorm_bwd_epilogue(x, w, dres):
    T = x.shape[0]

    def fn(dyv, ins, outs, first):
        x_ref, w_ref, dr_ref = ins
        dx_ref, dxb_ref, dw_ref = outs
        xv = x_ref[...]
        r = lax.rsqrt(jnp.mean(xv * xv, axis=-1, keepdims=True) + EPS)
        xh = xv * r
        dxh = dyv * w_ref[...]
        dx = r * (dxh - xh * jnp.mean(dxh * xh, axis=-1, keepdims=True)) + dr_ref[...]
        dx_ref[...] = dx
        dxb_ref[...] = dx.astype(BF16)
        _acc_out(dw_ref, _colsum(dyv * xh), first)

    return _Epilogue(fn, (x, w, dres), (((T, D), F32), ((T, D), BF16), ((1, D), F32)), 14 * D)


def _branch_a_fwd(proj, conv_w):
    T = proj.shape[0]
    R = _Rows(T, 512)
    tm = R.tm

    def body(p_ref, pp_ref, w_ref, o_ref):
        keep = (pl.program_id(1) > 0).astype(F32)
        cv = p_ref[:, D:2 * D].astype(F32) * p_ref[:, 2 * D:].astype(F32)
        cvp = pp_ref[:, D:2 * D].astype(F32) * pp_ref[:, 2 * D:].astype(F32) * keep
        sh = _shifts_causal(jnp.concatenate([cvp, cv], axis=0), 3, tm)
        ca = _wsum(w_ref[...], sh)
        o_ref[...] = (p_ref[:, :D].astype(F32) * ca).astype(BF16)

    return R.call(body, "branch_a_fwd", 1, [R.tile(3 * D), R.prev(3 * D), R.colvec(3, D)], R.tile(D),
                  jax.ShapeDtypeStruct((T, D), BF16), (proj, proj, conv_w))


def _branch_a_bwd(dya_in, proj, conv_w, dproj):
    T = proj.shape[0]
    R = _Rows(T, 256)
    tm = R.tm

    def body(d_ref, dn_ref, p_ref, pp_ref, pn_ref, w_ref, _alias, o_ref, dw_ref):
        i = pl.program_id(1)
        keep_p = (i > 0).astype(F32)
        keep_n = (i < R.nrow - 1).astype(F32)
        w = w_ref[...]
        b = p_ref[:, :D].astype(F32)
        c = p_ref[:, D:2 * D].astype(F32)
        v = p_ref[:, 2 * D:].astype(F32)
        cvp = pp_ref[:, D:2 * D].astype(F32) * pp_ref[:, 2 * D:].astype(F32) * keep_p
        sh = _shifts_causal(jnp.concatenate([cvp, c * v], axis=0), 3, tm)
        ca = _wsum(w, sh)
        d = d_ref[...].astype(F32)
        dca = d * b
        dca_n = dn_ref[...].astype(F32) * pn_ref[:, :D].astype(F32) * keep_n
        dsh = _shifts_anticausal(jnp.concatenate([dca, dca_n], axis=0), 3, tm)
        dcv = _wsum(w, dsh)
        o_ref[:, :D] = (d * ca).astype(BF16)
        o_ref[:, D:2 * D] = (dcv * v).astype(BF16)
        o_ref[:, 2 * D:] = (dcv * c).astype(BF16)
        _acc_rows(dw_ref, [_colsum(dca * s) for s in sh], i == 0)

    return R.call(
        body, "branch_a_bwd", 1,
        [R.tile(D), R.next(D), R.tile(3 * D), R.prev(3 * D), R.next(3 * D), R.colvec(3, D), ANY],
        [R.tile(3 * D), R.colvec(3, D)],
        [jax.ShapeDtypeStruct(dproj.shape, BF16), jax.ShapeDtypeStruct((3, D), F32)],
        (dya_in, dya_in, proj, proj, proj, conv_w, dproj), aliases={6: 0})


_XW = 512


def _xbc_fwd(proj, conv_w, conv_b):
    T = proj.shape[0]
    R = _Rows(T, 512)
    tm = R.tm
    cb = OFF_XBC // _XW

    def body(x_ref, xp_ref, w_ref, b_ref, o_ref):
        keep = (pl.program_id(1) > 0).astype(F32)
        ext = jnp.concatenate([xp_ref[...].astype(F32) * keep, x_ref[...].astype(F32)], axis=0)
        pre = _wsum(w_ref[...], _shifts_causal(ext, 4, tm)) + b_ref[...]
        o_ref[...] = (pre * _sigmoid(pre)).astype(BF16)

    return R.call(body, "xbc_fwd", DX // _XW,
                  [R.tile(_XW, cb), R.prev(_XW, cb), R.colvec(4, _XW), R.colvec(1, _XW)], R.tile(_XW),
                  jax.ShapeDtypeStruct((T, DX), BF16), (proj, proj, conv_w, conv_b))


def _xbc_bwd(dact, proj, conv_w, conv_b, dproj):
    T = proj.shape[0]
    R = _Rows(T, 512)
    tm = R.tm
    cb = OFF_XBC // _XW

    def body(d_ref, dn_ref, x_ref, xp_ref, xn_ref, w_ref, b_ref, _alias, o_ref, dw_ref, db_ref):
        i = pl.program_id(1)
        keep_p = (i > 0).astype(F32)
        keep_n = (i < R.nrow - 1).astype(F32)
        w = w_ref[...]
        ext = jnp.concatenate([xp_ref[...].astype(F32) * keep_p, x_ref[...].astype(F32),
                               xn_ref[...].astype(F32)], axis=0)
        sh = _shifts_causal(ext, 4, tm + HALO)
        pre = _wsum(w, sh) + b_ref[...]
        s = _sigmoid(pre)
        dsilu = s * (1.0 + pre * (1.0 - s))
        dext = jnp.concatenate([d_ref[...].astype(F32), dn_ref[...].astype(F32) * keep_n], axis=0)
        dpre = dext * dsilu
        dsh = _shifts_anticausal(dpre, 4, tm)
        o_ref[...] = _wsum(w, dsh).astype(BF16)
        dp = dpre[:tm]
        _acc_rows(dw_ref, [_colsum(dp * q[:tm]) for q in sh], i == 0)
        _acc_out(db_ref, _colsum(dp), i == 0)

    return R.call(
        body, "xbc_bwd", DX // _XW,
        [R.tile(_XW), R.next(_XW), R.tile(_XW, cb), R.prev(_XW, cb), R.next(_XW, cb),
         R.colvec(4, _XW), R.colvec(1, _XW), ANY],
        [R.tile(_XW, cb), R.colvec(4, _XW), R.colvec(1, _XW)],
        [jax.ShapeDtypeStruct(dproj.shape, BF16), jax.ShapeDtypeStruct((4, DX), F32),
         jax.ShapeDtypeStruct((1, DX), F32)],
        (dact, dact, proj, proj, proj, conv_w, conv_b, dproj), aliases={7: 0})


def _softplus(x):
    return jnp.maximum(x, 0.0) + jnp.log(1.0 + jnp.exp(-jnp.abs(x)))


def _dt_rows(T):
    return min(8 * CH, T // 2)


def _dt_fwd(dt_raw, dt_bias_p, a_log_p):
    T = dt_raw.shape[0]
    rows = _dt_rows(T)

    def body(r_ref, b_ref, al_ref, dt_ref, ac_ref, acT_ref):
        dt = _softplus(r_ref[...] + b_ref[...])
        s = dt * (-jnp.exp(al_ref[...]))
        row = lax.broadcasted_iota(jnp.int32, (rows, LANES), 0) % CH
        k = 1
        while k < CH:
            s = s + jnp.where(row >= k, pltpu.roll(s, k, 0), 0.0)
            k *= 2
        dt_ref[...] = dt
        ac_ref[...] = s
        for q in range(0, rows, CH):
            acT_ref[q:q + CH] = s[q:q + CH].T

    blk = pl.BlockSpec((rows, LANES), lambda i: (i, 0))
    vec = pl.BlockSpec((1, LANES), lambda i: (0, 0))
    return pl.pallas_call(
        body, name="dt_fwd", grid=(T // rows,), in_specs=[blk, vec, vec], out_specs=[blk, blk, blk],
        out_shape=[jax.ShapeDtypeStruct((T, LANES), F32)] * 3, compiler_params=_params(("parallel",)),
    )(dt_raw, dt_bias_p, a_log_p)


def _dt_bwd(dacum, ddt_x, dt_raw, dt_bias_p, a_log_p, dproj):
    T = dt_raw.shape[0]
    rows = _dt_rows(T)
    nc = T // rows

    def body(da_ref, dx_ref, r_ref, b_ref, al_ref, _alias, o_ref, db_ref, dal_ref):
        i = pl.program_id(0)
        a = -jnp.exp(al_ref[...])
        z = r_ref[...] + b_ref[...]
        dt = _softplus(z)
        s = da_ref[...]
        row = lax.broadcasted_iota(jnp.int32, (rows, LANES), 0) % CH
        k = 1
        while k < CH:
            s = s + jnp.where(row < CH - k, pltpu.roll(s, rows - k, 0), 0.0)
            k *= 2
        ddt = s * a + dx_ref[...]
        draw = ddt * _sigmoid(z)
        o_ref[:, :LANES] = draw.astype(BF16)
        o_ref[:, LANES:] = jnp.zeros((rows, NIP - OFF_DT - LANES), BF16)
        _acc_out(db_ref, _colsum(draw), i == 0)
        _acc_out(dal_ref, _colsum(s * dt), i == 0)

        @pl.when(i == nc - 1)
        def _():
            dal_ref[...] = dal_ref[...] * a

    blk = pl.BlockSpec((rows, LANES), lambda i: (i, 0))
    vec = pl.BlockSpec((1, LANES), lambda i: (0, 0))
    oblk = pl.BlockSpec((rows, NIP - OFF_DT), lambda i: (i, OFF_DT // (NIP - OFF_DT)))
    return pl.pallas_call(
        body, name="dt_bwd", grid=(nc,), in_specs=[blk, blk, blk, vec, vec, ANY], out_specs=[oblk, vec, vec],
        out_shape=[jax.ShapeDtypeStruct(dproj.shape, BF16), jax.ShapeDtypeStruct((1, LANES), F32),
                   jax.ShapeDtypeStruct((1, LANES), F32)],
        input_output_aliases={5: 0}, compiler_params=_params(("arbitrary",)),
    )(dacum, ddt_x, dt_raw, dt_bias_p, a_log_p, dproj)


_GW = DI // NG
_HG = NH // NG
_NEG = -1e30


def _interleave(gens):
    out, live = [None] * len(gens), list(range(len(gens)))
    while live:
        for i in list(live):
            try:
                next(gens[i])
            except StopIteration as stop:
                out[i] = stop.value
                live.remove(i)
    return out


def _pair_lanes(left, v0, v1):
    return jnp.where(left, v0, v1)


def _ssd_specs(T, rev):
    nc = T // CH
    cm = (lambda c: nc - 1 - c) if rev else (lambda c: c)
    bw = NG * NS
    return dict(
        xs=pl.BlockSpec((CH, DI), lambda c: (cm(c), 0)),
        bm=pl.BlockSpec((CH, bw), lambda c: (cm(c), DI // bw)),
        cmat=pl.BlockSpec((CH, bw), lambda c: (cm(c), DI // bw + 1)),
        xbc=pl.BlockSpec((CH, DX), lambda c: (cm(c), 0)),
        col=pl.BlockSpec((CH, LANES), lambda c: (cm(c), 0)),
        dsk=pl.BlockSpec((1, DI), lambda c: (0, 0)),
        state=pl.BlockSpec((1, NS, DI), lambda c: (cm(c), 0, 0)),
    )


def _last(ref, lo, hi):
    return ref.at[(slice(None),) * (len(ref.shape) - 1) + (slice(lo, hi),)]


def _group_views(g, wide, narrow):
    return [_last(r, g * _GW, (g + 1) * _GW) for r in wide] + [_last(r, g * NS, (g + 1) * NS) for r in narrow]


def _ssd_fwd(xact, dt, acum, acumT, dsk_rep):
    T = xact.shape[0]
    nc = T // CH
    sp = _ssd_specs(T, False)

    def body(*refs):
        xs, bm, cmat, dtr, acr, actr, dsk, y, spv, S_ref = refs

        @pl.when(pl.program_id(0) == 0)
        def _():
            S_ref[...] = jnp.zeros_like(S_ref)

        _interleave([group(g * _HG, dtr[...], acr[...], actr[...],
                           *_group_views(g, (xs, dsk, y, spv, S_ref), (bm, cmat))) for g in range(NG)])

    def group(hb, dt, ac, acT, xs_ref, dsk_ref, y_ref, sp_ref, S_ref, b_ref, c_ref):
        Bm, Cm = b_ref[...], c_ref[...]
        S = S_ref[...]
        sp_ref[0] = S
        cb = _dot(Cm, Bm, NT)
        CS = _dot(Cm, S.astype(BF16))
        row = lax.broadcasted_iota(jnp.int32, (CH, CH), 0)
        col = lax.broadcasted_iota(jnp.int32, (CH, CH), 1)
        tril = row >= col
        left = col < HP
        xd_parts, dec_parts = [], []
        for p in range(_HG // 2):
            sl = slice(p * LANES, (p + 1) * LANES)
            j0, j1 = hb + 2 * p, hb + 2 * p + 1
            xp = xs_ref[:, sl].astype(F32)
            a0, a1 = ac[:, j0:j0 + 1], ac[:, j1:j1 + 1]
            al0, al1 = ac[CH - 1:CH, j0:j0 + 1], ac[CH - 1:CH, j1:j1 + 1]
            X = xp * _pair_lanes(left, dt[:, j0:j0 + 1], dt[:, j1:j1 + 1])
            Xb = X.astype(BF16)
            Ws = [(cb * jnp.exp(jnp.where(tril, aj - acT[j:j + 1, :], _NEG))).astype(BF16)
                  for j, aj in ((j0, a0), (j1, a1))]
            Xs = [jnp.where(m, Xb, jnp.zeros_like(Xb)) for m in (left, jnp.logical_not(left))]
            yield
            yd = _dot(jnp.concatenate(Ws, axis=1), jnp.concatenate(Xs, axis=0))
            yield
            eal = _pair_lanes(left, jnp.exp(a0), jnp.exp(a1))
            y = yd + eal * CS[:, sl] + dsk_ref[:, sl] * xp
            y_ref[:, sl] = y.astype(BF16)
            xd_parts.append(X * _pair_lanes(left, jnp.exp(al0 - a0), jnp.exp(al1 - a1)))
            dec_parts.append(_pair_lanes(left[0:1], jnp.exp(al0), jnp.exp(al1)))
        Xd = jnp.concatenate(xd_parts, axis=1).astype(BF16)
        dec = jnp.concatenate(dec_parts, axis=1)
        S_ref[...] = dec * S + _dot(Bm, Xd, TN)

    return pl.pallas_call(
        body, name="ssd_fwd", grid=(nc,),
        in_specs=[sp["xs"], sp["bm"], sp["cmat"], sp["col"], sp["col"], sp["col"], sp["dsk"]],
        out_specs=[sp["xs"], sp["state"]],
        out_shape=[jax.ShapeDtypeStruct((T, DI), BF16), jax.ShapeDtypeStruct((nc, NS, DI), F32)],
        scratch_shapes=[pltpu.VMEM((NS, DI), F32)],
        compiler_params=_params(("arbitrary",)),
    )(xact, xact, xact, dt, acum, acumT, dsk_rep)


def _ssd_bwd(dy, xact, dt, acum, acumT, dsk_rep, sprev):
    T = xact.shape[0]
    nc = T // CH
    sp = _ssd_specs(T, True)

    def body(*refs):
        xs, bm, cmat, dtr, acr, actr, dsk, dyr, spv, dxa, ddtx, dAc, dskacc, dS_ref = refs
        first = pl.program_id(0) == 0

        @pl.when(first)
        def _():
            dS_ref[...] = jnp.zeros_like(dS_ref)

        dbc = _last(dxa, DI, DX)
        ddtx_sum = jnp.zeros((CH, LANES), F32)
        dAc_sum = jnp.zeros((CH, LANES), F32)
        for a, b in _interleave([group(first, g * _HG, dtr[...], acr[...], actr[...],
                                       *_group_views(g, (xs, dsk, dyr, spv, dxa, dskacc, dS_ref),
                                                     (bm, cmat, dbc, _last(dbc, NG * NS, 2 * NG * NS))))
                                 for g in range(NG)]):
            ddtx_sum, dAc_sum = ddtx_sum + a, dAc_sum + b
        ddtx[...] = ddtx_sum
        dAc[...] = dAc_sum

    def group(first, hb, dt, ac, acT, xs_ref, dsk_ref, dy_ref, sp_ref, dx_ref, dskacc_ref, dS_ref, b_ref, c_ref,
              dB_ref, dC_ref):
        Bm, Cm = b_ref[...], c_ref[...]
        S = sp_ref[0]
        dS = dS_ref[...]
        Sb, dSb = S.astype(BF16), dS.astype(BF16)
        cb = _dot(Cm, Bm, NT)
        cbT = _dot(Bm, Cm, NT)
        CmT = Cm.T
        CS = _dot(Cm, Sb)
        T1 = _dot(Bm, dSb)
        yield
        row = lax.broadcasted_iota(jnp.int32, (CH, CH), 0)
        col = lax.broadcasted_iota(jnp.int32, (CH, CH), 1)
        tril = row >= col
        triu = row <= col
        left = col < HP
        lane8 = lax.broadcasted_iota(jnp.int32, (1, LANES), 1)
        lastrow = lax.broadcasted_iota(jnp.int32, (CH, 1), 0) == CH - 1
        dCB = jnp.zeros((CH, CH), F32)
        dCBT = jnp.zeros((CH, CH), F32)
        dAc = jnp.zeros((CH, LANES), F32)
        ddtx = jnp.zeros((CH, LANES), F32)
        xd_parts, dye_parts, dec_parts, dsk_parts = [], [], [], []
        for p in range(_HG // 2):
            sl = slice(p * LANES, (p + 1) * LANES)
            j0, j1 = hb + 2 * p, hb + 2 * p + 1
            xp = xs_ref[:, sl].astype(F32)
            dyp = dy_ref[:, sl].astype(F32)
            a0, a1 = ac[:, j0:j0 + 1], ac[:, j1:j1 + 1]
            al0, al1 = ac[CH - 1:CH, j0:j0 + 1], ac[CH - 1:CH, j1:j1 + 1]
            dtl = _pair_lanes(left, dt[:, j0:j0 + 1], dt[:, j1:j1 + 1])
            X = xp * dtl
            Xb = X.astype(BF16)
            eal = _pair_lanes(left, jnp.exp(a0), jnp.exp(a1))
            dtel = _pair_lanes(left, jnp.exp(al0 - a0), jnp.exp(al1 - a1))
            T1p = T1[:, sl]
            Rm = T1p * dtel * X
            GR = dyp * (eal * CS[:, sl]) - Rm
            SdS = dS[:, sl] * S[:, sl]
            dXd = jnp.zeros((CH, LANES), F32)
            for j, aj, alj, mask in ((j0, a0, al0, left), (j1, a1, al1, jnp.logical_not(left))):
                dYm = jnp.where(mask, dyp, 0.0).astype(BF16)
                dWm = _dot(dYm, Xb, NT)
                dWmT = _dot(Xb, dYm, NT)
                yield
                e = aj - acT[j:j + 1, :]
                P = dWm * jnp.exp(jnp.where(tril, e, _NEG))
                LmT = jnp.exp(jnp.where(triu, -e, _NEG))
                PT = dWmT * LmT
                dCB = dCB + P
                dCBT = dCBT + PT
                yield
                dXd = dXd + _dot((cbT * LmT).astype(BF16), dYm)
                qd = P * cb - PT * cbT + jnp.where(mask, GR, 0.0)
                colv = jnp.sum(qd, axis=1, keepdims=True)
                tot = jnp.where(mask, Rm + jnp.exp(alj) * SdS, 0.0)
                dalast = jnp.sum(jnp.sum(tot, axis=0, keepdims=True), axis=1, keepdims=True)
                dAc = dAc + (colv + jnp.where(lastrow, dalast, 0.0)) * (lane8 == j).astype(F32)
                yield
            dX = dXd + dtel * T1p
            dXx = dX * xp
            for j, mask in ((j0, left), (j1, jnp.logical_not(left))):
                dd = jnp.sum(jnp.where(mask, dXx, 0.0), axis=1, keepdims=True)
                ddtx = ddtx + dd * (lane8 == j).astype(F32)
            dx_ref[:, sl] = (dX * dtl + dsk_ref[:, sl] * dyp).astype(BF16)
            dsk_parts.append(_colsum(dyp * xp))
            xd_parts.append(X * dtel)
            dye_parts.append(dyp * eal)
            dec_parts.append(_pair_lanes(left[0:1], jnp.exp(al0), jnp.exp(al1)))
            yield
        Xd = jnp.concatenate(xd_parts, axis=1).astype(BF16)
        dYe = jnp.concatenate(dye_parts, axis=1).astype(BF16)
        dec = jnp.concatenate(dec_parts, axis=1)
        dC_ref[...] = (_dot(dCB.astype(BF16), Bm) + _dot(dYe, Sb, NT)).astype(BF16)
        dB_ref[...] = (_dot(dCBT.astype(BF16), Cm) + _dot(Xd, dSb, NT)).astype(BF16)
        dS_ref[...] = _dot(CmT, dYe) + dec * dS
        _acc_out(dskacc_ref, jnp.concatenate(dsk_parts, axis=1), first)
        return ddtx, dAc

    return pl.pallas_call(
        body, name="ssd_bwd", grid=(nc,),
        in_specs=[sp["xs"], sp["bm"], sp["cmat"], sp["col"], sp["col"], sp["col"], sp["dsk"], sp["xs"],
                  sp["state"]],
        out_specs=[sp["xbc"], sp["col"], sp["col"], sp["dsk"]],
        out_shape=[jax.ShapeDtypeStruct((T, DX), BF16), jax.ShapeDtypeStruct((T, LANES), F32),
                   jax.ShapeDtypeStruct((T, LANES), F32), jax.ShapeDtypeStruct((1, DI), F32)],
        scratch_shapes=[pltpu.VMEM((NS, DI), F32)],
        compiler_params=_params(("arbitrary",)),
    )(xact, xact, xact, dt, acum, acumT, dsk_rep, dy, sprev)


def _gnorm_fwd(y, proj, w):
    T = y.shape[0]
    R = _Rows(T, 1024)
    zb = OFF_Z // _GW

    def body(y_ref, z_ref, w_ref, o_ref):
        z = z_ref[...].astype(F32)
        yf = y_ref[...].astype(F32) * z * _sigmoid(z)
        r = lax.rsqrt(jnp.mean(yf * yf, axis=-1, keepdims=True) + EPS)
        o_ref[...] = (yf * r * w_ref[...]).astype(BF16)

    return R.call(body, "gnorm_fwd", NG, [R.tile(_GW), R.tile(_GW, zb), R.colvec(1, _GW)], R.tile(_GW),
                  jax.ShapeDtypeStruct((T, DI), BF16), (y, proj, w))


def _gnorm_bwd(dn, y, proj, w, dproj):
    T = y.shape[0]
    R = _Rows(T, 1024)
    zb = OFF_Z // _GW

    def body(dn_ref, y_ref, z_ref, w_ref, _alias, dz_ref, dy_ref, dw_ref):
        z = z_ref[...].astype(F32)
        yv = y_ref[...].astype(F32)
        s = _sigmoid(z)
        silu = z * s
        yf = yv * silu
        r = lax.rsqrt(jnp.mean(yf * yf, axis=-1, keepdims=True) + EPS)
        yh = yf * r
        dnv = dn_ref[...].astype(F32)
        dyh = dnv * w_ref[...]
        dyf = r * (dyh - yh * jnp.mean(dyh * yh, axis=-1, keepdims=True))
        dy_ref[...] = (dyf * silu).astype(BF16)
        dz_ref[...] = (dyf * yv * s * (1.0 + z * (1.0 - s))).astype(BF16)
        _acc_out(dw_ref, _colsum(dnv * yh), pl.program_id(1) == 0)

    return R.call(
        body, "gnorm_bwd", NG, [R.tile(_GW), R.tile(_GW), R.tile(_GW, zb), R.colvec(1, _GW), ANY],
        [R.tile(_GW, zb), R.tile(_GW), R.colvec(1, _GW)],
        [jax.ShapeDtypeStruct(dproj.shape, BF16), jax.ShapeDtypeStruct((T, DI), BF16),
         jax.ShapeDtypeStruct((1, DI), F32)],
        (dn, y, proj, w, dproj), aliases={4: 0})


def _merge_fwd(proj, ya, ys):
    T = proj.shape[0]
    R = _Rows(T, 512)
    gb = OFF_G // (2 * D)

    def body(g_ref, ya_ref, ys_ref, o_ref):
        ga = _sigmoid(g_ref[:, :D].astype(F32))
        gs = _sigmoid(g_ref[:, D:].astype(F32))
        o_ref[...] = (ga * ya_ref[...].astype(F32) + gs * ys_ref[...].astype(F32)).astype(BF16)

    return R.call(body, "merge_fwd", 1, [R.tile(2 * D, gb), R.tile(D), R.tile(D)], R.tile(D),
                  jax.ShapeDtypeStruct((T, D), BF16), (proj, ya, ys))


def _merge_bwd(dm, proj, ya, ys, ncols):
    T = proj.shape[0]
    R = _Rows(T, 256)
    gb = OFF_G // (2 * D)

    def body(dm_ref, g_ref, ya_ref, ys_ref, dg_ref, dya_ref, dys_ref):
        d = dm_ref[...].astype(F32)
        ga = _sigmoid(g_ref[:, :D].astype(F32))
        gs = _sigmoid(g_ref[:, D:].astype(F32))
        dya_ref[...] = (d * ga).astype(BF16)
        dys_ref[...] = (d * gs).astype(BF16)
        dg_ref[:, :D] = (d * ya_ref[...].astype(F32) * ga * (1.0 - ga)).astype(BF16)
        dg_ref[:, D:] = (d * ys_ref[...].astype(F32) * gs * (1.0 - gs)).astype(BF16)

    return R.call(
        body, "merge_bwd", 1, [R.tile(D), R.tile(2 * D, gb), R.tile(D), R.tile(D)],
        [R.tile(2 * D, gb), R.tile(D), R.tile(D)],
        [jax.ShapeDtypeStruct((T, ncols), BF16), jax.ShapeDtypeStruct((T, D), BF16),
         jax.ShapeDtypeStruct((T, D), BF16)],
        (dm, proj, ya, ys))


_FW = 1408
_FB = FF // _FW


def _ffn_act_fwd(hv, conv_w, conv_b):
    T = hv.shape[0]
    R = _Rows(T, 256)
    tm = R.tm

    def body(h1_ref, h1p_ref, h3_ref, w_ref, b_ref, o_ref):
        keep = (pl.program_id(1) > 0).astype(F32)
        ext = jnp.concatenate([h1p_ref[...].astype(F32) * keep, h1_ref[...].astype(F32)], axis=0)
        pre = _wsum(w_ref[...], _shifts_causal(ext, 3, tm)) + b_ref[...]
        o_ref[...] = (pre * _sigmoid(pre) * h3_ref[...].astype(F32)).astype(BF16)

    return R.call(body, "ffn_act_fwd", _FB,
                  [R.tile(_FW), R.prev(_FW), R.tile(_FW, _FB), R.colvec(3, _FW), R.colvec(1, _FW)],
                  R.tile(_FW), jax.ShapeDtypeStruct((T, FF), BF16), (hv, hv, hv, conv_w, conv_b))


def _ffn_act_bwd(dg, hv, conv_w, conv_b):
    T = hv.shape[0]
    R = _Rows(T, 256)
    tm = R.tm

    def body(dg_ref, h1_ref, h1p_ref, h3_ref, w_ref, b_ref, dh3_ref, dpre_ref, dw_ref, db_ref):
        i = pl.program_id(1)
        keep = (i > 0).astype(F32)
        ext = jnp.concatenate([h1p_ref[...].astype(F32) * keep, h1_ref[...].astype(F32)], axis=0)
        sh = _shifts_causal(ext, 3, tm)
        pre = _wsum(w_ref[...], sh) + b_ref[...]
        s = _sigmoid(pre)
        d = dg_ref[...].astype(F32)
        dh3_ref[...] = (d * pre * s).astype(BF16)
        dpre = d * h3_ref[...].astype(F32) * s * (1.0 + pre * (1.0 - s))
        dpre_ref[...] = dpre.astype(BF16)
        _acc_rows(dw_ref, [_colsum(dpre * q) for q in sh], i == 0)
        _acc_out(db_ref, _colsum(dpre), i == 0)

    return R.call(
        body, "ffn_act_bwd", _FB,
        [R.tile(_FW), R.tile(_FW), R.prev(_FW), R.tile(_FW, _FB), R.colvec(3, _FW), R.colvec(1, _FW)],
        [R.tile(_FW), R.tile(_FW), R.colvec(3, _FW), R.colvec(1, _FW)],
        [jax.ShapeDtypeStruct((T, FF), BF16), jax.ShapeDtypeStruct((T, FF), BF16),
         jax.ShapeDtypeStruct((3, FF), F32), jax.ShapeDtypeStruct((1, FF), F32)],
        (dg, hv, hv, hv, conv_w, conv_b))


def _conv3_transpose(dpre, conv_w):
    T = dpre.shape[0]
    R = _Rows(T, 256)
    tm = R.tm

    def body(d_ref, dn_ref, w_ref, o_ref):
        keep = (pl.program_id(1) < R.nrow - 1).astype(F32)
        ext = jnp.concatenate([d_ref[...].astype(F32), dn_ref[...].astype(F32) * keep], axis=0)
        o_ref[...] = _wsum(w_ref[...], _shifts_anticausal(ext, 3, tm)).astype(BF16)

    return R.call(body, "ffn_conv_bwd", _FB, [R.tile(_FW), R.next(_FW), R.colvec(3, _FW)], R.tile(_FW),
                  jax.ShapeDtypeStruct((T, FF), BF16), (dpre, dpre, conv_w))


def _final_loss_epilogue(w, target):
    T = target.shape[0]

    def fn(xv, ins, outs, first):
        w_ref, t_ref = ins
        l_ref, dh_ref, dhb_ref, dw_ref = outs
        wv = w_ref[...]
        r = lax.rsqrt(jnp.mean(xv * xv, axis=-1, keepdims=True) + EPS)
        xh = xv * r
        err = xh * wv - t_ref[...]
        part = 0.5 * jnp.sum(jnp.mean(err * err, axis=-1, keepdims=True), axis=0, keepdims=True)
        _acc_out(l_ref, jnp.broadcast_to(part, l_ref.shape), first)
        dy = err * (1.0 / D)
        dxh = dy * wv
        dh = r * (dxh - xh * jnp.mean(dxh * xh, axis=-1, keepdims=True))
        dh_ref[...] = dh
        dhb_ref[...] = dh.astype(BF16)
        _acc_out(dw_ref, _colsum(dy * xh), first)

    return _Epilogue(fn, (w, target),
                     (((8, LANES), F32), ((T, D), F32), ((T, D), BF16), ((1, D), F32)), 10 * D)


def _pad_lanes(v, n=LANES):
    return jnp.pad(v, ((0, 0), (0, n - v.shape[1])))


class _Hooks:
    def before_in_proj(self, w_in):
        return w_in

    def late_weights(self, wts, after):
        return wts

    def grads_ready(self, grads, tie):
        return tie

    def mark(self, name, value):
        pass


def _local_step(x, target, wts, hooks=None):
    hooks = hooks or _Hooks()
    T = x.shape[0]
    w_in = wts["w_in"]
    dt_bias_p, a_log_p = _pad_lanes(wts["dt_bias"]), _pad_lanes(wts["a_log"])
    dsk_rep = jnp.repeat(wts["d_skip"], HP, axis=1)

    w_in = hooks.before_in_proj(w_in)
    proj, u, dt_raw = _norm_matmul(x, wts["norm_mix_w"], w_in, "norm_mm_in", w_in[:, OFF_DT:OFF_DT + LANES])
    ya_in = _branch_a_fwd(proj, wts["conv_a_w"])
    xact = _xbc_fwd(proj, wts["ssd_conv_w"], wts["ssd_conv_b"])
    dt, acum, acumT = _dt_fwd(dt_raw, dt_bias_p, a_log_p)
    y_ssd, sprev = _ssd_fwd(xact, dt, acum, acumT, dsk_rep)
    yn = _gnorm_fwd(y_ssd, proj, wts["ssd_norm_w"])
    late = hooks.late_weights(wts, yn)
    w_a_out, w_s_out, w_o, w_up, w_down = (late[k] for k in ("w_a_out", "w_s_out", "w_o", "w_up", "w_down"))
    y_a = _matmul(ya_in, w_a_out, mode="nn", out_dtype=BF16, name="mm_a_out")
    y_s = _matmul(yn, w_s_out, mode="nn", out_dtype=BF16, name="mm_s_out")
    merged = _merge_fwd(proj, y_a, y_s)
    h1 = _matmul(merged, w_o, mode="nn", out_dtype=F32, name="mm_o", residual=x)
    hv, v = _norm_matmul(h1, wts["norm_ffn_w"], w_up, "norm_mm_up")
    gact = _ffn_act_fwd(hv, wts["ffn_conv_w"], wts["ffn_conv_b"])
    loss, dh2, dh2b, g_final = _matmul(gact, w_down, mode="nn", out_dtype=F32, name="mm_down_loss", residual=h1,
                                       epilogue=_final_loss_epilogue(wts["final_norm_w"], target))

    grads = {"final_norm_w": g_final}
    grads["w_down"] = _matmul(gact, dh2b, mode="tn", out_dtype=F32, name="mm_down_dw")
    dgact = _matmul(dh2b, w_down, mode="nt", out_dtype=BF16, name="mm_down_dx")
    dh3, dpre, grads["ffn_conv_w"], grads["ffn_conv_b"] = _ffn_act_bwd(dgact, hv, wts["ffn_conv_w"], wts["ffn_conv_b"])
    dh1c = _conv3_transpose(dpre, wts["ffn_conv_w"])
    grads["w_up"] = (_matmul(v, dh1c, mode="tn", out_dtype=F32, name="mm_up_dw1"),
                     _matmul(v, dh3, mode="tn", out_dtype=F32, name="mm_up_dw3"))
    dv = _matmul(dh1c, w_up, mode="nt", out_dtype=F32, name="mm_up_dx1")
    dh1, dh1b, grads["norm_ffn_w"] = _matmul(
        dh3, w_up, mode="nt", out_dtype=F32, name="mm_up_dx3_norm", residual=dv, b_k_off=FF,
        epilogue=_rmsnorm_bwd_epilogue(h1, wts["norm_ffn_w"], dh2))
    grads["w_o"] = _matmul(merged, dh1b, mode="tn", out_dtype=F32, name="mm_o_dw")
    dmerged = _matmul(dh1b, w_o, mode="nt", out_dtype=BF16, name="mm_o_dx")
    dproj, dya, dys = _merge_bwd(dmerged, proj, y_a, y_s, NIP)
    grads["w_a_out"] = _matmul(ya_in, dya, mode="tn", out_dtype=F32, name="mm_a_out_dw")
    dya_in = _matmul(dya, w_a_out, mode="nt", out_dtype=BF16, name="mm_a_out_dx")
    dproj, grads["conv_a_w"] = _branch_a_bwd(dya_in, proj, wts["conv_a_w"], dproj)
    grads["w_s_out"] = _matmul(yn, dys, mode="tn", out_dtype=F32, name="mm_s_out_dw")
    dys = hooks.grads_ready({k: grads[k] for k in ("w_a_out", "w_s_out", "w_o", "w_up", "w_down")}, dys)
    dyn =_matmul(dys, w_s_out, mode="nt", out_dtype=BF16, name="mm_s_out_dx")
    dproj, dy_ssd, grads["ssd_norm_w"] = _gnorm_bwd(dyn, y_ssd, proj, wts["ssd_norm_w"], dproj)
    dxact, ddt_x, dacum, dskl = _ssd_bwd(dy_ssd, xact, dt, acum, acumT, dsk_rep, sprev)
    hooks.mark("ssd_bwd", dxact)
    grads["d_skip"] = dskl.reshape(NH, HP).sum(axis=1).reshape(1, NH)
    dproj, grads["ssd_conv_w"], grads["ssd_conv_b"] = _xbc_bwd(dxact, proj, wts["ssd_conv_w"], wts["ssd_conv_b"], dproj)
    dproj, g_dtb, g_alog = _dt_bwd(dacum, ddt_x, dt_raw, dt_bias_p, a_log_p, dproj)
    grads["dt_bias"], grads["a_log"] = g_dtb[:, :NH], g_alog[:, :NH]
    grads["w_in"] = _matmul(u, dproj, mode="tn", out_dtype=F32, name="mm_in_dw")
    dproj = hooks.grads_ready({"w_in": grads["w_in"]}, dproj)
    grad_x, _, grads["norm_mix_w"] = _matmul(dproj, w_in, mode="nt", out_dtype=F32, name="mm_in_dx_norm",
                                             epilogue=_rmsnorm_bwd_epilogue(x, wts["norm_mix_w"], dh1))
    return loss, grad_x, grads


def _permute_w_in(slabs):
    cs = slabs.shape[2]
    pieces = []
    for o, n, no in sorted(_SEGS, key=lambda seg: seg[2]):
        for s in range(slabs.shape[0]):
            lo, hi = max(o, s * cs), min(o + n, (s + 1) * cs)
            if lo < hi:
                pieces.append(slabs[s][:, lo - s * cs:hi - s * cs])
    pieces.append(jnp.zeros((slabs.shape[1], NIP - OFF_DT - _SEGS[-1][1]), slabs.dtype))
    return jnp.concatenate(pieces, axis=1)


def _unpermute_w_in(g):
    cs = NI // NCHIP
    slabs = []
    for s in range(NCHIP):
        pieces = []
        for o, n, no in sorted(_SEGS):
            lo, hi = max(o, s * cs), min(o + n, (s + 1) * cs)
            if lo < hi:
                pieces.append(g[:, no + lo - o:no + hi - o])
        slabs.append(jnp.concatenate(pieces, axis=1))
    return jnp.stack(slabs)


MESH = pl.DeviceIdType.MESH
NCHIP = 4
NDEV = 8

_W_IN = (("w_in", D, NI // NCHIP, 1),)
_W_REST = (("w_a_out", D // NCHIP, D, 0), ("w_s_out", DI // NCHIP, D, 0), ("w_o", D // NCHIP, D, 0),
           ("w_up", D, 2 * FF // NCHIP, 1), ("w_down", FF // NCHIP, D, 0))


def _coords():
    return lax.axis_index("x"), lax.axis_index("y"), lax.axis_index("c")


def _other_chips(x, y):
    return [(1 - x, y), (x, 1 - y), (1 - x, 1 - y)]


def _ag_weights(shard):
    nrows = shard.shape[0]
    hr = nrows // 2

    def body(x_ref, out_ref, send_sems, recv_sems, local_sem):
        x, y, c = _coords()
        me = 2 * x + y
        chips = _other_chips(x, y)

        def rows(s, h):
            return out_ref.at[s, pl.ds(h * hr, hr), :]

        def copy(k, s, h, to, src=None):
            return pltpu.make_async_remote_copy(
                src_ref=rows(s, h) if src is None else src, dst_ref=rows(s, h),
                send_sem=send_sems.at[k], recv_sem=recv_sems.at[k], device_id=to, device_id_type=MESH)

        mine = pltpu.make_async_copy(x_ref, out_ref.at[me], local_sem)
        mine.start()
        first = [copy(k, me, c, (*chip, c), src=x_ref.at[pl.ds(c * hr, hr), :]) for k, chip in enumerate(chips)]
        for cp in first:
            cp.start()
        passed = []
        for k, chip in enumerate(chips):
            s = 2 * chip[0] + chip[1]
            copy(k, s, c, (x, y, c)).wait_recv()
            fwd = copy(3 + k, s, c, (x, y, 1 - c))
            fwd.start()
            passed.append(fwd)
        for k, chip in enumerate(chips):
            copy(3 + k, 2 * chip[0] + chip[1], 1 - c, (x, y, c)).wait_recv()
        for cp in first + passed:
            cp.wait_send()
        mine.wait()

    return pl.pallas_call(
        body, name="ag_weights", in_specs=[ANY], out_specs=ANY,
        out_shape=jax.ShapeDtypeStruct((NCHIP,) + shard.shape, shard.dtype),
        scratch_shapes=[pltpu.SemaphoreType.DMA((6,)), pltpu.SemaphoreType.DMA((6,)), pltpu.SemaphoreType.DMA],
        compiler_params=pltpu.CompilerParams(has_side_effects=True),
    )(shard)


HBM = pl.BlockSpec(memory_space=pltpu.HBM)
SEM = pl.BlockSpec(memory_space=pltpu.SEMAPHORE)
_EFFECT = pltpu.SideEffectType.DATAFLOW_SIDE_EFFECTING
_NCOPY = NCHIP - 1


def _plan_bcast(src_ref, land_ref, send_sems, recv_sems, base):
    x, y, c = _coords()
    sends, lands = [], []
    for k, chip in enumerate(_other_chips(x, y)):
        def copy(slot):
            return pltpu.make_async_remote_copy(
                src_ref=src_ref, dst_ref=land_ref.at[slot], send_sem=send_sems.at[base + k],
                recv_sem=recv_sems.at[base + k], device_id=(*chip, c), device_id_type=MESH)
        sends.append(copy(2 * x + y))
        lands.append(copy(2 * chip[0] + chip[1]))
    return sends, lands


def _plan_scatter(src_ref, land_ref, send_sems, recv_sems, base):
    x, y, c = _coords()
    cps = [pltpu.make_async_remote_copy(
        src_ref=src_ref.at[2 * chip[0] + chip[1]], dst_ref=land_ref.at[k], send_sem=send_sems.at[base + k],
        recv_sem=recv_sems.at[base + k], device_id=(*chip, c), device_id_type=MESH)
        for k, chip in enumerate(_other_chips(x, y))]
    return cps, cps


def _plan_all(plan, refs, n):
    sends, lands = [], []
    for t in range(n):
        s, l = plan(refs[t], refs[n + t], refs[2 * n], refs[2 * n + 1], t * _NCOPY)
        sends += s
        lands += l
    return sends, lands


def _split_start(name, srcs, lands, plan):
    n = len(srcs)

    def body(*refs):
        for cp in _plan_all(plan, refs, n)[0]:
            cp.start()
        refs[-1][...] = jnp.zeros_like(refs[-1])

    arrays = list(srcs) + list(lands)
    outs = pl.pallas_call(
        body, name=name,
        out_shape=(pltpu.SemaphoreType.DMA((n * _NCOPY,)), pltpu.SemaphoreType.DMA((n * _NCOPY,)),
                   *[pltpu.HBM(a.shape, a.dtype) for a in arrays], jax.ShapeDtypeStruct((8, LANES), F32)),
        in_specs=(HBM,) * (2 * n),
        out_specs=(SEM, SEM) + (HBM,) * (2 * n) + (pl.BlockSpec(memory_space=pltpu.VMEM),),
        input_output_aliases={t: 2 + t for t in range(2 * n)},
        compiler_params=pltpu.CompilerParams(has_side_effects=_EFFECT),
    )(*[pltpu.with_memory_space_constraint(a, pltpu.HBM) for a in arrays])
    return (outs[0], outs[1], tuple(outs[2:2 + 2 * n])), outs[-1]


def _split_wait(name, handle, after, plan):
    send_sems, recv_sems, arrays = handle
    n = len(arrays) // 2

    def body(*refs):
        sends, lands = _plan_all(plan, refs[:2 * n] + refs[2 * n:2 * n + 2], n)
        for cp in sends:
            cp.wait_send()
        for cp in lands:
            cp.wait_recv()

    outs = pl.pallas_call(
        body, name=name, out_shape=tuple(pltpu.HBM(a.shape, a.dtype) for a in arrays),
        in_specs=(HBM,) * (2 * n) + (SEM, SEM, ANY), out_specs=(HBM,) * (2 * n),
        input_output_aliases={t: t for t in range(2 * n)},
        compiler_params=pltpu.CompilerParams(has_side_effects=_EFFECT),
    )(*arrays, send_sems, recv_sems, after)
    return outs[:n], outs[n:]


def _tie(x, token, name):
    def body(x_ref, t_ref, o_ref):
        pass

    return pl.pallas_call(
        body, name=name, in_specs=[ANY, pl.BlockSpec(memory_space=pltpu.VMEM)], out_specs=ANY,
        out_shape=jax.ShapeDtypeStruct(x.shape, x.dtype), input_output_aliases={0: 0},
    )(x, token)


def _swap_sibling(ps, name):
    n = len(ps)

    def body(*refs):
        x, y, c = _coords()
        cps = [pltpu.make_async_remote_copy(
            src_ref=refs[t], dst_ref=refs[n + t], send_sem=refs[2 * n].at[t], recv_sem=refs[2 * n + 1].at[t],
            device_id=(x, y, 1 - c), device_id_type=MESH) for t in range(n)]
        for cp in cps:
            cp.start()
        for cp in cps:
            cp.wait()

    return pl.pallas_call(
        body, name=name, in_specs=[ANY] * n, out_specs=[ANY] * n,
        out_shape=[jax.ShapeDtypeStruct(p.shape, p.dtype) for p in ps],
        scratch_shapes=[pltpu.SemaphoreType.DMA((n,)), pltpu.SemaphoreType.DMA((n,))],
        compiler_params=pltpu.CompilerParams(has_side_effects=True),
    )(*ps)


_ADD_BYTES = 7 << 19


def _add_tile(rows, cols):
    best = 32
    for t in range(32, rows + 1, 32):
        if rows % t == 0 and t * cols * 4 <= _ADD_BYTES:
            best = t
    return best


def _add_slabs(pack, land, me, name):
    rows, cols = pack.shape[1:]
    tr = _add_tile(rows, cols)

    def body(me_ref, p_ref, l_ref, o_ref):
        f = lambda r: r.astype(F32)
        o_ref[...] = ((f(p_ref[0]) + f(l_ref[0])) + f(l_ref[1])) + f(l_ref[2])

    return pl.pallas_call(
        body, name=name,
        grid_spec=pltpu.PrefetchScalarGridSpec(
            num_scalar_prefetch=1, grid=(rows // tr,),
            in_specs=[pl.BlockSpec((1, tr, cols), lambda i, me_ref: (me_ref[0], i, 0)),
                      pl.BlockSpec((_NCOPY, tr, cols), lambda i, me_ref: (0, i, 0))],
            out_specs=pl.BlockSpec((tr, cols), lambda i, me_ref: (i, 0))),
        out_shape=jax.ShapeDtypeStruct((rows, cols), F32),
        compiler_params=_params(("parallel",)),
    )(me, pack, land)


_STAGE_W = 1024


def _stage_rows(shapes):
    pieces, r = [], 0
    for i, (k, w) in enumerate(shapes):
        for a in range(k):
            for q in range(0, w, _STAGE_W):
                pieces.append((i, a, q, min(_STAGE_W, w - q), r))
                r += 1
    return pieces, -(-r // 8) * 8


def _gather8(parts, reduce, name):
    shapes = [p.shape for p in parts]
    pieces, rows = _stage_rows(shapes)
    n = len(parts)

    def body(*refs):
        ins, outs = refs[:n], refs[n:2 * n]
        stage, buf, res, send_sems, recv_sems = refs[2 * n:]
        x, y, c = _coords()
        me = 4 * x + 2 * y + c
        stage[...] = jnp.zeros_like(stage)
        for i, a, q, w, r in pieces:
            stage[r:r + 1, 0:w] = ins[i][a:a + 1, q:q + w]
        buf[pl.ds(me, 1)] = stage[...][None]
        cps, lands = [], []
        for k in range(1, NDEV):
            peer = (1 - x if k & 4 else x, 1 - y if k & 2 else y, 1 - c if k & 1 else c)

            def copy(slot):
                return pltpu.make_async_remote_copy(
                    src_ref=stage, dst_ref=buf.at[slot], send_sem=send_sems.at[k - 1],
                    recv_sem=recv_sems.at[k - 1], device_id=peer, device_id_type=MESH)

            cps.append(copy(me))
            lands.append(copy(4 * peer[0] + 2 * peer[1] + peer[2]))
        for cp in cps:
            cp.start()
        for cp, land in zip(cps, lands):
            land.wait_recv()
            cp.wait_send()
        if reduce:
            acc = buf[0]
            for d in range(1, NDEV):
                acc = acc + buf[d]
            res[...] = acc
            for i, a, q, w, r in pieces:
                outs[i][a:a + 1, q:q + w] = res[r:r + 1, 0:w]
        else:
            for i, a, q, w, r in pieces:
                for s in range(NCHIP):
                    outs[i][s, a:a + 1, q:q + w] = buf[2 * s, r:r + 1, 0:w]

    vm = pl.BlockSpec(memory_space=pltpu.VMEM)
    out_shapes = [jax.ShapeDtypeStruct(s if reduce else (NCHIP,) + s, F32) for s in shapes]
    return pl.pallas_call(
        body, name=name, in_specs=[vm] * n, out_specs=[vm] * n, out_shape=out_shapes,
        scratch_shapes=[pltpu.VMEM((rows, _STAGE_W), F32), pltpu.VMEM((NDEV, rows, _STAGE_W), F32),
                        pltpu.VMEM((rows, _STAGE_W), F32), pltpu.SemaphoreType.DMA((NDEV - 1,)),
                        pltpu.SemaphoreType.DMA((NDEV - 1,))],
        compiler_params=pltpu.CompilerParams(has_side_effects=True),
    )(*parts)


def _adamw_update(w_ref, g_ref, m_ref, v_ref, d_ref, mo_ref, vo_ref):
    c1 = 1.0 / (1.0 - ADAM_B1 ** ADAM_STEP)
    c2 = 1.0 / (1.0 - ADAM_B2 ** ADAM_STEP)
    gv = g_ref[...]
    mn = ADAM_B1 * m_ref[...] + (1.0 - ADAM_B1) * gv
    vn = ADAM_B2 * v_ref[...] + (1.0 - ADAM_B2) * (gv * gv)
    d_ref[...] = -ADAM_LR * ((mn * c1) / (jnp.sqrt(vn * c2) + ADAM_EPS) + ADAM_WD * w_ref[...])
    mo_ref[...] = mn
    vo_ref[...] = vn


def _adamw_small(ws, gs, ms, vs):
    n = len(ws)

    def body(*refs):
        for i in range(n):
            _adamw_update(*(refs[j * n + i] for j in range(7)))

    vm = pl.BlockSpec(memory_space=pltpu.VMEM)
    outs = pl.pallas_call(
        body, name="adamw_small", in_specs=[vm] * (4 * n), out_specs=[vm] * (3 * n),
        out_shape=[jax.ShapeDtypeStruct(w.shape, F32) for w in ws] * 3,
    )(*ws, *gs, *ms, *vs)
    return outs[:n], outs[n:2 * n], outs[2 * n:]


def _adamw(w, g_parts, m, v, name):
    rows, cols = w.shape
    tr = rows
    while tr * cols * 4 > (1 << 20) and tr % 16 == 0:
        tr //= 2

    def body(w_ref, ga_ref, gb_ref, m_ref, v_ref, g_ref, d_ref, mo_ref, vo_ref):
        g_ref[...] = ga_ref[...] + gb_ref[...]
        _adamw_update(w_ref, g_ref, m_ref, v_ref, d_ref, mo_ref, vo_ref)

    blk = pl.BlockSpec((tr, cols), lambda i: (i, 0))
    return pl.pallas_call(
        body, name=name, grid=(rows // tr,), in_specs=[blk] * 5, out_specs=[blk] * 4,
        out_shape=[jax.ShapeDtypeStruct((rows, cols), F32)] * 4, compiler_params=_params(("parallel",)),
    )(w, *g_parts, m, v)


def _by_chip(g, rr, cc, axis):
    if isinstance(g, tuple):
        n = NCHIP // len(g)
        return jnp.concatenate([h.reshape(rr, n, cc).transpose(1, 0, 2) for h in g], axis=0)
    return g.reshape(NCHIP, rr, cc) if axis == 0 else g.reshape(rr, NCHIP, cc).transpose(1, 0, 2)


_SMALL_REPL = ("norm_mix_w", "ssd_conv_b", "dt_bias", "a_log", "d_skip", "ssd_norm_w", "norm_ffn_w",
               "ffn_conv_b", "final_norm_w")
_SMALL_CONV = (("conv_a_w", 3, D), ("ssd_conv_w", 4, DX), ("ffn_conv_w", 3, FF))


def kernel(x, norm_mix_w, w_in, conv_a_w, w_a_out, ssd_conv_w, ssd_conv_b, dt_bias, a_log, d_skip, ssd_norm_w, w_s_out, w_o, norm_ffn_w, w_up, ffn_conv_w, ffn_conv_b, w_down, final_norm_w, loss_target, m_norm_mix_w, m_w_in, m_conv_a_w, m_w_a_out, m_ssd_conv_w, m_ssd_conv_b, m_dt_bias, m_a_log, m_d_skip, m_ssd_norm_w, m_w_s_out, m_w_o, m_norm_ffn_w, m_w_up, m_ffn_conv_w, m_ffn_conv_b, m_w_down, m_final_norm_w, v_norm_mix_w, v_w_in, v_conv_a_w, v_w_a_out, v_ssd_conv_w, v_ssd_conv_b, v_dt_bias, v_a_log, v_d_skip, v_ssd_norm_w, v_w_s_out, v_w_o, v_norm_ffn_w, v_w_up, v_ffn_conv_w, v_ffn_conv_b, v_w_down, v_final_norm_w):
    names = ("norm_mix_w", "w_in", "conv_a_w", "w_a_out", "ssd_conv_w", "ssd_conv_b", "dt_bias", "a_log", "d_skip",
             "ssd_norm_w", "w_s_out", "w_o", "norm_ffn_w", "w_up", "ffn_conv_w", "ffn_conv_b", "w_down", "final_norm_w")
    W = dict(zip(names, (norm_mix_w, w_in, conv_a_w, w_a_out, ssd_conv_w, ssd_conv_b, dt_bias, a_log, d_skip,
                         ssd_norm_w, w_s_out, w_o, norm_ffn_w, w_up, ffn_conv_w, ffn_conv_b, w_down, final_norm_w)))
    M = dict(zip(names, (m_norm_mix_w, m_w_in, m_conv_a_w, m_w_a_out, m_ssd_conv_w, m_ssd_conv_b, m_dt_bias, m_a_log,
                         m_d_skip, m_ssd_norm_w, m_w_s_out, m_w_o, m_norm_ffn_w, m_w_up, m_ffn_conv_w, m_ffn_conv_b,
                         m_w_down, m_final_norm_w)))
    V = dict(zip(names, (v_norm_mix_w, v_w_in, v_conv_a_w, v_w_a_out, v_ssd_conv_w, v_ssd_conv_b, v_dt_bias, v_a_log,
                         v_d_skip, v_ssd_norm_w, v_w_s_out, v_w_o, v_norm_ffn_w, v_w_up, v_ffn_conv_w, v_ffn_conv_b,
                         v_w_down, v_final_norm_w)))
    two_d = lambda a: a.reshape(-1, a.shape[-1])
    W2, M2, V2 = ({k: two_d(a) for k, a in t.items()} for t in (W, M, V))
    xi, yi, ci = _coords()
    me = 2 * xi + yi

    meidx = me.reshape(1).astype(jnp.int32)
    state = {}


    class Hooks(_Hooks):
        def before_in_proj(self, w_in):
            return _tie(w_in, state["rest_token"], "tie_ag_rest")

        def late_weights(self, wts, after):
            owns, lands = _split_wait("ag_rest_wait", state["rest"], after, _plan_bcast)
            full = {}
            for (n, rr, cc, axis), own, land in zip(_W_REST, owns, lands):
                slabs = lax.dynamic_update_slice(land, own[None], (me, 0, 0))
                full[n] = slabs.reshape(NCHIP * rr, cc) if axis == 0 else slabs.transpose(1, 0, 2).reshape(rr, NCHIP * cc)
            return {**wts, **full}

        def grads_ready(self, grads, tie):
            if "w_in" in grads:
                key, packs = "g_in", [_unpermute_w_in(grads["w_in"]).astype(BF16)]
            else:
                key = "g_rest"
                packs = [_by_chip(jax.tree.map(lambda t: t.astype(BF16), grads[n]), rr, cc, axis)
                         for n, rr, cc, axis in _W_REST]
            lands = [lax.empty((_NCOPY,) + p.shape[1:], BF16) for p in packs]
            state[key], token = _split_start("rs_" + key + "_start", packs, lands, _plan_scatter)
            return _tie(tie, token, "tie_" + key)

        def mark(self, name, value):
            state[name] = value

    def reduced(key, after, group):
        packs, lands = _split_wait("rs_" + key + "_wait", state[key], after, _plan_scatter)
        mines = [_add_slabs(p, l, meidx, "rs_add_chips_" + n) for (n, *_), p, l in zip(group, packs, lands)]
        return dict(zip([n for n, *_ in group], zip(mines, _swap_sibling(mines, "rs_" + key + "_swap"))))

    w_in_slabs = _ag_weights(W2["w_in"].astype(BF16))
    wts = {k: W2[k] for k in _SMALL_REPL}
    conv_by_chip = _gather8([W2[n] for n, *_ in _SMALL_CONV], False, "ag_conv_weights")
    for (n, kk, width), stacked in zip(_SMALL_CONV, conv_by_chip):
        wts[n] = stacked.transpose(1, 0, 2).reshape(kk, width)
    rest = [W2[n].astype(BF16) for n, *_ in _W_REST]
    rest[0] = _tie(rest[0], conv_by_chip[0], "tie_ag_order")
    state["rest"], state["rest_token"] = _split_start(
        "ag_rest_start", rest, [lax.empty((NCHIP,) + r.shape, BF16) for r in rest], _plan_bcast)
    wts["w_in"] = _permute_w_in(w_in_slabs)

    loss8, grad_x, grads = _local_step(x[0], loss_target[0], wts, Hooks())

    gbig = {**reduced("g_rest", state["ssd_bwd"], _W_REST), **reduced("g_in", grad_x, _W_IN)}

    small_parts = [grads[n] for n in _SMALL_REPL] + [loss8[0:1]] + [grads[n] for n, *_ in _SMALL_CONV]
    small_g = _gather8(small_parts, True, "allreduce_small")
    gsm = dict(zip(_SMALL_REPL, small_g[:len(_SMALL_REPL)]))
    loss = small_g[len(_SMALL_REPL)][0, 0]
    for (n, kk, width), gfull in zip(_SMALL_CONV, small_g[len(_SMALL_REPL) + 1:]):
        cw = width // NCHIP
        gsm[n] = lax.dynamic_slice(gfull, (0, me * cw), (kk, cw))

    G, DW, NM, NV = {}, {}, {}, {}
    for n in [b[0] for b in _W_IN + _W_REST]:
        G[n], DW[n], NM[n], NV[n] = _adamw(W2[n], gbig[n], M2[n], V2[n], "adamw_" + n)
    sm_names = list(_SMALL_REPL) + [n for n, *_ in _SMALL_CONV]
    outs = _adamw_small(*([t[n] for n in sm_names] for t in (W2, gsm, M2, V2)))
    for t, vals in zip((DW, NM, NV), outs):
        t.update(zip(sm_names, vals))
    G.update(gsm)

    def shaped(t):
        return [t[n].reshape(W[n].shape) for n in names]

    return (loss, grad_x.reshape(x.shape), *shaped(G), *shaped(DW), *shaped(NM), *shaped(NV))
```

```python
import jax
import jax.numpy as jnp
from jax import lax
from jax.experimental import pallas as pl
from jax.experimental.pallas import tpu as pltpu

F32 = jnp.float32
BF16 = jnp.bfloat16

D = 1024
DI = 2048
NH = 32
HP = 64
NG = 4
NS = 128
CH = 128
DX = 3072
FF = 2816
NI = 10272
EPS = 1e-5

OFF_BCV, OFF_XBC, OFF_G, OFF_Z, OFF_DT = 0, 3072, 6144, 8192, 10240
NIP = 10752
_SEGS = ((0, 2048, OFF_G), (2048, 3072, OFF_BCV), (5120, 2048, OFF_Z), (7168, 3072, OFF_XBC), (10240, 32, OFF_DT))

LANES = 128
HALO = 16
V7X_VMEM_LIMIT = 56 * 2 ** 20

ADAM_LR, ADAM_B1, ADAM_B2, ADAM_EPS, ADAM_WD, ADAM_STEP = 0.001, 0.9, 0.999, 1e-08, 0.01, 10

NN = (((1,), (0,)), ((), ()))
NT = (((1,), (1,)), ((), ()))
TN = (((0,), (0,)), ((), ()))


def _dot(a, b, dims=NN):
    return lax.dot_general(a, b, dims, preferred_element_type=F32)


def _params(sem, **kw):
    return pltpu.CompilerParams(dimension_semantics=sem, vmem_limit_bytes=V7X_VMEM_LIMIT, **kw)


V7X_MXU = 256
V7X_HBM_BYTES_PER_S = 3.5e12
STEP_S = 0.35e-6
MATMUL_VMEM = 40 * 2 ** 20
EPILOGUE_VMEM = 46 * 2 ** 20


ACC_BYTES_PER_S = 1.2e13


def _divisors(dim, cap, units):
    for unit in units:
        c = [t for t in range(unit, min(dim, cap) + 1, unit) if dim % t == 0]
        if c:
            return c
    return [dim]


def _tiles(M, N, K, out_bytes, has_res):
    best = None
    for tn in _divisors(N, 2816, (V7X_MXU, LANES)):
        for tm in _divisors(M, 2816, (LANES,)):
            for tk in _divisors(K, 2816, (V7X_MXU, LANES)):
                nk, ni, nj = K // tk, M // tm, N // tn
                vmem = 4 * (tm * tk + tk * tn) + 2 * tm * tn * out_bytes
                vmem += (4 * tm * tn if nk > 1 else 0) + (8 * tm * tn if has_res else 0)
                if vmem > MATMUL_VMEM:
                    continue
                a_reads = M * K * 2 * (nj if nk > 1 else 1)
                b_reads = K * N * 2 * (ni if nk * nj > 1 else 1)
                cost = (a_reads + b_reads + M * N * out_bytes) / V7X_HBM_BYTES_PER_S + ni * nj * nk * STEP_S
                cost += (nk - 1) * M * N * 8 / ACC_BYTES_PER_S
                if best is None or cost < best[0]:
                    best = (cost, tm, tn, tk)
    assert best is not None, (M, N, K)
    return best[1:]


def _sigmoid(x):
    return 1.0 / (1.0 + jnp.exp(-x))


class _Epilogue:
    def __init__(self, fn, ins, outs, tile_bytes, in_windows=None, out_windows=None):
        self.fn, self.ins, self.outs, self.tile_bytes = fn, tuple(ins), tuple(outs), tile_bytes
        self.in_windows, self.out_windows = in_windows or {}, out_windows or {}


def _matmul(a, b, *, mode, out_dtype, name, residual=None, b_k_off=0, epilogue=None):
    if mode == "nn":
        (M, K), (K2, N) = a.shape, b.shape
    elif mode == "nt":
        (M, K), (N, K2) = a.shape, (b.shape[0], a.shape[1])
        assert b_k_off + K <= b.shape[1]
    else:
        (K, M), (K2, N) = a.shape, b.shape
    assert K == K2, (name, a.shape, b.shape)
    tm, tn, tk = _tiles(M, N, K, jnp.dtype(out_dtype).itemsize, residual is not None)
    if epilogue is not None:
        tn = N
        fits = [(K * N * 2 * (M // t) / V7X_HBM_BYTES_PER_S + (K // q - 1) * M * N * 8 / ACC_BYTES_PER_S
                 + (M // t) * (K // q) * STEP_S, t, q)
                for t in (1024, 512, 256) if M % t == 0 for q in _divisors(K, 2816, (V7X_MXU, LANES))
                if 4 * (t * q + q * tn) + (4 * t * tn if K > q else 0) + (8 * t * tn if residual is not None else 0)
                + 2 * t * epilogue.tile_bytes <= EPILOGUE_VMEM]
        _, tm, tk = min(fits)
    nk = K // tk
    if mode == "tn":
        a_spec = pl.BlockSpec((tk, tm), lambda i, j, k: (k, i))
    else:
        a_spec = pl.BlockSpec((tm, tk), lambda i, j, k: (i, k))
    if mode == "nt":
        assert b_k_off % tk == 0
        b_spec = pl.BlockSpec((tn, tk), lambda i, j, k: (j, k + b_k_off // tk))
    else:
        b_spec = pl.BlockSpec((tk, tn), lambda i, j, k: (k, j))
    dims = {"nn": NN, "nt": NT, "tn": TN}[mode]
    o_spec = pl.BlockSpec((tm, tn), lambda i, j, k: (i, j))
    has_res = residual is not None

    def rows_or_whole(shape, window=None):
        if window is not None:
            off, width = window
            return pl.BlockSpec((tm, width), lambda i, j, k: (i, off // width))
        if shape[0] == M:
            return pl.BlockSpec((tm,) + tuple(shape[1:]), lambda i, j, k: (i,) + (0,) * (len(shape) - 1))
        return pl.BlockSpec(tuple(shape), lambda i, j, k: (0,) * len(shape))

    n_in = 2 + has_res + (len(epilogue.ins) if epilogue else 0)
    n_out = len(epilogue.outs) if epilogue else 1

    def body(*refs):
        a_ref, b_ref = refs[:2]
        r_ref = refs[2] if has_res else None
        out_refs = refs[n_in:n_in + n_out]
        acc_ref = refs[-1]
        k = pl.program_id(2)
        part = _dot(a_ref[...], b_ref[...], dims)

        def finish(r):
            if has_res:
                r = r + r_ref[...].astype(F32)
            if epilogue is None:
                out_refs[0][...] = r.astype(out_dtype)
            else:
                epilogue.fn(r, refs[2 + has_res:n_in], out_refs, pl.program_id(0) == 0)

        if nk == 1:
            finish(part)
            return

        @pl.when(k == 0)
        def _():
            acc_ref[...] = part

        @pl.when(jnp.logical_and(k > 0, k < nk - 1))
        def _():
            acc_ref[...] += part

        @pl.when(k == nk - 1)
        def _():
            finish(acc_ref[...] + part)

    in_specs = [a_spec, b_spec] + ([o_spec] if has_res else [])
    args = (a, b) + ((residual,) if has_res else ())
    if epilogue is None:
        out_specs, out_shape = o_spec, jax.ShapeDtypeStruct((M, N), out_dtype)
        sem = ("parallel", "parallel", "arbitrary")
    else:
        in_specs += [rows_or_whole(x.shape, epilogue.in_windows.get(n)) for n, x in enumerate(epilogue.ins)]
        args += epilogue.ins
        out_specs = [rows_or_whole(o[0], epilogue.out_windows.get(n)) for n, o in enumerate(epilogue.outs)]
        out_shape = [jax.ShapeDtypeStruct(shp, dt) for shp, dt in epilogue.outs]
        sem = ("arbitrary", "arbitrary", "arbitrary")
    return pl.pallas_call(
        body, name=name, grid=(M // tm, N // tn, nk), in_specs=in_specs, out_specs=out_specs,
        out_shape=out_shape, scratch_shapes=[pltpu.VMEM((tm, tn), F32)] if nk > 1 else [],
        compiler_params=_params(sem),
    )(*args)


class _Rows:
    def __init__(self, T, tm):
        self.T, self.tm = T, min(tm, T // 2)
        self.nrow = T // self.tm
        self.r = self.tm // HALO
        self.nb = T // HALO

    def tile(self, w, cb=0, step=1):
        return pl.BlockSpec((self.tm, w), lambda j, i: (i, cb + step * j))

    def prev(self, w, cb=0, step=1):
        r = self.r
        return pl.BlockSpec((HALO, w), lambda j, i: (jnp.maximum(i * r - 1, 0), cb + step * j))

    def next(self, w, cb=0, step=1):
        r, nb = self.r, self.nb
        return pl.BlockSpec((HALO, w), lambda j, i: (jnp.minimum((i + 1) * r, nb - 1), cb + step * j))

    def colvec(self, k, w, cb=0, step=1):
        return pl.BlockSpec((k, w), lambda j, i: (0, cb + step * j))

    def call(self, body, name, ncol, in_specs, out_specs, out_shape, args, aliases=None):
        return pl.pallas_call(
            body, name=name, grid=(ncol, self.nrow), in_specs=in_specs, out_specs=out_specs,
            out_shape=out_shape, input_output_aliases=aliases or {},
            compiler_params=_params(("parallel", "arbitrary")),
        )(*args)


ANY = pl.BlockSpec(memory_space=pl.ANY)


def _shifts_causal(ext, nk, tm):
    out = []
    for k in range(nk):
        s = nk - 1 - k
        r = ext if s == 0 else pltpu.roll(ext, s, 0)
        out.append(r[HALO:])
    return out


def _shifts_anticausal(ext, nk, tm):
    n = ext.shape[0]
    out = []
    for k in range(nk):
        s = nk - 1 - k
        r = ext if s == 0 else pltpu.roll(ext, n - s, 0)
        out.append(r[:tm])
    return out


def _wsum(w, parts):
    acc = w[0:1, :] * parts[0]
    for k in range(1, len(parts)):
        acc = acc + w[k:k + 1, :] * parts[k]
    return acc


def _colsum(x):
    return jnp.sum(x, axis=0, keepdims=True)


def _acc_out(ref, val, first):
    @pl.when(first)
    def _():
        ref[...] = val

    @pl.when(jnp.logical_not(first))
    def _():
        ref[...] += val


def _acc_rows(ref, rows, first):
    for k, r in enumerate(rows):
        _acc_out(ref.at[k:k + 1, :], r, first)


def _norm_matmul(x, wn, b, name, b_f32=None):
    T, N = x.shape[0], b.shape[1]
    tm = min(1024, T)
    tn = max(t for t in _divisors(N, 2816, (V7X_MXU, LANES))
             if 8 * tm * D + 6 * tm * D + 4 * D * t + 4 * tm * t <= MATMUL_VMEM)

    extra = b_f32 is not None

    def body(*refs):
        x_ref, wn_ref, b_ref = refs[:3]
        o_ref, u_ref = refs[3 + extra:5 + extra]
        keep_ref = refs[-1]

        @pl.when(pl.program_id(1) == 0)
        def _():
            xv = x_ref[...]
            r = lax.rsqrt(jnp.mean(xv * xv, axis=-1, keepdims=True) + EPS)
            u = (xv * r * wn_ref[...]).astype(BF16)
            keep_ref[...] = u
            u_ref[...] = u
            if extra:
                refs[5 + extra][...] = _dot(u, refs[3][...])

        o_ref[...] = _dot(keep_ref[...], b_ref[...]).astype(BF16)

    rows = pl.BlockSpec((tm, D), lambda i, j: (i, 0))
    whole = lambda shape: pl.BlockSpec(shape, lambda i, j: (0, 0))
    narrow = pl.BlockSpec((tm, LANES), lambda i, j: (i, 0))
    return pl.pallas_call(
        body, name=name, grid=(T // tm, N // tn),
        in_specs=[rows, whole((1, D)), pl.BlockSpec((D, tn), lambda i, j: (0, j))] + [whole((D, LANES))] * extra,
        out_specs=[pl.BlockSpec((tm, tn), lambda i, j: (i, j)), rows] + [narrow] * extra,
        out_shape=[jax.ShapeDtypeStruct((T, N), BF16), jax.ShapeDtypeStruct((T, D), BF16)]
        + [jax.ShapeDtypeStruct((T, LANES), F32)] * extra,
        scratch_shapes=[pltpu.VMEM((tm, D), BF16)],
        compiler_params=_params(("parallel", "arbitrary")),
    )(*((x, wn, b) + ((b_f32,) if extra else ())))


def _rmsnorm_bwd_epilogue(x, w, dres):
    T = x.shape[0]

    def fn(dyv, ins, outs, first):
        x_ref, w_ref, dr_ref = ins
        dx_ref, dxb_ref, dw_ref = outs
        xv = x_ref[...]
        r = lax.rsqrt(jnp.mean(xv * xv, axis=-1, keepdims=True) + EPS)
        xh = xv * r
        dxh = dyv * w_ref[...]
        dx = r * (dxh - xh * jnp.mean(dxh * xh, axis=-1, keepdims=True)) + dr_ref[...]
        dx_ref[...] = dx
        dxb_ref[...] = dx.astype(BF16)
        _acc_out(dw_ref, _colsum(dyv * xh), first)

    return _Epilogue(fn, (x, w, dres), (((T, D), F32), ((T, D), BF16), ((1, D), F32)), 14 * D)


def _branch_a_fwd(proj, conv_w):
    T = proj.shape[0]
    R = _Rows(T, 512)
    tm = R.tm

    def body(p_ref, pp_ref, w_ref, o_ref):
        keep = (pl.program_id(1) > 0).astype(F32)
        cv = p_ref[:, D:2 * D].astype(F32) * p_ref[:, 2 * D:].astype(F32)
        cvp = pp_ref[:, D:2 * D].astype(F32) * pp_ref[:, 2 * D:].astype(F32) * keep
        sh = _shifts_causal(jnp.concatenate([cvp, cv], axis=0), 3, tm)
        ca = _wsum(w_ref[...], sh)
        o_ref[...] = (p_ref[:, :D].astype(F32) * ca).astype(BF16)

    return R.call(body, "branch_a_fwd", 1, [R.tile(3 * D), R.prev(3 * D), R.colvec(3, D)], R.tile(D),
                  jax.ShapeDtypeStruct((T, D), BF16), (proj, proj, conv_w))


def _branch_a_bwd(dya_in, proj, conv_w, dproj):
    T = proj.shape[0]
    R = _Rows(T, 256)
    tm = R.tm

    def body(d_ref, dn_ref, p_ref, pp_ref, pn_ref, w_ref, _alias, o_ref, dw_ref):
        i = pl.program_id(1)
        keep_p = (i > 0).astype(F32)
        keep_n = (i < R.nrow - 1).astype(F32)
        w = w_ref[...]
        b = p_ref[:, :D].astype(F32)
        c = p_ref[:, D:2 * D].astype(F32)
        v = p_ref[:, 2 * D:].astype(F32)
        cvp = pp_ref[:, D:2 * D].astype(F32) * pp_ref[:, 2 * D:].astype(F32) * keep_p
        sh = _shifts_causal(jnp.concatenate([cvp, c * v], axis=0), 3, tm)
        ca = _wsum(w, sh)
        d = d_ref[...].astype(F32)
        dca = d * b
        dca_n = dn_ref[...].astype(F32) * pn_ref[:, :D].astype(F32) * keep_n
        dsh = _shifts_anticausal(jnp.concatenate([dca, dca_n], axis=0), 3, tm)
        dcv = _wsum(w, dsh)
        o_ref[:, :D] = (d * ca).astype(BF16)
        o_ref[:, D:2 * D] = (dcv * v).astype(BF16)
        o_ref[:, 2 * D:] = (dcv * c).astype(BF16)
        _acc_rows(dw_ref, [_colsum(dca * s) for s in sh], i == 0)

    return R.call(
        body, "branch_a_bwd", 1,
        [R.tile(D), R.next(D), R.tile(3 * D), R.prev(3 * D), R.next(3 * D), R.colvec(3, D), ANY],
        [R.tile(3 * D), R.colvec(3, D)],
        [jax.ShapeDtypeStruct(dproj.shape, BF16), jax.ShapeDtypeStruct((3, D), F32)],
        (dya_in, dya_in, proj, proj, proj, conv_w, dproj), aliases={6: 0})


_XW = 512


def _xbc_fwd(proj, conv_w, conv_b):
    T = proj.shape[0]
    R = _Rows(T, 512)
    tm = R.tm
    cb = OFF_XBC // _XW

    def body(x_ref, xp_ref, w_ref, b_ref, o_ref):
        keep = (pl.program_id(1) > 0).astype(F32)
        ext = jnp.concatenate([xp_ref[...].astype(F32) * keep, x_ref[...].astype(F32)], axis=0)
        pre = _wsum(w_ref[...], _shifts_causal(ext, 4, tm)) + b_ref[...]
        o_ref[...] = (pre * _sigmoid(pre)).astype(BF16)

    return R.call(body, "xbc_fwd", DX // _XW,
                  [R.tile(_XW, cb), R.prev(_XW, cb), R.colvec(4, _XW), R.colvec(1, _XW)], R.tile(_XW),
                  jax.ShapeDtypeStruct((T, DX), BF16), (proj, proj, conv_w, conv_b))


def _xbc_bwd(dact, proj, conv_w, conv_b, dproj):
    T = proj.shape[0]
    R = _Rows(T, 512)
    tm = R.tm
    cb = OFF_XBC // _XW

    def body(d_ref, dn_ref, x_ref, xp_ref, xn_ref, w_ref, b_ref, _alias, o_ref, dw_ref, db_ref):
        i = pl.program_id(1)
        keep_p = (i > 0).astype(F32)
        keep_n = (i < R.nrow - 1).astype(F32)
        w = w_ref[...]
        ext = jnp.concatenate([xp_ref[...].astype(F32) * keep_p, x_ref[...].astype(F32),
                               xn_ref[...].astype(F32)], axis=0)
        sh = _shifts_causal(ext, 4, tm + HALO)
        pre = _wsum(w, sh) + b_ref[...]
        s = _sigmoid(pre)
        dsilu = s * (1.0 + pre * (1.0 - s))
        dext = jnp.concatenate([d_ref[...].astype(F32), dn_ref[...].astype(F32) * keep_n], axis=0)
        dpre = dext * dsilu
        dsh = _shifts_anticausal(dpre, 4, tm)
        o_ref[...] = _wsum(w, dsh).astype(BF16)
        dp = dpre[:tm]
        _acc_rows(dw_ref, [_colsum(dp * q[:tm]) for q in sh], i == 0)
        _acc_out(db_ref, _colsum(dp), i == 0)

    return R.call(
        body, "xbc_bwd", DX // _XW,
        [R.tile(_XW), R.next(_XW), R.tile(_XW, cb), R.prev(_XW, cb), R.next(_XW, cb),
         R.colvec(4, _XW), R.colvec(1, _XW), ANY],
        [R.tile(_XW, cb), R.colvec(4, _XW), R.colvec(1, _XW)],
        [jax.ShapeDtypeStruct(dproj.shape, BF16), jax.ShapeDtypeStruct((4, DX), F32),
         jax.ShapeDtypeStruct((1, DX), F32)],
        (dact, dact, proj, proj, proj, conv_w, conv_b, dproj), aliases={7: 0})


def _softplus(x):
    return jnp.maximum(x, 0.0) + jnp.log(1.0 + jnp.exp(-jnp.abs(x)))


def _dt_rows(T):
    return min(8 * CH, T // 2)


def _dt_fwd(dt_raw, dt_bias_p, a_log_p):
    T = dt_raw.shape[0]
    rows = _dt_rows(T)

    def body(r_ref, b_ref, al_ref, dt_ref, ac_ref, acT_ref):
        dt = _softplus(r_ref[...] + b_ref[...])
        s = dt * (-jnp.exp(al_ref[...]))
        row = lax.broadcasted_iota(jnp.int32, (rows, LANES), 0) % CH
        k = 1
        while k < CH:
            s = s + jnp.where(row >= k, pltpu.roll(s, k, 0), 0.0)
            k *= 2
        dt_ref[...] = dt
        ac_ref[...] = s
        for q in range(0, rows, CH):
            acT_ref[q:q + CH] = s[q:q + CH].T

    blk = pl.BlockSpec((rows, LANES), lambda i: (i, 0))
    vec = pl.BlockSpec((1, LANES), lambda i: (0, 0))
    return pl.pallas_call(
        body, name="dt_fwd", grid=(T // rows,), in_specs=[blk, vec, vec], out_specs=[blk, blk, blk],
        out_shape=[jax.ShapeDtypeStruct((T, LANES), F32)] * 3, compiler_params=_params(("parallel",)),
    )(dt_raw, dt_bias_p, a_log_p)


def _dt_bwd(dacum, ddt_x, dt_raw, dt_bias_p, a_log_p, dproj):
    T = dt_raw.shape[0]
    rows = _dt_rows(T)
    nc = T // rows

    def body(da_ref, dx_ref, r_ref, b_ref, al_ref, _alias, o_ref, db_ref, dal_ref):
        i = pl.program_id(0)
        a = -jnp.exp(al_ref[...])
        z = r_ref[...] + b_ref[...]
        dt = _softplus(z)
        s = da_ref[...]
        row = lax.broadcasted_iota(jnp.int32, (rows, LANES), 0) % CH
        k = 1
        while k < CH:
            s = s + jnp.where(row < CH - k, pltpu.roll(s, rows - k, 0), 0.0)
            k *= 2
        ddt = s * a + dx_ref[...]
        draw = ddt * _sigmoid(z)
        o_ref[:, :LANES] = draw.astype(BF16)
        o_ref[:, LANES:] = jnp.zeros((rows, NIP - OFF_DT - LANES), BF16)
        _acc_out(db_ref, _colsum(draw), i == 0)
        _acc_out(dal_ref, _colsum(s * dt), i == 0)

        @pl.when(i == nc - 1)
        def _():
            dal_ref[...] = dal_ref[...] * a

    blk = pl.BlockSpec((rows, LANES), lambda i: (i, 0))
    vec = pl.BlockSpec((1, LANES), lambda i: (0, 0))
    oblk = pl.BlockSpec((rows, NIP - OFF_DT), lambda i: (i, OFF_DT // (NIP - OFF_DT)))
    return pl.pallas_call(
        body, name="dt_bwd", grid=(nc,), in_specs=[blk, blk, blk, vec, vec, ANY], out_specs=[oblk, vec, vec],
        out_shape=[jax.ShapeDtypeStruct(dproj.shape, BF16), jax.ShapeDtypeStruct((1, LANES), F32),
                   jax.ShapeDtypeStruct((1, LANES), F32)],
        input_output_aliases={5: 0}, compiler_params=_params(("arbitrary",)),
    )(dacum, ddt_x, dt_raw, dt_bias_p, a_log_p, dproj)


_GW = DI // NG
_HG = NH // NG
_NEG = -1e30


def _interleave(gens):
    out, live = [None] * len(gens), list(range(len(gens)))
    while live:
        for i in list(live):
            try:
                next(gens[i])
            except StopIteration as stop:
                out[i] = stop.value
                live.remove(i)
    return out


def _pair_lanes(left, v0, v1):
    return jnp.where(left, v0, v1)


def _ssd_specs(T, rev):
    nc = T // CH
    cm = (lambda c: nc - 1 - c) if rev else (lambda c: c)
    bw = NG * NS
    return dict(
        xs=pl.BlockSpec((CH, DI), lambda c: (cm(c), 0)),
        bm=pl.BlockSpec((CH, bw), lambda c: (cm(c), DI // bw)),
        cmat=pl.BlockSpec((CH, bw), lambda c: (cm(c), DI // bw + 1)),
        xbc=pl.BlockSpec((CH, DX), lambda c: (cm(c), 0)),
        col=pl.BlockSpec((CH, LANES), lambda c: (cm(c), 0)),
        dsk=pl.BlockSpec((1, DI), lambda c: (0, 0)),
        state=pl.BlockSpec((1, NS, DI), lambda c: (cm(c), 0, 0)),
    )


def _last(ref, lo, hi):
    return ref.at[(slice(None),) * (len(ref.shape) - 1) + (slice(lo, hi),)]


def _group_views(g, wide, narrow):
    return [_last(r, g * _GW, (g + 1) * _GW) for r in wide] + [_last(r, g * NS, (g + 1) * NS) for r in narrow]


def _ssd_fwd(xact, dt, acum, acumT, dsk_rep):
    T = xact.shape[0]
    nc = T // CH
    sp = _ssd_specs(T, False)

    def body(*refs):
        xs, bm, cmat, dtr, acr, actr, dsk, y, spv, S_ref = refs

        @pl.when(pl.program_id(0) == 0)
        def _():
            S_ref[...] = jnp.zeros_like(S_ref)

        _interleave([group(g * _HG, dtr[...], acr[...], actr[...],
                           *_group_views(g, (xs, dsk, y, spv, S_ref), (bm, cmat))) for g in range(NG)])

    def group(hb, dt, ac, acT, xs_ref, dsk_ref, y_ref, sp_ref, S_ref, b_ref, c_ref):
        Bm, Cm = b_ref[...], c_ref[...]
        S = S_ref[...]
        sp_ref[0] = S
        cb = _dot(Cm, Bm, NT)
        CS = _dot(Cm, S.astype(BF16))
        row = lax.broadcasted_iota(jnp.int32, (CH, CH), 0)
        col = lax.broadcasted_iota(jnp.int32, (CH, CH), 1)
        tril = row >= col
        left = col < HP
        xd_parts, dec_parts = [], []
        for p in range(_HG // 2):
            sl = slice(p * LANES, (p + 1) * LANES)
            j0, j1 = hb + 2 * p, hb + 2 * p + 1
            xp = xs_ref[:, sl].astype(F32)
            a0, a1 = ac[:, j0:j0 + 1], ac[:, j1:j1 + 1]
            al0, al1 = ac[CH - 1:CH, j0:j0 + 1], ac[CH - 1:CH, j1:j1 + 1]
            X = xp * _pair_lanes(left, dt[:, j0:j0 + 1], dt[:, j1:j1 + 1])
            Xb = X.astype(BF16)
            Ws = [(cb * jnp.exp(jnp.where(tril, aj - acT[j:j + 1, :], _NEG))).astype(BF16)
                  for j, aj in ((j0, a0), (j1, a1))]
            Xs = [jnp.where(m, Xb, jnp.zeros_like(Xb)) for m in (left, jnp.logical_not(left))]
            yield
            yd = _dot(jnp.concatenate(Ws, axis=1), jnp.concatenate(Xs, axis=0))
            yield
            eal = _pair_lanes(left, jnp.exp(a0), jnp.exp(a1))
            y = yd + eal * CS[:, sl] + dsk_ref[:, sl] * xp
            y_ref[:, sl] = y.astype(BF16)
            xd_parts.append(X * _pair_lanes(left, jnp.exp(al0 - a0), jnp.exp(al1 - a1)))
            dec_parts.append(_pair_lanes(left[0:1], jnp.exp(al0), jnp.exp(al1)))
        Xd = jnp.concatenate(xd_parts, axis=1).astype(BF16)
        dec = jnp.concatenate(dec_parts, axis=1)
        S_ref[...] = dec * S + _dot(Bm, Xd, TN)

    return pl.pallas_call(
        body, name="ssd_fwd", grid=(nc,),
        in_specs=[sp["xs"], sp["bm"], sp["cmat"], sp["col"], sp["col"], sp["col"], sp["dsk"]],
        out_specs=[sp["xs"], sp["state"]],
        out_shape=[jax.ShapeDtypeStruct((T, DI), BF16), jax.ShapeDtypeStruct((nc, NS, DI), F32)],
        scratch_shapes=[pltpu.VMEM((NS, DI), F32)],
        compiler_params=_params(("arbitrary",)),
    )(xact, xact, xact, dt, acum, acumT, dsk_rep)


def _ssd_bwd(dy, xact, dt, acum, acumT, dsk_rep, sprev):
    T = xact.shape[0]
    nc = T // CH
    sp = _ssd_specs(T, True)

    def body(*refs):
        xs, bm, cmat, dtr, acr, actr, dsk, dyr, spv, dxa, ddtx, dAc, dskacc, dS_ref = refs
        first = pl.program_id(0) == 0

        @pl.when(first)
        def _():
            dS_ref[...] = jnp.zeros_like(dS_ref)

        dbc = _last(dxa, DI, DX)
        ddtx_sum = jnp.zeros((CH, LANES), F32)
        dAc_sum = jnp.zeros((CH, LANES), F32)
        for a, b in _interleave([group(first, g * _HG, dtr[...], acr[...], actr[...],
                                       *_group_views(g, (xs, dsk, dyr, spv, dxa, dskacc, dS_ref),
                                                     (bm, cmat, dbc, _last(dbc, NG * NS, 2 * NG * NS))))
                                 for g in range(NG)]):
            ddtx_sum, dAc_sum = ddtx_sum + a, dAc_sum + b
        ddtx[...] = ddtx_sum
        dAc[...] = dAc_sum

    def group(first, hb, dt, ac, acT, xs_ref, dsk_ref, dy_ref, sp_ref, dx_ref, dskacc_ref, dS_ref, b_ref, c_ref,
              dB_ref, dC_ref):
        Bm, Cm = b_ref[...], c_ref[...]
        S = sp_ref[0]
        dS = dS_ref[...]
        Sb, dSb = S.astype(BF16), dS.astype(BF16)
        cb = _dot(Cm, Bm, NT)
        cbT = _dot(Bm, Cm, NT)
        CmT = Cm.T
        CS = _dot(Cm, Sb)
        T1 = _dot(Bm, dSb)
        yield
        row = lax.broadcasted_iota(jnp.int32, (CH, CH), 0)
        col = lax.broadcasted_iota(jnp.int32, (CH, CH), 1)
        tril = row >= col
        triu = row <= col
        left = col < HP
        lane8 = lax.broadcasted_iota(jnp.int32, (1, LANES), 1)
        lastrow = lax.broadcasted_iota(jnp.int32, (CH, 1), 0) == CH - 1
        dCB = jnp.zeros((CH, CH), F32)
        dCBT = jnp.zeros((CH, CH), F32)
        dAc = jnp.zeros((CH, LANES), F32)
        ddtx = jnp.zeros((CH, LANES), F32)
        xd_parts, dye_parts, dec_parts, dsk_parts = [], [], [], []
        for p in range(_HG // 2):
            sl = slice(p * LANES, (p + 1) * LANES)
            j0, j1 = hb + 2 * p, hb + 2 * p + 1
            xp = xs_ref[:, sl].astype(F32)
            dyp = dy_ref[:, sl].astype(F32)
            a0, a1 = ac[:, j0:j0 + 1], ac[:, j1:j1 + 1]
            al0, al1 = ac[CH - 1:CH, j0:j0 + 1], ac[CH - 1:CH, j1:j1 + 1]
            dtl = _pair_lanes(left, dt[:, j0:j0 + 1], dt[:, j1:j1 + 1])
            X = xp * dtl
            Xb = X.astype(BF16)
            eal = _pair_lanes(left, jnp.exp(a0), jnp.exp(a1))
            dtel = _pair_lanes(left, jnp.exp(al0 - a0), jnp.exp(al1 - a1))
            T1p = T1[:, sl]
            Rm = T1p * dtel * X
            GR = dyp * (eal * CS[:, sl]) - Rm
            SdS = dS[:, sl] * S[:, sl]
            dXd = jnp.zeros((CH, LANES), F32)
            for j, aj, alj, mask in ((j0, a0, al0, left), (j1, a1, al1, jnp.logical_not(left))):
                dYm = jnp.where(mask, dyp, 0.0).astype(BF16)
                dWm = _dot(dYm, Xb, NT)
                dWmT = _dot(Xb, dYm, NT)
                yield
                e = aj - acT[j:j + 1, :]
                P = dWm * jnp.exp(jnp.where(tril, e, _NEG))
                LmT = jnp.exp(jnp.where(triu, -e, _NEG))
                PT = dWmT * LmT
                dCB = dCB + P
                dCBT = dCBT + PT
                yield
                dXd = dXd + _dot((cbT * LmT).astype(BF16), dYm)
                qd = P * cb - PT * cbT + jnp.where(mask, GR, 0.0)
                colv = jnp.sum(qd, axis=1, keepdims=True)
                tot = jnp.where(mask, Rm + jnp.exp(alj) * SdS, 0.0)
                dalast = jnp.sum(jnp.sum(tot, axis=0, keepdims=True), axis=1, keepdims=True)
                dAc = dAc + (colv + jnp.where(lastrow, dalast, 0.0)) * (lane8 == j).astype(F32)
                yield
            dX = dXd + dtel * T1p
            dXx = dX * xp
            for j, mask in ((j0, left), (j1, jnp.logical_not(left))):
                dd = jnp.sum(jnp.where(mask, dXx, 0.0), axis=1, keepdims=True)
                ddtx = ddtx + dd * (lane8 == j).astype(F32)
            dx_ref[:, sl] = (dX * dtl + dsk_ref[:, sl] * dyp).astype(BF16)
            dsk_parts.append(_colsum(dyp * xp))
            xd_parts.append(X * dtel)
            dye_parts.append(dyp * eal)
            dec_parts.append(_pair_lanes(left[0:1], jnp.exp(al0), jnp.exp(al1)))
            yield
        Xd = jnp.concatenate(xd_parts, axis=1).astype(BF16)
        dYe = jnp.concatenate(dye_parts, axis=1).astype(BF16)
        dec = jnp.concatenate(dec_parts, axis=1)
        dC_ref[...] = (_dot(dCB.astype(BF16), Bm) + _dot(dYe, Sb, NT)).astype(BF16)
        dB_ref[...] = (_dot(dCBT.astype(BF16), Cm) + _dot(Xd, dSb, NT)).astype(BF16)
        dS_ref[...] = _dot(CmT, dYe) + dec * dS
        _acc_out(dskacc_ref, jnp.concatenate(dsk_parts, axis=1), first)
        return ddtx, dAc

    return pl.pallas_call(
        body, name="ssd_bwd", grid=(nc,),
        in_specs=[sp["xs"], sp["bm"], sp["cmat"], sp["col"], sp["col"], sp["col"], sp["dsk"], sp["xs"],
                  sp["state"]],
        out_specs=[sp["xbc"], sp["col"], sp["col"], sp["dsk"]],
        out_shape=[jax.ShapeDtypeStruct((T, DX), BF16), jax.ShapeDtypeStruct((T, LANES), F32),
                   jax.ShapeDtypeStruct((T, LANES), F32), jax.ShapeDtypeStruct((1, DI), F32)],
        scratch_shapes=[pltpu.VMEM((NS, DI), F32)],
        compiler_params=_params(("arbitrary",)),
    )(xact, xact, xact, dt, acum, acumT, dsk_rep, dy, sprev)


def _gnorm_fwd(y, proj, w):
    T = y.shape[0]
    R = _Rows(T, 1024)
    zb = OFF_Z // _GW

    def body(y_ref, z_ref, w_ref, o_ref):
        z = z_ref[...].astype(F32)
        yf = y_ref[...].astype(F32) * z * _sigmoid(z)
        r = lax.rsqrt(jnp.mean(yf * yf, axis=-1, keepdims=True) + EPS)
        o_ref[...] = (yf * r * w_ref[...]).astype(BF16)

    return R.call(body, "gnorm_fwd", NG, [R.tile(_GW), R.tile(_GW, zb), R.colvec(1, _GW)], R.tile(_GW),
                  jax.ShapeDtypeStruct((T, DI), BF16), (y, proj, w))


def _gnorm_bwd(dn, y, proj, w, dproj):
    T = y.shape[0]
    R = _Rows(T, 1024)
    zb = OFF_Z // _GW

    def body(dn_ref, y_ref, z_ref, w_ref, _alias, dz_ref, dy_ref, dw_ref):
        z = z_ref[...].astype(F32)
        yv = y_ref[...].astype(F32)
        s = _sigmoid(z)
        silu = z * s
        yf = yv * silu
        r = lax.rsqrt(jnp.mean(yf * yf, axis=-1, keepdims=True) + EPS)
        yh = yf * r
        dnv = dn_ref[...].astype(F32)
        dyh = dnv * w_ref[...]
        dyf = r * (dyh - yh * jnp.mean(dyh * yh, axis=-1, keepdims=True))
        dy_ref[...] = (dyf * silu).astype(BF16)
        dz_ref[...] = (dyf * yv * s * (1.0 + z * (1.0 - s))).astype(BF16)
        _acc_out(dw_ref, _colsum(dnv * yh), pl.program_id(1) == 0)

    return R.call(
        body, "gnorm_bwd", NG, [R.tile(_GW), R.tile(_GW), R.tile(_GW, zb), R.colvec(1, _GW), ANY],
        [R.tile(_GW, zb), R.tile(_GW), R.colvec(1, _GW)],
        [jax.ShapeDtypeStruct(dproj.shape, BF16), jax.ShapeDtypeStruct((T, DI), BF16),
         jax.ShapeDtypeStruct((1, DI), F32)],
        (dn, y, proj, w, dproj), aliases={4: 0})


def _merge_fwd_epilogue(proj, ya):
    T = proj.shape[0]

    def fn(ysv, ins, outs, first):
        g_ref, ya_ref = ins
        m_ref, ys_ref = outs
        ga = _sigmoid(g_ref[:, :D].astype(F32))
        gs = _sigmoid(g_ref[:, D:].astype(F32))
        m_ref[...] = (ga * ya_ref[...].astype(F32) + gs * ysv).astype(BF16)
        ys_ref[...] = ysv.astype(BF16)

    return _Epilogue(fn, (proj, ya), (((T, D), BF16), ((T, D), BF16)), 10 * D, in_windows={0: (OFF_G, 2 * D)})


def _merge_bwd_epilogue(proj, ya, ys, ncols):
    T = proj.shape[0]

    def fn(d, ins, outs, first):
        g_ref, ya_ref, ys_ref = ins
        dg_ref, dya_ref, dys_ref = outs
        ga = _sigmoid(g_ref[:, :D].astype(F32))
        gs = _sigmoid(g_ref[:, D:].astype(F32))
        dya_ref[...] = (d * ga).astype(BF16)
        dys_ref[...] = (d * gs).astype(BF16)
        dg_ref[:, :D] = (d * ya_ref[...].astype(F32) * ga * (1.0 - ga)).astype(BF16)
        dg_ref[:, D:] = (d * ys_ref[...].astype(F32) * gs * (1.0 - gs)).astype(BF16)

    window = (OFF_G, 2 * D)
    return _Epilogue(fn, (proj, ya, ys), (((T, ncols), BF16), ((T, D), BF16), ((T, D), BF16)), 16 * D,
                     in_windows={0: window}, out_windows={0: window})


_FW = 1408
_FB = FF // _FW


def _ffn_act_fwd(hv, conv_w, conv_b):
    T = hv.shape[0]
    R = _Rows(T, 256)
    tm = R.tm

    def body(h1_ref, h1p_ref, h3_ref, w_ref, b_ref, o_ref):
        keep = (pl.program_id(1) > 0).astype(F32)
        ext = jnp.concatenate([h1p_ref[...].astype(F32) * keep, h1_ref[...].astype(F32)], axis=0)
        pre = _wsum(w_ref[...], _shifts_causal(ext, 3, tm)) + b_ref[...]
        o_ref[...] = (pre * _sigmoid(pre) * h3_ref[...].astype(F32)).astype(BF16)

    return R.call(body, "ffn_act_fwd", _FB,
                  [R.tile(_FW), R.prev(_FW), R.tile(_FW, _FB), R.colvec(3, _FW), R.colvec(1, _FW)],
                  R.tile(_FW), jax.ShapeDtypeStruct((T, FF), BF16), (hv, hv, hv, conv_w, conv_b))


def _ffn_act_bwd(dg, hv, conv_w, conv_b):
    T = hv.shape[0]
    R = _Rows(T, 256)
    tm = R.tm

    def body(dg_ref, h1_ref, h1p_ref, h3_ref, w_ref, b_ref, dh3_ref, dpre_ref, dw_ref, db_ref):
        i = pl.program_id(1)
        keep = (i > 0).astype(F32)
        ext = jnp.concatenate([h1p_ref[...].astype(F32) * keep, h1_ref[...].astype(F32)], axis=0)
        sh = _shifts_causal(ext, 3, tm)
        pre = _wsum(w_ref[...], sh) + b_ref[...]
        s = _sigmoid(pre)
        d = dg_ref[...].astype(F32)
        dh3_ref[...] = (d * pre * s).astype(BF16)
        dpre = d * h3_ref[...].astype(F32) * s * (1.0 + pre * (1.0 - s))
        dpre_ref[...] = dpre.astype(BF16)
        _acc_rows(dw_ref, [_colsum(dpre * q) for q in sh], i == 0)
        _acc_out(db_ref, _colsum(dpre), i == 0)

    return R.call(
        body, "ffn_act_bwd", _FB,
        [R.tile(_FW), R.tile(_FW), R.prev(_FW), R.tile(_FW, _FB), R.colvec(3, _FW), R.colvec(1, _FW)],
        [R.tile(_FW), R.tile(_FW), R.colvec(3, _FW), R.colvec(1, _FW)],
        [jax.ShapeDtypeStruct((T, FF), BF16), jax.ShapeDtypeStruct((T, FF), BF16),
         jax.ShapeDtypeStruct((3, FF), F32), jax.ShapeDtypeStruct((1, FF), F32)],
        (dg, hv, hv, hv, conv_w, conv_b))


def _conv3_transpose(dpre, conv_w):
    T = dpre.shape[0]
    R = _Rows(T, 256)
    tm = R.tm

    def body(d_ref, dn_ref, w_ref, o_ref):
        keep = (pl.program_id(1) < R.nrow - 1).astype(F32)
        ext = jnp.concatenate([d_ref[...].astype(F32), dn_ref[...].astype(F32) * keep], axis=0)
        o_ref[...] = _wsum(w_ref[...], _shifts_anticausal(ext, 3, tm)).astype(BF16)

    return R.call(body, "ffn_conv_bwd", _FB, [R.tile(_FW), R.next(_FW), R.colvec(3, _FW)], R.tile(_FW),
                  jax.ShapeDtypeStruct((T, FF), BF16), (dpre, dpre, conv_w))


def _final_loss_epilogue(w, target):
    T = target.shape[0]

    def fn(xv, ins, outs, first):
        w_ref, t_ref = ins
        l_ref, dh_ref, dhb_ref, dw_ref = outs
        wv = w_ref[...]
        r = lax.rsqrt(jnp.mean(xv * xv, axis=-1, keepdims=True) + EPS)
        xh = xv * r
        err = xh * wv - t_ref[...]
        part = 0.5 * jnp.sum(jnp.mean(err * err, axis=-1, keepdims=True), axis=0, keepdims=True)
        _acc_out(l_ref, jnp.broadcast_to(part, l_ref.shape), first)
        dy = err * (1.0 / D)
        dxh = dy * wv
        dh = r * (dxh - xh * jnp.mean(dxh * xh, axis=-1, keepdims=True))
        dh_ref[...] = dh
        dhb_ref[...] = dh.astype(BF16)
        _acc_out(dw_ref, _colsum(dy * xh), first)

    return _Epilogue(fn, (w, target),
                     (((8, LANES), F32), ((T, D), F32), ((T, D), BF16), ((1, D), F32)), 10 * D)


def _pad_lanes(v, n=LANES):
    return jnp.pad(v, ((0, 0), (0, n - v.shape[1])))


class _Hooks:
    def before_in_proj(self, w_in):
        return w_in

    def late_weights(self, wts, after):
        return wts

    def grads_ready(self, grads, tie):
        return tie

    def mark(self, name, value):
        pass


def _local_step(x, target, wts, hooks=None):
    hooks = hooks or _Hooks()
    T = x.shape[0]
    w_in = wts["w_in"]
    dt_bias_p, a_log_p = _pad_lanes(wts["dt_bias"]), _pad_lanes(wts["a_log"])
    dsk_rep = jnp.repeat(wts["d_skip"], HP, axis=1)

    w_in = hooks.before_in_proj(w_in)
    proj, u, dt_raw = _norm_matmul(x, wts["norm_mix_w"], w_in, "norm_mm_in", w_in[:, OFF_DT:OFF_DT + LANES])
    ya_in = _branch_a_fwd(proj, wts["conv_a_w"])
    xact = _xbc_fwd(proj, wts["ssd_conv_w"], wts["ssd_conv_b"])
    dt, acum, acumT = _dt_fwd(dt_raw, dt_bias_p, a_log_p)
    y_ssd, sprev = _ssd_fwd(xact, dt, acum, acumT, dsk_rep)
    yn = _gnorm_fwd(y_ssd, proj, wts["ssd_norm_w"])
    late = hooks.late_weights(wts, yn)
    w_a_out, w_s_out, w_o, w_up, w_down = (late[k] for k in ("w_a_out", "w_s_out", "w_o", "w_up", "w_down"))
    y_a = _matmul(ya_in, w_a_out, mode="nn", out_dtype=BF16, name="mm_a_out")
    merged, y_s = _matmul(yn, w_s_out, mode="nn", out_dtype=BF16, name="mm_s_out_merge",
                          epilogue=_merge_fwd_epilogue(proj, y_a))
    h1 = _matmul(merged, w_o, mode="nn", out_dtype=F32, name="mm_o", residual=x)
    hv, v = _norm_matmul(h1, wts["norm_ffn_w"], w_up, "norm_mm_up")
    gact = _ffn_act_fwd(hv, wts["ffn_conv_w"], wts["ffn_conv_b"])
    loss, dh2, dh2b, g_final = _matmul(gact, w_down, mode="nn", out_dtype=F32, name="mm_down_loss", residual=h1,
                                       epilogue=_final_loss_epilogue(wts["final_norm_w"], target))

    grads = {"final_norm_w": g_final}
    grads["w_down"] = _matmul(gact, dh2b, mode="tn", out_dtype=F32, name="mm_down_dw")
    dgact = _matmul(dh2b, w_down, mode="nt", out_dtype=BF16, name="mm_down_dx")
    dh3, dpre, grads["ffn_conv_w"], grads["ffn_conv_b"] = _ffn_act_bwd(dgact, hv, wts["ffn_conv_w"], wts["ffn_conv_b"])
    dh1c = _conv3_transpose(dpre, wts["ffn_conv_w"])
    grads["w_up"] = (_matmul(v, dh1c, mode="tn", out_dtype=F32, name="mm_up_dw1"),
                     _matmul(v, dh3, mode="tn", out_dtype=F32, name="mm_up_dw3"))
    dv = _matmul(dh1c, w_up, mode="nt", out_dtype=F32, name="mm_up_dx1")
    dh1, dh1b, grads["norm_ffn_w"] = _matmul(
        dh3, w_up, mode="nt", out_dtype=F32, name="mm_up_dx3_norm", residual=dv, b_k_off=FF,
        epilogue=_rmsnorm_bwd_epilogue(h1, wts["norm_ffn_w"], dh2))
    grads["w_o"] = _matmul(merged, dh1b, mode="tn", out_dtype=F32, name="mm_o_dw")
    dproj, dya, dys = _matmul(dh1b, w_o, mode="nt", out_dtype=BF16, name="mm_o_dx_merge",
                              epilogue=_merge_bwd_epilogue(proj, y_a, y_s, NIP))
    grads["w_a_out"] = _matmul(ya_in, dya, mode="tn", out_dtype=F32, name="mm_a_out_dw")
    dya_in = _matmul(dya, w_a_out, mode="nt", out_dtype=BF16, name="mm_a_out_dx")
    dproj, grads["conv_a_w"] = _branch_a_bwd(dya_in, proj, wts["conv_a_w"], dproj)
    grads["w_s_out"] = _matmul(yn, dys, mode="tn", out_dtype=F32, name="mm_s_out_dw")
    dys = hooks.grads_ready({k: grads[k] for k in ("w_a_out", "w_s_out", "w_o", "w_up", "w_down")}, dys)
    dyn =_matmul(dys, w_s_out, mode="nt", out_dtype=BF16, name="mm_s_out_dx")
    dproj, dy_ssd, grads["ssd_norm_w"] = _gnorm_bwd(dyn, y_ssd, proj, wts["ssd_norm_w"], dproj)
    dxact, ddt_x, dacum, dskl = _ssd_bwd(dy_ssd, xact, dt, acum, acumT, dsk_rep, sprev)
    hooks.mark("ssd_bwd", dxact)
    grads["d_skip"] = dskl.reshape(NH, HP).sum(axis=1).reshape(1, NH)
    dproj, grads["ssd_conv_w"], grads["ssd_conv_b"] = _xbc_bwd(dxact, proj, wts["ssd_conv_w"], wts["ssd_conv_b"], dproj)
    dproj, g_dtb, g_alog = _dt_bwd(dacum, ddt_x, dt_raw, dt_bias_p, a_log_p, dproj)
    grads["dt_bias"], grads["a_log"] = g_dtb[:, :NH], g_alog[:, :NH]
    grads["w_in"] = _matmul(u, dproj, mode="tn", out_dtype=F32, name="mm_in_dw")
    dproj = hooks.grads_ready({"w_in": grads["w_in"]}, dproj)
    grad_x, _, grads["norm_mix_w"] = _matmul(dproj, w_in, mode="nt", out_dtype=F32, name="mm_in_dx_norm",
                                             epilogue=_rmsnorm_bwd_epilogue(x, wts["norm_mix_w"], dh1))
    return loss, grad_x, grads


def _permute_w_in(slabs):
    cs = slabs.shape[2]
    pieces = []
    for o, n, no in sorted(_SEGS, key=lambda seg: seg[2]):
        for s in range(slabs.shape[0]):
            lo, hi = max(o, s * cs), min(o + n, (s + 1) * cs)
            if lo < hi:
                pieces.append(slabs[s][:, lo - s * cs:hi - s * cs])
    pieces.append(jnp.zeros((slabs.shape[1], NIP - OFF_DT - _SEGS[-1][1]), slabs.dtype))
    return jnp.concatenate(pieces, axis=1)


def _unpermute_w_in(g):
    cs = NI // NCHIP
    slabs = []
    for s in range(NCHIP):
        pieces = []
        for o, n, no in sorted(_SEGS):
            lo, hi = max(o, s * cs), min(o + n, (s + 1) * cs)
            if lo < hi:
                pieces.append(g[:, no + lo - o:no + hi - o])
        slabs.append(jnp.concatenate(pieces, axis=1))
    return jnp.stack(slabs)


MESH = pl.DeviceIdType.MESH
NCHIP = 4
NDEV = 8

_W_IN = (("w_in", D, NI // NCHIP, 1),)
_W_REST = (("w_a_out", D // NCHIP, D, 0), ("w_s_out", DI // NCHIP, D, 0), ("w_o", D // NCHIP, D, 0),
           ("w_up", D, 2 * FF // NCHIP, 1), ("w_down", FF // NCHIP, D, 0))


def _coords():
    return lax.axis_index("x"), lax.axis_index("y"), lax.axis_index("c")


def _other_chips(x, y):
    return [(1 - x, y), (x, 1 - y), (1 - x, 1 - y)]


def _ag_weights(shard):
    nrows = shard.shape[0]
    hr = nrows // 2

    def body(x_ref, out_ref, send_sems, recv_sems, local_sem):
        x, y, c = _coords()
        me = 2 * x + y
        chips = _other_chips(x, y)

        def rows(s, h):
            return out_ref.at[s, pl.ds(h * hr, hr), :]

        def copy(k, s, h, to, src=None):
            return pltpu.make_async_remote_copy(
                src_ref=rows(s, h) if src is None else src, dst_ref=rows(s, h),
                send_sem=send_sems.at[k], recv_sem=recv_sems.at[k], device_id=to, device_id_type=MESH)

        mine = pltpu.make_async_copy(x_ref, out_ref.at[me], local_sem)
        mine.start()
        first = [copy(k, me, c, (*chip, c), src=x_ref.at[pl.ds(c * hr, hr), :]) for k, chip in enumerate(chips)]
        for cp in first:
            cp.start()
        passed = []
        for k, chip in enumerate(chips):
            s = 2 * chip[0] + chip[1]
            copy(k, s, c, (x, y, c)).wait_recv()
            fwd = copy(3 + k, s, c, (x, y, 1 - c))
            fwd.start()
            passed.append(fwd)
        for k, chip in enumerate(chips):
            copy(3 + k, 2 * chip[0] + chip[1], 1 - c, (x, y, c)).wait_recv()
        for cp in first + passed:
            cp.wait_send()
        mine.wait()

    return pl.pallas_call(
        body, name="ag_weights", in_specs=[ANY], out_specs=ANY,
        out_shape=jax.ShapeDtypeStruct((NCHIP,) + shard.shape, shard.dtype),
        scratch_shapes=[pltpu.SemaphoreType.DMA((6,)), pltpu.SemaphoreType.DMA((6,)), pltpu.SemaphoreType.DMA],
        compiler_params=pltpu.CompilerParams(has_side_effects=True),
    )(shard)


HBM = pl.BlockSpec(memory_space=pltpu.HBM)
SEM = pl.BlockSpec(memory_space=pltpu.SEMAPHORE)
_EFFECT = pltpu.SideEffectType.DATAFLOW_SIDE_EFFECTING
_NCOPY = NCHIP - 1


def _plan_bcast(src_ref, land_ref, send_sems, recv_sems, base):
    x, y, c = _coords()
    sends, lands = [], []
    for k, chip in enumerate(_other_chips(x, y)):
        def copy(slot):
            return pltpu.make_async_remote_copy(
                src_ref=src_ref, dst_ref=land_ref.at[slot], send_sem=send_sems.at[base + k],
                recv_sem=recv_sems.at[base + k], device_id=(*chip, c), device_id_type=MESH)
        sends.append(copy(2 * x + y))
        lands.append(copy(2 * chip[0] + chip[1]))
    return sends, lands


def _plan_scatter(src_ref, land_ref, send_sems, recv_sems, base):
    x, y, c = _coords()
    cps = [pltpu.make_async_remote_copy(
        src_ref=src_ref.at[2 * chip[0] + chip[1]], dst_ref=land_ref.at[k], send_sem=send_sems.at[base + k],
        recv_sem=recv_sems.at[base + k], device_id=(*chip, c), device_id_type=MESH)
        for k, chip in enumerate(_other_chips(x, y))]
    return cps, cps


def _plan_all(plan, refs, n):
    sends, lands = [], []
    for t in range(n):
        s, l = plan(refs[t], refs[n + t], refs[2 * n], refs[2 * n + 1], t * _NCOPY)
        sends += s
        lands += l
    return sends, lands


def _split_start(name, srcs, lands, plan):
    n = len(srcs)

    def body(*refs):
        for cp in _plan_all(plan, refs, n)[0]:
            cp.start()
        refs[-1][...] = jnp.zeros_like(refs[-1])

    arrays = list(srcs) + list(lands)
    outs = pl.pallas_call(
        body, name=name,
        out_shape=(pltpu.SemaphoreType.DMA((n * _NCOPY,)), pltpu.SemaphoreType.DMA((n * _NCOPY,)),
                   *[pltpu.HBM(a.shape, a.dtype) for a in arrays], jax.ShapeDtypeStruct((8, LANES), F32)),
        in_specs=(HBM,) * (2 * n),
        out_specs=(SEM, SEM) + (HBM,) * (2 * n) + (pl.BlockSpec(memory_space=pltpu.VMEM),),
        input_output_aliases={t: 2 + t for t in range(2 * n)},
        compiler_params=pltpu.CompilerParams(has_side_effects=_EFFECT),
    )(*[pltpu.with_memory_space_constraint(a, pltpu.HBM) for a in arrays])
    return (outs[0], outs[1], tuple(outs[2:2 + 2 * n])), outs[-1]


def _split_wait(name, handle, after, plan):
    send_sems, recv_sems, arrays = handle
    n = len(arrays) // 2

    def body(*refs):
        sends, lands = _plan_all(plan, refs[:2 * n] + refs[2 * n:2 * n + 2], n)
        for cp in sends:
            cp.wait_send()
        for cp in lands:
            cp.wait_recv()

    outs = pl.pallas_call(
        body, name=name, out_shape=tuple(pltpu.HBM(a.shape, a.dtype) for a in arrays),
        in_specs=(HBM,) * (2 * n) + (SEM, SEM, ANY), out_specs=(HBM,) * (2 * n),
        input_output_aliases={t: t for t in range(2 * n)},
        compiler_params=pltpu.CompilerParams(has_side_effects=_EFFECT),
    )(*arrays, send_sems, recv_sems, after)
    return outs[:n], outs[n:]


def _tie(x, token, name):
    def body(x_ref, t_ref, o_ref):
        pass

    return pl.pallas_call(
        body, name=name, in_specs=[ANY, pl.BlockSpec(memory_space=pltpu.VMEM)], out_specs=ANY,
        out_shape=jax.ShapeDtypeStruct(x.shape, x.dtype), input_output_aliases={0: 0},
    )(x, token)


def _swap_sibling(ps, name):
    n = len(ps)

    def body(*refs):
        x, y, c = _coords()
        cps = [pltpu.make_async_remote_copy(
            src_ref=refs[t], dst_ref=refs[n + t], send_sem=refs[2 * n].at[t], recv_sem=refs[2 * n + 1].at[t],
            device_id=(x, y, 1 - c), device_id_type=MESH) for t in range(n)]
        for cp in cps:
            cp.start()
        for cp in cps:
            cp.wait()

    return pl.pallas_call(
        body, name=name, in_specs=[ANY] * n, out_specs=[ANY] * n,
        out_shape=[jax.ShapeDtypeStruct(p.shape, p.dtype) for p in ps],
        scratch_shapes=[pltpu.SemaphoreType.DMA((n,)), pltpu.SemaphoreType.DMA((n,))],
        compiler_params=pltpu.CompilerParams(has_side_effects=True),
    )(*ps)


_ADD_BYTES = 7 << 19


def _add_tile(rows, cols):
    best = 32
    for t in range(32, rows + 1, 32):
        if rows % t == 0 and t * cols * 4 <= _ADD_BYTES:
            best = t
    return best


def _add_slabs(pack, land, me, name):
    rows, cols = pack.shape[1:]
    tr = _add_tile(rows, cols)

    def body(me_ref, p_ref, l_ref, o_ref):
        f = lambda r: r.astype(F32)
        o_ref[...] = ((f(p_ref[0]) + f(l_ref[0])) + f(l_ref[1])) + f(l_ref[2])

    return pl.pallas_call(
        body, name=name,
        grid_spec=pltpu.PrefetchScalarGridSpec(
            num_scalar_prefetch=1, grid=(rows // tr,),
            in_specs=[pl.BlockSpec((1, tr, cols), lambda i, me_ref: (me_ref[0], i, 0)),
                      pl.BlockSpec((_NCOPY, tr, cols), lambda i, me_ref: (0, i, 0))],
            out_specs=pl.BlockSpec((tr, cols), lambda i, me_ref: (i, 0))),
        out_shape=jax.ShapeDtypeStruct((rows, cols), F32),
        compiler_params=_params(("parallel",)),
    )(me, pack, land)


_STAGE_W = 1024


def _stage_rows(shapes):
    pieces, r = [], 0
    for i, (k, w) in enumerate(shapes):
        for a in range(k):
            for q in range(0, w, _STAGE_W):
                pieces.append((i, a, q, min(_STAGE_W, w - q), r))
                r += 1
    return pieces, -(-r // 8) * 8


def _gather8(parts, reduce, name):
    shapes = [p.shape for p in parts]
    pieces, rows = _stage_rows(shapes)
    n = len(parts)

    def body(*refs):
        ins, outs = refs[:n], refs[n:2 * n]
        stage, buf, res, send_sems, recv_sems = refs[2 * n:]
        x, y, c = _coords()
        me = 4 * x + 2 * y + c
        stage[...] = jnp.zeros_like(stage)
        for i, a, q, w, r in pieces:
            stage[r:r + 1, 0:w] = ins[i][a:a + 1, q:q + w]
        buf[pl.ds(me, 1)] = stage[...][None]
        cps, lands = [], []
        for k in range(1, NDEV):
            peer = (1 - x if k & 4 else x, 1 - y if k & 2 else y, 1 - c if k & 1 else c)

            def copy(slot):
                return pltpu.make_async_remote_copy(
                    src_ref=stage, dst_ref=buf.at[slot], send_sem=send_sems.at[k - 1],
                    recv_sem=recv_sems.at[k - 1], device_id=peer, device_id_type=MESH)

            cps.append(copy(me))
            lands.append(copy(4 * peer[0] + 2 * peer[1] + peer[2]))
        for cp in cps:
            cp.start()
        for cp, land in zip(cps, lands):
            land.wait_recv()
            cp.wait_send()
        if reduce:
            acc = buf[0]
            for d in range(1, NDEV):
                acc = acc + buf[d]
            res[...] = acc
            for i, a, q, w, r in pieces:
                outs[i][a:a + 1, q:q + w] = res[r:r + 1, 0:w]
        else:
            for i, a, q, w, r in pieces:
                for s in range(NCHIP):
                    outs[i][s, a:a + 1, q:q + w] = buf[2 * s, r:r + 1, 0:w]

    vm = pl.BlockSpec(memory_space=pltpu.VMEM)
    out_shapes = [jax.ShapeDtypeStruct(s if reduce else (NCHIP,) + s, F32) for s in shapes]
    return pl.pallas_call(
        body, name=name, in_specs=[vm] * n, out_specs=[vm] * n, out_shape=out_shapes,
        scratch_shapes=[pltpu.VMEM((rows, _STAGE_W), F32), pltpu.VMEM((NDEV, rows, _STAGE_W), F32),
                        pltpu.VMEM((rows, _STAGE_W), F32), pltpu.SemaphoreType.DMA((NDEV - 1,)),
                        pltpu.SemaphoreType.DMA((NDEV - 1,))],
        compiler_params=pltpu.CompilerParams(has_side_effects=True),
    )(*parts)


def _adamw_update(w_ref, g_ref, m_ref, v_ref, d_ref, mo_ref, vo_ref):
    c1 = 1.0 / (1.0 - ADAM_B1 ** ADAM_STEP)
    c2 = 1.0 / (1.0 - ADAM_B2 ** ADAM_STEP)
    gv = g_ref[...]
    mn = ADAM_B1 * m_ref[...] + (1.0 - ADAM_B1) * gv
    vn = ADAM_B2 * v_ref[...] + (1.0 - ADAM_B2) * (gv * gv)
    d_ref[...] = -ADAM_LR * ((mn * c1) / (jnp.sqrt(vn * c2) + ADAM_EPS) + ADAM_WD * w_ref[...])
    mo_ref[...] = mn
    vo_ref[...] = vn


def _adamw_small(ws, gs, ms, vs):
    n = len(ws)

    def body(*refs):
        for i in range(n):
            _adamw_update(*(refs[j * n + i] for j in range(7)))

    vm = pl.BlockSpec(memory_space=pltpu.VMEM)
    outs = pl.pallas_call(
        body, name="adamw_small", in_specs=[vm] * (4 * n), out_specs=[vm] * (3 * n),
        out_shape=[jax.ShapeDtypeStruct(w.shape, F32) for w in ws] * 3,
    )(*ws, *gs, *ms, *vs)
    return outs[:n], outs[n:2 * n], outs[2 * n:]


def _adamw(w, g_parts, m, v, name):
    rows, cols = w.shape
    tr = rows
    while tr * cols * 4 > (1 << 20) and tr % 16 == 0:
        tr //= 2

    def body(w_ref, ga_ref, gb_ref, m_ref, v_ref, g_ref, d_ref, mo_ref, vo_ref):
        g_ref[...] = ga_ref[...] + gb_ref[...]
        _adamw_update(w_ref, g_ref, m_ref, v_ref, d_ref, mo_ref, vo_ref)

    blk = pl.BlockSpec((tr, cols), lambda i: (i, 0))
    return pl.pallas_call(
        body, name=name, grid=(rows // tr,), in_specs=[blk] * 5, out_specs=[blk] * 4,
        out_shape=[jax.ShapeDtypeStruct((rows, cols), F32)] * 4, compiler_params=_params(("parallel",)),
    )(w, *g_parts, m, v)


def _by_chip(g, rr, cc, axis):
    if isinstance(g, tuple):
        n = NCHIP // len(g)
        return jnp.concatenate([h.reshape(rr, n, cc).transpose(1, 0, 2) for h in g], axis=0)
    return g.reshape(NCHIP, rr, cc) if axis == 0 else g.reshape(rr, NCHIP, cc).transpose(1, 0, 2)


_SMALL_REPL = ("norm_mix_w", "ssd_conv_b", "dt_bias", "a_log", "d_skip", "ssd_norm_w", "norm_ffn_w",
               "ffn_conv_b", "final_norm_w")
_SMALL_CONV = (("conv_a_w", 3, D), ("ssd_conv_w", 4, DX), ("ffn_conv_w", 3, FF))


def kernel(x, norm_mix_w, w_in, conv_a_w, w_a_out, ssd_conv_w, ssd_conv_b, dt_bias, a_log, d_skip, ssd_norm_w, w_s_out, w_o, norm_ffn_w, w_up, ffn_conv_w, ffn_conv_b, w_down, final_norm_w, loss_target, m_norm_mix_w, m_w_in, m_conv_a_w, m_w_a_out, m_ssd_conv_w, m_ssd_conv_b, m_dt_bias, m_a_log, m_d_skip, m_ssd_norm_w, m_w_s_out, m_w_o, m_norm_ffn_w, m_w_up, m_ffn_conv_w, m_ffn_conv_b, m_w_down, m_final_norm_w, v_norm_mix_w, v_w_in, v_conv_a_w, v_w_a_out, v_ssd_conv_w, v_ssd_conv_b, v_dt_bias, v_a_log, v_d_skip, v_ssd_norm_w, v_w_s_out, v_w_o, v_norm_ffn_w, v_w_up, v_ffn_conv_w, v_ffn_conv_b, v_w_down, v_final_norm_w):
    names = ("norm_mix_w", "w_in", "conv_a_w", "w_a_out", "ssd_conv_w", "ssd_conv_b", "dt_bias", "a_log", "d_skip",
             "ssd_norm_w", "w_s_out", "w_o", "norm_ffn_w", "w_up", "ffn_conv_w", "ffn_conv_b", "w_down", "final_norm_w")
    W = dict(zip(names, (norm_mix_w, w_in, conv_a_w, w_a_out, ssd_conv_w, ssd_conv_b, dt_bias, a_log, d_skip,
                         ssd_norm_w, w_s_out, w_o, norm_ffn_w, w_up, ffn_conv_w, ffn_conv_b, w_down, final_norm_w)))
    M = dict(zip(names, (m_norm_mix_w, m_w_in, m_conv_a_w, m_w_a_out, m_ssd_conv_w, m_ssd_conv_b, m_dt_bias, m_a_log,
                         m_d_skip, m_ssd_norm_w, m_w_s_out, m_w_o, m_norm_ffn_w, m_w_up, m_ffn_conv_w, m_ffn_conv_b,
                         m_w_down, m_final_norm_w)))
    V = dict(zip(names, (v_norm_mix_w, v_w_in, v_conv_a_w, v_w_a_out, v_ssd_conv_w, v_ssd_conv_b, v_dt_bias, v_a_log,
                         v_d_skip, v_ssd_norm_w, v_w_s_out, v_w_o, v_norm_ffn_w, v_w_up, v_ffn_conv_w, v_ffn_conv_b,
                         v_w_down, v_final_norm_w)))
    two_d = lambda a: a.reshape(-1, a.shape[-1])
    W2, M2, V2 = ({k: two_d(a) for k, a in t.items()} for t in (W, M, V))
    xi, yi, ci = _coords()
    me = 2 * xi + yi

    meidx = me.reshape(1).astype(jnp.int32)
    state = {}


    class Hooks(_Hooks):
        def before_in_proj(self, w_in):
            return _tie(w_in, state["rest_token"], "tie_ag_rest")

        def late_weights(self, wts, after):
            owns, lands = _split_wait("ag_rest_wait", state["rest"], after, _plan_bcast)
            full = {}
            for (n, rr, cc, axis), own, land in zip(_W_REST, owns, lands):
                slabs = lax.dynamic_update_slice(land, own[None], (me, 0, 0))
                full[n] = slabs.reshape(NCHIP * rr, cc) if axis == 0 else slabs.transpose(1, 0, 2).reshape(rr, NCHIP * cc)
            return {**wts, **full}

        def grads_ready(self, grads, tie):
            if "w_in" in grads:
                key, packs = "g_in", [_unpermute_w_in(grads["w_in"]).astype(BF16)]
            else:
                key = "g_rest"
                packs = [_by_chip(jax.tree.map(lambda t: t.astype(BF16), grads[n]), rr, cc, axis)
                         for n, rr, cc, axis in _W_REST]
            lands = [lax.empty((_NCOPY,) + p.shape[1:], BF16) for p in packs]
            state[key], token = _split_start("rs_" + key + "_start", packs, lands, _plan_scatter)
            return _tie(tie, token, "tie_" + key)

        def mark(self, name, value):
            state[name] = value

    def reduced(key, after, group):
        packs, lands = _split_wait("rs_" + key + "_wait", state[key], after, _plan_scatter)
        mines = [_add_slabs(p, l, meidx, "rs_add_chips_" + n) for (n, *_), p, l in zip(group, packs, lands)]
        return dict(zip([n for n, *_ in group], zip(mines, _swap_sibling(mines, "rs_" + key + "_swap"))))

    w_in_slabs = _ag_weights(W2["w_in"].astype(BF16))
    wts = {k: W2[k] for k in _SMALL_REPL}
    conv_by_chip = _gather8([W2[n] for n, *_ in _SMALL_CONV], False, "ag_conv_weights")
    for (n, kk, width), stacked in zip(_SMALL_CONV, conv_by_chip):
        wts[n] = stacked.transpose(1, 0, 2).reshape(kk, width)
    rest = [W2[n].astype(BF16) for n, *_ in _W_REST]
    rest[0] = _tie(rest[0], conv_by_chip[0], "tie_ag_order")
    state["rest"], state["rest_token"] = _split_start(
        "ag_rest_start", rest, [lax.empty((NCHIP,) + r.shape, BF16) for r in rest], _plan_bcast)
    wts["w_in"] = _permute_w_in(w_in_slabs)

    loss8, grad_x, grads = _local_step(x[0], loss_target[0], wts, Hooks())

    gbig = {**reduced("g_rest", state["ssd_bwd"], _W_REST), **reduced("g_in", grad_x, _W_IN)}

    small_parts = [grads[n] for n in _SMALL_REPL] + [loss8[0:1]] + [grads[n] for n, *_ in _SMALL_CONV]
    small_g = _gather8(small_parts, True, "allreduce_small")
    gsm = dict(zip(_SMALL_REPL, small_g[:len(_SMALL_REPL)]))
    loss = small_g[len(_SMALL_REPL)][0, 0]
    for (n, kk, width), gfull in zip(_SMALL_CONV, small_g[len(_SMALL_REPL) + 1:]):
        cw = width // NCHIP
        gsm[n] = lax.dynamic_slice(gfull, (0, me * cw), (kk, cw))

    G, DW, NM, NV = {}, {}, {}, {}
    for n in [b[0] for b in _W_IN + _W_REST]:
        G[n], DW[n], NM[n], NV[n] = _adamw(W2[n], gbig[n], M2[n], V2[n], "adamw_" + n)
    sm_names = list(_SMALL_REPL) + [n for n, *_ in _SMALL_CONV]
    outs = _adamw_small(*([t[n] for n in sm_names] for t in (W2, gsm, M2, V2)))
    for t, vals in zip((DW, NM, NV), outs):
        t.update(zip(sm_names, vals))
    G.update(gsm)

    def shaped(t):
        return [t[n].reshape(W[n].shape) for n in names]

    return (loss, grad_x.reshape(x.shape), *shaped(G), *shaped(DW), *shaped(NM), *shaped(NV))
```

```python
import jax
import jax.numpy as jnp
from jax import lax
from jax.experimental import pallas as pl
from jax.experimental.pallas import tpu as pltpu

F32 = jnp.float32
BF16 = jnp.bfloat16

D = 1024
DI = 2048
NH = 32
HP = 64
NG = 4
NS = 128
CH = 128
DX = 3072
FF = 2816
NI = 10272
EPS = 1e-5

OFF_BCV, OFF_XBC, OFF_G, OFF_Z, OFF_DT = 0, 3072, 6144, 8192, 10240
NIP = 10752
_SEGS = ((0, 2048, OFF_G), (2048, 3072, OFF_BCV), (5120, 2048, OFF_Z), (7168, 3072, OFF_XBC), (10240, 32, OFF_DT))

LANES = 128
HALO = 16
V7X_VMEM_LIMIT = 56 * 2 ** 20

ADAM_LR, ADAM_B1, ADAM_B2, ADAM_EPS, ADAM_WD, ADAM_STEP = 0.001, 0.9, 0.999, 1e-08, 0.01, 10

NN = (((1,), (0,)), ((), ()))
NT = (((1,), (1,)), ((), ()))
TN = (((0,), (0,)), ((), ()))


def _dot(a, b, dims=NN):
    return lax.dot_general(a, b, dims, preferred_element_type=F32)


def _params(sem, **kw):
    return pltpu.CompilerParams(dimension_semantics=sem, vmem_limit_bytes=V7X_VMEM_LIMIT, **kw)


V7X_MXU = 256
V7X_HBM_BYTES_PER_S = 3.5e12
STEP_S = 0.35e-6
MATMUL_VMEM = 40 * 2 ** 20
EPILOGUE_VMEM = 46 * 2 ** 20


ACC_BYTES_PER_S = 1.2e13


def _divisors(dim, cap, units):
    for unit in units:
        c = [t for t in range(unit, min(dim, cap) + 1, unit) if dim % t == 0]
        if c:
            return c
    return [dim]


def _tiles(M, N, K, out_bytes, has_res):
    best = None
    for tn in _divisors(N, 2816, (V7X_MXU, LANES)):
        for tm in _divisors(M, 2816, (LANES,)):
            for tk in _divisors(K, 2816, (V7X_MXU, LANES)):
                nk, ni, nj = K // tk, M // tm, N // tn
                vmem = 4 * (tm * tk + tk * tn) + 2 * tm * tn * out_bytes
                vmem += (4 * tm * tn if nk > 1 else 0) + (8 * tm * tn if has_res else 0)
                if vmem > MATMUL_VMEM:
                    continue
                a_reads = M * K * 2 * (nj if nk > 1 else 1)
                b_reads = K * N * 2 * (ni if nk * nj > 1 else 1)
                cost = (a_reads + b_reads + M * N * out_bytes) / V7X_HBM_BYTES_PER_S + ni * nj * nk * STEP_S
                cost += (nk - 1) * M * N * 8 / ACC_BYTES_PER_S
                if best is None or cost < best[0]:
                    best = (cost, tm, tn, tk)
    assert best is not None, (M, N, K)
    return best[1:]


def _sigmoid(x):
    return 1.0 / (1.0 + jnp.exp(-x))


class _Epilogue:
    def __init__(self, fn, ins, outs, tile_bytes, in_windows=None, out_windows=None):
        self.fn, self.ins, self.outs, self.tile_bytes = fn, tuple(ins), tuple(outs), tile_bytes
        self.in_windows, self.out_windows = in_windows or {}, out_windows or {}


def _matmul(a, b, *, mode, out_dtype, name, residual=None, b_k_off=0, epilogue=None):
    if mode == "nn":
        (M, K), (K2, N) = a.shape, b.shape
    elif mode == "nt":
        (M, K), (N, K2) = a.shape, (b.shape[0], a.shape[1])
        assert b_k_off + K <= b.shape[1]
    else:
        (K, M), (K2, N) = a.shape, b.shape
    assert K == K2, (name, a.shape, b.shape)
    tm, tn, tk = _tiles(M, N, K, jnp.dtype(out_dtype).itemsize, residual is not None)
    if epilogue is not None:
        tn = N
        fits = [(K * N * 2 * (M // t) / V7X_HBM_BYTES_PER_S + (K // q - 1) * M * N * 8 / ACC_BYTES_PER_S
                 + (M // t) * (K // q) * STEP_S, t, q)
                for t in (1024, 512, 256) if M % t == 0 for q in _divisors(K, 2816, (V7X_MXU, LANES))
                if 4 * (t * q + q * tn) + (4 * t * tn if K > q else 0) + (8 * t * tn if residual is not None else 0)
                + 2 * t * epilogue.tile_bytes <= EPILOGUE_VMEM]
        _, tm, tk = min(fits)
    nk = K // tk
    if mode == "tn":
        a_spec = pl.BlockSpec((tk, tm), lambda i, j, k: (k, i))
    else:
        a_spec = pl.BlockSpec((tm, tk), lambda i, j, k: (i, k))
    if mode == "nt":
        assert b_k_off % tk == 0
        b_spec = pl.BlockSpec((tn, tk), lambda i, j, k: (j, k + b_k_off // tk))
    else:
        b_spec = pl.BlockSpec((tk, tn), lambda i, j, k: (k, j))
    dims = {"nn": NN, "nt": NT, "tn": TN}[mode]
    o_spec = pl.BlockSpec((tm, tn), lambda i, j, k: (i, j))
    has_res = residual is not None

    def rows_or_whole(shape, window=None):
        if window is not None:
            off, width = window
            return pl.BlockSpec((tm, width), lambda i, j, k: (i, off // width))
        if shape[0] == M:
            return pl.BlockSpec((tm,) + tuple(shape[1:]), lambda i, j, k: (i,) + (0,) * (len(shape) - 1))
        return pl.BlockSpec(tuple(shape), lambda i, j, k: (0,) * len(shape))

    n_in = 2 + has_res + (len(epilogue.ins) if epilogue else 0)
    n_out = len(epilogue.outs) if epilogue else 1

    def body(*refs):
        a_ref, b_ref = refs[:2]
        r_ref = refs[2] if has_res else None
        out_refs = refs[n_in:n_in + n_out]
        acc_ref = refs[-1]
        k = pl.program_id(2)
        part = _dot(a_ref[...], b_ref[...], dims)

        def finish(r):
            if has_res:
                r = r + r_ref[...].astype(F32)
            if epilogue is None:
                out_refs[0][...] = r.astype(out_dtype)
            else:
                epilogue.fn(r, refs[2 + has_res:n_in], out_refs, pl.program_id(0) == 0)

        if nk == 1:
            finish(part)
            return

        @pl.when(k == 0)
        def _():
            acc_ref[...] = part

        @pl.when(jnp.logical_and(k > 0, k < nk - 1))
        def _():
            acc_ref[...] += part

        @pl.when(k == nk - 1)
        def _():
            finish(acc_ref[...] + part)

    in_specs = [a_spec, b_spec] + ([o_spec] if has_res else [])
    args = (a, b) + ((residual,) if has_res else ())
    if epilogue is None:
        out_specs, out_shape = o_spec, jax.ShapeDtypeStruct((M, N), out_dtype)
        sem = ("parallel", "parallel", "arbitrary")
    else:
        in_specs += [rows_or_whole(x.shape, epilogue.in_windows.get(n)) for n, x in enumerate(epilogue.ins)]
        args += epilogue.ins
        out_specs = [rows_or_whole(o[0], epilogue.out_windows.get(n)) for n, o in enumerate(epilogue.outs)]
        out_shape = [jax.ShapeDtypeStruct(shp, dt) for shp, dt in epilogue.outs]
        sem = ("arbitrary", "arbitrary", "arbitrary")
    return pl.pallas_call(
        body, name=name, grid=(M // tm, N // tn, nk), in_specs=in_specs, out_specs=out_specs,
        out_shape=out_shape, scratch_shapes=[pltpu.VMEM((tm, tn), F32)] if nk > 1 else [],
        compiler_params=_params(sem),
    )(*args)


class _Rows:
    def __init__(self, T, tm):
        self.T, self.tm = T, min(tm, T // 2)
        self.nrow = T // self.tm
        self.r = self.tm // HALO
        self.nb = T // HALO

    def tile(self, w, cb=0, step=1):
        return pl.BlockSpec((self.tm, w), lambda j, i: (i, cb + step * j))

    def prev(self, w, cb=0, step=1):
        r = self.r
        return pl.BlockSpec((HALO, w), lambda j, i: (jnp.maximum(i * r - 1, 0), cb + step * j))

    def next(self, w, cb=0, step=1):
        r, nb = self.r, self.nb
        return pl.BlockSpec((HALO, w), lambda j, i: (jnp.minimum((i + 1) * r, nb - 1), cb + step * j))

    def colvec(self, k, w, cb=0, step=1):
        return pl.BlockSpec((k, w), lambda j, i: (0, cb + step * j))

    def call(self, body, name, ncol, in_specs, out_specs, out_shape, args, aliases=None):
        return pl.pallas_call(
            body, name=name, grid=(ncol, self.nrow), in_specs=in_specs, out_specs=out_specs,
            out_shape=out_shape, input_output_aliases=aliases or {},
            compiler_params=_params(("parallel", "arbitrary")),
        )(*args)


ANY = pl.BlockSpec(memory_space=pl.ANY)


def _shifts_causal(ext, nk, tm):
    out = []
    for k in range(nk):
        s = nk - 1 - k
        r = ext if s == 0 else pltpu.roll(ext, s, 0)
        out.append(r[HALO:])
    return out


def _shifts_anticausal(ext, nk, tm):
    n = ext.shape[0]
    out = []
    for k in range(nk):
        s = nk - 1 - k
        r = ext if s == 0 else pltpu.roll(ext, n - s, 0)
        out.append(r[:tm])
    return out


def _wsum(w, parts):
    acc = w[0:1, :] * parts[0]
    for k in range(1, len(parts)):
        acc = acc + w[k:k + 1, :] * parts[k]
    return acc


def _colsum(x):
    return jnp.sum(x, axis=0, keepdims=True)


def _acc_out(ref, val, first):
    @pl.when(first)
    def _():
        ref[...] = val

    @pl.when(jnp.logical_not(first))
    def _():
        ref[...] += val


def _acc_rows(ref, rows, first):
    for k, r in enumerate(rows):
        _acc_out(ref.at[k:k + 1, :], r, first)


def _norm_matmul(x, wn, b, name, b_f32=None):
    T, N = x.shape[0], b.shape[1]
    tm = min(1024, T)
    tn = max(t for t in _divisors(N, 2816, (V7X_MXU, LANES))
             if 8 * tm * D + 6 * tm * D + 4 * D * t + 4 * tm * t <= MATMUL_VMEM)

    extra = b_f32 is not None

    def body(*refs):
        x_ref, wn_ref, b_ref = refs[:3]
        o_ref, u_ref = refs[3 + extra:5 + extra]
        keep_ref = refs[-1]

        @pl.when(pl.program_id(1) == 0)
        def _():
            xv = x_ref[...]
            r = lax.rsqrt(jnp.mean(xv * xv, axis=-1, keepdims=True) + EPS)
            u = (xv * r * wn_ref[...]).astype(BF16)
            keep_ref[...] = u
            u_ref[...] = u
            if extra:
                refs[5 + extra][...] = _dot(u, refs[3][...])

        o_ref[...] = _dot(keep_ref[...], b_ref[...]).astype(BF16)

    rows = pl.BlockSpec((tm, D), lambda i, j: (i, 0))
    whole = lambda shape: pl.BlockSpec(shape, lambda i, j: (0, 0))
    narrow = pl.BlockSpec((tm, LANES), lambda i, j: (i, 0))
    return pl.pallas_call(
        body, name=name, grid=(T // tm, N // tn),
        in_specs=[rows, whole((1, D)), pl.BlockSpec((D, tn), lambda i, j: (0, j))] + [whole((D, LANES))] * extra,
        out_specs=[pl.BlockSpec((tm, tn), lambda i, j: (i, j)), rows] + [narrow] * extra,
        out_shape=[jax.ShapeDtypeStruct((T, N), BF16), jax.ShapeDtypeStruct((T, D), BF16)]
        + [jax.ShapeDtypeStruct((T, LANES), F32)] * extra,
        scratch_shapes=[pltpu.VMEM((tm, D), BF16)],
        compiler_params=_params(("parallel", "arbitrary")),
    )(*((x, wn, b) + ((b_f32,) if extra else ())))


def _rmsnorm_bwd_epilogue(x, w, dres):
    T = x.shape[0]

    def fn(dyv, ins, outs, first):
        x_ref, w_ref, dr_ref = ins
        dx_ref, dxb_ref, dw_ref = outs
        xv = x_ref[...]
        r = lax.rsqrt(jnp.mean(xv * xv, axis=-1, keepdims=True) + EPS)
        xh = xv * r
        dxh = dyv * w_ref[...]
        dx = r * (dxh - xh * jnp.mean(dxh * xh, axis=-1, keepdims=True)) + dr_ref[...]
        dx_ref[...] = dx
        dxb_ref[...] = dx.astype(BF16)
        _acc_out(dw_ref, _colsum(dyv * xh), first)

    return _Epilogue(fn, (x, w, dres), (((T, D), F32), ((T, D), BF16), ((1, D), F32)), 14 * D)


def _branch_a_fwd(proj, conv_w):
    T = proj.shape[0]
    R = _Rows(T, 512)
    tm = R.tm

    def body(p_ref, pp_ref, w_ref, o_ref):
        keep = (pl.program_id(1) > 0).astype(F32)
        cv = p_ref[:, D:2 * D].astype(F32) * p_ref[:, 2 * D:].astype(F32)
        cvp = pp_ref[:, D:2 * D].astype(F32) * pp_ref[:, 2 * D:].astype(F32) * keep
        sh = _shifts_causal(jnp.concatenate([cvp, cv], axis=0), 3, tm)
        ca = _wsum(w_ref[...], sh)
        o_ref[...] = (p_ref[:, :D].astype(F32) * ca).astype(BF16)

    return R.call(body, "branch_a_fwd", 1, [R.tile(3 * D), R.prev(3 * D), R.colvec(3, D)], R.tile(D),
                  jax.ShapeDtypeStruct((T, D), BF16), (proj, proj, conv_w))


def _branch_a_bwd(dya_in, proj, conv_w, dproj):
    T = proj.shape[0]
    R = _Rows(T, 256)
    tm = R.tm

    def body(d_ref, dn_ref, p_ref, pp_ref, pn_ref, w_ref, _alias, o_ref, dw_ref):
        i = pl.program_id(1)
        keep_p = (i > 0).astype(F32)
        keep_n = (i < R.nrow - 1).astype(F32)
        w = w_ref[...]
        b = p_ref[:, :D].astype(F32)
        c = p_ref[:, D:2 * D].astype(F32)
        v = p_ref[:, 2 * D:].astype(F32)
        cvp = pp_ref[:, D:2 * D].astype(F32) * pp_ref[:, 2 * D:].astype(F32) * keep_p
        sh = _shifts_causal(jnp.concatenate([cvp, c * v], axis=0), 3, tm)
        ca = _wsum(w, sh)
        d = d_ref[...].astype(F32)
        dca = d * b
        dca_n = dn_ref[...].astype(F32) * pn_ref[:, :D].astype(F32) * keep_n
        dsh = _shifts_anticausal(jnp.concatenate([dca, dca_n], axis=0), 3, tm)
        dcv = _wsum(w, dsh)
        o_ref[:, :D] = (d * ca).astype(BF16)
        o_ref[:, D:2 * D] = (dcv * v).astype(BF16)
        o_ref[:, 2 * D:] = (dcv * c).astype(BF16)
        _acc_rows(dw_ref, [_colsum(dca * s) for s in sh], i == 0)

    return R.call(
        body, "branch_a_bwd", 1,
        [R.tile(D), R.next(D), R.tile(3 * D), R.prev(3 * D), R.next(3 * D), R.colvec(3, D), ANY],
        [R.tile(3 * D), R.colvec(3, D)],
        [jax.ShapeDtypeStruct(dproj.shape, BF16), jax.ShapeDtypeStruct((3, D), F32)],
        (dya_in, dya_in, proj, proj, proj, conv_w, dproj), aliases={6: 0})


_XW = 512


def _xbc_fwd(proj, conv_w, conv_b):
    T = proj.shape[0]
    R = _Rows(T, 512)
    tm = R.tm
    cb = OFF_XBC // _XW

    def body(x_ref, xp_ref, w_ref, b_ref, o_ref):
        keep = (pl.program_id(1) > 0).astype(F32)
        ext = jnp.concatenate([xp_ref[...].astype(F32) * keep, x_ref[...].astype(F32)], axis=0)
        pre = _wsum(w_ref[...], _shifts_causal(ext, 4, tm)) + b_ref[...]
        o_ref[...] = (pre * _sigmoid(pre)).astype(BF16)

    return R.call(body, "xbc_fwd", DX // _XW,
                  [R.tile(_XW, cb), R.prev(_XW, cb), R.colvec(4, _XW), R.colvec(1, _XW)], R.tile(_XW),
                  jax.ShapeDtypeStruct((T, DX), BF16), (proj, proj, conv_w, conv_b))


def _xbc_bwd(dact, proj, conv_w, conv_b, dproj):
    T = proj.shape[0]
    R = _Rows(T, 512)
    tm = R.tm
    cb = OFF_XBC // _XW

    def body(d_ref, dn_ref, x_ref, xp_ref, xn_ref, w_ref, b_ref, _alias, o_ref, dw_ref, db_ref):
        i = pl.program_id(1)
        keep_p = (i > 0).astype(F32)
        keep_n = (i < R.nrow - 1).astype(F32)
        w = w_ref[...]
        ext = jnp.concatenate([xp_ref[...].astype(F32) * keep_p, x_ref[...].astype(F32),
                               xn_ref[...].astype(F32)], axis=0)
        sh = _shifts_causal(ext, 4, tm + HALO)
        pre = _wsum(w, sh) + b_ref[...]
        s = _sigmoid(pre)
        dsilu = s * (1.0 + pre * (1.0 - s))
        dext = jnp.concatenate([d_ref[...].astype(F32), dn_ref[...].astype(F32) * keep_n], axis=0)
        dpre = dext * dsilu
        dsh = _shifts_anticausal(dpre, 4, tm)
        o_ref[...] = _wsum(w, dsh).astype(BF16)
        dp = dpre[:tm]
        _acc_rows(dw_ref, [_colsum(dp * q[:tm]) for q in sh], i == 0)
        _acc_out(db_ref, _colsum(dp), i == 0)

    return R.call(
        body, "xbc_bwd", DX // _XW,
        [R.tile(_XW), R.next(_XW), R.tile(_XW, cb), R.prev(_XW, cb), R.next(_XW, cb),
         R.colvec(4, _XW), R.colvec(1, _XW), ANY],
        [R.tile(_XW, cb), R.colvec(4, _XW), R.colvec(1, _XW)],
        [jax.ShapeDtypeStruct(dproj.shape, BF16), jax.ShapeDtypeStruct((4, DX), F32),
         jax.ShapeDtypeStruct((1, DX), F32)],
        (dact, dact, proj, proj, proj, conv_w, conv_b, dproj), aliases={7: 0})


def _softplus(x):
    return jnp.maximum(x, 0.0) + jnp.log(1.0 + jnp.exp(-jnp.abs(x)))


def _dt_rows(T):
    return min(8 * CH, T // 2)


def _dt_fwd(dt_raw, dt_bias_p, a_log_p):
    T = dt_raw.shape[0]
    rows = _dt_rows(T)

    def body(r_ref, b_ref, al_ref, dt_ref, ac_ref, acT_ref):
        dt = _softplus(r_ref[...] + b_ref[...])
        s = dt * (-jnp.exp(al_ref[...]))
        row = lax.broadcasted_iota(jnp.int32, (rows, LANES), 0) % CH
        k = 1
        while k < CH:
            s = s + jnp.where(row >= k, pltpu.roll(s, k, 0), 0.0)
            k *= 2
        dt_ref[...] = dt
        ac_ref[...] = s
        for q in range(0, rows, CH):
            acT_ref[q:q + CH] = s[q:q + CH].T

    blk = pl.BlockSpec((rows, LANES), lambda i: (i, 0))
    vec = pl.BlockSpec((1, LANES), lambda i: (0, 0))
    return pl.pallas_call(
        body, name="dt_fwd", grid=(T // rows,), in_specs=[blk, vec, vec], out_specs=[blk, blk, blk],
        out_shape=[jax.ShapeDtypeStruct((T, LANES), F32)] * 3, compiler_params=_params(("parallel",)),
    )(dt_raw, dt_bias_p, a_log_p)


def _dt_bwd(dacum, ddt_x, dt_raw, dt_bias_p, a_log_p, dproj):
    T = dt_raw.shape[0]
    rows = _dt_rows(T)
    nc = T // rows

    def body(da_ref, dx_ref, r_ref, b_ref, al_ref, _alias, o_ref, db_ref, dal_ref):
        i = pl.program_id(0)
        a = -jnp.exp(al_ref[...])
        z = r_ref[...] + b_ref[...]
        dt = _softplus(z)
        s = da_ref[...]
        row = lax.broadcasted_iota(jnp.int32, (rows, LANES), 0) % CH
        k = 1
        while k < CH:
            s = s + jnp.where(row < CH - k, pltpu.roll(s, rows - k, 0), 0.0)
            k *= 2
        ddt = s * a + dx_ref[...]
        draw = ddt * _sigmoid(z)
        o_ref[:, :LANES] = draw.astype(BF16)
        o_ref[:, LANES:] = jnp.zeros((rows, NIP - OFF_DT - LANES), BF16)
        _acc_out(db_ref, _colsum(draw), i == 0)
        _acc_out(dal_ref, _colsum(s * dt), i == 0)

        @pl.when(i == nc - 1)
        def _():
            dal_ref[...] = dal_ref[...] * a

    blk = pl.BlockSpec((rows, LANES), lambda i: (i, 0))
    vec = pl.BlockSpec((1, LANES), lambda i: (0, 0))
    oblk = pl.BlockSpec((rows, NIP - OFF_DT), lambda i: (i, OFF_DT // (NIP - OFF_DT)))
    return pl.pallas_call(
        body, name="dt_bwd", grid=(nc,), in_specs=[blk, blk, blk, vec, vec, ANY], out_specs=[oblk, vec, vec],
        out_shape=[jax.ShapeDtypeStruct(dproj.shape, BF16), jax.ShapeDtypeStruct((1, LANES), F32),
                   jax.ShapeDtypeStruct((1, LANES), F32)],
        input_output_aliases={5: 0}, compiler_params=_params(("arbitrary",)),
    )(dacum, ddt_x, dt_raw, dt_bias_p, a_log_p, dproj)


_GW = DI // NG
_HG = NH // NG
_NEG = -1e30


def _interleave(gens):
    out, live = [None] * len(gens), list(range(len(gens)))
    while live:
        for i in list(live):
            try:
                next(gens[i])
            except StopIteration as stop:
                out[i] = stop.value
                live.remove(i)
    return out


def _pair_lanes(left, v0, v1):
    return jnp.where(left, v0, v1)


def _ssd_specs(T, rev):
    nc = T // CH
    cm = (lambda c: nc - 1 - c) if rev else (lambda c: c)
    bw = NG * NS
    return dict(
        xs=pl.BlockSpec((CH, DI), lambda c: (cm(c), 0)),
        bm=pl.BlockSpec((CH, bw), lambda c: (cm(c), DI // bw)),
        cmat=pl.BlockSpec((CH, bw), lambda c: (cm(c), DI // bw + 1)),
        xbc=pl.BlockSpec((CH, DX), lambda c: (cm(c), 0)),
        col=pl.BlockSpec((CH, LANES), lambda c: (cm(c), 0)),
        dsk=pl.BlockSpec((1, DI), lambda c: (0, 0)),
        state=pl.BlockSpec((1, NS, DI), lambda c: (cm(c), 0, 0)),
    )


def _last(ref, lo, hi):
    return ref.at[(slice(None),) * (len(ref.shape) - 1) + (slice(lo, hi),)]


def _group_views(g, wide, narrow):
    return [_last(r, g * _GW, (g + 1) * _GW) for r in wide] + [_last(r, g * NS, (g + 1) * NS) for r in narrow]


def _ssd_fwd(xact, dt, acum, acumT, dsk_rep, proj, norm_w):
    T = xact.shape[0]
    nc = T // CH
    sp = _ssd_specs(T, False)

    def body(*refs):
        xs, bm, cmat, dtr, acr, actr, dsk, zr, nw, y, yn, spv, S_ref = refs

        @pl.when(pl.program_id(0) == 0)
        def _():
            S_ref[...] = jnp.zeros_like(S_ref)

        _interleave([group(g * _HG, dtr[...], acr[...], actr[...],
                           *_group_views(g, (xs, dsk, zr, nw, y, yn, spv, S_ref), (bm, cmat))) for g in range(NG)])

    def group(hb, dt, ac, acT, xs_ref, dsk_ref, z_ref, nw_ref, y_ref, yn_ref, sp_ref, S_ref, b_ref, c_ref):
        Bm, Cm = b_ref[...], c_ref[...]
        S = S_ref[...]
        sp_ref[0] = S
        cb = _dot(Cm, Bm, NT)
        CS = _dot(Cm, S.astype(BF16))
        row = lax.broadcasted_iota(jnp.int32, (CH, CH), 0)
        col = lax.broadcasted_iota(jnp.int32, (CH, CH), 1)
        tril = row >= col
        left = col < HP
        xd_parts, dec_parts = [], []
        for p in range(_HG // 2):
            sl = slice(p * LANES, (p + 1) * LANES)
            j0, j1 = hb + 2 * p, hb + 2 * p + 1
            xp = xs_ref[:, sl].astype(F32)
            a0, a1 = ac[:, j0:j0 + 1], ac[:, j1:j1 + 1]
            al0, al1 = ac[CH - 1:CH, j0:j0 + 1], ac[CH - 1:CH, j1:j1 + 1]
            X = xp * _pair_lanes(left, dt[:, j0:j0 + 1], dt[:, j1:j1 + 1])
            Xb = X.astype(BF16)
            Ws = [(cb * jnp.exp(jnp.where(tril, aj - acT[j:j + 1, :], _NEG))).astype(BF16)
                  for j, aj in ((j0, a0), (j1, a1))]
            Xs = [jnp.where(m, Xb, jnp.zeros_like(Xb)) for m in (left, jnp.logical_not(left))]
            yield
            yd = _dot(jnp.concatenate(Ws, axis=1), jnp.concatenate(Xs, axis=0))
            yield
            eal = _pair_lanes(left, jnp.exp(a0), jnp.exp(a1))
            y = yd + eal * CS[:, sl] + dsk_ref[:, sl] * xp
            y_ref[:, sl] = y.astype(BF16)
            xd_parts.append(X * _pair_lanes(left, jnp.exp(al0 - a0), jnp.exp(al1 - a1)))
            dec_parts.append(_pair_lanes(left[0:1], jnp.exp(al0), jnp.exp(al1)))
        Xd = jnp.concatenate(xd_parts, axis=1).astype(BF16)
        dec = jnp.concatenate(dec_parts, axis=1)
        S_ref[...] = dec * S + _dot(Bm, Xd, TN)
        yield
        z = z_ref[...].astype(F32)
        yf = y_ref[...].astype(F32) * z * _sigmoid(z)
        r = lax.rsqrt(jnp.mean(yf * yf, axis=-1, keepdims=True) + EPS)
        yn_ref[...] = (yf * r * nw_ref[...]).astype(BF16)

    zspec = pl.BlockSpec((CH, DI), lambda c: (c, OFF_Z // DI))
    return pl.pallas_call(
        body, name="ssd_fwd", grid=(nc,),
        in_specs=[sp["xs"], sp["bm"], sp["cmat"], sp["col"], sp["col"], sp["col"], sp["dsk"], zspec, sp["dsk"]],
        out_specs=[sp["xs"], sp["xs"], sp["state"]],
        out_shape=[jax.ShapeDtypeStruct((T, DI), BF16), jax.ShapeDtypeStruct((T, DI), BF16),
                   jax.ShapeDtypeStruct((nc, NS, DI), F32)],
        scratch_shapes=[pltpu.VMEM((NS, DI), F32)],
        compiler_params=_params(("arbitrary",)),
    )(xact, xact, xact, dt, acum, acumT, dsk_rep, proj, norm_w)


def _ssd_bwd(dn, y, proj, norm_w, dproj, xact, dt, acum, acumT, dsk_rep, sprev):
    T = xact.shape[0]
    nc = T // CH
    sp = _ssd_specs(T, True)

    def body(*refs):
        xs, bm, cmat, dtr, acr, actr, dsk, dnr, yr, zr, nw, spv, _alias, dxa, ddtx, dAc, dskacc, dzr, dnw, dS_ref = refs
        first = pl.program_id(0) == 0

        @pl.when(first)
        def _():
            dS_ref[...] = jnp.zeros_like(dS_ref)

        dbc = _last(dxa, DI, DX)
        ddtx_sum = jnp.zeros((CH, LANES), F32)
        dAc_sum = jnp.zeros((CH, LANES), F32)
        for a, b in _interleave([group(first, g * _HG, dtr[...], acr[...], actr[...],
                                       *_group_views(g, (xs, dsk, dnr, yr, zr, nw, dzr, dnw, spv, dxa, dskacc, dS_ref),
                                                     (bm, cmat, dbc, _last(dbc, NG * NS, 2 * NG * NS))))
                                 for g in range(NG)]):
            ddtx_sum, dAc_sum = ddtx_sum + a, dAc_sum + b
        ddtx[...] = ddtx_sum
        dAc[...] = dAc_sum

    def group(first, hb, dt, ac, acT, xs_ref, dsk_ref, dn_ref, y_ref, z_ref, nw_ref, dz_ref, dnw_ref, sp_ref, dx_ref,
              dskacc_ref, dS_ref, b_ref, c_ref, dB_ref, dC_ref):
        z = z_ref[...].astype(F32)
        yv = y_ref[...].astype(F32)
        sg = _sigmoid(z)
        silu = z * sg
        yf = yv * silu
        rn = lax.rsqrt(jnp.mean(yf * yf, axis=-1, keepdims=True) + EPS)
        yh = yf * rn
        dnv = dn_ref[...].astype(F32)
        dyh = dnv * nw_ref[...]
        dyf = rn * (dyh - yh * jnp.mean(dyh * yh, axis=-1, keepdims=True))
        dyg = dyf * silu
        dz_ref[...] = (dyf * yv * sg * (1.0 + z * (1.0 - sg))).astype(BF16)
        _acc_out(dnw_ref, _colsum(dnv * yh), first)
        Bm, Cm = b_ref[...], c_ref[...]
        S = sp_ref[0]
        dS = dS_ref[...]
        Sb, dSb = S.astype(BF16), dS.astype(BF16)
        cb = _dot(Cm, Bm, NT)
        cbT = _dot(Bm, Cm, NT)
        CmT = Cm.T
        CS = _dot(Cm, Sb)
        T1 = _dot(Bm, dSb)
        yield
        row = lax.broadcasted_iota(jnp.int32, (CH, CH), 0)
        col = lax.broadcasted_iota(jnp.int32, (CH, CH), 1)
        tril = row >= col
        triu = row <= col
        left = col < HP
        lane8 = lax.broadcasted_iota(jnp.int32, (1, LANES), 1)
        lastrow = lax.broadcasted_iota(jnp.int32, (CH, 1), 0) == CH - 1
        dCB = jnp.zeros((CH, CH), F32)
        dCBT = jnp.zeros((CH, CH), F32)
        dAc = jnp.zeros((CH, LANES), F32)
        ddtx = jnp.zeros((CH, LANES), F32)
        xd_parts, dye_parts, dec_parts, dsk_parts = [], [], [], []
        for p in range(_HG // 2):
            sl = slice(p * LANES, (p + 1) * LANES)
            j0, j1 = hb + 2 * p, hb + 2 * p + 1
            xp = xs_ref[:, sl].astype(F32)
            dyp = dyg[:, sl]
            a0, a1 = ac[:, j0:j0 + 1], ac[:, j1:j1 + 1]
            al0, al1 = ac[CH - 1:CH, j0:j0 + 1], ac[CH - 1:CH, j1:j1 + 1]
            dtl = _pair_lanes(left, dt[:, j0:j0 + 1], dt[:, j1:j1 + 1])
            X = xp * dtl
            Xb = X.astype(BF16)
            eal = _pair_lanes(left, jnp.exp(a0), jnp.exp(a1))
            dtel = _pair_lanes(left, jnp.exp(al0 - a0), jnp.exp(al1 - a1))
            T1p = T1[:, sl]
            Rm = T1p * dtel * X
            GR = dyp * (eal * CS[:, sl]) - Rm
            SdS = dS[:, sl] * S[:, sl]
            dXd = jnp.zeros((CH, LANES), F32)
            for j, aj, alj, mask in ((j0, a0, al0, left), (j1, a1, al1, jnp.logical_not(left))):
                dYm = jnp.where(mask, dyp, 0.0).astype(BF16)
                dWm = _dot(dYm, Xb, NT)
                dWmT = _dot(Xb, dYm, NT)
                yield
                e = aj - acT[j:j + 1, :]
                P = dWm * jnp.exp(jnp.where(tril, e, _NEG))
                LmT = jnp.exp(jnp.where(triu, -e, _NEG))
                PT = dWmT * LmT
                dCB = dCB + P
                dCBT = dCBT + PT
                yield
                dXd = dXd + _dot((cbT * LmT).astype(BF16), dYm)
                qd = P * cb - PT * cbT + jnp.where(mask, GR, 0.0)
                colv = jnp.sum(qd, axis=1, keepdims=True)
                tot = jnp.where(mask, Rm + jnp.exp(alj) * SdS, 0.0)
                dalast = jnp.sum(jnp.sum(tot, axis=0, keepdims=True), axis=1, keepdims=True)
                dAc = dAc + (colv + jnp.where(lastrow, dalast, 0.0)) * (lane8 == j).astype(F32)
                yield
            dX = dXd + dtel * T1p
            dXx = dX * xp
            for j, mask in ((j0, left), (j1, jnp.logical_not(left))):
                dd = jnp.sum(jnp.where(mask, dXx, 0.0), axis=1, keepdims=True)
                ddtx = ddtx + dd * (lane8 == j).astype(F32)
            dx_ref[:, sl] = (dX * dtl + dsk_ref[:, sl] * dyp).astype(BF16)
            dsk_parts.append(_colsum(dyp * xp))
            xd_parts.append(X * dtel)
            dye_parts.append(dyp * eal)
            dec_parts.append(_pair_lanes(left[0:1], jnp.exp(al0), jnp.exp(al1)))
            yield
        Xd = jnp.concatenate(xd_parts, axis=1).astype(BF16)
        dYe = jnp.concatenate(dye_parts, axis=1).astype(BF16)
        dec = jnp.concatenate(dec_parts, axis=1)
        dC_ref[...] = (_dot(dCB.astype(BF16), Bm) + _dot(dYe, Sb, NT)).astype(BF16)
        dB_ref[...] = (_dot(dCBT.astype(BF16), Cm) + _dot(Xd, dSb, NT)).astype(BF16)
        dS_ref[...] = _dot(CmT, dYe) + dec * dS
        _acc_out(dskacc_ref, jnp.concatenate(dsk_parts, axis=1), first)
        return ddtx, dAc

    zspec = pl.BlockSpec((CH, DI), lambda c: (nc - 1 - c, OFF_Z // DI))
    return pl.pallas_call(
        body, name="ssd_bwd", grid=(nc,),
        in_specs=[sp["xs"], sp["bm"], sp["cmat"], sp["col"], sp["col"], sp["col"], sp["dsk"], sp["xs"], sp["xs"],
                  zspec, sp["dsk"], sp["state"], ANY],
        out_specs=[sp["xbc"], sp["col"], sp["col"], sp["dsk"], zspec, sp["dsk"]],
        out_shape=[jax.ShapeDtypeStruct((T, DX), BF16), jax.ShapeDtypeStruct((T, LANES), F32),
                   jax.ShapeDtypeStruct((T, LANES), F32), jax.ShapeDtypeStruct((1, DI), F32),
                   jax.ShapeDtypeStruct(dproj.shape, BF16), jax.ShapeDtypeStruct((1, DI), F32)],
        scratch_shapes=[pltpu.VMEM((NS, DI), F32)], input_output_aliases={12: 4},
        compiler_params=_params(("arbitrary",)),
    )(xact, xact, xact, dt, acum, acumT, dsk_rep, dn, y, proj, norm_w, sprev, dproj)


def _merge_fwd_epilogue(proj, ya):
    T = proj.shape[0]

    def fn(ysv, ins, outs, first):
        g_ref, ya_ref = ins
        m_ref, ys_ref = outs
        ga = _sigmoid(g_ref[:, :D].astype(F32))
        gs = _sigmoid(g_ref[:, D:].astype(F32))
        m_ref[...] = (ga * ya_ref[...].astype(F32) + gs * ysv).astype(BF16)
        ys_ref[...] = ysv.astype(BF16)

    return _Epilogue(fn, (proj, ya), (((T, D), BF16), ((T, D), BF16)), 10 * D, in_windows={0: (OFF_G, 2 * D)})


def _merge_bwd_epilogue(proj, ya, ys, ncols):
    T = proj.shape[0]

    def fn(d, ins, outs, first):
        g_ref, ya_ref, ys_ref = ins
        dg_ref, dya_ref, dys_ref = outs
        ga = _sigmoid(g_ref[:, :D].astype(F32))
        gs = _sigmoid(g_ref[:, D:].astype(F32))
        dya_ref[...] = (d * ga).astype(BF16)
        dys_ref[...] = (d * gs).astype(BF16)
        dg_ref[:, :D] = (d * ya_ref[...].astype(F32) * ga * (1.0 - ga)).astype(BF16)
        dg_ref[:, D:] = (d * ys_ref[...].astype(F32) * gs * (1.0 - gs)).astype(BF16)

    window = (OFF_G, 2 * D)
    return _Epilogue(fn, (proj, ya, ys), (((T, ncols), BF16), ((T, D), BF16), ((T, D), BF16)), 16 * D,
                     in_windows={0: window}, out_windows={0: window})


_FW = 1408
_FB = FF // _FW


def _ffn_act_fwd(hv, conv_w, conv_b):
    T = hv.shape[0]
    R = _Rows(T, 256)
    tm = R.tm

    def body(h1_ref, h1p_ref, h3_ref, w_ref, b_ref, o_ref):
        keep = (pl.program_id(1) > 0).astype(F32)
        ext = jnp.concatenate([h1p_ref[...].astype(F32) * keep, h1_ref[...].astype(F32)], axis=0)
        pre = _wsum(w_ref[...], _shifts_causal(ext, 3, tm)) + b_ref[...]
        o_ref[...] = (pre * _sigmoid(pre) * h3_ref[...].astype(F32)).astype(BF16)

    return R.call(body, "ffn_act_fwd", _FB,
                  [R.tile(_FW), R.prev(_FW), R.tile(_FW, _FB), R.colvec(3, _FW), R.colvec(1, _FW)],
                  R.tile(_FW), jax.ShapeDtypeStruct((T, FF), BF16), (hv, hv, hv, conv_w, conv_b))


def _ffn_act_bwd(dg, hv, conv_w, conv_b):
    T = hv.shape[0]
    R = _Rows(T, 256)
    tm = R.tm

    def body(dg_ref, h1_ref, h1p_ref, h3_ref, w_ref, b_ref, dh3_ref, dpre_ref, dw_ref, db_ref):
        i = pl.program_id(1)
        keep = (i > 0).astype(F32)
        ext = jnp.concatenate([h1p_ref[...].astype(F32) * keep, h1_ref[...].astype(F32)], axis=0)
        sh = _shifts_causal(ext, 3, tm)
        pre = _wsum(w_ref[...], sh) + b_ref[...]
        s = _sigmoid(pre)
        d = dg_ref[...].astype(F32)
        dh3_ref[...] = (d * pre * s).astype(BF16)
        dpre = d * h3_ref[...].astype(F32) * s * (1.0 + pre * (1.0 - s))
        dpre_ref[...] = dpre.astype(BF16)
        _acc_rows(dw_ref, [_colsum(dpre * q) for q in sh], i == 0)
        _acc_out(db_ref, _colsum(dpre), i == 0)

    return R.call(
        body, "ffn_act_bwd", _FB,
        [R.tile(_FW), R.tile(_FW), R.prev(_FW), R.tile(_FW, _FB), R.colvec(3, _FW), R.colvec(1, _FW)],
        [R.tile(_FW), R.tile(_FW), R.colvec(3, _FW), R.colvec(1, _FW)],
        [jax.ShapeDtypeStruct((T, FF), BF16), jax.ShapeDtypeStruct((T, FF), BF16),
         jax.ShapeDtypeStruct((3, FF), F32), jax.ShapeDtypeStruct((1, FF), F32)],
        (dg, hv, hv, hv, conv_w, conv_b))


def _conv3_transpose(dpre, conv_w):
    T = dpre.shape[0]
    R = _Rows(T, 256)
    tm = R.tm

    def body(d_ref, dn_ref, w_ref, o_ref):
        keep = (pl.program_id(1) < R.nrow - 1).astype(F32)
        ext = jnp.concatenate([d_ref[...].astype(F32), dn_ref[...].astype(F32) * keep], axis=0)
        o_ref[...] = _wsum(w_ref[...], _shifts_anticausal(ext, 3, tm)).astype(BF16)

    return R.call(body, "ffn_conv_bwd", _FB, [R.tile(_FW), R.next(_FW), R.colvec(3, _FW)], R.tile(_FW),
                  jax.ShapeDtypeStruct((T, FF), BF16), (dpre, dpre, conv_w))


def _final_loss_epilogue(w, target):
    T = target.shape[0]

    def fn(xv, ins, outs, first):
        w_ref, t_ref = ins
        l_ref, dh_ref, dhb_ref, dw_ref = outs
        wv = w_ref[...]
        r = lax.rsqrt(jnp.mean(xv * xv, axis=-1, keepdims=True) + EPS)
        xh = xv * r
        err = xh * wv - t_ref[...]
        part = 0.5 * jnp.sum(jnp.mean(err * err, axis=-1, keepdims=True), axis=0, keepdims=True)
        _acc_out(l_ref, jnp.broadcast_to(part, l_ref.shape), first)
        dy = err * (1.0 / D)
        dxh = dy * wv
        dh = r * (dxh - xh * jnp.mean(dxh * xh, axis=-1, keepdims=True))
        dh_ref[...] = dh
        dhb_ref[...] = dh.astype(BF16)
        _acc_out(dw_ref, _colsum(dy * xh), first)

    return _Epilogue(fn, (w, target),
                     (((8, LANES), F32), ((T, D), F32), ((T, D), BF16), ((1, D), F32)), 10 * D)


def _pad_lanes(v, n=LANES):
    return jnp.pad(v, ((0, 0), (0, n - v.shape[1])))


class _Hooks:
    def before_in_proj(self, w_in):
        return w_in

    def late_weights(self, wts, after):
        return wts

    def grads_ready(self, grads, tie):
        return tie

    def mark(self, name, value):
        pass


def _local_step(x, target, wts, hooks=None):
    hooks = hooks or _Hooks()
    T = x.shape[0]
    w_in = wts["w_in"]
    dt_bias_p, a_log_p = _pad_lanes(wts["dt_bias"]), _pad_lanes(wts["a_log"])
    dsk_rep = jnp.repeat(wts["d_skip"], HP, axis=1)

    w_in = hooks.before_in_proj(w_in)
    proj, u, dt_raw = _norm_matmul(x, wts["norm_mix_w"], w_in, "norm_mm_in", w_in[:, OFF_DT:OFF_DT + LANES])
    ya_in = _branch_a_fwd(proj, wts["conv_a_w"])
    xact = _xbc_fwd(proj, wts["ssd_conv_w"], wts["ssd_conv_b"])
    dt, acum, acumT = _dt_fwd(dt_raw, dt_bias_p, a_log_p)
    y_ssd, yn, sprev = _ssd_fwd(xact, dt, acum, acumT, dsk_rep, proj, wts["ssd_norm_w"])
    late = hooks.late_weights(wts, yn)
    w_a_out, w_s_out, w_o, w_up, w_down = (late[k] for k in ("w_a_out", "w_s_out", "w_o", "w_up", "w_down"))
    y_a = _matmul(ya_in, w_a_out, mode="nn", out_dtype=BF16, name="mm_a_out")
    merged, y_s = _matmul(yn, w_s_out, mode="nn", out_dtype=BF16, name="mm_s_out_merge",
                          epilogue=_merge_fwd_epilogue(proj, y_a))
    h1 = _matmul(merged, w_o, mode="nn", out_dtype=F32, name="mm_o", residual=x)
    hv, v = _norm_matmul(h1, wts["norm_ffn_w"], w_up, "norm_mm_up")
    gact = _ffn_act_fwd(hv, wts["ffn_conv_w"], wts["ffn_conv_b"])
    loss, dh2, dh2b, g_final = _matmul(gact, w_down, mode="nn", out_dtype=F32, name="mm_down_loss", residual=h1,
                                       epilogue=_final_loss_epilogue(wts["final_norm_w"], target))

    grads = {"final_norm_w": g_final}
    grads["w_down"] = _matmul(gact, dh2b, mode="tn", out_dtype=F32, name="mm_down_dw")
    dgact = _matmul(dh2b, w_down, mode="nt", out_dtype=BF16, name="mm_down_dx")
    dh3, dpre, grads["ffn_conv_w"], grads["ffn_conv_b"] = _ffn_act_bwd(dgact, hv, wts["ffn_conv_w"], wts["ffn_conv_b"])
    dh1c = _conv3_transpose(dpre, wts["ffn_conv_w"])
    grads["w_up"] = (_matmul(v, dh1c, mode="tn", out_dtype=F32, name="mm_up_dw1"),
                     _matmul(v, dh3, mode="tn", out_dtype=F32, name="mm_up_dw3"))
    dv = _matmul(dh1c, w_up, mode="nt", out_dtype=F32, name="mm_up_dx1")
    dh1, dh1b, grads["norm_ffn_w"] = _matmul(
        dh3, w_up, mode="nt", out_dtype=F32, name="mm_up_dx3_norm", residual=dv, b_k_off=FF,
        epilogue=_rmsnorm_bwd_epilogue(h1, wts["norm_ffn_w"], dh2))
    grads["w_o"] = _matmul(merged, dh1b, mode="tn", out_dtype=F32, name="mm_o_dw")
    dproj, dya, dys = _matmul(dh1b, w_o, mode="nt", out_dtype=BF16, name="mm_o_dx_merge",
                              epilogue=_merge_bwd_epilogue(proj, y_a, y_s, NIP))
    grads["w_a_out"] = _matmul(ya_in, dya, mode="tn", out_dtype=F32, name="mm_a_out_dw")
    dya_in = _matmul(dya, w_a_out, mode="nt", out_dtype=BF16, name="mm_a_out_dx")
    dproj, grads["conv_a_w"] = _branch_a_bwd(dya_in, proj, wts["conv_a_w"], dproj)
    grads["w_s_out"] = _matmul(yn, dys, mode="tn", out_dtype=F32, name="mm_s_out_dw")
    dys = hooks.grads_ready({k: grads[k] for k in ("w_a_out", "w_s_out", "w_o", "w_up", "w_down")}, dys)
    dyn =_matmul(dys, w_s_out, mode="nt", out_dtype=BF16, name="mm_s_out_dx")
    dxact, ddt_x, dacum, dskl, dproj, grads["ssd_norm_w"] = _ssd_bwd(
        dyn, y_ssd, proj, wts["ssd_norm_w"], dproj, xact, dt, acum, acumT, dsk_rep, sprev)
    hooks.mark("ssd_bwd", dxact)
    grads["d_skip"] = dskl.reshape(NH, HP).sum(axis=1).reshape(1, NH)
    dproj, grads["ssd_conv_w"], grads["ssd_conv_b"] = _xbc_bwd(dxact, proj, wts["ssd_conv_w"], wts["ssd_conv_b"], dproj)
    dproj, g_dtb, g_alog = _dt_bwd(dacum, ddt_x, dt_raw, dt_bias_p, a_log_p, dproj)
    grads["dt_bias"], grads["a_log"] = g_dtb[:, :NH], g_alog[:, :NH]
    grads["w_in"] = _matmul(u, dproj, mode="tn", out_dtype=F32, name="mm_in_dw")
    dproj = hooks.grads_ready({"w_in": grads["w_in"]}, dproj)
    grad_x, _, grads["norm_mix_w"] = _matmul(dproj, w_in, mode="nt", out_dtype=F32, name="mm_in_dx_norm",
                                             epilogue=_rmsnorm_bwd_epilogue(x, wts["norm_mix_w"], dh1))
    return loss, grad_x, grads


def _permute_w_in(slabs):
    cs = slabs.shape[2]
    pieces = []
    for o, n, no in sorted(_SEGS, key=lambda seg: seg[2]):
        for s in range(slabs.shape[0]):
            lo, hi = max(o, s * cs), min(o + n, (s + 1) * cs)
            if lo < hi:
                pieces.append(slabs[s][:, lo - s * cs:hi - s * cs])
    pieces.append(jnp.zeros((slabs.shape[1], NIP - OFF_DT - _SEGS[-1][1]), slabs.dtype))
    return jnp.concatenate(pieces, axis=1)


def _unpermute_w_in(g):
    cs = NI // NCHIP
    slabs = []
    for s in range(NCHIP):
        pieces = []
        for o, n, no in sorted(_SEGS):
            lo, hi = max(o, s * cs), min(o + n, (s + 1) * cs)
            if lo < hi:
                pieces.append(g[:, no + lo - o:no + hi - o])
        slabs.append(jnp.concatenate(pieces, axis=1))
    return jnp.stack(slabs)


MESH = pl.DeviceIdType.MESH
NCHIP = 4
NDEV = 8

_W_IN = (("w_in", D, NI // NCHIP, 1),)
_W_REST = (("w_a_out", D // NCHIP, D, 0), ("w_s_out", DI // NCHIP, D, 0), ("w_o", D // NCHIP, D, 0),
           ("w_up", D, 2 * FF // NCHIP, 1), ("w_down", FF // NCHIP, D, 0))


def _coords():
    return lax.axis_index("x"), lax.axis_index("y"), lax.axis_index("c")


def _other_chips(x, y):
    return [(1 - x, y), (x, 1 - y), (1 - x, 1 - y)]


def _ag_weights(shard):
    nrows = shard.shape[0]
    hr = nrows // 2

    def body(x_ref, out_ref, send_sems, recv_sems, local_sem):
        x, y, c = _coords()
        me = 2 * x + y
        chips = _other_chips(x, y)

        def rows(s, h):
            return out_ref.at[s, pl.ds(h * hr, hr), :]

        def copy(k, s, h, to, src=None):
            return pltpu.make_async_remote_copy(
                src_ref=rows(s, h) if src is None else src, dst_ref=rows(s, h),
                send_sem=send_sems.at[k], recv_sem=recv_sems.at[k], device_id=to, device_id_type=MESH)

        mine = pltpu.make_async_copy(x_ref, out_ref.at[me], local_sem)
        mine.start()
        first = [copy(k, me, c, (*chip, c), src=x_ref.at[pl.ds(c * hr, hr), :]) for k, chip in enumerate(chips)]
        for cp in first:
            cp.start()
        passed = []
        for k, chip in enumerate(chips):
            s = 2 * chip[0] + chip[1]
            copy(k, s, c, (x, y, c)).wait_recv()
            fwd = copy(3 + k, s, c, (x, y, 1 - c))
            fwd.start()
            passed.append(fwd)
        for k, chip in enumerate(chips):
            copy(3 + k, 2 * chip[0] + chip[1], 1 - c, (x, y, c)).wait_recv()
        for cp in first + passed:
            cp.wait_send()
        mine.wait()

    return pl.pallas_call(
        body, name="ag_weights", in_specs=[ANY], out_specs=ANY,
        out_shape=jax.ShapeDtypeStruct((NCHIP,) + shard.shape, shard.dtype),
        scratch_shapes=[pltpu.SemaphoreType.DMA((6,)), pltpu.SemaphoreType.DMA((6,)), pltpu.SemaphoreType.DMA],
        compiler_params=pltpu.CompilerParams(has_side_effects=True),
    )(shard)


HBM = pl.BlockSpec(memory_space=pltpu.HBM)
SEM = pl.BlockSpec(memory_space=pltpu.SEMAPHORE)
_EFFECT = pltpu.SideEffectType.DATAFLOW_SIDE_EFFECTING
_NCOPY = NCHIP - 1


def _plan_bcast(src_ref, land_ref, send_sems, recv_sems, base):
    x, y, c = _coords()
    sends, lands = [], []
    for k, chip in enumerate(_other_chips(x, y)):
        def copy(slot):
            return pltpu.make_async_remote_copy(
                src_ref=src_ref, dst_ref=land_ref.at[slot], send_sem=send_sems.at[base + k],
                recv_sem=recv_sems.at[base + k], device_id=(*chip, c), device_id_type=MESH)
        sends.append(copy(2 * x + y))
        lands.append(copy(2 * chip[0] + chip[1]))
    return sends, lands


def _plan_scatter(src_ref, land_ref, send_sems, recv_sems, base):
    x, y, c = _coords()
    cps = [pltpu.make_async_remote_copy(
        src_ref=src_ref.at[2 * chip[0] + chip[1]], dst_ref=land_ref.at[k], send_sem=send_sems.at[base + k],
        recv_sem=recv_sems.at[base + k], device_id=(*chip, c), device_id_type=MESH)
        for k, chip in enumerate(_other_chips(x, y))]
    return cps, cps


def _plan_all(plan, refs, n):
    sends, lands = [], []
    for t in range(n):
        s, l = plan(refs[t], refs[n + t], refs[2 * n], refs[2 * n + 1], t * _NCOPY)
        sends += s
        lands += l
    return sends, lands


def _split_start(name, srcs, lands, plan):
    n = len(srcs)

    def body(*refs):
        for cp in _plan_all(plan, refs, n)[0]:
            cp.start()
        refs[-1][...] = jnp.zeros_like(refs[-1])

    arrays = list(srcs) + list(lands)
    outs = pl.pallas_call(
        body, name=name,
        out_shape=(pltpu.SemaphoreType.DMA((n * _NCOPY,)), pltpu.SemaphoreType.DMA((n * _NCOPY,)),
                   *[pltpu.HBM(a.shape, a.dtype) for a in arrays], jax.ShapeDtypeStruct((8, LANES), F32)),
        in_specs=(HBM,) * (2 * n),
        out_specs=(SEM, SEM) + (HBM,) * (2 * n) + (pl.BlockSpec(memory_space=pltpu.VMEM),),
        input_output_aliases={t: 2 + t for t in range(2 * n)},
        compiler_params=pltpu.CompilerParams(has_side_effects=_EFFECT),
    )(*[pltpu.with_memory_space_constraint(a, pltpu.HBM) for a in arrays])
    return (outs[0], outs[1], tuple(outs[2:2 + 2 * n])), outs[-1]


def _split_wait(name, handle, after, plan):
    send_sems, recv_sems, arrays = handle
    n = len(arrays) // 2

    def body(*refs):
        sends, lands = _plan_all(plan, refs[:2 * n] + refs[2 * n:2 * n + 2], n)
        for cp in sends:
            cp.wait_send()
        for cp in lands:
            cp.wait_recv()

    outs = pl.pallas_call(
        body, name=name, out_shape=tuple(pltpu.HBM(a.shape, a.dtype) for a in arrays),
        in_specs=(HBM,) * (2 * n) + (SEM, SEM, ANY), out_specs=(HBM,) * (2 * n),
        input_output_aliases={t: t for t in range(2 * n)},
        compiler_params=pltpu.CompilerParams(has_side_effects=_EFFECT),
    )(*arrays, send_sems, recv_sems, after)
    return outs[:n], outs[n:]


def _tie(x, token, name):
    def body(x_ref, t_ref, o_ref):
        pass

    return pl.pallas_call(
        body, name=name, in_specs=[ANY, pl.BlockSpec(memory_space=pltpu.VMEM)], out_specs=ANY,
        out_shape=jax.ShapeDtypeStruct(x.shape, x.dtype), input_output_aliases={0: 0},
    )(x, token)


def _swap_sibling(ps, name):
    n = len(ps)

    def body(*refs):
        x, y, c = _coords()
        cps = [pltpu.make_async_remote_copy(
            src_ref=refs[t], dst_ref=refs[n + t], send_sem=refs[2 * n].at[t], recv_sem=refs[2 * n + 1].at[t],
            device_id=(x, y, 1 - c), device_id_type=MESH) for t in range(n)]
        for cp in cps:
            cp.start()
        for cp in cps:
            cp.wait()

    return pl.pallas_call(
        body, name=name, in_specs=[ANY] * n, out_specs=[ANY] * n,
        out_shape=[jax.ShapeDtypeStruct(p.shape, p.dtype) for p in ps],
        scratch_shapes=[pltpu.SemaphoreType.DMA((n,)), pltpu.SemaphoreType.DMA((n,))],
        compiler_params=pltpu.CompilerParams(has_side_effects=True),
    )(*ps)


_ADD_BYTES = 7 << 19


def _add_tile(rows, cols):
    best = 32
    for t in range(32, rows + 1, 32):
        if rows % t == 0 and t * cols * 4 <= _ADD_BYTES:
            best = t
    return best


def _add_slabs(pack, land, me, name):
    rows, cols = pack.shape[1:]
    tr = _add_tile(rows, cols)

    def body(me_ref, p_ref, l_ref, o_ref):
        f = lambda r: r.astype(F32)
        o_ref[...] = ((f(p_ref[0]) + f(l_ref[0])) + f(l_ref[1])) + f(l_ref[2])

    return pl.pallas_call(
        body, name=name,
        grid_spec=pltpu.PrefetchScalarGridSpec(
            num_scalar_prefetch=1, grid=(rows // tr,),
            in_specs=[pl.BlockSpec((1, tr, cols), lambda i, me_ref: (me_ref[0], i, 0)),
                      pl.BlockSpec((_NCOPY, tr, cols), lambda i, me_ref: (0, i, 0))],
            out_specs=pl.BlockSpec((tr, cols), lambda i, me_ref: (i, 0))),
        out_shape=jax.ShapeDtypeStruct((rows, cols), F32),
        compiler_params=_params(("parallel",)),
    )(me, pack, land)


_STAGE_W = 1024


def _stage_rows(shapes):
    pieces, r = [], 0
    for i, (k, w) in enumerate(shapes):
        for a in range(k):
            for q in range(0, w, _STAGE_W):
                pieces.append((i, a, q, min(_STAGE_W, w - q), r))
                r += 1
    return pieces, -(-r // 8) * 8


def _gather8(parts, reduce, name):
    shapes = [p.shape for p in parts]
    pieces, rows = _stage_rows(shapes)
    n = len(parts)

    def body(*refs):
        ins, outs = refs[:n], refs[n:2 * n]
        stage, buf, res, send_sems, recv_sems = refs[2 * n:]
        x, y, c = _coords()
        me = 4 * x + 2 * y + c
        stage[...] = jnp.zeros_like(stage)
        for i, a, q, w, r in pieces:
            stage[r:r + 1, 0:w] = ins[i][a:a + 1, q:q + w]
        buf[pl.ds(me, 1)] = stage[...][None]
        cps, lands = [], []
        for k in range(1, NDEV):
            peer = (1 - x if k & 4 else x, 1 - y if k & 2 else y, 1 - c if k & 1 else c)

            def copy(slot):
                return pltpu.make_async_remote_copy(
                    src_ref=stage, dst_ref=buf.at[slot], send_sem=send_sems.at[k - 1],
                    recv_sem=recv_sems.at[k - 1], device_id=peer, device_id_type=MESH)

            cps.append(copy(me))
            lands.append(copy(4 * peer[0] + 2 * peer[1] + peer[2]))
        for cp in cps:
            cp.start()
        for cp, land in zip(cps, lands):
            land.wait_recv()
            cp.wait_send()
        if reduce:
            acc = buf[0]
            for d in range(1, NDEV):
                acc = acc + buf[d]
            res[...] = acc
            for i, a, q, w, r in pieces:
                outs[i][a:a + 1, q:q + w] = res[r:r + 1, 0:w]
        else:
            for i, a, q, w, r in pieces:
                for s in range(NCHIP):
                    outs[i][s, a:a + 1, q:q + w] = buf[2 * s, r:r + 1, 0:w]

    vm = pl.BlockSpec(memory_space=pltpu.VMEM)
    out_shapes = [jax.ShapeDtypeStruct(s if reduce else (NCHIP,) + s, F32) for s in shapes]
    return pl.pallas_call(
        body, name=name, in_specs=[vm] * n, out_specs=[vm] * n, out_shape=out_shapes,
        scratch_shapes=[pltpu.VMEM((rows, _STAGE_W), F32), pltpu.VMEM((NDEV, rows, _STAGE_W), F32),
                        pltpu.VMEM((rows, _STAGE_W), F32), pltpu.SemaphoreType.DMA((NDEV - 1,)),
                        pltpu.SemaphoreType.DMA((NDEV - 1,))],
        compiler_params=pltpu.CompilerParams(has_side_effects=True),
    )(*parts)


def _adamw_update(w_ref, g_ref, m_ref, v_ref, d_ref, mo_ref, vo_ref):
    c1 = 1.0 / (1.0 - ADAM_B1 ** ADAM_STEP)
    c2 = 1.0 / (1.0 - ADAM_B2 ** ADAM_STEP)
    gv = g_ref[...]
    mn = ADAM_B1 * m_ref[...] + (1.0 - ADAM_B1) * gv
    vn = ADAM_B2 * v_ref[...] + (1.0 - ADAM_B2) * (gv * gv)
    d_ref[...] = -ADAM_LR * ((mn * c1) / (jnp.sqrt(vn * c2) + ADAM_EPS) + ADAM_WD * w_ref[...])
    mo_ref[...] = mn
    vo_ref[...] = vn


def _adamw_small(ws, gs, ms, vs):
    n = len(ws)

    def body(*refs):
        for i in range(n):
            _adamw_update(*(refs[j * n + i] for j in range(7)))

    vm = pl.BlockSpec(memory_space=pltpu.VMEM)
    outs = pl.pallas_call(
        body, name="adamw_small", in_specs=[vm] * (4 * n), out_specs=[vm] * (3 * n),
        out_shape=[jax.ShapeDtypeStruct(w.shape, F32) for w in ws] * 3,
    )(*ws, *gs, *ms, *vs)
    return outs[:n], outs[n:2 * n], outs[2 * n:]


def _adamw(w, g_parts, m, v, name):
    rows, cols = w.shape
    tr = rows
    while tr * cols * 4 > (1 << 20) and tr % 16 == 0:
        tr //= 2

    def body(w_ref, ga_ref, gb_ref, m_ref, v_ref, g_ref, d_ref, mo_ref, vo_ref):
        g_ref[...] = ga_ref[...] + gb_ref[...]
        _adamw_update(w_ref, g_ref, m_ref, v_ref, d_ref, mo_ref, vo_ref)

    blk = pl.BlockSpec((tr, cols), lambda i: (i, 0))
    return pl.pallas_call(
        body, name=name, grid=(rows // tr,), in_specs=[blk] * 5, out_specs=[blk] * 4,
        out_shape=[jax.ShapeDtypeStruct((rows, cols), F32)] * 4, compiler_params=_params(("parallel",)),
    )(w, *g_parts, m, v)


def _by_chip(g, rr, cc, axis):
    if isinstance(g, tuple):
        n = NCHIP // len(g)
        return jnp.concatenate([h.reshape(rr, n, cc).transpose(1, 0, 2) for h in g], axis=0)
    return g.reshape(NCHIP, rr, cc) if axis == 0 else g.reshape(rr, NCHIP, cc).transpose(1, 0, 2)


_SMALL_REPL = ("norm_mix_w", "ssd_conv_b", "dt_bias", "a_log", "d_skip", "ssd_norm_w", "norm_ffn_w",
               "ffn_conv_b", "final_norm_w")
_SMALL_CONV = (("conv_a_w", 3, D), ("ssd_conv_w", 4, DX), ("ffn_conv_w", 3, FF))


def kernel(x, norm_mix_w, w_in, conv_a_w, w_a_out, ssd_conv_w, ssd_conv_b, dt_bias, a_log, d_skip, ssd_norm_w, w_s_out, w_o, norm_ffn_w, w_up, ffn_conv_w, ffn_conv_b, w_down, final_norm_w, loss_target, m_norm_mix_w, m_w_in, m_conv_a_w, m_w_a_out, m_ssd_conv_w, m_ssd_conv_b, m_dt_bias, m_a_log, m_d_skip, m_ssd_norm_w, m_w_s_out, m_w_o, m_norm_ffn_w, m_w_up, m_ffn_conv_w, m_ffn_conv_b, m_w_down, m_final_norm_w, v_norm_mix_w, v_w_in, v_conv_a_w, v_w_a_out, v_ssd_conv_w, v_ssd_conv_b, v_dt_bias, v_a_log, v_d_skip, v_ssd_norm_w, v_w_s_out, v_w_o, v_norm_ffn_w, v_w_up, v_ffn_conv_w, v_ffn_conv_b, v_w_down, v_final_norm_w):
    names = ("norm_mix_w", "w_in", "conv_a_w", "w_a_out", "ssd_conv_w", "ssd_conv_b", "dt_bias", "a_log", "d_skip",
             "ssd_norm_w", "w_s_out", "w_o", "norm_ffn_w", "w_up", "ffn_conv_w", "ffn_conv_b", "w_down", "final_norm_w")
    W = dict(zip(names, (norm_mix_w, w_in, conv_a_w, w_a_out, ssd_conv_w, ssd_conv_b, dt_bias, a_log, d_skip,
                         ssd_norm_w, w_s_out, w_o, norm_ffn_w, w_up, ffn_conv_w, ffn_conv_b, w_down, final_norm_w)))
    M = dict(zip(names, (m_norm_mix_w, m_w_in, m_conv_a_w, m_w_a_out, m_ssd_conv_w, m_ssd_conv_b, m_dt_bias, m_a_log,
                         m_d_skip, m_ssd_norm_w, m_w_s_out, m_w_o, m_norm_ffn_w, m_w_up, m_ffn_conv_w, m_ffn_conv_b,
                         m_w_down, m_final_norm_w)))
    V = dict(zip(names, (v_norm_mix_w, v_w_in, v_conv_a_w, v_w_a_out, v_ssd_conv_w, v_ssd_conv_b, v_dt_bias, v_a_log,
                         v_d_skip, v_ssd_norm_w, v_w_s_out, v_w_o, v_norm_ffn_w, v_w_up, v_ffn_conv_w, v_ffn_conv_b,
                         v_w_down, v_final_norm_w)))
    two_d = lambda a: a.reshape(-1, a.shape[-1])
    W2, M2, V2 = ({k: two_d(a) for k, a in t.items()} for t in (W, M, V))
    xi, yi, ci = _coords()
    me = 2 * xi + yi

    meidx = me.reshape(1).astype(jnp.int32)
    state = {}


    class Hooks(_Hooks):
        def before_in_proj(self, w_in):
            return _tie(w_in, state["rest_token"], "tie_ag_rest")

        def late_weights(self, wts, after):
            owns, lands = _split_wait("ag_rest_wait", state["rest"], after, _plan_bcast)
            full = {}
            for (n, rr, cc, axis), own, land in zip(_W_REST, owns, lands):
                slabs = lax.dynamic_update_slice(land, own[None], (me, 0, 0))
                full[n] = slabs.reshape(NCHIP * rr, cc) if axis == 0 else slabs.transpose(1, 0, 2).reshape(rr, NCHIP * cc)
            return {**wts, **full}

        def grads_ready(self, grads, tie):
            if "w_in" in grads:
                key, packs = "g_in", [_unpermute_w_in(grads["w_in"]).astype(BF16)]
            else:
                key = "g_rest"
                packs = [_by_chip(jax.tree.map(lambda t: t.astype(BF16), grads[n]), rr, cc, axis)
                         for n, rr, cc, axis in _W_REST]
            lands = [lax.empty((_NCOPY,) + p.shape[1:], BF16) for p in packs]
            state[key], token = _split_start("rs_" + key + "_start", packs, lands, _plan_scatter)
            return _tie(tie, token, "tie_" + key)

        def mark(self, name, value):
            state[name] = value

    def reduced(key, after, group):
        packs, lands = _split_wait("rs_" + key + "_wait", state[key], after, _plan_scatter)
        mines = [_add_slabs(p, l, meidx, "rs_add_chips_" + n) for (n, *_), p, l in zip(group, packs, lands)]
        return dict(zip([n for n, *_ in group], zip(mines, _swap_sibling(mines, "rs_" + key + "_swap"))))

    w_in_slabs = _ag_weights(W2["w_in"].astype(BF16))
    wts = {k: W2[k] for k in _SMALL_REPL}
    conv_by_chip = _gather8([W2[n] for n, *_ in _SMALL_CONV], False, "ag_conv_weights")
    for (n, kk, width), stacked in zip(_SMALL_CONV, conv_by_chip):
        wts[n] = stacked.transpose(1, 0, 2).reshape(kk, width)
    rest = [W2[n].astype(BF16) for n, *_ in _W_REST]
    rest[0] = _tie(rest[0], conv_by_chip[0], "tie_ag_order")
    state["rest"], state["rest_token"] = _split_start(
        "ag_rest_start", rest, [lax.empty((NCHIP,) + r.shape, BF16) for r in rest], _plan_bcast)
    wts["w_in"] = _permute_w_in(w_in_slabs)

    loss8, grad_x, grads = _local_step(x[0], loss_target[0], wts, Hooks())

    gbig = {**reduced("g_rest", state["ssd_bwd"], _W_REST), **reduced("g_in", grad_x, _W_IN)}

    small_parts = [grads[n] for n in _SMALL_REPL] + [loss8[0:1]] + [grads[n] for n, *_ in _SMALL_CONV]
    small_g = _gather8(small_parts, True, "allreduce_small")
    gsm = dict(zip(_SMALL_REPL, small_g[:len(_SMALL_REPL)]))
    loss = small_g[len(_SMALL_REPL)][0, 0]
    for (n, kk, width), gfull in zip(_SMALL_CONV, small_g[len(_SMALL_REPL) + 1:]):
        cw = width // NCHIP
        gsm[n] = lax.dynamic_slice(gfull, (0, me * cw), (kk, cw))

    G, DW, NM, NV = {}, {}, {}, {}
    for n in [b[0] for b in _W_IN + _W_REST]:
        G[n], DW[n], NM[n], NV[n] = _adamw(W2[n], gbig[n], M2[n], V2[n], "adamw_" + n)
    sm_names = list(_SMALL_REPL) + [n for n, *_ in _SMALL_CONV]
    outs = _adamw_small(*([t[n] for n in sm_names] for t in (W2, gsm, M2, V2)))
    for t, vals in zip((DW, NM, NV), outs):
        t.update(zip(sm_names, vals))
    G.update(gsm)

    def shaped(t):
        return [t[n].reshape(W[n].shape) for n in names]

    return (loss, grad_x.reshape(x.shape), *shaped(G), *shaped(DW), *shaped(NM), *shaped(NV))
```

```python
import jax
import jax.numpy as jnp
from jax import lax
from jax.experimental import pallas as pl
from jax.experimental.pallas import tpu as pltpu

F32 = jnp.float32
BF16 = jnp.bfloat16

D = 1024
DI = 2048
NH = 32
HP = 64
NG = 4
NS = 128
CH = 128
DX = 3072
FF = 2816
NI = 10272
EPS = 1e-5

OFF_BCV, OFF_XBC, OFF_G, OFF_Z, OFF_DT = 0, 3072, 6144, 8192, 10240
NIP = 10752
_SEGS = ((0, 2048, OFF_G), (2048, 3072, OFF_BCV), (5120, 2048, OFF_Z), (7168, 3072, OFF_XBC), (10240, 32, OFF_DT))

LANES = 128
HALO = 16
V7X_VMEM_LIMIT = 56 * 2 ** 20

ADAM_LR, ADAM_B1, ADAM_B2, ADAM_EPS, ADAM_WD, ADAM_STEP = 0.001, 0.9, 0.999, 1e-08, 0.01, 10

NN = (((1,), (0,)), ((), ()))
NT = (((1,), (1,)), ((), ()))
TN = (((0,), (0,)), ((), ()))


def _dot(a, b, dims=NN):
    return lax.dot_general(a, b, dims, preferred_element_type=F32)


def _params(sem, **kw):
    return pltpu.CompilerParams(dimension_semantics=sem, vmem_limit_bytes=V7X_VMEM_LIMIT, **kw)


V7X_MXU = 256
V7X_HBM_BYTES_PER_S = 3.5e12
STEP_S = 0.35e-6
MATMUL_VMEM = 40 * 2 ** 20
EPILOGUE_VMEM = 46 * 2 ** 20


ACC_BYTES_PER_S = 1.2e13


def _divisors(dim, cap, units):
    for unit in units:
        c = [t for t in range(unit, min(dim, cap) + 1, unit) if dim % t == 0]
        if c:
            return c
    return [dim]


def _tiles(M, N, K, out_bytes, has_res):
    best = None
    for tn in _divisors(N, 2816, (V7X_MXU, LANES)):
        for tm in _divisors(M, 2816, (LANES,)):
            for tk in _divisors(K, 2816, (V7X_MXU, LANES)):
                nk, ni, nj = K // tk, M // tm, N // tn
                vmem = 4 * (tm * tk + tk * tn) + 2 * tm * tn * out_bytes
                vmem += (4 * tm * tn if nk > 1 else 0) + (8 * tm * tn if has_res else 0)
                if vmem > MATMUL_VMEM:
                    continue
                a_reads = M * K * 2 * (nj if nk > 1 else 1)
                b_reads = K * N * 2 * (ni if nk * nj > 1 else 1)
                cost = (a_reads + b_reads + M * N * out_bytes) / V7X_HBM_BYTES_PER_S + ni * nj * nk * STEP_S
                cost += (nk - 1) * M * N * 8 / ACC_BYTES_PER_S
                if best is None or cost < best[0]:
                    best = (cost, tm, tn, tk)
    assert best is not None, (M, N, K)
    return best[1:]


def _sigmoid(x):
    return 1.0 / (1.0 + jnp.exp(-x))


class _Epilogue:
    def __init__(self, fn, ins, outs, tile_bytes, in_windows=None, out_windows=None):
        self.fn, self.ins, self.outs, self.tile_bytes = fn, tuple(ins), tuple(outs), tile_bytes
        self.in_windows, self.out_windows = in_windows or {}, out_windows or {}


def _matmul(a, b, *, mode, out_dtype, name, residual=None, b_k_off=0, epilogue=None):
    if mode == "nn":
        (M, K), (K2, N) = a.shape, b.shape
    elif mode == "nt":
        (M, K), (N, K2) = a.shape, (b.shape[0], a.shape[1])
        assert b_k_off + K <= b.shape[1]
    else:
        (K, M), (K2, N) = a.shape, b.shape
    assert K == K2, (name, a.shape, b.shape)
    tm, tn, tk = _tiles(M, N, K, jnp.dtype(out_dtype).itemsize, residual is not None)
    if epilogue is not None:
        tn = N
        fits = [(K * N * 2 * (M // t) / V7X_HBM_BYTES_PER_S + (K // q - 1) * M * N * 8 / ACC_BYTES_PER_S
                 + (M // t) * (K // q) * STEP_S, t, q)
                for t in (1024, 512, 256) if M % t == 0 for q in _divisors(K, 2816, (V7X_MXU, LANES))
                if 4 * (t * q + q * tn) + (4 * t * tn if K > q else 0) + (8 * t * tn if residual is not None else 0)
                + 2 * t * epilogue.tile_bytes <= EPILOGUE_VMEM]
        _, tm, tk = min(fits)
    nk = K // tk
    if mode == "tn":
        a_spec = pl.BlockSpec((tk, tm), lambda i, j, k: (k, i))
    else:
        a_spec = pl.BlockSpec((tm, tk), lambda i, j, k: (i, k))
    if mode == "nt":
        assert b_k_off % tk == 0
        b_spec = pl.BlockSpec((tn, tk), lambda i, j, k: (j, k + b_k_off // tk))
    else:
        b_spec = pl.BlockSpec((tk, tn), lambda i, j, k: (k, j))
    dims = {"nn": NN, "nt": NT, "tn": TN}[mode]
    o_spec = pl.BlockSpec((tm, tn), lambda i, j, k: (i, j))
    has_res = residual is not None

    def rows_or_whole(shape, window=None):
        if window is not None:
            off, width = window
            return pl.BlockSpec((tm, width), lambda i, j, k: (i, off // width))
        if shape[0] == M:
            return pl.BlockSpec((tm,) + tuple(shape[1:]), lambda i, j, k: (i,) + (0,) * (len(shape) - 1))
        return pl.BlockSpec(tuple(shape), lambda i, j, k: (0,) * len(shape))

    n_in = 2 + has_res + (len(epilogue.ins) if epilogue else 0)
    n_out = len(epilogue.outs) if epilogue else 1

    def body(*refs):
        a_ref, b_ref = refs[:2]
        r_ref = refs[2] if has_res else None
        out_refs = refs[n_in:n_in + n_out]
        acc_ref = refs[-1]
        k = pl.program_id(2)
        part = _dot(a_ref[...], b_ref[...], dims)

        def finish(r):
            if has_res:
                r = r + r_ref[...].astype(F32)
            if epilogue is None:
                out_refs[0][...] = r.astype(out_dtype)
            else:
                epilogue.fn(r, refs[2 + has_res:n_in], out_refs, pl.program_id(0) == 0)

        if nk == 1:
            finish(part)
            return

        @pl.when(k == 0)
        def _():
            acc_ref[...] = part

        @pl.when(jnp.logical_and(k > 0, k < nk - 1))
        def _():
            acc_ref[...] += part

        @pl.when(k == nk - 1)
        def _():
            finish(acc_ref[...] + part)

    in_specs = [a_spec, b_spec] + ([o_spec] if has_res else [])
    args = (a, b) + ((residual,) if has_res else ())
    if epilogue is None:
        out_specs, out_shape = o_spec, jax.ShapeDtypeStruct((M, N), out_dtype)
        sem = ("parallel", "parallel", "arbitrary")
    else:
        in_specs += [rows_or_whole(x.shape, epilogue.in_windows.get(n)) for n, x in enumerate(epilogue.ins)]
        args += epilogue.ins
        out_specs = [rows_or_whole(o[0], epilogue.out_windows.get(n)) for n, o in enumerate(epilogue.outs)]
        out_shape = [jax.ShapeDtypeStruct(shp, dt) for shp, dt in epilogue.outs]
        sem = ("arbitrary", "arbitrary", "arbitrary")
    return pl.pallas_call(
        body, name=name, grid=(M // tm, N // tn, nk), in_specs=in_specs, out_specs=out_specs,
        out_shape=out_shape, scratch_shapes=[pltpu.VMEM((tm, tn), F32)] if nk > 1 else [],
        compiler_params=_params(sem),
    )(*args)


class _Rows:
    def __init__(self, T, tm):
        self.T, self.tm = T, min(tm, T // 2)
        self.nrow = T // self.tm
        self.r = self.tm // HALO
        self.nb = T // HALO

    def tile(self, w, cb=0, step=1):
        return pl.BlockSpec((self.tm, w), lambda j, i: (i, cb + step * j))

    def prev(self, w, cb=0, step=1):
        r = self.r
        return pl.BlockSpec((HALO, w), lambda j, i: (jnp.maximum(i * r - 1, 0), cb + step * j))

    def next(self, w, cb=0, step=1):
        r, nb = self.r, self.nb
        return pl.BlockSpec((HALO, w), lambda j, i: (jnp.minimum((i + 1) * r, nb - 1), cb + step * j))

    def colvec(self, k, w, cb=0, step=1):
        return pl.BlockSpec((k, w), lambda j, i: (0, cb + step * j))

    def call(self, body, name, ncol, in_specs, out_specs, out_shape, args, aliases=None):
        return pl.pallas_call(
            body, name=name, grid=(ncol, self.nrow), in_specs=in_specs, out_specs=out_specs,
            out_shape=out_shape, input_output_aliases=aliases or {},
            compiler_params=_params(("parallel", "arbitrary")),
        )(*args)


ANY = pl.BlockSpec(memory_space=pl.ANY)


def _shifts_causal(ext, nk, tm):
    out = []
    for k in range(nk):
        s = nk - 1 - k
        r = ext if s == 0 else pltpu.roll(ext, s, 0)
        out.append(r[HALO:])
    return out


def _shifts_anticausal(ext, nk, tm):
    n = ext.shape[0]
    out = []
    for k in range(nk):
        s = nk - 1 - k
        r = ext if s == 0 else pltpu.roll(ext, n - s, 0)
        out.append(r[:tm])
    return out


def _wsum(w, parts):
    acc = w[0:1, :] * parts[0]
    for k in range(1, len(parts)):
        acc = acc + w[k:k + 1, :] * parts[k]
    return acc


def _colsum(x):
    return jnp.sum(x, axis=0, keepdims=True)


def _acc_out(ref, val, first):
    @pl.when(first)
    def _():
        ref[...] = val

    @pl.when(jnp.logical_not(first))
    def _():
        ref[...] += val


def _acc_rows(ref, rows, first):
    for k, r in enumerate(rows):
        _acc_out(ref.at[k:k + 1, :], r, first)


def _norm_matmul(x, wn, b, name, b_f32=None):
    T, N = x.shape[0], b.shape[1]
    tm = min(1024, T)
    tn = max(t for t in _divisors(N, 2816, (V7X_MXU, LANES))
             if 8 * tm * D + 6 * tm * D + 4 * D * t + 4 * tm * t <= MATMUL_VMEM)

    extra = b_f32 is not None

    def body(*refs):
        x_ref, wn_ref, b_ref = refs[:3]
        o_ref, u_ref = refs[3 + extra:5 + extra]
        keep_ref = refs[-1]

        @pl.when(pl.program_id(1) == 0)
        def _():
            xv = x_ref[...]
            r = lax.rsqrt(jnp.mean(xv * xv, axis=-1, keepdims=True) + EPS)
            u = (xv * r * wn_ref[...]).astype(BF16)
            keep_ref[...] = u
            u_ref[...] = u
            if extra:
                refs[5 + extra][...] = _dot(u, refs[3][...])

        o_ref[...] = _dot(keep_ref[...], b_ref[...]).astype(BF16)

    rows = pl.BlockSpec((tm, D), lambda i, j: (i, 0))
    whole = lambda shape: pl.BlockSpec(shape, lambda i, j: (0, 0))
    narrow = pl.BlockSpec((tm, LANES), lambda i, j: (i, 0))
    return pl.pallas_call(
        body, name=name, grid=(T // tm, N // tn),
        in_specs=[rows, whole((1, D)), pl.BlockSpec((D, tn), lambda i, j: (0, j))] + [whole((D, LANES))] * extra,
        out_specs=[pl.BlockSpec((tm, tn), lambda i, j: (i, j)), rows] + [narrow] * extra,
        out_shape=[jax.ShapeDtypeStruct((T, N), BF16), jax.ShapeDtypeStruct((T, D), BF16)]
        + [jax.ShapeDtypeStruct((T, LANES), F32)] * extra,
        scratch_shapes=[pltpu.VMEM((tm, D), BF16)],
        compiler_params=_params(("parallel", "arbitrary")),
    )(*((x, wn, b) + ((b_f32,) if extra else ())))


def _rmsnorm_bwd_epilogue(x, w, dres):
    T = x.shape[0]

    def fn(dyv, ins, outs, first):
        x_ref, w_ref, dr_ref = ins
        dx_ref, dxb_ref, dw_ref = outs
        xv = x_ref[...]
        r = lax.rsqrt(jnp.mean(xv * xv, axis=-1, keepdims=True) + EPS)
        xh = xv * r
        dxh = dyv * w_ref[...]
        dx = r * (dxh - xh * jnp.mean(dxh * xh, axis=-1, keepdims=True)) + dr_ref[...]
        dx_ref[...] = dx
        dxb_ref[...] = dx.astype(BF16)
        _acc_out(dw_ref, _colsum(dyv * xh), first)

    return _Epilogue(fn, (x, w, dres), (((T, D), F32), ((T, D), BF16), ((1, D), F32)), 14 * D)


def _branch_a_fwd(proj, conv_w):
    T = proj.shape[0]
    R = _Rows(T, 512)
    tm = R.tm

    def body(p_ref, pp_ref, w_ref, o_ref):
        keep = (pl.program_id(1) > 0).astype(F32)
        cv = p_ref[:, D:2 * D].astype(F32) * p_ref[:, 2 * D:].astype(F32)
        cvp = pp_ref[:, D:2 * D].astype(F32) * pp_ref[:, 2 * D:].astype(F32) * keep
        sh = _shifts_causal(jnp.concatenate([cvp, cv], axis=0), 3, tm)
        ca = _wsum(w_ref[...], sh)
        o_ref[...] = (p_ref[:, :D].astype(F32) * ca).astype(BF16)

    return R.call(body, "branch_a_fwd", 1, [R.tile(3 * D), R.prev(3 * D), R.colvec(3, D)], R.tile(D),
                  jax.ShapeDtypeStruct((T, D), BF16), (proj, proj, conv_w))


def _branch_a_bwd(dya_in, proj, conv_w, dproj):
    T = proj.shape[0]
    R = _Rows(T, 256)
    tm = R.tm

    def body(d_ref, dn_ref, p_ref, pp_ref, pn_ref, w_ref, _alias, o_ref, dw_ref):
        i = pl.program_id(1)
        keep_p = (i > 0).astype(F32)
        keep_n = (i < R.nrow - 1).astype(F32)
        w = w_ref[...]
        b = p_ref[:, :D].astype(F32)
        c = p_ref[:, D:2 * D].astype(F32)
        v = p_ref[:, 2 * D:].astype(F32)
        cvp = pp_ref[:, D:2 * D].astype(F32) * pp_ref[:, 2 * D:].astype(F32) * keep_p
        sh = _shifts_causal(jnp.concatenate([cvp, c * v], axis=0), 3, tm)
        ca = _wsum(w, sh)
        d = d_ref[...].astype(F32)
        dca = d * b
        dca_n = dn_ref[...].astype(F32) * pn_ref[:, :D].astype(F32) * keep_n
        dsh = _shifts_anticausal(jnp.concatenate([dca, dca_n], axis=0), 3, tm)
        dcv = _wsum(w, dsh)
        o_ref[:, :D] = (d * ca).astype(BF16)
        o_ref[:, D:2 * D] = (dcv * v).astype(BF16)
        o_ref[:, 2 * D:] = (dcv * c).astype(BF16)
        _acc_rows(dw_ref, [_colsum(dca * s) for s in sh], i == 0)

    return R.call(
        body, "branch_a_bwd", 1,
        [R.tile(D), R.next(D), R.tile(3 * D), R.prev(3 * D), R.next(3 * D), R.colvec(3, D), ANY],
        [R.tile(3 * D), R.colvec(3, D)],
        [jax.ShapeDtypeStruct(dproj.shape, BF16), jax.ShapeDtypeStruct((3, D), F32)],
        (dya_in, dya_in, proj, proj, proj, conv_w, dproj), aliases={6: 0})


_XW = 512


def _xbc_fwd(proj, conv_w, conv_b):
    T = proj.shape[0]
    R = _Rows(T, 512)
    tm = R.tm
    cb = OFF_XBC // _XW

    def body(x_ref, xp_ref, w_ref, b_ref, o_ref):
        keep = (pl.program_id(1) > 0).astype(F32)
        ext = jnp.concatenate([xp_ref[...].astype(F32) * keep, x_ref[...].astype(F32)], axis=0)
        pre = _wsum(w_ref[...], _shifts_causal(ext, 4, tm)) + b_ref[...]
        o_ref[...] = (pre * _sigmoid(pre)).astype(BF16)

    return R.call(body, "xbc_fwd", DX // _XW,
                  [R.tile(_XW, cb), R.prev(_XW, cb), R.colvec(4, _XW), R.colvec(1, _XW)], R.tile(_XW),
                  jax.ShapeDtypeStruct((T, DX), BF16), (proj, proj, conv_w, conv_b))


def _xbc_bwd(dact, proj, conv_w, conv_b, dproj):
    T = proj.shape[0]
    R = _Rows(T, 512)
    tm = R.tm
    cb = OFF_XBC // _XW

    def body(d_ref, dn_ref, x_ref, xp_ref, xn_ref, w_ref, b_ref, _alias, o_ref, dw_ref, db_ref):
        i = pl.program_id(1)
        keep_p = (i > 0).astype(F32)
        keep_n = (i < R.nrow - 1).astype(F32)
        w = w_ref[...]
        ext = jnp.concatenate([xp_ref[...].astype(F32) * keep_p, x_ref[...].astype(F32),
                               xn_ref[...].astype(F32)], axis=0)
        sh = _shifts_causal(ext, 4, tm + HALO)
        pre = _wsum(w, sh) + b_ref[...]
        s = _sigmoid(pre)
        dsilu = s * (1.0 + pre * (1.0 - s))
        dext = jnp.concatenate([d_ref[...].astype(F32), dn_ref[...].astype(F32) * keep_n], axis=0)
        dpre = dext * dsilu
        dsh = _shifts_anticausal(dpre, 4, tm)
        o_ref[...] = _wsum(w, dsh).astype(BF16)
        dp = dpre[:tm]
        _acc_rows(dw_ref, [_colsum(dp * q[:tm]) for q in sh], i == 0)
        _acc_out(db_ref, _colsum(dp), i == 0)

    return R.call(
        body, "xbc_bwd", DX // _XW,
        [R.tile(_XW), R.next(_XW), R.tile(_XW, cb), R.prev(_XW, cb), R.next(_XW, cb),
         R.colvec(4, _XW), R.colvec(1, _XW), ANY],
        [R.tile(_XW, cb), R.colvec(4, _XW), R.colvec(1, _XW)],
        [jax.ShapeDtypeStruct(dproj.shape, BF16), jax.ShapeDtypeStruct((4, DX), F32),
         jax.ShapeDtypeStruct((1, DX), F32)],
        (dact, dact, proj, proj, proj, conv_w, conv_b, dproj), aliases={7: 0})


def _softplus(x):
    return jnp.maximum(x, 0.0) + jnp.log(1.0 + jnp.exp(-jnp.abs(x)))


def _dt_rows(T):
    return min(8 * CH, T // 2)


def _dt_fwd(dt_raw, dt_bias_p, a_log_p):
    T = dt_raw.shape[0]
    rows = _dt_rows(T)

    def body(r_ref, b_ref, al_ref, dt_ref, ac_ref, acT_ref):
        dt = _softplus(r_ref[...] + b_ref[...])
        s = dt * (-jnp.exp(al_ref[...]))
        row = lax.broadcasted_iota(jnp.int32, (rows, LANES), 0) % CH
        k = 1
        while k < CH:
            s = s + jnp.where(row >= k, pltpu.roll(s, k, 0), 0.0)
            k *= 2
        dt_ref[...] = dt
        ac_ref[...] = s
        for q in range(0, rows, CH):
            acT_ref[q:q + CH] = s[q:q + CH].T

    blk = pl.BlockSpec((rows, LANES), lambda i: (i, 0))
    vec = pl.BlockSpec((1, LANES), lambda i: (0, 0))
    return pl.pallas_call(
        body, name="dt_fwd", grid=(T // rows,), in_specs=[blk, vec, vec], out_specs=[blk, blk, blk],
        out_shape=[jax.ShapeDtypeStruct((T, LANES), F32)] * 3, compiler_params=_params(("parallel",)),
    )(dt_raw, dt_bias_p, a_log_p)


def _dt_bwd(dacum, ddt_x, dt_raw, dt_bias_p, a_log_p, dproj):
    T = dt_raw.shape[0]
    rows = _dt_rows(T)
    nc = T // rows

    def body(da_ref, dx_ref, r_ref, b_ref, al_ref, _alias, o_ref, db_ref, dal_ref):
        i = pl.program_id(0)
        a = -jnp.exp(al_ref[...])
        z = r_ref[...] + b_ref[...]
        dt = _softplus(z)
        s = da_ref[...]
        row = lax.broadcasted_iota(jnp.int32, (rows, LANES), 0) % CH
        k = 1
        while k < CH:
            s = s + jnp.where(row < CH - k, pltpu.roll(s, rows - k, 0), 0.0)
            k *= 2
        ddt = s * a + dx_ref[...]
        draw = ddt * _sigmoid(z)
        o_ref[:, :LANES] = draw.astype(BF16)
        o_ref[:, LANES:] = jnp.zeros((rows, NIP - OFF_DT - LANES), BF16)
        _acc_out(db_ref, _colsum(draw), i == 0)
        _acc_out(dal_ref, _colsum(s * dt), i == 0)

        @pl.when(i == nc - 1)
        def _():
            dal_ref[...] = dal_ref[...] * a

    blk = pl.BlockSpec((rows, LANES), lambda i: (i, 0))
    vec = pl.BlockSpec((1, LANES), lambda i: (0, 0))
    oblk = pl.BlockSpec((rows, NIP - OFF_DT), lambda i: (i, OFF_DT // (NIP - OFF_DT)))
    return pl.pallas_call(
        body, name="dt_bwd", grid=(nc,), in_specs=[blk, blk, blk, vec, vec, ANY], out_specs=[oblk, vec, vec],
        out_shape=[jax.ShapeDtypeStruct(dproj.shape, BF16), jax.ShapeDtypeStruct((1, LANES), F32),
                   jax.ShapeDtypeStruct((1, LANES), F32)],
        input_output_aliases={5: 0}, compiler_params=_params(("arbitrary",)),
    )(dacum, ddt_x, dt_raw, dt_bias_p, a_log_p, dproj)


_GW = DI // NG
_HG = NH // NG
_NEG = -1e30


def _interleave(gens):
    out, live = [None] * len(gens), list(range(len(gens)))
    while live:
        for i in list(live):
            try:
                next(gens[i])
            except StopIteration as stop:
                out[i] = stop.value
                live.remove(i)
    return out


def _pair_lanes(left, v0, v1):
    return jnp.where(left, v0, v1)


def _ssd_specs(T, rev):
    nc = T // CH
    cm = (lambda c: nc - 1 - c) if rev else (lambda c: c)
    bw = NG * NS
    return dict(
        xs=pl.BlockSpec((CH, DI), lambda c: (cm(c), 0)),
        bm=pl.BlockSpec((CH, bw), lambda c: (cm(c), DI // bw)),
        cmat=pl.BlockSpec((CH, bw), lambda c: (cm(c), DI // bw + 1)),
        xbc=pl.BlockSpec((CH, DX), lambda c: (cm(c), 0)),
        col=pl.BlockSpec((CH, LANES), lambda c: (cm(c), 0)),
        dsk=pl.BlockSpec((1, DI), lambda c: (0, 0)),
        state=pl.BlockSpec((1, NS, DI), lambda c: (cm(c), 0, 0)),
    )


def _last(ref, lo, hi):
    return ref.at[(slice(None),) * (len(ref.shape) - 1) + (slice(lo, hi),)]


def _group_views(g, wide, narrow):
    return [_last(r, g * _GW, (g + 1) * _GW) for r in wide] + [_last(r, g * NS, (g + 1) * NS) for r in narrow]


def _ssd_fwd(xact, dt, acum, acumT, dsk_rep, proj, norm_w):
    T = xact.shape[0]
    nc = T // CH
    sp = _ssd_specs(T, False)

    def body(*refs):
        xs, bm, cmat, dtr, acr, actr, dsk, zr, nw, y, yn, spv, S_ref = refs

        @pl.when(pl.program_id(0) == 0)
        def _():
            S_ref[...] = jnp.zeros_like(S_ref)

        _interleave([group(g * _HG, dtr[...], acr[...], actr[...],
                           *_group_views(g, (xs, dsk, zr, nw, y, yn, spv, S_ref), (bm, cmat))) for g in range(NG)])

    def group(hb, dt, ac, acT, xs_ref, dsk_ref, z_ref, nw_ref, y_ref, yn_ref, sp_ref, S_ref, b_ref, c_ref):
        Bm, Cm = b_ref[...], c_ref[...]
        S = S_ref[...]
        sp_ref[0] = S
        cb = _dot(Cm, Bm, NT)
        CS = _dot(Cm, S.astype(BF16))
        row = lax.broadcasted_iota(jnp.int32, (CH, CH), 0)
        col = lax.broadcasted_iota(jnp.int32, (CH, CH), 1)
        tril = row >= col
        left = col < HP
        xd_parts, dec_parts = [], []
        for p in range(_HG // 2):
            sl = slice(p * LANES, (p + 1) * LANES)
            j0, j1 = hb + 2 * p, hb + 2 * p + 1
            xp = xs_ref[:, sl].astype(F32)
            a0, a1 = ac[:, j0:j0 + 1], ac[:, j1:j1 + 1]
            al0, al1 = ac[CH - 1:CH, j0:j0 + 1], ac[CH - 1:CH, j1:j1 + 1]
            X = xp * _pair_lanes(left, dt[:, j0:j0 + 1], dt[:, j1:j1 + 1])
            Xb = X.astype(BF16)
            Ws = [(cb * jnp.exp(jnp.where(tril, aj - acT[j:j + 1, :], _NEG))).astype(BF16)
                  for j, aj in ((j0, a0), (j1, a1))]
            Xs = [jnp.where(m, Xb, jnp.zeros_like(Xb)) for m in (left, jnp.logical_not(left))]
            yield
            yd = _dot(jnp.concatenate(Ws, axis=1), jnp.concatenate(Xs, axis=0))
            yield
            eal = _pair_lanes(left, jnp.exp(a0), jnp.exp(a1))
            y = yd + eal * CS[:, sl] + dsk_ref[:, sl] * xp
            y_ref[:, sl] = y.astype(BF16)
            xd_parts.append(X * _pair_lanes(left, jnp.exp(al0 - a0), jnp.exp(al1 - a1)))
            dec_parts.append(_pair_lanes(left[0:1], jnp.exp(al0), jnp.exp(al1)))
        Xd = jnp.concatenate(xd_parts, axis=1).astype(BF16)
        dec = jnp.concatenate(dec_parts, axis=1)
        S_ref[...] = dec * S + _dot(Bm, Xd, TN)
        yield
        z = z_ref[...].astype(F32)
        yf = y_ref[...].astype(F32) * z * _sigmoid(z)
        r = lax.rsqrt(jnp.mean(yf * yf, axis=-1, keepdims=True) + EPS)
        yn_ref[...] = (yf * r * nw_ref[...]).astype(BF16)

    zspec = pl.BlockSpec((CH, DI), lambda c: (c, OFF_Z // DI))
    return pl.pallas_call(
        body, name="ssd_fwd", grid=(nc,),
        in_specs=[sp["xs"], sp["bm"], sp["cmat"], sp["col"], sp["col"], sp["col"], sp["dsk"], zspec, sp["dsk"]],
        out_specs=[sp["xs"], sp["xs"], sp["state"]],
        out_shape=[jax.ShapeDtypeStruct((T, DI), BF16), jax.ShapeDtypeStruct((T, DI), BF16),
                   jax.ShapeDtypeStruct((nc, NS, DI), F32)],
        scratch_shapes=[pltpu.VMEM((NS, DI), F32)],
        compiler_params=_params(("arbitrary",)),
    )(xact, xact, xact, dt, acum, acumT, dsk_rep, proj, norm_w)


def _ssd_bwd(dn, y, proj, norm_w, dproj, xact, dt, acum, acumT, dsk_rep, sprev):
    T = xact.shape[0]
    nc = T // CH
    sp = _ssd_specs(T, True)

    def body(*refs):
        xs, bm, cmat, dtr, acr, actr, dsk, dnr, yr, zr, nw, spv, _alias, dxa, ddtx, dAc, dskacc, dzr, dnw, dS_ref = refs
        first = pl.program_id(0) == 0

        @pl.when(first)
        def _():
            dS_ref[...] = jnp.zeros_like(dS_ref)

        dbc = _last(dxa, DI, DX)
        ddtx_sum = jnp.zeros((CH, LANES), F32)
        dAc_sum = jnp.zeros((CH, LANES), F32)
        for a, b in _interleave([group(first, g * _HG, dtr[...], acr[...], actr[...],
                                       *_group_views(g, (xs, dsk, dnr, yr, zr, nw, dzr, dnw, spv, dxa, dskacc, dS_ref),
                                                     (bm, cmat, dbc, _last(dbc, NG * NS, 2 * NG * NS))))
                                 for g in range(NG)]):
            ddtx_sum, dAc_sum = ddtx_sum + a, dAc_sum + b
        ddtx[...] = ddtx_sum
        dAc[...] = dAc_sum

    def group(first, hb, dt, ac, acT, xs_ref, dsk_ref, dn_ref, y_ref, z_ref, nw_ref, dz_ref, dnw_ref, sp_ref, dx_ref,
              dskacc_ref, dS_ref, b_ref, c_ref, dB_ref, dC_ref):
        z = z_ref[...].astype(F32)
        yv = y_ref[...].astype(F32)
        sg = _sigmoid(z)
        silu = z * sg
        yf = yv * silu
        rn = lax.rsqrt(jnp.mean(yf * yf, axis=-1, keepdims=True) + EPS)
        yh = yf * rn
        dnv = dn_ref[...].astype(F32)
        dyh = dnv * nw_ref[...]
        dyf = rn * (dyh - yh * jnp.mean(dyh * yh, axis=-1, keepdims=True))
        dyg = dyf * silu
        dz_ref[...] = (dyf * yv * sg * (1.0 + z * (1.0 - sg))).astype(BF16)
        _acc_out(dnw_ref, _colsum(dnv * yh), first)
        Bm, Cm = b_ref[...], c_ref[...]
        S = sp_ref[0]
        dS = dS_ref[...]
        Sb, dSb = S.astype(BF16), dS.astype(BF16)
        cb = _dot(Cm, Bm, NT)
        cbT = _dot(Bm, Cm, NT)
        CmT = Cm.T
        CS = _dot(Cm, Sb)
        T1 = _dot(Bm, dSb)
        yield
        row = lax.broadcasted_iota(jnp.int32, (CH, CH), 0)
        col = lax.broadcasted_iota(jnp.int32, (CH, CH), 1)
        tril = row >= col
        triu = row <= col
        left = col < HP
        lane8 = lax.broadcasted_iota(jnp.int32, (1, LANES), 1)
        lastrow = lax.broadcasted_iota(jnp.int32, (CH, 1), 0) == CH - 1
        dCB = jnp.zeros((CH, CH), F32)
        dCBT = jnp.zeros((CH, CH), F32)
        dAc = jnp.zeros((CH, LANES), F32)
        ddtx = jnp.zeros((CH, LANES), F32)
        xd_parts, dye_parts, dec_parts, dsk_parts = [], [], [], []
        for p in range(_HG // 2):
            sl = slice(p * LANES, (p + 1) * LANES)
            j0, j1 = hb + 2 * p, hb + 2 * p + 1
            xp = xs_ref[:, sl].astype(F32)
            dyp = dyg[:, sl]
            a0, a1 = ac[:, j0:j0 + 1], ac[:, j1:j1 + 1]
            al0, al1 = ac[CH - 1:CH, j0:j0 + 1], ac[CH - 1:CH, j1:j1 + 1]
            dtl = _pair_lanes(left, dt[:, j0:j0 + 1], dt[:, j1:j1 + 1])
            X = xp * dtl
            Xb = X.astype(BF16)
            eal = _pair_lanes(left, jnp.exp(a0), jnp.exp(a1))
            dtel = _pair_lanes(left, jnp.exp(al0 - a0), jnp.exp(al1 - a1))
            T1p = T1[:, sl]
            Rm = T1p * dtel * X
            GR = dyp * (eal * CS[:, sl]) - Rm
            SdS = dS[:, sl] * S[:, sl]
            dXd = jnp.zeros((CH, LANES), F32)
            for j, aj, alj, mask in ((j0, a0, al0, left), (j1, a1, al1, jnp.logical_not(left))):
                dYm = jnp.where(mask, dyp, 0.0).astype(BF16)
                dWm = _dot(dYm, Xb, NT)
                dWmT = _dot(Xb, dYm, NT)
                yield
                e = aj - acT[j:j + 1, :]
                P = dWm * jnp.exp(jnp.where(tril, e, _NEG))
                LmT = jnp.exp(jnp.where(triu, -e, _NEG))
                PT = dWmT * LmT
                dCB = dCB + P
                dCBT = dCBT + PT
                yield
                dXd = dXd + _dot((cbT * LmT).astype(BF16), dYm)
                qd = P * cb - PT * cbT + jnp.where(mask, GR, 0.0)
                colv = jnp.sum(qd, axis=1, keepdims=True)
                tot = jnp.where(mask, Rm + jnp.exp(alj) * SdS, 0.0)
                dalast = jnp.sum(jnp.sum(tot, axis=0, keepdims=True), axis=1, keepdims=True)
                dAc = dAc + (colv + jnp.where(lastrow, dalast, 0.0)) * (lane8 == j).astype(F32)
                yield
            dX = dXd + dtel * T1p
            dXx = dX * xp
            for j, mask in ((j0, left), (j1, jnp.logical_not(left))):
                dd = jnp.sum(jnp.where(mask, dXx, 0.0), axis=1, keepdims=True)
                ddtx = ddtx + dd * (lane8 == j).astype(F32)
            dx_ref[:, sl] = (dX * dtl + dsk_ref[:, sl] * dyp).astype(BF16)
            dsk_parts.append(_colsum(dyp * xp))
            xd_parts.append(X * dtel)
            dye_parts.append(dyp * eal)
            dec_parts.append(_pair_lanes(left[0:1], jnp.exp(al0), jnp.exp(al1)))
            yield
        Xd = jnp.concatenate(xd_parts, axis=1).astype(BF16)
        dYe = jnp.concatenate(dye_parts, axis=1).astype(BF16)
        dec = jnp.concatenate(dec_parts, axis=1)
        dC_ref[...] = (_dot(dCB.astype(BF16), Bm) + _dot(dYe, Sb, NT)).astype(BF16)
        dB_ref[...] = (_dot(dCBT.astype(BF16), Cm) + _dot(Xd, dSb, NT)).astype(BF16)
        dS_ref[...] = _dot(CmT, dYe) + dec * dS
        _acc_out(dskacc_ref, jnp.concatenate(dsk_parts, axis=1), first)
        return ddtx, dAc

    zspec = pl.BlockSpec((CH, DI), lambda c: (nc - 1 - c, OFF_Z // DI))
    return pl.pallas_call(
        body, name="ssd_bwd", grid=(nc,),
        in_specs=[sp["xs"], sp["bm"], sp["cmat"], sp["col"], sp["col"], sp["col"], sp["dsk"], sp["xs"], sp["xs"],
                  zspec, sp["dsk"], sp["state"], ANY],
        out_specs=[sp["xbc"], sp["col"], sp["col"], sp["dsk"], zspec, sp["dsk"]],
        out_shape=[jax.ShapeDtypeStruct((T, DX), BF16), jax.ShapeDtypeStruct((T, LANES), F32),
                   jax.ShapeDtypeStruct((T, LANES), F32), jax.ShapeDtypeStruct((1, DI), F32),
                   jax.ShapeDtypeStruct(dproj.shape, BF16), jax.ShapeDtypeStruct((1, DI), F32)],
        scratch_shapes=[pltpu.VMEM((NS, DI), F32)], input_output_aliases={12: 4},
        compiler_params=_params(("arbitrary",)),
    )(xact, xact, xact, dt, acum, acumT, dsk_rep, dn, y, proj, norm_w, sprev, dproj)


def _merge_fwd_epilogue(proj, ya):
    T = proj.shape[0]

    def fn(ysv, ins, outs, first):
        g_ref, ya_ref = ins
        m_ref, ys_ref = outs
        ga = _sigmoid(g_ref[:, :D].astype(F32))
        gs = _sigmoid(g_ref[:, D:].astype(F32))
        m_ref[...] = (ga * ya_ref[...].astype(F32) + gs * ysv).astype(BF16)
        ys_ref[...] = ysv.astype(BF16)

    return _Epilogue(fn, (proj, ya), (((T, D), BF16), ((T, D), BF16)), 10 * D, in_windows={0: (OFF_G, 2 * D)})


def _merge_bwd_epilogue(proj, ya, ys, ncols):
    T = proj.shape[0]

    def fn(d, ins, outs, first):
        g_ref, ya_ref, ys_ref = ins
        dg_ref, dya_ref, dys_ref = outs
        ga = _sigmoid(g_ref[:, :D].astype(F32))
        gs = _sigmoid(g_ref[:, D:].astype(F32))
        dya_ref[...] = (d * ga).astype(BF16)
        dys_ref[...] = (d * gs).astype(BF16)
        dg_ref[:, :D] = (d * ya_ref[...].astype(F32) * ga * (1.0 - ga)).astype(BF16)
        dg_ref[:, D:] = (d * ys_ref[...].astype(F32) * gs * (1.0 - gs)).astype(BF16)

    window = (OFF_G, 2 * D)
    return _Epilogue(fn, (proj, ya, ys), (((T, ncols), BF16), ((T, D), BF16), ((T, D), BF16)), 16 * D,
                     in_windows={0: window}, out_windows={0: window})


_FW = 1408
_FB = FF // _FW


def _ffn_act_fwd(hv, conv_w, conv_b):
    T = hv.shape[0]
    R = _Rows(T, 256)
    tm = R.tm

    def body(h1_ref, h1p_ref, h3_ref, w_ref, b_ref, o_ref):
        keep = (pl.program_id(1) > 0).astype(F32)
        ext = jnp.concatenate([h1p_ref[...].astype(F32) * keep, h1_ref[...].astype(F32)], axis=0)
        pre = _wsum(w_ref[...], _shifts_causal(ext, 3, tm)) + b_ref[...]
        o_ref[...] = (pre * _sigmoid(pre) * h3_ref[...].astype(F32)).astype(BF16)

    return R.call(body, "ffn_act_fwd", _FB,
                  [R.tile(_FW), R.prev(_FW), R.tile(_FW, _FB), R.colvec(3, _FW), R.colvec(1, _FW)],
                  R.tile(_FW), jax.ShapeDtypeStruct((T, FF), BF16), (hv, hv, hv, conv_w, conv_b))


def _ffn_act_bwd(dg, hv, conv_w, conv_b):
    T = hv.shape[0]
    R = _Rows(T, 256)
    tm = R.tm

    def body(dg_ref, dgn_ref, h1_ref, h1p_ref, h1n_ref, h3_ref, h3n_ref, w_ref, b_ref, dh3_ref, dh1_ref, dw_ref,
             db_ref):
        i = pl.program_id(1)
        keep_p = (i > 0).astype(F32)
        keep_n = (i < R.nrow - 1).astype(F32)
        w = w_ref[...]
        ext = jnp.concatenate([h1p_ref[...].astype(F32) * keep_p, h1_ref[...].astype(F32),
                               h1n_ref[...].astype(F32)], axis=0)
        sh = _shifts_causal(ext, 3, tm + HALO)
        pre = _wsum(w, sh) + b_ref[...]
        s = _sigmoid(pre)
        d = jnp.concatenate([dg_ref[...].astype(F32), dgn_ref[...].astype(F32) * keep_n], axis=0)
        h3 = jnp.concatenate([h3_ref[...].astype(F32), h3n_ref[...].astype(F32)], axis=0)
        dh3_ref[...] = (d[:tm] * pre[:tm] * s[:tm]).astype(BF16)
        dpre = d * h3 * s * (1.0 + pre * (1.0 - s))
        dh1_ref[...] = _wsum(w, _shifts_anticausal(dpre, 3, tm)).astype(BF16)
        dp = dpre[:tm]
        _acc_rows(dw_ref, [_colsum(dp * q[:tm]) for q in sh], i == 0)
        _acc_out(db_ref, _colsum(dp), i == 0)

    return R.call(
        body, "ffn_act_bwd", _FB,
        [R.tile(_FW), R.next(_FW), R.tile(_FW), R.prev(_FW), R.next(_FW), R.tile(_FW, _FB), R.next(_FW, _FB),
         R.colvec(3, _FW), R.colvec(1, _FW)],
        [R.tile(_FW), R.tile(_FW), R.colvec(3, _FW), R.colvec(1, _FW)],
        [jax.ShapeDtypeStruct((T, FF), BF16), jax.ShapeDtypeStruct((T, FF), BF16),
         jax.ShapeDtypeStruct((3, FF), F32), jax.ShapeDtypeStruct((1, FF), F32)],
        (dg, dg, hv, hv, hv, hv, hv, conv_w, conv_b))


def _final_loss_epilogue(w, target):
    T = target.shape[0]

    def fn(xv, ins, outs, first):
        w_ref, t_ref = ins
        l_ref, dh_ref, dhb_ref, dw_ref = outs
        wv = w_ref[...]
        r = lax.rsqrt(jnp.mean(xv * xv, axis=-1, keepdims=True) + EPS)
        xh = xv * r
        err = xh * wv - t_ref[...]
        part = 0.5 * jnp.sum(jnp.mean(err * err, axis=-1, keepdims=True), axis=0, keepdims=True)
        _acc_out(l_ref, jnp.broadcast_to(part, l_ref.shape), first)
        dy = err * (1.0 / D)
        dxh = dy * wv
        dh = r * (dxh - xh * jnp.mean(dxh * xh, axis=-1, keepdims=True))
        dh_ref[...] = dh
        dhb_ref[...] = dh.astype(BF16)
        _acc_out(dw_ref, _colsum(dy * xh), first)

    return _Epilogue(fn, (w, target),
                     (((8, LANES), F32), ((T, D), F32), ((T, D), BF16), ((1, D), F32)), 10 * D)


def _pad_lanes(v, n=LANES):
    return jnp.pad(v, ((0, 0), (0, n - v.shape[1])))


class _Hooks:
    def before_in_proj(self, w_in):
        return w_in

    def late_weights(self, wts, after):
        return wts

    def grads_ready(self, grads, tie):
        return tie

    def mark(self, name, value):
        pass


def _local_step(x, target, wts, hooks=None):
    hooks = hooks or _Hooks()
    T = x.shape[0]
    w_in = wts["w_in"]
    dt_bias_p, a_log_p = _pad_lanes(wts["dt_bias"]), _pad_lanes(wts["a_log"])
    dsk_rep = jnp.repeat(wts["d_skip"], HP, axis=1)

    w_in = hooks.before_in_proj(w_in)
    proj, u, dt_raw = _norm_matmul(x, wts["norm_mix_w"], w_in, "norm_mm_in", w_in[:, OFF_DT:OFF_DT + LANES])
    ya_in = _branch_a_fwd(proj, wts["conv_a_w"])
    xact = _xbc_fwd(proj, wts["ssd_conv_w"], wts["ssd_conv_b"])
    dt, acum, acumT = _dt_fwd(dt_raw, dt_bias_p, a_log_p)
    y_ssd, yn, sprev = _ssd_fwd(xact, dt, acum, acumT, dsk_rep, proj, wts["ssd_norm_w"])
    late = hooks.late_weights(wts, yn)
    w_a_out, w_s_out, w_o, w_up, w_down = (late[k] for k in ("w_a_out", "w_s_out", "w_o", "w_up", "w_down"))
    y_a = _matmul(ya_in, w_a_out, mode="nn", out_dtype=BF16, name="mm_a_out")
    merged, y_s = _matmul(yn, w_s_out, mode="nn", out_dtype=BF16, name="mm_s_out_merge",
                          epilogue=_merge_fwd_epilogue(proj, y_a))
    h1 = _matmul(merged, w_o, mode="nn", out_dtype=F32, name="mm_o", residual=x)
    hv, v = _norm_matmul(h1, wts["norm_ffn_w"], w_up, "norm_mm_up")
    gact = _ffn_act_fwd(hv, wts["ffn_conv_w"], wts["ffn_conv_b"])
    loss, dh2, dh2b, g_final = _matmul(gact, w_down, mode="nn", out_dtype=F32, name="mm_down_loss", residual=h1,
                                       epilogue=_final_loss_epilogue(wts["final_norm_w"], target))

    grads = {"final_norm_w": g_final}
    grads["w_down"] = _matmul(gact, dh2b, mode="tn", out_dtype=F32, name="mm_down_dw")
    dgact = _matmul(dh2b, w_down, mode="nt", out_dtype=BF16, name="mm_down_dx")
    dh3, dh1c, grads["ffn_conv_w"], grads["ffn_conv_b"] = _ffn_act_bwd(dgact, hv, wts["ffn_conv_w"], wts["ffn_conv_b"])
    grads["w_up"] = (_matmul(v, dh1c, mode="tn", out_dtype=F32, name="mm_up_dw1"),
                     _matmul(v, dh3, mode="tn", out_dtype=F32, name="mm_up_dw3"))
    dv = _matmul(dh1c, w_up, mode="nt", out_dtype=F32, name="mm_up_dx1")
    dh1, dh1b, grads["norm_ffn_w"] = _matmul(
        dh3, w_up, mode="nt", out_dtype=F32, name="mm_up_dx3_norm", residual=dv, b_k_off=FF,
        epilogue=_rmsnorm_bwd_epilogue(h1, wts["norm_ffn_w"], dh2))
    grads["w_o"] = _matmul(merged, dh1b, mode="tn", out_dtype=F32, name="mm_o_dw")
    dproj, dya, dys = _matmul(dh1b, w_o, mode="nt", out_dtype=BF16, name="mm_o_dx_merge",
                              epilogue=_merge_bwd_epilogue(proj, y_a, y_s, NIP))
    grads["w_a_out"] = _matmul(ya_in, dya, mode="tn", out_dtype=F32, name="mm_a_out_dw")
    dya_in = _matmul(dya, w_a_out, mode="nt", out_dtype=BF16, name="mm_a_out_dx")
    dproj, grads["conv_a_w"] = _branch_a_bwd(dya_in, proj, wts["conv_a_w"], dproj)
    grads["w_s_out"] = _matmul(yn, dys, mode="tn", out_dtype=F32, name="mm_s_out_dw")
    dys = hooks.grads_ready({k: grads[k] for k in ("w_a_out", "w_s_out", "w_o", "w_up", "w_down")}, dys)
    dyn =_matmul(dys, w_s_out, mode="nt", out_dtype=BF16, name="mm_s_out_dx")
    dxact, ddt_x, dacum, dskl, dproj, grads["ssd_norm_w"] = _ssd_bwd(
        dyn, y_ssd, proj, wts["ssd_norm_w"], dproj, xact, dt, acum, acumT, dsk_rep, sprev)
    hooks.mark("ssd_bwd", dxact)
    grads["d_skip"] = dskl.reshape(NH, HP).sum(axis=1).reshape(1, NH)
    dproj, grads["ssd_conv_w"], grads["ssd_conv_b"] = _xbc_bwd(dxact, proj, wts["ssd_conv_w"], wts["ssd_conv_b"], dproj)
    dproj, g_dtb, g_alog = _dt_bwd(dacum, ddt_x, dt_raw, dt_bias_p, a_log_p, dproj)
    grads["dt_bias"], grads["a_log"] = g_dtb[:, :NH], g_alog[:, :NH]
    grads["w_in"] = _matmul(u, dproj, mode="tn", out_dtype=F32, name="mm_in_dw")
    dproj = hooks.grads_ready({"w_in": grads["w_in"]}, dproj)
    grad_x, _, grads["norm_mix_w"] = _matmul(dproj, w_in, mode="nt", out_dtype=F32, name="mm_in_dx_norm",
                                             epilogue=_rmsnorm_bwd_epilogue(x, wts["norm_mix_w"], dh1))
    return loss, grad_x, grads


def _permute_w_in(slabs):
    cs = slabs.shape[2]
    pieces = []
    for o, n, no in sorted(_SEGS, key=lambda seg: seg[2]):
        for s in range(slabs.shape[0]):
            lo, hi = max(o, s * cs), min(o + n, (s + 1) * cs)
            if lo < hi:
                pieces.append(slabs[s][:, lo - s * cs:hi - s * cs])
    pieces.append(jnp.zeros((slabs.shape[1], NIP - OFF_DT - _SEGS[-1][1]), slabs.dtype))
    return jnp.concatenate(pieces, axis=1)


def _unpermute_w_in(g):
    cs = NI // NCHIP
    slabs = []
    for s in range(NCHIP):
        pieces = []
        for o, n, no in sorted(_SEGS):
            lo, hi = max(o, s * cs), min(o + n, (s + 1) * cs)
            if lo < hi:
                pieces.append(g[:, no + lo - o:no + hi - o])
        slabs.append(jnp.concatenate(pieces, axis=1))
    return jnp.stack(slabs)


MESH = pl.DeviceIdType.MESH
NCHIP = 4
NDEV = 8

_W_IN = (("w_in", D, NI // NCHIP, 1),)
_W_REST = (("w_a_out", D // NCHIP, D, 0), ("w_s_out", DI // NCHIP, D, 0), ("w_o", D // NCHIP, D, 0),
           ("w_up", D, 2 * FF // NCHIP, 1), ("w_down", FF // NCHIP, D, 0))


def _coords():
    return lax.axis_index("x"), lax.axis_index("y"), lax.axis_index("c")


def _other_chips(x, y):
    return [(1 - x, y), (x, 1 - y), (1 - x, 1 - y)]


def _ag_weights(shard):
    nrows = shard.shape[0]
    hr = nrows // 2

    def body(x_ref, out_ref, send_sems, recv_sems, local_sem):
        x, y, c = _coords()
        me = 2 * x + y
        chips = _other_chips(x, y)

        def rows(s, h):
            return out_ref.at[s, pl.ds(h * hr, hr), :]

        def copy(k, s, h, to, src=None):
            return pltpu.make_async_remote_copy(
                src_ref=rows(s, h) if src is None else src, dst_ref=rows(s, h),
                send_sem=send_sems.at[k], recv_sem=recv_sems.at[k], device_id=to, device_id_type=MESH)

        mine = pltpu.make_async_copy(x_ref, out_ref.at[me], local_sem)
        mine.start()
        first = [copy(k, me, c, (*chip, c), src=x_ref.at[pl.ds(c * hr, hr), :]) for k, chip in enumerate(chips)]
        for cp in first:
            cp.start()
        passed = []
        for k, chip in enumerate(chips):
            s = 2 * chip[0] + chip[1]
            copy(k, s, c, (x, y, c)).wait_recv()
            fwd = copy(3 + k, s, c, (x, y, 1 - c))
            fwd.start()
            passed.append(fwd)
        for k, chip in enumerate(chips):
            copy(3 + k, 2 * chip[0] + chip[1], 1 - c, (x, y, c)).wait_recv()
        for cp in first + passed:
            cp.wait_send()
        mine.wait()

    return pl.pallas_call(
        body, name="ag_weights", in_specs=[ANY], out_specs=ANY,
        out_shape=jax.ShapeDtypeStruct((NCHIP,) + shard.shape, shard.dtype),
        scratch_shapes=[pltpu.SemaphoreType.DMA((6,)), pltpu.SemaphoreType.DMA((6,)), pltpu.SemaphoreType.DMA],
        compiler_params=pltpu.CompilerParams(has_side_effects=True),
    )(shard)


HBM = pl.BlockSpec(memory_space=pltpu.HBM)
SEM = pl.BlockSpec(memory_space=pltpu.SEMAPHORE)
_EFFECT = pltpu.SideEffectType.DATAFLOW_SIDE_EFFECTING
_NCOPY = NCHIP - 1


def _plan_bcast(src_ref, land_ref, send_sems, recv_sems, base):
    x, y, c = _coords()
    sends, lands = [], []
    for k, chip in enumerate(_other_chips(x, y)):
        def copy(slot):
            return pltpu.make_async_remote_copy(
                src_ref=src_ref, dst_ref=land_ref.at[slot], send_sem=send_sems.at[base + k],
                recv_sem=recv_sems.at[base + k], device_id=(*chip, c), device_id_type=MESH)
        sends.append(copy(2 * x + y))
        lands.append(copy(2 * chip[0] + chip[1]))
    return sends, lands


def _plan_scatter(src_ref, land_ref, send_sems, recv_sems, base):
    x, y, c = _coords()
    cps = [pltpu.make_async_remote_copy(
        src_ref=src_ref.at[2 * chip[0] + chip[1]], dst_ref=land_ref.at[k], send_sem=send_sems.at[base + k],
        recv_sem=recv_sems.at[base + k], device_id=(*chip, c), device_id_type=MESH)
        for k, chip in enumerate(_other_chips(x, y))]
    return cps, cps


def _plan_all(plan, refs, n):
    sends, lands = [], []
    for t in range(n):
        s, l = plan(refs[t], refs[n + t], refs[2 * n], refs[2 * n + 1], t * _NCOPY)
        sends += s
        lands += l
    return sends, lands


def _split_start(name, srcs, lands, plan):
    n = len(srcs)

    def body(*refs):
        for cp in _plan_all(plan, refs, n)[0]:
            cp.start()
        refs[-1][...] = jnp.zeros_like(refs[-1])

    arrays = list(srcs) + list(lands)
    outs = pl.pallas_call(
        body, name=name,
        out_shape=(pltpu.SemaphoreType.DMA((n * _NCOPY,)), pltpu.SemaphoreType.DMA((n * _NCOPY,)),
                   *[pltpu.HBM(a.shape, a.dtype) for a in arrays], jax.ShapeDtypeStruct((8, LANES), F32)),
        in_specs=(HBM,) * (2 * n),
        out_specs=(SEM, SEM) + (HBM,) * (2 * n) + (pl.BlockSpec(memory_space=pltpu.VMEM),),
        input_output_aliases={t: 2 + t for t in range(2 * n)},
        compiler_params=pltpu.CompilerParams(has_side_effects=_EFFECT),
    )(*[pltpu.with_memory_space_constraint(a, pltpu.HBM) for a in arrays])
    return (outs[0], outs[1], tuple(outs[2:2 + 2 * n])), outs[-1]


def _split_wait(name, handle, after, plan):
    send_sems, recv_sems, arrays = handle
    n = len(arrays) // 2

    def body(*refs):
        sends, lands = _plan_all(plan, refs[:2 * n] + refs[2 * n:2 * n + 2], n)
        for cp in sends:
            cp.wait_send()
        for cp in lands:
            cp.wait_recv()

    outs = pl.pallas_call(
        body, name=name, out_shape=tuple(pltpu.HBM(a.shape, a.dtype) for a in arrays),
        in_specs=(HBM,) * (2 * n) + (SEM, SEM, ANY), out_specs=(HBM,) * (2 * n),
        input_output_aliases={t: t for t in range(2 * n)},
        compiler_params=pltpu.CompilerParams(has_side_effects=_EFFECT),
    )(*arrays, send_sems, recv_sems, after)
    return outs[:n], outs[n:]


def _tie(x, token, name):
    def body(x_ref, t_ref, o_ref):
        pass

    return pl.pallas_call(
        body, name=name, in_specs=[ANY, pl.BlockSpec(memory_space=pltpu.VMEM)], out_specs=ANY,
        out_shape=jax.ShapeDtypeStruct(x.shape, x.dtype), input_output_aliases={0: 0},
    )(x, token)


def _swap_sibling(ps, name):
    n = len(ps)

    def body(*refs):
        x, y, c = _coords()
        cps = [pltpu.make_async_remote_copy(
            src_ref=refs[t], dst_ref=refs[n + t], send_sem=refs[2 * n].at[t], recv_sem=refs[2 * n + 1].at[t],
            device_id=(x, y, 1 - c), device_id_type=MESH) for t in range(n)]
        for cp in cps:
            cp.start()
        for cp in cps:
            cp.wait()

    return pl.pallas_call(
        body, name=name, in_specs=[ANY] * n, out_specs=[ANY] * n,
        out_shape=[jax.ShapeDtypeStruct(p.shape, p.dtype) for p in ps],
        scratch_shapes=[pltpu.SemaphoreType.DMA((n,)), pltpu.SemaphoreType.DMA((n,))],
        compiler_params=pltpu.CompilerParams(has_side_effects=True),
    )(*ps)


_ADD_BYTES = 7 << 19


def _add_tile(rows, cols):
    best = 32
    for t in range(32, rows + 1, 32):
        if rows % t == 0 and t * cols * 4 <= _ADD_BYTES:
            best = t
    return best


def _add_slabs(pack, land, me, name):
    rows, cols = pack.shape[1:]
    tr = _add_tile(rows, cols)

    def body(me_ref, p_ref, l_ref, o_ref):
        f = lambda r: r.astype(F32)
        o_ref[...] = ((f(p_ref[0]) + f(l_ref[0])) + f(l_ref[1])) + f(l_ref[2])

    return pl.pallas_call(
        body, name=name,
        grid_spec=pltpu.PrefetchScalarGridSpec(
            num_scalar_prefetch=1, grid=(rows // tr,),
            in_specs=[pl.BlockSpec((1, tr, cols), lambda i, me_ref: (me_ref[0], i, 0)),
                      pl.BlockSpec((_NCOPY, tr, cols), lambda i, me_ref: (0, i, 0))],
            out_specs=pl.BlockSpec((tr, cols), lambda i, me_ref: (i, 0))),
        out_shape=jax.ShapeDtypeStruct((rows, cols), F32),
        compiler_params=_params(("parallel",)),
    )(me, pack, land)


_STAGE_W = 1024


def _stage_rows(shapes):
    pieces, r = [], 0
    for i, (k, w) in enumerate(shapes):
        for a in range(k):
            for q in range(0, w, _STAGE_W):
                pieces.append((i, a, q, min(_STAGE_W, w - q), r))
                r += 1
    return pieces, -(-r // 8) * 8


def _gather8(parts, reduce, name):
    shapes = [p.shape for p in parts]
    pieces, rows = _stage_rows(shapes)
    n = len(parts)

    def body(*refs):
        ins, outs = refs[:n], refs[n:2 * n]
        stage, buf, res, send_sems, recv_sems = refs[2 * n:]
        x, y, c = _coords()
        me = 4 * x + 2 * y + c
        stage[...] = jnp.zeros_like(stage)
        for i, a, q, w, r in pieces:
            stage[r:r + 1, 0:w] = ins[i][a:a + 1, q:q + w]
        buf[pl.ds(me, 1)] = stage[...][None]
        cps, lands = [], []
        for k in range(1, NDEV):
            peer = (1 - x if k & 4 else x, 1 - y if k & 2 else y, 1 - c if k & 1 else c)

            def copy(slot):
                return pltpu.make_async_remote_copy(
                    src_ref=stage, dst_ref=buf.at[slot], send_sem=send_sems.at[k - 1],
                    recv_sem=recv_sems.at[k - 1], device_id=peer, device_id_type=MESH)

            cps.append(copy(me))
            lands.append(copy(4 * peer[0] + 2 * peer[1] + peer[2]))
        for cp in cps:
            cp.start()
        for cp, land in zip(cps, lands):
            land.wait_recv()
            cp.wait_send()
        if reduce:
            acc = buf[0]
            for d in range(1, NDEV):
                acc = acc + buf[d]
            res[...] = acc
            for i, a, q, w, r in pieces:
                outs[i][a:a + 1, q:q + w] = res[r:r + 1, 0:w]
        else:
            for i, a, q, w, r in pieces:
                for s in range(NCHIP):
                    outs[i][s, a:a + 1, q:q + w] = buf[2 * s, r:r + 1, 0:w]

    vm = pl.BlockSpec(memory_space=pltpu.VMEM)
    out_shapes = [jax.ShapeDtypeStruct(s if reduce else (NCHIP,) + s, F32) for s in shapes]
    return pl.pallas_call(
        body, name=name, in_specs=[vm] * n, out_specs=[vm] * n, out_shape=out_shapes,
        scratch_shapes=[pltpu.VMEM((rows, _STAGE_W), F32), pltpu.VMEM((NDEV, rows, _STAGE_W), F32),
                        pltpu.VMEM((rows, _STAGE_W), F32), pltpu.SemaphoreType.DMA((NDEV - 1,)),
                        pltpu.SemaphoreType.DMA((NDEV - 1,))],
        compiler_params=pltpu.CompilerParams(has_side_effects=True),
    )(*parts)


def _adamw_update(w_ref, g_ref, m_ref, v_ref, d_ref, mo_ref, vo_ref):
    c1 = 1.0 / (1.0 - ADAM_B1 ** ADAM_STEP)
    c2 = 1.0 / (1.0 - ADAM_B2 ** ADAM_STEP)
    gv = g_ref[...]
    mn = ADAM_B1 * m_ref[...] + (1.0 - ADAM_B1) * gv
    vn = ADAM_B2 * v_ref[...] + (1.0 - ADAM_B2) * (gv * gv)
    d_ref[...] = -ADAM_LR * ((mn * c1) / (jnp.sqrt(vn * c2) + ADAM_EPS) + ADAM_WD * w_ref[...])
    mo_ref[...] = mn
    vo_ref[...] = vn


def _adamw_small(ws, gs, ms, vs):
    n = len(ws)

    def body(*refs):
        for i in range(n):
            _adamw_update(*(refs[j * n + i] for j in range(7)))

    vm = pl.BlockSpec(memory_space=pltpu.VMEM)
    outs = pl.pallas_call(
        body, name="adamw_small", in_specs=[vm] * (4 * n), out_specs=[vm] * (3 * n),
        out_shape=[jax.ShapeDtypeStruct(w.shape, F32) for w in ws] * 3,
    )(*ws, *gs, *ms, *vs)
    return outs[:n], outs[n:2 * n], outs[2 * n:]


def _adamw(w, g_parts, m, v, name):
    rows, cols = w.shape
    tr = rows
    while tr * cols * 4 > (1 << 20) and tr % 16 == 0:
        tr //= 2

    def body(w_ref, ga_ref, gb_ref, m_ref, v_ref, g_ref, d_ref, mo_ref, vo_ref):
        g_ref[...] = ga_ref[...] + gb_ref[...]
        _adamw_update(w_ref, g_ref, m_ref, v_ref, d_ref, mo_ref, vo_ref)

    blk = pl.BlockSpec((tr, cols), lambda i: (i, 0))
    return pl.pallas_call(
        body, name=name, grid=(rows // tr,), in_specs=[blk] * 5, out_specs=[blk] * 4,
        out_shape=[jax.ShapeDtypeStruct((rows, cols), F32)] * 4, compiler_params=_params(("parallel",)),
    )(w, *g_parts, m, v)


def _by_chip(g, rr, cc, axis):
    if isinstance(g, tuple):
        n = NCHIP // len(g)
        return jnp.concatenate([h.reshape(rr, n, cc).transpose(1, 0, 2) for h in g], axis=0)
    return g.reshape(NCHIP, rr, cc) if axis == 0 else g.reshape(rr, NCHIP, cc).transpose(1, 0, 2)


_SMALL_REPL = ("norm_mix_w", "ssd_conv_b", "dt_bias", "a_log", "d_skip", "ssd_norm_w", "norm_ffn_w",
               "ffn_conv_b", "final_norm_w")
_SMALL_CONV = (("conv_a_w", 3, D), ("ssd_conv_w", 4, DX), ("ffn_conv_w", 3, FF))


def kernel(x, norm_mix_w, w_in, conv_a_w, w_a_out, ssd_conv_w, ssd_conv_b, dt_bias, a_log, d_skip, ssd_norm_w, w_s_out, w_o, norm_ffn_w, w_up, ffn_conv_w, ffn_conv_b, w_down, final_norm_w, loss_target, m_norm_mix_w, m_w_in, m_conv_a_w, m_w_a_out, m_ssd_conv_w, m_ssd_conv_b, m_dt_bias, m_a_log, m_d_skip, m_ssd_norm_w, m_w_s_out, m_w_o, m_norm_ffn_w, m_w_up, m_ffn_conv_w, m_ffn_conv_b, m_w_down, m_final_norm_w, v_norm_mix_w, v_w_in, v_conv_a_w, v_w_a_out, v_ssd_conv_w, v_ssd_conv_b, v_dt_bias, v_a_log, v_d_skip, v_ssd_norm_w, v_w_s_out, v_w_o, v_norm_ffn_w, v_w_up, v_ffn_conv_w, v_ffn_conv_b, v_w_down, v_final_norm_w):
    names = ("norm_mix_w", "w_in", "conv_a_w", "w_a_out", "ssd_conv_w", "ssd_conv_b", "dt_bias", "a_log", "d_skip",
             "ssd_norm_w", "w_s_out", "w_o", "norm_ffn_w", "w_up", "ffn_conv_w", "ffn_conv_b", "w_down", "final_norm_w")
    W = dict(zip(names, (norm_mix_w, w_in, conv_a_w, w_a_out, ssd_conv_w, ssd_conv_b, dt_bias, a_log, d_skip,
                         ssd_norm_w, w_s_out, w_o, norm_ffn_w, w_up, ffn_conv_w, ffn_conv_b, w_down, final_norm_w)))
    M = dict(zip(names, (m_norm_mix_w, m_w_in, m_conv_a_w, m_w_a_out, m_ssd_conv_w, m_ssd_conv_b, m_dt_bias, m_a_log,
                         m_d_skip, m_ssd_norm_w, m_w_s_out, m_w_o, m_norm_ffn_w, m_w_up, m_ffn_conv_w, m_ffn_conv_b,
                         m_w_down, m_final_norm_w)))
    V = dict(zip(names, (v_norm_mix_w, v_w_in, v_conv_a_w, v_w_a_out, v_ssd_conv_w, v_ssd_conv_b, v_dt_bias, v_a_log,
                         v_d_skip, v_ssd_norm_w, v_w_s_out, v_w_o, v_norm_ffn_w, v_w_up, v_ffn_conv_w, v_ffn_conv_b,
                         v_w_down, v_final_norm_w)))
    two_d = lambda a: a.reshape(-1, a.shape[-1])
    W2, M2, V2 = ({k: two_d(a) for k, a in t.items()} for t in (W, M, V))
    xi, yi, ci = _coords()
    me = 2 * xi + yi

    meidx = me.reshape(1).astype(jnp.int32)
    state = {}


    class Hooks(_Hooks):
        def before_in_proj(self, w_in):
            return _tie(w_in, state["rest_token"], "tie_ag_rest")

        def late_weights(self, wts, after):
            owns, lands = _split_wait("ag_rest_wait", state["rest"], after, _plan_bcast)
            full = {}
            for (n, rr, cc, axis), own, land in zip(_W_REST, owns, lands):
                slabs = lax.dynamic_update_slice(land, own[None], (me, 0, 0))
                full[n] = slabs.reshape(NCHIP * rr, cc) if axis == 0 else slabs.transpose(1, 0, 2).reshape(rr, NCHIP * cc)
            return {**wts, **full}

        def grads_ready(self, grads, tie):
            if "w_in" in grads:
                key, packs = "g_in", [_unpermute_w_in(grads["w_in"]).astype(BF16)]
            else:
                key = "g_rest"
                packs = [_by_chip(jax.tree.map(lambda t: t.astype(BF16), grads[n]), rr, cc, axis)
                         for n, rr, cc, axis in _W_REST]
            lands = [lax.empty((_NCOPY,) + p.shape[1:], BF16) for p in packs]
            state[key], token = _split_start("rs_" + key + "_start", packs, lands, _plan_scatter)
            return _tie(tie, token, "tie_" + key)

        def mark(self, name, value):
            state[name] = value

    def reduced(key, after, group):
        packs, lands = _split_wait("rs_" + key + "_wait", state[key], after, _plan_scatter)
        mines = [_add_slabs(p, l, meidx, "rs_add_chips_" + n) for (n, *_), p, l in zip(group, packs, lands)]
        return dict(zip([n for n, *_ in group], zip(mines, _swap_sibling(mines, "rs_" + key + "_swap"))))

    w_in_slabs = _ag_weights(W2["w_in"].astype(BF16))
    wts = {k: W2[k] for k in _SMALL_REPL}
    conv_by_chip = _gather8([W2[n] for n, *_ in _SMALL_CONV], False, "ag_conv_weights")
    for (n, kk, width), stacked in zip(_SMALL_CONV, conv_by_chip):
        wts[n] = stacked.transpose(1, 0, 2).reshape(kk, width)
    rest = [W2[n].astype(BF16) for n, *_ in _W_REST]
    rest[0] = _tie(rest[0], conv_by_chip[0], "tie_ag_order")
    state["rest"], state["rest_token"] = _split_start(
        "ag_rest_start", rest, [lax.empty((NCHIP,) + r.shape, BF16) for r in rest], _plan_bcast)
    wts["w_in"] = _permute_w_in(w_in_slabs)

    loss8, grad_x, grads = _local_step(x[0], loss_target[0], wts, Hooks())

    gbig = {**reduced("g_rest", state["ssd_bwd"], _W_REST), **reduced("g_in", grad_x, _W_IN)}

    small_parts = [grads[n] for n in _SMALL_REPL] + [loss8[0:1]] + [grads[n] for n, *_ in _SMALL_CONV]
    small_g = _gather8(small_parts, True, "allreduce_small")
    gsm = dict(zip(_SMALL_REPL, small_g[:len(_SMALL_REPL)]))
    loss = small_g[len(_SMALL_REPL)][0, 0]
    for (n, kk, width), gfull in zip(_SMALL_CONV, small_g[len(_SMALL_REPL) + 1:]):
        cw = width // NCHIP
        gsm[n] = lax.dynamic_slice(gfull, (0, me * cw), (kk, cw))

    G, DW, NM, NV = {}, {}, {}, {}
    for n in [b[0] for b in _W_IN + _W_REST]:
        G[n], DW[n], NM[n], NV[n] = _adamw(W2[n], gbig[n], M2[n], V2[n], "adamw_" + n)
    sm_names = list(_SMALL_REPL) + [n for n, *_ in _SMALL_CONV]
    outs = _adamw_small(*([t[n] for n in sm_names] for t in (W2, gsm, M2, V2)))
    for t, vals in zip((DW, NM, NV), outs):
        t.update(zip(sm_names, vals))
    G.update(gsm)

    def shaped(t):
        return [t[n].reshape(W[n].shape) for n in names]

    return (loss, grad_x.reshape(x.shape), *shaped(G), *shaped(DW), *shaped(NM), *shaped(NV))
```

```python
import jax
import jax.numpy as jnp
from jax import lax
from jax.experimental import pallas as pl
from jax.experimental.pallas import tpu as pltpu

F32 = jnp.float32
BF16 = jnp.bfloat16

D = 1024
DI = 2048
NH = 32
HP = 64
NG = 4
NS = 128
CH = 128
DX = 3072
FF = 2816
NI = 10272
EPS = 1e-5

OFF_BCV, OFF_XBC, OFF_G, OFF_Z, OFF_DT = 0, 3072, 6144, 8192, 10240
NIP = 10752
_SEGS = ((0, 2048, OFF_G), (2048, 3072, OFF_BCV), (5120, 2048, OFF_Z), (7168, 3072, OFF_XBC), (10240, 32, OFF_DT))

LANES = 128
HALO = 16
V7X_VMEM_LIMIT = 56 * 2 ** 20

ADAM_LR, ADAM_B1, ADAM_B2, ADAM_EPS, ADAM_WD, ADAM_STEP = 0.001, 0.9, 0.999, 1e-08, 0.01, 10

NN = (((1,), (0,)), ((), ()))
NT = (((1,), (1,)), ((), ()))
TN = (((0,), (0,)), ((), ()))


def _dot(a, b, dims=NN):
    return lax.dot_general(a, b, dims, preferred_element_type=F32)


def _params(sem, **kw):
    return pltpu.CompilerParams(dimension_semantics=sem, vmem_limit_bytes=V7X_VMEM_LIMIT, **kw)


V7X_MXU = 256
V7X_HBM_BYTES_PER_S = 3.5e12
STEP_S = 0.35e-6
MATMUL_VMEM = 40 * 2 ** 20
EPILOGUE_VMEM = 46 * 2 ** 20


ACC_BYTES_PER_S = 1.2e13


def _divisors(dim, cap, units):
    for unit in units:
        c = [t for t in range(unit, min(dim, cap) + 1, unit) if dim % t == 0]
        if c:
            return c
    return [dim]


def _tiles(M, N, K, out_bytes, has_res):
    best = None
    for tn in _divisors(N, 2816, (V7X_MXU, LANES)):
        for tm in _divisors(M, 2816, (LANES,)):
            for tk in _divisors(K, 2816, (V7X_MXU, LANES)):
                nk, ni, nj = K // tk, M // tm, N // tn
                vmem = 4 * (tm * tk + tk * tn) + 2 * tm * tn * out_bytes
                vmem += (4 * tm * tn if nk > 1 else 0) + (8 * tm * tn if has_res else 0)
                if vmem > MATMUL_VMEM:
                    continue
                a_reads = M * K * 2 * (nj if nk > 1 else 1)
                b_reads = K * N * 2 * (ni if nk * nj > 1 else 1)
                cost = (a_reads + b_reads + M * N * out_bytes) / V7X_HBM_BYTES_PER_S + ni * nj * nk * STEP_S
                cost += (nk - 1) * M * N * 8 / ACC_BYTES_PER_S
                if best is None or cost < best[0]:
                    best = (cost, tm, tn, tk)
    assert best is not None, (M, N, K)
    return best[1:]


def _sigmoid(x):
    return 1.0 / (1.0 + jnp.exp(-x))


class _Epilogue:
    def __init__(self, fn, ins, outs, tile_bytes, in_windows=None, out_windows=None):
        self.fn, self.ins, self.outs, self.tile_bytes = fn, tuple(ins), tuple(outs), tile_bytes
        self.in_windows, self.out_windows = in_windows or {}, out_windows or {}


def _matmul(a, b, *, mode, out_dtype, name, residual=None, b_k_off=0, epilogue=None):
    if mode == "nn":
        (M, K), (K2, N) = a.shape, b.shape
    elif mode == "nt":
        (M, K), (N, K2) = a.shape, (b.shape[0], a.shape[1])
        assert b_k_off + K <= b.shape[1]
    else:
        (K, M), (K2, N) = a.shape, b.shape
    assert K == K2, (name, a.shape, b.shape)
    tm, tn, tk = _tiles(M, N, K, jnp.dtype(out_dtype).itemsize, residual is not None)
    if epilogue is not None:
        tn = N
        fits = [(K * N * 2 * (M // t) / V7X_HBM_BYTES_PER_S + (K // q - 1) * M * N * 8 / ACC_BYTES_PER_S
                 + (M // t) * (K // q) * STEP_S, t, q)
                for t in (1024, 512, 256) if M % t == 0 for q in _divisors(K, 2816, (V7X_MXU, LANES))
                if 4 * (t * q + q * tn) + (4 * t * tn if K > q else 0) + (8 * t * tn if residual is not None else 0)
                + 2 * t * epilogue.tile_bytes <= EPILOGUE_VMEM]
        _, tm, tk = min(fits)
    nk = K // tk
    if mode == "tn":
        a_spec = pl.BlockSpec((tk, tm), lambda i, j, k: (k, i))
    else:
        a_spec = pl.BlockSpec((tm, tk), lambda i, j, k: (i, k))
    if mode == "nt":
        assert b_k_off % tk == 0
        b_spec = pl.BlockSpec((tn, tk), lambda i, j, k: (j, k + b_k_off // tk))
    else:
        b_spec = pl.BlockSpec((tk, tn), lambda i, j, k: (k, j))
    dims = {"nn": NN, "nt": NT, "tn": TN}[mode]
    o_spec = pl.BlockSpec((tm, tn), lambda i, j, k: (i, j))
    has_res = residual is not None

    def rows_or_whole(shape, window=None):
        if window is not None:
            off, width = window
            return pl.BlockSpec((tm, width), lambda i, j, k: (i, off // width))
        if shape[0] == M:
            return pl.BlockSpec((tm,) + tuple(shape[1:]), lambda i, j, k: (i,) + (0,) * (len(shape) - 1))
        return pl.BlockSpec(tuple(shape), lambda i, j, k: (0,) * len(shape))

    n_in = 2 + has_res + (len(epilogue.ins) if epilogue else 0)
    n_out = len(epilogue.outs) if epilogue else 1

    def body(*refs):
        a_ref, b_ref = refs[:2]
        r_ref = refs[2] if has_res else None
        out_refs = refs[n_in:n_in + n_out]
        acc_ref = refs[-1]
        k = pl.program_id(2)
        part = _dot(a_ref[...], b_ref[...], dims)

        def finish(r):
            if has_res:
                r = r + r_ref[...].astype(F32)
            if epilogue is None:
                out_refs[0][...] = r.astype(out_dtype)
            else:
                epilogue.fn(r, refs[2 + has_res:n_in], out_refs, pl.program_id(0) == 0)

        if nk == 1:
            finish(part)
            return

        @pl.when(k == 0)
        def _():
            acc_ref[...] = part

        @pl.when(jnp.logical_and(k > 0, k < nk - 1))
        def _():
            acc_ref[...] += part

        @pl.when(k == nk - 1)
        def _():
            finish(acc_ref[...] + part)

    in_specs = [a_spec, b_spec] + ([o_spec] if has_res else [])
    args = (a, b) + ((residual,) if has_res else ())
    if epilogue is None:
        out_specs, out_shape = o_spec, jax.ShapeDtypeStruct((M, N), out_dtype)
        sem = ("parallel", "parallel", "arbitrary")
    else:
        in_specs += [rows_or_whole(x.shape, epilogue.in_windows.get(n)) for n, x in enumerate(epilogue.ins)]
        args += epilogue.ins
        out_specs = [rows_or_whole(o[0], epilogue.out_windows.get(n)) for n, o in enumerate(epilogue.outs)]
        out_shape = [jax.ShapeDtypeStruct(shp, dt) for shp, dt in epilogue.outs]
        sem = ("arbitrary", "arbitrary", "arbitrary")
    return pl.pallas_call(
        body, name=name, grid=(M // tm, N // tn, nk), in_specs=in_specs, out_specs=out_specs,
        out_shape=out_shape, scratch_shapes=[pltpu.VMEM((tm, tn), F32)] if nk > 1 else [],
        compiler_params=_params(sem),
    )(*args)


class _Rows:
    def __init__(self, T, tm):
        self.T, self.tm = T, min(tm, T // 2)
        self.nrow = T // self.tm
        self.r = self.tm // HALO
        self.nb = T // HALO

    def tile(self, w, cb=0, step=1):
        return pl.BlockSpec((self.tm, w), lambda j, i: (i, cb + step * j))

    def prev(self, w, cb=0, step=1):
        r = self.r
        return pl.BlockSpec((HALO, w), lambda j, i: (jnp.maximum(i * r - 1, 0), cb + step * j))

    def next(self, w, cb=0, step=1):
        r, nb = self.r, self.nb
        return pl.BlockSpec((HALO, w), lambda j, i: (jnp.minimum((i + 1) * r, nb - 1), cb + step * j))

    def colvec(self, k, w, cb=0, step=1):
        return pl.BlockSpec((k, w), lambda j, i: (0, cb + step * j))

    def call(self, body, name, ncol, in_specs, out_specs, out_shape, args, aliases=None):
        return pl.pallas_call(
            body, name=name, grid=(ncol, self.nrow), in_specs=in_specs, out_specs=out_specs,
            out_shape=out_shape, input_output_aliases=aliases or {},
            compiler_params=_params(("parallel", "arbitrary")),
        )(*args)


ANY = pl.BlockSpec(memory_space=pl.ANY)


def _shifts_causal(ext, nk, tm):
    out = []
    for k in range(nk):
        s = nk - 1 - k
        r = ext if s == 0 else pltpu.roll(ext, s, 0)
        out.append(r[HALO:])
    return out


def _shifts_anticausal(ext, nk, tm):
    n = ext.shape[0]
    out = []
    for k in range(nk):
        s = nk - 1 - k
        r = ext if s == 0 else pltpu.roll(ext, n - s, 0)
        out.append(r[:tm])
    return out


def _wsum(w, parts):
    acc = w[0:1, :] * parts[0]
    for k in range(1, len(parts)):
        acc = acc + w[k:k + 1, :] * parts[k]
    return acc


def _colsum(x):
    return jnp.sum(x, axis=0, keepdims=True)


def _acc_out(ref, val, first):
    @pl.when(first)
    def _():
        ref[...] = val

    @pl.when(jnp.logical_not(first))
    def _():
        ref[...] += val


def _acc_rows(ref, rows, first):
    for k, r in enumerate(rows):
        _acc_out(ref.at[k:k + 1, :], r, first)


def _norm_matmul(x, wn, b, name, b_f32=None):
    T, N = x.shape[0], b.shape[1]
    tm = min(1024, T)
    tn = max(t for t in _divisors(N, 2816, (V7X_MXU, LANES))
             if 8 * tm * D + 6 * tm * D + 4 * D * t + 4 * tm * t <= MATMUL_VMEM)

    extra = b_f32 is not None

    def body(*refs):
        x_ref, wn_ref, b_ref = refs[:3]
        o_ref, u_ref = refs[3 + extra:5 + extra]
        keep_ref = refs[-1]

        @pl.when(pl.program_id(1) == 0)
        def _():
            xv = x_ref[...]
            r = lax.rsqrt(jnp.mean(xv * xv, axis=-1, keepdims=True) + EPS)
            u = (xv * r * wn_ref[...]).astype(BF16)
            keep_ref[...] = u
            u_ref[...] = u
            if extra:
                refs[5 + extra][...] = _dot(u, refs[3][...])

        o_ref[...] = _dot(keep_ref[...], b_ref[...]).astype(BF16)

    rows = pl.BlockSpec((tm, D), lambda i, j: (i, 0))
    whole = lambda shape: pl.BlockSpec(shape, lambda i, j: (0, 0))
    narrow = pl.BlockSpec((tm, LANES), lambda i, j: (i, 0))
    return pl.pallas_call(
        body, name=name, grid=(T // tm, N // tn),
        in_specs=[rows, whole((1, D)), pl.BlockSpec((D, tn), lambda i, j: (0, j))] + [whole((D, LANES))] * extra,
        out_specs=[pl.BlockSpec((tm, tn), lambda i, j: (i, j)), rows] + [narrow] * extra,
        out_shape=[jax.ShapeDtypeStruct((T, N), BF16), jax.ShapeDtypeStruct((T, D), BF16)]
        + [jax.ShapeDtypeStruct((T, LANES), F32)] * extra,
        scratch_shapes=[pltpu.VMEM((tm, D), BF16)],
        compiler_params=_params(("parallel", "arbitrary")),
    )(*((x, wn, b) + ((b_f32,) if extra else ())))


def _rmsnorm_bwd_epilogue(x, w, dres):
    T = x.shape[0]

    def fn(dyv, ins, outs, first):
        x_ref, w_ref, dr_ref = ins
        dx_ref, dxb_ref, dw_ref = outs
        xv = x_ref[...]
        r = lax.rsqrt(jnp.mean(xv * xv, axis=-1, keepdims=True) + EPS)
        xh = xv * r
        dxh = dyv * w_ref[...]
        dx = r * (dxh - xh * jnp.mean(dxh * xh, axis=-1, keepdims=True)) + dr_ref[...]
        dx_ref[...] = dx
        dxb_ref[...] = dx.astype(BF16)
        _acc_out(dw_ref, _colsum(dyv * xh), first)

    return _Epilogue(fn, (x, w, dres), (((T, D), F32), ((T, D), BF16), ((1, D), F32)), 14 * D)


def _branch_a_fwd(proj, conv_w):
    T = proj.shape[0]
    R = _Rows(T, 512)
    tm = R.tm

    def body(p_ref, pp_ref, w_ref, o_ref):
        keep = (pl.program_id(1) > 0).astype(F32)
        cv = p_ref[:, D:2 * D].astype(F32) * p_ref[:, 2 * D:].astype(F32)
        cvp = pp_ref[:, D:2 * D].astype(F32) * pp_ref[:, 2 * D:].astype(F32) * keep
        sh = _shifts_causal(jnp.concatenate([cvp, cv], axis=0), 3, tm)
        ca = _wsum(w_ref[...], sh)
        o_ref[...] = (p_ref[:, :D].astype(F32) * ca).astype(BF16)

    return R.call(body, "branch_a_fwd", 1, [R.tile(3 * D), R.prev(3 * D), R.colvec(3, D)], R.tile(D),
                  jax.ShapeDtypeStruct((T, D), BF16), (proj, proj, conv_w))


def _branch_a_bwd(dya_in, proj, conv_w, dproj):
    T = proj.shape[0]
    R = _Rows(T, 256)
    tm = R.tm

    def body(d_ref, dn_ref, p_ref, pp_ref, pn_ref, w_ref, _alias, o_ref, dw_ref):
        i = pl.program_id(1)
        keep_p = (i > 0).astype(F32)
        keep_n = (i < R.nrow - 1).astype(F32)
        w = w_ref[...]
        b = p_ref[:, :D].astype(F32)
        c = p_ref[:, D:2 * D].astype(F32)
        v = p_ref[:, 2 * D:].astype(F32)
        cvp = pp_ref[:, D:2 * D].astype(F32) * pp_ref[:, 2 * D:].astype(F32) * keep_p
        sh = _shifts_causal(jnp.concatenate([cvp, c * v], axis=0), 3, tm)
        ca = _wsum(w, sh)
        d = d_ref[...].astype(F32)
        dca = d * b
        dca_n = dn_ref[...].astype(F32) * pn_ref[:, :D].astype(F32) * keep_n
        dsh = _shifts_anticausal(jnp.concatenate([dca, dca_n], axis=0), 3, tm)
        dcv = _wsum(w, dsh)
        o_ref[:, :D] = (d * ca).astype(BF16)
        o_ref[:, D:2 * D] = (dcv * v).astype(BF16)
        o_ref[:, 2 * D:] = (dcv * c).astype(BF16)
        _acc_rows(dw_ref, [_colsum(dca * s) for s in sh], i == 0)

    return R.call(
        body, "branch_a_bwd", 1,
        [R.tile(D), R.next(D), R.tile(3 * D), R.prev(3 * D), R.next(3 * D), R.colvec(3, D), ANY],
        [R.tile(3 * D), R.colvec(3, D)],
        [jax.ShapeDtypeStruct(dproj.shape, BF16), jax.ShapeDtypeStruct((3, D), F32)],
        (dya_in, dya_in, proj, proj, proj, conv_w, dproj), aliases={6: 0})


_XW = 512


def _xbc_fwd(proj, conv_w, conv_b):
    T = proj.shape[0]
    R = _Rows(T, 512)
    tm = R.tm
    cb = OFF_XBC // _XW

    def body(x_ref, xp_ref, w_ref, b_ref, o_ref):
        keep = (pl.program_id(1) > 0).astype(F32)
        ext = jnp.concatenate([xp_ref[...].astype(F32) * keep, x_ref[...].astype(F32)], axis=0)
        pre = _wsum(w_ref[...], _shifts_causal(ext, 4, tm)) + b_ref[...]
        o_ref[...] = (pre * _sigmoid(pre)).astype(BF16)

    return R.call(body, "xbc_fwd", DX // _XW,
                  [R.tile(_XW, cb), R.prev(_XW, cb), R.colvec(4, _XW), R.colvec(1, _XW)], R.tile(_XW),
                  jax.ShapeDtypeStruct((T, DX), BF16), (proj, proj, conv_w, conv_b))


def _xbc_bwd(dact, proj, conv_w, conv_b, dproj):
    T = proj.shape[0]
    R = _Rows(T, 512)
    tm = R.tm
    cb = OFF_XBC // _XW

    def body(d_ref, dn_ref, x_ref, xp_ref, xn_ref, w_ref, b_ref, _alias, o_ref, dw_ref, db_ref):
        i = pl.program_id(1)
        keep_p = (i > 0).astype(F32)
        keep_n = (i < R.nrow - 1).astype(F32)
        w = w_ref[...]
        ext = jnp.concatenate([xp_ref[...].astype(F32) * keep_p, x_ref[...].astype(F32),
                               xn_ref[...].astype(F32)], axis=0)
        sh = _shifts_causal(ext, 4, tm + HALO)
        pre = _wsum(w, sh) + b_ref[...]
        s = _sigmoid(pre)
        dsilu = s * (1.0 + pre * (1.0 - s))
        dext = jnp.concatenate([d_ref[...].astype(F32), dn_ref[...].astype(F32) * keep_n], axis=0)
        dpre = dext * dsilu
        dsh = _shifts_anticausal(dpre, 4, tm)
        o_ref[...] = _wsum(w, dsh).astype(BF16)
        dp = dpre[:tm]
        _acc_rows(dw_ref, [_colsum(dp * q[:tm]) for q in sh], i == 0)
        _acc_out(db_ref, _colsum(dp), i == 0)

    return R.call(
        body, "xbc_bwd", DX // _XW,
        [R.tile(_XW), R.next(_XW), R.tile(_XW, cb), R.prev(_XW, cb), R.next(_XW, cb),
         R.colvec(4, _XW), R.colvec(1, _XW), ANY],
        [R.tile(_XW, cb), R.colvec(4, _XW), R.colvec(1, _XW)],
        [jax.ShapeDtypeStruct(dproj.shape, BF16), jax.ShapeDtypeStruct((4, DX), F32),
         jax.ShapeDtypeStruct((1, DX), F32)],
        (dact, dact, proj, proj, proj, conv_w, conv_b, dproj), aliases={7: 0})


def _softplus(x):
    return jnp.maximum(x, 0.0) + jnp.log(1.0 + jnp.exp(-jnp.abs(x)))


def _dt_rows(T):
    return min(8 * CH, T // 2)


def _dt_fwd(dt_raw, dt_bias_p, a_log_p):
    T = dt_raw.shape[0]
    rows = _dt_rows(T)

    def body(r_ref, b_ref, al_ref, dt_ref, ac_ref, acT_ref):
        dt = _softplus(r_ref[...] + b_ref[...])
        s = dt * (-jnp.exp(al_ref[...]))
        row = lax.broadcasted_iota(jnp.int32, (rows, LANES), 0) % CH
        k = 1
        while k < CH:
            s = s + jnp.where(row >= k, pltpu.roll(s, k, 0), 0.0)
            k *= 2
        dt_ref[...] = dt
        ac_ref[...] = s
        for q in range(0, rows, CH):
            acT_ref[q:q + CH] = s[q:q + CH].T

    blk = pl.BlockSpec((rows, LANES), lambda i: (i, 0))
    vec = pl.BlockSpec((1, LANES), lambda i: (0, 0))
    return pl.pallas_call(
        body, name="dt_fwd", grid=(T // rows,), in_specs=[blk, vec, vec], out_specs=[blk, blk, blk],
        out_shape=[jax.ShapeDtypeStruct((T, LANES), F32)] * 3, compiler_params=_params(("parallel",)),
    )(dt_raw, dt_bias_p, a_log_p)


def _dt_bwd(dacum, ddt_x, dt_raw, dt_bias_p, a_log_p, dproj):
    T = dt_raw.shape[0]
    rows = _dt_rows(T)
    nc = T // rows

    def body(da_ref, dx_ref, r_ref, b_ref, al_ref, _alias, o_ref, db_ref, dal_ref):
        i = pl.program_id(0)
        a = -jnp.exp(al_ref[...])
        z = r_ref[...] + b_ref[...]
        dt = _softplus(z)
        s = da_ref[...]
        row = lax.broadcasted_iota(jnp.int32, (rows, LANES), 0) % CH
        k = 1
        while k < CH:
            s = s + jnp.where(row < CH - k, pltpu.roll(s, rows - k, 0), 0.0)
            k *= 2
        ddt = s * a + dx_ref[...]
        draw = ddt * _sigmoid(z)
        o_ref[:, :LANES] = draw.astype(BF16)
        o_ref[:, LANES:] = jnp.zeros((rows, NIP - OFF_DT - LANES), BF16)
        _acc_out(db_ref, _colsum(draw), i == 0)
        _acc_out(dal_ref, _colsum(s * dt), i == 0)

        @pl.when(i == nc - 1)
        def _():
            dal_ref[...] = dal_ref[...] * a

    blk = pl.BlockSpec((rows, LANES), lambda i: (i, 0))
    vec = pl.BlockSpec((1, LANES), lambda i: (0, 0))
    oblk = pl.BlockSpec((rows, NIP - OFF_DT), lambda i: (i, OFF_DT // (NIP - OFF_DT)))
    return pl.pallas_call(
        body, name="dt_bwd", grid=(nc,), in_specs=[blk, blk, blk, vec, vec, ANY], out_specs=[oblk, vec, vec],
        out_shape=[jax.ShapeDtypeStruct(dproj.shape, BF16), jax.ShapeDtypeStruct((1, LANES), F32),
                   jax.ShapeDtypeStruct((1, LANES), F32)],
        input_output_aliases={5: 0}, compiler_params=_params(("arbitrary",)),
    )(dacum, ddt_x, dt_raw, dt_bias_p, a_log_p, dproj)


_GW = DI // NG
_HG = NH // NG
_NEG = -1e30


def _interleave(gens):
    out, live = [None] * len(gens), list(range(len(gens)))
    while live:
        for i in list(live):
            try:
                next(gens[i])
            except StopIteration as stop:
                out[i] = stop.value
                live.remove(i)
    return out


def _pair_lanes(left, v0, v1):
    return jnp.where(left, v0, v1)


def _ssd_specs(T, rev):
    nc = T // CH
    cm = (lambda c: nc - 1 - c) if rev else (lambda c: c)
    bw = NG * NS
    return dict(
        xs=pl.BlockSpec((CH, DI), lambda c: (cm(c), 0)),
        bm=pl.BlockSpec((CH, bw), lambda c: (cm(c), DI // bw)),
        cmat=pl.BlockSpec((CH, bw), lambda c: (cm(c), DI // bw + 1)),
        xbc=pl.BlockSpec((CH, DX), lambda c: (cm(c), 0)),
        col=pl.BlockSpec((CH, LANES), lambda c: (cm(c), 0)),
        dsk=pl.BlockSpec((1, DI), lambda c: (0, 0)),
        state=pl.BlockSpec((1, NS, DI), lambda c: (cm(c), 0, 0)),
    )


def _last(ref, lo, hi):
    return ref.at[(slice(None),) * (len(ref.shape) - 1) + (slice(lo, hi),)]


def _group_views(g, wide, narrow):
    return [_last(r, g * _GW, (g + 1) * _GW) for r in wide] + [_last(r, g * NS, (g + 1) * NS) for r in narrow]


def _ssd_fwd(xact, dt, acum, acumT, dsk_rep, proj, norm_w):
    T = xact.shape[0]
    nc = T // CH
    sp = _ssd_specs(T, False)

    def body(*refs):
        xs, bm, cmat, dtr, acr, actr, dsk, zr, nw, y, yn, spv, S_ref = refs

        @pl.when(pl.program_id(0) == 0)
        def _():
            S_ref[...] = jnp.zeros_like(S_ref)

        _interleave([group(g * _HG, dtr[...], acr[...], actr[...],
                           *_group_views(g, (xs, dsk, zr, nw, y, yn, spv, S_ref), (bm, cmat))) for g in range(NG)])

    def group(hb, dt, ac, acT, xs_ref, dsk_ref, z_ref, nw_ref, y_ref, yn_ref, sp_ref, S_ref, b_ref, c_ref):
        Bm, Cm = b_ref[...], c_ref[...]
        S = S_ref[...]
        sp_ref[0] = S
        cb = _dot(Cm, Bm, NT)
        CS = _dot(Cm, S.astype(BF16))
        row = lax.broadcasted_iota(jnp.int32, (CH, CH), 0)
        col = lax.broadcasted_iota(jnp.int32, (CH, CH), 1)
        tril = row >= col
        left = col < HP
        xd_parts, dec_parts = [], []
        for p in range(_HG // 2):
            sl = slice(p * LANES, (p + 1) * LANES)
            j0, j1 = hb + 2 * p, hb + 2 * p + 1
            xp = xs_ref[:, sl].astype(F32)
            a0, a1 = ac[:, j0:j0 + 1], ac[:, j1:j1 + 1]
            al0, al1 = ac[CH - 1:CH, j0:j0 + 1], ac[CH - 1:CH, j1:j1 + 1]
            X = xp * _pair_lanes(left, dt[:, j0:j0 + 1], dt[:, j1:j1 + 1])
            Xb = X.astype(BF16)
            Ws = [(cb * jnp.exp(jnp.where(tril, aj - acT[j:j + 1, :], _NEG))).astype(BF16)
                  for j, aj in ((j0, a0), (j1, a1))]
            Xs = [jnp.where(m, Xb, jnp.zeros_like(Xb)) for m in (left, jnp.logical_not(left))]
            yield
            yd = _dot(jnp.concatenate(Ws, axis=1), jnp.concatenate(Xs, axis=0))
            yield
            eal = _pair_lanes(left, jnp.exp(a0), jnp.exp(a1))
            y = yd + eal * CS[:, sl] + dsk_ref[:, sl] * xp
            y_ref[:, sl] = y.astype(BF16)
            xd_parts.append(X * _pair_lanes(left, jnp.exp(al0 - a0), jnp.exp(al1 - a1)))
            dec_parts.append(_pair_lanes(left[0:1], jnp.exp(al0), jnp.exp(al1)))
        Xd = jnp.concatenate(xd_parts, axis=1).astype(BF16)
        dec = jnp.concatenate(dec_parts, axis=1)
        S_ref[...] = dec * S + _dot(Bm, Xd, TN)
        yield
        z = z_ref[...].astype(F32)
        yf = y_ref[...].astype(F32) * z * _sigmoid(z)
        r = lax.rsqrt(jnp.mean(yf * yf, axis=-1, keepdims=True) + EPS)
        yn_ref[...] = (yf * r * nw_ref[...]).astype(BF16)

    zspec = pl.BlockSpec((CH, DI), lambda c: (c, OFF_Z // DI))
    return pl.pallas_call(
        body, name="ssd_fwd", grid=(nc,),
        in_specs=[sp["xs"], sp["bm"], sp["cmat"], sp["col"], sp["col"], sp["col"], sp["dsk"], zspec, sp["dsk"]],
        out_specs=[sp["xs"], sp["xs"], sp["state"]],
        out_shape=[jax.ShapeDtypeStruct((T, DI), BF16), jax.ShapeDtypeStruct((T, DI), BF16),
                   jax.ShapeDtypeStruct((nc, NS, DI), F32)],
        scratch_shapes=[pltpu.VMEM((NS, DI), F32)],
        compiler_params=_params(("arbitrary",)),
    )(xact, xact, xact, dt, acum, acumT, dsk_rep, proj, norm_w)


def _ssd_bwd(dn, y, proj, norm_w, dproj, xact, dt, acum, acumT, dsk_rep, sprev):
    T = xact.shape[0]
    nc = T // CH
    sp = _ssd_specs(T, True)

    def body(*refs):
        (xs, bm, cmat, dtr, acr, actr, dsk, dnr, yr, zr, nw, spv, _alias, lanes_of_ref, rows_of_ref,
         dxa, ddtx, dAc, dskacc, dzr, dnw, dS_ref) = refs
        first = pl.program_id(0) == 0

        @pl.when(first)
        def _():
            dS_ref[...] = jnp.zeros_like(dS_ref)

        dbc = _last(dxa, DI, DX)
        ddtx_sum = jnp.zeros((CH, LANES), F32)
        dAc_sum = jnp.zeros((CH, LANES), F32)
        for a, b in _interleave([group(first, g * _HG, dtr[...], acr[...], actr[...],
                                       lanes_of_ref.at[g * _GW:(g + 1) * _GW],
                                       rows_of_ref.at[g * _HG * CH:(g + 1) * _HG * CH],
                                       *_group_views(g, (xs, dsk, dnr, yr, zr, nw, dzr, dnw, spv, dxa, dskacc, dS_ref),
                                                     (bm, cmat, dbc, _last(dbc, NG * NS, 2 * NG * NS))))
                                 for g in range(NG)]):
            ddtx_sum, dAc_sum = ddtx_sum + a, dAc_sum + b
        ddtx[...] = ddtx_sum
        dAc[...] = dAc_sum

    def group(first, hb, dt, ac, acT, lanes_of_ref, rows_of_ref, xs_ref, dsk_ref, dn_ref, y_ref, z_ref, nw_ref, dz_ref, dnw_ref, sp_ref, dx_ref,
              dskacc_ref, dS_ref, b_ref, c_ref, dB_ref, dC_ref):
        z = z_ref[...].astype(F32)
        yv = y_ref[...].astype(F32)
        sg = _sigmoid(z)
        silu = z * sg
        yf = yv * silu
        rn = lax.rsqrt(jnp.mean(yf * yf, axis=-1, keepdims=True) + EPS)
        yh = yf * rn
        dnv = dn_ref[...].astype(F32)
        dyh = dnv * nw_ref[...]
        dyf = rn * (dyh - yh * jnp.mean(dyh * yh, axis=-1, keepdims=True))
        dyg = dyf * silu
        dz_ref[...] = (dyf * yv * sg * (1.0 + z * (1.0 - sg))).astype(BF16)
        _acc_out(dnw_ref, _colsum(dnv * yh), first)
        Bm, Cm = b_ref[...], c_ref[...]
        S = sp_ref[0]
        dS = dS_ref[...]
        Sb, dSb = S.astype(BF16), dS.astype(BF16)
        cb = _dot(Cm, Bm, NT)
        cbT = _dot(Bm, Cm, NT)
        CmT = Cm.T
        CS = _dot(Cm, Sb)
        T1 = _dot(Bm, dSb)
        yield
        row = lax.broadcasted_iota(jnp.int32, (CH, CH), 0)
        col = lax.broadcasted_iota(jnp.int32, (CH, CH), 1)
        tril = row >= col
        triu = row <= col
        left = col < HP
        lastrow = lax.broadcasted_iota(jnp.int32, (CH, 1), 0) == CH - 1
        dCB = jnp.zeros((CH, CH), F32)
        dCBT = jnp.zeros((CH, CH), F32)
        xd_parts, dye_parts, dec_parts, dsk_parts, dxx_parts, gr_parts, end_parts, qd_parts = ([] for _ in range(8))
        for p in range(_HG // 2):
            sl = slice(p * LANES, (p + 1) * LANES)
            j0, j1 = hb + 2 * p, hb + 2 * p + 1
            xp = xs_ref[:, sl].astype(F32)
            dyp = dyg[:, sl]
            a0, a1 = ac[:, j0:j0 + 1], ac[:, j1:j1 + 1]
            al0, al1 = ac[CH - 1:CH, j0:j0 + 1], ac[CH - 1:CH, j1:j1 + 1]
            dtl = _pair_lanes(left, dt[:, j0:j0 + 1], dt[:, j1:j1 + 1])
            X = xp * dtl
            Xb = X.astype(BF16)
            eal = _pair_lanes(left, jnp.exp(a0), jnp.exp(a1))
            dtel = _pair_lanes(left, jnp.exp(al0 - a0), jnp.exp(al1 - a1))
            T1p = T1[:, sl]
            Rm = T1p * dtel * X
            decp = _pair_lanes(left[0:1], jnp.exp(al0), jnp.exp(al1))
            gr_parts.append(dyp * (eal * CS[:, sl]) - Rm)
            end_parts.append(Rm + decp * (dS[:, sl] * S[:, sl]))
            dXd = jnp.zeros((CH, LANES), F32)
            for j, aj, mask in ((j0, a0, left), (j1, a1, jnp.logical_not(left))):
                dYm = jnp.where(mask, dyp, 0.0).astype(BF16)
                dWm = _dot(dYm, Xb, NT)
                dWmT = _dot(Xb, dYm, NT)
                yield
                e = aj - acT[j:j + 1, :]
                P = dWm * jnp.exp(jnp.where(tril, e, _NEG))
                LmT = jnp.exp(jnp.where(triu, -e, _NEG))
                PT = dWmT * LmT
                dCB = dCB + P
                dCBT = dCBT + PT
                yield
                dXd = dXd + _dot((cbT * LmT).astype(BF16), dYm)
                qd_parts.append((P * cb - PT * cbT).astype(BF16))
                yield
            dX = dXd + dtel * T1p
            dxx_parts.append(dX * xp)
            dx_ref[:, sl] = (dX * dtl + dsk_ref[:, sl] * dyp).astype(BF16)
            dsk_parts.append(_colsum(dyp * xp))
            xd_parts.append(X * dtel)
            dye_parts.append(dyp * eal)
            dec_parts.append(decp)
            yield
        Xd = jnp.concatenate(xd_parts, axis=1).astype(BF16)
        dYe = jnp.concatenate(dye_parts, axis=1).astype(BF16)
        dec = jnp.concatenate(dec_parts, axis=1)
        def lane_sums(parts):
            return _dot(jnp.concatenate(parts, axis=1).astype(BF16), lanes_of_ref[...])
        ddtx = lane_sums(dxx_parts)
        dAc = (_dot(jnp.concatenate(qd_parts, axis=1), rows_of_ref[...]) + lane_sums(gr_parts)
               + jnp.where(lastrow, _colsum(lane_sums(end_parts)), 0.0))
        dC_ref[...] = (_dot(dCB.astype(BF16), Bm) + _dot(dYe, Sb, NT)).astype(BF16)
        dB_ref[...] = (_dot(dCBT.astype(BF16), Cm) + _dot(Xd, dSb, NT)).astype(BF16)
        dS_ref[...] = _dot(CmT, dYe) + dec * dS
        _acc_out(dskacc_ref, jnp.concatenate(dsk_parts, axis=1), first)
        return ddtx, dAc

    zspec = pl.BlockSpec((CH, DI), lambda c: (nc - 1 - c, OFF_Z // DI))
    head = lax.broadcasted_iota(jnp.int32, (1, LANES), 1)
    lanes_of = (lax.broadcasted_iota(jnp.int32, (DI, 1), 0) // HP == head).astype(BF16)
    rows_of = (lax.broadcasted_iota(jnp.int32, (NH * CH, 1), 0) // CH == head).astype(BF16)
    return pl.pallas_call(
        body, name="ssd_bwd", grid=(nc,),
        in_specs=[sp["xs"], sp["bm"], sp["cmat"], sp["col"], sp["col"], sp["col"], sp["dsk"], sp["xs"], sp["xs"],
                  zspec, sp["dsk"], sp["state"], ANY, pl.BlockSpec(lanes_of.shape, lambda c: (0, 0)),
                  pl.BlockSpec(rows_of.shape, lambda c: (0, 0))],
        out_specs=[sp["xbc"], sp["col"], sp["col"], sp["dsk"], zspec, sp["dsk"]],
        out_shape=[jax.ShapeDtypeStruct((T, DX), BF16), jax.ShapeDtypeStruct((T, LANES), F32),
                   jax.ShapeDtypeStruct((T, LANES), F32), jax.ShapeDtypeStruct((1, DI), F32),
                   jax.ShapeDtypeStruct(dproj.shape, BF16), jax.ShapeDtypeStruct((1, DI), F32)],
        scratch_shapes=[pltpu.VMEM((NS, DI), F32)], input_output_aliases={12: 4},
        compiler_params=_params(("arbitrary",)),
    )(xact, xact, xact, dt, acum, acumT, dsk_rep, dn, y, proj, norm_w, sprev, dproj, lanes_of, rows_of)


def _merge_fwd_epilogue(proj, ya):
    T = proj.shape[0]

    def fn(ysv, ins, outs, first):
        g_ref, ya_ref = ins
        m_ref, ys_ref = outs
        ga = _sigmoid(g_ref[:, :D].astype(F32))
        gs = _sigmoid(g_ref[:, D:].astype(F32))
        m_ref[...] = (ga * ya_ref[...].astype(F32) + gs * ysv).astype(BF16)
        ys_ref[...] = ysv.astype(BF16)

    return _Epilogue(fn, (proj, ya), (((T, D), BF16), ((T, D), BF16)), 10 * D, in_windows={0: (OFF_G, 2 * D)})


def _merge_bwd_epilogue(proj, ya, ys, ncols):
    T = proj.shape[0]

    def fn(d, ins, outs, first):
        g_ref, ya_ref, ys_ref = ins
        dg_ref, dya_ref, dys_ref = outs
        ga = _sigmoid(g_ref[:, :D].astype(F32))
        gs = _sigmoid(g_ref[:, D:].astype(F32))
        dya_ref[...] = (d * ga).astype(BF16)
        dys_ref[...] = (d * gs).astype(BF16)
        dg_ref[:, :D] = (d * ya_ref[...].astype(F32) * ga * (1.0 - ga)).astype(BF16)
        dg_ref[:, D:] = (d * ys_ref[...].astype(F32) * gs * (1.0 - gs)).astype(BF16)

    window = (OFF_G, 2 * D)
    return _Epilogue(fn, (proj, ya, ys), (((T, ncols), BF16), ((T, D), BF16), ((T, D), BF16)), 16 * D,
                     in_windows={0: window}, out_windows={0: window})


_FW = 1408
_FB = FF // _FW


def _ffn_act_fwd(hv, conv_w, conv_b):
    T = hv.shape[0]
    R = _Rows(T, 256)
    tm = R.tm

    def body(h1_ref, h1p_ref, h3_ref, w_ref, b_ref, o_ref):
        keep = (pl.program_id(1) > 0).astype(F32)
        ext = jnp.concatenate([h1p_ref[...].astype(F32) * keep, h1_ref[...].astype(F32)], axis=0)
        pre = _wsum(w_ref[...], _shifts_causal(ext, 3, tm)) + b_ref[...]
        o_ref[...] = (pre * _sigmoid(pre) * h3_ref[...].astype(F32)).astype(BF16)

    return R.call(body, "ffn_act_fwd", _FB,
                  [R.tile(_FW), R.prev(_FW), R.tile(_FW, _FB), R.colvec(3, _FW), R.colvec(1, _FW)],
                  R.tile(_FW), jax.ShapeDtypeStruct((T, FF), BF16), (hv, hv, hv, conv_w, conv_b))


def _ffn_act_bwd(dg, hv, conv_w, conv_b):
    T = hv.shape[0]
    R = _Rows(T, 256)
    tm = R.tm

    def body(dg_ref, dgn_ref, h1_ref, h1p_ref, h1n_ref, h3_ref, h3n_ref, w_ref, b_ref, dh3_ref, dh1_ref, dw_ref,
             db_ref):
        i = pl.program_id(1)
        keep_p = (i > 0).astype(F32)
        keep_n = (i < R.nrow - 1).astype(F32)
        w = w_ref[...]
        ext = jnp.concatenate([h1p_ref[...].astype(F32) * keep_p, h1_ref[...].astype(F32),
                               h1n_ref[...].astype(F32)], axis=0)
        sh = _shifts_causal(ext, 3, tm + HALO)
        pre = _wsum(w, sh) + b_ref[...]
        s = _sigmoid(pre)
        d = jnp.concatenate([dg_ref[...].astype(F32), dgn_ref[...].astype(F32) * keep_n], axis=0)
        h3 = jnp.concatenate([h3_ref[...].astype(F32), h3n_ref[...].astype(F32)], axis=0)
        dh3_ref[...] = (d[:tm] * pre[:tm] * s[:tm]).astype(BF16)
        dpre = d * h3 * s * (1.0 + pre * (1.0 - s))
        dh1_ref[...] = _wsum(w, _shifts_anticausal(dpre, 3, tm)).astype(BF16)
        dp = dpre[:tm]
        _acc_rows(dw_ref, [_colsum(dp * q[:tm]) for q in sh], i == 0)
        _acc_out(db_ref, _colsum(dp), i == 0)

    return R.call(
        body, "ffn_act_bwd", _FB,
        [R.tile(_FW), R.next(_FW), R.tile(_FW), R.prev(_FW), R.next(_FW), R.tile(_FW, _FB), R.next(_FW, _FB),
         R.colvec(3, _FW), R.colvec(1, _FW)],
        [R.tile(_FW), R.tile(_FW), R.colvec(3, _FW), R.colvec(1, _FW)],
        [jax.ShapeDtypeStruct((T, FF), BF16), jax.ShapeDtypeStruct((T, FF), BF16),
         jax.ShapeDtypeStruct((3, FF), F32), jax.ShapeDtypeStruct((1, FF), F32)],
        (dg, dg, hv, hv, hv, hv, hv, conv_w, conv_b))


def _final_loss_epilogue(w, target):
    T = target.shape[0]

    def fn(xv, ins, outs, first):
        w_ref, t_ref = ins
        l_ref, dh_ref, dhb_ref, dw_ref = outs
        wv = w_ref[...]
        r = lax.rsqrt(jnp.mean(xv * xv, axis=-1, keepdims=True) + EPS)
        xh = xv * r
        err = xh * wv - t_ref[...]
        part = 0.5 * jnp.sum(jnp.mean(err * err, axis=-1, keepdims=True), axis=0, keepdims=True)
        _acc_out(l_ref, jnp.broadcast_to(part, l_ref.shape), first)
        dy = err * (1.0 / D)
        dxh = dy * wv
        dh = r * (dxh - xh * jnp.mean(dxh * xh, axis=-1, keepdims=True))
        dh_ref[...] = dh
        dhb_ref[...] = dh.astype(BF16)
        _acc_out(dw_ref, _colsum(dy * xh), first)

    return _Epilogue(fn, (w, target),
                     (((8, LANES), F32), ((T, D), F32), ((T, D), BF16), ((1, D), F32)), 10 * D)


def _pad_lanes(v, n=LANES):
    return jnp.pad(v, ((0, 0), (0, n - v.shape[1])))


class _Hooks:
    def before_in_proj(self, w_in):
        return w_in

    def late_weights(self, wts, after):
        return wts

    def grads_ready(self, grads, tie):
        return tie

    def mark(self, name, value):
        pass


def _local_step(x, target, wts, hooks=None):
    hooks = hooks or _Hooks()
    T = x.shape[0]
    w_in = wts["w_in"]
    dt_bias_p, a_log_p = _pad_lanes(wts["dt_bias"]), _pad_lanes(wts["a_log"])
    dsk_rep = jnp.repeat(wts["d_skip"], HP, axis=1)

    w_in = hooks.before_in_proj(w_in)
    proj, u, dt_raw = _norm_matmul(x, wts["norm_mix_w"], w_in, "norm_mm_in", w_in[:, OFF_DT:OFF_DT + LANES])
    ya_in = _branch_a_fwd(proj, wts["conv_a_w"])
    xact = _xbc_fwd(proj, wts["ssd_conv_w"], wts["ssd_conv_b"])
    dt, acum, acumT = _dt_fwd(dt_raw, dt_bias_p, a_log_p)
    y_ssd, yn, sprev = _ssd_fwd(xact, dt, acum, acumT, dsk_rep, proj, wts["ssd_norm_w"])
    late = hooks.late_weights(wts, yn)
    w_a_out, w_s_out, w_o, w_up, w_down = (late[k] for k in ("w_a_out", "w_s_out", "w_o", "w_up", "w_down"))
    y_a = _matmul(ya_in, w_a_out, mode="nn", out_dtype=BF16, name="mm_a_out")
    merged, y_s = _matmul(yn, w_s_out, mode="nn", out_dtype=BF16, name="mm_s_out_merge",
                          epilogue=_merge_fwd_epilogue(proj, y_a))
    h1 = _matmul(merged, w_o, mode="nn", out_dtype=F32, name="mm_o", residual=x)
    hv, v = _norm_matmul(h1, wts["norm_ffn_w"], w_up, "norm_mm_up")
    gact = _ffn_act_fwd(hv, wts["ffn_conv_w"], wts["ffn_conv_b"])
    loss, dh2, dh2b, g_final = _matmul(gact, w_down, mode="nn", out_dtype=F32, name="mm_down_loss", residual=h1,
                                       epilogue=_final_loss_epilogue(wts["final_norm_w"], target))

    grads = {"final_norm_w": g_final}
    grads["w_down"] = _matmul(gact, dh2b, mode="tn", out_dtype=F32, name="mm_down_dw")
    dgact = _matmul(dh2b, w_down, mode="nt", out_dtype=BF16, name="mm_down_dx")
    dh3, dh1c, grads["ffn_conv_w"], grads["ffn_conv_b"] = _ffn_act_bwd(dgact, hv, wts["ffn_conv_w"], wts["ffn_conv_b"])
    grads["w_up"] = (_matmul(v, dh1c, mode="tn", out_dtype=F32, name="mm_up_dw1"),
                     _matmul(v, dh3, mode="tn", out_dtype=F32, name="mm_up_dw3"))
    dv = _matmul(dh1c, w_up, mode="nt", out_dtype=F32, name="mm_up_dx1")
    dh1, dh1b, grads["norm_ffn_w"] = _matmul(
        dh3, w_up, mode="nt", out_dtype=F32, name="mm_up_dx3_norm", residual=dv, b_k_off=FF,
        epilogue=_rmsnorm_bwd_epilogue(h1, wts["norm_ffn_w"], dh2))
    grads["w_o"] = _matmul(merged, dh1b, mode="tn", out_dtype=F32, name="mm_o_dw")
    dproj, dya, dys = _matmul(dh1b, w_o, mode="nt", out_dtype=BF16, name="mm_o_dx_merge",
                              epilogue=_merge_bwd_epilogue(proj, y_a, y_s, NIP))
    grads["w_a_out"] = _matmul(ya_in, dya, mode="tn", out_dtype=F32, name="mm_a_out_dw")
    dya_in = _matmul(dya, w_a_out, mode="nt", out_dtype=BF16, name="mm_a_out_dx")
    dproj, grads["conv_a_w"] = _branch_a_bwd(dya_in, proj, wts["conv_a_w"], dproj)
    grads["w_s_out"] = _matmul(yn, dys, mode="tn", out_dtype=F32, name="mm_s_out_dw")
    dys = hooks.grads_ready({k: grads[k] for k in ("w_a_out", "w_s_out", "w_o", "w_up", "w_down")}, dys)
    dyn =_matmul(dys, w_s_out, mode="nt", out_dtype=BF16, name="mm_s_out_dx")
    dxact, ddt_x, dacum, dskl, dproj, grads["ssd_norm_w"] = _ssd_bwd(
        dyn, y_ssd, proj, wts["ssd_norm_w"], dproj, xact, dt, acum, acumT, dsk_rep, sprev)
    hooks.mark("ssd_bwd", dxact)
    grads["d_skip"] = dskl.reshape(NH, HP).sum(axis=1).reshape(1, NH)
    dproj, grads["ssd_conv_w"], grads["ssd_conv_b"] = _xbc_bwd(dxact, proj, wts["ssd_conv_w"], wts["ssd_conv_b"], dproj)
    dproj, g_dtb, g_alog = _dt_bwd(dacum, ddt_x, dt_raw, dt_bias_p, a_log_p, dproj)
    grads["dt_bias"], grads["a_log"] = g_dtb[:, :NH], g_alog[:, :NH]
    grads["w_in"] = _matmul(u, dproj, mode="tn", out_dtype=F32, name="mm_in_dw")
    dproj = hooks.grads_ready({"w_in": grads["w_in"]}, dproj)
    grad_x, _, grads["norm_mix_w"] = _matmul(dproj, w_in, mode="nt", out_dtype=F32, name="mm_in_dx_norm",
                                             epilogue=_rmsnorm_bwd_epilogue(x, wts["norm_mix_w"], dh1))
    return loss, grad_x, grads


def _permute_w_in(slabs):
    cs = slabs.shape[2]
    pieces = []
    for o, n, no in sorted(_SEGS, key=lambda seg: seg[2]):
        for s in range(slabs.shape[0]):
            lo, hi = max(o, s * cs), min(o + n, (s + 1) * cs)
            if lo < hi:
                pieces.append(slabs[s][:, lo - s * cs:hi - s * cs])
    pieces.append(jnp.zeros((slabs.shape[1], NIP - OFF_DT - _SEGS[-1][1]), slabs.dtype))
    return jnp.concatenate(pieces, axis=1)


def _unpermute_w_in(g):
    cs = NI // NCHIP
    slabs = []
    for s in range(NCHIP):
        pieces = []
        for o, n, no in sorted(_SEGS):
            lo, hi = max(o, s * cs), min(o + n, (s + 1) * cs)
            if lo < hi:
                pieces.append(g[:, no + lo - o:no + hi - o])
        slabs.append(jnp.concatenate(pieces, axis=1))
    return jnp.stack(slabs)


MESH = pl.DeviceIdType.MESH
NCHIP = 4
NDEV = 8

_W_IN = (("w_in", D, NI // NCHIP, 1),)
_W_REST = (("w_a_out", D // NCHIP, D, 0), ("w_s_out", DI // NCHIP, D, 0), ("w_o", D // NCHIP, D, 0),
           ("w_up", D, 2 * FF // NCHIP, 1), ("w_down", FF // NCHIP, D, 0))


def _coords():
    return lax.axis_index("x"), lax.axis_index("y"), lax.axis_index("c")


def _other_chips(x, y):
    return [(1 - x, y), (x, 1 - y), (1 - x, 1 - y)]


def _ag_weights(shard):
    nrows = shard.shape[0]
    hr = nrows // 2

    def body(x_ref, out_ref, send_sems, recv_sems, local_sem):
        x, y, c = _coords()
        me = 2 * x + y
        chips = _other_chips(x, y)

        def rows(s, h):
            return out_ref.at[s, pl.ds(h * hr, hr), :]

        def copy(k, s, h, to, src=None):
            return pltpu.make_async_remote_copy(
                src_ref=rows(s, h) if src is None else src, dst_ref=rows(s, h),
                send_sem=send_sems.at[k], recv_sem=recv_sems.at[k], device_id=to, device_id_type=MESH)

        mine = pltpu.make_async_copy(x_ref, out_ref.at[me], local_sem)
        mine.start()
        first = [copy(k, me, c, (*chip, c), src=x_ref.at[pl.ds(c * hr, hr), :]) for k, chip in enumerate(chips)]
        for cp in first:
            cp.start()
        passed = []
        for k, chip in enumerate(chips):
            s = 2 * chip[0] + chip[1]
            copy(k, s, c, (x, y, c)).wait_recv()
            fwd = copy(3 + k, s, c, (x, y, 1 - c))
            fwd.start()
            passed.append(fwd)
        for k, chip in enumerate(chips):
            copy(3 + k, 2 * chip[0] + chip[1], 1 - c, (x, y, c)).wait_recv()
        for cp in first + passed:
            cp.wait_send()
        mine.wait()

    return pl.pallas_call(
        body, name="ag_weights", in_specs=[ANY], out_specs=ANY,
        out_shape=jax.ShapeDtypeStruct((NCHIP,) + shard.shape, shard.dtype),
        scratch_shapes=[pltpu.SemaphoreType.DMA((6,)), pltpu.SemaphoreType.DMA((6,)), pltpu.SemaphoreType.DMA],
        compiler_params=pltpu.CompilerParams(has_side_effects=True),
    )(shard)


HBM = pl.BlockSpec(memory_space=pltpu.HBM)
SEM = pl.BlockSpec(memory_space=pltpu.SEMAPHORE)
_EFFECT = pltpu.SideEffectType.DATAFLOW_SIDE_EFFECTING
_NCOPY = NCHIP - 1


def _plan_bcast(src_ref, land_ref, send_sems, recv_sems, base):
    x, y, c = _coords()
    sends, lands = [], []
    for k, chip in enumerate(_other_chips(x, y)):
        def copy(slot):
            return pltpu.make_async_remote_copy(
                src_ref=src_ref, dst_ref=land_ref.at[slot], send_sem=send_sems.at[base + k],
                recv_sem=recv_sems.at[base + k], device_id=(*chip, c), device_id_type=MESH)
        sends.append(copy(2 * x + y))
        lands.append(copy(2 * chip[0] + chip[1]))
    return sends, lands


def _plan_scatter(src_ref, land_ref, send_sems, recv_sems, base):
    x, y, c = _coords()
    cps = [pltpu.make_async_remote_copy(
        src_ref=src_ref.at[2 * chip[0] + chip[1]], dst_ref=land_ref.at[k], send_sem=send_sems.at[base + k],
        recv_sem=recv_sems.at[base + k], device_id=(*chip, c), device_id_type=MESH)
        for k, chip in enumerate(_other_chips(x, y))]
    return cps, cps


def _plan_all(plan, refs, n):
    sends, lands = [], []
    for t in range(n):
        s, l = plan(refs[t], refs[n + t], refs[2 * n], refs[2 * n + 1], t * _NCOPY)
        sends += s
        lands += l
    return sends, lands


def _split_start(name, srcs, lands, plan):
    n = len(srcs)

    def body(*refs):
        for cp in _plan_all(plan, refs, n)[0]:
            cp.start()
        refs[-1][...] = jnp.zeros_like(refs[-1])

    arrays = list(srcs) + list(lands)
    outs = pl.pallas_call(
        body, name=name,
        out_shape=(pltpu.SemaphoreType.DMA((n * _NCOPY,)), pltpu.SemaphoreType.DMA((n * _NCOPY,)),
                   *[pltpu.HBM(a.shape, a.dtype) for a in arrays], jax.ShapeDtypeStruct((8, LANES), F32)),
        in_specs=(HBM,) * (2 * n),
        out_specs=(SEM, SEM) + (HBM,) * (2 * n) + (pl.BlockSpec(memory_space=pltpu.VMEM),),
        input_output_aliases={t: 2 + t for t in range(2 * n)},
        compiler_params=pltpu.CompilerParams(has_side_effects=_EFFECT),
    )(*[pltpu.with_memory_space_constraint(a, pltpu.HBM) for a in arrays])
    return (outs[0], outs[1], tuple(outs[2:2 + 2 * n])), outs[-1]


def _split_wait(name, handle, after, plan):
    send_sems, recv_sems, arrays = handle
    n = len(arrays) // 2

    def body(*refs):
        sends, lands = _plan_all(plan, refs[:2 * n] + refs[2 * n:2 * n + 2], n)
        for cp in sends:
            cp.wait_send()
        for cp in lands:
            cp.wait_recv()

    outs = pl.pallas_call(
        body, name=name, out_shape=tuple(pltpu.HBM(a.shape, a.dtype) for a in arrays),
        in_specs=(HBM,) * (2 * n) + (SEM, SEM, ANY), out_specs=(HBM,) * (2 * n),
        input_output_aliases={t: t for t in range(2 * n)},
        compiler_params=pltpu.CompilerParams(has_side_effects=_EFFECT),
    )(*arrays, send_sems, recv_sems, after)
    return outs[:n], outs[n:]


def _tie(x, token, name):
    def body(x_ref, t_ref, o_ref):
        pass

    return pl.pallas_call(
        body, name=name, in_specs=[ANY, pl.BlockSpec(memory_space=pltpu.VMEM)], out_specs=ANY,
        out_shape=jax.ShapeDtypeStruct(x.shape, x.dtype), input_output_aliases={0: 0},
    )(x, token)


def _swap_sibling(ps, name):
    n = len(ps)

    def body(*refs):
        x, y, c = _coords()
        cps = [pltpu.make_async_remote_copy(
            src_ref=refs[t], dst_ref=refs[n + t], send_sem=refs[2 * n].at[t], recv_sem=refs[2 * n + 1].at[t],
            device_id=(x, y, 1 - c), device_id_type=MESH) for t in range(n)]
        for cp in cps:
            cp.start()
        for cp in cps:
            cp.wait()

    return pl.pallas_call(
        body, name=name, in_specs=[ANY] * n, out_specs=[ANY] * n,
        out_shape=[jax.ShapeDtypeStruct(p.shape, p.dtype) for p in ps],
        scratch_shapes=[pltpu.SemaphoreType.DMA((n,)), pltpu.SemaphoreType.DMA((n,))],
        compiler_params=pltpu.CompilerParams(has_side_effects=True),
    )(*ps)


_ADD_BYTES = 7 << 19


def _add_tile(rows, cols):
    best = 32
    for t in range(32, rows + 1, 32):
        if rows % t == 0 and t * cols * 4 <= _ADD_BYTES:
            best = t
    return best


def _add_slabs(pack, land, me, name):
    rows, cols = pack.shape[1:]
    tr = _add_tile(rows, cols)

    def body(me_ref, p_ref, l_ref, o_ref):
        f = lambda r: r.astype(F32)
        o_ref[...] = ((f(p_ref[0]) + f(l_ref[0])) + f(l_ref[1])) + f(l_ref[2])

    return pl.pallas_call(
        body, name=name,
        grid_spec=pltpu.PrefetchScalarGridSpec(
            num_scalar_prefetch=1, grid=(rows // tr,),
            in_specs=[pl.BlockSpec((1, tr, cols), lambda i, me_ref: (me_ref[0], i, 0)),
                      pl.BlockSpec((_NCOPY, tr, cols), lambda i, me_ref: (0, i, 0))],
            out_specs=pl.BlockSpec((tr, cols), lambda i, me_ref: (i, 0))),
        out_shape=jax.ShapeDtypeStruct((rows, cols), F32),
        compiler_params=_params(("parallel",)),
    )(me, pack, land)


_STAGE_W = 1024


def _stage_rows(shapes):
    pieces, r = [], 0
    for i, (k, w) in enumerate(shapes):
        for a in range(k):
            for q in range(0, w, _STAGE_W):
                pieces.append((i, a, q, min(_STAGE_W, w - q), r))
                r += 1
    return pieces, -(-r // 8) * 8


def _gather8(parts, reduce, name):
    shapes = [p.shape for p in parts]
    pieces, rows = _stage_rows(shapes)
    n = len(parts)

    def body(*refs):
        ins, outs = refs[:n], refs[n:2 * n]
        stage, buf, res, send_sems, recv_sems = refs[2 * n:]
        x, y, c = _coords()
        me = 4 * x + 2 * y + c
        stage[...] = jnp.zeros_like(stage)
        for i, a, q, w, r in pieces:
            stage[r:r + 1, 0:w] = ins[i][a:a + 1, q:q + w]
        buf[pl.ds(me, 1)] = stage[...][None]
        cps, lands = [], []
        for k in range(1, NDEV):
            peer = (1 - x if k & 4 else x, 1 - y if k & 2 else y, 1 - c if k & 1 else c)

            def copy(slot):
                return pltpu.make_async_remote_copy(
                    src_ref=stage, dst_ref=buf.at[slot], send_sem=send_sems.at[k - 1],
                    recv_sem=recv_sems.at[k - 1], device_id=peer, device_id_type=MESH)

            cps.append(copy(me))
            lands.append(copy(4 * peer[0] + 2 * peer[1] + peer[2]))
        for cp in cps:
            cp.start()
        for cp, land in zip(cps, lands):
            land.wait_recv()
            cp.wait_send()
        if reduce:
            acc = buf[0]
            for d in range(1, NDEV):
                acc = acc + buf[d]
            res[...] = acc
            for i, a, q, w, r in pieces:
                outs[i][a:a + 1, q:q + w] = res[r:r + 1, 0:w]
        else:
            for i, a, q, w, r in pieces:
                for s in range(NCHIP):
                    outs[i][s, a:a + 1, q:q + w] = buf[2 * s, r:r + 1, 0:w]

    vm = pl.BlockSpec(memory_space=pltpu.VMEM)
    out_shapes = [jax.ShapeDtypeStruct(s if reduce else (NCHIP,) + s, F32) for s in shapes]
    return pl.pallas_call(
        body, name=name, in_specs=[vm] * n, out_specs=[vm] * n, out_shape=out_shapes,
        scratch_shapes=[pltpu.VMEM((rows, _STAGE_W), F32), pltpu.VMEM((NDEV, rows, _STAGE_W), F32),
                        pltpu.VMEM((rows, _STAGE_W), F32), pltpu.SemaphoreType.DMA((NDEV - 1,)),
                        pltpu.SemaphoreType.DMA((NDEV - 1,))],
        compiler_params=pltpu.CompilerParams(has_side_effects=True),
    )(*parts)


def _adamw_update(w_ref, g_ref, m_ref, v_ref, d_ref, mo_ref, vo_ref):
    c1 = 1.0 / (1.0 - ADAM_B1 ** ADAM_STEP)
    c2 = 1.0 / (1.0 - ADAM_B2 ** ADAM_STEP)
    gv = g_ref[...]
    mn = ADAM_B1 * m_ref[...] + (1.0 - ADAM_B1) * gv
    vn = ADAM_B2 * v_ref[...] + (1.0 - ADAM_B2) * (gv * gv)
    d_ref[...] = -ADAM_LR * ((mn * c1) / (jnp.sqrt(vn * c2) + ADAM_EPS) + ADAM_WD * w_ref[...])
    mo_ref[...] = mn
    vo_ref[...] = vn


def _adamw_small(ws, gs, ms, vs):
    n = len(ws)

    def body(*refs):
        for i in range(n):
            _adamw_update(*(refs[j * n + i] for j in range(7)))

    vm = pl.BlockSpec(memory_space=pltpu.VMEM)
    outs = pl.pallas_call(
        body, name="adamw_small", in_specs=[vm] * (4 * n), out_specs=[vm] * (3 * n),
        out_shape=[jax.ShapeDtypeStruct(w.shape, F32) for w in ws] * 3,
    )(*ws, *gs, *ms, *vs)
    return outs[:n], outs[n:2 * n], outs[2 * n:]


def _adamw(w, g_parts, m, v, name):
    rows, cols = w.shape
    tr = rows
    while tr * cols * 4 > (1 << 20) and tr % 16 == 0:
        tr //= 2

    def body(w_ref, ga_ref, gb_ref, m_ref, v_ref, g_ref, d_ref, mo_ref, vo_ref):
        g_ref[...] = ga_ref[...] + gb_ref[...]
        _adamw_update(w_ref, g_ref, m_ref, v_ref, d_ref, mo_ref, vo_ref)

    blk = pl.BlockSpec((tr, cols), lambda i: (i, 0))
    return pl.pallas_call(
        body, name=name, grid=(rows // tr,), in_specs=[blk] * 5, out_specs=[blk] * 4,
        out_shape=[jax.ShapeDtypeStruct((rows, cols), F32)] * 4, compiler_params=_params(("parallel",)),
    )(w, *g_parts, m, v)


def _by_chip(g, rr, cc, axis):
    if isinstance(g, tuple):
        n = NCHIP // len(g)
        return jnp.concatenate([h.reshape(rr, n, cc).transpose(1, 0, 2) for h in g], axis=0)
    return g.reshape(NCHIP, rr, cc) if axis == 0 else g.reshape(rr, NCHIP, cc).transpose(1, 0, 2)


_SMALL_REPL = ("norm_mix_w", "ssd_conv_b", "dt_bias", "a_log", "d_skip", "ssd_norm_w", "norm_ffn_w",
               "ffn_conv_b", "final_norm_w")
_SMALL_CONV = (("conv_a_w", 3, D), ("ssd_conv_w", 4, DX), ("ffn_conv_w", 3, FF))


def kernel(x, norm_mix_w, w_in, conv_a_w, w_a_out, ssd_conv_w, ssd_conv_b, dt_bias, a_log, d_skip, ssd_norm_w, w_s_out, w_o, norm_ffn_w, w_up, ffn_conv_w, ffn_conv_b, w_down, final_norm_w, loss_target, m_norm_mix_w, m_w_in, m_conv_a_w, m_w_a_out, m_ssd_conv_w, m_ssd_conv_b, m_dt_bias, m_a_log, m_d_skip, m_ssd_norm_w, m_w_s_out, m_w_o, m_norm_ffn_w, m_w_up, m_ffn_conv_w, m_ffn_conv_b, m_w_down, m_final_norm_w, v_norm_mix_w, v_w_in, v_conv_a_w, v_w_a_out, v_ssd_conv_w, v_ssd_conv_b, v_dt_bias, v_a_log, v_d_skip, v_ssd_norm_w, v_w_s_out, v_w_o, v_norm_ffn_w, v_w_up, v_ffn_conv_w, v_ffn_conv_b, v_w_down, v_final_norm_w):
    names = ("norm_mix_w", "w_in", "conv_a_w", "w_a_out", "ssd_conv_w", "ssd_conv_b", "dt_bias", "a_log", "d_skip",
             "ssd_norm_w", "w_s_out", "w_o", "norm_ffn_w", "w_up", "ffn_conv_w", "ffn_conv_b", "w_down", "final_norm_w")
    W = dict(zip(names, (norm_mix_w, w_in, conv_a_w, w_a_out, ssd_conv_w, ssd_conv_b, dt_bias, a_log, d_skip,
                         ssd_norm_w, w_s_out, w_o, norm_ffn_w, w_up, ffn_conv_w, ffn_conv_b, w_down, final_norm_w)))
    M = dict(zip(names, (m_norm_mix_w, m_w_in, m_conv_a_w, m_w_a_out, m_ssd_conv_w, m_ssd_conv_b, m_dt_bias, m_a_log,
                         m_d_skip, m_ssd_norm_w, m_w_s_out, m_w_o, m_norm_ffn_w, m_w_up, m_ffn_conv_w, m_ffn_conv_b,
                         m_w_down, m_final_norm_w)))
    V = dict(zip(names, (v_norm_mix_w, v_w_in, v_conv_a_w, v_w_a_out, v_ssd_conv_w, v_ssd_conv_b, v_dt_bias, v_a_log,
                         v_d_skip, v_ssd_norm_w, v_w_s_out, v_w_o, v_norm_ffn_w, v_w_up, v_ffn_conv_w, v_ffn_conv_b,
                         v_w_down, v_final_norm_w)))
    two_d = lambda a: a.reshape(-1, a.shape[-1])
    W2, M2, V2 = ({k: two_d(a) for k, a in t.items()} for t in (W, M, V))
    xi, yi, ci = _coords()
    me = 2 * xi + yi

    meidx = me.reshape(1).astype(jnp.int32)
    state = {}


    class Hooks(_Hooks):
        def before_in_proj(self, w_in):
            return _tie(w_in, state["rest_token"], "tie_ag_rest")

        def late_weights(self, wts, after):
            owns, lands = _split_wait("ag_rest_wait", state["rest"], after, _plan_bcast)
            full = {}
            for (n, rr, cc, axis), own, land in zip(_W_REST, owns, lands):
                slabs = lax.dynamic_update_slice(land, own[None], (me, 0, 0))
                full[n] = slabs.reshape(NCHIP * rr, cc) if axis == 0 else slabs.transpose(1, 0, 2).reshape(rr, NCHIP * cc)
            return {**wts, **full}

        def grads_ready(self, grads, tie):
            if "w_in" in grads:
                key, packs = "g_in", [_unpermute_w_in(grads["w_in"]).astype(BF16)]
            else:
                key = "g_rest"
                packs = [_by_chip(jax.tree.map(lambda t: t.astype(BF16), grads[n]), rr, cc, axis)
                         for n, rr, cc, axis in _W_REST]
            lands = [lax.empty((_NCOPY,) + p.shape[1:], BF16) for p in packs]
            state[key], token = _split_start("rs_" + key + "_start", packs, lands, _plan_scatter)
            return _tie(tie, token, "tie_" + key)

        def mark(self, name, value):
            state[name] = value

    def reduced(key, after, group):
        packs, lands = _split_wait("rs_" + key + "_wait", state[key], after, _plan_scatter)
        mines = [_add_slabs(p, l, meidx, "rs_add_chips_" + n) for (n, *_), p, l in zip(group, packs, lands)]
        return dict(zip([n for n, *_ in group], zip(mines, _swap_sibling(mines, "rs_" + key + "_swap"))))

    w_in_slabs = _ag_weights(W2["w_in"].astype(BF16))
    wts = {k: W2[k] for k in _SMALL_REPL}
    conv_by_chip = _gather8([W2[n] for n, *_ in _SMALL_CONV], False, "ag_conv_weights")
    for (n, kk, width), stacked in zip(_SMALL_CONV, conv_by_chip):
        wts[n] = stacked.transpose(1, 0, 2).reshape(kk, width)
    rest = [W2[n].astype(BF16) for n, *_ in _W_REST]
    rest[0] = _tie(rest[0], conv_by_chip[0], "tie_ag_order")
    state["rest"], state["rest_token"] = _split_start(
        "ag_rest_start", rest, [lax.empty((NCHIP,) + r.shape, BF16) for r in rest], _plan_bcast)
    wts["w_in"] = _permute_w_in(w_in_slabs)

    loss8, grad_x, grads = _local_step(x[0], loss_target[0], wts, Hooks())

    gbig = {**reduced("g_rest", state["ssd_bwd"], _W_REST), **reduced("g_in", grad_x, _W_IN)}

    small_parts = [grads[n] for n in _SMALL_REPL] + [loss8[0:1]] + [grads[n] for n, *_ in _SMALL_CONV]
    small_g = _gather8(small_parts, True, "allreduce_small")
    gsm = dict(zip(_SMALL_REPL, small_g[:len(_SMALL_REPL)]))
    loss = small_g[len(_SMALL_REPL)][0, 0]
    for (n, kk, width), gfull in zip(_SMALL_CONV, small_g[len(_SMALL_REPL) + 1:]):
        cw = width // NCHIP
        gsm[n] = lax.dynamic_slice(gfull, (0, me * cw), (kk, cw))

    G, DW, NM, NV = {}, {}, {}, {}
    for n in [b[0] for b in _W_IN + _W_REST]:
        G[n], DW[n], NM[n], NV[n] = _adamw(W2[n], gbig[n], M2[n], V2[n], "adamw_" + n)
    sm_names = list(_SMALL_REPL) + [n for n, *_ in _SMALL_CONV]
    outs = _adamw_small(*([t[n] for n in sm_names] for t in (W2, gsm, M2, V2)))
    for t, vals in zip((DW, NM, NV), outs):
        t.update(zip(sm_names, vals))
    G.update(gsm)

    def shaped(t):
        return [t[n].reshape(W[n].shape) for n in names]

    return (loss, grad_x.reshape(x.shape), *shaped(G), *shaped(DW), *shaped(NM), *shaped(NV))
```

```python
import jax
import jax.numpy as jnp
from jax import lax
from jax.experimental import pallas as pl
from jax.experimental.pallas import tpu as pltpu

F32 = jnp.float32
BF16 = jnp.bfloat16

D = 1024
DI = 2048
NH = 32
HP = 64
NG = 4
NS = 128
CH = 128
DX = 3072
FF = 2816
NI = 10272
EPS = 1e-5

OFF_BCV, OFF_XBC, OFF_G, OFF_Z, OFF_DT = 0, 3072, 6144, 8192, 10240
NIP = 10752
_SEGS = ((0, 2048, OFF_G), (2048, 3072, OFF_BCV), (5120, 2048, OFF_Z), (7168, 3072, OFF_XBC), (10240, 32, OFF_DT))

LANES = 128
HALO = 16
V7X_VMEM_LIMIT = 56 * 2 ** 20

ADAM_LR, ADAM_B1, ADAM_B2, ADAM_EPS, ADAM_WD, ADAM_STEP = 0.001, 0.9, 0.999, 1e-08, 0.01, 10

NN = (((1,), (0,)), ((), ()))
NT = (((1,), (1,)), ((), ()))
TN = (((0,), (0,)), ((), ()))


def _dot(a, b, dims=NN):
    return lax.dot_general(a, b, dims, preferred_element_type=F32)


def _params(sem, **kw):
    return pltpu.CompilerParams(dimension_semantics=sem, vmem_limit_bytes=V7X_VMEM_LIMIT, **kw)


V7X_MXU = 256
V7X_HBM_BYTES_PER_S = 3.5e12
STEP_S = 0.35e-6
MATMUL_VMEM = 40 * 2 ** 20
EPILOGUE_VMEM = 46 * 2 ** 20


ACC_BYTES_PER_S = 1.2e13


def _divisors(dim, cap, units):
    for unit in units:
        c = [t for t in range(unit, min(dim, cap) + 1, unit) if dim % t == 0]
        if c:
            return c
    return [dim]


def _tiles(M, N, K, out_bytes, has_res):
    best = None
    for tn in _divisors(N, 2816, (V7X_MXU, LANES)):
        for tm in _divisors(M, 2816, (LANES,)):
            for tk in _divisors(K, 2816, (V7X_MXU, LANES)):
                nk, ni, nj = K // tk, M // tm, N // tn
                vmem = 4 * (tm * tk + tk * tn) + 2 * tm * tn * out_bytes
                vmem += (4 * tm * tn if nk > 1 else 0) + (8 * tm * tn if has_res else 0)
                if vmem > MATMUL_VMEM:
                    continue
                a_reads = M * K * 2 * (nj if nk > 1 else 1)
                b_reads = K * N * 2 * (ni if nk * nj > 1 else 1)
                cost = (a_reads + b_reads + M * N * out_bytes) / V7X_HBM_BYTES_PER_S + ni * nj * nk * STEP_S
                cost += (nk - 1) * M * N * 8 / ACC_BYTES_PER_S
                if best is None or cost < best[0]:
                    best = (cost, tm, tn, tk)
    assert best is not None, (M, N, K)
    return best[1:]


def _sigmoid(x):
    return 1.0 / (1.0 + jnp.exp(-x))


class _Epilogue:
    def __init__(self, fn, ins, outs, tile_bytes, in_windows=None, out_windows=None):
        self.fn, self.ins, self.outs, self.tile_bytes = fn, tuple(ins), tuple(outs), tile_bytes
        self.in_windows, self.out_windows = in_windows or {}, out_windows or {}


def _matmul(a, b, *, mode, out_dtype, name, residual=None, b_k_off=0, epilogue=None):
    if mode == "nn":
        (M, K), (K2, N) = a.shape, b.shape
    elif mode == "nt":
        (M, K), (N, K2) = a.shape, (b.shape[0], a.shape[1])
        assert b_k_off + K <= b.shape[1]
    else:
        (K, M), (K2, N) = a.shape, b.shape
    assert K == K2, (name, a.shape, b.shape)
    tm, tn, tk = _tiles(M, N, K, jnp.dtype(out_dtype).itemsize, residual is not None)
    if epilogue is not None:
        tn = N
        fits = [(K * N * 2 * (M // t) / V7X_HBM_BYTES_PER_S + (K // q - 1) * M * N * 8 / ACC_BYTES_PER_S
                 + (M // t) * (K // q) * STEP_S, t, q)
                for t in (1024, 512, 256) if M % t == 0 for q in _divisors(K, 2816, (V7X_MXU, LANES))
                if 4 * (t * q + q * tn) + (4 * t * tn if K > q else 0) + (8 * t * tn if residual is not None else 0)
                + 2 * t * epilogue.tile_bytes <= EPILOGUE_VMEM]
        _, tm, tk = min(fits)
    nk = K // tk
    if mode == "tn":
        a_spec = pl.BlockSpec((tk, tm), lambda i, j, k: (k, i))
    else:
        a_spec = pl.BlockSpec((tm, tk), lambda i, j, k: (i, k))
    if mode == "nt":
        assert b_k_off % tk == 0
        b_spec = pl.BlockSpec((tn, tk), lambda i, j, k: (j, k + b_k_off // tk))
    else:
        b_spec = pl.BlockSpec((tk, tn), lambda i, j, k: (k, j))
    dims = {"nn": NN, "nt": NT, "tn": TN}[mode]
    o_spec = pl.BlockSpec((tm, tn), lambda i, j, k: (i, j))
    has_res = residual is not None

    def rows_or_whole(shape, window=None):
        if window is not None:
            off, width = window
            return pl.BlockSpec((tm, width), lambda i, j, k: (i, off // width))
        if shape[0] == M:
            return pl.BlockSpec((tm,) + tuple(shape[1:]), lambda i, j, k: (i,) + (0,) * (len(shape) - 1))
        return pl.BlockSpec(tuple(shape), lambda i, j, k: (0,) * len(shape))

    n_in = 2 + has_res + (len(epilogue.ins) if epilogue else 0)
    n_out = len(epilogue.outs) if epilogue else 1

    def body(*refs):
        a_ref, b_ref = refs[:2]
        r_ref = refs[2] if has_res else None
        out_refs = refs[n_in:n_in + n_out]
        acc_ref = refs[-1]
        k = pl.program_id(2)
        part = _dot(a_ref[...], b_ref[...], dims)

        def finish(r):
            if has_res:
                r = r + r_ref[...].astype(F32)
            if epilogue is None:
                out_refs[0][...] = r.astype(out_dtype)
            else:
                epilogue.fn(r, refs[2 + has_res:n_in], out_refs, pl.program_id(0) == 0)

        if nk == 1:
            finish(part)
            return

        @pl.when(k == 0)
        def _():
            acc_ref[...] = part

        @pl.when(jnp.logical_and(k > 0, k < nk - 1))
        def _():
            acc_ref[...] += part

        @pl.when(k == nk - 1)
        def _():
            finish(acc_ref[...] + part)

    in_specs = [a_spec, b_spec] + ([o_spec] if has_res else [])
    args = (a, b) + ((residual,) if has_res else ())
    if epilogue is None:
        out_specs, out_shape = o_spec, jax.ShapeDtypeStruct((M, N), out_dtype)
        sem = ("parallel", "parallel", "arbitrary")
    else:
        in_specs += [rows_or_whole(x.shape, epilogue.in_windows.get(n)) for n, x in enumerate(epilogue.ins)]
        args += epilogue.ins
        out_specs = [rows_or_whole(o[0], epilogue.out_windows.get(n)) for n, o in enumerate(epilogue.outs)]
        out_shape = [jax.ShapeDtypeStruct(shp, dt) for shp, dt in epilogue.outs]
        sem = ("arbitrary", "arbitrary", "arbitrary")
    return pl.pallas_call(
        body, name=name, grid=(M // tm, N // tn, nk), in_specs=in_specs, out_specs=out_specs,
        out_shape=out_shape, scratch_shapes=[pltpu.VMEM((tm, tn), F32)] if nk > 1 else [],
        compiler_params=_params(sem),
    )(*args)


class _Rows:
    def __init__(self, T, tm):
        self.T, self.tm = T, min(tm, T // 2)
        self.nrow = T // self.tm
        self.r = self.tm // HALO
        self.nb = T // HALO

    def tile(self, w, cb=0, step=1):
        return pl.BlockSpec((self.tm, w), lambda j, i: (i, cb + step * j))

    def prev(self, w, cb=0, step=1):
        r = self.r
        return pl.BlockSpec((HALO, w), lambda j, i: (jnp.maximum(i * r - 1, 0), cb + step * j))

    def next(self, w, cb=0, step=1):
        r, nb = self.r, self.nb
        return pl.BlockSpec((HALO, w), lambda j, i: (jnp.minimum((i + 1) * r, nb - 1), cb + step * j))

    def colvec(self, k, w, cb=0, step=1):
        return pl.BlockSpec((k, w), lambda j, i: (0, cb + step * j))

    def call(self, body, name, ncol, in_specs, out_specs, out_shape, args, aliases=None):
        return pl.pallas_call(
            body, name=name, grid=(ncol, self.nrow), in_specs=in_specs, out_specs=out_specs,
            out_shape=out_shape, input_output_aliases=aliases or {},
            compiler_params=_params(("parallel", "arbitrary")),
        )(*args)


ANY = pl.BlockSpec(memory_space=pl.ANY)


def _shifts_causal(ext, nk, tm):
    out = []
    for k in range(nk):
        s = nk - 1 - k
        r = ext if s == 0 else pltpu.roll(ext, s, 0)
        out.append(r[HALO:])
    return out


def _shifts_anticausal(ext, nk, tm):
    n = ext.shape[0]
    out = []
    for k in range(nk):
        s = nk - 1 - k
        r = ext if s == 0 else pltpu.roll(ext, n - s, 0)
        out.append(r[:tm])
    return out


def _wsum(w, parts):
    acc = w[0:1, :] * parts[0]
    for k in range(1, len(parts)):
        acc = acc + w[k:k + 1, :] * parts[k]
    return acc


def _colsum(x):
    return jnp.sum(x, axis=0, keepdims=True)


def _acc_out(ref, val, first):
    @pl.when(first)
    def _():
        ref[...] = val

    @pl.when(jnp.logical_not(first))
    def _():
        ref[...] += val


def _acc_rows(ref, rows, first):
    for k, r in enumerate(rows):
        _acc_out(ref.at[k:k + 1, :], r, first)


def _norm_matmul(x, wn, b, name, b_f32=None):
    T, N = x.shape[0], b.shape[1]
    tm = min(1024, T)
    tn = max(t for t in _divisors(N, 2816, (V7X_MXU, LANES))
             if 8 * tm * D + 6 * tm * D + 4 * D * t + 4 * tm * t <= MATMUL_VMEM)

    extra = b_f32 is not None

    def body(*refs):
        x_ref, wn_ref, b_ref = refs[:3]
        o_ref, u_ref = refs[3 + extra:5 + extra]
        keep_ref = refs[-1]

        @pl.when(pl.program_id(1) == 0)
        def _():
            xv = x_ref[...]
            r = lax.rsqrt(jnp.mean(xv * xv, axis=-1, keepdims=True) + EPS)
            u = (xv * r * wn_ref[...]).astype(BF16)
            keep_ref[...] = u
            u_ref[...] = u
            if extra:
                refs[5 + extra][...] = _dot(u, refs[3][...])

        o_ref[...] = _dot(keep_ref[...], b_ref[...]).astype(BF16)

    rows = pl.BlockSpec((tm, D), lambda i, j: (i, 0))
    whole = lambda shape: pl.BlockSpec(shape, lambda i, j: (0, 0))
    narrow = pl.BlockSpec((tm, LANES), lambda i, j: (i, 0))
    return pl.pallas_call(
        body, name=name, grid=(T // tm, N // tn),
        in_specs=[rows, whole((1, D)), pl.BlockSpec((D, tn), lambda i, j: (0, j))] + [whole((D, LANES))] * extra,
        out_specs=[pl.BlockSpec((tm, tn), lambda i, j: (i, j)), rows] + [narrow] * extra,
        out_shape=[jax.ShapeDtypeStruct((T, N), BF16), jax.ShapeDtypeStruct((T, D), BF16)]
        + [jax.ShapeDtypeStruct((T, LANES), F32)] * extra,
        scratch_shapes=[pltpu.VMEM((tm, D), BF16)],
        compiler_params=_params(("parallel", "arbitrary")),
    )(*((x, wn, b) + ((b_f32,) if extra else ())))


def _rmsnorm_bwd_epilogue(x, w, dres):
    T = x.shape[0]

    def fn(dyv, ins, outs, first):
        x_ref, w_ref, dr_ref = ins
        dx_ref, dxb_ref, dw_ref = outs
        xv = x_ref[...]
        r = lax.rsqrt(jnp.mean(xv * xv, axis=-1, keepdims=True) + EPS)
        xh = xv * r
        dxh = dyv * w_ref[...]
        dx = r * (dxh - xh * jnp.mean(dxh * xh, axis=-1, keepdims=True)) + dr_ref[...]
        dx_ref[...] = dx
        dxb_ref[...] = dx.astype(BF16)
        _acc_out(dw_ref, _colsum(dyv * xh), first)

    return _Epilogue(fn, (x, w, dres), (((T, D), F32), ((T, D), BF16), ((1, D), F32)), 14 * D)


def _branch_a_fwd(proj, conv_w):
    T = proj.shape[0]
    R = _Rows(T, 512)
    tm = R.tm

    def body(p_ref, pp_ref, w_ref, o_ref):
        keep = (pl.program_id(1) > 0).astype(F32)
        cv = p_ref[:, D:2 * D].astype(F32) * p_ref[:, 2 * D:].astype(F32)
        cvp = pp_ref[:, D:2 * D].astype(F32) * pp_ref[:, 2 * D:].astype(F32) * keep
        sh = _shifts_causal(jnp.concatenate([cvp, cv], axis=0), 3, tm)
        ca = _wsum(w_ref[...], sh)
        o_ref[...] = (p_ref[:, :D].astype(F32) * ca).astype(BF16)

    return R.call(body, "branch_a_fwd", 1, [R.tile(3 * D), R.prev(3 * D), R.colvec(3, D)], R.tile(D),
                  jax.ShapeDtypeStruct((T, D), BF16), (proj, proj, conv_w))


def _branch_a_bwd(dya_in, proj, conv_w, dproj):
    T = proj.shape[0]
    R = _Rows(T, 256)
    tm = R.tm

    def body(d_ref, dn_ref, p_ref, pp_ref, pn_ref, w_ref, _alias, o_ref, dw_ref):
        i = pl.program_id(1)
        keep_p = (i > 0).astype(F32)
        keep_n = (i < R.nrow - 1).astype(F32)
        w = w_ref[...]
        b = p_ref[:, :D].astype(F32)
        c = p_ref[:, D:2 * D].astype(F32)
        v = p_ref[:, 2 * D:].astype(F32)
        cvp = pp_ref[:, D:2 * D].astype(F32) * pp_ref[:, 2 * D:].astype(F32) * keep_p
        sh = _shifts_causal(jnp.concatenate([cvp, c * v], axis=0), 3, tm)
        ca = _wsum(w, sh)
        d = d_ref[...].astype(F32)
        dca = d * b
        dca_n = dn_ref[...].astype(F32) * pn_ref[:, :D].astype(F32) * keep_n
        dsh = _shifts_anticausal(jnp.concatenate([dca, dca_n], axis=0), 3, tm)
        dcv = _wsum(w, dsh)
        o_ref[:, :D] = (d * ca).astype(BF16)
        o_ref[:, D:2 * D] = (dcv * v).astype(BF16)
        o_ref[:, 2 * D:] = (dcv * c).astype(BF16)
        _acc_rows(dw_ref, [_colsum(dca * s) for s in sh], i == 0)

    return R.call(
        body, "branch_a_bwd", 1,
        [R.tile(D), R.next(D), R.tile(3 * D), R.prev(3 * D), R.next(3 * D), R.colvec(3, D), ANY],
        [R.tile(3 * D), R.colvec(3, D)],
        [jax.ShapeDtypeStruct(dproj.shape, BF16), jax.ShapeDtypeStruct((3, D), F32)],
        (dya_in, dya_in, proj, proj, proj, conv_w, dproj), aliases={6: 0})


_XW = 512


def _xbc_fwd(proj, conv_w, conv_b):
    T = proj.shape[0]
    R = _Rows(T, 512)
    tm = R.tm
    cb = OFF_XBC // _XW

    def body(x_ref, xp_ref, w_ref, b_ref, o_ref):
        keep = (pl.program_id(1) > 0).astype(F32)
        ext = jnp.concatenate([xp_ref[...].astype(F32) * keep, x_ref[...].astype(F32)], axis=0)
        pre = _wsum(w_ref[...], _shifts_causal(ext, 4, tm)) + b_ref[...]
        o_ref[...] = (pre * _sigmoid(pre)).astype(BF16)

    return R.call(body, "xbc_fwd", DX // _XW,
                  [R.tile(_XW, cb), R.prev(_XW, cb), R.colvec(4, _XW), R.colvec(1, _XW)], R.tile(_XW),
                  jax.ShapeDtypeStruct((T, DX), BF16), (proj, proj, conv_w, conv_b))


def _xbc_bwd(dact, proj, conv_w, conv_b, dproj):
    T = proj.shape[0]
    R = _Rows(T, 512)
    tm = R.tm
    cb = OFF_XBC // _XW

    def body(d_ref, dn_ref, x_ref, xp_ref, xn_ref, w_ref, b_ref, _alias, o_ref, dw_ref, db_ref):
        i = pl.program_id(1)
        keep_p = (i > 0).astype(F32)
        keep_n = (i < R.nrow - 1).astype(F32)
        w = w_ref[...]
        ext = jnp.concatenate([xp_ref[...].astype(F32) * keep_p, x_ref[...].astype(F32),
                               xn_ref[...].astype(F32)], axis=0)
        sh = _shifts_causal(ext, 4, tm + HALO)
        pre = _wsum(w, sh) + b_ref[...]
        s = _sigmoid(pre)
        dsilu = s * (1.0 + pre * (1.0 - s))
        dext = jnp.concatenate([d_ref[...].astype(F32), dn_ref[...].astype(F32) * keep_n], axis=0)
        dpre = dext * dsilu
        dsh = _shifts_anticausal(dpre, 4, tm)
        o_ref[...] = _wsum(w, dsh).astype(BF16)
        dp = dpre[:tm]
        _acc_rows(dw_ref, [_colsum(dp * q[:tm]) for q in sh], i == 0)
        _acc_out(db_ref, _colsum(dp), i == 0)

    return R.call(
        body, "xbc_bwd", DX // _XW,
        [R.tile(_XW), R.next(_XW), R.tile(_XW, cb), R.prev(_XW, cb), R.next(_XW, cb),
         R.colvec(4, _XW), R.colvec(1, _XW), ANY],
        [R.tile(_XW, cb), R.colvec(4, _XW), R.colvec(1, _XW)],
        [jax.ShapeDtypeStruct(dproj.shape, BF16), jax.ShapeDtypeStruct((4, DX), F32),
         jax.ShapeDtypeStruct((1, DX), F32)],
        (dact, dact, proj, proj, proj, conv_w, conv_b, dproj), aliases={7: 0})


def _softplus(x):
    return jnp.maximum(x, 0.0) + jnp.log(1.0 + jnp.exp(-jnp.abs(x)))


def _dt_rows(T):
    return min(8 * CH, T // 2)


def _dt_fwd(dt_raw, dt_bias_p, a_log_p):
    T = dt_raw.shape[0]
    rows = _dt_rows(T)

    def body(r_ref, b_ref, al_ref, dt_ref, ac_ref, acT_ref):
        dt = _softplus(r_ref[...] + b_ref[...])
        s = dt * (-jnp.exp(al_ref[...]))
        row = lax.broadcasted_iota(jnp.int32, (rows, LANES), 0) % CH
        k = 1
        while k < CH:
            s = s + jnp.where(row >= k, pltpu.roll(s, k, 0), 0.0)
            k *= 2
        dt_ref[...] = dt
        ac_ref[...] = s
        for q in range(0, rows, CH):
            acT_ref[q:q + CH] = s[q:q + CH].T

    blk = pl.BlockSpec((rows, LANES), lambda i: (i, 0))
    vec = pl.BlockSpec((1, LANES), lambda i: (0, 0))
    return pl.pallas_call(
        body, name="dt_fwd", grid=(T // rows,), in_specs=[blk, vec, vec], out_specs=[blk, blk, blk],
        out_shape=[jax.ShapeDtypeStruct((T, LANES), F32)] * 3, compiler_params=_params(("parallel",)),
    )(dt_raw, dt_bias_p, a_log_p)


def _dt_bwd(dacum, ddt_x, dt_raw, dt_bias_p, a_log_p, dproj):
    T = dt_raw.shape[0]
    rows = _dt_rows(T)
    nc = T // rows

    def body(da_ref, dx_ref, r_ref, b_ref, al_ref, _alias, o_ref, db_ref, dal_ref):
        i = pl.program_id(0)
        a = -jnp.exp(al_ref[...])
        z = r_ref[...] + b_ref[...]
        dt = _softplus(z)
        s = da_ref[...]
        row = lax.broadcasted_iota(jnp.int32, (rows, LANES), 0) % CH
        k = 1
        while k < CH:
            s = s + jnp.where(row < CH - k, pltpu.roll(s, rows - k, 0), 0.0)
            k *= 2
        ddt = s * a + dx_ref[...]
        draw = ddt * _sigmoid(z)
        o_ref[:, :LANES] = draw.astype(BF16)
        o_ref[:, LANES:] = jnp.zeros((rows, NIP - OFF_DT - LANES), BF16)
        _acc_out(db_ref, _colsum(draw), i == 0)
        _acc_out(dal_ref, _colsum(s * dt), i == 0)

        @pl.when(i == nc - 1)
        def _():
            dal_ref[...] = dal_ref[...] * a

    blk = pl.BlockSpec((rows, LANES), lambda i: (i, 0))
    vec = pl.BlockSpec((1, LANES), lambda i: (0, 0))
    oblk = pl.BlockSpec((rows, NIP - OFF_DT), lambda i: (i, OFF_DT // (NIP - OFF_DT)))
    return pl.pallas_call(
        body, name="dt_bwd", grid=(nc,), in_specs=[blk, blk, blk, vec, vec, ANY], out_specs=[oblk, vec, vec],
        out_shape=[jax.ShapeDtypeStruct(dproj.shape, BF16), jax.ShapeDtypeStruct((1, LANES), F32),
                   jax.ShapeDtypeStruct((1, LANES), F32)],
        input_output_aliases={5: 0}, compiler_params=_params(("arbitrary",)),
    )(dacum, ddt_x, dt_raw, dt_bias_p, a_log_p, dproj)


_GW = DI // NG
_HG = NH // NG
_NEG = -1e30


def _interleave(gens):
    out, live = [None] * len(gens), list(range(len(gens)))
    while live:
        for i in list(live):
            try:
                next(gens[i])
            except StopIteration as stop:
                out[i] = stop.value
                live.remove(i)
    return out


def _pair_lanes(left, v0, v1):
    return jnp.where(left, v0, v1)


def _ssd_specs(T, rev):
    nc = T // CH
    cm = (lambda c: nc - 1 - c) if rev else (lambda c: c)
    bw = NG * NS
    return dict(
        xs=pl.BlockSpec((CH, DI), lambda c: (cm(c), 0)),
        bm=pl.BlockSpec((CH, bw), lambda c: (cm(c), DI // bw)),
        cmat=pl.BlockSpec((CH, bw), lambda c: (cm(c), DI // bw + 1)),
        xbc=pl.BlockSpec((CH, DX), lambda c: (cm(c), 0)),
        col=pl.BlockSpec((CH, LANES), lambda c: (cm(c), 0)),
        dsk=pl.BlockSpec((1, DI), lambda c: (0, 0)),
        state=pl.BlockSpec((1, NS, DI), lambda c: (cm(c), 0, 0)),
    )


def _last(ref, lo, hi):
    return ref.at[(slice(None),) * (len(ref.shape) - 1) + (slice(lo, hi),)]


def _group_views(g, wide, narrow):
    return [_last(r, g * _GW, (g + 1) * _GW) for r in wide] + [_last(r, g * NS, (g + 1) * NS) for r in narrow]


def _ssd_fwd(xact, dt, acum, acumT, dsk_rep, proj, norm_w):
    T = xact.shape[0]
    nc = T // CH
    sp = _ssd_specs(T, False)

    def body(*refs):
        xs, bm, cmat, dtr, acr, actr, dsk, zr, nw, y, yn, spv, S_ref = refs

        @pl.when(pl.program_id(0) == 0)
        def _():
            S_ref[...] = jnp.zeros_like(S_ref)

        _interleave([group(g * _HG, dtr[...], acr[...], actr[...],
                           *_group_views(g, (xs, dsk, zr, nw, y, yn, spv, S_ref), (bm, cmat))) for g in range(NG)])

    def group(hb, dt, ac, acT, xs_ref, dsk_ref, z_ref, nw_ref, y_ref, yn_ref, sp_ref, S_ref, b_ref, c_ref):
        Bm, Cm = b_ref[...], c_ref[...]
        S = S_ref[...]
        sp_ref[0] = S
        cb = _dot(Cm, Bm, NT)
        CS = _dot(Cm, S.astype(BF16))
        row = lax.broadcasted_iota(jnp.int32, (CH, CH), 0)
        col = lax.broadcasted_iota(jnp.int32, (CH, CH), 1)
        tril = row >= col
        left = col < HP
        xd_parts, dec_parts = [], []
        for p in range(_HG // 2):
            sl = slice(p * LANES, (p + 1) * LANES)
            j0, j1 = hb + 2 * p, hb + 2 * p + 1
            xp = xs_ref[:, sl].astype(F32)
            a0, a1 = ac[:, j0:j0 + 1], ac[:, j1:j1 + 1]
            al0, al1 = ac[CH - 1:CH, j0:j0 + 1], ac[CH - 1:CH, j1:j1 + 1]
            X = xp * _pair_lanes(left, dt[:, j0:j0 + 1], dt[:, j1:j1 + 1])
            Xb = X.astype(BF16)
            Ws = [(cb * jnp.exp(jnp.where(tril, aj - acT[j:j + 1, :], _NEG))).astype(BF16)
                  for j, aj in ((j0, a0), (j1, a1))]
            Xs = [jnp.where(m, Xb, jnp.zeros_like(Xb)) for m in (left, jnp.logical_not(left))]
            yield
            yd = _dot(jnp.concatenate(Ws, axis=1), jnp.concatenate(Xs, axis=0))
            yield
            eal = _pair_lanes(left, jnp.exp(a0), jnp.exp(a1))
            y = yd + eal * CS[:, sl] + dsk_ref[:, sl] * xp
            y_ref[:, sl] = y.astype(BF16)
            xd_parts.append(X * _pair_lanes(left, jnp.exp(al0 - a0), jnp.exp(al1 - a1)))
            dec_parts.append(_pair_lanes(left[0:1], jnp.exp(al0), jnp.exp(al1)))
        Xd = jnp.concatenate(xd_parts, axis=1).astype(BF16)
        dec = jnp.concatenate(dec_parts, axis=1)
        S_ref[...] = dec * S + _dot(Bm, Xd, TN)
        yield
        z = z_ref[...].astype(F32)
        yf = y_ref[...].astype(F32) * z * _sigmoid(z)
        r = lax.rsqrt(jnp.mean(yf * yf, axis=-1, keepdims=True) + EPS)
        yn_ref[...] = (yf * r * nw_ref[...]).astype(BF16)

    zspec = pl.BlockSpec((CH, DI), lambda c: (c, OFF_Z // DI))
    return pl.pallas_call(
        body, name="ssd_fwd", grid=(nc,),
        in_specs=[sp["xs"], sp["bm"], sp["cmat"], sp["col"], sp["col"], sp["col"], sp["dsk"], zspec, sp["dsk"]],
        out_specs=[sp["xs"], sp["xs"], sp["state"]],
        out_shape=[jax.ShapeDtypeStruct((T, DI), BF16), jax.ShapeDtypeStruct((T, DI), BF16),
                   jax.ShapeDtypeStruct((nc, NS, DI), F32)],
        scratch_shapes=[pltpu.VMEM((NS, DI), F32)],
        compiler_params=_params(("arbitrary",)),
    )(xact, xact, xact, dt, acum, acumT, dsk_rep, proj, norm_w)


def _ssd_bwd(dn, y, proj, norm_w, dproj, xact, dt, acum, acumT, dsk_rep, sprev):
    T = xact.shape[0]
    nc = T // CH
    sp = _ssd_specs(T, True)

    def body(*refs):
        (xs, bm, cmat, dtr, acr, actr, dsk, dnr, yr, zr, nw, spv, _alias, lanes_of_ref, rows_of_ref,
         dxa, ddtx, dAc, dskacc, dzr, dnw, dS_ref) = refs
        first = pl.program_id(0) == 0

        @pl.when(first)
        def _():
            dS_ref[...] = jnp.zeros_like(dS_ref)

        dbc = _last(dxa, DI, DX)
        ddtx_sum = jnp.zeros((CH, LANES), F32)
        dAc_sum = jnp.zeros((CH, LANES), F32)
        for a, b in _interleave([group(first, g * _HG, dtr[...], acr[...], actr[...],
                                       lanes_of_ref.at[g * _GW:(g + 1) * _GW],
                                       rows_of_ref.at[g * _HG * CH:(g + 1) * _HG * CH],
                                       *_group_views(g, (xs, dsk, dnr, yr, zr, nw, dzr, dnw, spv, dxa, dskacc, dS_ref),
                                                     (bm, cmat, dbc, _last(dbc, NG * NS, 2 * NG * NS))))
                                 for g in range(NG)]):
            ddtx_sum, dAc_sum = ddtx_sum + a, dAc_sum + b
        ddtx[...] = ddtx_sum
        dAc[...] = dAc_sum

    def group(first, hb, dt, ac, acT, lanes_of_ref, rows_of_ref, xs_ref, dsk_ref, dn_ref, y_ref, z_ref, nw_ref, dz_ref, dnw_ref, sp_ref, dx_ref,
              dskacc_ref, dS_ref, b_ref, c_ref, dB_ref, dC_ref):
        z = z_ref[...].astype(F32)
        yv = y_ref[...].astype(F32)
        sg = _sigmoid(z)
        silu = z * sg
        yf = yv * silu
        rn = lax.rsqrt(jnp.mean(yf * yf, axis=-1, keepdims=True) + EPS)
        yh = yf * rn
        dnv = dn_ref[...].astype(F32)
        dyh = dnv * nw_ref[...]
        dyf = rn * (dyh - yh * jnp.mean(dyh * yh, axis=-1, keepdims=True))
        dyg = dyf * silu
        dz_ref[...] = (dyf * yv * sg * (1.0 + z * (1.0 - sg))).astype(BF16)
        _acc_out(dnw_ref, _colsum(dnv * yh), first)
        Bm, Cm = b_ref[...], c_ref[...]
        S = sp_ref[0]
        dS = dS_ref[...]
        Sb, dSb = S.astype(BF16), dS.astype(BF16)
        cb = _dot(Cm, Bm, NT)
        cbT = _dot(Bm, Cm, NT)
        CmT = Cm.T
        CS = _dot(Cm, Sb)
        T1 = _dot(Bm, dSb)
        yield
        row = lax.broadcasted_iota(jnp.int32, (CH, CH), 0)
        col = lax.broadcasted_iota(jnp.int32, (CH, CH), 1)
        tril = row >= col
        triu = row <= col
        left = col < HP
        lastrow = lax.broadcasted_iota(jnp.int32, (CH, 1), 0) == CH - 1
        dCB = jnp.zeros((CH, CH), F32)
        dCBT = jnp.zeros((CH, CH), F32)
        xd_parts, dye_parts, dec_parts, dsk_parts, dxx_parts, gr_parts, end_parts, qd_parts = ([] for _ in range(8))
        for p in range(_HG // 2):
            sl = slice(p * LANES, (p + 1) * LANES)
            j0, j1 = hb + 2 * p, hb + 2 * p + 1
            xp = xs_ref[:, sl].astype(F32)
            dyp = dyg[:, sl]
            a0, a1 = ac[:, j0:j0 + 1], ac[:, j1:j1 + 1]
            al0, al1 = ac[CH - 1:CH, j0:j0 + 1], ac[CH - 1:CH, j1:j1 + 1]
            dtl = _pair_lanes(left, dt[:, j0:j0 + 1], dt[:, j1:j1 + 1])
            X = xp * dtl
            Xb = X.astype(BF16)
            eal = _pair_lanes(left, jnp.exp(a0), jnp.exp(a1))
            dtel = _pair_lanes(left, jnp.exp(al0 - a0), jnp.exp(al1 - a1))
            T1p = T1[:, sl]
            Rm = T1p * dtel * X
            decp = _pair_lanes(left[0:1], jnp.exp(al0), jnp.exp(al1))
            gr_parts.append(dyp * (eal * CS[:, sl]) - Rm)
            end_parts.append(Rm + decp * (dS[:, sl] * S[:, sl]))
            dXd = jnp.zeros((CH, LANES), F32)
            for j, aj, mask in ((j0, a0, left), (j1, a1, jnp.logical_not(left))):
                dYm = jnp.where(mask, dyp, 0.0).astype(BF16)
                dWm = _dot(dYm, Xb, NT)
                dWmT = _dot(Xb, dYm, NT)
                yield
                e = aj - acT[j:j + 1, :]
                P = dWm * jnp.exp(jnp.where(tril, e, _NEG))
                LmT = jnp.exp(jnp.where(triu, -e, _NEG))
                PT = dWmT * LmT
                dCB = dCB + P
                dCBT = dCBT + PT
                yield
                dXd = dXd + _dot((cbT * LmT).astype(BF16), dYm)
                qd_parts.append((P * cb - PT * cbT).astype(BF16))
                yield
            dX = dXd + dtel * T1p
            dxx_parts.append(dX * xp)
            dx_ref[:, sl] = (dX * dtl + dsk_ref[:, sl] * dyp).astype(BF16)
            dsk_parts.append(_colsum(dyp * xp))
            xd_parts.append(X * dtel)
            dye_parts.append(dyp * eal)
            dec_parts.append(decp)
            yield
        Xd = jnp.concatenate(xd_parts, axis=1).astype(BF16)
        dYe = jnp.concatenate(dye_parts, axis=1).astype(BF16)
        dec = jnp.concatenate(dec_parts, axis=1)
        def lane_sums(parts):
            return _dot(jnp.concatenate(parts, axis=1).astype(BF16), lanes_of_ref[...])
        ddtx = lane_sums(dxx_parts)
        dAc = (_dot(jnp.concatenate(qd_parts, axis=1), rows_of_ref[...]) + lane_sums(gr_parts)
               + jnp.where(lastrow, _colsum(lane_sums(end_parts)), 0.0))
        dC_ref[...] = (_dot(dCB.astype(BF16), Bm) + _dot(dYe, Sb, NT)).astype(BF16)
        dB_ref[...] = (_dot(dCBT.astype(BF16), Cm) + _dot(Xd, dSb, NT)).astype(BF16)
        dS_ref[...] = _dot(CmT, dYe) + dec * dS
        _acc_out(dskacc_ref, jnp.concatenate(dsk_parts, axis=1), first)
        return ddtx, dAc

    zspec = pl.BlockSpec((CH, DI), lambda c: (nc - 1 - c, OFF_Z // DI))
    head = lax.broadcasted_iota(jnp.int32, (1, LANES), 1)
    lanes_of = (lax.broadcasted_iota(jnp.int32, (DI, 1), 0) // HP == head).astype(BF16)
    rows_of = (lax.broadcasted_iota(jnp.int32, (NH * CH, 1), 0) // CH == head).astype(BF16)
    return pl.pallas_call(
        body, name="ssd_bwd", grid=(nc,),
        in_specs=[sp["xs"], sp["bm"], sp["cmat"], sp["col"], sp["col"], sp["col"], sp["dsk"], sp["xs"], sp["xs"],
                  zspec, sp["dsk"], sp["state"], ANY, pl.BlockSpec(lanes_of.shape, lambda c: (0, 0)),
                  pl.BlockSpec(rows_of.shape, lambda c: (0, 0))],
        out_specs=[sp["xbc"], sp["col"], sp["col"], sp["dsk"], zspec, sp["dsk"]],
        out_shape=[jax.ShapeDtypeStruct((T, DX), BF16), jax.ShapeDtypeStruct((T, LANES), F32),
                   jax.ShapeDtypeStruct((T, LANES), F32), jax.ShapeDtypeStruct((1, DI), F32),
                   jax.ShapeDtypeStruct(dproj.shape, BF16), jax.ShapeDtypeStruct((1, DI), F32)],
        scratch_shapes=[pltpu.VMEM((NS, DI), F32)], input_output_aliases={12: 4},
        compiler_params=_params(("arbitrary",)),
    )(xact, xact, xact, dt, acum, acumT, dsk_rep, dn, y, proj, norm_w, sprev, dproj, lanes_of, rows_of)


def _merge_fwd_epilogue(proj, ya):
    T = proj.shape[0]

    def fn(ysv, ins, outs, first):
        g_ref, ya_ref = ins
        m_ref, ys_ref = outs
        ga = _sigmoid(g_ref[:, :D].astype(F32))
        gs = _sigmoid(g_ref[:, D:].astype(F32))
        m_ref[...] = (ga * ya_ref[...].astype(F32) + gs * ysv).astype(BF16)
        ys_ref[...] = ysv.astype(BF16)

    return _Epilogue(fn, (proj, ya), (((T, D), BF16), ((T, D), BF16)), 10 * D, in_windows={0: (OFF_G, 2 * D)})


def _merge_bwd_epilogue(proj, ya, ys, ncols):
    T = proj.shape[0]

    def fn(d, ins, outs, first):
        g_ref, ya_ref, ys_ref = ins
        dg_ref, dya_ref, dys_ref = outs
        ga = _sigmoid(g_ref[:, :D].astype(F32))
        gs = _sigmoid(g_ref[:, D:].astype(F32))
        dya_ref[...] = (d * ga).astype(BF16)
        dys_ref[...] = (d * gs).astype(BF16)
        dg_ref[:, :D] = (d * ya_ref[...].astype(F32) * ga * (1.0 - ga)).astype(BF16)
        dg_ref[:, D:] = (d * ys_ref[...].astype(F32) * gs * (1.0 - gs)).astype(BF16)

    window = (OFF_G, 2 * D)
    return _Epilogue(fn, (proj, ya, ys), (((T, ncols), BF16), ((T, D), BF16), ((T, D), BF16)), 16 * D,
                     in_windows={0: window}, out_windows={0: window})


_FW = 1408
_FB = FF // _FW


def _ffn_act_fwd(hv, conv_w, conv_b):
    T = hv.shape[0]
    R = _Rows(T, 256)
    tm = R.tm

    def body(h1_ref, h1p_ref, h3_ref, w_ref, b_ref, o_ref):
        keep = (pl.program_id(1) > 0).astype(F32)
        ext = jnp.concatenate([h1p_ref[...].astype(F32) * keep, h1_ref[...].astype(F32)], axis=0)
        pre = _wsum(w_ref[...], _shifts_causal(ext, 3, tm)) + b_ref[...]
        o_ref[...] = (pre * _sigmoid(pre) * h3_ref[...].astype(F32)).astype(BF16)

    return R.call(body, "ffn_act_fwd", _FB,
                  [R.tile(_FW), R.prev(_FW), R.tile(_FW, _FB), R.colvec(3, _FW), R.colvec(1, _FW)],
                  R.tile(_FW), jax.ShapeDtypeStruct((T, FF), BF16), (hv, hv, hv, conv_w, conv_b))


def _ffn_act_bwd(dg, hv, conv_w, conv_b):
    T = hv.shape[0]
    R = _Rows(T, 256)
    tm = R.tm

    def body(dg_ref, dgn_ref, h1_ref, h1p_ref, h1n_ref, h3_ref, h3n_ref, w_ref, b_ref, dh3_ref, dh1_ref, dw_ref,
             db_ref):
        i = pl.program_id(1)
        keep_p = (i > 0).astype(F32)
        keep_n = (i < R.nrow - 1).astype(F32)
        w = w_ref[...]
        ext = jnp.concatenate([h1p_ref[...].astype(F32) * keep_p, h1_ref[...].astype(F32),
                               h1n_ref[...].astype(F32)], axis=0)
        sh = _shifts_causal(ext, 3, tm + HALO)
        pre = _wsum(w, sh) + b_ref[...]
        s = _sigmoid(pre)
        d = jnp.concatenate([dg_ref[...].astype(F32), dgn_ref[...].astype(F32) * keep_n], axis=0)
        h3 = jnp.concatenate([h3_ref[...].astype(F32), h3n_ref[...].astype(F32)], axis=0)
        dh3_ref[...] = (d[:tm] * pre[:tm] * s[:tm]).astype(BF16)
        dpre = d * h3 * s * (1.0 + pre * (1.0 - s))
        dh1_ref[...] = _wsum(w, _shifts_anticausal(dpre, 3, tm)).astype(BF16)
        dp = dpre[:tm]
        _acc_rows(dw_ref, [_colsum(dp * q[:tm]) for q in sh], i == 0)
        _acc_out(db_ref, _colsum(dp), i == 0)

    return R.call(
        body, "ffn_act_bwd", _FB,
        [R.tile(_FW), R.next(_FW), R.tile(_FW), R.prev(_FW), R.next(_FW), R.tile(_FW, _FB), R.next(_FW, _FB),
         R.colvec(3, _FW), R.colvec(1, _FW)],
        [R.tile(_FW), R.tile(_FW), R.colvec(3, _FW), R.colvec(1, _FW)],
        [jax.ShapeDtypeStruct((T, FF), BF16), jax.ShapeDtypeStruct((T, FF), BF16),
         jax.ShapeDtypeStruct((3, FF), F32), jax.ShapeDtypeStruct((1, FF), F32)],
        (dg, dg, hv, hv, hv, hv, hv, conv_w, conv_b))


def _final_loss_epilogue(w, target):
    T = target.shape[0]

    def fn(xv, ins, outs, first):
        w_ref, t_ref = ins
        l_ref, dh_ref, dhb_ref, dw_ref = outs
        wv = w_ref[...]
        r = lax.rsqrt(jnp.mean(xv * xv, axis=-1, keepdims=True) + EPS)
        xh = xv * r
        err = xh * wv - t_ref[...]
        part = 0.5 * jnp.sum(jnp.mean(err * err, axis=-1, keepdims=True), axis=0, keepdims=True)
        _acc_out(l_ref, jnp.broadcast_to(part, l_ref.shape), first)
        dy = err * (1.0 / D)
        dxh = dy * wv
        dh = r * (dxh - xh * jnp.mean(dxh * xh, axis=-1, keepdims=True))
        dh_ref[...] = dh
        dhb_ref[...] = dh.astype(BF16)
        _acc_out(dw_ref, _colsum(dy * xh), first)

    return _Epilogue(fn, (w, target),
                     (((8, LANES), F32), ((T, D), F32), ((T, D), BF16), ((1, D), F32)), 10 * D)


def _pad_lanes(v, n=LANES):
    return jnp.pad(v, ((0, 0), (0, n - v.shape[1])))


class _Hooks:
    def before_in_proj(self, w_in):
        return w_in

    def late_weights(self, wts, after):
        return wts

    def grads_ready(self, grads, tie):
        return tie

    def mark(self, name, value):
        pass


def _local_step(x, target, wts, hooks=None):
    hooks = hooks or _Hooks()
    T = x.shape[0]
    w_in = wts["w_in"]
    dt_bias_p, a_log_p = _pad_lanes(wts["dt_bias"]), _pad_lanes(wts["a_log"])
    dsk_rep = jnp.repeat(wts["d_skip"], HP, axis=1)

    w_in = hooks.before_in_proj(w_in)
    proj, u, dt_raw = _norm_matmul(x, wts["norm_mix_w"], w_in, "norm_mm_in", w_in[:, OFF_DT:OFF_DT + LANES])
    ya_in = _branch_a_fwd(proj, wts["conv_a_w"])
    xact = _xbc_fwd(proj, wts["ssd_conv_w"], wts["ssd_conv_b"])
    dt, acum, acumT = _dt_fwd(dt_raw, dt_bias_p, a_log_p)
    y_ssd, yn, sprev = _ssd_fwd(xact, dt, acum, acumT, dsk_rep, proj, wts["ssd_norm_w"])
    late = hooks.late_weights(wts, yn)
    w_a_out, w_s_out, w_o, w_up, w_down = (late[k] for k in ("w_a_out", "w_s_out", "w_o", "w_up", "w_down"))
    y_a = _matmul(ya_in, w_a_out, mode="nn", out_dtype=BF16, name="mm_a_out")
    merged, y_s = _matmul(yn, w_s_out, mode="nn", out_dtype=BF16, name="mm_s_out_merge",
                          epilogue=_merge_fwd_epilogue(proj, y_a))
    h1 = _matmul(merged, w_o, mode="nn", out_dtype=F32, name="mm_o", residual=x)
    hv, v = _norm_matmul(h1, wts["norm_ffn_w"], w_up, "norm_mm_up")
    gact = _ffn_act_fwd(hv, wts["ffn_conv_w"], wts["ffn_conv_b"])
    loss, dh2, dh2b, g_final = _matmul(gact, w_down, mode="nn", out_dtype=F32, name="mm_down_loss", residual=h1,
                                       epilogue=_final_loss_epilogue(wts["final_norm_w"], target))

    grads = {"final_norm_w": g_final}
    grads["w_down"] = _matmul(gact, dh2b, mode="tn", out_dtype=F32, name="mm_down_dw")
    dgact = _matmul(dh2b, w_down, mode="nt", out_dtype=BF16, name="mm_down_dx")
    dh3, dh1c, grads["ffn_conv_w"], grads["ffn_conv_b"] = _ffn_act_bwd(dgact, hv, wts["ffn_conv_w"], wts["ffn_conv_b"])
    grads["w_up"] = (_matmul(v, dh1c, mode="tn", out_dtype=F32, name="mm_up_dw1"),
                     _matmul(v, dh3, mode="tn", out_dtype=F32, name="mm_up_dw3"))
    dv = _matmul(dh1c, w_up, mode="nt", out_dtype=F32, name="mm_up_dx1")
    dh1, dh1b, grads["norm_ffn_w"] = _matmul(
        dh3, w_up, mode="nt", out_dtype=F32, name="mm_up_dx3_norm", residual=dv, b_k_off=FF,
        epilogue=_rmsnorm_bwd_epilogue(h1, wts["norm_ffn_w"], dh2))
    grads["w_o"] = _matmul(merged, dh1b, mode="tn", out_dtype=F32, name="mm_o_dw")
    dproj, dya, dys = _matmul(dh1b, w_o, mode="nt", out_dtype=BF16, name="mm_o_dx_merge",
                              epilogue=_merge_bwd_epilogue(proj, y_a, y_s, NIP))
    grads["w_a_out"] = _matmul(ya_in, dya, mode="tn", out_dtype=F32, name="mm_a_out_dw")
    dya_in = _matmul(dya, w_a_out, mode="nt", out_dtype=BF16, name="mm_a_out_dx")
    dproj, grads["conv_a_w"] = _branch_a_bwd(dya_in, proj, wts["conv_a_w"], dproj)
    grads["w_s_out"] = _matmul(yn, dys, mode="tn", out_dtype=F32, name="mm_s_out_dw")
    dys = hooks.grads_ready({k: grads[k] for k in ("w_a_out", "w_s_out", "w_o", "w_up", "w_down")}, dys)
    dyn =_matmul(dys, w_s_out, mode="nt", out_dtype=BF16, name="mm_s_out_dx")
    dxact, ddt_x, dacum, dskl, dproj, grads["ssd_norm_w"] = _ssd_bwd(
        dyn, y_ssd, proj, wts["ssd_norm_w"], dproj, xact, dt, acum, acumT, dsk_rep, sprev)
    hooks.mark("ssd_bwd", dxact)
    grads["d_skip"] = dskl.reshape(NH, HP).sum(axis=1).reshape(1, NH)
    dproj, grads["ssd_conv_w"], grads["ssd_conv_b"] = _xbc_bwd(dxact, proj, wts["ssd_conv_w"], wts["ssd_conv_b"], dproj)
    dproj, g_dtb, g_alog = _dt_bwd(dacum, ddt_x, dt_raw, dt_bias_p, a_log_p, dproj)
    grads["dt_bias"], grads["a_log"] = g_dtb[:, :NH], g_alog[:, :NH]
    grads["w_in"] = _matmul(u, dproj, mode="tn", out_dtype=F32, name="mm_in_dw")
    dproj = hooks.grads_ready({"w_in": grads["w_in"]}, dproj)
    grad_x, _, grads["norm_mix_w"] = _matmul(dproj, w_in, mode="nt", out_dtype=F32, name="mm_in_dx_norm",
                                             epilogue=_rmsnorm_bwd_epilogue(x, wts["norm_mix_w"], dh1))
    return loss, grad_x, grads


def _permute_w_in(slabs):
    cs = slabs.shape[2]
    pieces = []
    for o, n, no in sorted(_SEGS, key=lambda seg: seg[2]):
        for s in range(slabs.shape[0]):
            lo, hi = max(o, s * cs), min(o + n, (s + 1) * cs)
            if lo < hi:
                pieces.append(slabs[s][:, lo - s * cs:hi - s * cs])
    pieces.append(jnp.zeros((slabs.shape[1], NIP - OFF_DT - _SEGS[-1][1]), slabs.dtype))
    return jnp.concatenate(pieces, axis=1)


def _unpermute_w_in(g):
    cs = NI // NCHIP
    slabs = []
    for s in range(NCHIP):
        pieces = []
        for o, n, no in sorted(_SEGS):
            lo, hi = max(o, s * cs), min(o + n, (s + 1) * cs)
            if lo < hi:
                pieces.append(g[:, no + lo - o:no + hi - o])
        slabs.append(jnp.concatenate(pieces, axis=1))
    return jnp.stack(slabs)


MESH = pl.DeviceIdType.MESH
NCHIP = 4
NDEV = 8

_W_IN = (("w_in", D, NI // NCHIP, 1),)
_W_REST = (("w_a_out", D // NCHIP, D, 0), ("w_s_out", DI // NCHIP, D, 0), ("w_o", D // NCHIP, D, 0),
           ("w_up", D, 2 * FF // NCHIP, 1), ("w_down", FF // NCHIP, D, 0))


def _coords():
    return lax.axis_index("x"), lax.axis_index("y"), lax.axis_index("c")


def _other_chips(x, y):
    return [(1 - x, y), (x, 1 - y), (1 - x, 1 - y)]


def _ag_weights(shard):
    nrows = shard.shape[0]
    qr = nrows // 4

    def body(x_ref, out_ref, send_sems, recv_sems, local_sem):
        x, y, c = _coords()
        me = 2 * x + y
        xn, yn, sibling = (1 - x, y, c), (x, 1 - y, c), (x, y, 1 - c)
        s_xn, s_yn, s_far = 2 * (1 - x) + y, 2 * x + 1 - y, 2 * (1 - x) + 1 - y

        def window(h, q):
            return pl.ds((2 * h + q) * qr, qr)

        def copy(k, s, h, q, to, src=None):
            dst = out_ref.at[s, window(h, q), :]
            return pltpu.make_async_remote_copy(
                src_ref=dst if src is None else src, dst_ref=dst,
                send_sem=send_sems.at[k], recv_sem=recv_sems.at[k], device_id=to, device_id_type=MESH)

        mine = pltpu.make_async_copy(x_ref, out_ref.at[me], local_sem)
        mine.start()
        started = [copy(k, me, c, q, to, src=x_ref.at[window(c, q), :])
                   for k, q, to in ((0, 1, xn), (2, 0, yn), (1, 0, xn), (3, 1, yn))]
        for cp in started:
            cp.start()
        landing = ((0, s_xn, 1, (4, yn)), (2, s_yn, 0, (5, xn)), (1, s_xn, 0, None), (3, s_yn, 1, None),
                   (4, s_far, 1, None), (5, s_far, 0, None))
        for k, s, q, onward in landing:
            copy(k, s, c, q, (x, y, c)).wait_recv()
            nxt = [copy(6 + k, s, c, q, sibling)] + ([copy(onward[0], s, c, q, onward[1])] if onward else [])
            for cp in nxt:
                cp.start()
            started += nxt
        for k, s, q, _ in landing:
            copy(6 + k, s, 1 - c, q, (x, y, c)).wait_recv()
        for cp in started:
            cp.wait_send()
        mine.wait()

    return pl.pallas_call(
        body, name="ag_weights", in_specs=[ANY], out_specs=ANY,
        out_shape=jax.ShapeDtypeStruct((NCHIP,) + shard.shape, shard.dtype),
        scratch_shapes=[pltpu.SemaphoreType.DMA((12,)), pltpu.SemaphoreType.DMA((12,)), pltpu.SemaphoreType.DMA],
        compiler_params=pltpu.CompilerParams(has_side_effects=True),
    )(shard)


HBM = pl.BlockSpec(memory_space=pltpu.HBM)
SEM = pl.BlockSpec(memory_space=pltpu.SEMAPHORE)
_EFFECT = pltpu.SideEffectType.DATAFLOW_SIDE_EFFECTING
_NCOPY = NCHIP - 1


def _plan_bcast(src_ref, land_ref, send_sems, recv_sems, base):
    x, y, c = _coords()
    sends, lands = [], []
    for k, chip in enumerate(_other_chips(x, y)):
        def copy(slot):
            return pltpu.make_async_remote_copy(
                src_ref=src_ref, dst_ref=land_ref.at[slot], send_sem=send_sems.at[base + k],
                recv_sem=recv_sems.at[base + k], device_id=(*chip, c), device_id_type=MESH)
        sends.append(copy(2 * x + y))
        lands.append(copy(2 * chip[0] + chip[1]))
    return sends, lands


def _plan_scatter(src_ref, land_ref, send_sems, recv_sems, base):
    x, y, c = _coords()
    cps = [pltpu.make_async_remote_copy(
        src_ref=src_ref.at[2 * chip[0] + chip[1]], dst_ref=land_ref.at[k], send_sem=send_sems.at[base + k],
        recv_sem=recv_sems.at[base + k], device_id=(*chip, c), device_id_type=MESH)
        for k, chip in enumerate(_other_chips(x, y))]
    return cps, cps


def _plan_all(plan, refs, n):
    sends, lands = [], []
    for t in range(n):
        s, l = plan(refs[t], refs[n + t], refs[2 * n], refs[2 * n + 1], t * _NCOPY)
        sends += s
        lands += l
    return sends, lands


def _split_start(name, srcs, lands, plan):
    n = len(srcs)

    def body(*refs):
        for cp in _plan_all(plan, refs, n)[0]:
            cp.start()
        refs[-1][...] = jnp.zeros_like(refs[-1])

    arrays = list(srcs) + list(lands)
    outs = pl.pallas_call(
        body, name=name,
        out_shape=(pltpu.SemaphoreType.DMA((n * _NCOPY,)), pltpu.SemaphoreType.DMA((n * _NCOPY,)),
                   *[pltpu.HBM(a.shape, a.dtype) for a in arrays], jax.ShapeDtypeStruct((8, LANES), F32)),
        in_specs=(HBM,) * (2 * n),
        out_specs=(SEM, SEM) + (HBM,) * (2 * n) + (pl.BlockSpec(memory_space=pltpu.VMEM),),
        input_output_aliases={t: 2 + t for t in range(2 * n)},
        compiler_params=pltpu.CompilerParams(has_side_effects=_EFFECT),
    )(*[pltpu.with_memory_space_constraint(a, pltpu.HBM) for a in arrays])
    return (outs[0], outs[1], tuple(outs[2:2 + 2 * n])), outs[-1]


def _split_wait(name, handle, after, plan):
    send_sems, recv_sems, arrays = handle
    n = len(arrays) // 2

    def body(*refs):
        sends, lands = _plan_all(plan, refs[:2 * n] + refs[2 * n:2 * n + 2], n)
        for cp in sends:
            cp.wait_send()
        for cp in lands:
            cp.wait_recv()

    outs = pl.pallas_call(
        body, name=name, out_shape=tuple(pltpu.HBM(a.shape, a.dtype) for a in arrays),
        in_specs=(HBM,) * (2 * n) + (SEM, SEM, ANY), out_specs=(HBM,) * (2 * n),
        input_output_aliases={t: t for t in range(2 * n)},
        compiler_params=pltpu.CompilerParams(has_side_effects=_EFFECT),
    )(*arrays, send_sems, recv_sems, after)
    return outs[:n], outs[n:]


def _tie(x, token, name):
    def body(x_ref, t_ref, o_ref):
        pass

    return pl.pallas_call(
        body, name=name, in_specs=[ANY, pl.BlockSpec(memory_space=pltpu.VMEM)], out_specs=ANY,
        out_shape=jax.ShapeDtypeStruct(x.shape, x.dtype), input_output_aliases={0: 0},
    )(x, token)


def _swap_sibling(ps, name):
    n = len(ps)

    def body(*refs):
        x, y, c = _coords()
        cps = [pltpu.make_async_remote_copy(
            src_ref=refs[t], dst_ref=refs[n + t], send_sem=refs[2 * n].at[t], recv_sem=refs[2 * n + 1].at[t],
            device_id=(x, y, 1 - c), device_id_type=MESH) for t in range(n)]
        for cp in cps:
            cp.start()
        for cp in cps:
            cp.wait()

    return pl.pallas_call(
        body, name=name, in_specs=[ANY] * n, out_specs=[ANY] * n,
        out_shape=[jax.ShapeDtypeStruct(p.shape, p.dtype) for p in ps],
        scratch_shapes=[pltpu.SemaphoreType.DMA((n,)), pltpu.SemaphoreType.DMA((n,))],
        compiler_params=pltpu.CompilerParams(has_side_effects=True),
    )(*ps)


_ADD_BYTES = 7 << 19


def _add_tile(rows, cols):
    best = 32
    for t in range(32, rows + 1, 32):
        if rows % t == 0 and t * cols * 4 <= _ADD_BYTES:
            best = t
    return best


def _add_slabs(pack, land, me, name):
    rows, cols = pack.shape[1:]
    tr = _add_tile(rows, cols)

    def body(me_ref, p_ref, l_ref, o_ref):
        f = lambda r: r.astype(F32)
        o_ref[...] = ((f(p_ref[0]) + f(l_ref[0])) + f(l_ref[1])) + f(l_ref[2])

    return pl.pallas_call(
        body, name=name,
        grid_spec=pltpu.PrefetchScalarGridSpec(
            num_scalar_prefetch=1, grid=(rows // tr,),
            in_specs=[pl.BlockSpec((1, tr, cols), lambda i, me_ref: (me_ref[0], i, 0)),
                      pl.BlockSpec((_NCOPY, tr, cols), lambda i, me_ref: (0, i, 0))],
            out_specs=pl.BlockSpec((tr, cols), lambda i, me_ref: (i, 0))),
        out_shape=jax.ShapeDtypeStruct((rows, cols), F32),
        compiler_params=_params(("parallel",)),
    )(me, pack, land)


_STAGE_W = 1024


def _stage_rows(shapes):
    pieces, r = [], 0
    for i, (k, w) in enumerate(shapes):
        for a in range(k):
            for q in range(0, w, _STAGE_W):
                pieces.append((i, a, q, min(_STAGE_W, w - q), r))
                r += 1
    return pieces, -(-r // 8) * 8


def _gather8(parts, reduce, name):
    shapes = [p.shape for p in parts]
    pieces, rows = _stage_rows(shapes)
    n = len(parts)

    def body(*refs):
        ins, outs = refs[:n], refs[n:2 * n]
        stage, buf, res, send_sems, recv_sems = refs[2 * n:]
        x, y, c = _coords()
        me = 4 * x + 2 * y + c
        stage[...] = jnp.zeros_like(stage)
        for i, a, q, w, r in pieces:
            stage[r:r + 1, 0:w] = ins[i][a:a + 1, q:q + w]
        buf[pl.ds(me, 1)] = stage[...][None]
        cps, lands = [], []
        for k in range(1, NDEV):
            peer = (1 - x if k & 4 else x, 1 - y if k & 2 else y, 1 - c if k & 1 else c)

            def copy(slot):
                return pltpu.make_async_remote_copy(
                    src_ref=stage, dst_ref=buf.at[slot], send_sem=send_sems.at[k - 1],
                    recv_sem=recv_sems.at[k - 1], device_id=peer, device_id_type=MESH)

            cps.append(copy(me))
            lands.append(copy(4 * peer[0] + 2 * peer[1] + peer[2]))
        for cp in cps:
            cp.start()
        for cp, land in zip(cps, lands):
            land.wait_recv()
            cp.wait_send()
        if reduce:
            acc = buf[0]
            for d in range(1, NDEV):
                acc = acc + buf[d]
            res[...] = acc
            for i, a, q, w, r in pieces:
                outs[i][a:a + 1, q:q + w] = res[r:r + 1, 0:w]
        else:
            for i, a, q, w, r in pieces:
                for s in range(NCHIP):
                    outs[i][s, a:a + 1, q:q + w] = buf[2 * s, r:r + 1, 0:w]

    vm = pl.BlockSpec(memory_space=pltpu.VMEM)
    out_shapes = [jax.ShapeDtypeStruct(s if reduce else (NCHIP,) + s, F32) for s in shapes]
    return pl.pallas_call(
        body, name=name, in_specs=[vm] * n, out_specs=[vm] * n, out_shape=out_shapes,
        scratch_shapes=[pltpu.VMEM((rows, _STAGE_W), F32), pltpu.VMEM((NDEV, rows, _STAGE_W), F32),
                        pltpu.VMEM((rows, _STAGE_W), F32), pltpu.SemaphoreType.DMA((NDEV - 1,)),
                        pltpu.SemaphoreType.DMA((NDEV - 1,))],
        compiler_params=pltpu.CompilerParams(has_side_effects=True),
    )(*parts)


def _adamw_update(w_ref, g_ref, m_ref, v_ref, d_ref, mo_ref, vo_ref):
    c1 = 1.0 / (1.0 - ADAM_B1 ** ADAM_STEP)
    c2 = 1.0 / (1.0 - ADAM_B2 ** ADAM_STEP)
    gv = g_ref[...]
    mn = ADAM_B1 * m_ref[...] + (1.0 - ADAM_B1) * gv
    vn = ADAM_B2 * v_ref[...] + (1.0 - ADAM_B2) * (gv * gv)
    d_ref[...] = -ADAM_LR * ((mn * c1) / (jnp.sqrt(vn * c2) + ADAM_EPS) + ADAM_WD * w_ref[...])
    mo_ref[...] = mn
    vo_ref[...] = vn


def _adamw_small(ws, gs, ms, vs):
    n = len(ws)

    def body(*refs):
        for i in range(n):
            _adamw_update(*(refs[j * n + i] for j in range(7)))

    vm = pl.BlockSpec(memory_space=pltpu.VMEM)
    outs = pl.pallas_call(
        body, name="adamw_small", in_specs=[vm] * (4 * n), out_specs=[vm] * (3 * n),
        out_shape=[jax.ShapeDtypeStruct(w.shape, F32) for w in ws] * 3,
    )(*ws, *gs, *ms, *vs)
    return outs[:n], outs[n:2 * n], outs[2 * n:]


def _adamw(w, g_parts, m, v, name):
    rows, cols = w.shape
    tr = rows
    while tr * cols * 4 > (1 << 20) and tr % 16 == 0:
        tr //= 2

    def body(w_ref, ga_ref, gb_ref, m_ref, v_ref, g_ref, d_ref, mo_ref, vo_ref):
        g_ref[...] = ga_ref[...] + gb_ref[...]
        _adamw_update(w_ref, g_ref, m_ref, v_ref, d_ref, mo_ref, vo_ref)

    blk = pl.BlockSpec((tr, cols), lambda i: (i, 0))
    return pl.pallas_call(
        body, name=name, grid=(rows // tr,), in_specs=[blk] * 5, out_specs=[blk] * 4,
        out_shape=[jax.ShapeDtypeStruct((rows, cols), F32)] * 4, compiler_params=_params(("parallel",)),
    )(w, *g_parts, m, v)


def _by_chip(g, rr, cc, axis):
    if isinstance(g, tuple):
        n = NCHIP // len(g)
        return jnp.concatenate([h.reshape(rr, n, cc).transpose(1, 0, 2) for h in g], axis=0)
    return g.reshape(NCHIP, rr, cc) if axis == 0 else g.reshape(rr, NCHIP, cc).transpose(1, 0, 2)


_SMALL_REPL = ("norm_mix_w", "ssd_conv_b", "dt_bias", "a_log", "d_skip", "ssd_norm_w", "norm_ffn_w",
               "ffn_conv_b", "final_norm_w")
_SMALL_CONV = (("conv_a_w", 3, D), ("ssd_conv_w", 4, DX), ("ffn_conv_w", 3, FF))


def kernel(x, norm_mix_w, w_in, conv_a_w, w_a_out, ssd_conv_w, ssd_conv_b, dt_bias, a_log, d_skip, ssd_norm_w, w_s_out, w_o, norm_ffn_w, w_up, ffn_conv_w, ffn_conv_b, w_down, final_norm_w, loss_target, m_norm_mix_w, m_w_in, m_conv_a_w, m_w_a_out, m_ssd_conv_w, m_ssd_conv_b, m_dt_bias, m_a_log, m_d_skip, m_ssd_norm_w, m_w_s_out, m_w_o, m_norm_ffn_w, m_w_up, m_ffn_conv_w, m_ffn_conv_b, m_w_down, m_final_norm_w, v_norm_mix_w, v_w_in, v_conv_a_w, v_w_a_out, v_ssd_conv_w, v_ssd_conv_b, v_dt_bias, v_a_log, v_d_skip, v_ssd_norm_w, v_w_s_out, v_w_o, v_norm_ffn_w, v_w_up, v_ffn_conv_w, v_ffn_conv_b, v_w_down, v_final_norm_w):
    names = ("norm_mix_w", "w_in", "conv_a_w", "w_a_out", "ssd_conv_w", "ssd_conv_b", "dt_bias", "a_log", "d_skip",
             "ssd_norm_w", "w_s_out", "w_o", "norm_ffn_w", "w_up", "ffn_conv_w", "ffn_conv_b", "w_down", "final_norm_w")
    W = dict(zip(names, (norm_mix_w, w_in, conv_a_w, w_a_out, ssd_conv_w, ssd_conv_b, dt_bias, a_log, d_skip,
                         ssd_norm_w, w_s_out, w_o, norm_ffn_w, w_up, ffn_conv_w, ffn_conv_b, w_down, final_norm_w)))
    M = dict(zip(names, (m_norm_mix_w, m_w_in, m_conv_a_w, m_w_a_out, m_ssd_conv_w, m_ssd_conv_b, m_dt_bias, m_a_log,
                         m_d_skip, m_ssd_norm_w, m_w_s_out, m_w_o, m_norm_ffn_w, m_w_up, m_ffn_conv_w, m_ffn_conv_b,
                         m_w_down, m_final_norm_w)))
    V = dict(zip(names, (v_norm_mix_w, v_w_in, v_conv_a_w, v_w_a_out, v_ssd_conv_w, v_ssd_conv_b, v_dt_bias, v_a_log,
                         v_d_skip, v_ssd_norm_w, v_w_s_out, v_w_o, v_norm_ffn_w, v_w_up, v_ffn_conv_w, v_ffn_conv_b,
                         v_w_down, v_final_norm_w)))
    two_d = lambda a: a.reshape(-1, a.shape[-1])
    W2, M2, V2 = ({k: two_d(a) for k, a in t.items()} for t in (W, M, V))
    xi, yi, ci = _coords()
    me = 2 * xi + yi

    meidx = me.reshape(1).astype(jnp.int32)
    state = {}


    class Hooks(_Hooks):
        def before_in_proj(self, w_in):
            return _tie(w_in, state["rest_token"], "tie_ag_rest")

        def late_weights(self, wts, after):
            owns, lands = _split_wait("ag_rest_wait", state["rest"], after, _plan_bcast)
            full = {}
            for (n, rr, cc, axis), own, land in zip(_W_REST, owns, lands):
                slabs = lax.dynamic_update_slice(land, own[None], (me, 0, 0))
                full[n] = slabs.reshape(NCHIP * rr, cc) if axis == 0 else slabs.transpose(1, 0, 2).reshape(rr, NCHIP * cc)
            return {**wts, **full}

        def grads_ready(self, grads, tie):
            if "w_in" in grads:
                key, packs = "g_in", [_unpermute_w_in(grads["w_in"]).astype(BF16)]
            else:
                key = "g_rest"
                packs = [_by_chip(jax.tree.map(lambda t: t.astype(BF16), grads[n]), rr, cc, axis)
                         for n, rr, cc, axis in _W_REST]
            lands = [lax.empty((_NCOPY,) + p.shape[1:], BF16) for p in packs]
            state[key], token = _split_start("rs_" + key + "_start", packs, lands, _plan_scatter)
            return _tie(tie, token, "tie_" + key)

        def mark(self, name, value):
            state[name] = value

    def reduced(key, after, group):
        packs, lands = _split_wait("rs_" + key + "_wait", state[key], after, _plan_scatter)
        mines = [_add_slabs(p, l, meidx, "rs_add_chips_" + n) for (n, *_), p, l in zip(group, packs, lands)]
        return dict(zip([n for n, *_ in group], zip(mines, _swap_sibling(mines, "rs_" + key + "_swap"))))

    w_in_slabs = _ag_weights(W2["w_in"].astype(BF16))
    wts = {k: W2[k] for k in _SMALL_REPL}
    conv_by_chip = _gather8([W2[n] for n, *_ in _SMALL_CONV], False, "ag_conv_weights")
    for (n, kk, width), stacked in zip(_SMALL_CONV, conv_by_chip):
        wts[n] = stacked.transpose(1, 0, 2).reshape(kk, width)
    rest = [W2[n].astype(BF16) for n, *_ in _W_REST]
    rest[0] = _tie(rest[0], conv_by_chip[0], "tie_ag_order")
    state["rest"], state["rest_token"] = _split_start(
        "ag_rest_start", rest, [lax.empty((NCHIP,) + r.shape, BF16) for r in rest], _plan_bcast)
    wts["w_in"] = _permute_w_in(w_in_slabs)

    loss8, grad_x, grads = _local_step(x[0], loss_target[0], wts, Hooks())

    gbig = {**reduced("g_rest", state["ssd_bwd"], _W_REST), **reduced("g_in", grad_x, _W_IN)}

    small_parts = [grads[n] for n in _SMALL_REPL] + [loss8[0:1]] + [grads[n] for n, *_ in _SMALL_CONV]
    small_g = _gather8(small_parts, True, "allreduce_small")
    gsm = dict(zip(_SMALL_REPL, small_g[:len(_SMALL_REPL)]))
    loss = small_g[len(_SMALL_REPL)][0, 0]
    for (n, kk, width), gfull in zip(_SMALL_CONV, small_g[len(_SMALL_REPL) + 1:]):
        cw = width // NCHIP
        gsm[n] = lax.dynamic_slice(gfull, (0, me * cw), (kk, cw))

    G, DW, NM, NV = {}, {}, {}, {}
    for n in [b[0] for b in _W_IN + _W_REST]:
        G[n], DW[n], NM[n], NV[n] = _adamw(W2[n], gbig[n], M2[n], V2[n], "adamw_" + n)
    sm_names = list(_SMALL_REPL) + [n for n, *_ in _SMALL_CONV]
    outs = _adamw_small(*([t[n] for n in sm_names] for t in (W2, gsm, M2, V2)))
    for t, vals in zip((DW, NM, NV), outs):
        t.update(zip(sm_names, vals))
    G.update(gsm)

    def shaped(t):
        return [t[n].reshape(W[n].shape) for n in names]

    return (loss, grad_x.reshape(x.shape), *shaped(G), *shaped(DW), *shaped(NM), *shaped(NV))
```

```python
import jax
import jax.numpy as jnp
from jax import lax
from jax.experimental import pallas as pl
from jax.experimental.pallas import tpu as pltpu

F32 = jnp.float32
BF16 = jnp.bfloat16

D = 1024
DI = 2048
NH = 32
HP = 64
NG = 4
NS = 128
CH = 128
DX = 3072
FF = 2816
NI = 10272
EPS = 1e-5

OFF_BCV, OFF_XBC, OFF_G, OFF_Z, OFF_DT = 0, 3072, 6144, 8192, 10240
NIP = 10752
_SEGS = ((0, 2048, OFF_G), (2048, 3072, OFF_BCV), (5120, 2048, OFF_Z), (7168, 3072, OFF_XBC), (10240, 32, OFF_DT))

LANES = 128
HALO = 16
V7X_VMEM_LIMIT = 56 * 2 ** 20

ADAM_LR, ADAM_B1, ADAM_B2, ADAM_EPS, ADAM_WD, ADAM_STEP = 0.001, 0.9, 0.999, 1e-08, 0.01, 10

NN = (((1,), (0,)), ((), ()))
NT = (((1,), (1,)), ((), ()))
TN = (((0,), (0,)), ((), ()))


def _dot(a, b, dims=NN):
    return lax.dot_general(a, b, dims, preferred_element_type=F32)


def _params(sem, **kw):
    return pltpu.CompilerParams(dimension_semantics=sem, vmem_limit_bytes=V7X_VMEM_LIMIT, **kw)


V7X_MXU = 256
V7X_HBM_BYTES_PER_S = 3.5e12
STEP_S = 0.35e-6
MATMUL_VMEM = 40 * 2 ** 20
EPILOGUE_VMEM = 46 * 2 ** 20


ACC_BYTES_PER_S = 1.2e13


def _divisors(dim, cap, units):
    for unit in units:
        c = [t for t in range(unit, min(dim, cap) + 1, unit) if dim % t == 0]
        if c:
            return c
    return [dim]


def _tiles(M, N, K, out_bytes, has_res):
    best = None
    for tn in _divisors(N, 2816, (V7X_MXU, LANES)):
        for tm in _divisors(M, 2816, (LANES,)):
            for tk in _divisors(K, 2816, (V7X_MXU, LANES)):
                nk, ni, nj = K // tk, M // tm, N // tn
                vmem = 4 * (tm * tk + tk * tn) + 2 * tm * tn * out_bytes
                vmem += (4 * tm * tn if nk > 1 else 0) + (8 * tm * tn if has_res else 0)
                if vmem > MATMUL_VMEM:
                    continue
                a_reads = M * K * 2 * (nj if nk > 1 else 1)
                b_reads = K * N * 2 * (ni if nk * nj > 1 else 1)
                cost = (a_reads + b_reads + M * N * out_bytes) / V7X_HBM_BYTES_PER_S + ni * nj * nk * STEP_S
                cost += (nk - 1) * M * N * 8 / ACC_BYTES_PER_S
                if best is None or cost < best[0]:
                    best = (cost, tm, tn, tk)
    assert best is not None, (M, N, K)
    return best[1:]


def _sigmoid(x):
    return 1.0 / (1.0 + jnp.exp(-x))


class _Epilogue:
    def __init__(self, fn, ins, outs, tile_bytes, in_windows=None, out_windows=None):
        self.fn, self.ins, self.outs, self.tile_bytes = fn, tuple(ins), tuple(outs), tile_bytes
        self.in_windows, self.out_windows = in_windows or {}, out_windows or {}


def _matmul(a, b, *, mode, out_dtype, name, residual=None, b_k_off=0, epilogue=None):
    if mode == "nn":
        (M, K), (K2, N) = a.shape, b.shape
    elif mode == "nt":
        (M, K), (N, K2) = a.shape, (b.shape[0], a.shape[1])
        assert b_k_off + K <= b.shape[1]
    else:
        (K, M), (K2, N) = a.shape, b.shape
    assert K == K2, (name, a.shape, b.shape)
    tm, tn, tk = _tiles(M, N, K, jnp.dtype(out_dtype).itemsize, residual is not None)
    if epilogue is not None:
        tn = N
        fits = [(K * N * 2 * (M // t) / V7X_HBM_BYTES_PER_S + (K // q - 1) * M * N * 8 / ACC_BYTES_PER_S
                 + (M // t) * (K // q) * STEP_S, t, q)
                for t in (1024, 512, 256) if M % t == 0 for q in _divisors(K, 2816, (V7X_MXU, LANES))
                if 4 * (t * q + q * tn) + (4 * t * tn if K > q else 0) + (8 * t * tn if residual is not None else 0)
                + 2 * t * epilogue.tile_bytes <= EPILOGUE_VMEM]
        _, tm, tk = min(fits)
    nk = K // tk
    if mode == "tn":
        a_spec = pl.BlockSpec((tk, tm), lambda i, j, k: (k, i))
    else:
        a_spec = pl.BlockSpec((tm, tk), lambda i, j, k: (i, k))
    if mode == "nt":
        assert b_k_off % tk == 0
        b_spec = pl.BlockSpec((tn, tk), lambda i, j, k: (j, k + b_k_off // tk))
    else:
        b_spec = pl.BlockSpec((tk, tn), lambda i, j, k: (k, j))
    dims = {"nn": NN, "nt": NT, "tn": TN}[mode]
    o_spec = pl.BlockSpec((tm, tn), lambda i, j, k: (i, j))
    has_res = residual is not None

    def rows_or_whole(shape, window=None):
        if window is not None:
            off, width = window
            return pl.BlockSpec((tm, width), lambda i, j, k: (i, off // width))
        if shape[0] == M:
            return pl.BlockSpec((tm,) + tuple(shape[1:]), lambda i, j, k: (i,) + (0,) * (len(shape) - 1))
        return pl.BlockSpec(tuple(shape), lambda i, j, k: (0,) * len(shape))

    n_in = 2 + has_res + (len(epilogue.ins) if epilogue else 0)
    n_out = len(epilogue.outs) if epilogue else 1

    def body(*refs):
        a_ref, b_ref = refs[:2]
        r_ref = refs[2] if has_res else None
        out_refs = refs[n_in:n_in + n_out]
        acc_ref = refs[-1]
        k = pl.program_id(2)
        part = _dot(a_ref[...], b_ref[...], dims)

        def finish(r):
            if has_res:
                r = r + r_ref[...].astype(F32)
            if epilogue is None:
                out_refs[0][...] = r.astype(out_dtype)
            else:
                epilogue.fn(r, refs[2 + has_res:n_in], out_refs, pl.program_id(0) == 0)

        if nk == 1:
            finish(part)
            return

        @pl.when(k == 0)
        def _():
            acc_ref[...] = part

        @pl.when(jnp.logical_and(k > 0, k < nk - 1))
        def _():
            acc_ref[...] += part

        @pl.when(k == nk - 1)
        def _():
            finish(acc_ref[...] + part)

    in_specs = [a_spec, b_spec] + ([o_spec] if has_res else [])
    args = (a, b) + ((residual,) if has_res else ())
    if epilogue is None:
        out_specs, out_shape = o_spec, jax.ShapeDtypeStruct((M, N), out_dtype)
        sem = ("parallel", "parallel", "arbitrary")
    else:
        in_specs += [rows_or_whole(x.shape, epilogue.in_windows.get(n)) for n, x in enumerate(epilogue.ins)]
        args += epilogue.ins
        out_specs = [rows_or_whole(o[0], epilogue.out_windows.get(n)) for n, o in enumerate(epilogue.outs)]
        out_shape = [jax.ShapeDtypeStruct(shp, dt) for shp, dt in epilogue.outs]
        sem = ("arbitrary", "arbitrary", "arbitrary")
    return pl.pallas_call(
        body, name=name, grid=(M // tm, N // tn, nk), in_specs=in_specs, out_specs=out_specs,
        out_shape=out_shape, scratch_shapes=[pltpu.VMEM((tm, tn), F32)] if nk > 1 else [],
        compiler_params=_params(sem),
    )(*args)


class _Rows:
    def __init__(self, T, tm):
        self.T, self.tm = T, min(tm, T // 2)
        self.nrow = T // self.tm
        self.r = self.tm // HALO
        self.nb = T // HALO

    def tile(self, w, cb=0, step=1):
        return pl.BlockSpec((self.tm, w), lambda j, i: (i, cb + step * j))

    def prev(self, w, cb=0, step=1):
        r = self.r
        return pl.BlockSpec((HALO, w), lambda j, i: (jnp.maximum(i * r - 1, 0), cb + step * j))

    def next(self, w, cb=0, step=1):
        r, nb = self.r, self.nb
        return pl.BlockSpec((HALO, w), lambda j, i: (jnp.minimum((i + 1) * r, nb - 1), cb + step * j))

    def colvec(self, k, w, cb=0, step=1):
        return pl.BlockSpec((k, w), lambda j, i: (0, cb + step * j))

    def call(self, body, name, ncol, in_specs, out_specs, out_shape, args, aliases=None):
        return pl.pallas_call(
            body, name=name, grid=(ncol, self.nrow), in_specs=in_specs, out_specs=out_specs,
            out_shape=out_shape, input_output_aliases=aliases or {},
            compiler_params=_params(("parallel", "arbitrary")),
        )(*args)


ANY = pl.BlockSpec(memory_space=pl.ANY)


def _shifts_causal(ext, nk, tm):
    out = []
    for k in range(nk):
        s = nk - 1 - k
        r = ext if s == 0 else pltpu.roll(ext, s, 0)
        out.append(r[HALO:])
    return out


def _shifts_anticausal(ext, nk, tm):
    n = ext.shape[0]
    out = []
    for k in range(nk):
        s = nk - 1 - k
        r = ext if s == 0 else pltpu.roll(ext, n - s, 0)
        out.append(r[:tm])
    return out


def _wsum(w, parts):
    acc = w[0:1, :] * parts[0]
    for k in range(1, len(parts)):
        acc = acc + w[k:k + 1, :] * parts[k]
    return acc


def _colsum(x):
    return jnp.sum(x, axis=0, keepdims=True)


def _acc_out(ref, val, first):
    @pl.when(first)
    def _():
        ref[...] = val

    @pl.when(jnp.logical_not(first))
    def _():
        ref[...] += val


def _acc_rows(ref, rows, first):
    for k, r in enumerate(rows):
        _acc_out(ref.at[k:k + 1, :], r, first)


def _norm_matmul(x, wn, b, name, b_f32=None):
    T, N = x.shape[0], b.shape[1]
    tm = min(1024, T)
    tn = max(t for t in _divisors(N, 2816, (V7X_MXU, LANES))
             if 8 * tm * D + 6 * tm * D + 4 * D * t + 4 * tm * t <= MATMUL_VMEM)

    extra = b_f32 is not None

    def body(*refs):
        x_ref, wn_ref, b_ref = refs[:3]
        o_ref, u_ref = refs[3 + extra:5 + extra]
        keep_ref = refs[-1]

        @pl.when(pl.program_id(1) == 0)
        def _():
            xv = x_ref[...]
            r = lax.rsqrt(jnp.mean(xv * xv, axis=-1, keepdims=True) + EPS)
            u = (xv * r * wn_ref[...]).astype(BF16)
            keep_ref[...] = u
            u_ref[...] = u
            if extra:
                refs[5 + extra][...] = _dot(u, refs[3][...])

        o_ref[...] = _dot(keep_ref[...], b_ref[...]).astype(BF16)

    rows = pl.BlockSpec((tm, D), lambda i, j: (i, 0))
    whole = lambda shape: pl.BlockSpec(shape, lambda i, j: (0, 0))
    narrow = pl.BlockSpec((tm, LANES), lambda i, j: (i, 0))
    return pl.pallas_call(
        body, name=name, grid=(T // tm, N // tn),
        in_specs=[rows, whole((1, D)), pl.BlockSpec((D, tn), lambda i, j: (0, j))] + [whole((D, LANES))] * extra,
        out_specs=[pl.BlockSpec((tm, tn), lambda i, j: (i, j)), rows] + [narrow] * extra,
        out_shape=[jax.ShapeDtypeStruct((T, N), BF16), jax.ShapeDtypeStruct((T, D), BF16)]
        + [jax.ShapeDtypeStruct((T, LANES), F32)] * extra,
        scratch_shapes=[pltpu.VMEM((tm, D), BF16)],
        compiler_params=_params(("parallel", "arbitrary")),
    )(*((x, wn, b) + ((b_f32,) if extra else ())))


def _rmsnorm_bwd_epilogue(x, w, dres):
    T = x.shape[0]

    def fn(dyv, ins, outs, first):
        x_ref, w_ref, dr_ref = ins
        dx_ref, dxb_ref, dw_ref = outs
        xv = x_ref[...]
        r = lax.rsqrt(jnp.mean(xv * xv, axis=-1, keepdims=True) + EPS)
        xh = xv * r
        dxh = dyv * w_ref[...]
        dx = r * (dxh - xh * jnp.mean(dxh * xh, axis=-1, keepdims=True)) + dr_ref[...]
        dx_ref[...] = dx
        dxb_ref[...] = dx.astype(BF16)
        _acc_out(dw_ref, _colsum(dyv * xh), first)

    return _Epilogue(fn, (x, w, dres), (((T, D), F32), ((T, D), BF16), ((1, D), F32)), 14 * D)


def _branch_a_fwd(proj, conv_w):
    T = proj.shape[0]
    R = _Rows(T, 512)
    tm = R.tm

    def body(p_ref, pp_ref, w_ref, o_ref):
        keep = (pl.program_id(1) > 0).astype(F32)
        cv = p_ref[:, D:2 * D].astype(F32) * p_ref[:, 2 * D:].astype(F32)
        cvp = pp_ref[:, D:2 * D].astype(F32) * pp_ref[:, 2 * D:].astype(F32) * keep
        sh = _shifts_causal(jnp.concatenate([cvp, cv], axis=0), 3, tm)
        ca = _wsum(w_ref[...], sh)
        o_ref[...] = (p_ref[:, :D].astype(F32) * ca).astype(BF16)

    return R.call(body, "branch_a_fwd", 1, [R.tile(3 * D), R.prev(3 * D), R.colvec(3, D)], R.tile(D),
                  jax.ShapeDtypeStruct((T, D), BF16), (proj, proj, conv_w))


def _branch_a_bwd(dya_in, proj, conv_w, dproj):
    T = proj.shape[0]
    R = _Rows(T, 256)
    tm = R.tm

    def body(d_ref, dn_ref, p_ref, pp_ref, pn_ref, w_ref, _alias, o_ref, dw_ref):
        i = pl.program_id(1)
        keep_p = (i > 0).astype(F32)
        keep_n = (i < R.nrow - 1).astype(F32)
        w = w_ref[...]
        b = p_ref[:, :D].astype(F32)
        c = p_ref[:, D:2 * D].astype(F32)
        v = p_ref[:, 2 * D:].astype(F32)
        cvp = pp_ref[:, D:2 * D].astype(F32) * pp_ref[:, 2 * D:].astype(F32) * keep_p
        sh = _shifts_causal(jnp.concatenate([cvp, c * v], axis=0), 3, tm)
        ca = _wsum(w, sh)
        d = d_ref[...].astype(F32)
        dca = d * b
        dca_n = dn_ref[...].astype(F32) * pn_ref[:, :D].astype(F32) * keep_n
        dsh = _shifts_anticausal(jnp.concatenate([dca, dca_n], axis=0), 3, tm)
        dcv = _wsum(w, dsh)
        o_ref[:, :D] = (d * ca).astype(BF16)
        o_ref[:, D:2 * D] = (dcv * v).astype(BF16)
        o_ref[:, 2 * D:] = (dcv * c).astype(BF16)
        _acc_rows(dw_ref, [_colsum(dca * s) for s in sh], i == 0)

    return R.call(
        body, "branch_a_bwd", 1,
        [R.tile(D), R.next(D), R.tile(3 * D), R.prev(3 * D), R.next(3 * D), R.colvec(3, D), ANY],
        [R.tile(3 * D), R.colvec(3, D)],
        [jax.ShapeDtypeStruct(dproj.shape, BF16), jax.ShapeDtypeStruct((3, D), F32)],
        (dya_in, dya_in, proj, proj, proj, conv_w, dproj), aliases={6: 0})


_XW = 512


def _xbc_fwd(proj, conv_w, conv_b):
    T = proj.shape[0]
    R = _Rows(T, 512)
    tm = R.tm
    cb = OFF_XBC // _XW

    def body(x_ref, xp_ref, w_ref, b_ref, o_ref):
        keep = (pl.program_id(1) > 0).astype(F32)
        ext = jnp.concatenate([xp_ref[...].astype(F32) * keep, x_ref[...].astype(F32)], axis=0)
        pre = _wsum(w_ref[...], _shifts_causal(ext, 4, tm)) + b_ref[...]
        o_ref[...] = (pre * _sigmoid(pre)).astype(BF16)

    return R.call(body, "xbc_fwd", DX // _XW,
                  [R.tile(_XW, cb), R.prev(_XW, cb), R.colvec(4, _XW), R.colvec(1, _XW)], R.tile(_XW),
                  jax.ShapeDtypeStruct((T, DX), BF16), (proj, proj, conv_w, conv_b))


def _xbc_bwd(dact, proj, conv_w, conv_b, dproj):
    T = proj.shape[0]
    R = _Rows(T, 512)
    tm = R.tm
    cb = OFF_XBC // _XW

    def body(d_ref, dn_ref, x_ref, xp_ref, xn_ref, w_ref, b_ref, _alias, o_ref, dw_ref, db_ref):
        i = pl.program_id(1)
        keep_p = (i > 0).astype(F32)
        keep_n = (i < R.nrow - 1).astype(F32)
        w = w_ref[...]
        ext = jnp.concatenate([xp_ref[...].astype(F32) * keep_p, x_ref[...].astype(F32),
                               xn_ref[...].astype(F32)], axis=0)
        sh = _shifts_causal(ext, 4, tm + HALO)
        pre = _wsum(w, sh) + b_ref[...]
        s = _sigmoid(pre)
        dsilu = s * (1.0 + pre * (1.0 - s))
        dext = jnp.concatenate([d_ref[...].astype(F32), dn_ref[...].astype(F32) * keep_n], axis=0)
        dpre = dext * dsilu
        dsh = _shifts_anticausal(dpre, 4, tm)
        o_ref[...] = _wsum(w, dsh).astype(BF16)
        dp = dpre[:tm]
        _acc_rows(dw_ref, [_colsum(dp * q[:tm]) for q in sh], i == 0)
        _acc_out(db_ref, _colsum(dp), i == 0)

    return R.call(
        body, "xbc_bwd", DX // _XW,
        [R.tile(_XW), R.next(_XW), R.tile(_XW, cb), R.prev(_XW, cb), R.next(_XW, cb),
         R.colvec(4, _XW), R.colvec(1, _XW), ANY],
        [R.tile(_XW, cb), R.colvec(4, _XW), R.colvec(1, _XW)],
        [jax.ShapeDtypeStruct(dproj.shape, BF16), jax.ShapeDtypeStruct((4, DX), F32),
         jax.ShapeDtypeStruct((1, DX), F32)],
        (dact, dact, proj, proj, proj, conv_w, conv_b, dproj), aliases={7: 0})


def _softplus(x):
    return jnp.maximum(x, 0.0) + jnp.log(1.0 + jnp.exp(-jnp.abs(x)))


def _dt_rows(T):
    return min(8 * CH, T // 2)


def _dt_fwd(dt_raw, dt_bias_p, a_log_p):
    T = dt_raw.shape[0]
    rows = _dt_rows(T)

    def body(r_ref, b_ref, al_ref, dt_ref, ac_ref, acT_ref):
        dt = _softplus(r_ref[...] + b_ref[...])
        s = dt * (-jnp.exp(al_ref[...]))
        row = lax.broadcasted_iota(jnp.int32, (rows, LANES), 0) % CH
        k = 1
        while k < CH:
            s = s + jnp.where(row >= k, pltpu.roll(s, k, 0), 0.0)
            k *= 2
        dt_ref[...] = dt
        ac_ref[...] = s
        for q in range(0, rows, CH):
            acT_ref[q:q + CH] = s[q:q + CH].T

    blk = pl.BlockSpec((rows, LANES), lambda i: (i, 0))
    vec = pl.BlockSpec((1, LANES), lambda i: (0, 0))
    return pl.pallas_call(
        body, name="dt_fwd", grid=(T // rows,), in_specs=[blk, vec, vec], out_specs=[blk, blk, blk],
        out_shape=[jax.ShapeDtypeStruct((T, LANES), F32)] * 3, compiler_params=_params(("parallel",)),
    )(dt_raw, dt_bias_p, a_log_p)


def _dt_bwd(dacum, ddt_x, dt_raw, dt_bias_p, a_log_p, dproj):
    T = dt_raw.shape[0]
    rows = _dt_rows(T)
    nc = T // rows

    def body(da_ref, dx_ref, r_ref, b_ref, al_ref, _alias, o_ref, db_ref, dal_ref):
        i = pl.program_id(0)
        a = -jnp.exp(al_ref[...])
        z = r_ref[...] + b_ref[...]
        dt = _softplus(z)
        s = da_ref[...]
        row = lax.broadcasted_iota(jnp.int32, (rows, LANES), 0) % CH
        k = 1
        while k < CH:
            s = s + jnp.where(row < CH - k, pltpu.roll(s, rows - k, 0), 0.0)
            k *= 2
        ddt = s * a + dx_ref[...]
        draw = ddt * _sigmoid(z)
        o_ref[:, :LANES] = draw.astype(BF16)
        o_ref[:, LANES:] = jnp.zeros((rows, NIP - OFF_DT - LANES), BF16)
        _acc_out(db_ref, _colsum(draw), i == 0)
        _acc_out(dal_ref, _colsum(s * dt), i == 0)

        @pl.when(i == nc - 1)
        def _():
            dal_ref[...] = dal_ref[...] * a

    blk = pl.BlockSpec((rows, LANES), lambda i: (i, 0))
    vec = pl.BlockSpec((1, LANES), lambda i: (0, 0))
    oblk = pl.BlockSpec((rows, NIP - OFF_DT), lambda i: (i, OFF_DT // (NIP - OFF_DT)))
    return pl.pallas_call(
        body, name="dt_bwd", grid=(nc,), in_specs=[blk, blk, blk, vec, vec, ANY], out_specs=[oblk, vec, vec],
        out_shape=[jax.ShapeDtypeStruct(dproj.shape, BF16), jax.ShapeDtypeStruct((1, LANES), F32),
                   jax.ShapeDtypeStruct((1, LANES), F32)],
        input_output_aliases={5: 0}, compiler_params=_params(("arbitrary",)),
    )(dacum, ddt_x, dt_raw, dt_bias_p, a_log_p, dproj)


_GW = DI // NG
_HG = NH // NG
_NEG = -1e30


def _interleave(gens):
    out, live = [None] * len(gens), list(range(len(gens)))
    while live:
        for i in list(live):
            try:
                next(gens[i])
            except StopIteration as stop:
                out[i] = stop.value
                live.remove(i)
    return out


def _pair_lanes(left, v0, v1):
    return jnp.where(left, v0, v1)


def _hi_lo(v):
    hi = v.astype(BF16)
    return jnp.concatenate([hi, (v - hi.astype(F32)).astype(BF16)], axis=1)


def _head_spread():
    row = lax.broadcasted_iota(jnp.int32, (2 * LANES, 1), 0) % LANES
    return (row == lax.broadcasted_iota(jnp.int32, (1, DI), 1) // HP).astype(BF16)


def _ssd_specs(T, rev):
    nc = T // CH
    cm = (lambda c: nc - 1 - c) if rev else (lambda c: c)
    bw = NG * NS
    return dict(
        xs=pl.BlockSpec((CH, DI), lambda c: (cm(c), 0)),
        bm=pl.BlockSpec((CH, bw), lambda c: (cm(c), DI // bw)),
        cmat=pl.BlockSpec((CH, bw), lambda c: (cm(c), DI // bw + 1)),
        xbc=pl.BlockSpec((CH, DX), lambda c: (cm(c), 0)),
        col=pl.BlockSpec((CH, LANES), lambda c: (cm(c), 0)),
        dsk=pl.BlockSpec((1, DI), lambda c: (0, 0)),
        state=pl.BlockSpec((1, NS, DI), lambda c: (cm(c), 0, 0)),
    )


def _last(ref, lo, hi):
    return ref.at[(slice(None),) * (len(ref.shape) - 1) + (slice(lo, hi),)]


def _group_views(g, wide, narrow):
    return [_last(r, g * _GW, (g + 1) * _GW) for r in wide] + [_last(r, g * NS, (g + 1) * NS) for r in narrow]


def _ssd_fwd(xact, dt, acum, acumT, dsk_rep, proj, norm_w):
    T = xact.shape[0]
    nc = T // CH
    sp = _ssd_specs(T, False)

    def body(*refs):
        xs, bm, cmat, dtr, acr, actr, dsk, zr, nw, spread_ref, y, yn, spv, S_ref = refs

        @pl.when(pl.program_id(0) == 0)
        def _():
            S_ref[...] = jnp.zeros_like(S_ref)

        ac = acr[...]
        cols = [_hi_lo(v) for v in (dtr[...], jnp.exp(ac), jnp.exp(ac[CH - 1:CH, :] - ac))]
        _interleave([group(g * _HG, cols, ac, actr[...], _last(spread_ref, g * _GW, (g + 1) * _GW),
                           *_group_views(g, (xs, dsk, zr, nw, y, yn, spv, S_ref), (bm, cmat))) for g in range(NG)])

    def group(hb, cols, ac, acT, spread_ref, xs_ref, dsk_ref, z_ref, nw_ref, y_ref, yn_ref, sp_ref, S_ref, b_ref, c_ref):
        dtl, eal, dtel = (_dot(v, spread_ref[...]) for v in cols)
        Bm, Cm = b_ref[...], c_ref[...]
        S = S_ref[...]
        sp_ref[0] = S
        cb = _dot(Cm, Bm, NT)
        CS = _dot(Cm, S.astype(BF16))
        row = lax.broadcasted_iota(jnp.int32, (CH, CH), 0)
        col = lax.broadcasted_iota(jnp.int32, (CH, CH), 1)
        tril = row >= col
        left = col < HP
        xd_parts = []
        for p in range(_HG // 2):
            sl = slice(p * LANES, (p + 1) * LANES)
            j0, j1 = hb + 2 * p, hb + 2 * p + 1
            xp = xs_ref[:, sl].astype(F32)
            a0, a1 = ac[:, j0:j0 + 1], ac[:, j1:j1 + 1]
            X = xp * dtl[:, sl]
            Xb = X.astype(BF16)
            Ws = [(cb * jnp.exp(jnp.where(tril, aj - acT[j:j + 1, :], _NEG))).astype(BF16)
                  for j, aj in ((j0, a0), (j1, a1))]
            Xs = [jnp.where(m, Xb, jnp.zeros_like(Xb)) for m in (left, jnp.logical_not(left))]
            yield
            yd = _dot(jnp.concatenate(Ws, axis=1), jnp.concatenate(Xs, axis=0))
            yield
            y = yd + eal[:, sl] * CS[:, sl] + dsk_ref[:, sl] * xp
            y_ref[:, sl] = y.astype(BF16)
            xd_parts.append(X * dtel[:, sl])
        Xd = jnp.concatenate(xd_parts, axis=1).astype(BF16)
        S_ref[...] = eal[CH - 1:CH, :] * S + _dot(Bm, Xd, TN)
        yield
        z = z_ref[...].astype(F32)
        yf = y_ref[...].astype(F32) * z * _sigmoid(z)
        r = lax.rsqrt(jnp.mean(yf * yf, axis=-1, keepdims=True) + EPS)
        yn_ref[...] = (yf * r * nw_ref[...]).astype(BF16)

    zspec = pl.BlockSpec((CH, DI), lambda c: (c, OFF_Z // DI))
    return pl.pallas_call(
        body, name="ssd_fwd", grid=(nc,),
        in_specs=[sp["xs"], sp["bm"], sp["cmat"], sp["col"], sp["col"], sp["col"], sp["dsk"], zspec, sp["dsk"],
                  pl.BlockSpec((2 * LANES, DI), lambda c: (0, 0))],
        out_specs=[sp["xs"], sp["xs"], sp["state"]],
        out_shape=[jax.ShapeDtypeStruct((T, DI), BF16), jax.ShapeDtypeStruct((T, DI), BF16),
                   jax.ShapeDtypeStruct((nc, NS, DI), F32)],
        scratch_shapes=[pltpu.VMEM((NS, DI), F32)],
        compiler_params=_params(("arbitrary",)),
    )(xact, xact, xact, dt, acum, acumT, dsk_rep, proj, norm_w, _head_spread())


def _ssd_bwd(dn, y, proj, norm_w, dproj, xact, dt, acum, acumT, dsk_rep, sprev):
    T = xact.shape[0]
    nc = T // CH
    sp = _ssd_specs(T, True)

    def body(*refs):
        (xs, bm, cmat, dtr, acr, actr, dsk, dnr, yr, zr, nw, spv, _alias, lanes_of_ref, rows_of_ref, spread_ref,
         dxa, ddtx, dAc, dskacc, dzr, dnw, dS_ref) = refs
        first = pl.program_id(0) == 0

        @pl.when(first)
        def _():
            dS_ref[...] = jnp.zeros_like(dS_ref)

        dbc = _last(dxa, DI, DX)
        ddtx_sum = jnp.zeros((CH, LANES), F32)
        dAc_sum = jnp.zeros((CH, LANES), F32)
        ac = acr[...]
        cols = [_hi_lo(v) for v in (dtr[...], jnp.exp(ac), jnp.exp(ac[CH - 1:CH, :] - ac))]
        for a, b in _interleave([group(first, g * _HG, cols, ac, actr[...],
                                       lanes_of_ref.at[g * _GW:(g + 1) * _GW],
                                       rows_of_ref.at[g * _HG * CH:(g + 1) * _HG * CH],
                                       _last(spread_ref, g * _GW, (g + 1) * _GW),
                                       *_group_views(g, (xs, dsk, dnr, yr, zr, nw, dzr, dnw, spv, dxa, dskacc, dS_ref),
                                                     (bm, cmat, dbc, _last(dbc, NG * NS, 2 * NG * NS))))
                                 for g in range(NG)]):
            ddtx_sum, dAc_sum = ddtx_sum + a, dAc_sum + b
        ddtx[...] = ddtx_sum
        dAc[...] = dAc_sum

    def group(first, hb, cols, ac, acT, lanes_of_ref, rows_of_ref, spread_ref, xs_ref, dsk_ref, dn_ref, y_ref, z_ref, nw_ref, dz_ref, dnw_ref, sp_ref, dx_ref,
              dskacc_ref, dS_ref, b_ref, c_ref, dB_ref, dC_ref):
        z = z_ref[...].astype(F32)
        yv = y_ref[...].astype(F32)
        sg = _sigmoid(z)
        silu = z * sg
        yf = yv * silu
        rn = lax.rsqrt(jnp.mean(yf * yf, axis=-1, keepdims=True) + EPS)
        yh = yf * rn
        dnv = dn_ref[...].astype(F32)
        dyh = dnv * nw_ref[...]
        dyf = rn * (dyh - yh * jnp.mean(dyh * yh, axis=-1, keepdims=True))
        dyg = dyf * silu
        dz_ref[...] = (dyf * yv * sg * (1.0 + z * (1.0 - sg))).astype(BF16)
        _acc_out(dnw_ref, _colsum(dnv * yh), first)
        Bm, Cm = b_ref[...], c_ref[...]
        S = sp_ref[0]
        dS = dS_ref[...]
        Sb, dSb = S.astype(BF16), dS.astype(BF16)
        cb = _dot(Cm, Bm, NT)
        cbT = _dot(Bm, Cm, NT)
        CmT = Cm.T
        CS = _dot(Cm, Sb)
        T1 = _dot(Bm, dSb)
        yield
        row = lax.broadcasted_iota(jnp.int32, (CH, CH), 0)
        col = lax.broadcasted_iota(jnp.int32, (CH, CH), 1)
        tril = row >= col
        triu = row <= col
        left = col < HP
        lastrow = lax.broadcasted_iota(jnp.int32, (CH, 1), 0) == CH - 1
        dCB = jnp.zeros((CH, CH), F32)
        dCBT = jnp.zeros((CH, CH), F32)
        xd_parts, dye_parts, dec_parts, dsk_parts, dxx_parts, gr_parts, end_parts, qd_parts = ([] for _ in range(8))
        dtls, eals, dtels = (_dot(v, spread_ref[...]) for v in cols)
        for p in range(_HG // 2):
            sl = slice(p * LANES, (p + 1) * LANES)
            j0, j1 = hb + 2 * p, hb + 2 * p + 1
            xp = xs_ref[:, sl].astype(F32)
            dyp = dyg[:, sl]
            a0, a1 = ac[:, j0:j0 + 1], ac[:, j1:j1 + 1]
            dtl, eal, dtel = dtls[:, sl], eals[:, sl], dtels[:, sl]
            X = xp * dtl
            Xb = X.astype(BF16)
            T1p = T1[:, sl]
            Rm = T1p * dtel * X
            decp = eal[CH - 1:CH, :]
            gr_parts.append(dyp * (eal * CS[:, sl]) - Rm)
            end_parts.append(Rm + decp * (dS[:, sl] * S[:, sl]))
            dXd = jnp.zeros((CH, LANES), F32)
            for j, aj, mask in ((j0, a0, left), (j1, a1, jnp.logical_not(left))):
                dYm = jnp.where(mask, dyp, 0.0).astype(BF16)
                dWm = _dot(dYm, Xb, NT)
                dWmT = _dot(Xb, dYm, NT)
                yield
                e = aj - acT[j:j + 1, :]
                P = dWm * jnp.exp(jnp.where(tril, e, _NEG))
                LmT = jnp.exp(jnp.where(triu, -e, _NEG))
                PT = dWmT * LmT
                dCB = dCB + P
                dCBT = dCBT + PT
                yield
                dXd = dXd + _dot((cbT * LmT).astype(BF16), dYm)
                qd_parts.append((P * cb - PT * cbT).astype(BF16))
                yield
            dX = dXd + dtel * T1p
            dxx_parts.append(dX * xp)
            dx_ref[:, sl] = (dX * dtl + dsk_ref[:, sl] * dyp).astype(BF16)
            dsk_parts.append(_colsum(dyp * xp))
            xd_parts.append(X * dtel)
            dye_parts.append(dyp * eal)
            dec_parts.append(decp)
            yield
        Xd = jnp.concatenate(xd_parts, axis=1).astype(BF16)
        dYe = jnp.concatenate(dye_parts, axis=1).astype(BF16)
        dec = jnp.concatenate(dec_parts, axis=1)
        def lane_sums(parts):
            return _dot(jnp.concatenate(parts, axis=1).astype(BF16), lanes_of_ref[...])
        ddtx = lane_sums(dxx_parts)
        dAc = (_dot(jnp.concatenate(qd_parts, axis=1), rows_of_ref[...]) + lane_sums(gr_parts)
               + jnp.where(lastrow, _colsum(lane_sums(end_parts)), 0.0))
        dC_ref[...] = (_dot(dCB.astype(BF16), Bm) + _dot(dYe, Sb, NT)).astype(BF16)
        dB_ref[...] = (_dot(dCBT.astype(BF16), Cm) + _dot(Xd, dSb, NT)).astype(BF16)
        dS_ref[...] = _dot(CmT, dYe) + dec * dS
        _acc_out(dskacc_ref, jnp.concatenate(dsk_parts, axis=1), first)
        return ddtx, dAc

    zspec = pl.BlockSpec((CH, DI), lambda c: (nc - 1 - c, OFF_Z // DI))
    head = lax.broadcasted_iota(jnp.int32, (1, LANES), 1)
    lanes_of = (lax.broadcasted_iota(jnp.int32, (DI, 1), 0) // HP == head).astype(BF16)
    rows_of = (lax.broadcasted_iota(jnp.int32, (NH * CH, 1), 0) // CH == head).astype(BF16)
    return pl.pallas_call(
        body, name="ssd_bwd", grid=(nc,),
        in_specs=[sp["xs"], sp["bm"], sp["cmat"], sp["col"], sp["col"], sp["col"], sp["dsk"], sp["xs"], sp["xs"],
                  zspec, sp["dsk"], sp["state"], ANY, pl.BlockSpec(lanes_of.shape, lambda c: (0, 0)),
                  pl.BlockSpec(rows_of.shape, lambda c: (0, 0)), pl.BlockSpec((2 * LANES, DI), lambda c: (0, 0))],
        out_specs=[sp["xbc"], sp["col"], sp["col"], sp["dsk"], zspec, sp["dsk"]],
        out_shape=[jax.ShapeDtypeStruct((T, DX), BF16), jax.ShapeDtypeStruct((T, LANES), F32),
                   jax.ShapeDtypeStruct((T, LANES), F32), jax.ShapeDtypeStruct((1, DI), F32),
                   jax.ShapeDtypeStruct(dproj.shape, BF16), jax.ShapeDtypeStruct((1, DI), F32)],
        scratch_shapes=[pltpu.VMEM((NS, DI), F32)], input_output_aliases={12: 4},
        compiler_params=_params(("arbitrary",)),
    )(xact, xact, xact, dt, acum, acumT, dsk_rep, dn, y, proj, norm_w, sprev, dproj, lanes_of, rows_of, _head_spread())


def _merge_fwd_epilogue(proj, ya):
    T = proj.shape[0]

    def fn(ysv, ins, outs, first):
        g_ref, ya_ref = ins
        m_ref, ys_ref = outs
        ga = _sigmoid(g_ref[:, :D].astype(F32))
        gs = _sigmoid(g_ref[:, D:].astype(F32))
        m_ref[...] = (ga * ya_ref[...].astype(F32) + gs * ysv).astype(BF16)
        ys_ref[...] = ysv.astype(BF16)

    return _Epilogue(fn, (proj, ya), (((T, D), BF16), ((T, D), BF16)), 10 * D, in_windows={0: (OFF_G, 2 * D)})


def _merge_bwd_epilogue(proj, ya, ys, ncols):
    T = proj.shape[0]

    def fn(d, ins, outs, first):
        g_ref, ya_ref, ys_ref = ins
        dg_ref, dya_ref, dys_ref = outs
        ga = _sigmoid(g_ref[:, :D].astype(F32))
        gs = _sigmoid(g_ref[:, D:].astype(F32))
        dya_ref[...] = (d * ga).astype(BF16)
        dys_ref[...] = (d * gs).astype(BF16)
        dg_ref[:, :D] = (d * ya_ref[...].astype(F32) * ga * (1.0 - ga)).astype(BF16)
        dg_ref[:, D:] = (d * ys_ref[...].astype(F32) * gs * (1.0 - gs)).astype(BF16)

    window = (OFF_G, 2 * D)
    return _Epilogue(fn, (proj, ya, ys), (((T, ncols), BF16), ((T, D), BF16), ((T, D), BF16)), 16 * D,
                     in_windows={0: window}, out_windows={0: window})


_FW = 1408
_FB = FF // _FW


def _ffn_act_fwd(hv, conv_w, conv_b):
    T = hv.shape[0]
    R = _Rows(T, 256)
    tm = R.tm

    def body(h1_ref, h1p_ref, h3_ref, w_ref, b_ref, o_ref):
        keep = (pl.program_id(1) > 0).astype(F32)
        ext = jnp.concatenate([h1p_ref[...].astype(F32) * keep, h1_ref[...].astype(F32)], axis=0)
        pre = _wsum(w_ref[...], _shifts_causal(ext, 3, tm)) + b_ref[...]
        o_ref[...] = (pre * _sigmoid(pre) * h3_ref[...].astype(F32)).astype(BF16)

    return R.call(body, "ffn_act_fwd", _FB,
                  [R.tile(_FW), R.prev(_FW), R.tile(_FW, _FB), R.colvec(3, _FW), R.colvec(1, _FW)],
                  R.tile(_FW), jax.ShapeDtypeStruct((T, FF), BF16), (hv, hv, hv, conv_w, conv_b))


def _ffn_act_bwd(dg, hv, conv_w, conv_b):
    T = hv.shape[0]
    R = _Rows(T, 256)
    tm = R.tm

    def body(dg_ref, dgn_ref, h1_ref, h1p_ref, h1n_ref, h3_ref, h3n_ref, w_ref, b_ref, dh3_ref, dh1_ref, dw_ref,
             db_ref):
        i = pl.program_id(1)
        keep_p = (i > 0).astype(F32)
        keep_n = (i < R.nrow - 1).astype(F32)
        w = w_ref[...]
        ext = jnp.concatenate([h1p_ref[...].astype(F32) * keep_p, h1_ref[...].astype(F32),
                               h1n_ref[...].astype(F32)], axis=0)
        sh = _shifts_causal(ext, 3, tm + HALO)
        pre = _wsum(w, sh) + b_ref[...]
        s = _sigmoid(pre)
        d = jnp.concatenate([dg_ref[...].astype(F32), dgn_ref[...].astype(F32) * keep_n], axis=0)
        h3 = jnp.concatenate([h3_ref[...].astype(F32), h3n_ref[...].astype(F32)], axis=0)
        dh3_ref[...] = (d[:tm] * pre[:tm] * s[:tm]).astype(BF16)
        dpre = d * h3 * s * (1.0 + pre * (1.0 - s))
        dh1_ref[...] = _wsum(w, _shifts_anticausal(dpre, 3, tm)).astype(BF16)
        dp = dpre[:tm]
        _acc_rows(dw_ref, [_colsum(dp * q[:tm]) for q in sh], i == 0)
        _acc_out(db_ref, _colsum(dp), i == 0)

    return R.call(
        body, "ffn_act_bwd", _FB,
        [R.tile(_FW), R.next(_FW), R.tile(_FW), R.prev(_FW), R.next(_FW), R.tile(_FW, _FB), R.next(_FW, _FB),
         R.colvec(3, _FW), R.colvec(1, _FW)],
        [R.tile(_FW), R.tile(_FW), R.colvec(3, _FW), R.colvec(1, _FW)],
        [jax.ShapeDtypeStruct((T, FF), BF16), jax.ShapeDtypeStruct((T, FF), BF16),
         jax.ShapeDtypeStruct((3, FF), F32), jax.ShapeDtypeStruct((1, FF), F32)],
        (dg, dg, hv, hv, hv, hv, hv, conv_w, conv_b))


def _final_loss_epilogue(w, target):
    T = target.shape[0]

    def fn(xv, ins, outs, first):
        w_ref, t_ref = ins
        l_ref, dh_ref, dhb_ref, dw_ref = outs
        wv = w_ref[...]
        r = lax.rsqrt(jnp.mean(xv * xv, axis=-1, keepdims=True) + EPS)
        xh = xv * r
        err = xh * wv - t_ref[...]
        part = 0.5 * jnp.sum(jnp.mean(err * err, axis=-1, keepdims=True), axis=0, keepdims=True)
        _acc_out(l_ref, jnp.broadcast_to(part, l_ref.shape), first)
        dy = err * (1.0 / D)
        dxh = dy * wv
        dh = r * (dxh - xh * jnp.mean(dxh * xh, axis=-1, keepdims=True))
        dh_ref[...] = dh
        dhb_ref[...] = dh.astype(BF16)
        _acc_out(dw_ref, _colsum(dy * xh), first)

    return _Epilogue(fn, (w, target),
                     (((8, LANES), F32), ((T, D), F32), ((T, D), BF16), ((1, D), F32)), 10 * D)


def _pad_lanes(v, n=LANES):
    return jnp.pad(v, ((0, 0), (0, n - v.shape[1])))


class _Hooks:
    def before_in_proj(self, w_in):
        return w_in

    def late_weights(self, wts, after):
        return wts

    def grads_ready(self, grads, tie):
        return tie

    def mark(self, name, value):
        pass


def _local_step(x, target, wts, hooks=None):
    hooks = hooks or _Hooks()
    T = x.shape[0]
    w_in = wts["w_in"]
    dt_bias_p, a_log_p = _pad_lanes(wts["dt_bias"]), _pad_lanes(wts["a_log"])
    dsk_rep = jnp.repeat(wts["d_skip"], HP, axis=1)

    w_in = hooks.before_in_proj(w_in)
    proj, u, dt_raw = _norm_matmul(x, wts["norm_mix_w"], w_in, "norm_mm_in", w_in[:, OFF_DT:OFF_DT + LANES])
    ya_in = _branch_a_fwd(proj, wts["conv_a_w"])
    xact = _xbc_fwd(proj, wts["ssd_conv_w"], wts["ssd_conv_b"])
    dt, acum, acumT = _dt_fwd(dt_raw, dt_bias_p, a_log_p)
    y_ssd, yn, sprev = _ssd_fwd(xact, dt, acum, acumT, dsk_rep, proj, wts["ssd_norm_w"])
    late = hooks.late_weights(wts, yn)
    w_a_out, w_s_out, w_o, w_up, w_down = (late[k] for k in ("w_a_out", "w_s_out", "w_o", "w_up", "w_down"))
    y_a = _matmul(ya_in, w_a_out, mode="nn", out_dtype=BF16, name="mm_a_out")
    merged, y_s = _matmul(yn, w_s_out, mode="nn", out_dtype=BF16, name="mm_s_out_merge",
                          epilogue=_merge_fwd_epilogue(proj, y_a))
    h1 = _matmul(merged, w_o, mode="nn", out_dtype=F32, name="mm_o", residual=x)
    hv, v = _norm_matmul(h1, wts["norm_ffn_w"], w_up, "norm_mm_up")
    gact = _ffn_act_fwd(hv, wts["ffn_conv_w"], wts["ffn_conv_b"])
    loss, dh2, dh2b, g_final = _matmul(gact, w_down, mode="nn", out_dtype=F32, name="mm_down_loss", residual=h1,
                                       epilogue=_final_loss_epilogue(wts["final_norm_w"], target))

    grads = {"final_norm_w": g_final}
    grads["w_down"] = _matmul(gact, dh2b, mode="tn", out_dtype=F32, name="mm_down_dw")
    dgact = _matmul(dh2b, w_down, mode="nt", out_dtype=BF16, name="mm_down_dx")
    dh3, dh1c, grads["ffn_conv_w"], grads["ffn_conv_b"] = _ffn_act_bwd(dgact, hv, wts["ffn_conv_w"], wts["ffn_conv_b"])
    grads["w_up"] = (_matmul(v, dh1c, mode="tn", out_dtype=F32, name="mm_up_dw1"),
                     _matmul(v, dh3, mode="tn", out_dtype=F32, name="mm_up_dw3"))
    dv = _matmul(dh1c, w_up, mode="nt", out_dtype=F32, name="mm_up_dx1")
    dh1, dh1b, grads["norm_ffn_w"] = _matmul(
        dh3, w_up, mode="nt", out_dtype=F32, name="mm_up_dx3_norm", residual=dv, b_k_off=FF,
        epilogue=_rmsnorm_bwd_epilogue(h1, wts["norm_ffn_w"], dh2))
    grads["w_o"] = _matmul(merged, dh1b, mode="tn", out_dtype=F32, name="mm_o_dw")
    dproj, dya, dys = _matmul(dh1b, w_o, mode="nt", out_dtype=BF16, name="mm_o_dx_merge",
                              epilogue=_merge_bwd_epilogue(proj, y_a, y_s, NIP))
    grads["w_a_out"] = _matmul(ya_in, dya, mode="tn", out_dtype=F32, name="mm_a_out_dw")
    dya_in = _matmul(dya, w_a_out, mode="nt", out_dtype=BF16, name="mm_a_out_dx")
    dproj, grads["conv_a_w"] = _branch_a_bwd(dya_in, proj, wts["conv_a_w"], dproj)
    grads["w_s_out"] = _matmul(yn, dys, mode="tn", out_dtype=F32, name="mm_s_out_dw")
    dys = hooks.grads_ready({k: grads[k] for k in ("w_a_out", "w_s_out", "w_o", "w_up", "w_down")}, dys)
    dyn =_matmul(dys, w_s_out, mode="nt", out_dtype=BF16, name="mm_s_out_dx")
    dxact, ddt_x, dacum, dskl, dproj, grads["ssd_norm_w"] = _ssd_bwd(
        dyn, y_ssd, proj, wts["ssd_norm_w"], dproj, xact, dt, acum, acumT, dsk_rep, sprev)
    hooks.mark("ssd_bwd", dxact)
    grads["d_skip"] = dskl.reshape(NH, HP).sum(axis=1).reshape(1, NH)
    dproj, grads["ssd_conv_w"], grads["ssd_conv_b"] = _xbc_bwd(dxact, proj, wts["ssd_conv_w"], wts["ssd_conv_b"], dproj)
    dproj, g_dtb, g_alog = _dt_bwd(dacum, ddt_x, dt_raw, dt_bias_p, a_log_p, dproj)
    grads["dt_bias"], grads["a_log"] = g_dtb[:, :NH], g_alog[:, :NH]
    grads["w_in"] = _matmul(u, dproj, mode="tn", out_dtype=F32, name="mm_in_dw")
    dproj = hooks.grads_ready({"w_in": grads["w_in"]}, dproj)
    grad_x, _, grads["norm_mix_w"] = _matmul(dproj, w_in, mode="nt", out_dtype=F32, name="mm_in_dx_norm",
                                             epilogue=_rmsnorm_bwd_epilogue(x, wts["norm_mix_w"], dh1))
    return loss, grad_x, grads


def _permute_w_in(slabs):
    cs = slabs.shape[2]
    pieces = []
    for o, n, no in sorted(_SEGS, key=lambda seg: seg[2]):
        for s in range(slabs.shape[0]):
            lo, hi = max(o, s * cs), min(o + n, (s + 1) * cs)
            if lo < hi:
                pieces.append(slabs[s][:, lo - s * cs:hi - s * cs])
    pieces.append(jnp.zeros((slabs.shape[1], NIP - OFF_DT - _SEGS[-1][1]), slabs.dtype))
    return jnp.concatenate(pieces, axis=1)


def _unpermute_w_in(g):
    cs = NI // NCHIP
    slabs = []
    for s in range(NCHIP):
        pieces = []
        for o, n, no in sorted(_SEGS):
            lo, hi = max(o, s * cs), min(o + n, (s + 1) * cs)
            if lo < hi:
                pieces.append(g[:, no + lo - o:no + hi - o])
        slabs.append(jnp.concatenate(pieces, axis=1))
    return jnp.stack(slabs)


MESH = pl.DeviceIdType.MESH
NCHIP = 4
NDEV = 8

_W_IN = (("w_in", D, NI // NCHIP, 1),)
_W_REST = (("w_a_out", D // NCHIP, D, 0), ("w_s_out", DI // NCHIP, D, 0), ("w_o", D // NCHIP, D, 0),
           ("w_up", D, 2 * FF // NCHIP, 1), ("w_down", FF // NCHIP, D, 0))


def _coords():
    return lax.axis_index("x"), lax.axis_index("y"), lax.axis_index("c")


def _other_chips(x, y):
    return [(1 - x, y), (x, 1 - y), (1 - x, 1 - y)]


def _ag_weights(shard):
    nrows = shard.shape[0]
    hr = nrows // 2

    def body(x_ref, out_ref, send_sems, recv_sems, local_sem):
        x, y, c = _coords()
        me = 2 * x + y
        chips = _other_chips(x, y)

        def rows(s, h):
            return out_ref.at[s, pl.ds(h * hr, hr), :]

        def copy(k, s, h, to, src=None):
            return pltpu.make_async_remote_copy(
                src_ref=rows(s, h) if src is None else src, dst_ref=rows(s, h),
                send_sem=send_sems.at[k], recv_sem=recv_sems.at[k], device_id=to, device_id_type=MESH)

        mine = pltpu.make_async_copy(x_ref, out_ref.at[me], local_sem)
        mine.start()
        first = [copy(k, me, c, (*chip, c), src=x_ref.at[pl.ds(c * hr, hr), :]) for k, chip in enumerate(chips)]
        for cp in first:
            cp.start()
        passed = []
        for k, chip in enumerate(chips):
            s = 2 * chip[0] + chip[1]
            copy(k, s, c, (x, y, c)).wait_recv()
            fwd = copy(3 + k, s, c, (x, y, 1 - c))
            fwd.start()
            passed.append(fwd)
        for k, chip in enumerate(chips):
            copy(3 + k, 2 * chip[0] + chip[1], 1 - c, (x, y, c)).wait_recv()
        for cp in first + passed:
            cp.wait_send()
        mine.wait()

    return pl.pallas_call(
        body, name="ag_weights", in_specs=[ANY], out_specs=ANY,
        out_shape=jax.ShapeDtypeStruct((NCHIP,) + shard.shape, shard.dtype),
        scratch_shapes=[pltpu.SemaphoreType.DMA((6,)), pltpu.SemaphoreType.DMA((6,)), pltpu.SemaphoreType.DMA],
        compiler_params=pltpu.CompilerParams(has_side_effects=True),
    )(shard)


HBM = pl.BlockSpec(memory_space=pltpu.HBM)
SEM = pl.BlockSpec(memory_space=pltpu.SEMAPHORE)
_EFFECT = pltpu.SideEffectType.DATAFLOW_SIDE_EFFECTING
_NCOPY = NCHIP - 1


def _plan_bcast(src_ref, land_ref, send_sems, recv_sems, base):
    x, y, c = _coords()
    sends, lands = [], []
    for k, chip in enumerate(_other_chips(x, y)):
        def copy(slot):
            return pltpu.make_async_remote_copy(
                src_ref=src_ref, dst_ref=land_ref.at[slot], send_sem=send_sems.at[base + k],
                recv_sem=recv_sems.at[base + k], device_id=(*chip, c), device_id_type=MESH)
        sends.append(copy(2 * x + y))
        lands.append(copy(2 * chip[0] + chip[1]))
    return sends, lands


def _plan_scatter(src_ref, land_ref, send_sems, recv_sems, base):
    x, y, c = _coords()
    cps = [pltpu.make_async_remote_copy(
        src_ref=src_ref.at[2 * chip[0] + chip[1]], dst_ref=land_ref.at[k], send_sem=send_sems.at[base + k],
        recv_sem=recv_sems.at[base + k], device_id=(*chip, c), device_id_type=MESH)
        for k, chip in enumerate(_other_chips(x, y))]
    return cps, cps


def _plan_all(plan, refs, n):
    sends, lands = [], []
    for t in range(n):
        s, l = plan(refs[t], refs[n + t], refs[2 * n], refs[2 * n + 1], t * _NCOPY)
        sends += s
        lands += l
    return sends, lands


def _split_start(name, srcs, lands, plan):
    n = len(srcs)

    def body(*refs):
        for cp in _plan_all(plan, refs, n)[0]:
            cp.start()
        refs[-1][...] = jnp.zeros_like(refs[-1])

    arrays = list(srcs) + list(lands)
    outs = pl.pallas_call(
        body, name=name,
        out_shape=(pltpu.SemaphoreType.DMA((n * _NCOPY,)), pltpu.SemaphoreType.DMA((n * _NCOPY,)),
                   *[pltpu.HBM(a.shape, a.dtype) for a in arrays], jax.ShapeDtypeStruct((8, LANES), F32)),
        in_specs=(HBM,) * (2 * n),
        out_specs=(SEM, SEM) + (HBM,) * (2 * n) + (pl.BlockSpec(memory_space=pltpu.VMEM),),
        input_output_aliases={t: 2 + t for t in range(2 * n)},
        compiler_params=pltpu.CompilerParams(has_side_effects=_EFFECT),
    )(*[pltpu.with_memory_space_constraint(a, pltpu.HBM) for a in arrays])
    return (outs[0], outs[1], tuple(outs[2:2 + 2 * n])), outs[-1]


def _split_wait(name, handle, after, plan):
    send_sems, recv_sems, arrays = handle
    n = len(arrays) // 2

    def body(*refs):
        sends, lands = _plan_all(plan, refs[:2 * n] + refs[2 * n:2 * n + 2], n)
        for cp in sends:
            cp.wait_send()
        for cp in lands:
            cp.wait_recv()

    outs = pl.pallas_call(
        body, name=name, out_shape=tuple(pltpu.HBM(a.shape, a.dtype) for a in arrays),
        in_specs=(HBM,) * (2 * n) + (SEM, SEM, ANY), out_specs=(HBM,) * (2 * n),
        input_output_aliases={t: t for t in range(2 * n)},
        compiler_params=pltpu.CompilerParams(has_side_effects=_EFFECT),
    )(*arrays, send_sems, recv_sems, after)
    return outs[:n], outs[n:]


def _tie(x, token, name):
    def body(x_ref, t_ref, o_ref):
        pass

    return pl.pallas_call(
        body, name=name, in_specs=[ANY, pl.BlockSpec(memory_space=pltpu.VMEM)], out_specs=ANY,
        out_shape=jax.ShapeDtypeStruct(x.shape, x.dtype), input_output_aliases={0: 0},
    )(x, token)


def _swap_sibling(ps, name):
    n = len(ps)

    def body(*refs):
        x, y, c = _coords()
        cps = [pltpu.make_async_remote_copy(
            src_ref=refs[t], dst_ref=refs[n + t], send_sem=refs[2 * n].at[t], recv_sem=refs[2 * n + 1].at[t],
            device_id=(x, y, 1 - c), device_id_type=MESH) for t in range(n)]
        for cp in cps:
            cp.start()
        for cp in cps:
            cp.wait()

    return pl.pallas_call(
        body, name=name, in_specs=[ANY] * n, out_specs=[ANY] * n,
        out_shape=[jax.ShapeDtypeStruct(p.shape, p.dtype) for p in ps],
        scratch_shapes=[pltpu.SemaphoreType.DMA((n,)), pltpu.SemaphoreType.DMA((n,))],
        compiler_params=pltpu.CompilerParams(has_side_effects=True),
    )(*ps)


_ADD_BYTES = 7 << 19


def _add_tile(rows, cols):
    best = 32
    for t in range(32, rows + 1, 32):
        if rows % t == 0 and t * cols * 4 <= _ADD_BYTES:
            best = t
    return best


def _add_slabs(pack, land, me, name):
    rows, cols = pack.shape[1:]
    tr = _add_tile(rows, cols)

    def body(me_ref, p_ref, l_ref, o_ref):
        f = lambda r: r.astype(F32)
        o_ref[...] = ((f(p_ref[0]) + f(l_ref[0])) + f(l_ref[1])) + f(l_ref[2])

    return pl.pallas_call(
        body, name=name,
        grid_spec=pltpu.PrefetchScalarGridSpec(
            num_scalar_prefetch=1, grid=(rows // tr,),
            in_specs=[pl.BlockSpec((1, tr, cols), lambda i, me_ref: (me_ref[0], i, 0)),
                      pl.BlockSpec((_NCOPY, tr, cols), lambda i, me_ref: (0, i, 0))],
            out_specs=pl.BlockSpec((tr, cols), lambda i, me_ref: (i, 0))),
        out_shape=jax.ShapeDtypeStruct((rows, cols), F32),
        compiler_params=_params(("parallel",)),
    )(me, pack, land)


_STAGE_W = 1024


def _stage_rows(shapes):
    pieces, r = [], 0
    for i, (k, w) in enumerate(shapes):
        for a in range(k):
            for q in range(0, w, _STAGE_W):
                pieces.append((i, a, q, min(_STAGE_W, w - q), r))
                r += 1
    return pieces, -(-r // 8) * 8


def _gather8(parts, reduce, name):
    shapes = [p.shape for p in parts]
    pieces, rows = _stage_rows(shapes)
    n = len(parts)

    def body(*refs):
        ins, outs = refs[:n], refs[n:2 * n]
        stage, buf, res, send_sems, recv_sems = refs[2 * n:]
        x, y, c = _coords()
        me = 4 * x + 2 * y + c
        stage[...] = jnp.zeros_like(stage)
        for i, a, q, w, r in pieces:
            stage[r:r + 1, 0:w] = ins[i][a:a + 1, q:q + w]
        buf[pl.ds(me, 1)] = stage[...][None]
        cps, lands = [], []
        for k in range(1, NDEV):
            peer = (1 - x if k & 4 else x, 1 - y if k & 2 else y, 1 - c if k & 1 else c)

            def copy(slot):
                return pltpu.make_async_remote_copy(
                    src_ref=stage, dst_ref=buf.at[slot], send_sem=send_sems.at[k - 1],
                    recv_sem=recv_sems.at[k - 1], device_id=peer, device_id_type=MESH)

            cps.append(copy(me))
            lands.append(copy(4 * peer[0] + 2 * peer[1] + peer[2]))
        for cp in cps:
            cp.start()
        for cp, land in zip(cps, lands):
            land.wait_recv()
            cp.wait_send()
        if reduce:
            acc = buf[0]
            for d in range(1, NDEV):
                acc = acc + buf[d]
            res[...] = acc
            for i, a, q, w, r in pieces:
                outs[i][a:a + 1, q:q + w] = res[r:r + 1, 0:w]
        else:
            for i, a, q, w, r in pieces:
                for s in range(NCHIP):
                    outs[i][s, a:a + 1, q:q + w] = buf[2 * s, r:r + 1, 0:w]

    vm = pl.BlockSpec(memory_space=pltpu.VMEM)
    out_shapes = [jax.ShapeDtypeStruct(s if reduce else (NCHIP,) + s, F32) for s in shapes]
    return pl.pallas_call(
        body, name=name, in_specs=[vm] * n, out_specs=[vm] * n, out_shape=out_shapes,
        scratch_shapes=[pltpu.VMEM((rows, _STAGE_W), F32), pltpu.VMEM((NDEV, rows, _STAGE_W), F32),
                        pltpu.VMEM((rows, _STAGE_W), F32), pltpu.SemaphoreType.DMA((NDEV - 1,)),
                        pltpu.SemaphoreType.DMA((NDEV - 1,))],
        compiler_params=pltpu.CompilerParams(has_side_effects=True),
    )(*parts)


def _adamw_update(w_ref, g_ref, m_ref, v_ref, d_ref, mo_ref, vo_ref):
    c1 = 1.0 / (1.0 - ADAM_B1 ** ADAM_STEP)
    c2 = 1.0 / (1.0 - ADAM_B2 ** ADAM_STEP)
    gv = g_ref[...]
    mn = ADAM_B1 * m_ref[...] + (1.0 - ADAM_B1) * gv
    vn = ADAM_B2 * v_ref[...] + (1.0 - ADAM_B2) * (gv * gv)
    d_ref[...] = -ADAM_LR * ((mn * c1) / (jnp.sqrt(vn * c2) + ADAM_EPS) + ADAM_WD * w_ref[...])
    mo_ref[...] = mn
    vo_ref[...] = vn


def _adamw_small(ws, gs, ms, vs):
    n = len(ws)

    def body(*refs):
        for i in range(n):
            _adamw_update(*(refs[j * n + i] for j in range(7)))

    vm = pl.BlockSpec(memory_space=pltpu.VMEM)
    outs = pl.pallas_call(
        body, name="adamw_small", in_specs=[vm] * (4 * n), out_specs=[vm] * (3 * n),
        out_shape=[jax.ShapeDtypeStruct(w.shape, F32) for w in ws] * 3,
    )(*ws, *gs, *ms, *vs)
    return outs[:n], outs[n:2 * n], outs[2 * n:]


def _adamw(w, g_parts, m, v, name):
    rows, cols = w.shape
    tr = rows
    while tr * cols * 4 > (1 << 20) and tr % 16 == 0:
        tr //= 2

    def body(w_ref, ga_ref, gb_ref, m_ref, v_ref, g_ref, d_ref, mo_ref, vo_ref):
        g_ref[...] = ga_ref[...] + gb_ref[...]
        _adamw_update(w_ref, g_ref, m_ref, v_ref, d_ref, mo_ref, vo_ref)

    blk = pl.BlockSpec((tr, cols), lambda i: (i, 0))
    return pl.pallas_call(
        body, name=name, grid=(rows // tr,), in_specs=[blk] * 5, out_specs=[blk] * 4,
        out_shape=[jax.ShapeDtypeStruct((rows, cols), F32)] * 4, compiler_params=_params(("parallel",)),
    )(w, *g_parts, m, v)


def _by_chip(g, rr, cc, axis):
    if isinstance(g, tuple):
        n = NCHIP // len(g)
        return jnp.concatenate([h.reshape(rr, n, cc).transpose(1, 0, 2) for h in g], axis=0)
    return g.reshape(NCHIP, rr, cc) if axis == 0 else g.reshape(rr, NCHIP, cc).transpose(1, 0, 2)


_SMALL_REPL = ("norm_mix_w", "ssd_conv_b", "dt_bias", "a_log", "d_skip", "ssd_norm_w", "norm_ffn_w",
               "ffn_conv_b", "final_norm_w")
_SMALL_CONV = (("conv_a_w", 3, D), ("ssd_conv_w", 4, DX), ("ffn_conv_w", 3, FF))


def kernel(x, norm_mix_w, w_in, conv_a_w, w_a_out, ssd_conv_w, ssd_conv_b, dt_bias, a_log, d_skip, ssd_norm_w, w_s_out, w_o, norm_ffn_w, w_up, ffn_conv_w, ffn_conv_b, w_down, final_norm_w, loss_target, m_norm_mix_w, m_w_in, m_conv_a_w, m_w_a_out, m_ssd_conv_w, m_ssd_conv_b, m_dt_bias, m_a_log, m_d_skip, m_ssd_norm_w, m_w_s_out, m_w_o, m_norm_ffn_w, m_w_up, m_ffn_conv_w, m_ffn_conv_b, m_w_down, m_final_norm_w, v_norm_mix_w, v_w_in, v_conv_a_w, v_w_a_out, v_ssd_conv_w, v_ssd_conv_b, v_dt_bias, v_a_log, v_d_skip, v_ssd_norm_w, v_w_s_out, v_w_o, v_norm_ffn_w, v_w_up, v_ffn_conv_w, v_ffn_conv_b, v_w_down, v_final_norm_w):
    names = ("norm_mix_w", "w_in", "conv_a_w", "w_a_out", "ssd_conv_w", "ssd_conv_b", "dt_bias", "a_log", "d_skip",
             "ssd_norm_w", "w_s_out", "w_o", "norm_ffn_w", "w_up", "ffn_conv_w", "ffn_conv_b", "w_down", "final_norm_w")
    W = dict(zip(names, (norm_mix_w, w_in, conv_a_w, w_a_out, ssd_conv_w, ssd_conv_b, dt_bias, a_log, d_skip,
                         ssd_norm_w, w_s_out, w_o, norm_ffn_w, w_up, ffn_conv_w, ffn_conv_b, w_down, final_norm_w)))
    M = dict(zip(names, (m_norm_mix_w, m_w_in, m_conv_a_w, m_w_a_out, m_ssd_conv_w, m_ssd_conv_b, m_dt_bias, m_a_log,
                         m_d_skip, m_ssd_norm_w, m_w_s_out, m_w_o, m_norm_ffn_w, m_w_up, m_ffn_conv_w, m_ffn_conv_b,
                         m_w_down, m_final_norm_w)))
    V = dict(zip(names, (v_norm_mix_w, v_w_in, v_conv_a_w, v_w_a_out, v_ssd_conv_w, v_ssd_conv_b, v_dt_bias, v_a_log,
                         v_d_skip, v_ssd_norm_w, v_w_s_out, v_w_o, v_norm_ffn_w, v_w_up, v_ffn_conv_w, v_ffn_conv_b,
                         v_w_down, v_final_norm_w)))
    two_d = lambda a: a.reshape(-1, a.shape[-1])
    W2, M2, V2 = ({k: two_d(a) for k, a in t.items()} for t in (W, M, V))
    xi, yi, ci = _coords()
    me = 2 * xi + yi

    meidx = me.reshape(1).astype(jnp.int32)
    state = {}


    class Hooks(_Hooks):
        def before_in_proj(self, w_in):
            return _tie(w_in, state["rest_token"], "tie_ag_rest")

        def late_weights(self, wts, after):
            owns, lands = _split_wait("ag_rest_wait", state["rest"], after, _plan_bcast)
            full = {}
            for (n, rr, cc, axis), own, land in zip(_W_REST, owns, lands):
                slabs = lax.dynamic_update_slice(land, own[None], (me, 0, 0))
                full[n] = slabs.reshape(NCHIP * rr, cc) if axis == 0 else slabs.transpose(1, 0, 2).reshape(rr, NCHIP * cc)
            return {**wts, **full}

        def grads_ready(self, grads, tie):
            if "w_in" in grads:
                key, packs = "g_in", [_unpermute_w_in(grads["w_in"]).astype(BF16)]
            else:
                key = "g_rest"
                packs = [_by_chip(jax.tree.map(lambda t: t.astype(BF16), grads[n]), rr, cc, axis)
                         for n, rr, cc, axis in _W_REST]
            lands = [lax.empty((_NCOPY,) + p.shape[1:], BF16) for p in packs]
            state[key], token = _split_start("rs_" + key + "_start", packs, lands, _plan_scatter)
            return _tie(tie, token, "tie_" + key)

        def mark(self, name, value):
            state[name] = value

    def reduced(key, after, group):
        packs, lands = _split_wait("rs_" + key + "_wait", state[key], after, _plan_scatter)
        mines = [_add_slabs(p, l, meidx, "rs_add_chips_" + n) for (n, *_), p, l in zip(group, packs, lands)]
        return dict(zip([n for n, *_ in group], zip(mines, _swap_sibling(mines, "rs_" + key + "_swap"))))

    w_in_slabs = _ag_weights(W2["w_in"].astype(BF16))
    wts = {k: W2[k] for k in _SMALL_REPL}
    conv_by_chip = _gather8([W2[n] for n, *_ in _SMALL_CONV], False, "ag_conv_weights")
    for (n, kk, width), stacked in zip(_SMALL_CONV, conv_by_chip):
        wts[n] = stacked.transpose(1, 0, 2).reshape(kk, width)
    rest = [W2[n].astype(BF16) for n, *_ in _W_REST]
    rest[0] = _tie(rest[0], conv_by_chip[0], "tie_ag_order")
    state["rest"], state["rest_token"] = _split_start(
        "ag_rest_start", rest, [lax.empty((NCHIP,) + r.shape, BF16) for r in rest], _plan_bcast)
    wts["w_in"] = _permute_w_in(w_in_slabs)

    loss8, grad_x, grads = _local_step(x[0], loss_target[0], wts, Hooks())

    gbig = {**reduced("g_rest", state["ssd_bwd"], _W_REST), **reduced("g_in", grad_x, _W_IN)}

    small_parts = [grads[n] for n in _SMALL_REPL] + [loss8[0:1]] + [grads[n] for n, *_ in _SMALL_CONV]
    small_g = _gather8(small_parts, True, "allreduce_small")
    gsm = dict(zip(_SMALL_REPL, small_g[:len(_SMALL_REPL)]))
    loss = small_g[len(_SMALL_REPL)][0, 0]
    for (n, kk, width), gfull in zip(_SMALL_CONV, small_g[len(_SMALL_REPL) + 1:]):
        cw = width // NCHIP
        gsm[n] = lax.dynamic_slice(gfull, (0, me * cw), (kk, cw))

    G, DW, NM, NV = {}, {}, {}, {}
    for n in [b[0] for b in _W_IN + _W_REST]:
        G[n], DW[n], NM[n], NV[n] = _adamw(W2[n], gbig[n], M2[n], V2[n], "adamw_" + n)
    sm_names = list(_SMALL_REPL) + [n for n, *_ in _SMALL_CONV]
    outs = _adamw_small(*([t[n] for n in sm_names] for t in (W2, gsm, M2, V2)))
    for t, vals in zip((DW, NM, NV), outs):
        t.update(zip(sm_names, vals))
    G.update(gsm)

    def shaped(t):
        return [t[n].reshape(W[n].shape) for n in names]

    return (loss, grad_x.reshape(x.shape), *shaped(G), *shaped(DW), *shaped(NM), *shaped(NV))
```

```python
import jax
import jax.numpy as jnp
from jax import lax
from jax.experimental import pallas as pl
from jax.experimental.pallas import tpu as pltpu

F32 = jnp.float32
BF16 = jnp.bfloat16

D = 1024
DI = 2048
NH = 32
HP = 64
NG = 4
NS = 128
CH = 128
DX = 3072
FF = 2816
NI = 10272
EPS = 1e-5

OFF_BCV, OFF_XBC, OFF_G, OFF_Z, OFF_DT = 0, 3072, 6144, 8192, 10240
NIP = 10752
_SEGS = ((0, 2048, OFF_G), (2048, 3072, OFF_BCV), (5120, 2048, OFF_Z), (7168, 3072, OFF_XBC), (10240, 32, OFF_DT))

LANES = 128
HALO = 16
V7X_VMEM_LIMIT = 56 * 2 ** 20

ADAM_LR, ADAM_B1, ADAM_B2, ADAM_EPS, ADAM_WD, ADAM_STEP = 0.001, 0.9, 0.999, 1e-08, 0.01, 10

NN = (((1,), (0,)), ((), ()))
NT = (((1,), (1,)), ((), ()))
TN = (((0,), (0,)), ((), ()))


def _dot(a, b, dims=NN):
    return lax.dot_general(a, b, dims, preferred_element_type=F32)


def _params(sem, **kw):
    return pltpu.CompilerParams(dimension_semantics=sem, vmem_limit_bytes=V7X_VMEM_LIMIT, **kw)


V7X_MXU = 256
V7X_HBM_BYTES_PER_S = 3.5e12
STEP_S = 0.35e-6
MATMUL_VMEM = 40 * 2 ** 20
EPILOGUE_VMEM = 46 * 2 ** 20


ACC_BYTES_PER_S = 1.2e13


def _divisors(dim, cap, units):
    for unit in units:
        c = [t for t in range(unit, min(dim, cap) + 1, unit) if dim % t == 0]
        if c:
            return c
    return [dim]


def _tiles(M, N, K, out_bytes, has_res):
    best = None
    for tn in _divisors(N, 2816, (V7X_MXU, LANES)):
        for tm in _divisors(M, 2816, (LANES,)):
            for tk in _divisors(K, 2816, (V7X_MXU, LANES)):
                nk, ni, nj = K // tk, M // tm, N // tn
                vmem = 4 * (tm * tk + tk * tn) + 2 * tm * tn * out_bytes
                vmem += (4 * tm * tn if nk > 1 else 0) + (8 * tm * tn if has_res else 0)
                if vmem > MATMUL_VMEM:
                    continue
                a_reads = M * K * 2 * (nj if nk > 1 else 1)
                b_reads = K * N * 2 * (ni if nk * nj > 1 else 1)
                cost = (a_reads + b_reads + M * N * out_bytes) / V7X_HBM_BYTES_PER_S + ni * nj * nk * STEP_S
                cost += (nk - 1) * M * N * 8 / ACC_BYTES_PER_S
                if best is None or cost < best[0]:
                    best = (cost, tm, tn, tk)
    assert best is not None, (M, N, K)
    return best[1:]


def _sigmoid(x):
    return 1.0 / (1.0 + jnp.exp(-x))


class _Epilogue:
    def __init__(self, fn, ins, outs, tile_bytes, in_windows=None, out_windows=None):
        self.fn, self.ins, self.outs, self.tile_bytes = fn, tuple(ins), tuple(outs), tile_bytes
        self.in_windows, self.out_windows = in_windows or {}, out_windows or {}


def _matmul(a, b, *, mode, out_dtype, name, residual=None, b_k_off=0, epilogue=None):
    if mode == "nn":
        (M, K), (K2, N) = a.shape, b.shape
    elif mode == "nt":
        (M, K), (N, K2) = a.shape, (b.shape[0], a.shape[1])
        assert b_k_off + K <= b.shape[1]
    else:
        (K, M), (K2, N) = a.shape, b.shape
    assert K == K2, (name, a.shape, b.shape)
    tm, tn, tk = _tiles(M, N, K, jnp.dtype(out_dtype).itemsize, residual is not None)
    if epilogue is not None:
        tn = N
        fits = [(K * N * 2 * (M // t) / V7X_HBM_BYTES_PER_S + (K // q - 1) * M * N * 8 / ACC_BYTES_PER_S
                 + (M // t) * (K // q) * STEP_S, t, q)
                for t in (1024, 512, 256) if M % t == 0 for q in _divisors(K, 2816, (V7X_MXU, LANES))
                if 4 * (t * q + q * tn) + (4 * t * tn if K > q else 0) + (8 * t * tn if residual is not None else 0)
                + 2 * t * epilogue.tile_bytes <= EPILOGUE_VMEM]
        _, tm, tk = min(fits)
    nk = K // tk
    if mode == "tn":
        a_spec = pl.BlockSpec((tk, tm), lambda i, j, k: (k, i))
    else:
        a_spec = pl.BlockSpec((tm, tk), lambda i, j, k: (i, k))
    if mode == "nt":
        assert b_k_off % tk == 0
        b_spec = pl.BlockSpec((tn, tk), lambda i, j, k: (j, k + b_k_off // tk))
    else:
        b_spec = pl.BlockSpec((tk, tn), lambda i, j, k: (k, j))
    dims = {"nn": NN, "nt": NT, "tn": TN}[mode]
    o_spec = pl.BlockSpec((tm, tn), lambda i, j, k: (i, j))
    has_res = residual is not None

    def rows_or_whole(shape, window=None):
        if window is not None:
            off, width = window
            return pl.BlockSpec((tm, width), lambda i, j, k: (i, off // width))
        if shape[0] == M:
            return pl.BlockSpec((tm,) + tuple(shape[1:]), lambda i, j, k: (i,) + (0,) * (len(shape) - 1))
        return pl.BlockSpec(tuple(shape), lambda i, j, k: (0,) * len(shape))

    n_in = 2 + has_res + (len(epilogue.ins) if epilogue else 0)
    n_out = len(epilogue.outs) if epilogue else 1

    def body(*refs):
        a_ref, b_ref = refs[:2]
        r_ref = refs[2] if has_res else None
        out_refs = refs[n_in:n_in + n_out]
        acc_ref = refs[-1]
        k = pl.program_id(2)
        part = _dot(a_ref[...], b_ref[...], dims)

        def finish(r):
            if has_res:
                r = r + r_ref[...].astype(F32)
            if epilogue is None:
                out_refs[0][...] = r.astype(out_dtype)
            else:
                epilogue.fn(r, refs[2 + has_res:n_in], out_refs, pl.program_id(0) == 0)

        if nk == 1:
            finish(part)
            return

        @pl.when(k == 0)
        def _():
            acc_ref[...] = part

        @pl.when(jnp.logical_and(k > 0, k < nk - 1))
        def _():
            acc_ref[...] += part

        @pl.when(k == nk - 1)
        def _():
            finish(acc_ref[...] + part)

    in_specs = [a_spec, b_spec] + ([o_spec] if has_res else [])
    args = (a, b) + ((residual,) if has_res else ())
    if epilogue is None:
        out_specs, out_shape = o_spec, jax.ShapeDtypeStruct((M, N), out_dtype)
        sem = ("parallel", "parallel", "arbitrary")
    else:
        in_specs += [rows_or_whole(x.shape, epilogue.in_windows.get(n)) for n, x in enumerate(epilogue.ins)]
        args += epilogue.ins
        out_specs = [rows_or_whole(o[0], epilogue.out_windows.get(n)) for n, o in enumerate(epilogue.outs)]
        out_shape = [jax.ShapeDtypeStruct(shp, dt) for shp, dt in epilogue.outs]
        sem = ("arbitrary", "arbitrary", "arbitrary")
    return pl.pallas_call(
        body, name=name, grid=(M // tm, N // tn, nk), in_specs=in_specs, out_specs=out_specs,
        out_shape=out_shape, scratch_shapes=[pltpu.VMEM((tm, tn), F32)] if nk > 1 else [],
        compiler_params=_params(sem),
    )(*args)


class _Rows:
    def __init__(self, T, tm):
        self.T, self.tm = T, min(tm, T // 2)
        self.nrow = T // self.tm
        self.r = self.tm // HALO
        self.nb = T // HALO

    def tile(self, w, cb=0, step=1):
        return pl.BlockSpec((self.tm, w), lambda j, i: (i, cb + step * j))

    def prev(self, w, cb=0, step=1):
        r = self.r
        return pl.BlockSpec((HALO, w), lambda j, i: (jnp.maximum(i * r - 1, 0), cb + step * j))

    def next(self, w, cb=0, step=1):
        r, nb = self.r, self.nb
        return pl.BlockSpec((HALO, w), lambda j, i: (jnp.minimum((i + 1) * r, nb - 1), cb + step * j))

    def colvec(self, k, w, cb=0, step=1):
        return pl.BlockSpec((k, w), lambda j, i: (0, cb + step * j))

    def call(self, body, name, ncol, in_specs, out_specs, out_shape, args, aliases=None):
        return pl.pallas_call(
            body, name=name, grid=(ncol, self.nrow), in_specs=in_specs, out_specs=out_specs,
            out_shape=out_shape, input_output_aliases=aliases or {},
            compiler_params=_params(("parallel", "arbitrary")),
        )(*args)


ANY = pl.BlockSpec(memory_space=pl.ANY)


def _shifts_causal(ext, nk, tm):
    out = []
    for k in range(nk):
        s = nk - 1 - k
        r = ext if s == 0 else pltpu.roll(ext, s, 0)
        out.append(r[HALO:])
    return out


def _shifts_anticausal(ext, nk, tm):
    n = ext.shape[0]
    out = []
    for k in range(nk):
        s = nk - 1 - k
        r = ext if s == 0 else pltpu.roll(ext, n - s, 0)
        out.append(r[:tm])
    return out


def _wsum(w, parts):
    acc = w[0:1, :] * parts[0]
    for k in range(1, len(parts)):
        acc = acc + w[k:k + 1, :] * parts[k]
    return acc


def _colsum(x):
    return jnp.sum(x, axis=0, keepdims=True)


def _acc_out(ref, val, first):
    @pl.when(first)
    def _():
        ref[...] = val

    @pl.when(jnp.logical_not(first))
    def _():
        ref[...] += val


def _acc_rows(ref, rows, first):
    for k, r in enumerate(rows):
        _acc_out(ref.at[k:k + 1, :], r, first)


def _norm_matmul(x, wn, b, name, b_f32=None):
    T, N = x.shape[0], b.shape[1]
    tm = min(1024, T)
    tn = max(t for t in _divisors(N, 2816, (V7X_MXU, LANES))
             if 8 * tm * D + 6 * tm * D + 4 * D * t + 4 * tm * t <= MATMUL_VMEM)

    extra = b_f32 is not None

    def body(*refs):
        x_ref, wn_ref, b_ref = refs[:3]
        o_ref, u_ref = refs[3 + extra:5 + extra]
        keep_ref = refs[-1]

        @pl.when(pl.program_id(1) == 0)
        def _():
            xv = x_ref[...]
            r = lax.rsqrt(jnp.mean(xv * xv, axis=-1, keepdims=True) + EPS)
            u = (xv * r * wn_ref[...]).astype(BF16)
            keep_ref[...] = u
            u_ref[...] = u
            if extra:
                refs[5 + extra][...] = _dot(u, refs[3][...])

        o_ref[...] = _dot(keep_ref[...], b_ref[...]).astype(BF16)

    rows = pl.BlockSpec((tm, D), lambda i, j: (i, 0))
    whole = lambda shape: pl.BlockSpec(shape, lambda i, j: (0, 0))
    narrow = pl.BlockSpec((tm, LANES), lambda i, j: (i, 0))
    return pl.pallas_call(
        body, name=name, grid=(T // tm, N // tn),
        in_specs=[rows, whole((1, D)), pl.BlockSpec((D, tn), lambda i, j: (0, j))] + [whole((D, LANES))] * extra,
        out_specs=[pl.BlockSpec((tm, tn), lambda i, j: (i, j)), rows] + [narrow] * extra,
        out_shape=[jax.ShapeDtypeStruct((T, N), BF16), jax.ShapeDtypeStruct((T, D), BF16)]
        + [jax.ShapeDtypeStruct((T, LANES), F32)] * extra,
        scratch_shapes=[pltpu.VMEM((tm, D), BF16)],
        compiler_params=_params(("parallel", "arbitrary")),
    )(*((x, wn, b) + ((b_f32,) if extra else ())))


def _rmsnorm_bwd_epilogue(x, w, dres):
    T = x.shape[0]

    def fn(dyv, ins, outs, first):
        x_ref, w_ref, dr_ref = ins
        dx_ref, dxb_ref, dw_ref = outs
        xv = x_ref[...]
        r = lax.rsqrt(jnp.mean(xv * xv, axis=-1, keepdims=True) + EPS)
        xh = xv * r
        dxh = dyv * w_ref[...]
        dx = r * (dxh - xh * jnp.mean(dxh * xh, axis=-1, keepdims=True)) + dr_ref[...]
        dx_ref[...] = dx
        dxb_ref[...] = dx.astype(BF16)
        _acc_out(dw_ref, _colsum(dyv * xh), first)

    return _Epilogue(fn, (x, w, dres), (((T, D), F32), ((T, D), BF16), ((1, D), F32)), 14 * D)


def _branch_a_fwd(proj, conv_w):
    T = proj.shape[0]
    R = _Rows(T, 512)
    tm = R.tm

    def body(p_ref, pp_ref, w_ref, o_ref):
        keep = (pl.program_id(1) > 0).astype(F32)
        cv = p_ref[:, D:2 * D].astype(F32) * p_ref[:, 2 * D:].astype(F32)
        cvp = pp_ref[:, D:2 * D].astype(F32) * pp_ref[:, 2 * D:].astype(F32) * keep
        sh = _shifts_causal(jnp.concatenate([cvp, cv], axis=0), 3, tm)
        ca = _wsum(w_ref[...], sh)
        o_ref[...] = (p_ref[:, :D].astype(F32) * ca).astype(BF16)

    return R.call(body, "branch_a_fwd", 1, [R.tile(3 * D), R.prev(3 * D), R.colvec(3, D)], R.tile(D),
                  jax.ShapeDtypeStruct((T, D), BF16), (proj, proj, conv_w))


def _branch_a_bwd(dya_in, proj, conv_w, dproj):
    T = proj.shape[0]
    R = _Rows(T, 256)
    tm = R.tm

    def body(d_ref, dn_ref, p_ref, pp_ref, pn_ref, w_ref, _alias, o_ref, dw_ref):
        i = pl.program_id(1)
        keep_p = (i > 0).astype(F32)
        keep_n = (i < R.nrow - 1).astype(F32)
        w = w_ref[...]
        b = p_ref[:, :D].astype(F32)
        c = p_ref[:, D:2 * D].astype(F32)
        v = p_ref[:, 2 * D:].astype(F32)
        cvp = pp_ref[:, D:2 * D].astype(F32) * pp_ref[:, 2 * D:].astype(F32) * keep_p
        sh = _shifts_causal(jnp.concatenate([cvp, c * v], axis=0), 3, tm)
        ca = _wsum(w, sh)
        d = d_ref[...].astype(F32)
        dca = d * b
        dca_n = dn_ref[...].astype(F32) * pn_ref[:, :D].astype(F32) * keep_n
        dsh = _shifts_anticausal(jnp.concatenate([dca, dca_n], axis=0), 3, tm)
        dcv = _wsum(w, dsh)
        o_ref[:, :D] = (d * ca).astype(BF16)
        o_ref[:, D:2 * D] = (dcv * v).astype(BF16)
        o_ref[:, 2 * D:] = (dcv * c).astype(BF16)
        _acc_rows(dw_ref, [_colsum(dca * s) for s in sh], i == 0)

    return R.call(
        body, "branch_a_bwd", 1,
        [R.tile(D), R.next(D), R.tile(3 * D), R.prev(3 * D), R.next(3 * D), R.colvec(3, D), ANY],
        [R.tile(3 * D), R.colvec(3, D)],
        [jax.ShapeDtypeStruct(dproj.shape, BF16), jax.ShapeDtypeStruct((3, D), F32)],
        (dya_in, dya_in, proj, proj, proj, conv_w, dproj), aliases={6: 0})


_XW = 512


def _xbc_fwd(proj, conv_w, conv_b):
    T = proj.shape[0]
    R = _Rows(T, 512)
    tm = R.tm
    cb = OFF_XBC // _XW

    def body(x_ref, xp_ref, w_ref, b_ref, o_ref):
        keep = (pl.program_id(1) > 0).astype(F32)
        ext = jnp.concatenate([xp_ref[...].astype(F32) * keep, x_ref[...].astype(F32)], axis=0)
        pre = _wsum(w_ref[...], _shifts_causal(ext, 4, tm)) + b_ref[...]
        o_ref[...] = (pre * _sigmoid(pre)).astype(BF16)

    return R.call(body, "xbc_fwd", DX // _XW,
                  [R.tile(_XW, cb), R.prev(_XW, cb), R.colvec(4, _XW), R.colvec(1, _XW)], R.tile(_XW),
                  jax.ShapeDtypeStruct((T, DX), BF16), (proj, proj, conv_w, conv_b))


def _xbc_bwd(dact, proj, conv_w, conv_b, dproj):
    T = proj.shape[0]
    R = _Rows(T, 512)
    tm = R.tm
    cb = OFF_XBC // _XW

    def body(d_ref, dn_ref, x_ref, xp_ref, xn_ref, w_ref, b_ref, _alias, o_ref, dw_ref, db_ref):
        i = pl.program_id(1)
        keep_p = (i > 0).astype(F32)
        keep_n = (i < R.nrow - 1).astype(F32)
        w = w_ref[...]
        ext = jnp.concatenate([xp_ref[...].astype(F32) * keep_p, x_ref[...].astype(F32),
                               xn_ref[...].astype(F32)], axis=0)
        sh = _shifts_causal(ext, 4, tm + HALO)
        pre = _wsum(w, sh) + b_ref[...]
        s = _sigmoid(pre)
        dsilu = s * (1.0 + pre * (1.0 - s))
        dext = jnp.concatenate([d_ref[...].astype(F32), dn_ref[...].astype(F32) * keep_n], axis=0)
        dpre = dext * dsilu
        dsh = _shifts_anticausal(dpre, 4, tm)
        o_ref[...] = _wsum(w, dsh).astype(BF16)
        dp = dpre[:tm]
        _acc_rows(dw_ref, [_colsum(dp * q[:tm]) for q in sh], i == 0)
        _acc_out(db_ref, _colsum(dp), i == 0)

    return R.call(
        body, "xbc_bwd", DX // _XW,
        [R.tile(_XW), R.next(_XW), R.tile(_XW, cb), R.prev(_XW, cb), R.next(_XW, cb),
         R.colvec(4, _XW), R.colvec(1, _XW), ANY],
        [R.tile(_XW, cb), R.colvec(4, _XW), R.colvec(1, _XW)],
        [jax.ShapeDtypeStruct(dproj.shape, BF16), jax.ShapeDtypeStruct((4, DX), F32),
         jax.ShapeDtypeStruct((1, DX), F32)],
        (dact, dact, proj, proj, proj, conv_w, conv_b, dproj), aliases={7: 0})


def _softplus(x):
    return jnp.maximum(x, 0.0) + jnp.log(1.0 + jnp.exp(-jnp.abs(x)))


def _dt_rows(T):
    return min(8 * CH, T // 2)


def _dt_fwd(dt_raw, dt_bias_p, a_log_p):
    T = dt_raw.shape[0]
    rows = _dt_rows(T)

    def body(r_ref, b_ref, al_ref, dt_ref, ac_ref, acT_ref):
        dt = _softplus(r_ref[...] + b_ref[...])
        s = dt * (-jnp.exp(al_ref[...]))
        row = lax.broadcasted_iota(jnp.int32, (rows, LANES), 0) % CH
        k = 1
        while k < CH:
            s = s + jnp.where(row >= k, pltpu.roll(s, k, 0), 0.0)
            k *= 2
        dt_ref[...] = dt
        ac_ref[...] = s
        for q in range(0, rows, CH):
            acT_ref[q:q + CH] = s[q:q + CH].T

    blk = pl.BlockSpec((rows, LANES), lambda i: (i, 0))
    vec = pl.BlockSpec((1, LANES), lambda i: (0, 0))
    return pl.pallas_call(
        body, name="dt_fwd", grid=(T // rows,), in_specs=[blk, vec, vec], out_specs=[blk, blk, blk],
        out_shape=[jax.ShapeDtypeStruct((T, LANES), F32)] * 3, compiler_params=_params(("parallel",)),
    )(dt_raw, dt_bias_p, a_log_p)


def _dt_bwd(dacum, ddt_x, dt_raw, dt_bias_p, a_log_p, dproj):
    T = dt_raw.shape[0]
    rows = _dt_rows(T)
    nc = T // rows

    def body(da_ref, dx_ref, r_ref, b_ref, al_ref, _alias, o_ref, db_ref, dal_ref):
        i = pl.program_id(0)
        a = -jnp.exp(al_ref[...])
        z = r_ref[...] + b_ref[...]
        dt = _softplus(z)
        s = da_ref[...]
        row = lax.broadcasted_iota(jnp.int32, (rows, LANES), 0) % CH
        k = 1
        while k < CH:
            s = s + jnp.where(row < CH - k, pltpu.roll(s, rows - k, 0), 0.0)
            k *= 2
        ddt = s * a + dx_ref[...]
        draw = ddt * _sigmoid(z)
        o_ref[:, :LANES] = draw.astype(BF16)
        o_ref[:, LANES:] = jnp.zeros((rows, NIP - OFF_DT - LANES), BF16)
        _acc_out(db_ref, _colsum(draw), i == 0)
        _acc_out(dal_ref, _colsum(s * dt), i == 0)

        @pl.when(i == nc - 1)
        def _():
            dal_ref[...] = dal_ref[...] * a

    blk = pl.BlockSpec((rows, LANES), lambda i: (i, 0))
    vec = pl.BlockSpec((1, LANES), lambda i: (0, 0))
    oblk = pl.BlockSpec((rows, NIP - OFF_DT), lambda i: (i, OFF_DT // (NIP - OFF_DT)))
    return pl.pallas_call(
        body, name="dt_bwd", grid=(nc,), in_specs=[blk, blk, blk, vec, vec, ANY], out_specs=[oblk, vec, vec],
        out_shape=[jax.ShapeDtypeStruct(dproj.shape, BF16), jax.ShapeDtypeStruct((1, LANES), F32),
                   jax.ShapeDtypeStruct((1, LANES), F32)],
        input_output_aliases={5: 0}, compiler_params=_params(("arbitrary",)),
    )(dacum, ddt_x, dt_raw, dt_bias_p, a_log_p, dproj)


_GW = DI // NG
_HG = NH // NG
_NEG = -1e30


def _interleave(gens):
    out, live = [None] * len(gens), list(range(len(gens)))
    while live:
        for i in list(live):
            try:
                next(gens[i])
            except StopIteration as stop:
                out[i] = stop.value
                live.remove(i)
    return out


def _pair_lanes(left, v0, v1):
    return jnp.where(left, v0, v1)


def _hi_lo(v):
    hi = v.astype(BF16)
    return jnp.concatenate([hi, (v - hi.astype(F32)).astype(BF16)], axis=1)


def _head_spread():
    row = lax.broadcasted_iota(jnp.int32, (2 * LANES, 1), 0) % LANES
    return (row == lax.broadcasted_iota(jnp.int32, (1, DI), 1) // HP).astype(BF16)


def _ssd_specs(T, rev):
    nc = T // CH
    cm = (lambda c: nc - 1 - c) if rev else (lambda c: c)
    bw = NG * NS
    return dict(
        xs=pl.BlockSpec((CH, DI), lambda c: (cm(c), 0)),
        bm=pl.BlockSpec((CH, bw), lambda c: (cm(c), DI // bw)),
        cmat=pl.BlockSpec((CH, bw), lambda c: (cm(c), DI // bw + 1)),
        xbc=pl.BlockSpec((CH, DX), lambda c: (cm(c), 0)),
        col=pl.BlockSpec((CH, LANES), lambda c: (cm(c), 0)),
        dsk=pl.BlockSpec((1, DI), lambda c: (0, 0)),
        state=pl.BlockSpec((1, NS, DI), lambda c: (cm(c), 0, 0)),
    )


def _last(ref, lo, hi):
    return ref.at[(slice(None),) * (len(ref.shape) - 1) + (slice(lo, hi),)]


def _group_views(g, wide, narrow):
    return [_last(r, g * _GW, (g + 1) * _GW) for r in wide] + [_last(r, g * NS, (g + 1) * NS) for r in narrow]


def _ssd_fwd(xact, dt, acum, acumT, dsk_rep, proj, norm_w):
    T = xact.shape[0]
    nc = T // CH
    sp = _ssd_specs(T, False)

    def body(*refs):
        xs, bm, cmat, dtr, acr, actr, dsk, zr, nw, spread_ref, y, yn, spv, S_ref = refs

        @pl.when(pl.program_id(0) == 0)
        def _():
            S_ref[...] = jnp.zeros_like(S_ref)

        ac = acr[...]
        cols = [_hi_lo(v) for v in (dtr[...], jnp.exp(ac), jnp.exp(ac[CH - 1:CH, :] - ac))]
        _interleave([group(g * _HG, cols, ac, actr[...], _last(spread_ref, g * _GW, (g + 1) * _GW),
                           *_group_views(g, (xs, dsk, zr, nw, y, yn, spv, S_ref), (bm, cmat))) for g in range(NG)])

    def group(hb, cols, ac, acT, spread_ref, xs_ref, dsk_ref, z_ref, nw_ref, y_ref, yn_ref, sp_ref, S_ref, b_ref, c_ref):
        dtl, eal, dtel = (_dot(v, spread_ref[...]) for v in cols)
        Bm, Cm = b_ref[...], c_ref[...]
        S = S_ref[...]
        sp_ref[0] = S
        cb = _dot(Cm, Bm, NT)
        CS = _dot(Cm, S.astype(BF16))
        row = lax.broadcasted_iota(jnp.int32, (CH, CH), 0)
        col = lax.broadcasted_iota(jnp.int32, (CH, CH), 1)
        tril = row >= col
        left = col < HP
        xd_parts = []
        for p in range(_HG // 2):
            sl = slice(p * LANES, (p + 1) * LANES)
            j0, j1 = hb + 2 * p, hb + 2 * p + 1
            xp = xs_ref[:, sl].astype(F32)
            a0, a1 = ac[:, j0:j0 + 1], ac[:, j1:j1 + 1]
            X = xp * dtl[:, sl]
            Xb = X.astype(BF16)
            Ws = [(cb * jnp.exp(jnp.where(tril, aj - acT[j:j + 1, :], _NEG))).astype(BF16)
                  for j, aj in ((j0, a0), (j1, a1))]
            Xs = [jnp.where(m, Xb, jnp.zeros_like(Xb)) for m in (left, jnp.logical_not(left))]
            yield
            yd = _dot(jnp.concatenate(Ws, axis=1), jnp.concatenate(Xs, axis=0))
            yield
            y = yd + eal[:, sl] * CS[:, sl] + dsk_ref[:, sl] * xp
            y_ref[:, sl] = y.astype(BF16)
            xd_parts.append(X * dtel[:, sl])
        Xd = jnp.concatenate(xd_parts, axis=1).astype(BF16)
        S_ref[...] = eal[CH - 1:CH, :] * S + _dot(Bm, Xd, TN)
        yield
        z = z_ref[...].astype(F32)
        yf = y_ref[...].astype(F32) * z * _sigmoid(z)
        r = lax.rsqrt(jnp.mean(yf * yf, axis=-1, keepdims=True) + EPS)
        yn_ref[...] = (yf * r * nw_ref[...]).astype(BF16)

    zspec = pl.BlockSpec((CH, DI), lambda c: (c, OFF_Z // DI))
    return pl.pallas_call(
        body, name="ssd_fwd", grid=(nc,),
        in_specs=[sp["xs"], sp["bm"], sp["cmat"], sp["col"], sp["col"], sp["col"], sp["dsk"], zspec, sp["dsk"],
                  pl.BlockSpec((2 * LANES, DI), lambda c: (0, 0))],
        out_specs=[sp["xs"], sp["xs"], sp["state"]],
        out_shape=[jax.ShapeDtypeStruct((T, DI), BF16), jax.ShapeDtypeStruct((T, DI), BF16),
                   jax.ShapeDtypeStruct((nc, NS, DI), F32)],
        scratch_shapes=[pltpu.VMEM((NS, DI), F32)],
        compiler_params=_params(("arbitrary",)),
    )(xact, xact, xact, dt, acum, acumT, dsk_rep, proj, norm_w, _head_spread())


def _ssd_bwd(dn, y, proj, norm_w, dproj, xact, dt, acum, acumT, dsk_rep, sprev):
    T = xact.shape[0]
    nc = T // CH
    sp = _ssd_specs(T, True)

    def body(*refs):
        (xs, bm, cmat, dtr, acr, actr, dsk, dnr, yr, zr, nw, spv, _alias, lanes_of_ref, rows_of_ref, spread_ref,
         dxa, ddtx, dAc, dskacc, dzr, dnw, dS_ref) = refs
        first = pl.program_id(0) == 0

        @pl.when(first)
        def _():
            dS_ref[...] = jnp.zeros_like(dS_ref)

        dbc = _last(dxa, DI, DX)
        ddtx_sum = jnp.zeros((CH, LANES), F32)
        dAc_sum = jnp.zeros((CH, LANES), F32)
        ac = acr[...]
        cols = [_hi_lo(v) for v in (dtr[...], jnp.exp(ac), jnp.exp(ac[CH - 1:CH, :] - ac))]
        for a, b in _interleave([group(first, g * _HG, cols, ac, actr[...],
                                       lanes_of_ref.at[g * _GW:(g + 1) * _GW],
                                       rows_of_ref.at[g * _HG * CH:(g + 1) * _HG * CH],
                                       _last(spread_ref, g * _GW, (g + 1) * _GW),
                                       *_group_views(g, (xs, dsk, dnr, yr, zr, nw, dzr, dnw, spv, dxa, dskacc, dS_ref),
                                                     (bm, cmat, dbc, _last(dbc, NG * NS, 2 * NG * NS))))
                                 for g in range(NG)]):
            ddtx_sum, dAc_sum = ddtx_sum + a, dAc_sum + b
        ddtx[...] = ddtx_sum
        dAc[...] = dAc_sum

    def group(first, hb, cols, ac, acT, lanes_of_ref, rows_of_ref, spread_ref, xs_ref, dsk_ref, dn_ref, y_ref, z_ref,
              nw_ref, dz_ref, dnw_ref, sp_ref, dx_ref, dskacc_ref, dS_ref, b_ref, c_ref, dB_ref, dC_ref):
        z = z_ref[...].astype(F32)
        yv = y_ref[...].astype(F32)
        sg = _sigmoid(z)
        silu = z * sg
        yf = yv * silu
        rn = lax.rsqrt(jnp.mean(yf * yf, axis=-1, keepdims=True) + EPS)
        yh = yf * rn
        dnv = dn_ref[...].astype(F32)
        dyh = dnv * nw_ref[...]
        dyf = rn * (dyh - yh * jnp.mean(dyh * yh, axis=-1, keepdims=True))
        dyg = dyf * silu
        dz_ref[...] = (dyf * yv * sg * (1.0 + z * (1.0 - sg))).astype(BF16)
        _acc_out(dnw_ref, _colsum(dnv * yh), first)
        Bm, Cm = b_ref[...], c_ref[...]
        S = sp_ref[0]
        dS = dS_ref[...]
        Sb, dSb = S.astype(BF16), dS.astype(BF16)
        cb = _dot(Cm, Bm, NT)
        cbT = _dot(Bm, Cm, NT)
        CmT = Cm.T
        CS = _dot(Cm, Sb)
        T1 = _dot(Bm, dSb)
        yield
        row = lax.broadcasted_iota(jnp.int32, (CH, CH), 0)
        col = lax.broadcasted_iota(jnp.int32, (CH, CH), 1)
        tril = row >= col
        triu = row <= col
        left = col < HP
        lastrow = lax.broadcasted_iota(jnp.int32, (CH, 1), 0) == CH - 1
        dCB = jnp.zeros((CH, CH), F32)
        dCBT = jnp.zeros((CH, CH), F32)
        xd_parts, dye_parts, dsk_parts, dxx_parts, gr_parts, end_parts, qd_parts = ([] for _ in range(7))
        dtls, eals, dtels = (_dot(v, spread_ref[...]) for v in cols)
        for p in range(_HG // 2):
            sl = slice(p * LANES, (p + 1) * LANES)
            j0, j1 = hb + 2 * p, hb + 2 * p + 1
            xp = xs_ref[:, sl].astype(F32)
            dyp = dyg[:, sl]
            a0, a1 = ac[:, j0:j0 + 1], ac[:, j1:j1 + 1]
            dtl, eal, dtel = dtls[:, sl], eals[:, sl], dtels[:, sl]
            X = xp * dtl
            Xb = X.astype(BF16)
            T1d = T1[:, sl] * dtel
            Rm = T1d * X
            decp = eal[CH - 1:CH, :]
            gr_parts.append(dyp * (eal * CS[:, sl]) - Rm)
            end_parts.append(Rm + decp * (dS[:, sl] * S[:, sl]))
            dXd = jnp.zeros((CH, LANES), F32)
            for j, aj, mask in ((j0, a0, left), (j1, a1, jnp.logical_not(left))):
                dYm = jnp.where(mask, dyp, 0.0).astype(BF16)
                dWm = _dot(dYm, Xb, NT)
                dWmT = _dot(Xb, dYm, NT)
                yield
                e = aj - acT[j:j + 1, :]
                P = dWm * jnp.exp(jnp.where(tril, e, _NEG))
                LmT = jnp.exp(jnp.where(triu, -e, _NEG))
                PT = dWmT * LmT
                dCB = dCB + P
                dCBT = dCBT + PT
                yield
                dXd = dXd + _dot((cbT * LmT).astype(BF16), dYm)
                qd_parts.append((P * cb - PT * cbT).astype(BF16))
                yield
            dX = dXd + T1d
            dxx_parts.append(dX * xp)
            dx_ref[:, sl] = (dX * dtl + dsk_ref[:, sl] * dyp).astype(BF16)
            dsk_parts.append(_colsum(dyp * xp))
            xd_parts.append(X * dtel)
            dye_parts.append(dyp * eal)
            yield
        Xd = jnp.concatenate(xd_parts, axis=1).astype(BF16)
        dYe = jnp.concatenate(dye_parts, axis=1).astype(BF16)
        def lane_sums(parts):
            return _dot(jnp.concatenate(parts, axis=1).astype(BF16), lanes_of_ref[...])
        ddtx = lane_sums(dxx_parts)
        dAc = (_dot(jnp.concatenate(qd_parts, axis=1), rows_of_ref[...]) + lane_sums(gr_parts)
               + jnp.where(lastrow, _colsum(lane_sums(end_parts)), 0.0))
        dC_ref[...] = (_dot(dCB.astype(BF16), Bm) + _dot(dYe, Sb, NT)).astype(BF16)
        dB_ref[...] = (_dot(dCBT.astype(BF16), Cm) + _dot(Xd, dSb, NT)).astype(BF16)
        dS_ref[...] = _dot(CmT, dYe) + eals[CH - 1:CH, :] * dS
        _acc_out(dskacc_ref, jnp.concatenate(dsk_parts, axis=1), first)
        return ddtx, dAc

    zspec = pl.BlockSpec((CH, DI), lambda c: (nc - 1 - c, OFF_Z // DI))
    head = lax.broadcasted_iota(jnp.int32, (1, LANES), 1)
    lanes_of = (lax.broadcasted_iota(jnp.int32, (DI, 1), 0) // HP == head).astype(BF16)
    rows_of = (lax.broadcasted_iota(jnp.int32, (NH * CH, 1), 0) // CH == head).astype(BF16)
    return pl.pallas_call(
        body, name="ssd_bwd", grid=(nc,),
        in_specs=[sp["xs"], sp["bm"], sp["cmat"], sp["col"], sp["col"], sp["col"], sp["dsk"], sp["xs"], sp["xs"],
                  zspec, sp["dsk"], sp["state"], ANY, pl.BlockSpec(lanes_of.shape, lambda c: (0, 0)),
                  pl.BlockSpec(rows_of.shape, lambda c: (0, 0)), pl.BlockSpec((2 * LANES, DI), lambda c: (0, 0))],
        out_specs=[sp["xbc"], sp["col"], sp["col"], sp["dsk"], zspec, sp["dsk"]],
        out_shape=[jax.ShapeDtypeStruct((T, DX), BF16), jax.ShapeDtypeStruct((T, LANES), F32),
                   jax.ShapeDtypeStruct((T, LANES), F32), jax.ShapeDtypeStruct((1, DI), F32),
                   jax.ShapeDtypeStruct(dproj.shape, BF16), jax.ShapeDtypeStruct((1, DI), F32)],
        scratch_shapes=[pltpu.VMEM((NS, DI), F32)], input_output_aliases={12: 4},
        compiler_params=_params(("arbitrary",)),
    )(xact, xact, xact, dt, acum, acumT, dsk_rep, dn, y, proj, norm_w, sprev, dproj, lanes_of, rows_of, _head_spread())


def _merge_fwd_epilogue(proj, ya):
    T = proj.shape[0]

    def fn(ysv, ins, outs, first):
        g_ref, ya_ref = ins
        m_ref, ys_ref = outs
        ga = _sigmoid(g_ref[:, :D].astype(F32))
        gs = _sigmoid(g_ref[:, D:].astype(F32))
        m_ref[...] = (ga * ya_ref[...].astype(F32) + gs * ysv).astype(BF16)
        ys_ref[...] = ysv.astype(BF16)

    return _Epilogue(fn, (proj, ya), (((T, D), BF16), ((T, D), BF16)), 10 * D, in_windows={0: (OFF_G, 2 * D)})


def _merge_bwd_epilogue(proj, ya, ys, ncols):
    T = proj.shape[0]

    def fn(d, ins, outs, first):
        g_ref, ya_ref, ys_ref = ins
        dg_ref, dya_ref, dys_ref = outs
        ga = _sigmoid(g_ref[:, :D].astype(F32))
        gs = _sigmoid(g_ref[:, D:].astype(F32))
        dya_ref[...] = (d * ga).astype(BF16)
        dys_ref[...] = (d * gs).astype(BF16)
        dg_ref[:, :D] = (d * ya_ref[...].astype(F32) * ga * (1.0 - ga)).astype(BF16)
        dg_ref[:, D:] = (d * ys_ref[...].astype(F32) * gs * (1.0 - gs)).astype(BF16)

    window = (OFF_G, 2 * D)
    return _Epilogue(fn, (proj, ya, ys), (((T, ncols), BF16), ((T, D), BF16), ((T, D), BF16)), 16 * D,
                     in_windows={0: window}, out_windows={0: window})


_FW = 1408
_FB = FF // _FW


def _ffn_act_fwd(hv, conv_w, conv_b):
    T = hv.shape[0]
    R = _Rows(T, 256)
    tm = R.tm

    def body(h1_ref, h1p_ref, h3_ref, w_ref, b_ref, o_ref):
        keep = (pl.program_id(1) > 0).astype(F32)
        ext = jnp.concatenate([h1p_ref[...].astype(F32) * keep, h1_ref[...].astype(F32)], axis=0)
        pre = _wsum(w_ref[...], _shifts_causal(ext, 3, tm)) + b_ref[...]
        o_ref[...] = (pre * _sigmoid(pre) * h3_ref[...].astype(F32)).astype(BF16)

    return R.call(body, "ffn_act_fwd", _FB,
                  [R.tile(_FW), R.prev(_FW), R.tile(_FW, _FB), R.colvec(3, _FW), R.colvec(1, _FW)],
                  R.tile(_FW), jax.ShapeDtypeStruct((T, FF), BF16), (hv, hv, hv, conv_w, conv_b))


def _ffn_act_bwd(dg, hv, conv_w, conv_b):
    T = hv.shape[0]
    R = _Rows(T, 256)
    tm = R.tm

    def body(dg_ref, dgn_ref, h1_ref, h1p_ref, h1n_ref, h3_ref, h3n_ref, w_ref, b_ref, dh3_ref, dh1_ref, dw_ref,
             db_ref):
        i = pl.program_id(1)
        keep_p = (i > 0).astype(F32)
        keep_n = (i < R.nrow - 1).astype(F32)
        w = w_ref[...]
        ext = jnp.concatenate([h1p_ref[...].astype(F32) * keep_p, h1_ref[...].astype(F32),
                               h1n_ref[...].astype(F32)], axis=0)
        sh = _shifts_causal(ext, 3, tm + HALO)
        pre = _wsum(w, sh) + b_ref[...]
        s = _sigmoid(pre)
        d = jnp.concatenate([dg_ref[...].astype(F32), dgn_ref[...].astype(F32) * keep_n], axis=0)
        h3 = jnp.concatenate([h3_ref[...].astype(F32), h3n_ref[...].astype(F32)], axis=0)
        dh3_ref[...] = (d[:tm] * pre[:tm] * s[:tm]).astype(BF16)
        dpre = d * h3 * s * (1.0 + pre * (1.0 - s))
        dh1_ref[...] = _wsum(w, _shifts_anticausal(dpre, 3, tm)).astype(BF16)
        dp = dpre[:tm]
        _acc_rows(dw_ref, [_colsum(dp * q[:tm]) for q in sh], i == 0)
        _acc_out(db_ref, _colsum(dp), i == 0)

    return R.call(
        body, "ffn_act_bwd", _FB,
        [R.tile(_FW), R.next(_FW), R.tile(_FW), R.prev(_FW), R.next(_FW), R.tile(_FW, _FB), R.next(_FW, _FB),
         R.colvec(3, _FW), R.colvec(1, _FW)],
        [R.tile(_FW), R.tile(_FW), R.colvec(3, _FW), R.colvec(1, _FW)],
        [jax.ShapeDtypeStruct((T, FF), BF16), jax.ShapeDtypeStruct((T, FF), BF16),
         jax.ShapeDtypeStruct((3, FF), F32), jax.ShapeDtypeStruct((1, FF), F32)],
        (dg, dg, hv, hv, hv, hv, hv, conv_w, conv_b))


def _final_loss_epilogue(w, target):
    T = target.shape[0]

    def fn(xv, ins, outs, first):
        w_ref, t_ref = ins
        l_ref, dh_ref, dhb_ref, dw_ref = outs
        wv = w_ref[...]
        r = lax.rsqrt(jnp.mean(xv * xv, axis=-1, keepdims=True) + EPS)
        xh = xv * r
        err = xh * wv - t_ref[...]
        part = 0.5 * jnp.sum(jnp.mean(err * err, axis=-1, keepdims=True), axis=0, keepdims=True)
        _acc_out(l_ref, jnp.broadcast_to(part, l_ref.shape), first)
        dy = err * (1.0 / D)
        dxh = dy * wv
        dh = r * (dxh - xh * jnp.mean(dxh * xh, axis=-1, keepdims=True))
        dh_ref[...] = dh
        dhb_ref[...] = dh.astype(BF16)
        _acc_out(dw_ref, _colsum(dy * xh), first)

    return _Epilogue(fn, (w, target),
                     (((8, LANES), F32), ((T, D), F32), ((T, D), BF16), ((1, D), F32)), 10 * D)


def _pad_lanes(v, n=LANES):
    return jnp.pad(v, ((0, 0), (0, n - v.shape[1])))


class _Hooks:
    def before_in_proj(self, w_in):
        return w_in

    def late_weights(self, wts, after):
        return wts

    def grads_ready(self, grads, tie):
        return tie

    def mark(self, name, value):
        pass


def _local_step(x, target, wts, hooks=None):
    hooks = hooks or _Hooks()
    T = x.shape[0]
    w_in = wts["w_in"]
    dt_bias_p, a_log_p = _pad_lanes(wts["dt_bias"]), _pad_lanes(wts["a_log"])
    dsk_rep = jnp.repeat(wts["d_skip"], HP, axis=1)

    w_in = hooks.before_in_proj(w_in)
    proj, u, dt_raw = _norm_matmul(x, wts["norm_mix_w"], w_in, "norm_mm_in", w_in[:, OFF_DT:OFF_DT + LANES])
    ya_in = _branch_a_fwd(proj, wts["conv_a_w"])
    xact = _xbc_fwd(proj, wts["ssd_conv_w"], wts["ssd_conv_b"])
    dt, acum, acumT = _dt_fwd(dt_raw, dt_bias_p, a_log_p)
    y_ssd, yn, sprev = _ssd_fwd(xact, dt, acum, acumT, dsk_rep, proj, wts["ssd_norm_w"])
    late = hooks.late_weights(wts, yn)
    w_a_out, w_s_out, w_o, w_up, w_down = (late[k] for k in ("w_a_out", "w_s_out", "w_o", "w_up", "w_down"))
    y_a = _matmul(ya_in, w_a_out, mode="nn", out_dtype=BF16, name="mm_a_out")
    merged, y_s = _matmul(yn, w_s_out, mode="nn", out_dtype=BF16, name="mm_s_out_merge",
                          epilogue=_merge_fwd_epilogue(proj, y_a))
    h1 = _matmul(merged, w_o, mode="nn", out_dtype=F32, name="mm_o", residual=x)
    hv, v = _norm_matmul(h1, wts["norm_ffn_w"], w_up, "norm_mm_up")
    gact = _ffn_act_fwd(hv, wts["ffn_conv_w"], wts["ffn_conv_b"])
    loss, dh2, dh2b, g_final = _matmul(gact, w_down, mode="nn", out_dtype=F32, name="mm_down_loss", residual=h1,
                                       epilogue=_final_loss_epilogue(wts["final_norm_w"], target))

    grads = {"final_norm_w": g_final}
    grads["w_down"] = _matmul(gact, dh2b, mode="tn", out_dtype=BF16, name="mm_down_dw")
    dgact = _matmul(dh2b, w_down, mode="nt", out_dtype=BF16, name="mm_down_dx")
    dh3, dh1c, grads["ffn_conv_w"], grads["ffn_conv_b"] = _ffn_act_bwd(dgact, hv, wts["ffn_conv_w"], wts["ffn_conv_b"])
    grads["w_up"] = (_matmul(v, dh1c, mode="tn", out_dtype=BF16, name="mm_up_dw1"),
                     _matmul(v, dh3, mode="tn", out_dtype=BF16, name="mm_up_dw3"))
    dv = _matmul(dh1c, w_up, mode="nt", out_dtype=F32, name="mm_up_dx1")
    dh1, dh1b, grads["norm_ffn_w"] = _matmul(
        dh3, w_up, mode="nt", out_dtype=F32, name="mm_up_dx3_norm", residual=dv, b_k_off=FF,
        epilogue=_rmsnorm_bwd_epilogue(h1, wts["norm_ffn_w"], dh2))
    grads["w_o"] = _matmul(merged, dh1b, mode="tn", out_dtype=BF16, name="mm_o_dw")
    dproj, dya, dys = _matmul(dh1b, w_o, mode="nt", out_dtype=BF16, name="mm_o_dx_merge",
                              epilogue=_merge_bwd_epilogue(proj, y_a, y_s, NIP))
    grads["w_a_out"] = _matmul(ya_in, dya, mode="tn", out_dtype=BF16, name="mm_a_out_dw")
    dya_in = _matmul(dya, w_a_out, mode="nt", out_dtype=BF16, name="mm_a_out_dx")
    dproj, grads["conv_a_w"] = _branch_a_bwd(dya_in, proj, wts["conv_a_w"], dproj)
    grads["w_s_out"] = _matmul(yn, dys, mode="tn", out_dtype=BF16, name="mm_s_out_dw")
    dys = hooks.grads_ready({k: grads[k] for k in ("w_a_out", "w_s_out", "w_o", "w_up", "w_down")}, dys)
    dyn =_matmul(dys, w_s_out, mode="nt", out_dtype=BF16, name="mm_s_out_dx")
    dxact, ddt_x, dacum, dskl, dproj, grads["ssd_norm_w"] = _ssd_bwd(
        dyn, y_ssd, proj, wts["ssd_norm_w"], dproj, xact, dt, acum, acumT, dsk_rep, sprev)
    hooks.mark("ssd_bwd", dxact)
    grads["d_skip"] = dskl.reshape(NH, HP).sum(axis=1).reshape(1, NH)
    dproj, grads["ssd_conv_w"], grads["ssd_conv_b"] = _xbc_bwd(dxact, proj, wts["ssd_conv_w"], wts["ssd_conv_b"], dproj)
    dproj, g_dtb, g_alog = _dt_bwd(dacum, ddt_x, dt_raw, dt_bias_p, a_log_p, dproj)
    grads["dt_bias"], grads["a_log"] = g_dtb[:, :NH], g_alog[:, :NH]
    grads["w_in"] = _matmul(u, dproj, mode="tn", out_dtype=BF16, name="mm_in_dw")
    dproj = hooks.grads_ready({"w_in": grads["w_in"]}, dproj)
    grad_x, _, grads["norm_mix_w"] = _matmul(dproj, w_in, mode="nt", out_dtype=F32, name="mm_in_dx_norm",
                                             epilogue=_rmsnorm_bwd_epilogue(x, wts["norm_mix_w"], dh1))
    return loss, grad_x, grads


def _permute_w_in(slabs):
    cs = slabs.shape[2]
    pieces = []
    for o, n, no in sorted(_SEGS, key=lambda seg: seg[2]):
        for s in range(slabs.shape[0]):
            lo, hi = max(o, s * cs), min(o + n, (s + 1) * cs)
            if lo < hi:
                pieces.append(slabs[s][:, lo - s * cs:hi - s * cs])
    pieces.append(jnp.zeros((slabs.shape[1], NIP - OFF_DT - _SEGS[-1][1]), slabs.dtype))
    return jnp.concatenate(pieces, axis=1)


def _unpermute_w_in(g):
    cs = NI // NCHIP
    slabs = []
    for s in range(NCHIP):
        pieces = []
        for o, n, no in sorted(_SEGS):
            lo, hi = max(o, s * cs), min(o + n, (s + 1) * cs)
            if lo < hi:
                pieces.append(g[:, no + lo - o:no + hi - o])
        slabs.append(jnp.concatenate(pieces, axis=1))
    return jnp.stack(slabs)


MESH = pl.DeviceIdType.MESH
NCHIP = 4
NDEV = 8

_W_IN = (("w_in", D, NI // NCHIP, 1),)
_W_REST = (("w_a_out", D // NCHIP, D, 0), ("w_s_out", DI // NCHIP, D, 0), ("w_o", D // NCHIP, D, 0),
           ("w_up", D, 2 * FF // NCHIP, 1), ("w_down", FF // NCHIP, D, 0))


def _coords():
    return lax.axis_index("x"), lax.axis_index("y"), lax.axis_index("c")


def _other_chips(x, y):
    return [(1 - x, y), (x, 1 - y), (1 - x, 1 - y)]


def _ag_weights(shard):
    nrows = shard.shape[0]
    hr = nrows // 2

    def body(x_ref, out_ref, send_sems, recv_sems, local_sem):
        x, y, c = _coords()
        me = 2 * x + y
        chips = _other_chips(x, y)

        def rows(s, h):
            return out_ref.at[s, pl.ds(h * hr, hr), :]

        def copy(k, s, h, to, src=None):
            return pltpu.make_async_remote_copy(
                src_ref=rows(s, h) if src is None else src, dst_ref=rows(s, h),
                send_sem=send_sems.at[k], recv_sem=recv_sems.at[k], device_id=to, device_id_type=MESH)

        mine = pltpu.make_async_copy(x_ref, out_ref.at[me], local_sem)
        mine.start()
        first = [copy(k, me, c, (*chip, c), src=x_ref.at[pl.ds(c * hr, hr), :]) for k, chip in enumerate(chips)]
        for cp in first:
            cp.start()
        passed = []
        for k, chip in enumerate(chips):
            s = 2 * chip[0] + chip[1]
            copy(k, s, c, (x, y, c)).wait_recv()
            fwd = copy(3 + k, s, c, (x, y, 1 - c))
            fwd.start()
            passed.append(fwd)
        for k, chip in enumerate(chips):
            copy(3 + k, 2 * chip[0] + chip[1], 1 - c, (x, y, c)).wait_recv()
        for cp in first + passed:
            cp.wait_send()
        mine.wait()

    return pl.pallas_call(
        body, name="ag_weights", in_specs=[ANY], out_specs=ANY,
        out_shape=jax.ShapeDtypeStruct((NCHIP,) + shard.shape, shard.dtype),
        scratch_shapes=[pltpu.SemaphoreType.DMA((6,)), pltpu.SemaphoreType.DMA((6,)), pltpu.SemaphoreType.DMA],
        compiler_params=pltpu.CompilerParams(has_side_effects=True),
    )(shard)


HBM = pl.BlockSpec(memory_space=pltpu.HBM)
SEM = pl.BlockSpec(memory_space=pltpu.SEMAPHORE)
_EFFECT = pltpu.SideEffectType.DATAFLOW_SIDE_EFFECTING
_NCOPY = NCHIP - 1


def _plan_bcast(src_ref, land_ref, send_sems, recv_sems, base):
    x, y, c = _coords()
    sends, lands = [], []
    for k, chip in enumerate(_other_chips(x, y)):
        def copy(slot):
            return pltpu.make_async_remote_copy(
                src_ref=src_ref, dst_ref=land_ref.at[slot], send_sem=send_sems.at[base + k],
                recv_sem=recv_sems.at[base + k], device_id=(*chip, c), device_id_type=MESH)
        sends.append(copy(2 * x + y))
        lands.append(copy(2 * chip[0] + chip[1]))
    return sends, lands


def _plan_scatter(src_ref, land_ref, send_sems, recv_sems, base):
    x, y, c = _coords()
    cps = [pltpu.make_async_remote_copy(
        src_ref=src_ref.at[2 * chip[0] + chip[1]], dst_ref=land_ref.at[k], send_sem=send_sems.at[base + k],
        recv_sem=recv_sems.at[base + k], device_id=(*chip, c), device_id_type=MESH)
        for k, chip in enumerate(_other_chips(x, y))]
    return cps, cps


def _plan_all(plan, refs, n):
    sends, lands = [], []
    for t in range(n):
        s, l = plan(refs[t], refs[n + t], refs[2 * n], refs[2 * n + 1], t * _NCOPY)
        sends += s
        lands += l
    return sends, lands


def _split_start(name, srcs, lands, plan):
    n = len(srcs)

    def body(*refs):
        for cp in _plan_all(plan, refs, n)[0]:
            cp.start()
        refs[-1][...] = jnp.zeros_like(refs[-1])

    arrays = list(srcs) + list(lands)
    outs = pl.pallas_call(
        body, name=name,
        out_shape=(pltpu.SemaphoreType.DMA((n * _NCOPY,)), pltpu.SemaphoreType.DMA((n * _NCOPY,)),
                   *[pltpu.HBM(a.shape, a.dtype) for a in arrays], jax.ShapeDtypeStruct((8, LANES), F32)),
        in_specs=(HBM,) * (2 * n),
        out_specs=(SEM, SEM) + (HBM,) * (2 * n) + (pl.BlockSpec(memory_space=pltpu.VMEM),),
        input_output_aliases={t: 2 + t for t in range(2 * n)},
        compiler_params=pltpu.CompilerParams(has_side_effects=_EFFECT),
    )(*[pltpu.with_memory_space_constraint(a, pltpu.HBM) for a in arrays])
    return (outs[0], outs[1], tuple(outs[2:2 + 2 * n])), outs[-1]


def _split_wait(name, handle, after, plan):
    send_sems, recv_sems, arrays = handle
    n = len(arrays) // 2

    def body(*refs):
        sends, lands = _plan_all(plan, refs[:2 * n] + refs[2 * n:2 * n + 2], n)
        for cp in sends:
            cp.wait_send()
        for cp in lands:
            cp.wait_recv()

    outs = pl.pallas_call(
        body, name=name, out_shape=tuple(pltpu.HBM(a.shape, a.dtype) for a in arrays),
        in_specs=(HBM,) * (2 * n) + (SEM, SEM, ANY), out_specs=(HBM,) * (2 * n),
        input_output_aliases={t: t for t in range(2 * n)},
        compiler_params=pltpu.CompilerParams(has_side_effects=_EFFECT),
    )(*arrays, send_sems, recv_sems, after)
    return outs[:n], outs[n:]


def _tie(x, token, name):
    def body(x_ref, t_ref, o_ref):
        pass

    return pl.pallas_call(
        body, name=name, in_specs=[ANY, pl.BlockSpec(memory_space=pltpu.VMEM)], out_specs=ANY,
        out_shape=jax.ShapeDtypeStruct(x.shape, x.dtype), input_output_aliases={0: 0},
    )(x, token)


def _swap_sibling(ps, name):
    n = len(ps)

    def body(*refs):
        x, y, c = _coords()
        cps = [pltpu.make_async_remote_copy(
            src_ref=refs[t], dst_ref=refs[n + t], send_sem=refs[2 * n].at[t], recv_sem=refs[2 * n + 1].at[t],
            device_id=(x, y, 1 - c), device_id_type=MESH) for t in range(n)]
        for cp in cps:
            cp.start()
        for cp in cps:
            cp.wait()

    return pl.pallas_call(
        body, name=name, in_specs=[ANY] * n, out_specs=[ANY] * n,
        out_shape=[jax.ShapeDtypeStruct(p.shape, p.dtype) for p in ps],
        scratch_shapes=[pltpu.SemaphoreType.DMA((n,)), pltpu.SemaphoreType.DMA((n,))],
        compiler_params=pltpu.CompilerParams(has_side_effects=True),
    )(*ps)


_ADD_BYTES = 7 << 19


def _add_tile(rows, cols):
    best = 32
    for t in range(32, rows + 1, 32):
        if rows % t == 0 and t * cols * 4 <= _ADD_BYTES:
            best = t
    return best


def _add_slabs(pack, land, me, name):
    rows, cols = pack.shape[1:]
    tr = _add_tile(rows, cols)

    def body(me_ref, p_ref, l_ref, o_ref):
        f = lambda r: r.astype(F32)
        o_ref[...] = ((f(p_ref[0]) + f(l_ref[0])) + f(l_ref[1])) + f(l_ref[2])

    return pl.pallas_call(
        body, name=name,
        grid_spec=pltpu.PrefetchScalarGridSpec(
            num_scalar_prefetch=1, grid=(rows // tr,),
            in_specs=[pl.BlockSpec((1, tr, cols), lambda i, me_ref: (me_ref[0], i, 0)),
                      pl.BlockSpec((_NCOPY, tr, cols), lambda i, me_ref: (0, i, 0))],
            out_specs=pl.BlockSpec((tr, cols), lambda i, me_ref: (i, 0))),
        out_shape=jax.ShapeDtypeStruct((rows, cols), F32),
        compiler_params=_params(("parallel",)),
    )(me, pack, land)


_STAGE_W = 1024


def _stage_rows(shapes):
    pieces, r = [], 0
    for i, (k, w) in enumerate(shapes):
        for a in range(k):
            for q in range(0, w, _STAGE_W):
                pieces.append((i, a, q, min(_STAGE_W, w - q), r))
                r += 1
    return pieces, -(-r // 8) * 8


def _gather8(parts, reduce, name):
    shapes = [p.shape for p in parts]
    pieces, rows = _stage_rows(shapes)
    n = len(parts)

    def body(*refs):
        ins, outs = refs[:n], refs[n:2 * n]
        stage, buf, res, send_sems, recv_sems = refs[2 * n:]
        x, y, c = _coords()
        me = 4 * x + 2 * y + c
        stage[...] = jnp.zeros_like(stage)
        for i, a, q, w, r in pieces:
            stage[r:r + 1, 0:w] = ins[i][a:a + 1, q:q + w]
        buf[pl.ds(me, 1)] = stage[...][None]
        cps, lands = [], []
        for k in range(1, NDEV):
            peer = (1 - x if k & 4 else x, 1 - y if k & 2 else y, 1 - c if k & 1 else c)

            def copy(slot):
                return pltpu.make_async_remote_copy(
                    src_ref=stage, dst_ref=buf.at[slot], send_sem=send_sems.at[k - 1],
                    recv_sem=recv_sems.at[k - 1], device_id=peer, device_id_type=MESH)

            cps.append(copy(me))
            lands.append(copy(4 * peer[0] + 2 * peer[1] + peer[2]))
        for cp in cps:
            cp.start()
        for cp, land in zip(cps, lands):
            land.wait_recv()
            cp.wait_send()
        if reduce:
            acc = buf[0]
            for d in range(1, NDEV):
                acc = acc + buf[d]
            res[...] = acc
            for i, a, q, w, r in pieces:
                outs[i][a:a + 1, q:q + w] = res[r:r + 1, 0:w]
        else:
            for i, a, q, w, r in pieces:
                for s in range(NCHIP):
                    outs[i][s, a:a + 1, q:q + w] = buf[2 * s, r:r + 1, 0:w]

    vm = pl.BlockSpec(memory_space=pltpu.VMEM)
    out_shapes = [jax.ShapeDtypeStruct(s if reduce else (NCHIP,) + s, F32) for s in shapes]
    return pl.pallas_call(
        body, name=name, in_specs=[vm] * n, out_specs=[vm] * n, out_shape=out_shapes,
        scratch_shapes=[pltpu.VMEM((rows, _STAGE_W), F32), pltpu.VMEM((NDEV, rows, _STAGE_W), F32),
                        pltpu.VMEM((rows, _STAGE_W), F32), pltpu.SemaphoreType.DMA((NDEV - 1,)),
                        pltpu.SemaphoreType.DMA((NDEV - 1,))],
        compiler_params=pltpu.CompilerParams(has_side_effects=True),
    )(*parts)


def _adamw_update(w_ref, g_ref, m_ref, v_ref, d_ref, mo_ref, vo_ref):
    c1 = 1.0 / (1.0 - ADAM_B1 ** ADAM_STEP)
    c2 = 1.0 / (1.0 - ADAM_B2 ** ADAM_STEP)
    gv = g_ref[...]
    mn = ADAM_B1 * m_ref[...] + (1.0 - ADAM_B1) * gv
    vn = ADAM_B2 * v_ref[...] + (1.0 - ADAM_B2) * (gv * gv)
    d_ref[...] = -ADAM_LR * ((mn * c1) / (jnp.sqrt(vn * c2) + ADAM_EPS) + ADAM_WD * w_ref[...])
    mo_ref[...] = mn
    vo_ref[...] = vn


def _adamw_small(ws, gs, ms, vs):
    n = len(ws)

    def body(*refs):
        for i in range(n):
            _adamw_update(*(refs[j * n + i] for j in range(7)))

    vm = pl.BlockSpec(memory_space=pltpu.VMEM)
    outs = pl.pallas_call(
        body, name="adamw_small", in_specs=[vm] * (4 * n), out_specs=[vm] * (3 * n),
        out_shape=[jax.ShapeDtypeStruct(w.shape, F32) for w in ws] * 3,
    )(*ws, *gs, *ms, *vs)
    return outs[:n], outs[n:2 * n], outs[2 * n:]


def _adamw(w, g_parts, m, v, name):
    rows, cols = w.shape
    tr = rows
    while tr * cols * 4 > (1 << 20) and tr % 16 == 0:
        tr //= 2

    def body(w_ref, ga_ref, gb_ref, m_ref, v_ref, g_ref, d_ref, mo_ref, vo_ref):
        g_ref[...] = ga_ref[...] + gb_ref[...]
        _adamw_update(w_ref, g_ref, m_ref, v_ref, d_ref, mo_ref, vo_ref)

    blk = pl.BlockSpec((tr, cols), lambda i: (i, 0))
    return pl.pallas_call(
        body, name=name, grid=(rows // tr,), in_specs=[blk] * 5, out_specs=[blk] * 4,
        out_shape=[jax.ShapeDtypeStruct((rows, cols), F32)] * 4, compiler_params=_params(("parallel",)),
    )(w, *g_parts, m, v)


def _by_chip(g, rr, cc, axis):
    if isinstance(g, tuple):
        n = NCHIP // len(g)
        return jnp.concatenate([h.reshape(rr, n, cc).transpose(1, 0, 2) for h in g], axis=0)
    return g.reshape(NCHIP, rr, cc) if axis == 0 else g.reshape(rr, NCHIP, cc).transpose(1, 0, 2)


_SMALL_REPL = ("norm_mix_w", "ssd_conv_b", "dt_bias", "a_log", "d_skip", "ssd_norm_w", "norm_ffn_w",
               "ffn_conv_b", "final_norm_w")
_SMALL_CONV = (("conv_a_w", 3, D), ("ssd_conv_w", 4, DX), ("ffn_conv_w", 3, FF))


def kernel(x, norm_mix_w, w_in, conv_a_w, w_a_out, ssd_conv_w, ssd_conv_b, dt_bias, a_log, d_skip, ssd_norm_w, w_s_out, w_o, norm_ffn_w, w_up, ffn_conv_w, ffn_conv_b, w_down, final_norm_w, loss_target, m_norm_mix_w, m_w_in, m_conv_a_w, m_w_a_out, m_ssd_conv_w, m_ssd_conv_b, m_dt_bias, m_a_log, m_d_skip, m_ssd_norm_w, m_w_s_out, m_w_o, m_norm_ffn_w, m_w_up, m_ffn_conv_w, m_ffn_conv_b, m_w_down, m_final_norm_w, v_norm_mix_w, v_w_in, v_conv_a_w, v_w_a_out, v_ssd_conv_w, v_ssd_conv_b, v_dt_bias, v_a_log, v_d_skip, v_ssd_norm_w, v_w_s_out, v_w_o, v_norm_ffn_w, v_w_up, v_ffn_conv_w, v_ffn_conv_b, v_w_down, v_final_norm_w):
    names = ("norm_mix_w", "w_in", "conv_a_w", "w_a_out", "ssd_conv_w", "ssd_conv_b", "dt_bias", "a_log", "d_skip",
             "ssd_norm_w", "w_s_out", "w_o", "norm_ffn_w", "w_up", "ffn_conv_w", "ffn_conv_b", "w_down", "final_norm_w")
    W = dict(zip(names, (norm_mix_w, w_in, conv_a_w, w_a_out, ssd_conv_w, ssd_conv_b, dt_bias, a_log, d_skip,
                         ssd_norm_w, w_s_out, w_o, norm_ffn_w, w_up, ffn_conv_w, ffn_conv_b, w_down, final_norm_w)))
    M = dict(zip(names, (m_norm_mix_w, m_w_in, m_conv_a_w, m_w_a_out, m_ssd_conv_w, m_ssd_conv_b, m_dt_bias, m_a_log,
                         m_d_skip, m_ssd_norm_w, m_w_s_out, m_w_o, m_norm_ffn_w, m_w_up, m_ffn_conv_w, m_ffn_conv_b,
                         m_w_down, m_final_norm_w)))
    V = dict(zip(names, (v_norm_mix_w, v_w_in, v_conv_a_w, v_w_a_out, v_ssd_conv_w, v_ssd_conv_b, v_dt_bias, v_a_log,
                         v_d_skip, v_ssd_norm_w, v_w_s_out, v_w_o, v_norm_ffn_w, v_w_up, v_ffn_conv_w, v_ffn_conv_b,
                         v_w_down, v_final_norm_w)))
    two_d = lambda a: a.reshape(-1, a.shape[-1])
    W2, M2, V2 = ({k: two_d(a) for k, a in t.items()} for t in (W, M, V))
    xi, yi, ci = _coords()
    me = 2 * xi + yi

    meidx = me.reshape(1).astype(jnp.int32)
    state = {}


    class Hooks(_Hooks):
        def before_in_proj(self, w_in):
            return _tie(w_in, state["rest_token"], "tie_ag_rest")

        def late_weights(self, wts, after):
            owns, lands = _split_wait("ag_rest_wait", state["rest"], after, _plan_bcast)
            full = {}
            for (n, rr, cc, axis), own, land in zip(_W_REST, owns, lands):
                slabs = lax.dynamic_update_slice(land, own[None], (me, 0, 0))
                full[n] = slabs.reshape(NCHIP * rr, cc) if axis == 0 else slabs.transpose(1, 0, 2).reshape(rr, NCHIP * cc)
            return {**wts, **full}

        def grads_ready(self, grads, tie):
            if "w_in" in grads:
                key, packs = "g_in", [_unpermute_w_in(grads["w_in"])]
            else:
                key = "g_rest"
                packs = [_by_chip(grads[n], rr, cc, axis)
                         for n, rr, cc, axis in _W_REST]
            lands = [lax.empty((_NCOPY,) + p.shape[1:], BF16) for p in packs]
            state[key], token = _split_start("rs_" + key + "_start", packs, lands, _plan_scatter)
            return _tie(tie, token, "tie_" + key)

        def mark(self, name, value):
            state[name] = value

    def reduced(key, after, group):
        packs, lands = _split_wait("rs_" + key + "_wait", state[key], after, _plan_scatter)
        mines = [_add_slabs(p, l, meidx, "rs_add_chips_" + n) for (n, *_), p, l in zip(group, packs, lands)]
        return dict(zip([n for n, *_ in group], zip(mines, _swap_sibling(mines, "rs_" + key + "_swap"))))

    w_in_slabs = _ag_weights(W2["w_in"].astype(BF16))
    wts = {k: W2[k] for k in _SMALL_REPL}
    conv_by_chip = _gather8([W2[n] for n, *_ in _SMALL_CONV], False, "ag_conv_weights")
    for (n, kk, width), stacked in zip(_SMALL_CONV, conv_by_chip):
        wts[n] = stacked.transpose(1, 0, 2).reshape(kk, width)
    rest = [W2[n].astype(BF16) for n, *_ in _W_REST]
    rest[0] = _tie(rest[0], conv_by_chip[0], "tie_ag_order")
    state["rest"], state["rest_token"] = _split_start(
        "ag_rest_start", rest, [lax.empty((NCHIP,) + r.shape, BF16) for r in rest], _plan_bcast)
    wts["w_in"] = _permute_w_in(w_in_slabs)

    loss8, grad_x, grads = _local_step(x[0], loss_target[0], wts, Hooks())

    gbig = {**reduced("g_rest", state["ssd_bwd"], _W_REST), **reduced("g_in", grad_x, _W_IN)}

    small_parts = [grads[n] for n in _SMALL_REPL] + [loss8[0:1]] + [grads[n] for n, *_ in _SMALL_CONV]
    small_g = _gather8(small_parts, True, "allreduce_small")
    gsm = dict(zip(_SMALL_REPL, small_g[:len(_SMALL_REPL)]))
    loss = small_g[len(_SMALL_REPL)][0, 0]
    for (n, kk, width), gfull in zip(_SMALL_CONV, small_g[len(_SMALL_REPL) + 1:]):
        cw = width // NCHIP
        gsm[n] = lax.dynamic_slice(gfull, (0, me * cw), (kk, cw))

    G, DW, NM, NV = {}, {}, {}, {}
    for n in [b[0] for b in _W_IN + _W_REST]:
        G[n], DW[n], NM[n], NV[n] = _adamw(W2[n], gbig[n], M2[n], V2[n], "adamw_" + n)
    sm_names = list(_SMALL_REPL) + [n for n, *_ in _SMALL_CONV]
    outs = _adamw_small(*([t[n] for n in sm_names] for t in (W2, gsm, M2, V2)))
    for t, vals in zip((DW, NM, NV), outs):
        t.update(zip(sm_names, vals))
    G.update(gsm)

    def shaped(t):
        return [t[n].reshape(W[n].shape) for n in names]

    return (loss, grad_x.reshape(x.shape), *shaped(G), *shaped(DW), *shaped(NM), *shaped(NV))
```

```python
import jax
import jax.numpy as jnp
from jax import lax
from jax.experimental import pallas as pl
from jax.experimental.pallas import tpu as pltpu

F32 = jnp.float32
BF16 = jnp.bfloat16

D = 1024
DI = 2048
NH = 32
HP = 64
NG = 4
NS = 128
CH = 128
DX = 3072
FF = 2816
NI = 10272
EPS = 1e-5

OFF_BCV, OFF_XBC, OFF_G, OFF_Z, OFF_DT = 0, 3072, 6144, 8192, 10240
NIP = 10752
_SEGS = ((0, 2048, OFF_G), (2048, 3072, OFF_BCV), (5120, 2048, OFF_Z), (7168, 3072, OFF_XBC), (10240, 32, OFF_DT))

LANES = 128
HALO = 16
V7X_VMEM_LIMIT = 56 * 2 ** 20

ADAM_LR, ADAM_B1, ADAM_B2, ADAM_EPS, ADAM_WD, ADAM_STEP = 0.001, 0.9, 0.999, 1e-08, 0.01, 10

NN = (((1,), (0,)), ((), ()))
NT = (((1,), (1,)), ((), ()))
TN = (((0,), (0,)), ((), ()))


def _dot(a, b, dims=NN):
    return lax.dot_general(a, b, dims, preferred_element_type=F32)


def _params(sem, **kw):
    return pltpu.CompilerParams(dimension_semantics=sem, vmem_limit_bytes=V7X_VMEM_LIMIT, **kw)


V7X_MXU = 256
V7X_HBM_BYTES_PER_S = 3.5e12
STEP_S = 0.35e-6
MATMUL_VMEM = 40 * 2 ** 20
EPILOGUE_VMEM = 46 * 2 ** 20


ACC_BYTES_PER_S = 1.2e13


def _divisors(dim, cap, units):
    for unit in units:
        c = [t for t in range(unit, min(dim, cap) + 1, unit) if dim % t == 0]
        if c:
            return c
    return [dim]


def _tiles(M, N, K, out_bytes, has_res):
    best = None
    for tn in _divisors(N, 2816, (V7X_MXU, LANES)):
        for tm in _divisors(M, 2816, (LANES,)):
            for tk in _divisors(K, 2816, (V7X_MXU, LANES)):
                nk, ni, nj = K // tk, M // tm, N // tn
                vmem = 4 * (tm * tk + tk * tn) + 2 * tm * tn * out_bytes
                vmem += (4 * tm * tn if nk > 1 else 0) + (8 * tm * tn if has_res else 0)
                if vmem > MATMUL_VMEM:
                    continue
                a_reads = M * K * 2 * (nj if nk > 1 else 1)
                b_reads = K * N * 2 * (ni if nk * nj > 1 else 1)
                cost = (a_reads + b_reads + M * N * out_bytes) / V7X_HBM_BYTES_PER_S + ni * nj * nk * STEP_S
                cost += (nk - 1) * M * N * 8 / ACC_BYTES_PER_S
                if best is None or cost < best[0]:
                    best = (cost, tm, tn, tk)
    assert best is not None, (M, N, K)
    return best[1:]


def _sigmoid(x):
    return 1.0 / (1.0 + jnp.exp(-x))


class _Epilogue:
    def __init__(self, fn, ins, outs, tile_bytes, in_windows=None, out_windows=None):
        self.fn, self.ins, self.outs, self.tile_bytes = fn, tuple(ins), tuple(outs), tile_bytes
        self.in_windows, self.out_windows = in_windows or {}, out_windows or {}


def _matmul(a, b, *, mode, out_dtype, name, residual=None, b_k_off=0, epilogue=None):
    if mode == "nn":
        (M, K), (K2, N) = a.shape, b.shape
    elif mode == "nt":
        (M, K), (N, K2) = a.shape, (b.shape[0], a.shape[1])
        assert b_k_off + K <= b.shape[1]
    else:
        (K, M), (K2, N) = a.shape, b.shape
    assert K == K2, (name, a.shape, b.shape)
    tm, tn, tk = _tiles(M, N, K, jnp.dtype(out_dtype).itemsize, residual is not None)
    if epilogue is not None:
        tn = N
        fits = [(K * N * 2 * (M // t) / V7X_HBM_BYTES_PER_S + (K // q - 1) * M * N * 8 / ACC_BYTES_PER_S
                 + (M // t) * (K // q) * STEP_S, t, q)
                for t in (1024, 512, 256) if M % t == 0 for q in _divisors(K, 2816, (V7X_MXU, LANES))
                if 4 * (t * q + q * tn) + (4 * t * tn if K > q else 0) + (8 * t * tn if residual is not None else 0)
                + 2 * t * epilogue.tile_bytes <= EPILOGUE_VMEM]
        _, tm, tk = min(fits)
    nk = K // tk
    if mode == "tn":
        a_spec = pl.BlockSpec((tk, tm), lambda i, j, k: (k, i))
    else:
        a_spec = pl.BlockSpec((tm, tk), lambda i, j, k: (i, k))
    if mode == "nt":
        assert b_k_off % tk == 0
        b_spec = pl.BlockSpec((tn, tk), lambda i, j, k: (j, k + b_k_off // tk))
    else:
        b_spec = pl.BlockSpec((tk, tn), lambda i, j, k: (k, j))
    dims = {"nn": NN, "nt": NT, "tn": TN}[mode]
    o_spec = pl.BlockSpec((tm, tn), lambda i, j, k: (i, j))
    has_res = residual is not None

    def rows_or_whole(shape, window=None):
        if window is not None:
            off, width = window
            return pl.BlockSpec((tm, width), lambda i, j, k: (i, off // width))
        if shape[0] == M:
            return pl.BlockSpec((tm,) + tuple(shape[1:]), lambda i, j, k: (i,) + (0,) * (len(shape) - 1))
        return pl.BlockSpec(tuple(shape), lambda i, j, k: (0,) * len(shape))

    n_in = 2 + has_res + (len(epilogue.ins) if epilogue else 0)
    n_out = len(epilogue.outs) if epilogue else 1

    def body(*refs):
        a_ref, b_ref = refs[:2]
        r_ref = refs[2] if has_res else None
        out_refs = refs[n_in:n_in + n_out]
        acc_ref = refs[-1]
        k = pl.program_id(2)
        part = _dot(a_ref[...], b_ref[...], dims)

        def finish(r):
            if has_res:
                r = r + r_ref[...].astype(F32)
            if epilogue is None:
                out_refs[0][...] = r.astype(out_dtype)
            else:
                epilogue.fn(r, refs[2 + has_res:n_in], out_refs, pl.program_id(0) == 0)

        if nk == 1:
            finish(part)
            return

        @pl.when(k == 0)
        def _():
            acc_ref[...] = part

        @pl.when(jnp.logical_and(k > 0, k < nk - 1))
        def _():
            acc_ref[...] += part

        @pl.when(k == nk - 1)
        def _():
            finish(acc_ref[...] + part)

    in_specs = [a_spec, b_spec] + ([o_spec] if has_res else [])
    args = (a, b) + ((residual,) if has_res else ())
    if epilogue is None:
        out_specs, out_shape = o_spec, jax.ShapeDtypeStruct((M, N), out_dtype)
        sem = ("parallel", "parallel", "arbitrary")
    else:
        in_specs += [rows_or_whole(x.shape, epilogue.in_windows.get(n)) for n, x in enumerate(epilogue.ins)]
        args += epilogue.ins
        out_specs = [rows_or_whole(o[0], epilogue.out_windows.get(n)) for n, o in enumerate(epilogue.outs)]
        out_shape = [jax.ShapeDtypeStruct(shp, dt) for shp, dt in epilogue.outs]
        sem = ("arbitrary", "arbitrary", "arbitrary")
    return pl.pallas_call(
        body, name=name, grid=(M // tm, N // tn, nk), in_specs=in_specs, out_specs=out_specs,
        out_shape=out_shape, scratch_shapes=[pltpu.VMEM((tm, tn), F32)] if nk > 1 else [],
        compiler_params=_params(sem),
    )(*args)


class _Rows:
    def __init__(self, T, tm):
        self.T, self.tm = T, min(tm, T // 2)
        self.nrow = T // self.tm
        self.r = self.tm // HALO
        self.nb = T // HALO

    def tile(self, w, cb=0, step=1):
        return pl.BlockSpec((self.tm, w), lambda j, i: (i, cb + step * j))

    def prev(self, w, cb=0, step=1):
        r = self.r
        return pl.BlockSpec((HALO, w), lambda j, i: (jnp.maximum(i * r - 1, 0), cb + step * j))

    def next(self, w, cb=0, step=1):
        r, nb = self.r, self.nb
        return pl.BlockSpec((HALO, w), lambda j, i: (jnp.minimum((i + 1) * r, nb - 1), cb + step * j))

    def colvec(self, k, w, cb=0, step=1):
        return pl.BlockSpec((k, w), lambda j, i: (0, cb + step * j))

    def call(self, body, name, ncol, in_specs, out_specs, out_shape, args, aliases=None):
        return pl.pallas_call(
            body, name=name, grid=(ncol, self.nrow), in_specs=in_specs, out_specs=out_specs,
            out_shape=out_shape, input_output_aliases=aliases or {},
            compiler_params=_params(("parallel", "arbitrary")),
        )(*args)


ANY = pl.BlockSpec(memory_space=pl.ANY)


def _shifts_causal(ext, nk, tm):
    out = []
    for k in range(nk):
        s = nk - 1 - k
        r = ext if s == 0 else pltpu.roll(ext, s, 0)
        out.append(r[HALO:])
    return out


def _shifts_anticausal(ext, nk, tm):
    n = ext.shape[0]
    out = []
    for k in range(nk):
        s = nk - 1 - k
        r = ext if s == 0 else pltpu.roll(ext, n - s, 0)
        out.append(r[:tm])
    return out


def _wsum(w, parts):
    acc = w[0:1, :] * parts[0]
    for k in range(1, len(parts)):
        acc = acc + w[k:k + 1, :] * parts[k]
    return acc


def _colsum(x):
    return jnp.sum(x, axis=0, keepdims=True)


def _acc_out(ref, val, first):
    @pl.when(first)
    def _():
        ref[...] = val

    @pl.when(jnp.logical_not(first))
    def _():
        ref[...] += val


def _acc_rows(ref, rows, first):
    for k, r in enumerate(rows):
        _acc_out(ref.at[k:k + 1, :], r, first)


def _norm_matmul(x, wn, b, name, b_f32=None):
    T, N = x.shape[0], b.shape[1]
    tm = min(1024, T)
    tn = max(t for t in _divisors(N, 2816, (V7X_MXU, LANES))
             if 8 * tm * D + 6 * tm * D + 4 * D * t + 4 * tm * t <= MATMUL_VMEM)

    extra = b_f32 is not None

    def body(*refs):
        x_ref, wn_ref, b_ref = refs[:3]
        o_ref, u_ref = refs[3 + extra:5 + extra]
        keep_ref = refs[-1]

        @pl.when(pl.program_id(1) == 0)
        def _():
            xv = x_ref[...]
            r = lax.rsqrt(jnp.mean(xv * xv, axis=-1, keepdims=True) + EPS)
            u = (xv * r * wn_ref[...]).astype(BF16)
            keep_ref[...] = u
            u_ref[...] = u
            if extra:
                refs[5 + extra][...] = _dot(u, refs[3][...])

        o_ref[...] = _dot(keep_ref[...], b_ref[...]).astype(BF16)

    rows = pl.BlockSpec((tm, D), lambda i, j: (i, 0))
    whole = lambda shape: pl.BlockSpec(shape, lambda i, j: (0, 0))
    narrow = pl.BlockSpec((tm, LANES), lambda i, j: (i, 0))
    return pl.pallas_call(
        body, name=name, grid=(T // tm, N // tn),
        in_specs=[rows, whole((1, D)), pl.BlockSpec((D, tn), lambda i, j: (0, j))] + [whole((D, LANES))] * extra,
        out_specs=[pl.BlockSpec((tm, tn), lambda i, j: (i, j)), rows] + [narrow] * extra,
        out_shape=[jax.ShapeDtypeStruct((T, N), BF16), jax.ShapeDtypeStruct((T, D), BF16)]
        + [jax.ShapeDtypeStruct((T, LANES), F32)] * extra,
        scratch_shapes=[pltpu.VMEM((tm, D), BF16)],
        compiler_params=_params(("parallel", "arbitrary")),
    )(*((x, wn, b) + ((b_f32,) if extra else ())))


def _rmsnorm_bwd_epilogue(x, w, dres):
    T = x.shape[0]

    def fn(dyv, ins, outs, first):
        x_ref, w_ref, dr_ref = ins
        dx_ref, dxb_ref, dw_ref = outs
        xv = x_ref[...]
        r = lax.rsqrt(jnp.mean(xv * xv, axis=-1, keepdims=True) + EPS)
        xh = xv * r
        dxh = dyv * w_ref[...]
        dx = r * (dxh - xh * jnp.mean(dxh * xh, axis=-1, keepdims=True)) + dr_ref[...]
        dx_ref[...] = dx
        dxb_ref[...] = dx.astype(BF16)
        _acc_out(dw_ref, _colsum(dyv * xh), first)

    return _Epilogue(fn, (x, w, dres), (((T, D), F32), ((T, D), BF16), ((1, D), F32)), 14 * D)


def _branch_a_fwd(proj, conv_w):
    T = proj.shape[0]
    R = _Rows(T, 512)
    tm = R.tm

    def body(p_ref, pp_ref, w_ref, o_ref):
        keep = (pl.program_id(1) > 0).astype(F32)
        cv = p_ref[:, D:2 * D].astype(F32) * p_ref[:, 2 * D:].astype(F32)
        cvp = pp_ref[:, D:2 * D].astype(F32) * pp_ref[:, 2 * D:].astype(F32) * keep
        sh = _shifts_causal(jnp.concatenate([cvp, cv], axis=0), 3, tm)
        ca = _wsum(w_ref[...], sh)
        o_ref[...] = (p_ref[:, :D].astype(F32) * ca).astype(BF16)

    return R.call(body, "branch_a_fwd", 1, [R.tile(3 * D), R.prev(3 * D), R.colvec(3, D)], R.tile(D),
                  jax.ShapeDtypeStruct((T, D), BF16), (proj, proj, conv_w))


def _branch_a_bwd(dya_in, proj, conv_w, dproj):
    T = proj.shape[0]
    R = _Rows(T, 256)
    tm = R.tm

    def body(d_ref, dn_ref, p_ref, pp_ref, pn_ref, w_ref, _alias, o_ref, dw_ref):
        i = pl.program_id(1)
        keep_p = (i > 0).astype(F32)
        keep_n = (i < R.nrow - 1).astype(F32)
        w = w_ref[...]
        b = p_ref[:, :D].astype(F32)
        c = p_ref[:, D:2 * D].astype(F32)
        v = p_ref[:, 2 * D:].astype(F32)
        cvp = pp_ref[:, D:2 * D].astype(F32) * pp_ref[:, 2 * D:].astype(F32) * keep_p
        sh = _shifts_causal(jnp.concatenate([cvp, c * v], axis=0), 3, tm)
        ca = _wsum(w, sh)
        d = d_ref[...].astype(F32)
        dca = d * b
        dca_n = dn_ref[...].astype(F32) * pn_ref[:, :D].astype(F32) * keep_n
        dsh = _shifts_anticausal(jnp.concatenate([dca, dca_n], axis=0), 3, tm)
        dcv = _wsum(w, dsh)
        o_ref[:, :D] = (d * ca).astype(BF16)
        o_ref[:, D:2 * D] = (dcv * v).astype(BF16)
        o_ref[:, 2 * D:] = (dcv * c).astype(BF16)
        _acc_rows(dw_ref, [_colsum(dca * s) for s in sh], i == 0)

    return R.call(
        body, "branch_a_bwd", 1,
        [R.tile(D), R.next(D), R.tile(3 * D), R.prev(3 * D), R.next(3 * D), R.colvec(3, D), ANY],
        [R.tile(3 * D), R.colvec(3, D)],
        [jax.ShapeDtypeStruct(dproj.shape, BF16), jax.ShapeDtypeStruct((3, D), F32)],
        (dya_in, dya_in, proj, proj, proj, conv_w, dproj), aliases={6: 0})


_XW = 512


def _xbc_fwd(proj, conv_w, conv_b):
    T = proj.shape[0]
    R = _Rows(T, 512)
    tm = R.tm
    cb = OFF_XBC // _XW

    def body(x_ref, xp_ref, w_ref, b_ref, o_ref):
        keep = (pl.program_id(1) > 0).astype(F32)
        ext = jnp.concatenate([xp_ref[...].astype(F32) * keep, x_ref[...].astype(F32)], axis=0)
        pre = _wsum(w_ref[...], _shifts_causal(ext, 4, tm)) + b_ref[...]
        o_ref[...] = (pre * _sigmoid(pre)).astype(BF16)

    return R.call(body, "xbc_fwd", DX // _XW,
                  [R.tile(_XW, cb), R.prev(_XW, cb), R.colvec(4, _XW), R.colvec(1, _XW)], R.tile(_XW),
                  jax.ShapeDtypeStruct((T, DX), BF16), (proj, proj, conv_w, conv_b))


def _xbc_bwd(dact, proj, conv_w, conv_b, dproj):
    T = proj.shape[0]
    R = _Rows(T, 512)
    tm = R.tm
    cb = OFF_XBC // _XW

    def body(d_ref, dn_ref, x_ref, xp_ref, xn_ref, w_ref, b_ref, _alias, o_ref, dw_ref, db_ref):
        i = pl.program_id(1)
        keep_p = (i > 0).astype(F32)
        keep_n = (i < R.nrow - 1).astype(F32)
        w = w_ref[...]
        ext = jnp.concatenate([xp_ref[...].astype(F32) * keep_p, x_ref[...].astype(F32),
                               xn_ref[...].astype(F32)], axis=0)
        sh = _shifts_causal(ext, 4, tm + HALO)
        pre = _wsum(w, sh) + b_ref[...]
        s = _sigmoid(pre)
        dsilu = s * (1.0 + pre * (1.0 - s))
        dext = jnp.concatenate([d_ref[...].astype(F32), dn_ref[...].astype(F32) * keep_n], axis=0)
        dpre = dext * dsilu
        dsh = _shifts_anticausal(dpre, 4, tm)
        o_ref[...] = _wsum(w, dsh).astype(BF16)
        dp = dpre[:tm]
        _acc_rows(dw_ref, [_colsum(dp * q[:tm]) for q in sh], i == 0)
        _acc_out(db_ref, _colsum(dp), i == 0)

    return R.call(
        body, "xbc_bwd", DX // _XW,
        [R.tile(_XW), R.next(_XW), R.tile(_XW, cb), R.prev(_XW, cb), R.next(_XW, cb),
         R.colvec(4, _XW), R.colvec(1, _XW), ANY],
        [R.tile(_XW, cb), R.colvec(4, _XW), R.colvec(1, _XW)],
        [jax.ShapeDtypeStruct(dproj.shape, BF16), jax.ShapeDtypeStruct((4, DX), F32),
         jax.ShapeDtypeStruct((1, DX), F32)],
        (dact, dact, proj, proj, proj, conv_w, conv_b, dproj), aliases={7: 0})


def _softplus(x):
    return jnp.maximum(x, 0.0) + jnp.log(1.0 + jnp.exp(-jnp.abs(x)))


def _dt_rows(T):
    return min(8 * CH, T // 2)


def _dt_fwd(dt_raw, dt_bias_p, a_log_p):
    T = dt_raw.shape[0]
    rows = _dt_rows(T)

    def body(r_ref, b_ref, al_ref, dt_ref, ac_ref, acT_ref):
        dt = _softplus(r_ref[...] + b_ref[...])
        s = dt * (-jnp.exp(al_ref[...]))
        row = lax.broadcasted_iota(jnp.int32, (rows, LANES), 0) % CH
        k = 1
        while k < CH:
            s = s + jnp.where(row >= k, pltpu.roll(s, k, 0), 0.0)
            k *= 2
        dt_ref[...] = dt
        ac_ref[...] = s
        for q in range(0, rows, CH):
            acT_ref[q:q + CH] = s[q:q + CH].T

    blk = pl.BlockSpec((rows, LANES), lambda i: (i, 0))
    vec = pl.BlockSpec((1, LANES), lambda i: (0, 0))
    return pl.pallas_call(
        body, name="dt_fwd", grid=(T // rows,), in_specs=[blk, vec, vec], out_specs=[blk, blk, blk],
        out_shape=[jax.ShapeDtypeStruct((T, LANES), F32)] * 3, compiler_params=_params(("parallel",)),
    )(dt_raw, dt_bias_p, a_log_p)


def _dt_bwd(dacum, ddt_x, dt_raw, dt_bias_p, a_log_p, dproj):
    T = dt_raw.shape[0]
    rows = _dt_rows(T)
    nc = T // rows

    def body(da_ref, dx_ref, r_ref, b_ref, al_ref, _alias, o_ref, db_ref, dal_ref):
        i = pl.program_id(0)
        a = -jnp.exp(al_ref[...])
        z = r_ref[...] + b_ref[...]
        dt = _softplus(z)
        s = da_ref[...]
        row = lax.broadcasted_iota(jnp.int32, (rows, LANES), 0) % CH
        k = 1
        while k < CH:
            s = s + jnp.where(row < CH - k, pltpu.roll(s, rows - k, 0), 0.0)
            k *= 2
        ddt = s * a + dx_ref[...]
        draw = ddt * _sigmoid(z)
        o_ref[:, :LANES] = draw.astype(BF16)
        o_ref[:, LANES:] = jnp.zeros((rows, NIP - OFF_DT - LANES), BF16)
        _acc_out(db_ref, _colsum(draw), i == 0)
        _acc_out(dal_ref, _colsum(s * dt), i == 0)

        @pl.when(i == nc - 1)
        def _():
            dal_ref[...] = dal_ref[...] * a

    blk = pl.BlockSpec((rows, LANES), lambda i: (i, 0))
    vec = pl.BlockSpec((1, LANES), lambda i: (0, 0))
    oblk = pl.BlockSpec((rows, NIP - OFF_DT), lambda i: (i, OFF_DT // (NIP - OFF_DT)))
    return pl.pallas_call(
        body, name="dt_bwd", grid=(nc,), in_specs=[blk, blk, blk, vec, vec, ANY], out_specs=[oblk, vec, vec],
        out_shape=[jax.ShapeDtypeStruct(dproj.shape, BF16), jax.ShapeDtypeStruct((1, LANES), F32),
                   jax.ShapeDtypeStruct((1, LANES), F32)],
        input_output_aliases={5: 0}, compiler_params=_params(("arbitrary",)),
    )(dacum, ddt_x, dt_raw, dt_bias_p, a_log_p, dproj)


_GW = DI // NG
_HG = NH // NG
_NEG = -1e30


def _interleave(gens):
    out, live = [None] * len(gens), list(range(len(gens)))
    while live:
        for i in list(live):
            try:
                next(gens[i])
            except StopIteration as stop:
                out[i] = stop.value
                live.remove(i)
    return out


def _pair_lanes(left, v0, v1):
    return jnp.where(left, v0, v1)


def _hi_lo(v):
    hi = v.astype(BF16)
    return jnp.concatenate([hi, (v - hi.astype(F32)).astype(BF16)], axis=1)


def _head_spread():
    row = lax.broadcasted_iota(jnp.int32, (2 * LANES, 1), 0) % LANES
    return (row == lax.broadcasted_iota(jnp.int32, (1, DI), 1) // HP).astype(BF16)


def _ssd_specs(T, rev):
    nc = T // CH
    cm = (lambda c: nc - 1 - c) if rev else (lambda c: c)
    bw = NG * NS
    return dict(
        xs=pl.BlockSpec((CH, DI), lambda c: (cm(c), 0)),
        bm=pl.BlockSpec((CH, bw), lambda c: (cm(c), DI // bw)),
        cmat=pl.BlockSpec((CH, bw), lambda c: (cm(c), DI // bw + 1)),
        xbc=pl.BlockSpec((CH, DX), lambda c: (cm(c), 0)),
        col=pl.BlockSpec((CH, LANES), lambda c: (cm(c), 0)),
        dsk=pl.BlockSpec((1, DI), lambda c: (0, 0)),
        state=pl.BlockSpec((1, NS, DI), lambda c: (cm(c), 0, 0)),
    )


def _last(ref, lo, hi):
    return ref.at[(slice(None),) * (len(ref.shape) - 1) + (slice(lo, hi),)]


def _group_views(g, wide, narrow):
    return [_last(r, g * _GW, (g + 1) * _GW) for r in wide] + [_last(r, g * NS, (g + 1) * NS) for r in narrow]


def _ssd_fwd(xact, dt, acum, acumT, dsk_rep, proj, norm_w):
    T = xact.shape[0]
    nc = T // CH
    sp = _ssd_specs(T, False)

    def body(*refs):
        xs, bm, cmat, dtr, acr, actr, dsk, zr, nw, spread_ref, y, yn, spv, S_ref = refs

        @pl.when(pl.program_id(0) == 0)
        def _():
            S_ref[...] = jnp.zeros_like(S_ref)

        ac = acr[...]
        cols = [_hi_lo(v) for v in (dtr[...], jnp.exp(ac), jnp.exp(ac[CH - 1:CH, :] - ac))]
        _interleave([group(g * _HG, cols, ac, actr[...], _last(spread_ref, g * _GW, (g + 1) * _GW),
                           *_group_views(g, (xs, dsk, zr, nw, y, yn, spv, S_ref), (bm, cmat))) for g in range(NG)])

    def group(hb, cols, ac, acT, spread_ref, xs_ref, dsk_ref, z_ref, nw_ref, y_ref, yn_ref, sp_ref, S_ref, b_ref, c_ref):
        dtl, eal, dtel = (_dot(v, spread_ref[...]) for v in cols)
        Bm, Cm = b_ref[...], c_ref[...]
        S = S_ref[...]
        sp_ref[0] = S
        cb = _dot(Cm, Bm, NT)
        CS = _dot(Cm, S.astype(BF16))
        row = lax.broadcasted_iota(jnp.int32, (CH, CH), 0)
        col = lax.broadcasted_iota(jnp.int32, (CH, CH), 1)
        tril = row >= col
        left = col < HP
        xd_parts = []
        for p in range(_HG // 2):
            sl = slice(p * LANES, (p + 1) * LANES)
            j0, j1 = hb + 2 * p, hb + 2 * p + 1
            xp = xs_ref[:, sl].astype(F32)
            a0, a1 = ac[:, j0:j0 + 1], ac[:, j1:j1 + 1]
            X = xp * dtl[:, sl]
            Xb = X.astype(BF16)
            Ws = [(cb * jnp.exp(jnp.where(tril, aj - acT[j:j + 1, :], _NEG))).astype(BF16)
                  for j, aj in ((j0, a0), (j1, a1))]
            Xs = [jnp.where(m, Xb, jnp.zeros_like(Xb)) for m in (left, jnp.logical_not(left))]
            yield
            yd = _dot(jnp.concatenate(Ws, axis=1), jnp.concatenate(Xs, axis=0))
            yield
            y = yd + eal[:, sl] * CS[:, sl] + dsk_ref[:, sl] * xp
            y_ref[:, sl] = y.astype(BF16)
            xd_parts.append(X * dtel[:, sl])
        Xd = jnp.concatenate(xd_parts, axis=1).astype(BF16)
        S_ref[...] = eal[CH - 1:CH, :] * S + _dot(Bm, Xd, TN)
        yield
        z = z_ref[...].astype(F32)
        yf = y_ref[...].astype(F32) * z * _sigmoid(z)
        r = lax.rsqrt(jnp.mean(yf * yf, axis=-1, keepdims=True) + EPS)
        yn_ref[...] = (yf * r * nw_ref[...]).astype(BF16)

    zspec = pl.BlockSpec((CH, DI), lambda c: (c, OFF_Z // DI))
    return pl.pallas_call(
        body, name="ssd_fwd", grid=(nc,),
        in_specs=[sp["xs"], sp["bm"], sp["cmat"], sp["col"], sp["col"], sp["col"], sp["dsk"], zspec, sp["dsk"],
                  pl.BlockSpec((2 * LANES, DI), lambda c: (0, 0))],
        out_specs=[sp["xs"], sp["xs"], sp["state"]],
        out_shape=[jax.ShapeDtypeStruct((T, DI), BF16), jax.ShapeDtypeStruct((T, DI), BF16),
                   jax.ShapeDtypeStruct((nc, NS, DI), F32)],
        scratch_shapes=[pltpu.VMEM((NS, DI), F32)],
        compiler_params=_params(("arbitrary",)),
    )(xact, xact, xact, dt, acum, acumT, dsk_rep, proj, norm_w, _head_spread())


def _ssd_bwd(dn, y, proj, norm_w, dproj, xact, dt, acum, acumT, dsk_rep, sprev):
    T = xact.shape[0]
    nc = T // CH
    sp = _ssd_specs(T, True)

    def body(*refs):
        (xs, bm, cmat, dtr, acr, actr, dsk, dnr, yr, zr, nw, spv, _alias, lanes_of_ref, rows_of_ref, spread_ref,
         dxa, ddtx, dAc, dskacc, dzr, dnw, dS_ref) = refs
        first = pl.program_id(0) == 0

        @pl.when(first)
        def _():
            dS_ref[...] = jnp.zeros_like(dS_ref)

        dbc = _last(dxa, DI, DX)
        ddtx_sum = jnp.zeros((CH, LANES), F32)
        dAc_sum = jnp.zeros((CH, LANES), F32)
        ac = acr[...]
        cols = [_hi_lo(v) for v in (dtr[...], jnp.exp(ac), jnp.exp(ac[CH - 1:CH, :] - ac))]
        for a, b in _interleave([group(first, g * _HG, cols, ac, actr[...],
                                       lanes_of_ref.at[g * _GW:(g + 1) * _GW],
                                       rows_of_ref.at[g * _HG * CH:(g + 1) * _HG * CH],
                                       _last(spread_ref, g * _GW, (g + 1) * _GW),
                                       *_group_views(g, (xs, dsk, dnr, yr, zr, nw, dzr, dnw, spv, dxa, dskacc, dS_ref),
                                                     (bm, cmat, dbc, _last(dbc, NG * NS, 2 * NG * NS))))
                                 for g in range(NG)]):
            ddtx_sum, dAc_sum = ddtx_sum + a, dAc_sum + b
        ddtx[...] = ddtx_sum
        dAc[...] = dAc_sum

    def group(first, hb, cols, ac, acT, lanes_of_ref, rows_of_ref, spread_ref, xs_ref, dsk_ref, dn_ref, y_ref, z_ref,
              nw_ref, dz_ref, dnw_ref, sp_ref, dx_ref, dskacc_ref, dS_ref, b_ref, c_ref, dB_ref, dC_ref):
        z = z_ref[...].astype(F32)
        yv = y_ref[...].astype(F32)
        sg = _sigmoid(z)
        silu = z * sg
        yf = yv * silu
        rn = lax.rsqrt(jnp.mean(yf * yf, axis=-1, keepdims=True) + EPS)
        yh = yf * rn
        dnv = dn_ref[...].astype(F32)
        dyh = dnv * nw_ref[...]
        dyf = rn * (dyh - yh * jnp.mean(dyh * yh, axis=-1, keepdims=True))
        dyg = dyf * silu
        dz_ref[...] = (dyf * yv * sg * (1.0 + z * (1.0 - sg))).astype(BF16)
        _acc_out(dnw_ref, _colsum(dnv * yh), first)
        Bm, Cm = b_ref[...], c_ref[...]
        S = sp_ref[0]
        dS = dS_ref[...]
        Sb, dSb = S.astype(BF16), dS.astype(BF16)
        cb = _dot(Cm, Bm, NT)
        cbT = _dot(Bm, Cm, NT)
        CmT = Cm.T
        CS = _dot(Cm, Sb)
        T1 = _dot(Bm, dSb)
        yield
        row = lax.broadcasted_iota(jnp.int32, (CH, CH), 0)
        col = lax.broadcasted_iota(jnp.int32, (CH, CH), 1)
        tril = row >= col
        triu = row <= col
        left = col < HP
        lastrow = lax.broadcasted_iota(jnp.int32, (CH, 1), 0) == CH - 1
        dCB = jnp.zeros((CH, CH), F32)
        dCBT = jnp.zeros((CH, CH), F32)
        xd_parts, dye_parts, dsk_parts, dxx_parts, gr_parts, end_parts, qd_parts = ([] for _ in range(7))
        dtls, eals, dtels = (_dot(v, spread_ref[...]) for v in cols)
        for p in range(_HG // 2):
            sl = slice(p * LANES, (p + 1) * LANES)
            j0, j1 = hb + 2 * p, hb + 2 * p + 1
            xp = xs_ref[:, sl].astype(F32)
            dyp = dyg[:, sl]
            a0, a1 = ac[:, j0:j0 + 1], ac[:, j1:j1 + 1]
            dtl, eal, dtel = dtls[:, sl], eals[:, sl], dtels[:, sl]
            X = xp * dtl
            Xb = X.astype(BF16)
            T1d = T1[:, sl] * dtel
            Rm = T1d * X
            decp = eal[CH - 1:CH, :]
            gr_parts.append(dyp * (eal * CS[:, sl]) - Rm)
            end_parts.append(Rm + decp * (dS[:, sl] * S[:, sl]))
            dXd = jnp.zeros((CH, LANES), F32)
            for j, aj, mask in ((j0, a0, left), (j1, a1, jnp.logical_not(left))):
                dYm = jnp.where(mask, dyp, 0.0).astype(BF16)
                dWm = _dot(dYm, Xb, NT)
                dWmT = _dot(Xb, dYm, NT)
                yield
                e = aj - acT[j:j + 1, :]
                P = dWm * jnp.exp(jnp.where(tril, e, _NEG))
                LmT = jnp.exp(jnp.where(triu, -e, _NEG))
                PT = dWmT * LmT
                dCB = dCB + P
                dCBT = dCBT + PT
                yield
                dXd = dXd + _dot((cbT * LmT).astype(BF16), dYm)
                qd_parts.append((P * cb - PT * cbT).astype(BF16))
                yield
            dX = dXd + T1d
            dxx_parts.append(dX * xp)
            dx_ref[:, sl] = (dX * dtl + dsk_ref[:, sl] * dyp).astype(BF16)
            dsk_parts.append(_colsum(dyp * xp))
            xd_parts.append(X * dtel)
            dye_parts.append(dyp * eal)
            yield
        Xd = jnp.concatenate(xd_parts, axis=1).astype(BF16)
        dYe = jnp.concatenate(dye_parts, axis=1).astype(BF16)
        def lane_sums(parts):
            return _dot(jnp.concatenate(parts, axis=1).astype(BF16), lanes_of_ref[...])
        ddtx = lane_sums(dxx_parts)
        dAc = (_dot(jnp.concatenate(qd_parts, axis=1), rows_of_ref[...]) + lane_sums(gr_parts)
               + jnp.where(lastrow, _colsum(lane_sums(end_parts)), 0.0))
        dC_ref[...] = (_dot(dCB.astype(BF16), Bm) + _dot(dYe, Sb, NT)).astype(BF16)
        dB_ref[...] = (_dot(dCBT.astype(BF16), Cm) + _dot(Xd, dSb, NT)).astype(BF16)
        dS_ref[...] = _dot(CmT, dYe) + eals[CH - 1:CH, :] * dS
        _acc_out(dskacc_ref, jnp.concatenate(dsk_parts, axis=1), first)
        return ddtx, dAc

    zspec = pl.BlockSpec((CH, DI), lambda c: (nc - 1 - c, OFF_Z // DI))
    head = lax.broadcasted_iota(jnp.int32, (1, LANES), 1)
    lanes_of = (lax.broadcasted_iota(jnp.int32, (DI, 1), 0) // HP == head).astype(BF16)
    rows_of = (lax.broadcasted_iota(jnp.int32, (NH * CH, 1), 0) // CH == head).astype(BF16)
    return pl.pallas_call(
        body, name="ssd_bwd", grid=(nc,),
        in_specs=[sp["xs"], sp["bm"], sp["cmat"], sp["col"], sp["col"], sp["col"], sp["dsk"], sp["xs"], sp["xs"],
                  zspec, sp["dsk"], sp["state"], ANY, pl.BlockSpec(lanes_of.shape, lambda c: (0, 0)),
                  pl.BlockSpec(rows_of.shape, lambda c: (0, 0)), pl.BlockSpec((2 * LANES, DI), lambda c: (0, 0))],
        out_specs=[sp["xbc"], sp["col"], sp["col"], sp["dsk"], zspec, sp["dsk"]],
        out_shape=[jax.ShapeDtypeStruct((T, DX), BF16), jax.ShapeDtypeStruct((T, LANES), F32),
                   jax.ShapeDtypeStruct((T, LANES), F32), jax.ShapeDtypeStruct((1, DI), F32),
                   jax.ShapeDtypeStruct(dproj.shape, BF16), jax.ShapeDtypeStruct((1, DI), F32)],
        scratch_shapes=[pltpu.VMEM((NS, DI), F32)], input_output_aliases={12: 4},
        compiler_params=_params(("arbitrary",)),
    )(xact, xact, xact, dt, acum, acumT, dsk_rep, dn, y, proj, norm_w, sprev, dproj, lanes_of, rows_of, _head_spread())


def _merge_fwd_epilogue(proj, ya):
    T = proj.shape[0]

    def fn(ysv, ins, outs, first):
        g_ref, ya_ref = ins
        m_ref, ys_ref = outs
        ga = _sigmoid(g_ref[:, :D].astype(F32))
        gs = _sigmoid(g_ref[:, D:].astype(F32))
        m_ref[...] = (ga * ya_ref[...].astype(F32) + gs * ysv).astype(BF16)
        ys_ref[...] = ysv.astype(BF16)

    return _Epilogue(fn, (proj, ya), (((T, D), BF16), ((T, D), BF16)), 10 * D, in_windows={0: (OFF_G, 2 * D)})


def _merge_bwd_epilogue(proj, ya, ys, ncols):
    T = proj.shape[0]

    def fn(d, ins, outs, first):
        g_ref, ya_ref, ys_ref = ins
        dg_ref, dya_ref, dys_ref = outs
        ga = _sigmoid(g_ref[:, :D].astype(F32))
        gs = _sigmoid(g_ref[:, D:].astype(F32))
        dya_ref[...] = (d * ga).astype(BF16)
        dys_ref[...] = (d * gs).astype(BF16)
        dg_ref[:, :D] = (d * ya_ref[...].astype(F32) * ga * (1.0 - ga)).astype(BF16)
        dg_ref[:, D:] = (d * ys_ref[...].astype(F32) * gs * (1.0 - gs)).astype(BF16)

    window = (OFF_G, 2 * D)
    return _Epilogue(fn, (proj, ya, ys), (((T, ncols), BF16), ((T, D), BF16), ((T, D), BF16)), 16 * D,
                     in_windows={0: window}, out_windows={0: window})


_FW = 1408
_FB = FF // _FW


def _ffn_act_fwd(hv, conv_w, conv_b):
    T = hv.shape[0]
    R = _Rows(T, 256)
    tm = R.tm

    def body(h1_ref, h1p_ref, h3_ref, w_ref, b_ref, o_ref):
        keep = (pl.program_id(1) > 0).astype(F32)
        ext = jnp.concatenate([h1p_ref[...].astype(F32) * keep, h1_ref[...].astype(F32)], axis=0)
        pre = _wsum(w_ref[...], _shifts_causal(ext, 3, tm)) + b_ref[...]
        o_ref[...] = (pre * _sigmoid(pre) * h3_ref[...].astype(F32)).astype(BF16)

    return R.call(body, "ffn_act_fwd", _FB,
                  [R.tile(_FW), R.prev(_FW), R.tile(_FW, _FB), R.colvec(3, _FW), R.colvec(1, _FW)],
                  R.tile(_FW), jax.ShapeDtypeStruct((T, FF), BF16), (hv, hv, hv, conv_w, conv_b))


def _ffn_act_bwd(dg, hv, conv_w, conv_b):
    T = hv.shape[0]
    R = _Rows(T, 256)
    tm = R.tm

    def body(dg_ref, dgn_ref, h1_ref, h1p_ref, h1n_ref, h3_ref, h3n_ref, w_ref, b_ref, dh3_ref, dh1_ref, dw_ref,
             db_ref):
        i = pl.program_id(1)
        keep_p = (i > 0).astype(F32)
        keep_n = (i < R.nrow - 1).astype(F32)
        w = w_ref[...]
        ext = jnp.concatenate([h1p_ref[...].astype(F32) * keep_p, h1_ref[...].astype(F32),
                               h1n_ref[...].astype(F32)], axis=0)
        sh = _shifts_causal(ext, 3, tm + HALO)
        pre = _wsum(w, sh) + b_ref[...]
        s = _sigmoid(pre)
        d = jnp.concatenate([dg_ref[...].astype(F32), dgn_ref[...].astype(F32) * keep_n], axis=0)
        h3 = jnp.concatenate([h3_ref[...].astype(F32), h3n_ref[...].astype(F32)], axis=0)
        dh3_ref[...] = (d[:tm] * pre[:tm] * s[:tm]).astype(BF16)
        dpre = d * h3 * s * (1.0 + pre * (1.0 - s))
        dh1_ref[...] = _wsum(w, _shifts_anticausal(dpre, 3, tm)).astype(BF16)
        dp = dpre[:tm]
        _acc_rows(dw_ref, [_colsum(dp * q[:tm]) for q in sh], i == 0)
        _acc_out(db_ref, _colsum(dp), i == 0)

    return R.call(
        body, "ffn_act_bwd", _FB,
        [R.tile(_FW), R.next(_FW), R.tile(_FW), R.prev(_FW), R.next(_FW), R.tile(_FW, _FB), R.next(_FW, _FB),
         R.colvec(3, _FW), R.colvec(1, _FW)],
        [R.tile(_FW), R.tile(_FW), R.colvec(3, _FW), R.colvec(1, _FW)],
        [jax.ShapeDtypeStruct((T, FF), BF16), jax.ShapeDtypeStruct((T, FF), BF16),
         jax.ShapeDtypeStruct((3, FF), F32), jax.ShapeDtypeStruct((1, FF), F32)],
        (dg, dg, hv, hv, hv, hv, hv, conv_w, conv_b))


def _final_loss_epilogue(w, target):
    T = target.shape[0]

    def fn(xv, ins, outs, first):
        w_ref, t_ref = ins
        l_ref, dh_ref, dhb_ref, dw_ref = outs
        wv = w_ref[...]
        r = lax.rsqrt(jnp.mean(xv * xv, axis=-1, keepdims=True) + EPS)
        xh = xv * r
        err = xh * wv - t_ref[...]
        part = 0.5 * jnp.sum(jnp.mean(err * err, axis=-1, keepdims=True), axis=0, keepdims=True)
        _acc_out(l_ref, jnp.broadcast_to(part, l_ref.shape), first)
        dy = err * (1.0 / D)
        dxh = dy * wv
        dh = r * (dxh - xh * jnp.mean(dxh * xh, axis=-1, keepdims=True))
        dh_ref[...] = dh
        dhb_ref[...] = dh.astype(BF16)
        _acc_out(dw_ref, _colsum(dy * xh), first)

    return _Epilogue(fn, (w, target),
                     (((8, LANES), F32), ((T, D), F32), ((T, D), BF16), ((1, D), F32)), 10 * D)


def _pad_lanes(v, n=LANES):
    return jnp.pad(v, ((0, 0), (0, n - v.shape[1])))


class _Hooks:
    def before_in_proj(self, w_in):
        return w_in

    def late_weights(self, wts, after):
        return wts

    def grads_ready(self, grads, tie):
        return tie

    def mark(self, name, value):
        pass


def _local_step(x, target, wts, hooks=None):
    hooks = hooks or _Hooks()
    T = x.shape[0]
    w_in = wts["w_in"]
    dt_bias_p, a_log_p = _pad_lanes(wts["dt_bias"]), _pad_lanes(wts["a_log"])
    dsk_rep = jnp.repeat(wts["d_skip"], HP, axis=1)

    w_in = hooks.before_in_proj(w_in)
    proj, u, dt_raw = _norm_matmul(x, wts["norm_mix_w"], w_in, "norm_mm_in", w_in[:, OFF_DT:OFF_DT + LANES])
    ya_in = _branch_a_fwd(proj, wts["conv_a_w"])
    xact = _xbc_fwd(proj, wts["ssd_conv_w"], wts["ssd_conv_b"])
    dt, acum, acumT = _dt_fwd(dt_raw, dt_bias_p, a_log_p)
    y_ssd, yn, sprev = _ssd_fwd(xact, dt, acum, acumT, dsk_rep, proj, wts["ssd_norm_w"])
    late = hooks.late_weights(wts, yn)
    w_a_out, w_s_out, w_o, w_up, w_down = (late[k] for k in ("w_a_out", "w_s_out", "w_o", "w_up", "w_down"))
    y_a = _matmul(ya_in, w_a_out, mode="nn", out_dtype=BF16, name="mm_a_out")
    merged, y_s = _matmul(yn, w_s_out, mode="nn", out_dtype=BF16, name="mm_s_out_merge",
                          epilogue=_merge_fwd_epilogue(proj, y_a))
    h1 = _matmul(merged, w_o, mode="nn", out_dtype=F32, name="mm_o", residual=x)
    hv, v = _norm_matmul(h1, wts["norm_ffn_w"], w_up, "norm_mm_up")
    gact = _ffn_act_fwd(hv, wts["ffn_conv_w"], wts["ffn_conv_b"])
    loss, dh2, dh2b, g_final = _matmul(gact, w_down, mode="nn", out_dtype=F32, name="mm_down_loss", residual=h1,
                                       epilogue=_final_loss_epilogue(wts["final_norm_w"], target))

    grads = {"final_norm_w": g_final}
    grads["w_down"] = _matmul(gact, dh2b, mode="tn", out_dtype=BF16, name="mm_down_dw")
    dgact = _matmul(dh2b, w_down, mode="nt", out_dtype=BF16, name="mm_down_dx")
    dh3, dh1c, grads["ffn_conv_w"], grads["ffn_conv_b"] = _ffn_act_bwd(dgact, hv, wts["ffn_conv_w"], wts["ffn_conv_b"])
    grads["w_up"] = (_matmul(v, dh1c, mode="tn", out_dtype=BF16, name="mm_up_dw1"),
                     _matmul(v, dh3, mode="tn", out_dtype=BF16, name="mm_up_dw3"))
    dv = _matmul(dh1c, w_up, mode="nt", out_dtype=F32, name="mm_up_dx1")
    dh1, dh1b, grads["norm_ffn_w"] = _matmul(
        dh3, w_up, mode="nt", out_dtype=F32, name="mm_up_dx3_norm", residual=dv, b_k_off=FF,
        epilogue=_rmsnorm_bwd_epilogue(h1, wts["norm_ffn_w"], dh2))
    grads["w_o"] = _matmul(merged, dh1b, mode="tn", out_dtype=BF16, name="mm_o_dw")
    dproj, dya, dys = _matmul(dh1b, w_o, mode="nt", out_dtype=BF16, name="mm_o_dx_merge",
                              epilogue=_merge_bwd_epilogue(proj, y_a, y_s, NIP))
    grads["w_a_out"] = _matmul(ya_in, dya, mode="tn", out_dtype=BF16, name="mm_a_out_dw")
    dya_in = _matmul(dya, w_a_out, mode="nt", out_dtype=BF16, name="mm_a_out_dx")
    dproj, grads["conv_a_w"] = _branch_a_bwd(dya_in, proj, wts["conv_a_w"], dproj)
    grads["w_s_out"] = _matmul(yn, dys, mode="tn", out_dtype=BF16, name="mm_s_out_dw")
    dys = hooks.grads_ready({k: grads[k] for k in ("w_a_out", "w_s_out", "w_o", "w_up", "w_down")}, dys)
    dyn =_matmul(dys, w_s_out, mode="nt", out_dtype=BF16, name="mm_s_out_dx")
    dxact, ddt_x, dacum, dskl, dproj, grads["ssd_norm_w"] = _ssd_bwd(
        dyn, y_ssd, proj, wts["ssd_norm_w"], dproj, xact, dt, acum, acumT, dsk_rep, sprev)
    hooks.mark("ssd_bwd", dxact)
    grads["d_skip"] = dskl.reshape(NH, HP).sum(axis=1).reshape(1, NH)
    dproj, grads["ssd_conv_w"], grads["ssd_conv_b"] = _xbc_bwd(dxact, proj, wts["ssd_conv_w"], wts["ssd_conv_b"], dproj)
    dproj, g_dtb, g_alog = _dt_bwd(dacum, ddt_x, dt_raw, dt_bias_p, a_log_p, dproj)
    grads["dt_bias"], grads["a_log"] = g_dtb[:, :NH], g_alog[:, :NH]
    grads["w_in"] = _matmul(u, dproj, mode="tn", out_dtype=BF16, name="mm_in_dw")
    dproj = hooks.grads_ready({"w_in": grads["w_in"]}, dproj)
    grad_x, _, grads["norm_mix_w"] = _matmul(dproj, w_in, mode="nt", out_dtype=F32, name="mm_in_dx_norm",
                                             epilogue=_rmsnorm_bwd_epilogue(x, wts["norm_mix_w"], dh1))
    return loss, grad_x, grads


def _permute_w_in(slabs):
    cs = slabs.shape[2]
    pieces = []
    for o, n, no in sorted(_SEGS, key=lambda seg: seg[2]):
        for s in range(slabs.shape[0]):
            lo, hi = max(o, s * cs), min(o + n, (s + 1) * cs)
            if lo < hi:
                pieces.append(slabs[s][:, lo - s * cs:hi - s * cs])
    pieces.append(jnp.zeros((slabs.shape[1], NIP - OFF_DT - _SEGS[-1][1]), slabs.dtype))
    return jnp.concatenate(pieces, axis=1)


def _unpermute_w_in(g):
    cs = NI // NCHIP
    slabs = []
    for s in range(NCHIP):
        pieces = []
        for o, n, no in sorted(_SEGS):
            lo, hi = max(o, s * cs), min(o + n, (s + 1) * cs)
            if lo < hi:
                pieces.append(g[:, no + lo - o:no + hi - o])
        slabs.append(jnp.concatenate(pieces, axis=1))
    return jnp.stack(slabs)


MESH = pl.DeviceIdType.MESH
NCHIP = 4
NDEV = 8

_W_IN = (("w_in", D, NI // NCHIP, 1),)
_W_REST = (("w_a_out", D // NCHIP, D, 0), ("w_s_out", DI // NCHIP, D, 0), ("w_o", D // NCHIP, D, 0),
           ("w_up", D, 2 * FF // NCHIP, 1), ("w_down", FF // NCHIP, D, 0))


def _coords():
    return lax.axis_index("x"), lax.axis_index("y"), lax.axis_index("c")


def _other_chips(x, y):
    return [(1 - x, y), (x, 1 - y), (1 - x, 1 - y)]


def _ag_weights(shard):
    nrows = shard.shape[0]
    hr = nrows // 2

    def body(x_ref, out_ref, send_sems, recv_sems, local_sem):
        x, y, c = _coords()
        me = 2 * x + y
        chips = _other_chips(x, y)

        def rows(s, h):
            return out_ref.at[s, pl.ds(h * hr, hr), :]

        def copy(k, s, h, to, src=None):
            return pltpu.make_async_remote_copy(
                src_ref=rows(s, h) if src is None else src, dst_ref=rows(s, h),
                send_sem=send_sems.at[k], recv_sem=recv_sems.at[k], device_id=to, device_id_type=MESH)

        mine = pltpu.make_async_copy(x_ref, out_ref.at[me], local_sem)
        mine.start()
        first = [copy(k, me, c, (*chip, c), src=x_ref.at[pl.ds(c * hr, hr), :]) for k, chip in enumerate(chips)]
        for cp in first:
            cp.start()
        passed = []
        for k, chip in enumerate(chips):
            s = 2 * chip[0] + chip[1]
            copy(k, s, c, (x, y, c)).wait_recv()
            fwd = copy(3 + k, s, c, (x, y, 1 - c))
            fwd.start()
            passed.append(fwd)
        for k, chip in enumerate(chips):
            copy(3 + k, 2 * chip[0] + chip[1], 1 - c, (x, y, c)).wait_recv()
        for cp in first + passed:
            cp.wait_send()
        mine.wait()

    return pl.pallas_call(
        body, name="ag_weights", in_specs=[ANY], out_specs=ANY,
        out_shape=jax.ShapeDtypeStruct((NCHIP,) + shard.shape, shard.dtype),
        scratch_shapes=[pltpu.SemaphoreType.DMA((6,)), pltpu.SemaphoreType.DMA((6,)), pltpu.SemaphoreType.DMA],
        compiler_params=pltpu.CompilerParams(has_side_effects=True),
    )(shard)


HBM = pl.BlockSpec(memory_space=pltpu.HBM)
SEM = pl.BlockSpec(memory_space=pltpu.SEMAPHORE)
_EFFECT = pltpu.SideEffectType.DATAFLOW_SIDE_EFFECTING
_NCOPY = NCHIP - 1


def _plan_bcast(src_ref, land_ref, send_sems, recv_sems, base):
    x, y, c = _coords()
    sends, lands = [], []
    for k, chip in enumerate(_other_chips(x, y)):
        def copy(slot):
            return pltpu.make_async_remote_copy(
                src_ref=src_ref, dst_ref=land_ref.at[slot], send_sem=send_sems.at[base + k],
                recv_sem=recv_sems.at[base + k], device_id=(*chip, c), device_id_type=MESH)
        sends.append(copy(2 * x + y))
        lands.append(copy(2 * chip[0] + chip[1]))
    return sends, lands


def _plan_scatter(src_ref, land_ref, send_sems, recv_sems, base):
    x, y, c = _coords()
    cps = [pltpu.make_async_remote_copy(
        src_ref=src_ref.at[2 * chip[0] + chip[1]], dst_ref=land_ref.at[k], send_sem=send_sems.at[base + k],
        recv_sem=recv_sems.at[base + k], device_id=(*chip, c), device_id_type=MESH)
        for k, chip in enumerate(_other_chips(x, y))]
    return cps, cps


def _plan_swap(src_ref, land_ref, send_sems, recv_sems, base):
    x, y, c = _coords()
    cp = pltpu.make_async_remote_copy(
        src_ref=src_ref, dst_ref=land_ref, send_sem=send_sems.at[base], recv_sem=recv_sems.at[base],
        device_id=(x, y, 1 - c), device_id_type=MESH)
    return [cp], [cp]


def _plan_all(plan, refs, n):
    sends, lands = [], []
    for t in range(n):
        s, l = plan(refs[t], refs[n + t], refs[2 * n], refs[2 * n + 1], t * _NCOPY)
        sends += s
        lands += l
    return sends, lands


def _split_start(name, srcs, lands, plan):
    n = len(srcs)

    def body(*refs):
        for cp in _plan_all(plan, refs, n)[0]:
            cp.start()
        refs[-1][...] = jnp.zeros_like(refs[-1])

    arrays = list(srcs) + list(lands)
    outs = pl.pallas_call(
        body, name=name,
        out_shape=(pltpu.SemaphoreType.DMA((n * _NCOPY,)), pltpu.SemaphoreType.DMA((n * _NCOPY,)),
                   *[pltpu.HBM(a.shape, a.dtype) for a in arrays], jax.ShapeDtypeStruct((8, LANES), F32)),
        in_specs=(HBM,) * (2 * n),
        out_specs=(SEM, SEM) + (HBM,) * (2 * n) + (pl.BlockSpec(memory_space=pltpu.VMEM),),
        input_output_aliases={t: 2 + t for t in range(2 * n)},
        compiler_params=pltpu.CompilerParams(has_side_effects=_EFFECT),
    )(*[pltpu.with_memory_space_constraint(a, pltpu.HBM) for a in arrays])
    return (outs[0], outs[1], tuple(outs[2:2 + 2 * n])), outs[-1]


def _split_wait(name, handle, after, plan):
    send_sems, recv_sems, arrays = handle
    n = len(arrays) // 2

    def body(*refs):
        sends, lands = _plan_all(plan, refs[:2 * n] + refs[2 * n:2 * n + 2], n)
        for cp in sends:
            cp.wait_send()
        for cp in lands:
            cp.wait_recv()

    outs = pl.pallas_call(
        body, name=name, out_shape=tuple(pltpu.HBM(a.shape, a.dtype) for a in arrays),
        in_specs=(HBM,) * (2 * n) + (SEM, SEM, ANY), out_specs=(HBM,) * (2 * n),
        input_output_aliases={t: t for t in range(2 * n)},
        compiler_params=pltpu.CompilerParams(has_side_effects=_EFFECT),
    )(*arrays, send_sems, recv_sems, after)
    return outs[:n], outs[n:]


def _tie(x, token, name):
    def body(x_ref, t_ref, o_ref):
        pass

    return pl.pallas_call(
        body, name=name, in_specs=[ANY, pl.BlockSpec(memory_space=pltpu.VMEM)], out_specs=ANY,
        out_shape=jax.ShapeDtypeStruct(x.shape, x.dtype), input_output_aliases={0: 0},
    )(x, token)


_ADD_BYTES = 7 << 19


def _add_tile(rows, cols):
    best = 32
    for t in range(32, rows + 1, 32):
        if rows % t == 0 and t * cols * 4 <= _ADD_BYTES:
            best = t
    return best


def _add_slabs(pack, land, me, name):
    rows, cols = pack.shape[1:]
    tr = _add_tile(rows, cols)

    def body(me_ref, p_ref, l_ref, o_ref):
        f = lambda r: r.astype(F32)
        o_ref[...] = ((f(p_ref[0]) + f(l_ref[0])) + f(l_ref[1])) + f(l_ref[2])

    return pl.pallas_call(
        body, name=name,
        grid_spec=pltpu.PrefetchScalarGridSpec(
            num_scalar_prefetch=1, grid=(rows // tr,),
            in_specs=[pl.BlockSpec((1, tr, cols), lambda i, me_ref: (me_ref[0], i, 0)),
                      pl.BlockSpec((_NCOPY, tr, cols), lambda i, me_ref: (0, i, 0))],
            out_specs=pl.BlockSpec((tr, cols), lambda i, me_ref: (i, 0))),
        out_shape=jax.ShapeDtypeStruct((rows, cols), F32),
        compiler_params=_params(("parallel",)),
    )(me, pack, land)


_STAGE_W = 1024


def _stage_rows(shapes):
    pieces, r = [], 0
    for i, (k, w) in enumerate(shapes):
        for a in range(k):
            for q in range(0, w, _STAGE_W):
                pieces.append((i, a, q, min(_STAGE_W, w - q), r))
                r += 1
    return pieces, -(-r // 8) * 8


def _gather8(parts, reduce, name):
    shapes = [p.shape for p in parts]
    pieces, rows = _stage_rows(shapes)
    n = len(parts)

    def body(*refs):
        ins, outs = refs[:n], refs[n:2 * n]
        stage, buf, res, send_sems, recv_sems = refs[2 * n:]
        x, y, c = _coords()
        me = 4 * x + 2 * y + c
        stage[...] = jnp.zeros_like(stage)
        for i, a, q, w, r in pieces:
            stage[r:r + 1, 0:w] = ins[i][a:a + 1, q:q + w]
        buf[pl.ds(me, 1)] = stage[...][None]
        cps, lands = [], []
        for k in range(1, NDEV):
            peer = (1 - x if k & 4 else x, 1 - y if k & 2 else y, 1 - c if k & 1 else c)

            def copy(slot):
                return pltpu.make_async_remote_copy(
                    src_ref=stage, dst_ref=buf.at[slot], send_sem=send_sems.at[k - 1],
                    recv_sem=recv_sems.at[k - 1], device_id=peer, device_id_type=MESH)

            cps.append(copy(me))
            lands.append(copy(4 * peer[0] + 2 * peer[1] + peer[2]))
        for cp in cps:
            cp.start()
        for cp, land in zip(cps, lands):
            land.wait_recv()
            cp.wait_send()
        if reduce:
            acc = buf[0]
            for d in range(1, NDEV):
                acc = acc + buf[d]
            res[...] = acc
            for i, a, q, w, r in pieces:
                outs[i][a:a + 1, q:q + w] = res[r:r + 1, 0:w]
        else:
            for i, a, q, w, r in pieces:
                for s in range(NCHIP):
                    outs[i][s, a:a + 1, q:q + w] = buf[2 * s, r:r + 1, 0:w]

    vm = pl.BlockSpec(memory_space=pltpu.VMEM)
    out_shapes = [jax.ShapeDtypeStruct(s if reduce else (NCHIP,) + s, F32) for s in shapes]
    return pl.pallas_call(
        body, name=name, in_specs=[vm] * n, out_specs=[vm] * n, out_shape=out_shapes,
        scratch_shapes=[pltpu.VMEM((rows, _STAGE_W), F32), pltpu.VMEM((NDEV, rows, _STAGE_W), F32),
                        pltpu.VMEM((rows, _STAGE_W), F32), pltpu.SemaphoreType.DMA((NDEV - 1,)),
                        pltpu.SemaphoreType.DMA((NDEV - 1,))],
        compiler_params=pltpu.CompilerParams(has_side_effects=True),
    )(*parts)


def _adamw_update(w_ref, g_ref, m_ref, v_ref, d_ref, mo_ref, vo_ref):
    c1 = 1.0 / (1.0 - ADAM_B1 ** ADAM_STEP)
    c2 = 1.0 / (1.0 - ADAM_B2 ** ADAM_STEP)
    gv = g_ref[...]
    mn = ADAM_B1 * m_ref[...] + (1.0 - ADAM_B1) * gv
    vn = ADAM_B2 * v_ref[...] + (1.0 - ADAM_B2) * (gv * gv)
    d_ref[...] = -ADAM_LR * ((mn * c1) / (jnp.sqrt(vn * c2) + ADAM_EPS) + ADAM_WD * w_ref[...])
    mo_ref[...] = mn
    vo_ref[...] = vn


def _adamw_small(ws, gs, ms, vs):
    n = len(ws)

    def body(*refs):
        for i in range(n):
            _adamw_update(*(refs[j * n + i] for j in range(7)))

    vm = pl.BlockSpec(memory_space=pltpu.VMEM)
    outs = pl.pallas_call(
        body, name="adamw_small", in_specs=[vm] * (4 * n), out_specs=[vm] * (3 * n),
        out_shape=[jax.ShapeDtypeStruct(w.shape, F32) for w in ws] * 3,
    )(*ws, *gs, *ms, *vs)
    return outs[:n], outs[n:2 * n], outs[2 * n:]


def _adamw(w, g_parts, m, v, name):
    rows, cols = w.shape
    tr = rows
    while tr * cols * 4 > (1 << 20) and tr % 16 == 0:
        tr //= 2

    def body(w_ref, ga_ref, gb_ref, m_ref, v_ref, g_ref, d_ref, mo_ref, vo_ref):
        g_ref[...] = ga_ref[...] + gb_ref[...]
        _adamw_update(w_ref, g_ref, m_ref, v_ref, d_ref, mo_ref, vo_ref)

    blk = pl.BlockSpec((tr, cols), lambda i: (i, 0))
    return pl.pallas_call(
        body, name=name, grid=(rows // tr,), in_specs=[blk] * 5, out_specs=[blk] * 4,
        out_shape=[jax.ShapeDtypeStruct((rows, cols), F32)] * 4, compiler_params=_params(("parallel",)),
    )(w, *g_parts, m, v)


def _by_chip(g, rr, cc, axis):
    if isinstance(g, tuple):
        n = NCHIP // len(g)
        return jnp.concatenate([h.reshape(rr, n, cc).transpose(1, 0, 2) for h in g], axis=0)
    return g.reshape(NCHIP, rr, cc) if axis == 0 else g.reshape(rr, NCHIP, cc).transpose(1, 0, 2)


_SMALL_REPL = ("norm_mix_w", "ssd_conv_b", "dt_bias", "a_log", "d_skip", "ssd_norm_w", "norm_ffn_w",
               "ffn_conv_b", "final_norm_w")
_SMALL_CONV = (("conv_a_w", 3, D), ("ssd_conv_w", 4, DX), ("ffn_conv_w", 3, FF))


def kernel(x, norm_mix_w, w_in, conv_a_w, w_a_out, ssd_conv_w, ssd_conv_b, dt_bias, a_log, d_skip, ssd_norm_w, w_s_out, w_o, norm_ffn_w, w_up, ffn_conv_w, ffn_conv_b, w_down, final_norm_w, loss_target, m_norm_mix_w, m_w_in, m_conv_a_w, m_w_a_out, m_ssd_conv_w, m_ssd_conv_b, m_dt_bias, m_a_log, m_d_skip, m_ssd_norm_w, m_w_s_out, m_w_o, m_norm_ffn_w, m_w_up, m_ffn_conv_w, m_ffn_conv_b, m_w_down, m_final_norm_w, v_norm_mix_w, v_w_in, v_conv_a_w, v_w_a_out, v_ssd_conv_w, v_ssd_conv_b, v_dt_bias, v_a_log, v_d_skip, v_ssd_norm_w, v_w_s_out, v_w_o, v_norm_ffn_w, v_w_up, v_ffn_conv_w, v_ffn_conv_b, v_w_down, v_final_norm_w):
    names = ("norm_mix_w", "w_in", "conv_a_w", "w_a_out", "ssd_conv_w", "ssd_conv_b", "dt_bias", "a_log", "d_skip",
             "ssd_norm_w", "w_s_out", "w_o", "norm_ffn_w", "w_up", "ffn_conv_w", "ffn_conv_b", "w_down", "final_norm_w")
    W = dict(zip(names, (norm_mix_w, w_in, conv_a_w, w_a_out, ssd_conv_w, ssd_conv_b, dt_bias, a_log, d_skip,
                         ssd_norm_w, w_s_out, w_o, norm_ffn_w, w_up, ffn_conv_w, ffn_conv_b, w_down, final_norm_w)))
    M = dict(zip(names, (m_norm_mix_w, m_w_in, m_conv_a_w, m_w_a_out, m_ssd_conv_w, m_ssd_conv_b, m_dt_bias, m_a_log,
                         m_d_skip, m_ssd_norm_w, m_w_s_out, m_w_o, m_norm_ffn_w, m_w_up, m_ffn_conv_w, m_ffn_conv_b,
                         m_w_down, m_final_norm_w)))
    V = dict(zip(names, (v_norm_mix_w, v_w_in, v_conv_a_w, v_w_a_out, v_ssd_conv_w, v_ssd_conv_b, v_dt_bias, v_a_log,
                         v_d_skip, v_ssd_norm_w, v_w_s_out, v_w_o, v_norm_ffn_w, v_w_up, v_ffn_conv_w, v_ffn_conv_b,
                         v_w_down, v_final_norm_w)))
    two_d = lambda a: a.reshape(-1, a.shape[-1])
    W2, M2, V2 = ({k: two_d(a) for k, a in t.items()} for t in (W, M, V))
    xi, yi, ci = _coords()
    me = 2 * xi + yi

    meidx = me.reshape(1).astype(jnp.int32)
    state = {}


    class Hooks(_Hooks):
        def before_in_proj(self, w_in):
            return _tie(w_in, state["rest_token"], "tie_ag_rest")

        def late_weights(self, wts, after):
            owns, lands = _split_wait("ag_rest_wait", state["rest"], after, _plan_bcast)
            full = {}
            for (n, rr, cc, axis), own, land in zip(_W_REST, owns, lands):
                slabs = lax.dynamic_update_slice(land, own[None], (me, 0, 0))
                full[n] = slabs.reshape(NCHIP * rr, cc) if axis == 0 else slabs.transpose(1, 0, 2).reshape(rr, NCHIP * cc)
            return {**wts, **full}

        def grads_ready(self, grads, tie):
            if "w_in" in grads:
                key, packs = "g_in", [_unpermute_w_in(grads["w_in"])]
            else:
                key = "g_rest"
                packs = [_by_chip(grads[n], rr, cc, axis)
                         for n, rr, cc, axis in _W_REST]
            lands = [lax.empty((_NCOPY,) + p.shape[1:], BF16) for p in packs]
            state[key], token = _split_start("rs_" + key + "_start", packs, lands, _plan_scatter)
            return _tie(tie, token, "tie_" + key)

        def mark(self, name, value):
            state[name] = value

    def reduced(key, after, group):
        packs, lands = _split_wait("rs_" + key + "_wait", state[key], after, _plan_scatter)
        mines = [_add_slabs(p, l, meidx, "rs_add_chips_" + n) for (n, *_), p, l in zip(group, packs, lands)]
        state[key + "_swap"], _ = _split_start(
            "rs_" + key + "_swap_start", mines, [lax.empty(m.shape, F32) for m in mines], _plan_swap)

    def swapped(key, after, group):
        mines, theirs = _split_wait("rs_" + key + "_swap_wait", state[key + "_swap"], after, _plan_swap)
        return dict(zip([n for n, *_ in group], zip(mines, theirs)))

    w_in_slabs = _ag_weights(W2["w_in"].astype(BF16))
    wts = {k: W2[k] for k in _SMALL_REPL}
    conv_by_chip = _gather8([W2[n] for n, *_ in _SMALL_CONV], False, "ag_conv_weights")
    for (n, kk, width), stacked in zip(_SMALL_CONV, conv_by_chip):
        wts[n] = stacked.transpose(1, 0, 2).reshape(kk, width)
    rest = [W2[n].astype(BF16) for n, *_ in _W_REST]
    rest[0] = _tie(rest[0], conv_by_chip[0], "tie_ag_order")
    state["rest"], state["rest_token"] = _split_start(
        "ag_rest_start", rest, [lax.empty((NCHIP,) + r.shape, BF16) for r in rest], _plan_bcast)
    wts["w_in"] = _permute_w_in(w_in_slabs)

    loss8, grad_x, grads = _local_step(x[0], loss_target[0], wts, Hooks())

    reduced("g_rest", state["ssd_bwd"], _W_REST)
    reduced("g_in", grad_x, _W_IN)

    small_parts = [grads[n] for n in _SMALL_REPL] + [loss8[0:1]] + [grads[n] for n, *_ in _SMALL_CONV]
    small_g = _gather8(small_parts, True, "allreduce_small")
    gsm = dict(zip(_SMALL_REPL, small_g[:len(_SMALL_REPL)]))
    loss = small_g[len(_SMALL_REPL)][0, 0]
    for (n, kk, width), gfull in zip(_SMALL_CONV, small_g[len(_SMALL_REPL) + 1:]):
        cw = width // NCHIP
        gsm[n] = lax.dynamic_slice(gfull, (0, me * cw), (kk, cw))

    G, DW, NM, NV = {}, {}, {}, {}
    gbig = swapped("g_rest", grad_x, _W_REST)
    for n in [b[0] for b in _W_REST]:
        G[n], DW[n], NM[n], NV[n] = _adamw(W2[n], gbig[n], M2[n], V2[n], "adamw_" + n)
    gbig = swapped("g_in", DW[_W_REST[-1][0]], _W_IN)
    for n in [b[0] for b in _W_IN]:
        G[n], DW[n], NM[n], NV[n] = _adamw(W2[n], gbig[n], M2[n], V2[n], "adamw_" + n)
    sm_names = list(_SMALL_REPL) + [n for n, *_ in _SMALL_CONV]
    outs = _adamw_small(*([t[n] for n in sm_names] for t in (W2, gsm, M2, V2)))
    for t, vals in zip((DW, NM, NV), outs):
        t.update(zip(sm_names, vals))
    G.update(gsm)

    def shaped(t):
        return [t[n].reshape(W[n].shape) for n in names]

    return (loss, grad_x.reshape(x.shape), *shaped(G), *shaped(DW), *shaped(NM), *shaped(NV))
```

```python
import jax
import jax.numpy as jnp
from jax import lax
from jax.experimental import pallas as pl
from jax.experimental.pallas import tpu as pltpu

F32 = jnp.float32
BF16 = jnp.bfloat16

D = 1024
DI = 2048
NH = 32
HP = 64
NG = 4
NS = 128
CH = 128
DX = 3072
FF = 2816
NI = 10272
EPS = 1e-5

OFF_BCV, OFF_XBC, OFF_G, OFF_Z, OFF_DT = 0, 3072, 6144, 8192, 10240
NIP = 10752
_SEGS = ((0, 2048, OFF_G), (2048, 3072, OFF_BCV), (5120, 2048, OFF_Z), (7168, 3072, OFF_XBC), (10240, 32, OFF_DT))

LANES = 128
HALO = 16
V7X_VMEM_LIMIT = 56 * 2 ** 20

ADAM_LR, ADAM_B1, ADAM_B2, ADAM_EPS, ADAM_WD, ADAM_STEP = 0.001, 0.9, 0.999, 1e-08, 0.01, 10

NN = (((1,), (0,)), ((), ()))
NT = (((1,), (1,)), ((), ()))
TN = (((0,), (0,)), ((), ()))


def _dot(a, b, dims=NN):
    return lax.dot_general(a, b, dims, preferred_element_type=F32)


def _params(sem, **kw):
    return pltpu.CompilerParams(dimension_semantics=sem, vmem_limit_bytes=V7X_VMEM_LIMIT, **kw)


V7X_MXU = 256
V7X_HBM_BYTES_PER_S = 3.5e12
STEP_S = 0.35e-6
MATMUL_VMEM = 40 * 2 ** 20
EPILOGUE_VMEM = 46 * 2 ** 20


ACC_BYTES_PER_S = 1.2e13


def _divisors(dim, cap, units):
    for unit in units:
        c = [t for t in range(unit, min(dim, cap) + 1, unit) if dim % t == 0]
        if c:
            return c
    return [dim]


def _tiles(M, N, K, out_bytes, has_res):
    best = None
    for tn in _divisors(N, 2816, (V7X_MXU, LANES)):
        for tm in _divisors(M, 2816, (LANES,)):
            for tk in _divisors(K, 2816, (V7X_MXU, LANES)):
                nk, ni, nj = K // tk, M // tm, N // tn
                vmem = 4 * (tm * tk + tk * tn) + 2 * tm * tn * out_bytes
                vmem += (4 * tm * tn if nk > 1 else 0) + (8 * tm * tn if has_res else 0)
                if vmem > MATMUL_VMEM:
                    continue
                a_reads = M * K * 2 * (nj if nk > 1 else 1)
                b_reads = K * N * 2 * (ni if nk * nj > 1 else 1)
                cost = (a_reads + b_reads + M * N * out_bytes) / V7X_HBM_BYTES_PER_S + ni * nj * nk * STEP_S
                cost += (nk - 1) * M * N * 8 / ACC_BYTES_PER_S
                if best is None or cost < best[0]:
                    best = (cost, tm, tn, tk)
    assert best is not None, (M, N, K)
    return best[1:]


def _sigmoid(x):
    return 1.0 / (1.0 + jnp.exp(-x))


class _Epilogue:
    def __init__(self, fn, ins, outs, tile_bytes, in_windows=None, out_windows=None):
        self.fn, self.ins, self.outs, self.tile_bytes = fn, tuple(ins), tuple(outs), tile_bytes
        self.in_windows, self.out_windows = in_windows or {}, out_windows or {}


def _matmul(a, b, *, mode, out_dtype, name, residual=None, b_k_off=0, epilogue=None):
    if mode == "nn":
        (M, K), (K2, N) = a.shape, b.shape
    elif mode == "nt":
        (M, K), (N, K2) = a.shape, (b.shape[0], a.shape[1])
        assert b_k_off + K <= b.shape[1]
    else:
        (K, M), (K2, N) = a.shape, b.shape
    assert K == K2, (name, a.shape, b.shape)
    tm, tn, tk = _tiles(M, N, K, jnp.dtype(out_dtype).itemsize, residual is not None)
    if epilogue is not None:
        tn = N
        fits = [(K * N * 2 * (M // t) / V7X_HBM_BYTES_PER_S + (K // q - 1) * M * N * 8 / ACC_BYTES_PER_S
                 + (M // t) * (K // q) * STEP_S, t, q)
                for t in (1024, 512, 256) if M % t == 0 for q in _divisors(K, 2816, (V7X_MXU, LANES))
                if 4 * (t * q + q * tn) + (4 * t * tn if K > q else 0) + (8 * t * tn if residual is not None else 0)
                + 2 * t * epilogue.tile_bytes <= EPILOGUE_VMEM]
        _, tm, tk = min(fits)
    nk = K // tk
    if mode == "tn":
        a_spec = pl.BlockSpec((tk, tm), lambda i, j, k: (k, i))
    else:
        a_spec = pl.BlockSpec((tm, tk), lambda i, j, k: (i, k))
    if mode == "nt":
        assert b_k_off % tk == 0
        b_spec = pl.BlockSpec((tn, tk), lambda i, j, k: (j, k + b_k_off // tk))
    else:
        b_spec = pl.BlockSpec((tk, tn), lambda i, j, k: (k, j))
    dims = {"nn": NN, "nt": NT, "tn": TN}[mode]
    o_spec = pl.BlockSpec((tm, tn), lambda i, j, k: (i, j))
    has_res = residual is not None

    def rows_or_whole(shape, window=None):
        if window is not None:
            off, width = window
            return pl.BlockSpec((tm, width), lambda i, j, k: (i, off // width))
        if shape[0] == M:
            return pl.BlockSpec((tm,) + tuple(shape[1:]), lambda i, j, k: (i,) + (0,) * (len(shape) - 1))
        return pl.BlockSpec(tuple(shape), lambda i, j, k: (0,) * len(shape))

    n_in = 2 + has_res + (len(epilogue.ins) if epilogue else 0)
    n_out = len(epilogue.outs) if epilogue else 1

    def body(*refs):
        a_ref, b_ref = refs[:2]
        r_ref = refs[2] if has_res else None
        out_refs = refs[n_in:n_in + n_out]
        acc_ref = refs[-1]
        k = pl.program_id(2)
        part = _dot(a_ref[...], b_ref[...], dims)

        def finish(r):
            if has_res:
                r = r + r_ref[...].astype(F32)
            if epilogue is None:
                out_refs[0][...] = r.astype(out_dtype)
            else:
                epilogue.fn(r, refs[2 + has_res:n_in], out_refs, pl.program_id(0) == 0)

        if nk == 1:
            finish(part)
            return

        @pl.when(k == 0)
        def _():
            acc_ref[...] = part

        @pl.when(jnp.logical_and(k > 0, k < nk - 1))
        def _():
            acc_ref[...] += part

        @pl.when(k == nk - 1)
        def _():
            finish(acc_ref[...] + part)

    in_specs = [a_spec, b_spec] + ([o_spec] if has_res else [])
    args = (a, b) + ((residual,) if has_res else ())
    if epilogue is None:
        out_specs, out_shape = o_spec, jax.ShapeDtypeStruct((M, N), out_dtype)
        sem = ("parallel", "parallel", "arbitrary")
    else:
        in_specs += [rows_or_whole(x.shape, epilogue.in_windows.get(n)) for n, x in enumerate(epilogue.ins)]
        args += epilogue.ins
        out_specs = [rows_or_whole(o[0], epilogue.out_windows.get(n)) for n, o in enumerate(epilogue.outs)]
        out_shape = [jax.ShapeDtypeStruct(shp, dt) for shp, dt in epilogue.outs]
        sem = ("arbitrary", "arbitrary", "arbitrary")
    return pl.pallas_call(
        body, name=name, grid=(M // tm, N // tn, nk), in_specs=in_specs, out_specs=out_specs,
        out_shape=out_shape, scratch_shapes=[pltpu.VMEM((tm, tn), F32)] if nk > 1 else [],
        compiler_params=_params(sem),
    )(*args)


class _Rows:
    def __init__(self, T, tm):
        self.T, self.tm = T, min(tm, T // 2)
        self.nrow = T // self.tm
        self.r = self.tm // HALO
        self.nb = T // HALO

    def tile(self, w, cb=0, step=1):
        return pl.BlockSpec((self.tm, w), lambda j, i: (i, cb + step * j))

    def prev(self, w, cb=0, step=1):
        r = self.r
        return pl.BlockSpec((HALO, w), lambda j, i: (jnp.maximum(i * r - 1, 0), cb + step * j))

    def next(self, w, cb=0, step=1):
        r, nb = self.r, self.nb
        return pl.BlockSpec((HALO, w), lambda j, i: (jnp.minimum((i + 1) * r, nb - 1), cb + step * j))

    def colvec(self, k, w, cb=0, step=1):
        return pl.BlockSpec((k, w), lambda j, i: (0, cb + step * j))

    def call(self, body, name, ncol, in_specs, out_specs, out_shape, args, aliases=None):
        return pl.pallas_call(
            body, name=name, grid=(ncol, self.nrow), in_specs=in_specs, out_specs=out_specs,
            out_shape=out_shape, input_output_aliases=aliases or {},
            compiler_params=_params(("parallel", "arbitrary")),
        )(*args)


ANY = pl.BlockSpec(memory_space=pl.ANY)


def _shifts_causal(ext, nk, tm):
    out = []
    for k in range(nk):
        s = nk - 1 - k
        r = ext if s == 0 else pltpu.roll(ext, s, 0)
        out.append(r[HALO:])
    return out


def _shifts_anticausal(ext, nk, tm):
    n = ext.shape[0]
    out = []
    for k in range(nk):
        s = nk - 1 - k
        r = ext if s == 0 else pltpu.roll(ext, n - s, 0)
        out.append(r[:tm])
    return out


def _wsum(w, parts):
    acc = w[0:1, :] * parts[0]
    for k in range(1, len(parts)):
        acc = acc + w[k:k + 1, :] * parts[k]
    return acc


def _colsum(x):
    return jnp.sum(x, axis=0, keepdims=True)


def _acc_out(ref, val, first):
    @pl.when(first)
    def _():
        ref[...] = val

    @pl.when(jnp.logical_not(first))
    def _():
        ref[...] += val


def _acc_rows(ref, rows, first):
    for k, r in enumerate(rows):
        _acc_out(ref.at[k:k + 1, :], r, first)


def _norm_matmul(x, wn, b, name, b_f32=None):
    T, N = x.shape[0], b.shape[1]
    tm = min(1024, T)
    tn = max(t for t in _divisors(N, 2816, (V7X_MXU, LANES))
             if 8 * tm * D + 6 * tm * D + 4 * D * t + 4 * tm * t <= MATMUL_VMEM)

    extra = b_f32 is not None

    def body(*refs):
        x_ref, wn_ref, b_ref = refs[:3]
        o_ref, u_ref = refs[3 + extra:5 + extra]
        keep_ref = refs[-1]

        @pl.when(pl.program_id(1) == 0)
        def _():
            xv = x_ref[...]
            r = lax.rsqrt(jnp.mean(xv * xv, axis=-1, keepdims=True) + EPS)
            u = (xv * r * wn_ref[...]).astype(BF16)
            keep_ref[...] = u
            u_ref[...] = u
            if extra:
                refs[5 + extra][...] = _dot(u, refs[3][...])

        o_ref[...] = _dot(keep_ref[...], b_ref[...]).astype(BF16)

    rows = pl.BlockSpec((tm, D), lambda i, j: (i, 0))
    whole = lambda shape: pl.BlockSpec(shape, lambda i, j: (0, 0))
    narrow = pl.BlockSpec((tm, LANES), lambda i, j: (i, 0))
    return pl.pallas_call(
        body, name=name, grid=(T // tm, N // tn),
        in_specs=[rows, whole((1, D)), pl.BlockSpec((D, tn), lambda i, j: (0, j))] + [whole((D, LANES))] * extra,
        out_specs=[pl.BlockSpec((tm, tn), lambda i, j: (i, j)), rows] + [narrow] * extra,
        out_shape=[jax.ShapeDtypeStruct((T, N), BF16), jax.ShapeDtypeStruct((T, D), BF16)]
        + [jax.ShapeDtypeStruct((T, LANES), F32)] * extra,
        scratch_shapes=[pltpu.VMEM((tm, D), BF16)],
        compiler_params=_params(("parallel", "arbitrary")),
    )(*((x, wn, b) + ((b_f32,) if extra else ())))


def _rmsnorm_bwd_epilogue(x, w, dres):
    T = x.shape[0]

    def fn(dyv, ins, outs, first):
        x_ref, w_ref, dr_ref = ins
        dx_ref, dxb_ref, dw_ref = outs
        xv = x_ref[...]
        r = lax.rsqrt(jnp.mean(xv * xv, axis=-1, keepdims=True) + EPS)
        xh = xv * r
        dxh = dyv * w_ref[...]
        dx = r * (dxh - xh * jnp.mean(dxh * xh, axis=-1, keepdims=True)) + dr_ref[...]
        dx_ref[...] = dx
        dxb_ref[...] = dx.astype(BF16)
        _acc_out(dw_ref, _colsum(dyv * xh), first)

    return _Epilogue(fn, (x, w, dres), (((T, D), F32), ((T, D), BF16), ((1, D), F32)), 14 * D)


def _branch_a_fwd(proj, conv_w):
    T = proj.shape[0]
    R = _Rows(T, 512)
    tm = R.tm

    def body(p_ref, pp_ref, w_ref, o_ref):
        keep = (pl.program_id(1) > 0).astype(F32)
        cv = p_ref[:, D:2 * D].astype(F32) * p_ref[:, 2 * D:].astype(F32)
        cvp = pp_ref[:, D:2 * D].astype(F32) * pp_ref[:, 2 * D:].astype(F32) * keep
        sh = _shifts_causal(jnp.concatenate([cvp, cv], axis=0), 3, tm)
        ca = _wsum(w_ref[...], sh)
        o_ref[...] = (p_ref[:, :D].astype(F32) * ca).astype(BF16)

    return R.call(body, "branch_a_fwd", 1, [R.tile(3 * D), R.prev(3 * D), R.colvec(3, D)], R.tile(D),
                  jax.ShapeDtypeStruct((T, D), BF16), (proj, proj, conv_w))


def _branch_a_bwd(dya_in, proj, conv_w, dproj):
    T = proj.shape[0]
    R = _Rows(T, 256)
    tm = R.tm

    def body(d_ref, dn_ref, p_ref, pp_ref, pn_ref, w_ref, _alias, o_ref, dw_ref):
        i = pl.program_id(1)
        keep_p = (i > 0).astype(F32)
        keep_n = (i < R.nrow - 1).astype(F32)
        w = w_ref[...]
        b = p_ref[:, :D].astype(F32)
        c = p_ref[:, D:2 * D].astype(F32)
        v = p_ref[:, 2 * D:].astype(F32)
        cvp = pp_ref[:, D:2 * D].astype(F32) * pp_ref[:, 2 * D:].astype(F32) * keep_p
        sh = _shifts_causal(jnp.concatenate([cvp, c * v], axis=0), 3, tm)
        ca = _wsum(w, sh)
        d = d_ref[...].astype(F32)
        dca = d * b
        dca_n = dn_ref[...].astype(F32) * pn_ref[:, :D].astype(F32) * keep_n
        dsh = _shifts_anticausal(jnp.concatenate([dca, dca_n], axis=0), 3, tm)
        dcv = _wsum(w, dsh)
        o_ref[:, :D] = (d * ca).astype(BF16)
        o_ref[:, D:2 * D] = (dcv * v).astype(BF16)
        o_ref[:, 2 * D:] = (dcv * c).astype(BF16)
        _acc_rows(dw_ref, [_colsum(dca * s) for s in sh], i == 0)

    return R.call(
        body, "branch_a_bwd", 1,
        [R.tile(D), R.next(D), R.tile(3 * D), R.prev(3 * D), R.next(3 * D), R.colvec(3, D), ANY],
        [R.tile(3 * D), R.colvec(3, D)],
        [jax.ShapeDtypeStruct(dproj.shape, BF16), jax.ShapeDtypeStruct((3, D), F32)],
        (dya_in, dya_in, proj, proj, proj, conv_w, dproj), aliases={6: 0})


_XW = 512


def _xbc_fwd(proj, conv_w, conv_b):
    T = proj.shape[0]
    R = _Rows(T, 512)
    tm = R.tm
    cb = OFF_XBC // _XW

    def body(x_ref, xp_ref, w_ref, b_ref, o_ref):
        keep = (pl.program_id(1) > 0).astype(F32)
        ext = jnp.concatenate([xp_ref[...].astype(F32) * keep, x_ref[...].astype(F32)], axis=0)
        pre = _wsum(w_ref[...], _shifts_causal(ext, 4, tm)) + b_ref[...]
        o_ref[...] = (pre * _sigmoid(pre)).astype(BF16)

    return R.call(body, "xbc_fwd", DX // _XW,
                  [R.tile(_XW, cb), R.prev(_XW, cb), R.colvec(4, _XW), R.colvec(1, _XW)], R.tile(_XW),
                  jax.ShapeDtypeStruct((T, DX), BF16), (proj, proj, conv_w, conv_b))


def _xbc_bwd(dact, proj, conv_w, conv_b, dproj):
    T = proj.shape[0]
    R = _Rows(T, 512)
    tm = R.tm
    cb = OFF_XBC // _XW

    def body(d_ref, dn_ref, x_ref, xp_ref, xn_ref, w_ref, b_ref, _alias, o_ref, dw_ref, db_ref):
        i = pl.program_id(1)
        keep_p = (i > 0).astype(F32)
        keep_n = (i < R.nrow - 1).astype(F32)
        w = w_ref[...]
        ext = jnp.concatenate([xp_ref[...].astype(F32) * keep_p, x_ref[...].astype(F32),
                               xn_ref[...].astype(F32)], axis=0)
        sh = _shifts_causal(ext, 4, tm + HALO)
        pre = _wsum(w, sh) + b_ref[...]
        s = _sigmoid(pre)
        dsilu = s * (1.0 + pre * (1.0 - s))
        dext = jnp.concatenate([d_ref[...].astype(F32), dn_ref[...].astype(F32) * keep_n], axis=0)
        dpre = dext * dsilu
        dsh = _shifts_anticausal(dpre, 4, tm)
        o_ref[...] = _wsum(w, dsh).astype(BF16)
        dp = dpre[:tm]
        _acc_rows(dw_ref, [_colsum(dp * q[:tm]) for q in sh], i == 0)
        _acc_out(db_ref, _colsum(dp), i == 0)

    return R.call(
        body, "xbc_bwd", DX // _XW,
        [R.tile(_XW), R.next(_XW), R.tile(_XW, cb), R.prev(_XW, cb), R.next(_XW, cb),
         R.colvec(4, _XW), R.colvec(1, _XW), ANY],
        [R.tile(_XW, cb), R.colvec(4, _XW), R.colvec(1, _XW)],
        [jax.ShapeDtypeStruct(dproj.shape, BF16), jax.ShapeDtypeStruct((4, DX), F32),
         jax.ShapeDtypeStruct((1, DX), F32)],
        (dact, dact, proj, proj, proj, conv_w, conv_b, dproj), aliases={7: 0})


def _softplus(x):
    return jnp.maximum(x, 0.0) + jnp.log(1.0 + jnp.exp(-jnp.abs(x)))


def _dt_rows(T):
    return min(8 * CH, T // 2)


def _dt_fwd(dt_raw, dt_bias_p, a_log_p):
    T = dt_raw.shape[0]
    rows = _dt_rows(T)

    def body(r_ref, b_ref, al_ref, dt_ref, ac_ref, acT_ref):
        dt = _softplus(r_ref[...] + b_ref[...])
        s = dt * (-jnp.exp(al_ref[...]))
        row = lax.broadcasted_iota(jnp.int32, (rows, LANES), 0) % CH
        k = 1
        while k < CH:
            s = s + jnp.where(row >= k, pltpu.roll(s, k, 0), 0.0)
            k *= 2
        dt_ref[...] = dt
        ac_ref[...] = s
        for q in range(0, rows, CH):
            acT_ref[q:q + CH] = s[q:q + CH].T

    blk = pl.BlockSpec((rows, LANES), lambda i: (i, 0))
    vec = pl.BlockSpec((1, LANES), lambda i: (0, 0))
    return pl.pallas_call(
        body, name="dt_fwd", grid=(T // rows,), in_specs=[blk, vec, vec], out_specs=[blk, blk, blk],
        out_shape=[jax.ShapeDtypeStruct((T, LANES), F32)] * 3, compiler_params=_params(("parallel",)),
    )(dt_raw, dt_bias_p, a_log_p)


def _dt_bwd(dacum, ddt_x, dt_raw, dt_bias_p, a_log_p, dproj):
    T = dt_raw.shape[0]
    rows = _dt_rows(T)
    nc = T // rows

    def body(da_ref, dx_ref, r_ref, b_ref, al_ref, _alias, o_ref, db_ref, dal_ref):
        i = pl.program_id(0)
        a = -jnp.exp(al_ref[...])
        z = r_ref[...] + b_ref[...]
        dt = _softplus(z)
        s = da_ref[...]
        row = lax.broadcasted_iota(jnp.int32, (rows, LANES), 0) % CH
        k = 1
        while k < CH:
            s = s + jnp.where(row < CH - k, pltpu.roll(s, rows - k, 0), 0.0)
            k *= 2
        ddt = s * a + dx_ref[...]
        draw = ddt * _sigmoid(z)
        o_ref[:, :LANES] = draw.astype(BF16)
        o_ref[:, LANES:] = jnp.zeros((rows, NIP - OFF_DT - LANES), BF16)
        _acc_out(db_ref, _colsum(draw), i == 0)
        _acc_out(dal_ref, _colsum(s * dt), i == 0)

        @pl.when(i == nc - 1)
        def _():
            dal_ref[...] = dal_ref[...] * a

    blk = pl.BlockSpec((rows, LANES), lambda i: (i, 0))
    vec = pl.BlockSpec((1, LANES), lambda i: (0, 0))
    oblk = pl.BlockSpec((rows, NIP - OFF_DT), lambda i: (i, OFF_DT // (NIP - OFF_DT)))
    return pl.pallas_call(
        body, name="dt_bwd", grid=(nc,), in_specs=[blk, blk, blk, vec, vec, ANY], out_specs=[oblk, vec, vec],
        out_shape=[jax.ShapeDtypeStruct(dproj.shape, BF16), jax.ShapeDtypeStruct((1, LANES), F32),
                   jax.ShapeDtypeStruct((1, LANES), F32)],
        input_output_aliases={5: 0}, compiler_params=_params(("arbitrary",)),
    )(dacum, ddt_x, dt_raw, dt_bias_p, a_log_p, dproj)


_GW = DI // NG
_HG = NH // NG
_NEG = -1e30


def _interleave(gens):
    out, live = [None] * len(gens), list(range(len(gens)))
    while live:
        for i in list(live):
            try:
                next(gens[i])
            except StopIteration as stop:
                out[i] = stop.value
                live.remove(i)
    return out


def _pair_lanes(left, v0, v1):
    return jnp.where(left, v0, v1)


def _hi_lo(v):
    hi = v.astype(BF16)
    return jnp.concatenate([hi, (v - hi.astype(F32)).astype(BF16)], axis=1)


def _head_spread():
    row = lax.broadcasted_iota(jnp.int32, (2 * LANES, 1), 0) % LANES
    return (row == lax.broadcasted_iota(jnp.int32, (1, DI), 1) // HP).astype(BF16)


def _ssd_specs(T, rev):
    nc = T // CH
    cm = (lambda c: nc - 1 - c) if rev else (lambda c: c)
    bw = NG * NS
    return dict(
        xs=pl.BlockSpec((CH, DI), lambda c: (cm(c), 0)),
        bm=pl.BlockSpec((CH, bw), lambda c: (cm(c), DI // bw)),
        cmat=pl.BlockSpec((CH, bw), lambda c: (cm(c), DI // bw + 1)),
        xbc=pl.BlockSpec((CH, DX), lambda c: (cm(c), 0)),
        col=pl.BlockSpec((CH, LANES), lambda c: (cm(c), 0)),
        dsk=pl.BlockSpec((1, DI), lambda c: (0, 0)),
        state=pl.BlockSpec((1, NS, DI), lambda c: (cm(c), 0, 0)),
    )


def _last(ref, lo, hi):
    return ref.at[(slice(None),) * (len(ref.shape) - 1) + (slice(lo, hi),)]


def _group_views(g, wide, narrow):
    return [_last(r, g * _GW, (g + 1) * _GW) for r in wide] + [_last(r, g * NS, (g + 1) * NS) for r in narrow]


def _ssd_fwd(xact, dt, acum, acumT, dsk_rep, proj, norm_w):
    T = xact.shape[0]
    nc = T // CH
    sp = _ssd_specs(T, False)

    def body(*refs):
        xs, bm, cmat, dtr, acr, actr, dsk, zr, nw, spread_ref, y, yn, spv, S_ref = refs

        @pl.when(pl.program_id(0) == 0)
        def _():
            S_ref[...] = jnp.zeros_like(S_ref)

        ac = acr[...]
        cols = [_hi_lo(v) for v in (dtr[...], jnp.exp(ac), jnp.exp(ac[CH - 1:CH, :] - ac))]
        _interleave([group(g * _HG, cols, ac, actr[...], _last(spread_ref, g * _GW, (g + 1) * _GW),
                           *_group_views(g, (xs, dsk, zr, nw, y, yn, spv, S_ref), (bm, cmat))) for g in range(NG)])

    def group(hb, cols, ac, acT, spread_ref, xs_ref, dsk_ref, z_ref, nw_ref, y_ref, yn_ref, sp_ref, S_ref, b_ref, c_ref):
        dtl, eal, dtel = (_dot(v, spread_ref[...]) for v in cols)
        Bm, Cm = b_ref[...], c_ref[...]
        S = S_ref[...]
        sp_ref[0] = S
        cb = _dot(Cm, Bm, NT)
        CS = _dot(Cm, S.astype(BF16))
        row = lax.broadcasted_iota(jnp.int32, (CH, CH), 0)
        col = lax.broadcasted_iota(jnp.int32, (CH, CH), 1)
        tril = row >= col
        left = col < HP
        xd_parts = []
        for p in range(_HG // 2):
            sl = slice(p * LANES, (p + 1) * LANES)
            j0, j1 = hb + 2 * p, hb + 2 * p + 1
            xp = xs_ref[:, sl].astype(F32)
            a0, a1 = ac[:, j0:j0 + 1], ac[:, j1:j1 + 1]
            X = xp * dtl[:, sl]
            Xb = X.astype(BF16)
            Ws = [(cb * jnp.exp(jnp.where(tril, aj - acT[j:j + 1, :], _NEG))).astype(BF16)
                  for j, aj in ((j0, a0), (j1, a1))]
            Xs = [jnp.where(m, Xb, jnp.zeros_like(Xb)) for m in (left, jnp.logical_not(left))]
            yield
            yd = _dot(jnp.concatenate(Ws, axis=1), jnp.concatenate(Xs, axis=0))
            yield
            y = yd + eal[:, sl] * CS[:, sl] + dsk_ref[:, sl] * xp
            y_ref[:, sl] = y.astype(BF16)
            xd_parts.append(X * dtel[:, sl])
        Xd = jnp.concatenate(xd_parts, axis=1).astype(BF16)
        S_ref[...] = eal[CH - 1:CH, :] * S + _dot(Bm, Xd, TN)
        yield
        z = z_ref[...].astype(F32)
        yf = y_ref[...].astype(F32) * z * _sigmoid(z)
        r = lax.rsqrt(jnp.mean(yf * yf, axis=-1, keepdims=True) + EPS)
        yn_ref[...] = (yf * r * nw_ref[...]).astype(BF16)

    zspec = pl.BlockSpec((CH, DI), lambda c: (c, OFF_Z // DI))
    return pl.pallas_call(
        body, name="ssd_fwd", grid=(nc,),
        in_specs=[sp["xs"], sp["bm"], sp["cmat"], sp["col"], sp["col"], sp["col"], sp["dsk"], zspec, sp["dsk"],
                  pl.BlockSpec((2 * LANES, DI), lambda c: (0, 0))],
        out_specs=[sp["xs"], sp["xs"], sp["state"]],
        out_shape=[jax.ShapeDtypeStruct((T, DI), BF16), jax.ShapeDtypeStruct((T, DI), BF16),
                   jax.ShapeDtypeStruct((nc, NS, DI), F32)],
        scratch_shapes=[pltpu.VMEM((NS, DI), F32)],
        compiler_params=_params(("arbitrary",)),
    )(xact, xact, xact, dt, acum, acumT, dsk_rep, proj, norm_w, _head_spread())


def _ssd_bwd(dn, y, proj, norm_w, dproj, xact, dt, acum, acumT, dsk_rep, sprev):
    T = xact.shape[0]
    nc = T // CH
    sp = _ssd_specs(T, True)

    def body(*refs):
        (xs, bm, cmat, dtr, acr, actr, dsk, dnr, yr, zr, nw, spv, _alias, lanes_of_ref, rows_of_ref, spread_ref,
         dxa, ddtx, dAc, dskacc, dzr, dnw, dS_ref) = refs
        first = pl.program_id(0) == 0

        @pl.when(first)
        def _():
            dS_ref[...] = jnp.zeros_like(dS_ref)

        dbc = _last(dxa, DI, DX)
        ddtx_sum = jnp.zeros((CH, LANES), F32)
        dAc_sum = jnp.zeros((CH, LANES), F32)
        ac = acr[...]
        cols = [_hi_lo(v) for v in (dtr[...], jnp.exp(ac), jnp.exp(ac[CH - 1:CH, :] - ac))]
        for a, b in _interleave([group(first, g * _HG, cols, ac, actr[...],
                                       lanes_of_ref.at[g * _GW:(g + 1) * _GW],
                                       rows_of_ref.at[g * _HG * CH:(g + 1) * _HG * CH],
                                       _last(spread_ref, g * _GW, (g + 1) * _GW),
                                       *_group_views(g, (xs, dsk, dnr, yr, zr, nw, dzr, dnw, spv, dxa, dskacc, dS_ref),
                                                     (bm, cmat, dbc, _last(dbc, NG * NS, 2 * NG * NS))))
                                 for g in range(NG)]):
            ddtx_sum, dAc_sum = ddtx_sum + a, dAc_sum + b
        ddtx[...] = ddtx_sum
        dAc[...] = dAc_sum

    def group(first, hb, cols, ac, acT, lanes_of_ref, rows_of_ref, spread_ref, xs_ref, dsk_ref, dn_ref, y_ref, z_ref,
              nw_ref, dz_ref, dnw_ref, sp_ref, dx_ref, dskacc_ref, dS_ref, b_ref, c_ref, dB_ref, dC_ref):
        z = z_ref[...].astype(F32)
        yv = y_ref[...].astype(F32)
        sg = _sigmoid(z)
        silu = z * sg
        yf = yv * silu
        rn = lax.rsqrt(jnp.mean(yf * yf, axis=-1, keepdims=True) + EPS)
        yh = yf * rn
        dnv = dn_ref[...].astype(F32)
        dyh = dnv * nw_ref[...]
        dyf = rn * (dyh - yh * jnp.mean(dyh * yh, axis=-1, keepdims=True))
        dyg = dyf * silu
        dz_ref[...] = (dyf * yv * sg * (1.0 + z * (1.0 - sg))).astype(BF16)
        _acc_out(dnw_ref, _colsum(dnv * yh), first)
        Bm, Cm = b_ref[...], c_ref[...]
        S = sp_ref[0]
        dS = dS_ref[...]
        Sb, dSb = S.astype(BF16), dS.astype(BF16)
        cb = _dot(Cm, Bm, NT)
        cbT = _dot(Bm, Cm, NT)
        CmT = Cm.T
        CS = _dot(Cm, Sb)
        T1 = _dot(Bm, dSb)
        yield
        row = lax.broadcasted_iota(jnp.int32, (CH, CH), 0)
        col = lax.broadcasted_iota(jnp.int32, (CH, CH), 1)
        tril = row >= col
        triu = row <= col
        left = col < HP
        lastrow = lax.broadcasted_iota(jnp.int32, (CH, 1), 0) == CH - 1
        dCB = jnp.zeros((CH, CH), F32)
        dCBT = jnp.zeros((CH, CH), F32)
        xd_parts, dye_parts, dsk_parts, dxx_parts, gr_parts, end_parts, qd_parts = ([] for _ in range(7))
        dtls, eals, dtels = (_dot(v, spread_ref[...]) for v in cols)
        for p in range(_HG // 2):
            sl = slice(p * LANES, (p + 1) * LANES)
            j0, j1 = hb + 2 * p, hb + 2 * p + 1
            xp = xs_ref[:, sl].astype(F32)
            dyp = dyg[:, sl]
            a0, a1 = ac[:, j0:j0 + 1], ac[:, j1:j1 + 1]
            dtl, eal, dtel = dtls[:, sl], eals[:, sl], dtels[:, sl]
            X = xp * dtl
            Xb = X.astype(BF16)
            T1d = T1[:, sl] * dtel
            Rm = T1d * X
            decp = eal[CH - 1:CH, :]
            gr_parts.append(dyp * (eal * CS[:, sl]) - Rm)
            end_parts.append(Rm + decp * (dS[:, sl] * S[:, sl]))
            dXd = jnp.zeros((CH, LANES), F32)
            for j, aj, mask in ((j0, a0, left), (j1, a1, jnp.logical_not(left))):
                dYm = jnp.where(mask, dyp, 0.0).astype(BF16)
                dWm = _dot(dYm, Xb, NT)
                dWmT = _dot(Xb, dYm, NT)
                yield
                e = aj - acT[j:j + 1, :]
                P = dWm * jnp.exp(jnp.where(tril, e, _NEG))
                LmT = jnp.exp(jnp.where(triu, -e, _NEG))
                PT = dWmT * LmT
                dCB = dCB + P
                dCBT = dCBT + PT
                yield
                dXd = dXd + _dot((cbT * LmT).astype(BF16), dYm)
                qd_parts.append((P * cb - PT * cbT).astype(BF16))
                yield
            dX = dXd + T1d
            dxx_parts.append(dX * xp)
            dx_ref[:, sl] = (dX * dtl + dsk_ref[:, sl] * dyp).astype(BF16)
            dsk_parts.append(_colsum(dyp * xp))
            xd_parts.append(X * dtel)
            dye_parts.append(dyp * eal)
            yield
        Xd = jnp.concatenate(xd_parts, axis=1).astype(BF16)
        dYe = jnp.concatenate(dye_parts, axis=1).astype(BF16)
        def lane_sums(parts):
            return _dot(jnp.concatenate(parts, axis=1).astype(BF16), lanes_of_ref[...])
        ddtx = lane_sums(dxx_parts)
        dAc = (_dot(jnp.concatenate(qd_parts, axis=1), rows_of_ref[...]) + lane_sums(gr_parts)
               + jnp.where(lastrow, _colsum(lane_sums(end_parts)), 0.0))
        dC_ref[...] = (_dot(dCB.astype(BF16), Bm) + _dot(dYe, Sb, NT)).astype(BF16)
        dB_ref[...] = (_dot(dCBT.astype(BF16), Cm) + _dot(Xd, dSb, NT)).astype(BF16)
        dS_ref[...] = _dot(CmT, dYe) + eals[CH - 1:CH, :] * dS
        _acc_out(dskacc_ref, jnp.concatenate(dsk_parts, axis=1), first)
        return ddtx, dAc

    zspec = pl.BlockSpec((CH, DI), lambda c: (nc - 1 - c, OFF_Z // DI))
    head = lax.broadcasted_iota(jnp.int32, (1, LANES), 1)
    lanes_of = (lax.broadcasted_iota(jnp.int32, (DI, 1), 0) // HP == head).astype(BF16)
    rows_of = (lax.broadcasted_iota(jnp.int32, (NH * CH, 1), 0) // CH == head).astype(BF16)
    return pl.pallas_call(
        body, name="ssd_bwd", grid=(nc,),
        in_specs=[sp["xs"], sp["bm"], sp["cmat"], sp["col"], sp["col"], sp["col"], sp["dsk"], sp["xs"], sp["xs"],
                  zspec, sp["dsk"], sp["state"], ANY, pl.BlockSpec(lanes_of.shape, lambda c: (0, 0)),
                  pl.BlockSpec(rows_of.shape, lambda c: (0, 0)), pl.BlockSpec((2 * LANES, DI), lambda c: (0, 0))],
        out_specs=[sp["xbc"], sp["col"], sp["col"], sp["dsk"], zspec, sp["dsk"]],
        out_shape=[jax.ShapeDtypeStruct((T, DX), BF16), jax.ShapeDtypeStruct((T, LANES), F32),
                   jax.ShapeDtypeStruct((T, LANES), F32), jax.ShapeDtypeStruct((1, DI), F32),
                   jax.ShapeDtypeStruct(dproj.shape, BF16), jax.ShapeDtypeStruct((1, DI), F32)],
        scratch_shapes=[pltpu.VMEM((NS, DI), F32)], input_output_aliases={12: 4},
        compiler_params=_params(("arbitrary",)),
    )(xact, xact, xact, dt, acum, acumT, dsk_rep, dn, y, proj, norm_w, sprev, dproj, lanes_of, rows_of, _head_spread())


def _merge_fwd_epilogue(proj, ya):
    T = proj.shape[0]

    def fn(ysv, ins, outs, first):
        g_ref, ya_ref = ins
        m_ref, ys_ref = outs
        ga = _sigmoid(g_ref[:, :D].astype(F32))
        gs = _sigmoid(g_ref[:, D:].astype(F32))
        m_ref[...] = (ga * ya_ref[...].astype(F32) + gs * ysv).astype(BF16)
        ys_ref[...] = ysv.astype(BF16)

    return _Epilogue(fn, (proj, ya), (((T, D), BF16), ((T, D), BF16)), 10 * D, in_windows={0: (OFF_G, 2 * D)})


def _merge_bwd_epilogue(proj, ya, ys, ncols):
    T = proj.shape[0]

    def fn(d, ins, outs, first):
        g_ref, ya_ref, ys_ref = ins
        dg_ref, dya_ref, dys_ref = outs
        ga = _sigmoid(g_ref[:, :D].astype(F32))
        gs = _sigmoid(g_ref[:, D:].astype(F32))
        dya_ref[...] = (d * ga).astype(BF16)
        dys_ref[...] = (d * gs).astype(BF16)
        dg_ref[:, :D] = (d * ya_ref[...].astype(F32) * ga * (1.0 - ga)).astype(BF16)
        dg_ref[:, D:] = (d * ys_ref[...].astype(F32) * gs * (1.0 - gs)).astype(BF16)

    window = (OFF_G, 2 * D)
    return _Epilogue(fn, (proj, ya, ys), (((T, ncols), BF16), ((T, D), BF16), ((T, D), BF16)), 16 * D,
                     in_windows={0: window}, out_windows={0: window})


_FW = 1408
_FB = FF // _FW


def _ffn_act_fwd(hv, conv_w, conv_b):
    T = hv.shape[0]
    R = _Rows(T, 256)
    tm = R.tm

    def body(h1_ref, h1p_ref, h3_ref, w_ref, b_ref, o_ref):
        keep = (pl.program_id(1) > 0).astype(F32)
        ext = jnp.concatenate([h1p_ref[...].astype(F32) * keep, h1_ref[...].astype(F32)], axis=0)
        pre = _wsum(w_ref[...], _shifts_causal(ext, 3, tm)) + b_ref[...]
        o_ref[...] = (pre * _sigmoid(pre) * h3_ref[...].astype(F32)).astype(BF16)

    return R.call(body, "ffn_act_fwd", _FB,
                  [R.tile(_FW), R.prev(_FW), R.tile(_FW, _FB), R.colvec(3, _FW), R.colvec(1, _FW)],
                  R.tile(_FW), jax.ShapeDtypeStruct((T, FF), BF16), (hv, hv, hv, conv_w, conv_b))


def _ffn_act_bwd(dg, hv, conv_w, conv_b):
    T = hv.shape[0]
    R = _Rows(T, 256)
    tm = R.tm

    def body(dg_ref, dgn_ref, h1_ref, h1p_ref, h1n_ref, h3_ref, h3n_ref, w_ref, b_ref, dh3_ref, dh1_ref, dw_ref,
             db_ref):
        i = pl.program_id(1)
        keep_p = (i > 0).astype(F32)
        keep_n = (i < R.nrow - 1).astype(F32)
        w = w_ref[...]
        ext = jnp.concatenate([h1p_ref[...].astype(F32) * keep_p, h1_ref[...].astype(F32),
                               h1n_ref[...].astype(F32)], axis=0)
        sh = _shifts_causal(ext, 3, tm + HALO)
        pre = _wsum(w, sh) + b_ref[...]
        s = _sigmoid(pre)
        d = jnp.concatenate([dg_ref[...].astype(F32), dgn_ref[...].astype(F32) * keep_n], axis=0)
        h3 = jnp.concatenate([h3_ref[...].astype(F32), h3n_ref[...].astype(F32)], axis=0)
        dh3_ref[...] = (d[:tm] * pre[:tm] * s[:tm]).astype(BF16)
        dpre = d * h3 * s * (1.0 + pre * (1.0 - s))
        dh1_ref[...] = _wsum(w, _shifts_anticausal(dpre, 3, tm)).astype(BF16)
        dp = dpre[:tm]
        _acc_rows(dw_ref, [_colsum(dp * q[:tm]) for q in sh], i == 0)
        _acc_out(db_ref, _colsum(dp), i == 0)

    return R.call(
        body, "ffn_act_bwd", _FB,
        [R.tile(_FW), R.next(_FW), R.tile(_FW), R.prev(_FW), R.next(_FW), R.tile(_FW, _FB), R.next(_FW, _FB),
         R.colvec(3, _FW), R.colvec(1, _FW)],
        [R.tile(_FW), R.tile(_FW), R.colvec(3, _FW), R.colvec(1, _FW)],
        [jax.ShapeDtypeStruct((T, FF), BF16), jax.ShapeDtypeStruct((T, FF), BF16),
         jax.ShapeDtypeStruct((3, FF), F32), jax.ShapeDtypeStruct((1, FF), F32)],
        (dg, dg, hv, hv, hv, hv, hv, conv_w, conv_b))


def _final_loss_epilogue(w, target):
    T = target.shape[0]

    def fn(xv, ins, outs, first):
        w_ref, t_ref = ins
        l_ref, dh_ref, dhb_ref, dw_ref = outs
        wv = w_ref[...]
        r = lax.rsqrt(jnp.mean(xv * xv, axis=-1, keepdims=True) + EPS)
        xh = xv * r
        err = xh * wv - t_ref[...]
        part = 0.5 * jnp.sum(jnp.mean(err * err, axis=-1, keepdims=True), axis=0, keepdims=True)
        _acc_out(l_ref, jnp.broadcast_to(part, l_ref.shape), first)
        dy = err * (1.0 / D)
        dxh = dy * wv
        dh = r * (dxh - xh * jnp.mean(dxh * xh, axis=-1, keepdims=True))
        dh_ref[...] = dh
        dhb_ref[...] = dh.astype(BF16)
        _acc_out(dw_ref, _colsum(dy * xh), first)

    return _Epilogue(fn, (w, target),
                     (((8, LANES), F32), ((T, D), F32), ((T, D), BF16), ((1, D), F32)), 10 * D)


def _pad_lanes(v, n=LANES):
    return jnp.pad(v, ((0, 0), (0, n - v.shape[1])))


class _Hooks:
    def before_in_proj(self, w_in):
        return w_in

    def late_weights(self, wts, after):
        return wts

    def grads_ready(self, grads, tie):
        return tie

    def mark(self, name, value):
        return value


def _local_step(x, target, wts, hooks=None):
    hooks = hooks or _Hooks()
    T = x.shape[0]
    w_in = wts["w_in"]
    dt_bias_p, a_log_p = _pad_lanes(wts["dt_bias"]), _pad_lanes(wts["a_log"])
    dsk_rep = jnp.repeat(wts["d_skip"], HP, axis=1)

    w_in = hooks.before_in_proj(w_in)
    proj, u, dt_raw = _norm_matmul(x, wts["norm_mix_w"], w_in, "norm_mm_in", w_in[:, OFF_DT:OFF_DT + LANES])
    ya_in = _branch_a_fwd(proj, wts["conv_a_w"])
    xact = _xbc_fwd(proj, wts["ssd_conv_w"], wts["ssd_conv_b"])
    dt, acum, acumT = _dt_fwd(dt_raw, dt_bias_p, a_log_p)
    y_ssd, yn, sprev = _ssd_fwd(xact, dt, acum, acumT, dsk_rep, proj, wts["ssd_norm_w"])
    late = hooks.late_weights(wts, yn)
    w_a_out, w_s_out, w_o, w_up, w_down = (late[k] for k in ("w_a_out", "w_s_out", "w_o", "w_up", "w_down"))
    y_a = _matmul(ya_in, w_a_out, mode="nn", out_dtype=BF16, name="mm_a_out")
    merged, y_s = _matmul(yn, w_s_out, mode="nn", out_dtype=BF16, name="mm_s_out_merge",
                          epilogue=_merge_fwd_epilogue(proj, y_a))
    h1 = _matmul(merged, w_o, mode="nn", out_dtype=F32, name="mm_o", residual=x)
    hv, v = _norm_matmul(h1, wts["norm_ffn_w"], w_up, "norm_mm_up")
    gact = _ffn_act_fwd(hv, wts["ffn_conv_w"], wts["ffn_conv_b"])
    loss, dh2, dh2b, g_final = _matmul(gact, w_down, mode="nn", out_dtype=F32, name="mm_down_loss", residual=h1,
                                       epilogue=_final_loss_epilogue(wts["final_norm_w"], target))

    grads = {"final_norm_w": g_final}
    grads["w_down"] = _matmul(gact, dh2b, mode="tn", out_dtype=BF16, name="mm_down_dw")
    dgact = _matmul(dh2b, w_down, mode="nt", out_dtype=BF16, name="mm_down_dx")
    dh3, dh1c, grads["ffn_conv_w"], grads["ffn_conv_b"] = _ffn_act_bwd(dgact, hv, wts["ffn_conv_w"], wts["ffn_conv_b"])
    grads["w_up"] = (_matmul(v, dh1c, mode="tn", out_dtype=BF16, name="mm_up_dw1"),
                     _matmul(v, dh3, mode="tn", out_dtype=BF16, name="mm_up_dw3"))
    dv = _matmul(dh1c, w_up, mode="nt", out_dtype=F32, name="mm_up_dx1")
    dh1, dh1b, grads["norm_ffn_w"] = _matmul(
        dh3, w_up, mode="nt", out_dtype=F32, name="mm_up_dx3_norm", residual=dv, b_k_off=FF,
        epilogue=_rmsnorm_bwd_epilogue(h1, wts["norm_ffn_w"], dh2))
    grads["w_o"] = _matmul(merged, dh1b, mode="tn", out_dtype=BF16, name="mm_o_dw")
    dproj, dya, dys = _matmul(dh1b, w_o, mode="nt", out_dtype=BF16, name="mm_o_dx_merge",
                              epilogue=_merge_bwd_epilogue(proj, y_a, y_s, NIP))
    grads["w_a_out"] = _matmul(ya_in, dya, mode="tn", out_dtype=BF16, name="mm_a_out_dw")
    dya_in = _matmul(dya, w_a_out, mode="nt", out_dtype=BF16, name="mm_a_out_dx")
    dproj, grads["conv_a_w"] = _branch_a_bwd(dya_in, proj, wts["conv_a_w"], dproj)
    grads["w_s_out"] = _matmul(yn, dys, mode="tn", out_dtype=BF16, name="mm_s_out_dw")
    dys = hooks.grads_ready({k: grads[k] for k in ("w_a_out", "w_s_out", "w_o", "w_up", "w_down")}, dys)
    dyn =_matmul(dys, w_s_out, mode="nt", out_dtype=BF16, name="mm_s_out_dx")
    dxact, ddt_x, dacum, dskl, dproj, grads["ssd_norm_w"] = _ssd_bwd(
        dyn, y_ssd, proj, wts["ssd_norm_w"], dproj, xact, dt, acum, acumT, dsk_rep, sprev)
    dxact = hooks.mark("ssd_bwd", dxact)
    grads["d_skip"] = dskl.reshape(NH, HP).sum(axis=1).reshape(1, NH)
    dproj, grads["ssd_conv_w"], grads["ssd_conv_b"] = _xbc_bwd(dxact, proj, wts["ssd_conv_w"], wts["ssd_conv_b"], dproj)
    dproj, g_dtb, g_alog = _dt_bwd(dacum, ddt_x, dt_raw, dt_bias_p, a_log_p, dproj)
    grads["dt_bias"], grads["a_log"] = g_dtb[:, :NH], g_alog[:, :NH]
    grads["w_in"] = _matmul(u, dproj, mode="tn", out_dtype=BF16, name="mm_in_dw")
    dproj = hooks.grads_ready({"w_in": grads["w_in"]}, dproj)
    grad_x, _, grads["norm_mix_w"] = _matmul(dproj, w_in, mode="nt", out_dtype=F32, name="mm_in_dx_norm",
                                             epilogue=_rmsnorm_bwd_epilogue(x, wts["norm_mix_w"], dh1))
    return loss, grad_x, grads


def _permute_w_in(slabs):
    cs = slabs.shape[2]
    pieces = []
    for o, n, no in sorted(_SEGS, key=lambda seg: seg[2]):
        for s in range(slabs.shape[0]):
            lo, hi = max(o, s * cs), min(o + n, (s + 1) * cs)
            if lo < hi:
                pieces.append(slabs[s][:, lo - s * cs:hi - s * cs])
    pieces.append(jnp.zeros((slabs.shape[1], NIP - OFF_DT - _SEGS[-1][1]), slabs.dtype))
    return jnp.concatenate(pieces, axis=1)


def _unpermute_w_in(g):
    cs = NI // NCHIP
    slabs = []
    for s in range(NCHIP):
        pieces = []
        for o, n, no in sorted(_SEGS):
            lo, hi = max(o, s * cs), min(o + n, (s + 1) * cs)
            if lo < hi:
                pieces.append(g[:, no + lo - o:no + hi - o])
        slabs.append(jnp.concatenate(pieces, axis=1))
    return jnp.stack(slabs)


MESH = pl.DeviceIdType.MESH
NCHIP = 4
NDEV = 8

_W_IN = (("w_in", D, NI // NCHIP, 1),)
_W_REST = (("w_a_out", D // NCHIP, D, 0), ("w_s_out", DI // NCHIP, D, 0), ("w_o", D // NCHIP, D, 0),
           ("w_up", D, 2 * FF // NCHIP, 1), ("w_down", FF // NCHIP, D, 0))


def _coords():
    return lax.axis_index("x"), lax.axis_index("y"), lax.axis_index("c")


def _other_chips(x, y):
    return [(1 - x, y), (x, 1 - y), (1 - x, 1 - y)]


def _ag_weights(shard):
    nrows = shard.shape[0]
    hr = nrows // 2

    def body(x_ref, out_ref, send_sems, recv_sems, local_sem):
        x, y, c = _coords()
        me = 2 * x + y
        chips = _other_chips(x, y)

        def rows(s, h):
            return out_ref.at[s, pl.ds(h * hr, hr), :]

        def copy(k, s, h, to, src=None):
            return pltpu.make_async_remote_copy(
                src_ref=rows(s, h) if src is None else src, dst_ref=rows(s, h),
                send_sem=send_sems.at[k], recv_sem=recv_sems.at[k], device_id=to, device_id_type=MESH)

        mine = pltpu.make_async_copy(x_ref, out_ref.at[me], local_sem)
        mine.start()
        first = [copy(k, me, c, (*chip, c), src=x_ref.at[pl.ds(c * hr, hr), :]) for k, chip in enumerate(chips)]
        for cp in first:
            cp.start()
        passed = []
        for k, chip in enumerate(chips):
            s = 2 * chip[0] + chip[1]
            copy(k, s, c, (x, y, c)).wait_recv()
            fwd = copy(3 + k, s, c, (x, y, 1 - c))
            fwd.start()
            passed.append(fwd)
        for k, chip in enumerate(chips):
            copy(3 + k, 2 * chip[0] + chip[1], 1 - c, (x, y, c)).wait_recv()
        for cp in first + passed:
            cp.wait_send()
        mine.wait()

    return pl.pallas_call(
        body, name="ag_weights", in_specs=[ANY], out_specs=ANY,
        out_shape=jax.ShapeDtypeStruct((NCHIP,) + shard.shape, shard.dtype),
        scratch_shapes=[pltpu.SemaphoreType.DMA((6,)), pltpu.SemaphoreType.DMA((6,)), pltpu.SemaphoreType.DMA],
        compiler_params=pltpu.CompilerParams(has_side_effects=True),
    )(shard)


HBM = pl.BlockSpec(memory_space=pltpu.HBM)
SEM = pl.BlockSpec(memory_space=pltpu.SEMAPHORE)
_EFFECT = pltpu.SideEffectType.DATAFLOW_SIDE_EFFECTING
_NCOPY = NCHIP - 1


def _plan_bcast(src_ref, land_ref, send_sems, recv_sems, base):
    x, y, c = _coords()
    sends, lands = [], []
    for k, chip in enumerate(_other_chips(x, y)):
        def copy(slot):
            return pltpu.make_async_remote_copy(
                src_ref=src_ref, dst_ref=land_ref.at[slot], send_sem=send_sems.at[base + k],
                recv_sem=recv_sems.at[base + k], device_id=(*chip, c), device_id_type=MESH)
        sends.append(copy(2 * x + y))
        lands.append(copy(2 * chip[0] + chip[1]))
    return sends, lands


def _plan_scatter(src_ref, land_ref, send_sems, recv_sems, base):
    x, y, c = _coords()
    cps = [pltpu.make_async_remote_copy(
        src_ref=src_ref.at[2 * chip[0] + chip[1]], dst_ref=land_ref.at[k], send_sem=send_sems.at[base + k],
        recv_sem=recv_sems.at[base + k], device_id=(*chip, c), device_id_type=MESH)
        for k, chip in enumerate(_other_chips(x, y))]
    return cps, cps


def _plan_swap(src_ref, land_ref, send_sems, recv_sems, base):
    x, y, c = _coords()
    cp = pltpu.make_async_remote_copy(
        src_ref=src_ref, dst_ref=land_ref, send_sem=send_sems.at[base], recv_sem=recv_sems.at[base],
        device_id=(x, y, 1 - c), device_id_type=MESH)
    return [cp], [cp]


def _plan_all(plan, refs, n):
    sends, lands = [], []
    for t in range(n):
        s, l = plan(refs[t], refs[n + t], refs[2 * n], refs[2 * n + 1], t * _NCOPY)
        sends += s
        lands += l
    return sends, lands


def _split_start(name, srcs, lands, plan):
    n = len(srcs)

    def body(*refs):
        for cp in _plan_all(plan, refs, n)[0]:
            cp.start()
        refs[-1][...] = jnp.zeros_like(refs[-1])

    arrays = list(srcs) + list(lands)
    outs = pl.pallas_call(
        body, name=name,
        out_shape=(pltpu.SemaphoreType.DMA((n * _NCOPY,)), pltpu.SemaphoreType.DMA((n * _NCOPY,)),
                   *[pltpu.HBM(a.shape, a.dtype) for a in arrays], jax.ShapeDtypeStruct((8, LANES), F32)),
        in_specs=(HBM,) * (2 * n),
        out_specs=(SEM, SEM) + (HBM,) * (2 * n) + (pl.BlockSpec(memory_space=pltpu.VMEM),),
        input_output_aliases={t: 2 + t for t in range(2 * n)},
        compiler_params=pltpu.CompilerParams(has_side_effects=_EFFECT),
    )(*[pltpu.with_memory_space_constraint(a, pltpu.HBM) for a in arrays])
    return (outs[0], outs[1], tuple(outs[2:2 + 2 * n])), outs[-1]


def _split_wait(name, handle, after, plan):
    send_sems, recv_sems, arrays = handle
    n = len(arrays) // 2

    def body(*refs):
        sends, lands = _plan_all(plan, refs[:2 * n] + refs[2 * n:2 * n + 2], n)
        for cp in sends:
            cp.wait_send()
        for cp in lands:
            cp.wait_recv()

    outs = pl.pallas_call(
        body, name=name, out_shape=tuple(pltpu.HBM(a.shape, a.dtype) for a in arrays),
        in_specs=(HBM,) * (2 * n) + (SEM, SEM, ANY), out_specs=(HBM,) * (2 * n),
        input_output_aliases={t: t for t in range(2 * n)},
        compiler_params=pltpu.CompilerParams(has_side_effects=_EFFECT),
    )(*arrays, send_sems, recv_sems, after)
    return outs[:n], outs[n:]


def _tie(x, token, name):
    def body(x_ref, t_ref, o_ref):
        pass

    return pl.pallas_call(
        body, name=name, in_specs=[ANY, pl.BlockSpec(memory_space=pltpu.VMEM)], out_specs=ANY,
        out_shape=jax.ShapeDtypeStruct(x.shape, x.dtype), input_output_aliases={0: 0},
    )(x, token)


_ADD_BYTES = 7 << 19


def _add_tile(rows, cols):
    best = 32
    for t in range(32, rows + 1, 32):
        if rows % t == 0 and t * cols * 4 <= _ADD_BYTES:
            best = t
    return best


def _add_slabs(pack, land, me, name):
    rows, cols = pack.shape[1:]
    tr = _add_tile(rows, cols)

    def body(me_ref, p_ref, l_ref, o_ref):
        f = lambda r: r.astype(F32)
        o_ref[...] = ((f(p_ref[0]) + f(l_ref[0])) + f(l_ref[1])) + f(l_ref[2])

    return pl.pallas_call(
        body, name=name,
        grid_spec=pltpu.PrefetchScalarGridSpec(
            num_scalar_prefetch=1, grid=(rows // tr,),
            in_specs=[pl.BlockSpec((1, tr, cols), lambda i, me_ref: (me_ref[0], i, 0)),
                      pl.BlockSpec((_NCOPY, tr, cols), lambda i, me_ref: (0, i, 0))],
            out_specs=pl.BlockSpec((tr, cols), lambda i, me_ref: (i, 0))),
        out_shape=jax.ShapeDtypeStruct((rows, cols), F32),
        compiler_params=_params(("parallel",)),
    )(me, pack, land)


_STAGE_W = 1024


def _stage_rows(shapes):
    pieces, r = [], 0
    for i, (k, w) in enumerate(shapes):
        for a in range(k):
            for q in range(0, w, _STAGE_W):
                pieces.append((i, a, q, min(_STAGE_W, w - q), r))
                r += 1
    return pieces, -(-r // 8) * 8


def _gather8(parts, reduce, name):
    shapes = [p.shape for p in parts]
    pieces, rows = _stage_rows(shapes)
    n = len(parts)

    def body(*refs):
        ins, outs = refs[:n], refs[n:2 * n]
        stage, buf, res, send_sems, recv_sems = refs[2 * n:]
        x, y, c = _coords()
        me = 4 * x + 2 * y + c
        stage[...] = jnp.zeros_like(stage)
        for i, a, q, w, r in pieces:
            stage[r:r + 1, 0:w] = ins[i][a:a + 1, q:q + w]
        buf[pl.ds(me, 1)] = stage[...][None]
        cps, lands = [], []
        for k in range(1, NDEV):
            peer = (1 - x if k & 4 else x, 1 - y if k & 2 else y, 1 - c if k & 1 else c)

            def copy(slot):
                return pltpu.make_async_remote_copy(
                    src_ref=stage, dst_ref=buf.at[slot], send_sem=send_sems.at[k - 1],
                    recv_sem=recv_sems.at[k - 1], device_id=peer, device_id_type=MESH)

            cps.append(copy(me))
            lands.append(copy(4 * peer[0] + 2 * peer[1] + peer[2]))
        for cp in cps:
            cp.start()
        for cp, land in zip(cps, lands):
            land.wait_recv()
            cp.wait_send()
        if reduce:
            acc = buf[0]
            for d in range(1, NDEV):
                acc = acc + buf[d]
            res[...] = acc
            for i, a, q, w, r in pieces:
                outs[i][a:a + 1, q:q + w] = res[r:r + 1, 0:w]
        else:
            for i, a, q, w, r in pieces:
                for s in range(NCHIP):
                    outs[i][s, a:a + 1, q:q + w] = buf[2 * s, r:r + 1, 0:w]

    vm = pl.BlockSpec(memory_space=pltpu.VMEM)
    out_shapes = [jax.ShapeDtypeStruct(s if reduce else (NCHIP,) + s, F32) for s in shapes]
    return pl.pallas_call(
        body, name=name, in_specs=[vm] * n, out_specs=[vm] * n, out_shape=out_shapes,
        scratch_shapes=[pltpu.VMEM((rows, _STAGE_W), F32), pltpu.VMEM((NDEV, rows, _STAGE_W), F32),
                        pltpu.VMEM((rows, _STAGE_W), F32), pltpu.SemaphoreType.DMA((NDEV - 1,)),
                        pltpu.SemaphoreType.DMA((NDEV - 1,))],
        compiler_params=pltpu.CompilerParams(has_side_effects=True),
    )(*parts)


def _adamw_update(w_ref, g_ref, m_ref, v_ref, d_ref, mo_ref, vo_ref):
    c1 = 1.0 / (1.0 - ADAM_B1 ** ADAM_STEP)
    c2 = 1.0 / (1.0 - ADAM_B2 ** ADAM_STEP)
    gv = g_ref[...]
    mn = ADAM_B1 * m_ref[...] + (1.0 - ADAM_B1) * gv
    vn = ADAM_B2 * v_ref[...] + (1.0 - ADAM_B2) * (gv * gv)
    d_ref[...] = -ADAM_LR * ((mn * c1) / (jnp.sqrt(vn * c2) + ADAM_EPS) + ADAM_WD * w_ref[...])
    mo_ref[...] = mn
    vo_ref[...] = vn


def _adamw_small(ws, gs, ms, vs):
    n = len(ws)

    def body(*refs):
        for i in range(n):
            _adamw_update(*(refs[j * n + i] for j in range(7)))

    vm = pl.BlockSpec(memory_space=pltpu.VMEM)
    outs = pl.pallas_call(
        body, name="adamw_small", in_specs=[vm] * (4 * n), out_specs=[vm] * (3 * n),
        out_shape=[jax.ShapeDtypeStruct(w.shape, F32) for w in ws] * 3,
    )(*ws, *gs, *ms, *vs)
    return outs[:n], outs[n:2 * n], outs[2 * n:]


def _adamw(w, g_parts, m, v, name):
    rows, cols = w.shape
    tr = rows
    while tr * cols * 4 > (1 << 20) and tr % 16 == 0:
        tr //= 2

    def body(w_ref, ga_ref, gb_ref, m_ref, v_ref, g_ref, d_ref, mo_ref, vo_ref):
        g_ref[...] = ga_ref[...] + gb_ref[...]
        _adamw_update(w_ref, g_ref, m_ref, v_ref, d_ref, mo_ref, vo_ref)

    blk = pl.BlockSpec((tr, cols), lambda i: (i, 0))
    return pl.pallas_call(
        body, name=name, grid=(rows // tr,), in_specs=[blk] * 5, out_specs=[blk] * 4,
        out_shape=[jax.ShapeDtypeStruct((rows, cols), F32)] * 4, compiler_params=_params(("parallel",)),
    )(w, *g_parts, m, v)


def _by_chip(g, rr, cc, axis):
    if isinstance(g, tuple):
        n = NCHIP // len(g)
        return jnp.concatenate([h.reshape(rr, n, cc).transpose(1, 0, 2) for h in g], axis=0)
    return g.reshape(NCHIP, rr, cc) if axis == 0 else g.reshape(rr, NCHIP, cc).transpose(1, 0, 2)


_SMALL_REPL = ("norm_mix_w", "ssd_conv_b", "dt_bias", "a_log", "d_skip", "ssd_norm_w", "norm_ffn_w",
               "ffn_conv_b", "final_norm_w")
_SMALL_CONV = (("conv_a_w", 3, D), ("ssd_conv_w", 4, DX), ("ffn_conv_w", 3, FF))


def kernel(x, norm_mix_w, w_in, conv_a_w, w_a_out, ssd_conv_w, ssd_conv_b, dt_bias, a_log, d_skip, ssd_norm_w, w_s_out, w_o, norm_ffn_w, w_up, ffn_conv_w, ffn_conv_b, w_down, final_norm_w, loss_target, m_norm_mix_w, m_w_in, m_conv_a_w, m_w_a_out, m_ssd_conv_w, m_ssd_conv_b, m_dt_bias, m_a_log, m_d_skip, m_ssd_norm_w, m_w_s_out, m_w_o, m_norm_ffn_w, m_w_up, m_ffn_conv_w, m_ffn_conv_b, m_w_down, m_final_norm_w, v_norm_mix_w, v_w_in, v_conv_a_w, v_w_a_out, v_ssd_conv_w, v_ssd_conv_b, v_dt_bias, v_a_log, v_d_skip, v_ssd_norm_w, v_w_s_out, v_w_o, v_norm_ffn_w, v_w_up, v_ffn_conv_w, v_ffn_conv_b, v_w_down, v_final_norm_w):
    names = ("norm_mix_w", "w_in", "conv_a_w", "w_a_out", "ssd_conv_w", "ssd_conv_b", "dt_bias", "a_log", "d_skip",
             "ssd_norm_w", "w_s_out", "w_o", "norm_ffn_w", "w_up", "ffn_conv_w", "ffn_conv_b", "w_down", "final_norm_w")
    W = dict(zip(names, (norm_mix_w, w_in, conv_a_w, w_a_out, ssd_conv_w, ssd_conv_b, dt_bias, a_log, d_skip,
                         ssd_norm_w, w_s_out, w_o, norm_ffn_w, w_up, ffn_conv_w, ffn_conv_b, w_down, final_norm_w)))
    M = dict(zip(names, (m_norm_mix_w, m_w_in, m_conv_a_w, m_w_a_out, m_ssd_conv_w, m_ssd_conv_b, m_dt_bias, m_a_log,
                         m_d_skip, m_ssd_norm_w, m_w_s_out, m_w_o, m_norm_ffn_w, m_w_up, m_ffn_conv_w, m_ffn_conv_b,
                         m_w_down, m_final_norm_w)))
    V = dict(zip(names, (v_norm_mix_w, v_w_in, v_conv_a_w, v_w_a_out, v_ssd_conv_w, v_ssd_conv_b, v_dt_bias, v_a_log,
                         v_d_skip, v_ssd_norm_w, v_w_s_out, v_w_o, v_norm_ffn_w, v_w_up, v_ffn_conv_w, v_ffn_conv_b,
                         v_w_down, v_final_norm_w)))
    two_d = lambda a: a.reshape(-1, a.shape[-1])
    W2, M2, V2 = ({k: two_d(a) for k, a in t.items()} for t in (W, M, V))
    xi, yi, ci = _coords()
    me = 2 * xi + yi

    meidx = me.reshape(1).astype(jnp.int32)
    state = {}


    class Hooks(_Hooks):
        def before_in_proj(self, w_in):
            return _tie(w_in, state["rest_token"], "tie_ag_rest")

        def late_weights(self, wts, after):
            owns, lands = _split_wait("ag_rest_wait", state["rest"], after, _plan_bcast)
            full = {}
            for (n, rr, cc, axis), own, land in zip(_W_REST, owns, lands):
                slabs = lax.dynamic_update_slice(land, own[None], (me, 0, 0))
                full[n] = slabs.reshape(NCHIP * rr, cc) if axis == 0 else slabs.transpose(1, 0, 2).reshape(rr, NCHIP * cc)
            return {**wts, **full}

        def grads_ready(self, grads, tie):
            if "w_in" in grads:
                key, packs = "g_in", [_unpermute_w_in(grads["w_in"])]
            else:
                key = "g_rest"
                packs = [_by_chip(grads[n], rr, cc, axis)
                         for n, rr, cc, axis in _W_REST]
            lands = [lax.empty((_NCOPY,) + p.shape[1:], BF16) for p in packs]
            state[key], token = _split_start("rs_" + key + "_start", packs, lands, _plan_scatter)
            return _tie(tie, token, "tie_" + key)

        def mark(self, name, value):
            return _tie(value, reduced("g_rest", value, _W_REST), "tie_g_rest_swap")

    def reduced(key, after, group):
        packs, lands = _split_wait("rs_" + key + "_wait", state[key], after, _plan_scatter)
        mines = [_add_slabs(p, l, meidx, "rs_add_chips_" + n) for (n, *_), p, l in zip(group, packs, lands)]
        state[key + "_swap"], token = _split_start(
            "rs_" + key + "_swap_start", mines, [lax.empty(m.shape, F32) for m in mines], _plan_swap)
        return token

    def swapped(key, after, group):
        mines, theirs = _split_wait("rs_" + key + "_swap_wait", state[key + "_swap"], after, _plan_swap)
        return dict(zip([n for n, *_ in group], zip(mines, theirs)))

    w_in_slabs = _ag_weights(W2["w_in"].astype(BF16))
    wts = {k: W2[k] for k in _SMALL_REPL}
    conv_by_chip = _gather8([W2[n] for n, *_ in _SMALL_CONV], False, "ag_conv_weights")
    for (n, kk, width), stacked in zip(_SMALL_CONV, conv_by_chip):
        wts[n] = stacked.transpose(1, 0, 2).reshape(kk, width)
    rest = [W2[n].astype(BF16) for n, *_ in _W_REST]
    rest[0] = _tie(rest[0], conv_by_chip[0], "tie_ag_order")
    state["rest"], state["rest_token"] = _split_start(
        "ag_rest_start", rest, [lax.empty((NCHIP,) + r.shape, BF16) for r in rest], _plan_bcast)
    wts["w_in"] = _permute_w_in(w_in_slabs)

    loss8, grad_x, grads = _local_step(x[0], loss_target[0], wts, Hooks())

    reduced("g_in", grad_x, _W_IN)

    small_parts = [grads[n] for n in _SMALL_REPL] + [loss8[0:1]] + [grads[n] for n, *_ in _SMALL_CONV]
    small_g = _gather8(small_parts, True, "allreduce_small")
    gsm = dict(zip(_SMALL_REPL, small_g[:len(_SMALL_REPL)]))
    loss = small_g[len(_SMALL_REPL)][0, 0]
    for (n, kk, width), gfull in zip(_SMALL_CONV, small_g[len(_SMALL_REPL) + 1:]):
        cw = width // NCHIP
        gsm[n] = lax.dynamic_slice(gfull, (0, me * cw), (kk, cw))

    G, DW, NM, NV = {}, {}, {}, {}
    gbig = swapped("g_rest", grad_x, _W_REST)
    for n in [b[0] for b in _W_REST]:
        G[n], DW[n], NM[n], NV[n] = _adamw(W2[n], gbig[n], M2[n], V2[n], "adamw_" + n)
    gbig = swapped("g_in", DW[_W_REST[-1][0]], _W_IN)
    for n in [b[0] for b in _W_IN]:
        G[n], DW[n], NM[n], NV[n] = _adamw(W2[n], gbig[n], M2[n], V2[n], "adamw_" + n)
    sm_names = list(_SMALL_REPL) + [n for n, *_ in _SMALL_CONV]
    outs = _adamw_small(*([t[n] for n in sm_names] for t in (W2, gsm, M2, V2)))
    for t, vals in zip((DW, NM, NV), outs):
        t.update(zip(sm_names, vals))
    G.update(gsm)

    def shaped(t):
        return [t[n].reshape(W[n].shape) for n in names]

    return (loss, grad_x.reshape(x.shape), *shaped(G), *shaped(DW), *shaped(NM), *shaped(NV))
```

```python
import jax
import jax.numpy as jnp
from jax import lax
from jax.experimental import pallas as pl
from jax.experimental.pallas import tpu as pltpu

F32 = jnp.float32
BF16 = jnp.bfloat16

D = 1024
DI = 2048
NH = 32
HP = 64
NG = 4
NS = 128
CH = 128
DX = 3072
FF = 2816
NI = 10272
EPS = 1e-5

OFF_BCV, OFF_XBC, OFF_G, OFF_Z, OFF_DT = 0, 3072, 6144, 8192, 10240
NIP = 10752
_SEGS = ((0, 2048, OFF_G), (2048, 3072, OFF_BCV), (5120, 2048, OFF_Z), (7168, 3072, OFF_XBC), (10240, 32, OFF_DT))

LANES = 128
HALO = 16
V7X_VMEM_LIMIT = 56 * 2 ** 20

ADAM_LR, ADAM_B1, ADAM_B2, ADAM_EPS, ADAM_WD, ADAM_STEP = 0.001, 0.9, 0.999, 1e-08, 0.01, 10

NN = (((1,), (0,)), ((), ()))
NT = (((1,), (1,)), ((), ()))
TN = (((0,), (0,)), ((), ()))


def _dot(a, b, dims=NN):
    return lax.dot_general(a, b, dims, preferred_element_type=F32)


def _params(sem, **kw):
    return pltpu.CompilerParams(dimension_semantics=sem, vmem_limit_bytes=V7X_VMEM_LIMIT, **kw)


V7X_MXU = 256
V7X_HBM_BYTES_PER_S = 3.5e12
STEP_S = 0.35e-6
MATMUL_VMEM = 40 * 2 ** 20
EPILOGUE_VMEM = 46 * 2 ** 20


ACC_BYTES_PER_S = 1.2e13


def _divisors(dim, cap, units):
    for unit in units:
        c = [t for t in range(unit, min(dim, cap) + 1, unit) if dim % t == 0]
        if c:
            return c
    return [dim]


def _tiles(M, N, K, out_bytes, has_res):
    best = None
    for tn in _divisors(N, 2816, (V7X_MXU, LANES)):
        for tm in _divisors(M, 2816, (LANES,)):
            for tk in _divisors(K, 2816, (V7X_MXU, LANES)):
                nk, ni, nj = K // tk, M // tm, N // tn
                vmem = 4 * (tm * tk + tk * tn) + 2 * tm * tn * out_bytes
                vmem += (4 * tm * tn if nk > 1 else 0) + (8 * tm * tn if has_res else 0)
                if vmem > MATMUL_VMEM:
                    continue
                a_reads = M * K * 2 * (nj if nk > 1 else 1)
                b_reads = K * N * 2 * (ni if nk * nj > 1 else 1)
                cost = (a_reads + b_reads + M * N * out_bytes) / V7X_HBM_BYTES_PER_S + ni * nj * nk * STEP_S
                cost += (nk - 1) * M * N * 8 / ACC_BYTES_PER_S
                if best is None or cost < best[0]:
                    best = (cost, tm, tn, tk)
    assert best is not None, (M, N, K)
    return best[1:]


def _sigmoid(x):
    return 1.0 / (1.0 + jnp.exp(-x))


class _Epilogue:
    def __init__(self, fn, ins, outs, tile_bytes, in_windows=None, out_windows=None):
        self.fn, self.ins, self.outs, self.tile_bytes = fn, tuple(ins), tuple(outs), tile_bytes
        self.in_windows, self.out_windows = in_windows or {}, out_windows or {}


def _matmul(a, b, *, mode, out_dtype, name, residual=None, b_k_off=0, epilogue=None):
    if mode == "nn":
        (M, K), (K2, N) = a.shape, b.shape
    elif mode == "nt":
        (M, K), (N, K2) = a.shape, (b.shape[0], a.shape[1])
        assert b_k_off + K <= b.shape[1]
    else:
        (K, M), (K2, N) = a.shape, b.shape
    assert K == K2, (name, a.shape, b.shape)
    tm, tn, tk = _tiles(M, N, K, jnp.dtype(out_dtype).itemsize, residual is not None)
    if epilogue is not None:
        tn = N
        fits = [(K * N * 2 * (M // t) / V7X_HBM_BYTES_PER_S + (K // q - 1) * M * N * 8 / ACC_BYTES_PER_S
                 + (M // t) * (K // q) * STEP_S, t, q)
                for t in (1024, 512, 256) if M % t == 0 for q in _divisors(K, 2816, (V7X_MXU, LANES))
                if 4 * (t * q + q * tn) + (4 * t * tn if K > q else 0) + (8 * t * tn if residual is not None else 0)
                + 2 * t * epilogue.tile_bytes <= EPILOGUE_VMEM]
        _, tm, tk = min(fits)
    nk = K // tk
    if mode == "tn":
        a_spec = pl.BlockSpec((tk, tm), lambda i, j, k: (k, i))
    else:
        a_spec = pl.BlockSpec((tm, tk), lambda i, j, k: (i, k))
    if mode == "nt":
        assert b_k_off % tk == 0
        b_spec = pl.BlockSpec((tn, tk), lambda i, j, k: (j, k + b_k_off // tk))
    else:
        b_spec = pl.BlockSpec((tk, tn), lambda i, j, k: (k, j))
    dims = {"nn": NN, "nt": NT, "tn": TN}[mode]
    o_spec = pl.BlockSpec((tm, tn), lambda i, j, k: (i, j))
    has_res = residual is not None

    def rows_or_whole(shape, window=None):
        if window is not None:
            off, width = window
            return pl.BlockSpec((tm, width), lambda i, j, k: (i, off // width))
        if shape[0] == M:
            return pl.BlockSpec((tm,) + tuple(shape[1:]), lambda i, j, k: (i,) + (0,) * (len(shape) - 1))
        return pl.BlockSpec(tuple(shape), lambda i, j, k: (0,) * len(shape))

    n_in = 2 + has_res + (len(epilogue.ins) if epilogue else 0)
    n_out = len(epilogue.outs) if epilogue else 1

    def body(*refs):
        a_ref, b_ref = refs[:2]
        r_ref = refs[2] if has_res else None
        out_refs = refs[n_in:n_in + n_out]
        acc_ref = refs[-1]
        k = pl.program_id(2)
        part = _dot(a_ref[...], b_ref[...], dims)

        def finish(r):
            if has_res:
                r = r + r_ref[...].astype(F32)
            if epilogue is None:
                out_refs[0][...] = r.astype(out_dtype)
            else:
                epilogue.fn(r, refs[2 + has_res:n_in], out_refs, pl.program_id(0) == 0)

        if nk == 1:
            finish(part)
            return

        @pl.when(k == 0)
        def _():
            acc_ref[...] = part

        @pl.when(jnp.logical_and(k > 0, k < nk - 1))
        def _():
            acc_ref[...] += part

        @pl.when(k == nk - 1)
        def _():
            finish(acc_ref[...] + part)

    in_specs = [a_spec, b_spec] + ([o_spec] if has_res else [])
    args = (a, b) + ((residual,) if has_res else ())
    if epilogue is None:
        out_specs, out_shape = o_spec, jax.ShapeDtypeStruct((M, N), out_dtype)
        sem = ("parallel", "parallel", "arbitrary")
    else:
        in_specs += [rows_or_whole(x.shape, epilogue.in_windows.get(n)) for n, x in enumerate(epilogue.ins)]
        args += epilogue.ins
        out_specs = [rows_or_whole(o[0], epilogue.out_windows.get(n)) for n, o in enumerate(epilogue.outs)]
        out_shape = [jax.ShapeDtypeStruct(shp, dt) for shp, dt in epilogue.outs]
        sem = ("arbitrary", "arbitrary", "arbitrary")
    return pl.pallas_call(
        body, name=name, grid=(M // tm, N // tn, nk), in_specs=in_specs, out_specs=out_specs,
        out_shape=out_shape, scratch_shapes=[pltpu.VMEM((tm, tn), F32)] if nk > 1 else [],
        compiler_params=_params(sem),
    )(*args)


class _Rows:
    def __init__(self, T, tm):
        self.T, self.tm = T, min(tm, T // 2)
        self.nrow = T // self.tm
        self.r = self.tm // HALO
        self.nb = T // HALO

    def tile(self, w, cb=0, step=1):
        return pl.BlockSpec((self.tm, w), lambda j, i: (i, cb + step * j))

    def prev(self, w, cb=0, step=1):
        r = self.r
        return pl.BlockSpec((HALO, w), lambda j, i: (jnp.maximum(i * r - 1, 0), cb + step * j))

    def next(self, w, cb=0, step=1):
        r, nb = self.r, self.nb
        return pl.BlockSpec((HALO, w), lambda j, i: (jnp.minimum((i + 1) * r, nb - 1), cb + step * j))

    def colvec(self, k, w, cb=0, step=1):
        return pl.BlockSpec((k, w), lambda j, i: (0, cb + step * j))

    def call(self, body, name, ncol, in_specs, out_specs, out_shape, args, aliases=None):
        return pl.pallas_call(
            body, name=name, grid=(ncol, self.nrow), in_specs=in_specs, out_specs=out_specs,
            out_shape=out_shape, input_output_aliases=aliases or {},
            compiler_params=_params(("parallel", "arbitrary")),
        )(*args)


ANY = pl.BlockSpec(memory_space=pl.ANY)


def _shifts_causal(ext, nk, tm):
    out = []
    for k in range(nk):
        s = nk - 1 - k
        r = ext if s == 0 else pltpu.roll(ext, s, 0)
        out.append(r[HALO:])
    return out


def _shifts_anticausal(ext, nk, tm):
    n = ext.shape[0]
    out = []
    for k in range(nk):
        s = nk - 1 - k
        r = ext if s == 0 else pltpu.roll(ext, n - s, 0)
        out.append(r[:tm])
    return out


def _wsum(w, parts):
    acc = w[0:1, :] * parts[0]
    for k in range(1, len(parts)):
        acc = acc + w[k:k + 1, :] * parts[k]
    return acc


def _colsum(x):
    return jnp.sum(x, axis=0, keepdims=True)


def _acc_out(ref, val, first):
    @pl.when(first)
    def _():
        ref[...] = val

    @pl.when(jnp.logical_not(first))
    def _():
        ref[...] += val


def _acc_rows(ref, rows, first):
    for k, r in enumerate(rows):
        _acc_out(ref.at[k:k + 1, :], r, first)


def _norm_matmul(x, wn, b, name, b_f32=None):
    T, N = x.shape[0], b.shape[1]
    tm = min(1024, T)
    tn = max(t for t in _divisors(N, 2816, (V7X_MXU, LANES))
             if 8 * tm * D + 6 * tm * D + 4 * D * t + 4 * tm * t <= MATMUL_VMEM)

    extra = b_f32 is not None

    def body(*refs):
        x_ref, wn_ref, b_ref = refs[:3]
        o_ref, u_ref = refs[3 + extra:5 + extra]
        keep_ref = refs[-1]

        @pl.when(pl.program_id(1) == 0)
        def _():
            xv = x_ref[...]
            r = lax.rsqrt(jnp.mean(xv * xv, axis=-1, keepdims=True) + EPS)
            u = (xv * r * wn_ref[...]).astype(BF16)
            keep_ref[...] = u
            u_ref[...] = u
            if extra:
                refs[5 + extra][...] = _dot(u, refs[3][...])

        o_ref[...] = _dot(keep_ref[...], b_ref[...]).astype(BF16)

    rows = pl.BlockSpec((tm, D), lambda i, j: (i, 0))
    whole = lambda shape: pl.BlockSpec(shape, lambda i, j: (0, 0))
    narrow = pl.BlockSpec((tm, LANES), lambda i, j: (i, 0))
    return pl.pallas_call(
        body, name=name, grid=(T // tm, N // tn),
        in_specs=[rows, whole((1, D)), pl.BlockSpec((D, tn), lambda i, j: (0, j))] + [whole((D, LANES))] * extra,
        out_specs=[pl.BlockSpec((tm, tn), lambda i, j: (i, j)), rows] + [narrow] * extra,
        out_shape=[jax.ShapeDtypeStruct((T, N), BF16), jax.ShapeDtypeStruct((T, D), BF16)]
        + [jax.ShapeDtypeStruct((T, LANES), F32)] * extra,
        scratch_shapes=[pltpu.VMEM((tm, D), BF16)],
        compiler_params=_params(("parallel", "arbitrary")),
    )(*((x, wn, b) + ((b_f32,) if extra else ())))


def _rmsnorm_bwd_epilogue(x, w, dres):
    T = x.shape[0]

    def fn(dyv, ins, outs, first):
        x_ref, w_ref, dr_ref = ins
        dx_ref, dxb_ref, dw_ref = outs
        xv = x_ref[...]
        r = lax.rsqrt(jnp.mean(xv * xv, axis=-1, keepdims=True) + EPS)
        xh = xv * r
        dxh = dyv * w_ref[...]
        dx = r * (dxh - xh * jnp.mean(dxh * xh, axis=-1, keepdims=True)) + dr_ref[...]
        dx_ref[...] = dx
        dxb_ref[...] = dx.astype(BF16)
        _acc_out(dw_ref, _colsum(dyv * xh), first)

    return _Epilogue(fn, (x, w, dres), (((T, D), F32), ((T, D), BF16), ((1, D), F32)), 14 * D)


def _branch_a_fwd(proj, conv_w):
    T = proj.shape[0]
    R = _Rows(T, 512)
    tm = R.tm

    def body(p_ref, pp_ref, w_ref, o_ref):
        keep = (pl.program_id(1) > 0).astype(F32)
        cv = p_ref[:, D:2 * D].astype(F32) * p_ref[:, 2 * D:].astype(F32)
        cvp = pp_ref[:, D:2 * D].astype(F32) * pp_ref[:, 2 * D:].astype(F32) * keep
        sh = _shifts_causal(jnp.concatenate([cvp, cv], axis=0), 3, tm)
        ca = _wsum(w_ref[...], sh)
        o_ref[...] = (p_ref[:, :D].astype(F32) * ca).astype(BF16)

    return R.call(body, "branch_a_fwd", 1, [R.tile(3 * D), R.prev(3 * D), R.colvec(3, D)], R.tile(D),
                  jax.ShapeDtypeStruct((T, D), BF16), (proj, proj, conv_w))


def _branch_a_bwd(dya_in, proj, conv_w, dproj):
    T = proj.shape[0]
    R = _Rows(T, 256)
    tm = R.tm

    def body(d_ref, dn_ref, p_ref, pp_ref, pn_ref, w_ref, _alias, o_ref, dw_ref):
        i = pl.program_id(1)
        keep_p = (i > 0).astype(F32)
        keep_n = (i < R.nrow - 1).astype(F32)
        w = w_ref[...]
        b = p_ref[:, :D].astype(F32)
        c = p_ref[:, D:2 * D].astype(F32)
        v = p_ref[:, 2 * D:].astype(F32)
        cvp = pp_ref[:, D:2 * D].astype(F32) * pp_ref[:, 2 * D:].astype(F32) * keep_p
        sh = _shifts_causal(jnp.concatenate([cvp, c * v], axis=0), 3, tm)
        ca = _wsum(w, sh)
        d = d_ref[...].astype(F32)
        dca = d * b
        dca_n = dn_ref[...].astype(F32) * pn_ref[:, :D].astype(F32) * keep_n
        dsh = _shifts_anticausal(jnp.concatenate([dca, dca_n], axis=0), 3, tm)
        dcv = _wsum(w, dsh)
        o_ref[:, :D] = (d * ca).astype(BF16)
        o_ref[:, D:2 * D] = (dcv * v).astype(BF16)
        o_ref[:, 2 * D:] = (dcv * c).astype(BF16)
        _acc_rows(dw_ref, [_colsum(dca * s) for s in sh], i == 0)

    return R.call(
        body, "branch_a_bwd", 1,
        [R.tile(D), R.next(D), R.tile(3 * D), R.prev(3 * D), R.next(3 * D), R.colvec(3, D), ANY],
        [R.tile(3 * D), R.colvec(3, D)],
        [jax.ShapeDtypeStruct(dproj.shape, BF16), jax.ShapeDtypeStruct((3, D), F32)],
        (dya_in, dya_in, proj, proj, proj, conv_w, dproj), aliases={6: 0})


_XW = 512


def _xbc_fwd(proj, conv_w, conv_b):
    T = proj.shape[0]
    R = _Rows(T, 512)
    tm = R.tm
    cb = OFF_XBC // _XW

    def body(x_ref, xp_ref, w_ref, b_ref, o_ref):
        keep = (pl.program_id(1) > 0).astype(F32)
        ext = jnp.concatenate([xp_ref[...].astype(F32) * keep, x_ref[...].astype(F32)], axis=0)
        pre = _wsum(w_ref[...], _shifts_causal(ext, 4, tm)) + b_ref[...]
        o_ref[...] = (pre * _sigmoid(pre)).astype(BF16)

    return R.call(body, "xbc_fwd", DX // _XW,
                  [R.tile(_XW, cb), R.prev(_XW, cb), R.colvec(4, _XW), R.colvec(1, _XW)], R.tile(_XW),
                  jax.ShapeDtypeStruct((T, DX), BF16), (proj, proj, conv_w, conv_b))


def _xbc_bwd(dact, proj, conv_w, conv_b, dproj):
    T = proj.shape[0]
    R = _Rows(T, 512)
    tm = R.tm
    cb = OFF_XBC // _XW

    def body(d_ref, dn_ref, x_ref, xp_ref, xn_ref, w_ref, b_ref, _alias, o_ref, dw_ref, db_ref):
        i = pl.program_id(1)
        keep_p = (i > 0).astype(F32)
        keep_n = (i < R.nrow - 1).astype(F32)
        w = w_ref[...]
        ext = jnp.concatenate([xp_ref[...].astype(F32) * keep_p, x_ref[...].astype(F32),
                               xn_ref[...].astype(F32)], axis=0)
        sh = _shifts_causal(ext, 4, tm + HALO)
        pre = _wsum(w, sh) + b_ref[...]
        s = _sigmoid(pre)
        dsilu = s * (1.0 + pre * (1.0 - s))
        dext = jnp.concatenate([d_ref[...].astype(F32), dn_ref[...].astype(F32) * keep_n], axis=0)
        dpre = dext * dsilu
        dsh = _shifts_anticausal(dpre, 4, tm)
        o_ref[...] = _wsum(w, dsh).astype(BF16)
        dp = dpre[:tm]
        _acc_rows(dw_ref, [_colsum(dp * q[:tm]) for q in sh], i == 0)
        _acc_out(db_ref, _colsum(dp), i == 0)

    return R.call(
        body, "xbc_bwd", DX // _XW,
        [R.tile(_XW), R.next(_XW), R.tile(_XW, cb), R.prev(_XW, cb), R.next(_XW, cb),
         R.colvec(4, _XW), R.colvec(1, _XW), ANY],
        [R.tile(_XW, cb), R.colvec(4, _XW), R.colvec(1, _XW)],
        [jax.ShapeDtypeStruct(dproj.shape, BF16), jax.ShapeDtypeStruct((4, DX), F32),
         jax.ShapeDtypeStruct((1, DX), F32)],
        (dact, dact, proj, proj, proj, conv_w, conv_b, dproj), aliases={7: 0})


def _softplus(x):
    return jnp.maximum(x, 0.0) + jnp.log(1.0 + jnp.exp(-jnp.abs(x)))


def _dt_rows(T):
    return min(8 * CH, T // 2)


def _dt_fwd(dt_raw, dt_bias_p, a_log_p):
    T = dt_raw.shape[0]
    rows = _dt_rows(T)

    def body(r_ref, b_ref, al_ref, dt_ref, ac_ref, acT_ref):
        dt = _softplus(r_ref[...] + b_ref[...])
        s = dt * (-jnp.exp(al_ref[...]))
        row = lax.broadcasted_iota(jnp.int32, (rows, LANES), 0) % CH
        k = 1
        while k < CH:
            s = s + jnp.where(row >= k, pltpu.roll(s, k, 0), 0.0)
            k *= 2
        dt_ref[...] = dt
        ac_ref[...] = s
        for q in range(0, rows, CH):
            acT_ref[q:q + CH] = s[q:q + CH].T

    blk = pl.BlockSpec((rows, LANES), lambda i: (i, 0))
    vec = pl.BlockSpec((1, LANES), lambda i: (0, 0))
    return pl.pallas_call(
        body, name="dt_fwd", grid=(T // rows,), in_specs=[blk, vec, vec], out_specs=[blk, blk, blk],
        out_shape=[jax.ShapeDtypeStruct((T, LANES), F32)] * 3, compiler_params=_params(("parallel",)),
    )(dt_raw, dt_bias_p, a_log_p)


def _dt_bwd(dacum, ddt_x, dt_raw, dt_bias_p, a_log_p, dproj):
    T = dt_raw.shape[0]
    rows = _dt_rows(T)
    nc = T // rows

    def body(da_ref, dx_ref, r_ref, b_ref, al_ref, _alias, o_ref, db_ref, dal_ref):
        i = pl.program_id(0)
        a = -jnp.exp(al_ref[...])
        z = r_ref[...] + b_ref[...]
        dt = _softplus(z)
        s = da_ref[...]
        row = lax.broadcasted_iota(jnp.int32, (rows, LANES), 0) % CH
        k = 1
        while k < CH:
            s = s + jnp.where(row < CH - k, pltpu.roll(s, rows - k, 0), 0.0)
            k *= 2
        ddt = s * a + dx_ref[...]
        draw = ddt * _sigmoid(z)
        o_ref[:, :LANES] = draw.astype(BF16)
        o_ref[:, LANES:] = jnp.zeros((rows, NIP - OFF_DT - LANES), BF16)
        _acc_out(db_ref, _colsum(draw), i == 0)
        _acc_out(dal_ref, _colsum(s * dt), i == 0)

        @pl.when(i == nc - 1)
        def _():
            dal_ref[...] = dal_ref[...] * a

    blk = pl.BlockSpec((rows, LANES), lambda i: (i, 0))
    vec = pl.BlockSpec((1, LANES), lambda i: (0, 0))
    oblk = pl.BlockSpec((rows, NIP - OFF_DT), lambda i: (i, OFF_DT // (NIP - OFF_DT)))
    return pl.pallas_call(
        body, name="dt_bwd", grid=(nc,), in_specs=[blk, blk, blk, vec, vec, ANY], out_specs=[oblk, vec, vec],
        out_shape=[jax.ShapeDtypeStruct(dproj.shape, BF16), jax.ShapeDtypeStruct((1, LANES), F32),
                   jax.ShapeDtypeStruct((1, LANES), F32)],
        input_output_aliases={5: 0}, compiler_params=_params(("arbitrary",)),
    )(dacum, ddt_x, dt_raw, dt_bias_p, a_log_p, dproj)


_GW = DI // NG
_HG = NH // NG
_NEG = -1e30


def _interleave(gens):
    out, live = [None] * len(gens), list(range(len(gens)))
    while live:
        for i in list(live):
            try:
                next(gens[i])
            except StopIteration as stop:
                out[i] = stop.value
                live.remove(i)
    return out


def _pair_lanes(left, v0, v1):
    return jnp.where(left, v0, v1)


def _hi_lo(v):
    hi = v.astype(BF16)
    return jnp.concatenate([hi, (v - hi.astype(F32)).astype(BF16)], axis=1)


def _head_spread():
    row = lax.broadcasted_iota(jnp.int32, (2 * LANES, 1), 0) % LANES
    return (row == lax.broadcasted_iota(jnp.int32, (1, DI), 1) // HP).astype(BF16)


def _ssd_specs(T, rev):
    nc = T // CH
    cm = (lambda c: nc - 1 - c) if rev else (lambda c: c)
    bw = NG * NS
    return dict(
        xs=pl.BlockSpec((CH, DI), lambda c: (cm(c), 0)),
        bm=pl.BlockSpec((CH, bw), lambda c: (cm(c), DI // bw)),
        cmat=pl.BlockSpec((CH, bw), lambda c: (cm(c), DI // bw + 1)),
        xbc=pl.BlockSpec((CH, DX), lambda c: (cm(c), 0)),
        col=pl.BlockSpec((CH, LANES), lambda c: (cm(c), 0)),
        dsk=pl.BlockSpec((1, DI), lambda c: (0, 0)),
        state=pl.BlockSpec((1, NS, DI), lambda c: (cm(c), 0, 0)),
    )


def _last(ref, lo, hi):
    return ref.at[(slice(None),) * (len(ref.shape) - 1) + (slice(lo, hi),)]


def _group_views(g, wide, narrow):
    return [_last(r, g * _GW, (g + 1) * _GW) for r in wide] + [_last(r, g * NS, (g + 1) * NS) for r in narrow]


def _ssd_fwd(xact, dt, acum, acumT, dsk_rep, proj, norm_w):
    T = xact.shape[0]
    nc = T // CH
    sp = _ssd_specs(T, False)

    def body(*refs):
        xs, bm, cmat, dtr, acr, actr, dsk, zr, nw, spread_ref, y, yn, spv, S_ref = refs

        @pl.when(pl.program_id(0) == 0)
        def _():
            S_ref[...] = jnp.zeros_like(S_ref)

        ac = acr[...]
        cols = [_hi_lo(v) for v in (dtr[...], jnp.exp(ac), jnp.exp(ac[CH - 1:CH, :] - ac))]
        _interleave([group(g * _HG, cols, ac, actr[...], _last(spread_ref, g * _GW, (g + 1) * _GW),
                           *_group_views(g, (xs, dsk, zr, nw, y, yn, spv, S_ref), (bm, cmat))) for g in range(NG)])

    def group(hb, cols, ac, acT, spread_ref, xs_ref, dsk_ref, z_ref, nw_ref, y_ref, yn_ref, sp_ref, S_ref, b_ref, c_ref):
        dtl, eal, dtel = (_dot(v, spread_ref[...]) for v in cols)
        Bm, Cm = b_ref[...], c_ref[...]
        S = S_ref[...]
        sp_ref[0] = S
        cb = _dot(Cm, Bm, NT)
        CS = _dot(Cm, S.astype(BF16))
        row = lax.broadcasted_iota(jnp.int32, (CH, CH), 0)
        col = lax.broadcasted_iota(jnp.int32, (CH, CH), 1)
        tril = row >= col
        left = col < HP
        xd_parts = []
        for p in range(_HG // 2):
            sl = slice(p * LANES, (p + 1) * LANES)
            j0, j1 = hb + 2 * p, hb + 2 * p + 1
            xp = xs_ref[:, sl].astype(F32)
            a0, a1 = ac[:, j0:j0 + 1], ac[:, j1:j1 + 1]
            X = xp * dtl[:, sl]
            Xb = X.astype(BF16)
            Ws = [(cb * jnp.exp(jnp.where(tril, aj - acT[j:j + 1, :], _NEG))).astype(BF16)
                  for j, aj in ((j0, a0), (j1, a1))]
            Xs = [jnp.where(m, Xb, jnp.zeros_like(Xb)) for m in (left, jnp.logical_not(left))]
            yield
            yd = _dot(jnp.concatenate(Ws, axis=1), jnp.concatenate(Xs, axis=0))
            yield
            y = yd + eal[:, sl] * CS[:, sl] + dsk_ref[:, sl] * xp
            y_ref[:, sl] = y.astype(BF16)
            xd_parts.append(X * dtel[:, sl])
        Xd = jnp.concatenate(xd_parts, axis=1).astype(BF16)
        S_ref[...] = eal[CH - 1:CH, :] * S + _dot(Bm, Xd, TN)
        yield
        z = z_ref[...].astype(F32)
        yf = y_ref[...].astype(F32) * z * _sigmoid(z)
        r = lax.rsqrt(jnp.mean(yf * yf, axis=-1, keepdims=True) + EPS)
        yn_ref[...] = (yf * r * nw_ref[...]).astype(BF16)

    zspec = pl.BlockSpec((CH, DI), lambda c: (c, OFF_Z // DI))
    return pl.pallas_call(
        body, name="ssd_fwd", grid=(nc,),
        in_specs=[sp["xs"], sp["bm"], sp["cmat"], sp["col"], sp["col"], sp["col"], sp["dsk"], zspec, sp["dsk"],
                  pl.BlockSpec((2 * LANES, DI), lambda c: (0, 0))],
        out_specs=[sp["xs"], sp["xs"], sp["state"]],
        out_shape=[jax.ShapeDtypeStruct((T, DI), BF16), jax.ShapeDtypeStruct((T, DI), BF16),
                   jax.ShapeDtypeStruct((nc, NS, DI), F32)],
        scratch_shapes=[pltpu.VMEM((NS, DI), F32)],
        compiler_params=_params(("arbitrary",)),
    )(xact, xact, xact, dt, acum, acumT, dsk_rep, proj, norm_w, _head_spread())


def _ssd_bwd(dn, y, proj, norm_w, dproj, xact, dt, acum, acumT, dsk_rep, sprev):
    T = xact.shape[0]
    nc = T // CH
    sp = _ssd_specs(T, True)

    def body(*refs):
        (xs, bm, cmat, dtr, acr, actr, dsk, dnr, yr, zr, nw, spv, _alias, lanes_of_ref, rows_of_ref, spread_ref,
         dxa, ddtx, dAc, dskacc, dzr, dnw, dS_ref) = refs
        first = pl.program_id(0) == 0

        @pl.when(first)
        def _():
            dS_ref[...] = jnp.zeros_like(dS_ref)

        dbc = _last(dxa, DI, DX)
        ddtx_sum = jnp.zeros((CH, LANES), F32)
        dAc_sum = jnp.zeros((CH, LANES), F32)
        ac = acr[...]
        cols = [_hi_lo(v) for v in (dtr[...], jnp.exp(ac), jnp.exp(ac[CH - 1:CH, :] - ac))]
        for a, b in _interleave([group(first, g * _HG, cols, ac, actr[...],
                                       lanes_of_ref.at[g * _GW:(g + 1) * _GW],
                                       rows_of_ref.at[g * _HG * CH:(g + 1) * _HG * CH],
                                       _last(spread_ref, g * _GW, (g + 1) * _GW),
                                       *_group_views(g, (xs, dsk, dnr, yr, zr, nw, dzr, dnw, spv, dxa, dskacc, dS_ref),
                                                     (bm, cmat, dbc, _last(dbc, NG * NS, 2 * NG * NS))))
                                 for g in range(NG)]):
            ddtx_sum, dAc_sum = ddtx_sum + a, dAc_sum + b
        ddtx[...] = ddtx_sum
        dAc[...] = dAc_sum

    def group(first, hb, cols, ac, acT, lanes_of_ref, rows_of_ref, spread_ref, xs_ref, dsk_ref, dn_ref, y_ref, z_ref,
              nw_ref, dz_ref, dnw_ref, sp_ref, dx_ref, dskacc_ref, dS_ref, b_ref, c_ref, dB_ref, dC_ref):
        z = z_ref[...].astype(F32)
        yv = y_ref[...].astype(F32)
        sg = _sigmoid(z)
        silu = z * sg
        yf = yv * silu
        rn = lax.rsqrt(jnp.mean(yf * yf, axis=-1, keepdims=True) + EPS)
        yh = yf * rn
        dnv = dn_ref[...].astype(F32)
        dyh = dnv * nw_ref[...]
        dyf = rn * (dyh - yh * jnp.mean(dyh * yh, axis=-1, keepdims=True))
        dyg = dyf * silu
        dz_ref[...] = (dyf * yv * sg * (1.0 + z * (1.0 - sg))).astype(BF16)
        _acc_out(dnw_ref, _colsum(dnv * yh), first)
        Bm, Cm = b_ref[...], c_ref[...]
        S = sp_ref[0]
        dS = dS_ref[...]
        Sb, dSb = S.astype(BF16), dS.astype(BF16)
        cb = _dot(Cm, Bm, NT)
        cbT = _dot(Bm, Cm, NT)
        CmT = Cm.T
        CS = _dot(Cm, Sb)
        T1 = _dot(Bm, dSb)
        yield
        row = lax.broadcasted_iota(jnp.int32, (CH, CH), 0)
        col = lax.broadcasted_iota(jnp.int32, (CH, CH), 1)
        tril = row >= col
        triu = row <= col
        left = col < HP
        lastrow = lax.broadcasted_iota(jnp.int32, (CH, 1), 0) == CH - 1
        dCB = jnp.zeros((CH, CH), F32)
        dCBT = jnp.zeros((CH, CH), F32)
        xd_parts, dye_parts, dsk_parts, dxx_parts, gr_parts, end_parts, qd_parts = ([] for _ in range(7))
        dtls, eals, dtels = (_dot(v, spread_ref[...]) for v in cols)
        for p in range(_HG // 2):
            sl = slice(p * LANES, (p + 1) * LANES)
            j0, j1 = hb + 2 * p, hb + 2 * p + 1
            xp = xs_ref[:, sl].astype(F32)
            dyp = dyg[:, sl]
            a0, a1 = ac[:, j0:j0 + 1], ac[:, j1:j1 + 1]
            dtl, eal, dtel = dtls[:, sl], eals[:, sl], dtels[:, sl]
            X = xp * dtl
            Xb = X.astype(BF16)
            T1d = T1[:, sl] * dtel
            Rm = T1d * X
            decp = eal[CH - 1:CH, :]
            gr_parts.append(dyp * (eal * CS[:, sl]) - Rm)
            end_parts.append(Rm + decp * (dS[:, sl] * S[:, sl]))
            dXd = jnp.zeros((CH, LANES), F32)
            for j, aj, mask in ((j0, a0, left), (j1, a1, jnp.logical_not(left))):
                dYm = jnp.where(mask, dyp, 0.0).astype(BF16)
                dWm = _dot(dYm, Xb, NT)
                dWmT = _dot(Xb, dYm, NT)
                yield
                e = aj - acT[j:j + 1, :]
                P = dWm * jnp.exp(jnp.where(tril, e, _NEG))
                LmT = jnp.exp(jnp.where(triu, -e, _NEG))
                PT = dWmT * LmT
                dCB = dCB + P
                dCBT = dCBT + PT
                yield
                dXd = dXd + _dot((cbT * LmT).astype(BF16), dYm)
                qd_parts.append((P * cb - PT * cbT).astype(BF16))
                yield
            dX = dXd + T1d
            dxx_parts.append(dX * xp)
            dx_ref[:, sl] = (dX * dtl + dsk_ref[:, sl] * dyp).astype(BF16)
            dsk_parts.append(_colsum(dyp * xp))
            xd_parts.append(X * dtel)
            dye_parts.append(dyp * eal)
            yield
        Xd = jnp.concatenate(xd_parts, axis=1).astype(BF16)
        dYe = jnp.concatenate(dye_parts, axis=1).astype(BF16)
        def lane_sums(parts):
            return _dot(jnp.concatenate(parts, axis=1).astype(BF16), lanes_of_ref[...])
        ddtx = lane_sums(dxx_parts)
        dAc = (_dot(jnp.concatenate(qd_parts, axis=1), rows_of_ref[...]) + lane_sums(gr_parts)
               + jnp.where(lastrow, _colsum(lane_sums(end_parts)), 0.0))
        dC_ref[...] = (_dot(dCB.astype(BF16), Bm) + _dot(dYe, Sb, NT)).astype(BF16)
        dB_ref[...] = (_dot(dCBT.astype(BF16), Cm) + _dot(Xd, dSb, NT)).astype(BF16)
        dS_ref[...] = _dot(CmT, dYe) + eals[CH - 1:CH, :] * dS
        _acc_out(dskacc_ref, jnp.concatenate(dsk_parts, axis=1), first)
        return ddtx, dAc

    zspec = pl.BlockSpec((CH, DI), lambda c: (nc - 1 - c, OFF_Z // DI))
    head = lax.broadcasted_iota(jnp.int32, (1, LANES), 1)
    lanes_of = (lax.broadcasted_iota(jnp.int32, (DI, 1), 0) // HP == head).astype(BF16)
    rows_of = (lax.broadcasted_iota(jnp.int32, (NH * CH, 1), 0) // CH == head).astype(BF16)
    return pl.pallas_call(
        body, name="ssd_bwd", grid=(nc,),
        in_specs=[sp["xs"], sp["bm"], sp["cmat"], sp["col"], sp["col"], sp["col"], sp["dsk"], sp["xs"], sp["xs"],
                  zspec, sp["dsk"], sp["state"], ANY, pl.BlockSpec(lanes_of.shape, lambda c: (0, 0)),
                  pl.BlockSpec(rows_of.shape, lambda c: (0, 0)), pl.BlockSpec((2 * LANES, DI), lambda c: (0, 0))],
        out_specs=[sp["xbc"], sp["col"], sp["col"], sp["dsk"], zspec, sp["dsk"]],
        out_shape=[jax.ShapeDtypeStruct((T, DX), BF16), jax.ShapeDtypeStruct((T, LANES), F32),
                   jax.ShapeDtypeStruct((T, LANES), F32), jax.ShapeDtypeStruct((1, DI), F32),
                   jax.ShapeDtypeStruct(dproj.shape, BF16), jax.ShapeDtypeStruct((1, DI), F32)],
        scratch_shapes=[pltpu.VMEM((NS, DI), F32)], input_output_aliases={12: 4},
        compiler_params=_params(("arbitrary",)),
    )(xact, xact, xact, dt, acum, acumT, dsk_rep, dn, y, proj, norm_w, sprev, dproj, lanes_of, rows_of, _head_spread())


def _merge_fwd_epilogue(proj, ya):
    T = proj.shape[0]

    def fn(ysv, ins, outs, first):
        g_ref, ya_ref = ins
        m_ref, ys_ref = outs
        ga = _sigmoid(g_ref[:, :D].astype(F32))
        gs = _sigmoid(g_ref[:, D:].astype(F32))
        m_ref[...] = (ga * ya_ref[...].astype(F32) + gs * ysv).astype(BF16)
        ys_ref[...] = ysv.astype(BF16)

    return _Epilogue(fn, (proj, ya), (((T, D), BF16), ((T, D), BF16)), 10 * D, in_windows={0: (OFF_G, 2 * D)})


def _merge_bwd_epilogue(proj, ya, ys, ncols):
    T = proj.shape[0]

    def fn(d, ins, outs, first):
        g_ref, ya_ref, ys_ref = ins
        dg_ref, dya_ref, dys_ref = outs
        ga = _sigmoid(g_ref[:, :D].astype(F32))
        gs = _sigmoid(g_ref[:, D:].astype(F32))
        dya_ref[...] = (d * ga).astype(BF16)
        dys_ref[...] = (d * gs).astype(BF16)
        dg_ref[:, :D] = (d * ya_ref[...].astype(F32) * ga * (1.0 - ga)).astype(BF16)
        dg_ref[:, D:] = (d * ys_ref[...].astype(F32) * gs * (1.0 - gs)).astype(BF16)

    window = (OFF_G, 2 * D)
    return _Epilogue(fn, (proj, ya, ys), (((T, ncols), BF16), ((T, D), BF16), ((T, D), BF16)), 16 * D,
                     in_windows={0: window}, out_windows={0: window})


_FW = 1408
_FB = FF // _FW


def _ffn_act_fwd(hv, conv_w, conv_b):
    T = hv.shape[0]
    R = _Rows(T, 256)
    tm = R.tm

    def body(h1_ref, h1p_ref, h3_ref, w_ref, b_ref, o_ref):
        keep = (pl.program_id(1) > 0).astype(F32)
        ext = jnp.concatenate([h1p_ref[...].astype(F32) * keep, h1_ref[...].astype(F32)], axis=0)
        pre = _wsum(w_ref[...], _shifts_causal(ext, 3, tm)) + b_ref[...]
        o_ref[...] = (pre * _sigmoid(pre) * h3_ref[...].astype(F32)).astype(BF16)

    return R.call(body, "ffn_act_fwd", _FB,
                  [R.tile(_FW), R.prev(_FW), R.tile(_FW, _FB), R.colvec(3, _FW), R.colvec(1, _FW)],
                  R.tile(_FW), jax.ShapeDtypeStruct((T, FF), BF16), (hv, hv, hv, conv_w, conv_b))


def _ffn_act_bwd(dg, hv, conv_w, conv_b):
    T = hv.shape[0]
    R = _Rows(T, 256)
    tm = R.tm

    def body(dg_ref, dgn_ref, h1_ref, h1p_ref, h1n_ref, h3_ref, h3n_ref, w_ref, b_ref, dh3_ref, dh1_ref, dw_ref,
             db_ref):
        i = pl.program_id(1)
        keep_p = (i > 0).astype(F32)
        keep_n = (i < R.nrow - 1).astype(F32)
        w = w_ref[...]
        ext = jnp.concatenate([h1p_ref[...].astype(F32) * keep_p, h1_ref[...].astype(F32),
                               h1n_ref[...].astype(F32)], axis=0)
        sh = _shifts_causal(ext, 3, tm + HALO)
        pre = _wsum(w, sh) + b_ref[...]
        s = _sigmoid(pre)
        d = jnp.concatenate([dg_ref[...].astype(F32), dgn_ref[...].astype(F32) * keep_n], axis=0)
        h3 = jnp.concatenate([h3_ref[...].astype(F32), h3n_ref[...].astype(F32)], axis=0)
        dh3_ref[...] = (d[:tm] * pre[:tm] * s[:tm]).astype(BF16)
        dpre = d * h3 * s * (1.0 + pre * (1.0 - s))
        dh1_ref[...] = _wsum(w, _shifts_anticausal(dpre, 3, tm)).astype(BF16)
        dp = dpre[:tm]
        _acc_rows(dw_ref, [_colsum(dp * q[:tm]) for q in sh], i == 0)
        _acc_out(db_ref, _colsum(dp), i == 0)

    return R.call(
        body, "ffn_act_bwd", _FB,
        [R.tile(_FW), R.next(_FW), R.tile(_FW), R.prev(_FW), R.next(_FW), R.tile(_FW, _FB), R.next(_FW, _FB),
         R.colvec(3, _FW), R.colvec(1, _FW)],
        [R.tile(_FW), R.tile(_FW), R.colvec(3, _FW), R.colvec(1, _FW)],
        [jax.ShapeDtypeStruct((T, FF), BF16), jax.ShapeDtypeStruct((T, FF), BF16),
         jax.ShapeDtypeStruct((3, FF), F32), jax.ShapeDtypeStruct((1, FF), F32)],
        (dg, dg, hv, hv, hv, hv, hv, conv_w, conv_b))


def _final_loss_epilogue(w, target):
    T = target.shape[0]

    def fn(xv, ins, outs, first):
        w_ref, t_ref = ins
        l_ref, dh_ref, dhb_ref, dw_ref = outs
        wv = w_ref[...]
        r = lax.rsqrt(jnp.mean(xv * xv, axis=-1, keepdims=True) + EPS)
        xh = xv * r
        err = xh * wv - t_ref[...]
        part = 0.5 * jnp.sum(jnp.mean(err * err, axis=-1, keepdims=True), axis=0, keepdims=True)
        _acc_out(l_ref, jnp.broadcast_to(part, l_ref.shape), first)
        dy = err * (1.0 / D)
        dxh = dy * wv
        dh = r * (dxh - xh * jnp.mean(dxh * xh, axis=-1, keepdims=True))
        dh_ref[...] = dh
        dhb_ref[...] = dh.astype(BF16)
        _acc_out(dw_ref, _colsum(dy * xh), first)

    return _Epilogue(fn, (w, target),
                     (((8, LANES), F32), ((T, D), F32), ((T, D), BF16), ((1, D), F32)), 10 * D)


def _pad_lanes(v, n=LANES):
    return jnp.pad(v, ((0, 0), (0, n - v.shape[1])))


class _Hooks:
    def before_in_proj(self, w_in):
        return w_in

    def late_weights(self, wts, after):
        return wts

    def grads_ready(self, grads, tie):
        return tie

    def mark(self, name, value):
        return value


def _local_step(x, target, wts, hooks=None):
    hooks = hooks or _Hooks()
    T = x.shape[0]
    w_in = wts["w_in"]
    dt_bias_p, a_log_p = _pad_lanes(wts["dt_bias"]), _pad_lanes(wts["a_log"])
    dsk_rep = jnp.repeat(wts["d_skip"], HP, axis=1)

    w_in = hooks.before_in_proj(w_in)
    proj, u, dt_raw = _norm_matmul(x, wts["norm_mix_w"], w_in, "norm_mm_in", w_in[:, OFF_DT:OFF_DT + LANES])
    ya_in = _branch_a_fwd(proj, wts["conv_a_w"])
    xact = _xbc_fwd(proj, wts["ssd_conv_w"], wts["ssd_conv_b"])
    dt, acum, acumT = _dt_fwd(dt_raw, dt_bias_p, a_log_p)
    y_ssd, yn, sprev = _ssd_fwd(xact, dt, acum, acumT, dsk_rep, proj, wts["ssd_norm_w"])
    late = hooks.late_weights(wts, yn)
    w_a_out, w_s_out, w_o, w_up, w_down = (late[k] for k in ("w_a_out", "w_s_out", "w_o", "w_up", "w_down"))
    y_a = _matmul(ya_in, w_a_out, mode="nn", out_dtype=BF16, name="mm_a_out")
    merged, y_s = _matmul(yn, w_s_out, mode="nn", out_dtype=BF16, name="mm_s_out_merge",
                          epilogue=_merge_fwd_epilogue(proj, y_a))
    h1 = _matmul(merged, w_o, mode="nn", out_dtype=F32, name="mm_o", residual=x)
    hv, v = _norm_matmul(h1, wts["norm_ffn_w"], w_up, "norm_mm_up")
    gact = _ffn_act_fwd(hv, wts["ffn_conv_w"], wts["ffn_conv_b"])
    loss, dh2, dh2b, g_final = _matmul(gact, w_down, mode="nn", out_dtype=F32, name="mm_down_loss", residual=h1,
                                       epilogue=_final_loss_epilogue(wts["final_norm_w"], target))

    grads = {"final_norm_w": g_final}
    grads["w_down"] = _matmul(gact, dh2b, mode="tn", out_dtype=BF16, name="mm_down_dw")
    dgact = _matmul(dh2b, w_down, mode="nt", out_dtype=BF16, name="mm_down_dx")
    dh3, dh1c, grads["ffn_conv_w"], grads["ffn_conv_b"] = _ffn_act_bwd(dgact, hv, wts["ffn_conv_w"], wts["ffn_conv_b"])
    grads["w_up"] = (_matmul(v, dh1c, mode="tn", out_dtype=BF16, name="mm_up_dw1"),
                     _matmul(v, dh3, mode="tn", out_dtype=BF16, name="mm_up_dw3"))
    dv = _matmul(dh1c, w_up, mode="nt", out_dtype=F32, name="mm_up_dx1")
    dh1, dh1b, grads["norm_ffn_w"] = _matmul(
        dh3, w_up, mode="nt", out_dtype=F32, name="mm_up_dx3_norm", residual=dv, b_k_off=FF,
        epilogue=_rmsnorm_bwd_epilogue(h1, wts["norm_ffn_w"], dh2))
    grads["w_o"] = _matmul(merged, dh1b, mode="tn", out_dtype=BF16, name="mm_o_dw")
    dproj, dya, dys = _matmul(dh1b, w_o, mode="nt", out_dtype=BF16, name="mm_o_dx_merge",
                              epilogue=_merge_bwd_epilogue(proj, y_a, y_s, NIP))
    grads["w_a_out"] = _matmul(ya_in, dya, mode="tn", out_dtype=BF16, name="mm_a_out_dw")
    dya_in = _matmul(dya, w_a_out, mode="nt", out_dtype=BF16, name="mm_a_out_dx")
    dproj, grads["conv_a_w"] = _branch_a_bwd(dya_in, proj, wts["conv_a_w"], dproj)
    grads["w_s_out"] = _matmul(yn, dys, mode="tn", out_dtype=BF16, name="mm_s_out_dw")
    dys = hooks.grads_ready({k: grads[k] for k in ("w_a_out", "w_s_out", "w_o", "w_up", "w_down")}, dys)
    dyn =_matmul(dys, w_s_out, mode="nt", out_dtype=BF16, name="mm_s_out_dx")
    dxact, ddt_x, dacum, dskl, dproj, grads["ssd_norm_w"] = _ssd_bwd(
        dyn, y_ssd, proj, wts["ssd_norm_w"], dproj, xact, dt, acum, acumT, dsk_rep, sprev)
    dxact = hooks.mark("ssd_bwd", dxact)
    grads["d_skip"] = dskl.reshape(NH, HP).sum(axis=1).reshape(1, NH)
    dproj, grads["ssd_conv_w"], grads["ssd_conv_b"] = _xbc_bwd(dxact, proj, wts["ssd_conv_w"], wts["ssd_conv_b"], dproj)
    dproj, g_dtb, g_alog = _dt_bwd(dacum, ddt_x, dt_raw, dt_bias_p, a_log_p, dproj)
    grads["dt_bias"], grads["a_log"] = g_dtb[:, :NH], g_alog[:, :NH]
    grads["w_in"] = _matmul(u, dproj, mode="tn", out_dtype=BF16, name="mm_in_dw")
    dproj = hooks.grads_ready({"w_in": grads["w_in"]}, dproj)
    grad_x, _, grads["norm_mix_w"] = _matmul(dproj, w_in, mode="nt", out_dtype=F32, name="mm_in_dx_norm",
                                             epilogue=_rmsnorm_bwd_epilogue(x, wts["norm_mix_w"], dh1))
    return loss, grad_x, grads


def _permute_w_in(slabs):
    cs = slabs.shape[2]
    pieces = []
    for o, n, no in sorted(_SEGS, key=lambda seg: seg[2]):
        for s in range(slabs.shape[0]):
            lo, hi = max(o, s * cs), min(o + n, (s + 1) * cs)
            if lo < hi:
                pieces.append(slabs[s][:, lo - s * cs:hi - s * cs])
    pieces.append(jnp.zeros((slabs.shape[1], NIP - OFF_DT - _SEGS[-1][1]), slabs.dtype))
    return jnp.concatenate(pieces, axis=1)


def _unpermute_w_in(g):
    cs = NI // NCHIP
    slabs = []
    for s in range(NCHIP):
        pieces = []
        for o, n, no in sorted(_SEGS):
            lo, hi = max(o, s * cs), min(o + n, (s + 1) * cs)
            if lo < hi:
                pieces.append(g[:, no + lo - o:no + hi - o])
        slabs.append(jnp.concatenate(pieces, axis=1))
    return jnp.stack(slabs)


MESH = pl.DeviceIdType.MESH
NCHIP = 4
NDEV = 8

_W_IN = (("w_in", D, NI // NCHIP, 1),)
_W_REST = (("w_a_out", D // NCHIP, D, 0), ("w_s_out", DI // NCHIP, D, 0), ("w_o", D // NCHIP, D, 0),
           ("w_up", D, 2 * FF // NCHIP, 1), ("w_down", FF // NCHIP, D, 0))


def _coords():
    return lax.axis_index("x"), lax.axis_index("y"), lax.axis_index("c")


def _other_chips(x, y):
    return [(1 - x, y), (x, 1 - y), (1 - x, 1 - y)]


def _ag_weights(shard):
    nrows = shard.shape[0]
    hr = nrows // 2

    def body(x_ref, out_ref, send_sems, recv_sems, local_sem):
        x, y, c = _coords()
        me = 2 * x + y
        chips = _other_chips(x, y)

        def rows(s, h):
            return out_ref.at[s, pl.ds(h * hr, hr), :]

        def copy(k, s, h, to, src=None):
            return pltpu.make_async_remote_copy(
                src_ref=rows(s, h) if src is None else src, dst_ref=rows(s, h),
                send_sem=send_sems.at[k], recv_sem=recv_sems.at[k], device_id=to, device_id_type=MESH)

        mine = pltpu.make_async_copy(x_ref, out_ref.at[me], local_sem)
        mine.start()
        first = [copy(k, me, c, (*chip, c), src=x_ref.at[pl.ds(c * hr, hr), :]) for k, chip in enumerate(chips)]
        for cp in first:
            cp.start()
        passed = []
        for k, chip in enumerate(chips):
            s = 2 * chip[0] + chip[1]
            copy(k, s, c, (x, y, c)).wait_recv()
            fwd = copy(3 + k, s, c, (x, y, 1 - c))
            fwd.start()
            passed.append(fwd)
        for k, chip in enumerate(chips):
            copy(3 + k, 2 * chip[0] + chip[1], 1 - c, (x, y, c)).wait_recv()
        for cp in first + passed:
            cp.wait_send()
        mine.wait()

    return pl.pallas_call(
        body, name="ag_weights", in_specs=[ANY], out_specs=ANY,
        out_shape=jax.ShapeDtypeStruct((NCHIP,) + shard.shape, shard.dtype),
        scratch_shapes=[pltpu.SemaphoreType.DMA((6,)), pltpu.SemaphoreType.DMA((6,)), pltpu.SemaphoreType.DMA],
        compiler_params=pltpu.CompilerParams(has_side_effects=True),
    )(shard)


HBM = pl.BlockSpec(memory_space=pltpu.HBM)
SEM = pl.BlockSpec(memory_space=pltpu.SEMAPHORE)
_EFFECT = pltpu.SideEffectType.DATAFLOW_SIDE_EFFECTING
_NCOPY = NCHIP - 1


def _plan_bcast(src_ref, land_ref, send_sems, recv_sems, base):
    x, y, c = _coords()
    sends, lands = [], []
    for k, chip in enumerate(_other_chips(x, y)):
        def copy(slot):
            return pltpu.make_async_remote_copy(
                src_ref=src_ref, dst_ref=land_ref.at[slot], send_sem=send_sems.at[base + k],
                recv_sem=recv_sems.at[base + k], device_id=(*chip, c), device_id_type=MESH)
        sends.append(copy(2 * x + y))
        lands.append(copy(2 * chip[0] + chip[1]))
    return sends, lands


def _plan_scatter(src_ref, land_ref, send_sems, recv_sems, base):
    x, y, c = _coords()
    cps = [pltpu.make_async_remote_copy(
        src_ref=src_ref.at[2 * chip[0] + chip[1]], dst_ref=land_ref.at[k], send_sem=send_sems.at[base + k],
        recv_sem=recv_sems.at[base + k], device_id=(*chip, c), device_id_type=MESH)
        for k, chip in enumerate(_other_chips(x, y))]
    return cps, cps


def _plan_swap(src_ref, land_ref, send_sems, recv_sems, base):
    x, y, c = _coords()
    cp = pltpu.make_async_remote_copy(
        src_ref=src_ref, dst_ref=land_ref, send_sem=send_sems.at[base], recv_sem=recv_sems.at[base],
        device_id=(x, y, 1 - c), device_id_type=MESH)
    return [cp], [cp]


def _plan_all(plan, refs, n):
    sends, lands = [], []
    for t in range(n):
        s, l = plan(refs[t], refs[n + t], refs[2 * n], refs[2 * n + 1], t * _NCOPY)
        sends += s
        lands += l
    return sends, lands


def _split_start(name, srcs, lands, plan):
    n = len(srcs)

    def body(*refs):
        for cp in _plan_all(plan, refs, n)[0]:
            cp.start()
        refs[-1][...] = jnp.zeros_like(refs[-1])

    arrays = list(srcs) + list(lands)
    outs = pl.pallas_call(
        body, name=name,
        out_shape=(pltpu.SemaphoreType.DMA((n * _NCOPY,)), pltpu.SemaphoreType.DMA((n * _NCOPY,)),
                   *[pltpu.HBM(a.shape, a.dtype) for a in arrays], jax.ShapeDtypeStruct((8, LANES), F32)),
        in_specs=(HBM,) * (2 * n),
        out_specs=(SEM, SEM) + (HBM,) * (2 * n) + (pl.BlockSpec(memory_space=pltpu.VMEM),),
        input_output_aliases={t: 2 + t for t in range(2 * n)},
        compiler_params=pltpu.CompilerParams(has_side_effects=_EFFECT),
    )(*[pltpu.with_memory_space_constraint(a, pltpu.HBM) for a in arrays])
    return (outs[0], outs[1], tuple(outs[2:2 + 2 * n])), outs[-1]


def _split_wait(name, handle, after, plan):
    send_sems, recv_sems, arrays = handle
    n = len(arrays) // 2

    def body(*refs):
        sends, lands = _plan_all(plan, refs[:2 * n] + refs[2 * n:2 * n + 2], n)
        for cp in sends:
            cp.wait_send()
        for cp in lands:
            cp.wait_recv()

    outs = pl.pallas_call(
        body, name=name, out_shape=tuple(pltpu.HBM(a.shape, a.dtype) for a in arrays),
        in_specs=(HBM,) * (2 * n) + (SEM, SEM, ANY), out_specs=(HBM,) * (2 * n),
        input_output_aliases={t: t for t in range(2 * n)},
        compiler_params=pltpu.CompilerParams(has_side_effects=_EFFECT),
    )(*arrays, send_sems, recv_sems, after)
    return outs[:n], outs[n:]


def _tie(x, token, name):
    def body(x_ref, t_ref, o_ref):
        pass

    return pl.pallas_call(
        body, name=name, in_specs=[ANY, pl.BlockSpec(memory_space=pltpu.VMEM)], out_specs=ANY,
        out_shape=jax.ShapeDtypeStruct(x.shape, x.dtype), input_output_aliases={0: 0},
    )(x, token)


_ADD_BYTES = 7 << 19


def _add_tile(rows, cols):
    best = 32
    for t in range(32, rows + 1, 32):
        if rows % t == 0 and t * cols * 4 <= _ADD_BYTES:
            best = t
    return best


def _add_slabs(pack, land, me, name):
    rows, cols = pack.shape[1:]
    tr = _add_tile(rows, cols)

    def body(me_ref, p_ref, l_ref, o_ref):
        f = lambda r: r.astype(F32)
        o_ref[...] = ((f(p_ref[0]) + f(l_ref[0])) + f(l_ref[1])) + f(l_ref[2])

    return pl.pallas_call(
        body, name=name,
        grid_spec=pltpu.PrefetchScalarGridSpec(
            num_scalar_prefetch=1, grid=(rows // tr,),
            in_specs=[pl.BlockSpec((1, tr, cols), lambda i, me_ref: (me_ref[0], i, 0)),
                      pl.BlockSpec((_NCOPY, tr, cols), lambda i, me_ref: (0, i, 0))],
            out_specs=pl.BlockSpec((tr, cols), lambda i, me_ref: (i, 0))),
        out_shape=jax.ShapeDtypeStruct((rows, cols), F32),
        compiler_params=_params(("parallel",)),
    )(me, pack, land)


_STAGE_W = 1024


def _stage_rows(shapes):
    pieces, r = [], 0
    for i, (k, w) in enumerate(shapes):
        for a in range(k):
            for q in range(0, w, _STAGE_W):
                pieces.append((i, a, q, min(_STAGE_W, w - q), r))
                r += 1
    return pieces, -(-r // 8) * 8


def _gather8(parts, reduce, name):
    shapes = [p.shape for p in parts]
    pieces, rows = _stage_rows(shapes)
    n = len(parts)

    def body(*refs):
        ins, outs = refs[:n], refs[n:2 * n]
        stage, buf, res, send_sems, recv_sems = refs[2 * n:]
        x, y, c = _coords()
        me = 4 * x + 2 * y + c
        stage[...] = jnp.zeros_like(stage)
        for i, a, q, w, r in pieces:
            stage[r:r + 1, 0:w] = ins[i][a:a + 1, q:q + w]
        buf[pl.ds(me, 1)] = stage[...][None]
        cps, lands = [], []
        for k in range(1, NDEV):
            peer = (1 - x if k & 4 else x, 1 - y if k & 2 else y, 1 - c if k & 1 else c)

            def copy(slot):
                return pltpu.make_async_remote_copy(
                    src_ref=stage, dst_ref=buf.at[slot], send_sem=send_sems.at[k - 1],
                    recv_sem=recv_sems.at[k - 1], device_id=peer, device_id_type=MESH)

            cps.append(copy(me))
            lands.append(copy(4 * peer[0] + 2 * peer[1] + peer[2]))
        for cp in cps:
            cp.start()
        for cp, land in zip(cps, lands):
            land.wait_recv()
            cp.wait_send()
        if reduce:
            acc = buf[0]
            for d in range(1, NDEV):
                acc = acc + buf[d]
            res[...] = acc
            for i, a, q, w, r in pieces:
                outs[i][a:a + 1, q:q + w] = res[r:r + 1, 0:w]
        else:
            for i, a, q, w, r in pieces:
                for s in range(NCHIP):
                    outs[i][s, a:a + 1, q:q + w] = buf[2 * s, r:r + 1, 0:w]

    vm = pl.BlockSpec(memory_space=pltpu.VMEM)
    out_shapes = [jax.ShapeDtypeStruct(s if reduce else (NCHIP,) + s, F32) for s in shapes]
    return pl.pallas_call(
        body, name=name, in_specs=[vm] * n, out_specs=[vm] * n, out_shape=out_shapes,
        scratch_shapes=[pltpu.VMEM((rows, _STAGE_W), F32), pltpu.VMEM((NDEV, rows, _STAGE_W), F32),
                        pltpu.VMEM((rows, _STAGE_W), F32), pltpu.SemaphoreType.DMA((NDEV - 1,)),
                        pltpu.SemaphoreType.DMA((NDEV - 1,))],
        compiler_params=pltpu.CompilerParams(has_side_effects=True),
    )(*parts)


def _adamw_update(w_ref, g_ref, m_ref, v_ref, d_ref, mo_ref, vo_ref):
    c1 = 1.0 / (1.0 - ADAM_B1 ** ADAM_STEP)
    c2 = 1.0 / (1.0 - ADAM_B2 ** ADAM_STEP)
    gv = g_ref[...]
    mn = ADAM_B1 * m_ref[...] + (1.0 - ADAM_B1) * gv
    vn = ADAM_B2 * v_ref[...] + (1.0 - ADAM_B2) * (gv * gv)
    d_ref[...] = -ADAM_LR * ((mn * c1) / (jnp.sqrt(vn * c2) + ADAM_EPS) + ADAM_WD * w_ref[...])
    mo_ref[...] = mn
    vo_ref[...] = vn


def _adamw_small(ws, gs, ms, vs):
    n = len(ws)

    def body(*refs):
        for i in range(n):
            _adamw_update(*(refs[j * n + i] for j in range(7)))

    vm = pl.BlockSpec(memory_space=pltpu.VMEM)
    outs = pl.pallas_call(
        body, name="adamw_small", in_specs=[vm] * (4 * n), out_specs=[vm] * (3 * n),
        out_shape=[jax.ShapeDtypeStruct(w.shape, F32) for w in ws] * 3,
    )(*ws, *gs, *ms, *vs)
    return outs[:n], outs[n:2 * n], outs[2 * n:]


def _adamw(w, g_parts, m, v, name):
    rows, cols = w.shape
    tr = rows
    while tr * cols * 4 > (1 << 21) and tr % 16 == 0:
        tr //= 2

    def body(w_ref, ga_ref, gb_ref, m_ref, v_ref, g_ref, d_ref, mo_ref, vo_ref):
        g_ref[...] = ga_ref[...] + gb_ref[...]
        _adamw_update(w_ref, g_ref, m_ref, v_ref, d_ref, mo_ref, vo_ref)

    blk = pl.BlockSpec((tr, cols), lambda i: (i, 0))
    return pl.pallas_call(
        body, name=name, grid=(rows // tr,), in_specs=[blk] * 5, out_specs=[blk] * 4,
        out_shape=[jax.ShapeDtypeStruct((rows, cols), F32)] * 4, compiler_params=_params(("parallel",)),
    )(w, *g_parts, m, v)


def _by_chip(g, rr, cc, axis):
    if isinstance(g, tuple):
        n = NCHIP // len(g)
        return jnp.concatenate([h.reshape(rr, n, cc).transpose(1, 0, 2) for h in g], axis=0)
    return g.reshape(NCHIP, rr, cc) if axis == 0 else g.reshape(rr, NCHIP, cc).transpose(1, 0, 2)


_SMALL_REPL = ("norm_mix_w", "ssd_conv_b", "dt_bias", "a_log", "d_skip", "ssd_norm_w", "norm_ffn_w",
               "ffn_conv_b", "final_norm_w")
_SMALL_CONV = (("conv_a_w", 3, D), ("ssd_conv_w", 4, DX), ("ffn_conv_w", 3, FF))


def kernel(x, norm_mix_w, w_in, conv_a_w, w_a_out, ssd_conv_w, ssd_conv_b, dt_bias, a_log, d_skip, ssd_norm_w, w_s_out, w_o, norm_ffn_w, w_up, ffn_conv_w, ffn_conv_b, w_down, final_norm_w, loss_target, m_norm_mix_w, m_w_in, m_conv_a_w, m_w_a_out, m_ssd_conv_w, m_ssd_conv_b, m_dt_bias, m_a_log, m_d_skip, m_ssd_norm_w, m_w_s_out, m_w_o, m_norm_ffn_w, m_w_up, m_ffn_conv_w, m_ffn_conv_b, m_w_down, m_final_norm_w, v_norm_mix_w, v_w_in, v_conv_a_w, v_w_a_out, v_ssd_conv_w, v_ssd_conv_b, v_dt_bias, v_a_log, v_d_skip, v_ssd_norm_w, v_w_s_out, v_w_o, v_norm_ffn_w, v_w_up, v_ffn_conv_w, v_ffn_conv_b, v_w_down, v_final_norm_w):
    names = ("norm_mix_w", "w_in", "conv_a_w", "w_a_out", "ssd_conv_w", "ssd_conv_b", "dt_bias", "a_log", "d_skip",
             "ssd_norm_w", "w_s_out", "w_o", "norm_ffn_w", "w_up", "ffn_conv_w", "ffn_conv_b", "w_down", "final_norm_w")
    W = dict(zip(names, (norm_mix_w, w_in, conv_a_w, w_a_out, ssd_conv_w, ssd_conv_b, dt_bias, a_log, d_skip,
                         ssd_norm_w, w_s_out, w_o, norm_ffn_w, w_up, ffn_conv_w, ffn_conv_b, w_down, final_norm_w)))
    M = dict(zip(names, (m_norm_mix_w, m_w_in, m_conv_a_w, m_w_a_out, m_ssd_conv_w, m_ssd_conv_b, m_dt_bias, m_a_log,
                         m_d_skip, m_ssd_norm_w, m_w_s_out, m_w_o, m_norm_ffn_w, m_w_up, m_ffn_conv_w, m_ffn_conv_b,
                         m_w_down, m_final_norm_w)))
    V = dict(zip(names, (v_norm_mix_w, v_w_in, v_conv_a_w, v_w_a_out, v_ssd_conv_w, v_ssd_conv_b, v_dt_bias, v_a_log,
                         v_d_skip, v_ssd_norm_w, v_w_s_out, v_w_o, v_norm_ffn_w, v_w_up, v_ffn_conv_w, v_ffn_conv_b,
                         v_w_down, v_final_norm_w)))
    two_d = lambda a: a.reshape(-1, a.shape[-1])
    W2, M2, V2 = ({k: two_d(a) for k, a in t.items()} for t in (W, M, V))
    xi, yi, ci = _coords()
    me = 2 * xi + yi

    meidx = me.reshape(1).astype(jnp.int32)
    state = {}


    class Hooks(_Hooks):
        def before_in_proj(self, w_in):
            return _tie(w_in, state["rest_token"], "tie_ag_rest")

        def late_weights(self, wts, after):
            owns, lands = _split_wait("ag_rest_wait", state["rest"], after, _plan_bcast)
            full = {}
            for (n, rr, cc, axis), own, land in zip(_W_REST, owns, lands):
                slabs = lax.dynamic_update_slice(land, own[None], (me, 0, 0))
                full[n] = slabs.reshape(NCHIP * rr, cc) if axis == 0 else slabs.transpose(1, 0, 2).reshape(rr, NCHIP * cc)
            return {**wts, **full}

        def grads_ready(self, grads, tie):
            if "w_in" in grads:
                key, packs = "g_in", [_unpermute_w_in(grads["w_in"])]
            else:
                key = "g_rest"
                packs = [_by_chip(grads[n], rr, cc, axis)
                         for n, rr, cc, axis in _W_REST]
            lands = [lax.empty((_NCOPY,) + p.shape[1:], BF16) for p in packs]
            state[key], token = _split_start("rs_" + key + "_start", packs, lands, _plan_scatter)
            return _tie(tie, token, "tie_" + key)

        def mark(self, name, value):
            return _tie(value, reduced("g_rest", value, _W_REST), "tie_g_rest_swap")

    def reduced(key, after, group):
        packs, lands = _split_wait("rs_" + key + "_wait", state[key], after, _plan_scatter)
        mines = [_add_slabs(p, l, meidx, "rs_add_chips_" + n) for (n, *_), p, l in zip(group, packs, lands)]
        state[key + "_swap"], token = _split_start(
            "rs_" + key + "_swap_start", mines, [lax.empty(m.shape, F32) for m in mines], _plan_swap)
        return token

    def swapped(key, after, group):
        mines, theirs = _split_wait("rs_" + key + "_swap_wait", state[key + "_swap"], after, _plan_swap)
        return dict(zip([n for n, *_ in group], zip(mines, theirs)))

    w_in_slabs = _ag_weights(W2["w_in"].astype(BF16))
    wts = {k: W2[k] for k in _SMALL_REPL}
    conv_by_chip = _gather8([W2[n] for n, *_ in _SMALL_CONV], False, "ag_conv_weights")
    for (n, kk, width), stacked in zip(_SMALL_CONV, conv_by_chip):
        wts[n] = stacked.transpose(1, 0, 2).reshape(kk, width)
    rest = [W2[n].astype(BF16) for n, *_ in _W_REST]
    rest[0] = _tie(rest[0], conv_by_chip[0], "tie_ag_order")
    state["rest"], state["rest_token"] = _split_start(
        "ag_rest_start", rest, [lax.empty((NCHIP,) + r.shape, BF16) for r in rest], _plan_bcast)
    wts["w_in"] = _permute_w_in(w_in_slabs)

    loss8, grad_x, grads = _local_step(x[0], loss_target[0], wts, Hooks())

    reduced("g_in", grad_x, _W_IN)

    small_parts = [grads[n] for n in _SMALL_REPL] + [loss8[0:1]] + [grads[n] for n, *_ in _SMALL_CONV]
    small_g = _gather8(small_parts, True, "allreduce_small")
    gsm = dict(zip(_SMALL_REPL, small_g[:len(_SMALL_REPL)]))
    loss = small_g[len(_SMALL_REPL)][0, 0]
    for (n, kk, width), gfull in zip(_SMALL_CONV, small_g[len(_SMALL_REPL) + 1:]):
        cw = width // NCHIP
        gsm[n] = lax.dynamic_slice(gfull, (0, me * cw), (kk, cw))

    G, DW, NM, NV = {}, {}, {}, {}
    gbig = swapped("g_rest", grad_x, _W_REST)
    for n in [b[0] for b in _W_REST]:
        G[n], DW[n], NM[n], NV[n] = _adamw(W2[n], gbig[n], M2[n], V2[n], "adamw_" + n)
    gbig = swapped("g_in", DW[_W_REST[-1][0]], _W_IN)
    for n in [b[0] for b in _W_IN]:
        G[n], DW[n], NM[n], NV[n] = _adamw(W2[n], gbig[n], M2[n], V2[n], "adamw_" + n)
    sm_names = list(_SMALL_REPL) + [n for n, *_ in _SMALL_CONV]
    outs = _adamw_small(*([t[n] for n in sm_names] for t in (W2, gsm, M2, V2)))
    for t, vals in zip((DW, NM, NV), outs):
        t.update(zip(sm_names, vals))
    G.update(gsm)

    def shaped(t):
        return [t[n].reshape(W[n].shape) for n in names]

    return (loss, grad_x.reshape(x.shape), *shaped(G), *shaped(DW), *shaped(NM), *shaped(NV))
```

```python
import jax
import jax.numpy as jnp
from jax import lax
from jax.experimental import pallas as pl
from jax.experimental.pallas import tpu as pltpu

F32 = jnp.float32
BF16 = jnp.bfloat16

D = 1024
DI = 2048
NH = 32
HP = 64
NG = 4
NS = 128
CH = 128
DX = 3072
FF = 2816
NI = 10272
EPS = 1e-5

OFF_BCV, OFF_XBC, OFF_G, OFF_Z, OFF_DT = 0, 3072, 6144, 8192, 10240
NIP = 10752
_SEGS = ((0, 2048, OFF_G), (2048, 3072, OFF_BCV), (5120, 2048, OFF_Z), (7168, 3072, OFF_XBC), (10240, 32, OFF_DT))

LANES = 128
HALO = 16
V7X_VMEM_LIMIT = 56 * 2 ** 20

ADAM_LR, ADAM_B1, ADAM_B2, ADAM_EPS, ADAM_WD, ADAM_STEP = 0.001, 0.9, 0.999, 1e-08, 0.01, 10

NN = (((1,), (0,)), ((), ()))
NT = (((1,), (1,)), ((), ()))
TN = (((0,), (0,)), ((), ()))


def _dot(a, b, dims=NN):
    return lax.dot_general(a, b, dims, preferred_element_type=F32)


def _params(sem, **kw):
    return pltpu.CompilerParams(dimension_semantics=sem, vmem_limit_bytes=V7X_VMEM_LIMIT, **kw)


V7X_MXU = 256
V7X_HBM_BYTES_PER_S = 3.5e12
STEP_S = 0.35e-6
MATMUL_VMEM = 40 * 2 ** 20
EPILOGUE_VMEM = 46 * 2 ** 20


ACC_BYTES_PER_S = 1.2e13


def _divisors(dim, cap, units):
    for unit in units:
        c = [t for t in range(unit, min(dim, cap) + 1, unit) if dim % t == 0]
        if c:
            return c
    return [dim]


def _tiles(M, N, K, out_bytes, has_res):
    best = None
    for tn in _divisors(N, 2816, (V7X_MXU, LANES)):
        for tm in _divisors(M, 2816, (LANES,)):
            for tk in _divisors(K, 2816, (V7X_MXU, LANES)):
                nk, ni, nj = K // tk, M // tm, N // tn
                vmem = 4 * (tm * tk + tk * tn) + 2 * tm * tn * out_bytes
                vmem += (4 * tm * tn if nk > 1 else 0) + (8 * tm * tn if has_res else 0)
                if vmem > MATMUL_VMEM:
                    continue
                a_reads = M * K * 2 * (nj if nk > 1 else 1)
                b_reads = K * N * 2 * (ni if nk * nj > 1 else 1)
                cost = (a_reads + b_reads + M * N * out_bytes) / V7X_HBM_BYTES_PER_S + ni * nj * nk * STEP_S
                cost += (nk - 1) * M * N * 8 / ACC_BYTES_PER_S
                if best is None or cost < best[0]:
                    best = (cost, tm, tn, tk)
    assert best is not None, (M, N, K)
    return best[1:]


def _sigmoid(x):
    return 1.0 / (1.0 + jnp.exp(-x))


class _Epilogue:
    def __init__(self, fn, ins, outs, tile_bytes, in_windows=None, out_windows=None):
        self.fn, self.ins, self.outs, self.tile_bytes = fn, tuple(ins), tuple(outs), tile_bytes
        self.in_windows, self.out_windows = in_windows or {}, out_windows or {}


def _matmul(a, b, *, mode, out_dtype, name, residual=None, b_k_off=0, epilogue=None):
    if mode == "nn":
        (M, K), (K2, N) = a.shape, b.shape
    elif mode == "nt":
        (M, K), (N, K2) = a.shape, (b.shape[0], a.shape[1])
        assert b_k_off + K <= b.shape[1]
    else:
        (K, M), (K2, N) = a.shape, b.shape
    assert K == K2, (name, a.shape, b.shape)
    tm, tn, tk = _tiles(M, N, K, jnp.dtype(out_dtype).itemsize, residual is not None)
    if epilogue is not None:
        tn = N
        fits = [(K * N * 2 * (M // t) / V7X_HBM_BYTES_PER_S + (K // q - 1) * M * N * 8 / ACC_BYTES_PER_S
                 + (M // t) * (K // q) * STEP_S, t, q)
                for t in (1024, 512, 256) if M % t == 0 for q in _divisors(K, 2816, (V7X_MXU, LANES))
                if 4 * (t * q + q * tn) + (4 * t * tn if K > q else 0) + (8 * t * tn if residual is not None else 0)
                + 2 * t * epilogue.tile_bytes <= EPILOGUE_VMEM]
        _, tm, tk = min(fits)
    nk = K // tk
    if mode == "tn":
        a_spec = pl.BlockSpec((tk, tm), lambda i, j, k: (k, i))
    else:
        a_spec = pl.BlockSpec((tm, tk), lambda i, j, k: (i, k))
    if mode == "nt":
        assert b_k_off % tk == 0
        b_spec = pl.BlockSpec((tn, tk), lambda i, j, k: (j, k + b_k_off // tk))
    else:
        b_spec = pl.BlockSpec((tk, tn), lambda i, j, k: (k, j))
    dims = {"nn": NN, "nt": NT, "tn": TN}[mode]
    o_spec = pl.BlockSpec((tm, tn), lambda i, j, k: (i, j))
    has_res = residual is not None

    def rows_or_whole(shape, window=None):
        if window is not None:
            off, width = window
            return pl.BlockSpec((tm, width), lambda i, j, k: (i, off // width))
        if shape[0] == M:
            return pl.BlockSpec((tm,) + tuple(shape[1:]), lambda i, j, k: (i,) + (0,) * (len(shape) - 1))
        return pl.BlockSpec(tuple(shape), lambda i, j, k: (0,) * len(shape))

    n_in = 2 + has_res + (len(epilogue.ins) if epilogue else 0)
    n_out = len(epilogue.outs) if epilogue else 1

    def body(*refs):
        a_ref, b_ref = refs[:2]
        r_ref = refs[2] if has_res else None
        out_refs = refs[n_in:n_in + n_out]
        acc_ref = refs[-1]
        k = pl.program_id(2)
        part = _dot(a_ref[...], b_ref[...], dims)

        def finish(r):
            if has_res:
                r = r + r_ref[...].astype(F32)
            if epilogue is None:
                out_refs[0][...] = r.astype(out_dtype)
            else:
                epilogue.fn(r, refs[2 + has_res:n_in], out_refs, pl.program_id(0) == 0)

        if nk == 1:
            finish(part)
            return

        @pl.when(k == 0)
        def _():
            acc_ref[...] = part

        @pl.when(jnp.logical_and(k > 0, k < nk - 1))
        def _():
            acc_ref[...] += part

        @pl.when(k == nk - 1)
        def _():
            finish(acc_ref[...] + part)

    in_specs = [a_spec, b_spec] + ([o_spec] if has_res else [])
    args = (a, b) + ((residual,) if has_res else ())
    if epilogue is None:
        out_specs, out_shape = o_spec, jax.ShapeDtypeStruct((M, N), out_dtype)
        sem = ("parallel", "parallel", "arbitrary")
    else:
        in_specs += [rows_or_whole(x.shape, epilogue.in_windows.get(n)) for n, x in enumerate(epilogue.ins)]
        args += epilogue.ins
        out_specs = [rows_or_whole(o[0], epilogue.out_windows.get(n)) for n, o in enumerate(epilogue.outs)]
        out_shape = [jax.ShapeDtypeStruct(shp, dt) for shp, dt in epilogue.outs]
        sem = ("arbitrary", "arbitrary", "arbitrary")
    return pl.pallas_call(
        body, name=name, grid=(M // tm, N // tn, nk), in_specs=in_specs, out_specs=out_specs,
        out_shape=out_shape, scratch_shapes=[pltpu.VMEM((tm, tn), F32)] if nk > 1 else [],
        compiler_params=_params(sem),
    )(*args)


class _Rows:
    def __init__(self, T, tm):
        self.T, self.tm = T, min(tm, T // 2)
        self.nrow = T // self.tm
        self.r = self.tm // HALO
        self.nb = T // HALO

    def tile(self, w, cb=0, step=1):
        return pl.BlockSpec((self.tm, w), lambda j, i: (i, cb + step * j))

    def prev(self, w, cb=0, step=1):
        r = self.r
        return pl.BlockSpec((HALO, w), lambda j, i: (jnp.maximum(i * r - 1, 0), cb + step * j))

    def next(self, w, cb=0, step=1):
        r, nb = self.r, self.nb
        return pl.BlockSpec((HALO, w), lambda j, i: (jnp.minimum((i + 1) * r, nb - 1), cb + step * j))

    def colvec(self, k, w, cb=0, step=1):
        return pl.BlockSpec((k, w), lambda j, i: (0, cb + step * j))

    def call(self, body, name, ncol, in_specs, out_specs, out_shape, args, aliases=None):
        return pl.pallas_call(
            body, name=name, grid=(ncol, self.nrow), in_specs=in_specs, out_specs=out_specs,
            out_shape=out_shape, input_output_aliases=aliases or {},
            compiler_params=_params(("parallel", "arbitrary")),
        )(*args)


ANY = pl.BlockSpec(memory_space=pl.ANY)


def _shifts_causal(ext, nk, tm):
    out = []
    for k in range(nk):
        s = nk - 1 - k
        r = ext if s == 0 else pltpu.roll(ext, s, 0)
        out.append(r[HALO:])
    return out


def _shifts_anticausal(ext, nk, tm):
    n = ext.shape[0]
    out = []
    for k in range(nk):
        s = nk - 1 - k
        r = ext if s == 0 else pltpu.roll(ext, n - s, 0)
        out.append(r[:tm])
    return out


def _wsum(w, parts):
    acc = w[0:1, :] * parts[0]
    for k in range(1, len(parts)):
        acc = acc + w[k:k + 1, :] * parts[k]
    return acc


def _colsum(x):
    return jnp.sum(x, axis=0, keepdims=True)


def _acc_out(ref, val, first):
    @pl.when(first)
    def _():
        ref[...] = val

    @pl.when(jnp.logical_not(first))
    def _():
        ref[...] += val


def _acc_rows(ref, rows, first):
    for k, r in enumerate(rows):
        _acc_out(ref.at[k:k + 1, :], r, first)


def _norm_matmul(x, wn, b, name, b_f32=None):
    T, N = x.shape[0], b.shape[1]
    tm = min(1024, T)
    tn = max(t for t in _divisors(N, 2816, (V7X_MXU, LANES))
             if 8 * tm * D + 6 * tm * D + 4 * D * t + 4 * tm * t <= MATMUL_VMEM)

    extra = b_f32 is not None

    def body(*refs):
        x_ref, wn_ref, b_ref = refs[:3]
        o_ref, u_ref = refs[3 + extra:5 + extra]
        keep_ref = refs[-1]

        @pl.when(pl.program_id(1) == 0)
        def _():
            xv = x_ref[...]
            r = lax.rsqrt(jnp.mean(xv * xv, axis=-1, keepdims=True) + EPS)
            u = (xv * r * wn_ref[...]).astype(BF16)
            keep_ref[...] = u
            u_ref[...] = u
            if extra:
                refs[5 + extra][...] = _dot(u, refs[3][...])

        o_ref[...] = _dot(keep_ref[...], b_ref[...]).astype(BF16)

    rows = pl.BlockSpec((tm, D), lambda i, j: (i, 0))
    whole = lambda shape: pl.BlockSpec(shape, lambda i, j: (0, 0))
    narrow = pl.BlockSpec((tm, LANES), lambda i, j: (i, 0))
    return pl.pallas_call(
        body, name=name, grid=(T // tm, N // tn),
        in_specs=[rows, whole((1, D)), pl.BlockSpec((D, tn), lambda i, j: (0, j))] + [whole((D, LANES))] * extra,
        out_specs=[pl.BlockSpec((tm, tn), lambda i, j: (i, j)), rows] + [narrow] * extra,
        out_shape=[jax.ShapeDtypeStruct((T, N), BF16), jax.ShapeDtypeStruct((T, D), BF16)]
        + [jax.ShapeDtypeStruct((T, LANES), F32)] * extra,
        scratch_shapes=[pltpu.VMEM((tm, D), BF16)],
        compiler_params=_params(("parallel", "arbitrary")),
    )(*((x, wn, b) + ((b_f32,) if extra else ())))


def _rmsnorm_bwd_epilogue(x, w, dres):
    T = x.shape[0]

    def fn(dyv, ins, outs, first):
        x_ref, w_ref, dr_ref = ins
        dx_ref, dxb_ref, dw_ref = outs
        xv = x_ref[...]
        r = lax.rsqrt(jnp.mean(xv * xv, axis=-1, keepdims=True) + EPS)
        xh = xv * r
        dxh = dyv * w_ref[...]
        dx = r * (dxh - xh * jnp.mean(dxh * xh, axis=-1, keepdims=True)) + dr_ref[...]
        dx_ref[...] = dx
        dxb_ref[...] = dx.astype(BF16)
        _acc_out(dw_ref, _colsum(dyv * xh), first)

    return _Epilogue(fn, (x, w, dres), (((T, D), F32), ((T, D), BF16), ((1, D), F32)), 14 * D)


def _branch_a_fwd(proj, conv_w):
    T = proj.shape[0]
    R = _Rows(T, 512)
    tm = R.tm

    def body(p_ref, pp_ref, w_ref, o_ref):
        keep = (pl.program_id(1) > 0).astype(F32)
        cv = p_ref[:, D:2 * D].astype(F32) * p_ref[:, 2 * D:].astype(F32)
        cvp = pp_ref[:, D:2 * D].astype(F32) * pp_ref[:, 2 * D:].astype(F32) * keep
        sh = _shifts_causal(jnp.concatenate([cvp, cv], axis=0), 3, tm)
        ca = _wsum(w_ref[...], sh)
        o_ref[...] = (p_ref[:, :D].astype(F32) * ca).astype(BF16)

    return R.call(body, "branch_a_fwd", 1, [R.tile(3 * D), R.prev(3 * D), R.colvec(3, D)], R.tile(D),
                  jax.ShapeDtypeStruct((T, D), BF16), (proj, proj, conv_w))


def _branch_a_bwd(dya_in, proj, conv_w, dproj):
    T = proj.shape[0]
    R = _Rows(T, 256)
    tm = R.tm

    def body(d_ref, dn_ref, p_ref, pp_ref, pn_ref, w_ref, _alias, o_ref, dw_ref):
        i = pl.program_id(1)
        keep_p = (i > 0).astype(F32)
        keep_n = (i < R.nrow - 1).astype(F32)
        w = w_ref[...]
        b = p_ref[:, :D].astype(F32)
        c = p_ref[:, D:2 * D].astype(F32)
        v = p_ref[:, 2 * D:].astype(F32)
        cvp = pp_ref[:, D:2 * D].astype(F32) * pp_ref[:, 2 * D:].astype(F32) * keep_p
        sh = _shifts_causal(jnp.concatenate([cvp, c * v], axis=0), 3, tm)
        ca = _wsum(w, sh)
        d = d_ref[...].astype(F32)
        dca = d * b
        dca_n = dn_ref[...].astype(F32) * pn_ref[:, :D].astype(F32) * keep_n
        dsh = _shifts_anticausal(jnp.concatenate([dca, dca_n], axis=0), 3, tm)
        dcv = _wsum(w, dsh)
        o_ref[:, :D] = (d * ca).astype(BF16)
        o_ref[:, D:2 * D] = (dcv * v).astype(BF16)
        o_ref[:, 2 * D:] = (dcv * c).astype(BF16)
        _acc_rows(dw_ref, [_colsum(dca * s) for s in sh], i == 0)

    return R.call(
        body, "branch_a_bwd", 1,
        [R.tile(D), R.next(D), R.tile(3 * D), R.prev(3 * D), R.next(3 * D), R.colvec(3, D), ANY],
        [R.tile(3 * D), R.colvec(3, D)],
        [jax.ShapeDtypeStruct(dproj.shape, BF16), jax.ShapeDtypeStruct((3, D), F32)],
        (dya_in, dya_in, proj, proj, proj, conv_w, dproj), aliases={6: 0})


_XW = 512


def _xbc_fwd(proj, conv_w, conv_b):
    T = proj.shape[0]
    R = _Rows(T, 512)
    tm = R.tm
    cb = OFF_XBC // _XW

    def body(x_ref, xp_ref, w_ref, b_ref, o_ref):
        keep = (pl.program_id(1) > 0).astype(F32)
        ext = jnp.concatenate([xp_ref[...].astype(F32) * keep, x_ref[...].astype(F32)], axis=0)
        pre = _wsum(w_ref[...], _shifts_causal(ext, 4, tm)) + b_ref[...]
        o_ref[...] = (pre * _sigmoid(pre)).astype(BF16)

    return R.call(body, "xbc_fwd", DX // _XW,
                  [R.tile(_XW, cb), R.prev(_XW, cb), R.colvec(4, _XW), R.colvec(1, _XW)], R.tile(_XW),
                  jax.ShapeDtypeStruct((T, DX), BF16), (proj, proj, conv_w, conv_b))


def _xbc_bwd(dact, proj, conv_w, conv_b, dproj):
    T = proj.shape[0]
    R = _Rows(T, 512)
    tm = R.tm
    cb = OFF_XBC // _XW

    def body(d_ref, dn_ref, x_ref, xp_ref, xn_ref, w_ref, b_ref, _alias, o_ref, dw_ref, db_ref):
        i = pl.program_id(1)
        keep_p = (i > 0).astype(F32)
        keep_n = (i < R.nrow - 1).astype(F32)
        w = w_ref[...]
        ext = jnp.concatenate([xp_ref[...].astype(F32) * keep_p, x_ref[...].astype(F32),
                               xn_ref[...].astype(F32)], axis=0)
        sh = _shifts_causal(ext, 4, tm + HALO)
        pre = _wsum(w, sh) + b_ref[...]
        s = _sigmoid(pre)
        dsilu = s * (1.0 + pre * (1.0 - s))
        dext = jnp.concatenate([d_ref[...].astype(F32), dn_ref[...].astype(F32) * keep_n], axis=0)
        dpre = dext * dsilu
        dsh = _shifts_anticausal(dpre, 4, tm)
        o_ref[...] = _wsum(w, dsh).astype(BF16)
        dp = dpre[:tm]
        _acc_rows(dw_ref, [_colsum(dp * q[:tm]) for q in sh], i == 0)
        _acc_out(db_ref, _colsum(dp), i == 0)

    return R.call(
        body, "xbc_bwd", DX // _XW,
        [R.tile(_XW), R.next(_XW), R.tile(_XW, cb), R.prev(_XW, cb), R.next(_XW, cb),
         R.colvec(4, _XW), R.colvec(1, _XW), ANY],
        [R.tile(_XW, cb), R.colvec(4, _XW), R.colvec(1, _XW)],
        [jax.ShapeDtypeStruct(dproj.shape, BF16), jax.ShapeDtypeStruct((4, DX), F32),
         jax.ShapeDtypeStruct((1, DX), F32)],
        (dact, dact, proj, proj, proj, conv_w, conv_b, dproj), aliases={7: 0})


def _softplus(x):
    return jnp.maximum(x, 0.0) + jnp.log(1.0 + jnp.exp(-jnp.abs(x)))


def _dt_rows(T):
    return min(8 * CH, T // 2)


def _dt_fwd(dt_raw, dt_bias_p, a_log_p):
    T = dt_raw.shape[0]
    rows = _dt_rows(T)

    def body(r_ref, b_ref, al_ref, dt_ref, ac_ref, acT_ref):
        dt = _softplus(r_ref[...] + b_ref[...])
        s = dt * (-jnp.exp(al_ref[...]))
        row = lax.broadcasted_iota(jnp.int32, (rows, LANES), 0) % CH
        k = 1
        while k < CH:
            s = s + jnp.where(row >= k, pltpu.roll(s, k, 0), 0.0)
            k *= 2
        dt_ref[...] = dt
        ac_ref[...] = s
        for q in range(0, rows, CH):
            acT_ref[q:q + CH] = s[q:q + CH].T

    blk = pl.BlockSpec((rows, LANES), lambda i: (i, 0))
    vec = pl.BlockSpec((1, LANES), lambda i: (0, 0))
    return pl.pallas_call(
        body, name="dt_fwd", grid=(T // rows,), in_specs=[blk, vec, vec], out_specs=[blk, blk, blk],
        out_shape=[jax.ShapeDtypeStruct((T, LANES), F32)] * 3, compiler_params=_params(("parallel",)),
    )(dt_raw, dt_bias_p, a_log_p)


def _dt_bwd(dacum, ddt_x, dt_raw, dt_bias_p, a_log_p, dproj):
    T = dt_raw.shape[0]
    rows = _dt_rows(T)
    nc = T // rows

    def body(da_ref, dx_ref, r_ref, b_ref, al_ref, _alias, o_ref, db_ref, dal_ref):
        i = pl.program_id(0)
        a = -jnp.exp(al_ref[...])
        z = r_ref[...] + b_ref[...]
        dt = _softplus(z)
        s = da_ref[...]
        row = lax.broadcasted_iota(jnp.int32, (rows, LANES), 0) % CH
        k = 1
        while k < CH:
            s = s + jnp.where(row < CH - k, pltpu.roll(s, rows - k, 0), 0.0)
            k *= 2
        ddt = s * a + dx_ref[...]
        draw = ddt * _sigmoid(z)
        o_ref[:, :LANES] = draw.astype(BF16)
        o_ref[:, LANES:] = jnp.zeros((rows, NIP - OFF_DT - LANES), BF16)
        _acc_out(db_ref, _colsum(draw), i == 0)
        _acc_out(dal_ref, _colsum(s * dt), i == 0)

        @pl.when(i == nc - 1)
        def _():
            dal_ref[...] = dal_ref[...] * a

    blk = pl.BlockSpec((rows, LANES), lambda i: (i, 0))
    vec = pl.BlockSpec((1, LANES), lambda i: (0, 0))
    oblk = pl.BlockSpec((rows, NIP - OFF_DT), lambda i: (i, OFF_DT // (NIP - OFF_DT)))
    return pl.pallas_call(
        body, name="dt_bwd", grid=(nc,), in_specs=[blk, blk, blk, vec, vec, ANY], out_specs=[oblk, vec, vec],
        out_shape=[jax.ShapeDtypeStruct(dproj.shape, BF16), jax.ShapeDtypeStruct((1, LANES), F32),
                   jax.ShapeDtypeStruct((1, LANES), F32)],
        input_output_aliases={5: 0}, compiler_params=_params(("arbitrary",)),
    )(dacum, ddt_x, dt_raw, dt_bias_p, a_log_p, dproj)


_GW = DI // NG
_HG = NH // NG
_NEG = -1e30


def _interleave(gens):
    out, live = [None] * len(gens), list(range(len(gens)))
    while live:
        for i in list(live):
            try:
                next(gens[i])
            except StopIteration as stop:
                out[i] = stop.value
                live.remove(i)
    return out


def _pair_lanes(left, v0, v1):
    return jnp.where(left, v0, v1)


def _hi_lo(v):
    hi = v.astype(BF16)
    return jnp.concatenate([hi, (v - hi.astype(F32)).astype(BF16)], axis=1)


def _head_spread():
    row = lax.broadcasted_iota(jnp.int32, (2 * LANES, 1), 0) % LANES
    return (row == lax.broadcasted_iota(jnp.int32, (1, DI), 1) // HP).astype(BF16)


def _ssd_specs(T, rev):
    nc = T // CH
    cm = (lambda c: nc - 1 - c) if rev else (lambda c: c)
    bw = NG * NS
    return dict(
        xs=pl.BlockSpec((CH, DI), lambda c: (cm(c), 0)),
        bm=pl.BlockSpec((CH, bw), lambda c: (cm(c), DI // bw)),
        cmat=pl.BlockSpec((CH, bw), lambda c: (cm(c), DI // bw + 1)),
        xbc=pl.BlockSpec((CH, DX), lambda c: (cm(c), 0)),
        col=pl.BlockSpec((CH, LANES), lambda c: (cm(c), 0)),
        dsk=pl.BlockSpec((1, DI), lambda c: (0, 0)),
        state=pl.BlockSpec((1, NS, DI), lambda c: (cm(c), 0, 0)),
    )


def _last(ref, lo, hi):
    return ref.at[(slice(None),) * (len(ref.shape) - 1) + (slice(lo, hi),)]


def _group_views(g, wide, narrow):
    return [_last(r, g * _GW, (g + 1) * _GW) for r in wide] + [_last(r, g * NS, (g + 1) * NS) for r in narrow]


def _ssd_fwd(xact, dt, acum, acumT, dsk_rep, proj, norm_w):
    T = xact.shape[0]
    nc = T // CH
    sp = _ssd_specs(T, False)

    def body(*refs):
        xs, bm, cmat, dtr, acr, actr, dsk, zr, nw, spread_ref, y, yn, spv, S_ref = refs

        @pl.when(pl.program_id(0) == 0)
        def _():
            S_ref[...] = jnp.zeros_like(S_ref)

        ac = acr[...]
        cols = [_hi_lo(v) for v in (dtr[...], jnp.exp(ac), jnp.exp(ac[CH - 1:CH, :] - ac))]
        _interleave([group(g * _HG, cols, ac, actr[...], _last(spread_ref, g * _GW, (g + 1) * _GW),
                           *_group_views(g, (xs, dsk, zr, nw, y, yn, spv, S_ref), (bm, cmat))) for g in range(NG)])

    def group(hb, cols, ac, acT, spread_ref, xs_ref, dsk_ref, z_ref, nw_ref, y_ref, yn_ref, sp_ref, S_ref, b_ref, c_ref):
        dtl, eal, dtel = (_dot(v, spread_ref[...]) for v in cols)
        Bm, Cm = b_ref[...], c_ref[...]
        S = S_ref[...]
        sp_ref[0] = S
        cb = _dot(Cm, Bm, NT)
        CS = _dot(Cm, S.astype(BF16))
        row = lax.broadcasted_iota(jnp.int32, (CH, CH), 0)
        col = lax.broadcasted_iota(jnp.int32, (CH, CH), 1)
        tril = row >= col
        left = col < HP
        xd_parts = []
        for p in range(_HG // 2):
            sl = slice(p * LANES, (p + 1) * LANES)
            j0, j1 = hb + 2 * p, hb + 2 * p + 1
            xp = xs_ref[:, sl].astype(F32)
            a0, a1 = ac[:, j0:j0 + 1], ac[:, j1:j1 + 1]
            X = xp * dtl[:, sl]
            Xb = X.astype(BF16)
            Ws = [(cb * jnp.exp(jnp.where(tril, aj - acT[j:j + 1, :], _NEG))).astype(BF16)
                  for j, aj in ((j0, a0), (j1, a1))]
            Xs = [jnp.where(m, Xb, jnp.zeros_like(Xb)) for m in (left, jnp.logical_not(left))]
            yield
            yd = _dot(jnp.concatenate(Ws, axis=1), jnp.concatenate(Xs, axis=0))
            yield
            y = yd + eal[:, sl] * CS[:, sl] + dsk_ref[:, sl] * xp
            y_ref[:, sl] = y.astype(BF16)
            xd_parts.append(X * dtel[:, sl])
        Xd = jnp.concatenate(xd_parts, axis=1).astype(BF16)
        S_ref[...] = eal[CH - 1:CH, :] * S + _dot(Bm, Xd, TN)
        yield
        z = z_ref[...].astype(F32)
        yf = y_ref[...].astype(F32) * z * _sigmoid(z)
        r = lax.rsqrt(jnp.mean(yf * yf, axis=-1, keepdims=True) + EPS)
        yn_ref[...] = (yf * r * nw_ref[...]).astype(BF16)

    zspec = pl.BlockSpec((CH, DI), lambda c: (c, OFF_Z // DI))
    return pl.pallas_call(
        body, name="ssd_fwd", grid=(nc,),
        in_specs=[sp["xs"], sp["bm"], sp["cmat"], sp["col"], sp["col"], sp["col"], sp["dsk"], zspec, sp["dsk"],
                  pl.BlockSpec((2 * LANES, DI), lambda c: (0, 0))],
        out_specs=[sp["xs"], sp["xs"], sp["state"]],
        out_shape=[jax.ShapeDtypeStruct((T, DI), BF16), jax.ShapeDtypeStruct((T, DI), BF16),
                   jax.ShapeDtypeStruct((nc, NS, DI), F32)],
        scratch_shapes=[pltpu.VMEM((NS, DI), F32)],
        compiler_params=_params(("arbitrary",)),
    )(xact, xact, xact, dt, acum, acumT, dsk_rep, proj, norm_w, _head_spread())


def _ssd_bwd(dn, y, proj, norm_w, dproj, xact, dt, acum, acumT, dsk_rep, sprev):
    T = xact.shape[0]
    nc = T // CH
    sp = _ssd_specs(T, True)

    def body(*refs):
        (xs, bm, cmat, dtr, acr, actr, dsk, dnr, yr, zr, nw, spv, _alias, lanes_of_ref, rows_of_ref, spread_ref,
         dxa, ddtx, dAc, dskacc, dzr, dnw, dS_ref) = refs
        first = pl.program_id(0) == 0

        @pl.when(first)
        def _():
            dS_ref[...] = jnp.zeros_like(dS_ref)

        dbc = _last(dxa, DI, DX)
        ddtx_sum = jnp.zeros((CH, LANES), F32)
        dAc_sum = jnp.zeros((CH, LANES), F32)
        ac = acr[...]
        cols = [_hi_lo(v) for v in (dtr[...], jnp.exp(ac), jnp.exp(ac[CH - 1:CH, :] - ac))]
        for a, b in _interleave([group(first, g * _HG, cols, ac, actr[...],
                                       lanes_of_ref.at[g * _GW:(g + 1) * _GW],
                                       rows_of_ref.at[g * _HG * CH:(g + 1) * _HG * CH],
                                       _last(spread_ref, g * _GW, (g + 1) * _GW),
                                       *_group_views(g, (xs, dsk, dnr, yr, zr, nw, dzr, dnw, spv, dxa, dskacc, dS_ref),
                                                     (bm, cmat, dbc, _last(dbc, NG * NS, 2 * NG * NS))))
                                 for g in range(NG)]):
            ddtx_sum, dAc_sum = ddtx_sum + a, dAc_sum + b
        ddtx[...] = ddtx_sum
        dAc[...] = dAc_sum

    def group(first, hb, cols, ac, acT, lanes_of_ref, rows_of_ref, spread_ref, xs_ref, dsk_ref, dn_ref, y_ref, z_ref,
              nw_ref, dz_ref, dnw_ref, sp_ref, dx_ref, dskacc_ref, dS_ref, b_ref, c_ref, dB_ref, dC_ref):
        z = z_ref[...].astype(F32)
        yv = y_ref[...].astype(F32)
        sg = _sigmoid(z)
        silu = z * sg
        yf = yv * silu
        rn = lax.rsqrt(jnp.mean(yf * yf, axis=-1, keepdims=True) + EPS)
        yh = yf * rn
        dnv = dn_ref[...].astype(F32)
        dyh = dnv * nw_ref[...]
        dyf = rn * (dyh - yh * jnp.mean(dyh * yh, axis=-1, keepdims=True))
        dyg = dyf * silu
        dz_ref[...] = (dyf * yv * sg * (1.0 + z * (1.0 - sg))).astype(BF16)
        _acc_out(dnw_ref, _colsum(dnv * yh), first)
        Bm, Cm = b_ref[...], c_ref[...]
        S = sp_ref[0]
        dS = dS_ref[...]
        Sb, dSb = S.astype(BF16), dS.astype(BF16)
        cb = _dot(Cm, Bm, NT)
        cbT = _dot(Bm, Cm, NT)
        CmT = Cm.T
        CS = _dot(Cm, Sb)
        T1 = _dot(Bm, dSb)
        yield
        row = lax.broadcasted_iota(jnp.int32, (CH, CH), 0)
        col = lax.broadcasted_iota(jnp.int32, (CH, CH), 1)
        tril = row >= col
        triu = row <= col
        left = col < HP
        lastrow = lax.broadcasted_iota(jnp.int32, (CH, 1), 0) == CH - 1
        dCB = jnp.zeros((CH, CH), F32)
        dCBT = jnp.zeros((CH, CH), F32)
        xd_parts, dye_parts, dsk_parts, dxx_parts, gr_parts, end_parts, qd_parts = ([] for _ in range(7))
        dtls, eals, dtels = (_dot(v, spread_ref[...]) for v in cols)
        for p in range(_HG // 2):
            sl = slice(p * LANES, (p + 1) * LANES)
            j0, j1 = hb + 2 * p, hb + 2 * p + 1
            xp = xs_ref[:, sl].astype(F32)
            dyp = dyg[:, sl]
            a0, a1 = ac[:, j0:j0 + 1], ac[:, j1:j1 + 1]
            dtl, eal, dtel = dtls[:, sl], eals[:, sl], dtels[:, sl]
            X = xp * dtl
            Xb = X.astype(BF16)
            T1d = T1[:, sl] * dtel
            Rm = T1d * X
            decp = eal[CH - 1:CH, :]
            gr_parts.append(dyp * (eal * CS[:, sl]) - Rm)
            end_parts.append(Rm + decp * (dS[:, sl] * S[:, sl]))
            dXd = jnp.zeros((CH, LANES), F32)
            for j, aj, mask in ((j0, a0, left), (j1, a1, jnp.logical_not(left))):
                dYm = jnp.where(mask, dyp, 0.0).astype(BF16)
                dWm = _dot(dYm, Xb, NT)
                yield
                e = aj - acT[j:j + 1, :]
                P = dWm * jnp.exp(jnp.where(tril, e, _NEG))
                LmT = jnp.exp(jnp.where(triu, -e, _NEG))
                PT = P.T
                dCB = dCB + P
                dCBT = dCBT + PT
                yield
                dXd = dXd + _dot((cbT * LmT).astype(BF16), dYm)
                qd_parts.append((P * cb - PT * cbT).astype(BF16))
                yield
            dX = dXd + T1d
            dxx_parts.append(dX * xp)
            dx_ref[:, sl] = (dX * dtl + dsk_ref[:, sl] * dyp).astype(BF16)
            dsk_parts.append(_colsum(dyp * xp))
            xd_parts.append(X * dtel)
            dye_parts.append(dyp * eal)
            yield
        Xd = jnp.concatenate(xd_parts, axis=1).astype(BF16)
        dYe = jnp.concatenate(dye_parts, axis=1).astype(BF16)
        def lane_sums(parts):
            return _dot(jnp.concatenate(parts, axis=1).astype(BF16), lanes_of_ref[...])
        ddtx = lane_sums(dxx_parts)
        dAc = (_dot(jnp.concatenate(qd_parts, axis=1), rows_of_ref[...]) + lane_sums(gr_parts)
               + jnp.where(lastrow, _colsum(lane_sums(end_parts)), 0.0))
        dC_ref[...] = (_dot(dCB.astype(BF16), Bm) + _dot(dYe, Sb, NT)).astype(BF16)
        dB_ref[...] = (_dot(dCBT.astype(BF16), Cm) + _dot(Xd, dSb, NT)).astype(BF16)
        dS_ref[...] = _dot(CmT, dYe) + eals[CH - 1:CH, :] * dS
        _acc_out(dskacc_ref, jnp.concatenate(dsk_parts, axis=1), first)
        return ddtx, dAc

    zspec = pl.BlockSpec((CH, DI), lambda c: (nc - 1 - c, OFF_Z // DI))
    head = lax.broadcasted_iota(jnp.int32, (1, LANES), 1)
    lanes_of = (lax.broadcasted_iota(jnp.int32, (DI, 1), 0) // HP == head).astype(BF16)
    rows_of = (lax.broadcasted_iota(jnp.int32, (NH * CH, 1), 0) // CH == head).astype(BF16)
    return pl.pallas_call(
        body, name="ssd_bwd", grid=(nc,),
        in_specs=[sp["xs"], sp["bm"], sp["cmat"], sp["col"], sp["col"], sp["col"], sp["dsk"], sp["xs"], sp["xs"],
                  zspec, sp["dsk"], sp["state"], ANY, pl.BlockSpec(lanes_of.shape, lambda c: (0, 0)),
                  pl.BlockSpec(rows_of.shape, lambda c: (0, 0)), pl.BlockSpec((2 * LANES, DI), lambda c: (0, 0))],
        out_specs=[sp["xbc"], sp["col"], sp["col"], sp["dsk"], zspec, sp["dsk"]],
        out_shape=[jax.ShapeDtypeStruct((T, DX), BF16), jax.ShapeDtypeStruct((T, LANES), F32),
                   jax.ShapeDtypeStruct((T, LANES), F32), jax.ShapeDtypeStruct((1, DI), F32),
                   jax.ShapeDtypeStruct(dproj.shape, BF16), jax.ShapeDtypeStruct((1, DI), F32)],
        scratch_shapes=[pltpu.VMEM((NS, DI), F32)], input_output_aliases={12: 4},
        compiler_params=_params(("arbitrary",)),
    )(xact, xact, xact, dt, acum, acumT, dsk_rep, dn, y, proj, norm_w, sprev, dproj, lanes_of, rows_of, _head_spread())


def _merge_fwd_epilogue(proj, ya):
    T = proj.shape[0]

    def fn(ysv, ins, outs, first):
        g_ref, ya_ref = ins
        m_ref, ys_ref = outs
        ga = _sigmoid(g_ref[:, :D].astype(F32))
        gs = _sigmoid(g_ref[:, D:].astype(F32))
        m_ref[...] = (ga * ya_ref[...].astype(F32) + gs * ysv).astype(BF16)
        ys_ref[...] = ysv.astype(BF16)

    return _Epilogue(fn, (proj, ya), (((T, D), BF16), ((T, D), BF16)), 10 * D, in_windows={0: (OFF_G, 2 * D)})


def _merge_bwd_epilogue(proj, ya, ys, ncols):
    T = proj.shape[0]

    def fn(d, ins, outs, first):
        g_ref, ya_ref, ys_ref = ins
        dg_ref, dya_ref, dys_ref = outs
        ga = _sigmoid(g_ref[:, :D].astype(F32))
        gs = _sigmoid(g_ref[:, D:].astype(F32))
        dya_ref[...] = (d * ga).astype(BF16)
        dys_ref[...] = (d * gs).astype(BF16)
        dg_ref[:, :D] = (d * ya_ref[...].astype(F32) * ga * (1.0 - ga)).astype(BF16)
        dg_ref[:, D:] = (d * ys_ref[...].astype(F32) * gs * (1.0 - gs)).astype(BF16)

    window = (OFF_G, 2 * D)
    return _Epilogue(fn, (proj, ya, ys), (((T, ncols), BF16), ((T, D), BF16), ((T, D), BF16)), 16 * D,
                     in_windows={0: window}, out_windows={0: window})


_FW = 1408
_FB = FF // _FW


def _ffn_act_fwd(hv, conv_w, conv_b):
    T = hv.shape[0]
    R = _Rows(T, 256)
    tm = R.tm

    def body(h1_ref, h1p_ref, h3_ref, w_ref, b_ref, o_ref):
        keep = (pl.program_id(1) > 0).astype(F32)
        ext = jnp.concatenate([h1p_ref[...].astype(F32) * keep, h1_ref[...].astype(F32)], axis=0)
        pre = _wsum(w_ref[...], _shifts_causal(ext, 3, tm)) + b_ref[...]
        o_ref[...] = (pre * _sigmoid(pre) * h3_ref[...].astype(F32)).astype(BF16)

    return R.call(body, "ffn_act_fwd", _FB,
                  [R.tile(_FW), R.prev(_FW), R.tile(_FW, _FB), R.colvec(3, _FW), R.colvec(1, _FW)],
                  R.tile(_FW), jax.ShapeDtypeStruct((T, FF), BF16), (hv, hv, hv, conv_w, conv_b))


def _ffn_act_bwd(dg, hv, conv_w, conv_b):
    T = hv.shape[0]
    R = _Rows(T, 256)
    tm = R.tm

    def body(dg_ref, dgn_ref, h1_ref, h1p_ref, h1n_ref, h3_ref, h3n_ref, w_ref, b_ref, dh3_ref, dh1_ref, dw_ref,
             db_ref):
        i = pl.program_id(1)
        keep_p = (i > 0).astype(F32)
        keep_n = (i < R.nrow - 1).astype(F32)
        w = w_ref[...]
        ext = jnp.concatenate([h1p_ref[...].astype(F32) * keep_p, h1_ref[...].astype(F32),
                               h1n_ref[...].astype(F32)], axis=0)
        sh = _shifts_causal(ext, 3, tm + HALO)
        pre = _wsum(w, sh) + b_ref[...]
        s = _sigmoid(pre)
        d = jnp.concatenate([dg_ref[...].astype(F32), dgn_ref[...].astype(F32) * keep_n], axis=0)
        h3 = jnp.concatenate([h3_ref[...].astype(F32), h3n_ref[...].astype(F32)], axis=0)
        dh3_ref[...] = (d[:tm] * pre[:tm] * s[:tm]).astype(BF16)
        dpre = d * h3 * s * (1.0 + pre * (1.0 - s))
        dh1_ref[...] = _wsum(w, _shifts_anticausal(dpre, 3, tm)).astype(BF16)
        dp = dpre[:tm]
        _acc_rows(dw_ref, [_colsum(dp * q[:tm]) for q in sh], i == 0)
        _acc_out(db_ref, _colsum(dp), i == 0)

    return R.call(
        body, "ffn_act_bwd", _FB,
        [R.tile(_FW), R.next(_FW), R.tile(_FW), R.prev(_FW), R.next(_FW), R.tile(_FW, _FB), R.next(_FW, _FB),
         R.colvec(3, _FW), R.colvec(1, _FW)],
        [R.tile(_FW), R.tile(_FW), R.colvec(3, _FW), R.colvec(1, _FW)],
        [jax.ShapeDtypeStruct((T, FF), BF16), jax.ShapeDtypeStruct((T, FF), BF16),
         jax.ShapeDtypeStruct((3, FF), F32), jax.ShapeDtypeStruct((1, FF), F32)],
        (dg, dg, hv, hv, hv, hv, hv, conv_w, conv_b))


def _final_loss_epilogue(w, target):
    T = target.shape[0]

    def fn(xv, ins, outs, first):
        w_ref, t_ref = ins
        l_ref, dh_ref, dhb_ref, dw_ref = outs
        wv = w_ref[...]
        r = lax.rsqrt(jnp.mean(xv * xv, axis=-1, keepdims=True) + EPS)
        xh = xv * r
        err = xh * wv - t_ref[...]
        part = 0.5 * jnp.sum(jnp.mean(err * err, axis=-1, keepdims=True), axis=0, keepdims=True)
        _acc_out(l_ref, jnp.broadcast_to(part, l_ref.shape), first)
        dy = err * (1.0 / D)
        dxh = dy * wv
        dh = r * (dxh - xh * jnp.mean(dxh * xh, axis=-1, keepdims=True))
        dh_ref[...] = dh
        dhb_ref[...] = dh.astype(BF16)
        _acc_out(dw_ref, _colsum(dy * xh), first)

    return _Epilogue(fn, (w, target),
                     (((8, LANES), F32), ((T, D), F32), ((T, D), BF16), ((1, D), F32)), 10 * D)


def _pad_lanes(v, n=LANES):
    return jnp.pad(v, ((0, 0), (0, n - v.shape[1])))


class _Hooks:
    def before_in_proj(self, w_in):
        return w_in

    def late_weights(self, wts, after):
        return wts

    def grads_ready(self, grads, tie):
        return tie

    def mark(self, name, value):
        return value


def _local_step(x, target, wts, hooks=None):
    hooks = hooks or _Hooks()
    T = x.shape[0]
    w_in = wts["w_in"]
    dt_bias_p, a_log_p = _pad_lanes(wts["dt_bias"]), _pad_lanes(wts["a_log"])
    dsk_rep = jnp.repeat(wts["d_skip"], HP, axis=1)

    w_in = hooks.before_in_proj(w_in)
    proj, u, dt_raw = _norm_matmul(x, wts["norm_mix_w"], w_in, "norm_mm_in", w_in[:, OFF_DT:OFF_DT + LANES])
    ya_in = _branch_a_fwd(proj, wts["conv_a_w"])
    xact = _xbc_fwd(proj, wts["ssd_conv_w"], wts["ssd_conv_b"])
    dt, acum, acumT = _dt_fwd(dt_raw, dt_bias_p, a_log_p)
    y_ssd, yn, sprev = _ssd_fwd(xact, dt, acum, acumT, dsk_rep, proj, wts["ssd_norm_w"])
    late = hooks.late_weights(wts, yn)
    w_a_out, w_s_out, w_o, w_up, w_down = (late[k] for k in ("w_a_out", "w_s_out", "w_o", "w_up", "w_down"))
    y_a = _matmul(ya_in, w_a_out, mode="nn", out_dtype=BF16, name="mm_a_out")
    merged, y_s = _matmul(yn, w_s_out, mode="nn", out_dtype=BF16, name="mm_s_out_merge",
                          epilogue=_merge_fwd_epilogue(proj, y_a))
    h1 = _matmul(merged, w_o, mode="nn", out_dtype=F32, name="mm_o", residual=x)
    hv, v = _norm_matmul(h1, wts["norm_ffn_w"], w_up, "norm_mm_up")
    gact = _ffn_act_fwd(hv, wts["ffn_conv_w"], wts["ffn_conv_b"])
    loss, dh2, dh2b, g_final = _matmul(gact, w_down, mode="nn", out_dtype=F32, name="mm_down_loss", residual=h1,
                                       epilogue=_final_loss_epilogue(wts["final_norm_w"], target))

    grads = {"final_norm_w": g_final}
    grads["w_down"] = _matmul(gact, dh2b, mode="tn", out_dtype=BF16, name="mm_down_dw")
    dgact = _matmul(dh2b, w_down, mode="nt", out_dtype=BF16, name="mm_down_dx")
    dh3, dh1c, grads["ffn_conv_w"], grads["ffn_conv_b"] = _ffn_act_bwd(dgact, hv, wts["ffn_conv_w"], wts["ffn_conv_b"])
    grads["w_up"] = (_matmul(v, dh1c, mode="tn", out_dtype=BF16, name="mm_up_dw1"),
                     _matmul(v, dh3, mode="tn", out_dtype=BF16, name="mm_up_dw3"))
    dv = _matmul(dh1c, w_up, mode="nt", out_dtype=F32, name="mm_up_dx1")
    dh1, dh1b, grads["norm_ffn_w"] = _matmul(
        dh3, w_up, mode="nt", out_dtype=F32, name="mm_up_dx3_norm", residual=dv, b_k_off=FF,
        epilogue=_rmsnorm_bwd_epilogue(h1, wts["norm_ffn_w"], dh2))
    grads["w_o"] = _matmul(merged, dh1b, mode="tn", out_dtype=BF16, name="mm_o_dw")
    dproj, dya, dys = _matmul(dh1b, w_o, mode="nt", out_dtype=BF16, name="mm_o_dx_merge",
                              epilogue=_merge_bwd_epilogue(proj, y_a, y_s, NIP))
    grads["w_a_out"] = _matmul(ya_in, dya, mode="tn", out_dtype=BF16, name="mm_a_out_dw")
    dya_in = _matmul(dya, w_a_out, mode="nt", out_dtype=BF16, name="mm_a_out_dx")
    dproj, grads["conv_a_w"] = _branch_a_bwd(dya_in, proj, wts["conv_a_w"], dproj)
    grads["w_s_out"] = _matmul(yn, dys, mode="tn", out_dtype=BF16, name="mm_s_out_dw")
    dys = hooks.grads_ready({k: grads[k] for k in ("w_a_out", "w_s_out", "w_o", "w_up", "w_down")}, dys)
    dyn =_matmul(dys, w_s_out, mode="nt", out_dtype=BF16, name="mm_s_out_dx")
    dxact, ddt_x, dacum, dskl, dproj, grads["ssd_norm_w"] = _ssd_bwd(
        dyn, y_ssd, proj, wts["ssd_norm_w"], dproj, xact, dt, acum, acumT, dsk_rep, sprev)
    dxact = hooks.mark("ssd_bwd", dxact)
    grads["d_skip"] = dskl.reshape(NH, HP).sum(axis=1).reshape(1, NH)
    dproj, grads["ssd_conv_w"], grads["ssd_conv_b"] = _xbc_bwd(dxact, proj, wts["ssd_conv_w"], wts["ssd_conv_b"], dproj)
    dproj, g_dtb, g_alog = _dt_bwd(dacum, ddt_x, dt_raw, dt_bias_p, a_log_p, dproj)
    grads["dt_bias"], grads["a_log"] = g_dtb[:, :NH], g_alog[:, :NH]
    grads["w_in"] = _matmul(u, dproj, mode="tn", out_dtype=BF16, name="mm_in_dw")
    dproj = hooks.grads_ready({"w_in": grads["w_in"]}, dproj)
    grad_x, _, grads["norm_mix_w"] = _matmul(dproj, w_in, mode="nt", out_dtype=F32, name="mm_in_dx_norm",
                                             epilogue=_rmsnorm_bwd_epilogue(x, wts["norm_mix_w"], dh1))
    return loss, grad_x, grads


def _permute_w_in(slabs):
    cs = slabs.shape[2]
    pieces = []
    for o, n, no in sorted(_SEGS, key=lambda seg: seg[2]):
        for s in range(slabs.shape[0]):
            lo, hi = max(o, s * cs), min(o + n, (s + 1) * cs)
            if lo < hi:
                pieces.append(slabs[s][:, lo - s * cs:hi - s * cs])
    pieces.append(jnp.zeros((slabs.shape[1], NIP - OFF_DT - _SEGS[-1][1]), slabs.dtype))
    return jnp.concatenate(pieces, axis=1)


def _unpermute_w_in(g):
    cs = NI // NCHIP
    slabs = []
    for s in range(NCHIP):
        pieces = []
        for o, n, no in sorted(_SEGS):
            lo, hi = max(o, s * cs), min(o + n, (s + 1) * cs)
            if lo < hi:
                pieces.append(g[:, no + lo - o:no + hi - o])
        slabs.append(jnp.concatenate(pieces, axis=1))
    return jnp.stack(slabs)


MESH = pl.DeviceIdType.MESH
NCHIP = 4
NDEV = 8

_W_IN = (("w_in", D, NI // NCHIP, 1),)
_W_REST = (("w_a_out", D // NCHIP, D, 0), ("w_s_out", DI // NCHIP, D, 0), ("w_o", D // NCHIP, D, 0),
           ("w_up", D, 2 * FF // NCHIP, 1), ("w_down", FF // NCHIP, D, 0))


def _coords():
    return lax.axis_index("x"), lax.axis_index("y"), lax.axis_index("c")


def _other_chips(x, y):
    return [(1 - x, y), (x, 1 - y), (1 - x, 1 - y)]


def _ag_weights(shard):
    nrows = shard.shape[0]
    hr = nrows // 2

    def body(x_ref, out_ref, send_sems, recv_sems, local_sem):
        x, y, c = _coords()
        me = 2 * x + y
        chips = _other_chips(x, y)

        def rows(s, h):
            return out_ref.at[s, pl.ds(h * hr, hr), :]

        def copy(k, s, h, to, src=None):
            return pltpu.make_async_remote_copy(
                src_ref=rows(s, h) if src is None else src, dst_ref=rows(s, h),
                send_sem=send_sems.at[k], recv_sem=recv_sems.at[k], device_id=to, device_id_type=MESH)

        mine = pltpu.make_async_copy(x_ref, out_ref.at[me], local_sem)
        mine.start()
        first = [copy(k, me, c, (*chip, c), src=x_ref.at[pl.ds(c * hr, hr), :]) for k, chip in enumerate(chips)]
        for cp in first:
            cp.start()
        passed = []
        for k, chip in enumerate(chips):
            s = 2 * chip[0] + chip[1]
            copy(k, s, c, (x, y, c)).wait_recv()
            fwd = copy(3 + k, s, c, (x, y, 1 - c))
            fwd.start()
            passed.append(fwd)
        for k, chip in enumerate(chips):
            copy(3 + k, 2 * chip[0] + chip[1], 1 - c, (x, y, c)).wait_recv()
        for cp in first + passed:
            cp.wait_send()
        mine.wait()

    return pl.pallas_call(
        body, name="ag_weights", in_specs=[ANY], out_specs=ANY,
        out_shape=jax.ShapeDtypeStruct((NCHIP,) + shard.shape, shard.dtype),
        scratch_shapes=[pltpu.SemaphoreType.DMA((6,)), pltpu.SemaphoreType.DMA((6,)), pltpu.SemaphoreType.DMA],
        compiler_params=pltpu.CompilerParams(has_side_effects=True),
    )(shard)


HBM = pl.BlockSpec(memory_space=pltpu.HBM)
SEM = pl.BlockSpec(memory_space=pltpu.SEMAPHORE)
_EFFECT = pltpu.SideEffectType.DATAFLOW_SIDE_EFFECTING
_NCOPY = NCHIP - 1


def _plan_bcast(src_ref, land_ref, send_sems, recv_sems, base):
    x, y, c = _coords()
    sends, lands = [], []
    for k, chip in enumerate(_other_chips(x, y)):
        def copy(slot):
            return pltpu.make_async_remote_copy(
                src_ref=src_ref, dst_ref=land_ref.at[slot], send_sem=send_sems.at[base + k],
                recv_sem=recv_sems.at[base + k], device_id=(*chip, c), device_id_type=MESH)
        sends.append(copy(2 * x + y))
        lands.append(copy(2 * chip[0] + chip[1]))
    return sends, lands


def _plan_scatter(src_ref, land_ref, send_sems, recv_sems, base):
    x, y, c = _coords()
    cps = [pltpu.make_async_remote_copy(
        src_ref=src_ref.at[2 * chip[0] + chip[1]], dst_ref=land_ref.at[k], send_sem=send_sems.at[base + k],
        recv_sem=recv_sems.at[base + k], device_id=(*chip, c), device_id_type=MESH)
        for k, chip in enumerate(_other_chips(x, y))]
    return cps, cps


def _plan_swap(src_ref, land_ref, send_sems, recv_sems, base):
    x, y, c = _coords()
    cp = pltpu.make_async_remote_copy(
        src_ref=src_ref, dst_ref=land_ref, send_sem=send_sems.at[base], recv_sem=recv_sems.at[base],
        device_id=(x, y, 1 - c), device_id_type=MESH)
    return [cp], [cp]


def _plan_all(plan, refs, n):
    sends, lands = [], []
    for t in range(n):
        s, l = plan(refs[t], refs[n + t], refs[2 * n], refs[2 * n + 1], t * _NCOPY)
        sends += s
        lands += l
    return sends, lands


def _split_start(name, srcs, lands, plan):
    n = len(srcs)

    def body(*refs):
        for cp in _plan_all(plan, refs, n)[0]:
            cp.start()
        refs[-1][...] = jnp.zeros_like(refs[-1])

    arrays = list(srcs) + list(lands)
    outs = pl.pallas_call(
        body, name=name,
        out_shape=(pltpu.SemaphoreType.DMA((n * _NCOPY,)), pltpu.SemaphoreType.DMA((n * _NCOPY,)),
                   *[pltpu.HBM(a.shape, a.dtype) for a in arrays], jax.ShapeDtypeStruct((8, LANES), F32)),
        in_specs=(HBM,) * (2 * n),
        out_specs=(SEM, SEM) + (HBM,) * (2 * n) + (pl.BlockSpec(memory_space=pltpu.VMEM),),
        input_output_aliases={t: 2 + t for t in range(2 * n)},
        compiler_params=pltpu.CompilerParams(has_side_effects=_EFFECT),
    )(*[pltpu.with_memory_space_constraint(a, pltpu.HBM) for a in arrays])
    return (outs[0], outs[1], tuple(outs[2:2 + 2 * n])), outs[-1]


def _split_wait(name, handle, after, plan):
    send_sems, recv_sems, arrays = handle
    n = len(arrays) // 2

    def body(*refs):
        sends, lands = _plan_all(plan, refs[:2 * n] + refs[2 * n:2 * n + 2], n)
        for cp in sends:
            cp.wait_send()
        for cp in lands:
            cp.wait_recv()

    outs = pl.pallas_call(
        body, name=name, out_shape=tuple(pltpu.HBM(a.shape, a.dtype) for a in arrays),
        in_specs=(HBM,) * (2 * n) + (SEM, SEM, ANY), out_specs=(HBM,) * (2 * n),
        input_output_aliases={t: t for t in range(2 * n)},
        compiler_params=pltpu.CompilerParams(has_side_effects=_EFFECT),
    )(*arrays, send_sems, recv_sems, after)
    return outs[:n], outs[n:]


def _tie(x, token, name):
    def body(x_ref, t_ref, o_ref):
        pass

    return pl.pallas_call(
        body, name=name, in_specs=[ANY, pl.BlockSpec(memory_space=pltpu.VMEM)], out_specs=ANY,
        out_shape=jax.ShapeDtypeStruct(x.shape, x.dtype), input_output_aliases={0: 0},
    )(x, token)


_ADD_BYTES = 7 << 19


def _add_tile(rows, cols):
    best = 32
    for t in range(32, rows + 1, 32):
        if rows % t == 0 and t * cols * 4 <= _ADD_BYTES:
            best = t
    return best


def _add_slabs(pack, land, me, name):
    rows, cols = pack.shape[1:]
    tr = _add_tile(rows, cols)

    def body(me_ref, p_ref, l_ref, o_ref):
        f = lambda r: r.astype(F32)
        o_ref[...] = ((f(p_ref[0]) + f(l_ref[0])) + f(l_ref[1])) + f(l_ref[2])

    return pl.pallas_call(
        body, name=name,
        grid_spec=pltpu.PrefetchScalarGridSpec(
            num_scalar_prefetch=1, grid=(rows // tr,),
            in_specs=[pl.BlockSpec((1, tr, cols), lambda i, me_ref: (me_ref[0], i, 0)),
                      pl.BlockSpec((_NCOPY, tr, cols), lambda i, me_ref: (0, i, 0))],
            out_specs=pl.BlockSpec((tr, cols), lambda i, me_ref: (i, 0))),
        out_shape=jax.ShapeDtypeStruct((rows, cols), F32),
        compiler_params=_params(("parallel",)),
    )(me, pack, land)


_STAGE_W = 1024


def _stage_rows(shapes):
    pieces, r = [], 0
    for i, (k, w) in enumerate(shapes):
        for a in range(k):
            for q in range(0, w, _STAGE_W):
                pieces.append((i, a, q, min(_STAGE_W, w - q), r))
                r += 1
    return pieces, -(-r // 8) * 8


def _gather8(parts, reduce, name):
    shapes = [p.shape for p in parts]
    pieces, rows = _stage_rows(shapes)
    n = len(parts)

    def body(*refs):
        ins, outs = refs[:n], refs[n:2 * n]
        stage, buf, res, send_sems, recv_sems = refs[2 * n:]
        x, y, c = _coords()
        me = 4 * x + 2 * y + c
        stage[...] = jnp.zeros_like(stage)
        for i, a, q, w, r in pieces:
            stage[r:r + 1, 0:w] = ins[i][a:a + 1, q:q + w]
        buf[pl.ds(me, 1)] = stage[...][None]
        cps, lands = [], []
        for k in range(1, NDEV):
            peer = (1 - x if k & 4 else x, 1 - y if k & 2 else y, 1 - c if k & 1 else c)

            def copy(slot):
                return pltpu.make_async_remote_copy(
                    src_ref=stage, dst_ref=buf.at[slot], send_sem=send_sems.at[k - 1],
                    recv_sem=recv_sems.at[k - 1], device_id=peer, device_id_type=MESH)

            cps.append(copy(me))
            lands.append(copy(4 * peer[0] + 2 * peer[1] + peer[2]))
        for cp in cps:
            cp.start()
        for cp, land in zip(cps, lands):
            land.wait_recv()
            cp.wait_send()
        if reduce:
            acc = buf[0]
            for d in range(1, NDEV):
                acc = acc + buf[d]
            res[...] = acc
            for i, a, q, w, r in pieces:
                outs[i][a:a + 1, q:q + w] = res[r:r + 1, 0:w]
        else:
            for i, a, q, w, r in pieces:
                for s in range(NCHIP):
                    outs[i][s, a:a + 1, q:q + w] = buf[2 * s, r:r + 1, 0:w]

    vm = pl.BlockSpec(memory_space=pltpu.VMEM)
    out_shapes = [jax.ShapeDtypeStruct(s if reduce else (NCHIP,) + s, F32) for s in shapes]
    return pl.pallas_call(
        body, name=name, in_specs=[vm] * n, out_specs=[vm] * n, out_shape=out_shapes,
        scratch_shapes=[pltpu.VMEM((rows, _STAGE_W), F32), pltpu.VMEM((NDEV, rows, _STAGE_W), F32),
                        pltpu.VMEM((rows, _STAGE_W), F32), pltpu.SemaphoreType.DMA((NDEV - 1,)),
                        pltpu.SemaphoreType.DMA((NDEV - 1,))],
        compiler_params=pltpu.CompilerParams(has_side_effects=True),
    )(*parts)


def _adamw_update(w_ref, g_ref, m_ref, v_ref, d_ref, mo_ref, vo_ref):
    c1 = 1.0 / (1.0 - ADAM_B1 ** ADAM_STEP)
    c2 = 1.0 / (1.0 - ADAM_B2 ** ADAM_STEP)
    gv = g_ref[...]
    mn = ADAM_B1 * m_ref[...] + (1.0 - ADAM_B1) * gv
    vn = ADAM_B2 * v_ref[...] + (1.0 - ADAM_B2) * (gv * gv)
    d_ref[...] = -ADAM_LR * ((mn * c1) / (jnp.sqrt(vn * c2) + ADAM_EPS) + ADAM_WD * w_ref[...])
    mo_ref[...] = mn
    vo_ref[...] = vn


def _adamw_small(ws, gs, ms, vs):
    n = len(ws)

    def body(*refs):
        for i in range(n):
            _adamw_update(*(refs[j * n + i] for j in range(7)))

    vm = pl.BlockSpec(memory_space=pltpu.VMEM)
    outs = pl.pallas_call(
        body, name="adamw_small", in_specs=[vm] * (4 * n), out_specs=[vm] * (3 * n),
        out_shape=[jax.ShapeDtypeStruct(w.shape, F32) for w in ws] * 3,
    )(*ws, *gs, *ms, *vs)
    return outs[:n], outs[n:2 * n], outs[2 * n:]


def _adamw(w, g_parts, m, v, name):
    rows, cols = w.shape
    tr = rows
    while tr * cols * 4 > (1 << 20) and tr % 16 == 0:
        tr //= 2

    def body(w_ref, ga_ref, gb_ref, m_ref, v_ref, g_ref, d_ref, mo_ref, vo_ref):
        g_ref[...] = ga_ref[...] + gb_ref[...]
        _adamw_update(w_ref, g_ref, m_ref, v_ref, d_ref, mo_ref, vo_ref)

    blk = pl.BlockSpec((tr, cols), lambda i: (i, 0))
    return pl.pallas_call(
        body, name=name, grid=(rows // tr,), in_specs=[blk] * 5, out_specs=[blk] * 4,
        out_shape=[jax.ShapeDtypeStruct((rows, cols), F32)] * 4, compiler_params=_params(("parallel",)),
    )(w, *g_parts, m, v)


def _by_chip(g, rr, cc, axis):
    if isinstance(g, tuple):
        n = NCHIP // len(g)
        return jnp.concatenate([h.reshape(rr, n, cc).transpose(1, 0, 2) for h in g], axis=0)
    return g.reshape(NCHIP, rr, cc) if axis == 0 else g.reshape(rr, NCHIP, cc).transpose(1, 0, 2)


_SMALL_REPL = ("norm_mix_w", "ssd_conv_b", "dt_bias", "a_log", "d_skip", "ssd_norm_w", "norm_ffn_w",
               "ffn_conv_b", "final_norm_w")
_SMALL_CONV = (("conv_a_w", 3, D), ("ssd_conv_w", 4, DX), ("ffn_conv_w", 3, FF))


def kernel(x, norm_mix_w, w_in, conv_a_w, w_a_out, ssd_conv_w, ssd_conv_b, dt_bias, a_log, d_skip, ssd_norm_w, w_s_out, w_o, norm_ffn_w, w_up, ffn_conv_w, ffn_conv_b, w_down, final_norm_w, loss_target, m_norm_mix_w, m_w_in, m_conv_a_w, m_w_a_out, m_ssd_conv_w, m_ssd_conv_b, m_dt_bias, m_a_log, m_d_skip, m_ssd_norm_w, m_w_s_out, m_w_o, m_norm_ffn_w, m_w_up, m_ffn_conv_w, m_ffn_conv_b, m_w_down, m_final_norm_w, v_norm_mix_w, v_w_in, v_conv_a_w, v_w_a_out, v_ssd_conv_w, v_ssd_conv_b, v_dt_bias, v_a_log, v_d_skip, v_ssd_norm_w, v_w_s_out, v_w_o, v_norm_ffn_w, v_w_up, v_ffn_conv_w, v_ffn_conv_b, v_w_down, v_final_norm_w):
    names = ("norm_mix_w", "w_in", "conv_a_w", "w_a_out", "ssd_conv_w", "ssd_conv_b", "dt_bias", "a_log", "d_skip",
             "ssd_norm_w", "w_s_out", "w_o", "norm_ffn_w", "w_up", "ffn_conv_w", "ffn_conv_b", "w_down", "final_norm_w")
    W = dict(zip(names, (norm_mix_w, w_in, conv_a_w, w_a_out, ssd_conv_w, ssd_conv_b, dt_bias, a_log, d_skip,
                         ssd_norm_w, w_s_out, w_o, norm_ffn_w, w_up, ffn_conv_w, ffn_conv_b, w_down, final_norm_w)))
    M = dict(zip(names, (m_norm_mix_w, m_w_in, m_conv_a_w, m_w_a_out, m_ssd_conv_w, m_ssd_conv_b, m_dt_bias, m_a_log,
                         m_d_skip, m_ssd_norm_w, m_w_s_out, m_w_o, m_norm_ffn_w, m_w_up, m_ffn_conv_w, m_ffn_conv_b,
                         m_w_down, m_final_norm_w)))
    V = dict(zip(names, (v_norm_mix_w, v_w_in, v_conv_a_w, v_w_a_out, v_ssd_conv_w, v_ssd_conv_b, v_dt_bias, v_a_log,
                         v_d_skip, v_ssd_norm_w, v_w_s_out, v_w_o, v_norm_ffn_w, v_w_up, v_ffn_conv_w, v_ffn_conv_b,
                         v_w_down, v_final_norm_w)))
    two_d = lambda a: a.reshape(-1, a.shape[-1])
    W2, M2, V2 = ({k: two_d(a) for k, a in t.items()} for t in (W, M, V))
    xi, yi, ci = _coords()
    me = 2 * xi + yi

    meidx = me.reshape(1).astype(jnp.int32)
    state = {}


    class Hooks(_Hooks):
        def before_in_proj(self, w_in):
            return _tie(w_in, state["rest_token"], "tie_ag_rest")

        def late_weights(self, wts, after):
            owns, lands = _split_wait("ag_rest_wait", state["rest"], after, _plan_bcast)
            full = {}
            for (n, rr, cc, axis), own, land in zip(_W_REST, owns, lands):
                slabs = lax.dynamic_update_slice(land, own[None], (me, 0, 0))
                full[n] = slabs.reshape(NCHIP * rr, cc) if axis == 0 else slabs.transpose(1, 0, 2).reshape(rr, NCHIP * cc)
            return {**wts, **full}

        def grads_ready(self, grads, tie):
            if "w_in" in grads:
                key, packs = "g_in", [_unpermute_w_in(grads["w_in"])]
            else:
                key = "g_rest"
                packs = [_by_chip(grads[n], rr, cc, axis)
                         for n, rr, cc, axis in _W_REST]
            lands = [lax.empty((_NCOPY,) + p.shape[1:], BF16) for p in packs]
            state[key], token = _split_start("rs_" + key + "_start", packs, lands, _plan_scatter)
            return _tie(tie, token, "tie_" + key)

        def mark(self, name, value):
            return _tie(value, reduced("g_rest", value, _W_REST), "tie_g_rest_swap")

    def reduced(key, after, group):
        packs, lands = _split_wait("rs_" + key + "_wait", state[key], after, _plan_scatter)
        mines = [_add_slabs(p, l, meidx, "rs_add_chips_" + n) for (n, *_), p, l in zip(group, packs, lands)]
        state[key + "_swap"], token = _split_start(
            "rs_" + key + "_swap_start", mines, [lax.empty(m.shape, F32) for m in mines], _plan_swap)
        return token

    def swapped(key, after, group):
        mines, theirs = _split_wait("rs_" + key + "_swap_wait", state[key + "_swap"], after, _plan_swap)
        return dict(zip([n for n, *_ in group], zip(mines, theirs)))

    w_in_slabs = _ag_weights(W2["w_in"].astype(BF16))
    wts = {k: W2[k] for k in _SMALL_REPL}
    conv_by_chip = _gather8([W2[n] for n, *_ in _SMALL_CONV], False, "ag_conv_weights")
    for (n, kk, width), stacked in zip(_SMALL_CONV, conv_by_chip):
        wts[n] = stacked.transpose(1, 0, 2).reshape(kk, width)
    rest = [W2[n].astype(BF16) for n, *_ in _W_REST]
    rest[0] = _tie(rest[0], conv_by_chip[0], "tie_ag_order")
    state["rest"], state["rest_token"] = _split_start(
        "ag_rest_start", rest, [lax.empty((NCHIP,) + r.shape, BF16) for r in rest], _plan_bcast)
    wts["w_in"] = _permute_w_in(w_in_slabs)

    loss8, grad_x, grads = _local_step(x[0], loss_target[0], wts, Hooks())

    reduced("g_in", grad_x, _W_IN)

    small_parts = [grads[n] for n in _SMALL_REPL] + [loss8[0:1]] + [grads[n] for n, *_ in _SMALL_CONV]
    small_g = _gather8(small_parts, True, "allreduce_small")
    gsm = dict(zip(_SMALL_REPL, small_g[:len(_SMALL_REPL)]))
    loss = small_g[len(_SMALL_REPL)][0, 0]
    for (n, kk, width), gfull in zip(_SMALL_CONV, small_g[len(_SMALL_REPL) + 1:]):
        cw = width // NCHIP
        gsm[n] = lax.dynamic_slice(gfull, (0, me * cw), (kk, cw))

    G, DW, NM, NV = {}, {}, {}, {}
    gbig = swapped("g_rest", grad_x, _W_REST)
    for n in [b[0] for b in _W_REST]:
        G[n], DW[n], NM[n], NV[n] = _adamw(W2[n], gbig[n], M2[n], V2[n], "adamw_" + n)
    gbig = swapped("g_in", DW[_W_REST[-1][0]], _W_IN)
    for n in [b[0] for b in _W_IN]:
        G[n], DW[n], NM[n], NV[n] = _adamw(W2[n], gbig[n], M2[n], V2[n], "adamw_" + n)
    sm_names = list(_SMALL_REPL) + [n for n, *_ in _SMALL_CONV]
    outs = _adamw_small(*([t[n] for n in sm_names] for t in (W2, gsm, M2, V2)))
    for t, vals in zip((DW, NM, NV), outs):
        t.update(zip(sm_names, vals))
    G.update(gsm)

    def shaped(t):
        return [t[n].reshape(W[n].shape) for n in names]

    return (loss, grad_x.reshape(x.shape), *shaped(G), *shaped(DW), *shaped(NM), *shaped(NV))
```
